```python
import jax, jax.numpy as jnp
from jax import lax
import numpy as np

D_MODEL = 1024
BATCH = 8
SEQ = 8192
DEPTH = 1

D_MIX = D_MODEL
D_POOL = D_MIX // 2
D_LRU = D_MIX - D_POOL
POOL_WINDOWS = (2, 4, 8, 16)
N_POOL_GROUPS = len(POOL_WINDOWS)
POOL_GROUP_WIDTH = D_POOL // N_POOL_GROUPS
LRU_HEADS = 8
LRU_BLOCK = D_LRU // LRU_HEADS
CONV_WIDTH = 4
LRU_C = 8.0
D_FF = 4 * D_MODEL
PLE_DIM = 256
RMS_EPS = 1e-6
D_IN_PROJ = D_POOL + D_LRU + D_LRU

kernel_name = "hybrid_pool_rglru_block"


def rms_norm(x, g):
    xf = x.astype(jnp.float32)
    y = xf * lax.rsqrt(jnp.mean(xf * xf, axis=-1, keepdims=True) + RMS_EPS)
    return (y * g.astype(jnp.float32)).astype(x.dtype)


def multiscale_pool_mixer(u, pool_w, pool_b, pool_scale):
    B, S, _ = u.shape
    ug = u.reshape(B, S, N_POOL_GROUPS, POOL_GROUP_WIDTH).astype(jnp.float32)
    csum = jnp.cumsum(ug, axis=1)
    t = jnp.arange(S)
    outs = []
    for g, w in enumerate(POOL_WINDOWS):
        c = csum[:, :, g]
        c_lag = jnp.pad(c, ((0, 0), (w, 0), (0, 0)))[:, :S]
        count = jnp.minimum(t + 1, w).astype(jnp.float32)
        outs.append((c - c_lag) / count[None, :, None] - ug[:, :, g])
    d = jnp.stack(outs, axis=2).astype(u.dtype)
    y = jnp.einsum('bsgc,gcd->bsgd', d, pool_w) + pool_b
    return y.reshape(B, S, D_POOL) * pool_scale


def causal_depthwise_conv(u, w, b):
    S = u.shape[1]
    upad = jnp.pad(u, ((0, 0), (CONV_WIDTH - 1, 0), (0, 0)))
    y = b
    for k in range(CONV_WIDTH):
        y = y + upad[:, k:k + S] * w[k]
    return y


def _linear_recurrence_combine(c1, c2):
    a1, b1 = c1
    a2, b2 = c2
    return a1 * a2, a2 * b1 + b2


def rg_lru(u, gate_a_w, gate_a_b, gate_x_w, gate_x_b, lru_L):
    B, S, W = u.shape
    uh = u.reshape(B, S, LRU_HEADS, LRU_BLOCK)
    r = jax.nn.sigmoid(jnp.einsum('bshi,hij->bshj', uh, gate_a_w) + gate_a_b).reshape(B, S, W)
    i = jax.nn.sigmoid(jnp.einsum('bshi,hij->bshj', uh, gate_x_w) + gate_x_b).reshape(B, S, W)
    log_a = LRU_C * r.astype(jnp.float32) * jax.nn.log_sigmoid(lru_L.astype(jnp.float32))
    a = jnp.exp(log_a)
    mult = jnp.sqrt(-jnp.expm1(2.0 * log_a))
    is_first = (jnp.arange(S) == 0)[None, :, None]
    mult = jnp.where(is_first, 1.0, mult)
    b = mult * (i * u).astype(jnp.float32)
    _, h = lax.associative_scan(_linear_recurrence_combine, (a, b), axis=1)
    return h.astype(u.dtype)


def _fwd_setup_inputs(seed: int = 0) -> dict:
    key = jax.random.key(seed)
    ks = jax.random.split(key, 24)
    f32 = jnp.float32
    nrm = lambda k, shape, scale: jax.random.normal(k, shape, f32) * scale
    gain = lambda k, shape: 1.0 + 0.05 * jax.random.normal(k, shape, f32)
    L = DEPTH
    rad = jnp.sqrt(jax.random.uniform(ks[12], (L, D_LRU), f32, 0.9 ** 2, 0.999 ** 2))
    lru_L = jnp.log(rad) - jnp.log1p(-rad)
    return {
        "x": jax.random.normal(ks[0], (BATCH, SEQ, D_MODEL), f32),
        "p": jax.random.normal(ks[1], (DEPTH, BATCH, SEQ, PLE_DIM), f32),
        "norm_mix_g": gain(ks[2], (L, D_MODEL)),
        "w_in": nrm(ks[3], (L, D_MODEL, D_IN_PROJ), D_MODEL ** -0.5),
        "pool_w": nrm(ks[4], (L, N_POOL_GROUPS, POOL_GROUP_WIDTH, POOL_GROUP_WIDTH), POOL_GROUP_WIDTH ** -0.5),
        "pool_b": nrm(ks[5], (L, N_POOL_GROUPS, POOL_GROUP_WIDTH), 0.01),
        "pool_scale": gain(ks[6], (L, D_POOL)),
        "conv_w": nrm(ks[7], (L, CONV_WIDTH, D_LRU), CONV_WIDTH ** -0.5),
        "conv_b": nrm(ks[8], (L, D_LRU), 0.01),
        "gate_a_w": nrm(ks[9], (L, LRU_HEADS, LRU_BLOCK, LRU_BLOCK), LRU_BLOCK ** -0.5),
        "gate_a_b": nrm(ks[10], (L, LRU_HEADS, LRU_BLOCK), 0.01),
        "gate_x_w": nrm(ks[11], (L, LRU_HEADS, LRU_BLOCK, LRU_BLOCK), LRU_BLOCK ** -0.5),
        "gate_x_b": nrm(ks[13], (L, LRU_HEADS, LRU_BLOCK), 0.01),
        "lru_L": lru_L,
        "w_out": nrm(ks[14], (L, D_MIX, D_MODEL), D_MIX ** -0.5),
        "norm_mlp_g": gain(ks[15], (L, D_MODEL)),
        "w_up": nrm(ks[16], (L, D_MODEL, D_FF), D_MODEL ** -0.5),
        "w_down": nrm(ks[17], (L, D_FF, D_MODEL), D_FF ** -0.5),
        "norm_ple_g": gain(ks[18], (L, D_MODEL)),
        "w_ple_gate": nrm(ks[19], (L, D_MODEL, D_MODEL), D_MODEL ** -0.5),
        "b_ple_gate": nrm(ks[20], (L, D_MODEL), 0.01),
        "w_ple_proj": nrm(ks[21], (L, PLE_DIM, D_MODEL), PLE_DIM ** -0.5),
        "norm_final_g": gain(ks[22], (D_MODEL,)),
    }


def _fwd_reference(x, p, norm_mix_g, w_in, pool_w, pool_b, pool_scale, conv_w, conv_b,
              gate_a_w, gate_a_b, gate_x_w, gate_x_b, lru_L, w_out, norm_mlp_g,
              w_up, w_down, norm_ple_g, w_ple_gate, b_ple_gate, w_ple_proj, norm_final_g):
    h = x
    for l in range(DEPTH):
        z = rms_norm(h, norm_mix_g[l])
        proj = z @ w_in[l]
        u_pool = proj[..., :D_POOL]
        u_lru = proj[..., D_POOL:D_POOL + D_LRU]
        u_gate = proj[..., D_POOL + D_LRU:]
        y_pool = multiscale_pool_mixer(u_pool, pool_w[l], pool_b[l], pool_scale[l])
        xb = causal_depthwise_conv(u_lru, conv_w[l], conv_b[l])
        y_lru = rg_lru(xb, gate_a_w[l], gate_a_b[l], gate_x_w[l], gate_x_b[l], lru_L[l])
        y_lru = y_lru * jax.nn.gelu(u_gate)
        h = h + jnp.concatenate([y_pool, y_lru], axis=-1) @ w_out[l]
        z = rms_norm(h, norm_mlp_g[l])
        h = h + jnp.square(jax.nn.relu(z @ w_up[l])) @ w_down[l]
        z = rms_norm(h, norm_ple_g[l])
        gate = jax.nn.sigmoid(z @ w_ple_gate[l] + b_ple_gate[l])
        h = h + gate * (p[l] @ w_ple_proj[l])
    return rms_norm(h, norm_final_g)


import jax as _jax
import jax.numpy as _jnp

TWIN_FORMAT = 'train_step'
FWD_PARAMS = ['x', 'p', 'norm_mix_g', 'w_in', 'pool_w', 'pool_b', 'pool_scale', 'conv_w', 'conv_b', 'gate_a_w', 'gate_a_b', 'gate_x_w', 'gate_x_b', 'lru_L', 'w_out', 'norm_mlp_g', 'w_up', 'w_down', 'norm_ple_g', 'w_ple_gate', 'b_ple_gate', 'w_ple_proj', 'norm_final_g']
TWIN_WEIGHTS = ['norm_mix_g', 'w_in', 'pool_w', 'pool_b', 'pool_scale', 'conv_w', 'conv_b', 'gate_a_w', 'gate_a_b', 'gate_x_w', 'gate_x_b', 'lru_L', 'w_out', 'norm_mlp_g', 'w_up', 'w_down', 'norm_ple_g', 'w_ple_gate', 'b_ple_gate', 'w_ple_proj', 'norm_final_g']
TWIN_DIFF_INPUT = 'x'
TWIN_INPUTS = ['x', 'p', 'norm_mix_g', 'w_in', 'pool_w', 'pool_b', 'pool_scale', 'conv_w', 'conv_b', 'gate_a_w', 'gate_a_b', 'gate_x_w', 'gate_x_b', 'lru_L', 'w_out', 'norm_mlp_g', 'w_up', 'w_down', 'norm_ple_g', 'w_ple_gate', 'b_ple_gate', 'w_ple_proj', 'norm_final_g', 'loss_target', 'm_norm_mix_g', 'm_w_in', 'm_pool_w', 'm_pool_b', 'm_pool_scale', 'm_conv_w', 'm_conv_b', 'm_gate_a_w', 'm_gate_a_b', 'm_gate_x_w', 'm_gate_x_b', 'm_lru_L', 'm_w_out', 'm_norm_mlp_g', 'm_w_up', 'm_w_down', 'm_norm_ple_g', 'm_w_ple_gate', 'm_b_ple_gate', 'm_w_ple_proj', 'm_norm_final_g', 'v_norm_mix_g', 'v_w_in', 'v_pool_w', 'v_pool_b', 'v_pool_scale', 'v_conv_w', 'v_conv_b', 'v_gate_a_w', 'v_gate_a_b', 'v_gate_x_w', 'v_gate_x_b', 'v_lru_L', 'v_w_out', 'v_norm_mlp_g', 'v_w_up', 'v_w_down', 'v_norm_ple_g', 'v_w_ple_gate', 'v_b_ple_gate', 'v_w_ple_proj', 'v_norm_final_g']
TWIN_OUTPUTS = ['loss', 'grad_x', 'grad_norm_mix_g', 'grad_w_in', 'grad_pool_w', 'grad_pool_b', 'grad_pool_scale', 'grad_conv_w', 'grad_conv_b', 'grad_gate_a_w', 'grad_gate_a_b', 'grad_gate_x_w', 'grad_gate_x_b', 'grad_lru_L', 'grad_w_out', 'grad_norm_mlp_g', 'grad_w_up', 'grad_w_down', 'grad_norm_ple_g', 'grad_w_ple_gate', 'grad_b_ple_gate', 'grad_w_ple_proj', 'grad_norm_final_g', 'delta_norm_mix_g', 'delta_w_in', 'delta_pool_w', 'delta_pool_b', 'delta_pool_scale', 'delta_conv_w', 'delta_conv_b', 'delta_gate_a_w', 'delta_gate_a_b', 'delta_gate_x_w', 'delta_gate_x_b', 'delta_lru_L', 'delta_w_out', 'delta_norm_mlp_g', 'delta_w_up', 'delta_w_down', 'delta_norm_ple_g', 'delta_w_ple_gate', 'delta_b_ple_gate', 'delta_w_ple_proj', 'delta_norm_final_g', 'new_m_norm_mix_g', 'new_m_w_in', 'new_m_pool_w', 'new_m_pool_b', 'new_m_pool_scale', 'new_m_conv_w', 'new_m_conv_b', 'new_m_gate_a_w', 'new_m_gate_a_b', 'new_m_gate_x_w', 'new_m_gate_x_b', 'new_m_lru_L', 'new_m_w_out', 'new_m_norm_mlp_g', 'new_m_w_up', 'new_m_w_down', 'new_m_norm_ple_g', 'new_m_w_ple_gate', 'new_m_b_ple_gate', 'new_m_w_ple_proj', 'new_m_norm_final_g', 'new_v_norm_mix_g', 'new_v_w_in', 'new_v_pool_w', 'new_v_pool_b', 'new_v_pool_scale', 'new_v_conv_w', 'new_v_conv_b', 'new_v_gate_a_w', 'new_v_gate_a_b', 'new_v_gate_x_w', 'new_v_gate_x_b', 'new_v_lru_L', 'new_v_w_out', 'new_v_norm_mlp_g', 'new_v_w_up', 'new_v_w_down', 'new_v_norm_ple_g', 'new_v_w_ple_gate', 'new_v_b_ple_gate', 'new_v_w_ple_proj', 'new_v_norm_final_g']
TWIN_LEAF_KINDS = {'loss': 'loss', 'grad_x': 'grad_x', 'grad_norm_mix_g': 'grad_w', 'grad_w_in': 'grad_w', 'grad_pool_w': 'grad_w', 'grad_pool_b': 'grad_w', 'grad_pool_scale': 'grad_w', 'grad_conv_w': 'grad_w', 'grad_conv_b': 'grad_w', 'grad_gate_a_w': 'grad_w', 'grad_gate_a_b': 'grad_w', 'grad_gate_x_w': 'grad_w', 'grad_gate_x_b': 'grad_w', 'grad_lru_L': 'grad_w', 'grad_w_out': 'grad_w', 'grad_norm_mlp_g': 'grad_w', 'grad_w_up': 'grad_w', 'grad_w_down': 'grad_w', 'grad_norm_ple_g': 'grad_w', 'grad_w_ple_gate': 'grad_w', 'grad_b_ple_gate': 'grad_w', 'grad_w_ple_proj': 'grad_w', 'grad_norm_final_g': 'grad_w', 'delta_norm_mix_g': 'delta_w', 'delta_w_in': 'delta_w', 'delta_pool_w': 'delta_w', 'delta_pool_b': 'delta_w', 'delta_pool_scale': 'delta_w', 'delta_conv_w': 'delta_w', 'delta_conv_b': 'delta_w', 'delta_gate_a_w': 'delta_w', 'delta_gate_a_b': 'delta_w', 'delta_gate_x_w': 'delta_w', 'delta_gate_x_b': 'delta_w', 'delta_lru_L': 'delta_w', 'delta_w_out': 'delta_w', 'delta_norm_mlp_g': 'delta_w', 'delta_w_up': 'delta_w', 'delta_w_down': 'delta_w', 'delta_norm_ple_g': 'delta_w', 'delta_w_ple_gate': 'delta_w', 'delta_b_ple_gate': 'delta_w', 'delta_w_ple_proj': 'delta_w', 'delta_norm_final_g': 'delta_w', 'new_m_norm_mix_g': 'new_m', 'new_m_w_in': 'new_m', 'new_m_pool_w': 'new_m', 'new_m_pool_b': 'new_m', 'new_m_pool_scale': 'new_m', 'new_m_conv_w': 'new_m', 'new_m_conv_b': 'new_m', 'new_m_gate_a_w': 'new_m', 'new_m_gate_a_b': 'new_m', 'new_m_gate_x_w': 'new_m', 'new_m_gate_x_b': 'new_m', 'new_m_lru_L': 'new_m', 'new_m_w_out': 'new_m', 'new_m_norm_mlp_g': 'new_m', 'new_m_w_up': 'new_m', 'new_m_w_down': 'new_m', 'new_m_norm_ple_g': 'new_m', 'new_m_w_ple_gate': 'new_m', 'new_m_b_ple_gate': 'new_m', 'new_m_w_ple_proj': 'new_m', 'new_m_norm_final_g': 'new_m', 'new_v_norm_mix_g': 'new_v', 'new_v_w_in': 'new_v', 'new_v_pool_w': 'new_v', 'new_v_pool_b': 'new_v', 'new_v_pool_scale': 'new_v', 'new_v_conv_w': 'new_v', 'new_v_conv_b': 'new_v', 'new_v_gate_a_w': 'new_v', 'new_v_gate_a_b': 'new_v', 'new_v_gate_x_w': 'new_v', 'new_v_gate_x_b': 'new_v', 'new_v_lru_L': 'new_v', 'new_v_w_out': 'new_v', 'new_v_norm_mlp_g': 'new_v', 'new_v_w_up': 'new_v', 'new_v_w_down': 'new_v', 'new_v_norm_ple_g': 'new_v', 'new_v_w_ple_gate': 'new_v', 'new_v_b_ple_gate': 'new_v', 'new_v_w_ple_proj': 'new_v', 'new_v_norm_final_g': 'new_v'}


def _forward(args):
    return _fwd_reference(*[args[k] for k in FWD_PARAMS])


def _output_shape():
    def fwd():
        inp = _fwd_setup_inputs(0)
        return _fwd_reference(*[inp[k] for k in FWD_PARAMS])
    out = _jax.eval_shape(fwd)
    return out.shape, out.dtype

N_MICROBATCH = 1
ADAM_LR = 0.001
ADAM_B1 = 0.9
ADAM_B2 = 0.999
ADAM_EPS = 1e-08
ADAM_WD = 0.01
ADAM_STEP = 10
PER_EXAMPLE_BATCH_AXIS = {'x': 0, 'p': 1, 'loss_target': 0}
SHARED_INPUTS = []
_WEIGHT_DTYPES = {'norm_mix_g': _jnp.float32, 'w_in': _jnp.float32, 'pool_w': _jnp.float32, 'pool_b': _jnp.float32, 'pool_scale': _jnp.float32, 'conv_w': _jnp.float32, 'conv_b': _jnp.float32, 'gate_a_w': _jnp.float32, 'gate_a_b': _jnp.float32, 'gate_x_w': _jnp.float32, 'gate_x_b': _jnp.float32, 'lru_L': _jnp.float32, 'w_out': _jnp.float32, 'norm_mlp_g': _jnp.float32, 'w_up': _jnp.float32, 'w_down': _jnp.float32, 'norm_ple_g': _jnp.float32, 'w_ple_gate': _jnp.float32, 'b_ple_gate': _jnp.float32, 'w_ple_proj': _jnp.float32, 'norm_final_g': _jnp.float32}
MOMENT_SCALE = {'norm_mix_g': 1.623248e-01, 'w_in': 1.332354e-01, 'pool_w': 2.011498e-01, 'pool_b': 1.000216e+00, 'pool_scale': 2.152588e-01, 'conv_w': 8.227484e-02, 'conv_b': 6.363582e-01, 'gate_a_w': 2.725122e-02, 'gate_a_b': 2.058944e-02, 'gate_x_w': 5.052214e-02, 'gate_x_b': 2.959018e-02, 'lru_L': 3.765779e-02, 'w_out': 1.566775e-01, 'norm_mlp_g': 2.022284e-01, 'w_up': 1.030670e-01, 'w_down': 3.587338e-01, 'norm_ple_g': 4.554714e-02, 'w_ple_gate': 4.447232e-02, 'b_ple_gate': 1.636961e-01, 'w_ple_proj': 7.801231e-02, 'norm_final_g': 6.449946e+01}


def _to_microbatches(a, axis):
    t = _jnp.moveaxis(a, axis, 0)
    t = t.reshape((N_MICROBATCH, t.shape[0] // N_MICROBATCH) + t.shape[1:])
    return _jnp.moveaxis(t, 1, axis + 1)


def setup_inputs(seed: int = 0) -> dict:
    inp = _fwd_setup_inputs(seed)
    key = _jax.random.fold_in(_jax.random.key(seed), 7919)
    shape, _ = _output_shape()
    out = dict(inp)
    out["loss_target"] = _jax.random.normal(_jax.random.fold_in(key, 0), shape, _jnp.float32)
    for i, name in enumerate(TWIN_WEIGHTS):
        w = inp[name].astype(_jnp.float32)
        if MOMENT_SCALE is None:
            s = _jnp.sqrt(_jnp.mean(_jnp.square(w)) + 1e-30)
        else:
            s = MOMENT_SCALE[name]
        km, kv = _jax.random.split(_jax.random.fold_in(key, i + 1))
        out[name] = w
        out["m_" + name] = s * _jax.random.normal(km, w.shape, _jnp.float32)
        out["v_" + name] = (s * s) * _jax.random.uniform(kv, w.shape, _jnp.float32, 0.5, 1.5)
    if N_MICROBATCH > 1:
        for name, axis in PER_EXAMPLE_BATCH_AXIS.items():
            out[name] = _to_microbatches(out[name], axis)
    return {'x': out['x'], 'p': out['p'], 'norm_mix_g': out['norm_mix_g'], 'w_in': out['w_in'], 'pool_w': out['pool_w'], 'pool_b': out['pool_b'], 'pool_scale': out['pool_scale'], 'conv_w': out['conv_w'], 'conv_b': out['conv_b'], 'gate_a_w': out['gate_a_w'], 'gate_a_b': out['gate_a_b'], 'gate_x_w': out['gate_x_w'], 'gate_x_b': out['gate_x_b'], 'lru_L': out['lru_L'], 'w_out': out['w_out'], 'norm_mlp_g': out['norm_mlp_g'], 'w_up': out['w_up'], 'w_down': out['w_down'], 'norm_ple_g': out['norm_ple_g'], 'w_ple_gate': out['w_ple_gate'], 'b_ple_gate': out['b_ple_gate'], 'w_ple_proj': out['w_ple_proj'], 'norm_final_g': out['norm_final_g'], 'loss_target': out['loss_target'], 'm_norm_mix_g': out['m_norm_mix_g'], 'm_w_in': out['m_w_in'], 'm_pool_w': out['m_pool_w'], 'm_pool_b': out['m_pool_b'], 'm_pool_scale': out['m_pool_scale'], 'm_conv_w': out['m_conv_w'], 'm_conv_b': out['m_conv_b'], 'm_gate_a_w': out['m_gate_a_w'], 'm_gate_a_b': out['m_gate_a_b'], 'm_gate_x_w': out['m_gate_x_w'], 'm_gate_x_b': out['m_gate_x_b'], 'm_lru_L': out['m_lru_L'], 'm_w_out': out['m_w_out'], 'm_norm_mlp_g': out['m_norm_mlp_g'], 'm_w_up': out['m_w_up'], 'm_w_down': out['m_w_down'], 'm_norm_ple_g': out['m_norm_ple_g'], 'm_w_ple_gate': out['m_w_ple_gate'], 'm_b_ple_gate': out['m_b_ple_gate'], 'm_w_ple_proj': out['m_w_ple_proj'], 'm_norm_final_g': out['m_norm_final_g'], 'v_norm_mix_g': out['v_norm_mix_g'], 'v_w_in': out['v_w_in'], 'v_pool_w': out['v_pool_w'], 'v_pool_b': out['v_pool_b'], 'v_pool_scale': out['v_pool_scale'], 'v_conv_w': out['v_conv_w'], 'v_conv_b': out['v_conv_b'], 'v_gate_a_w': out['v_gate_a_w'], 'v_gate_a_b': out['v_gate_a_b'], 'v_gate_x_w': out['v_gate_x_w'], 'v_gate_x_b': out['v_gate_x_b'], 'v_lru_L': out['v_lru_L'], 'v_w_out': out['v_w_out'], 'v_norm_mlp_g': out['v_norm_mlp_g'], 'v_w_up': out['v_w_up'], 'v_w_down': out['v_w_down'], 'v_norm_ple_g': out['v_norm_ple_g'], 'v_w_ple_gate': out['v_w_ple_gate'], 'v_b_ple_gate': out['v_b_ple_gate'], 'v_w_ple_proj': out['v_w_ple_proj'], 'v_norm_final_g': out['v_norm_final_g']}


def _loss(weights, diff, rest, loss_target):
    with _jax.named_scope("forward"):
        args = {**rest, TWIN_DIFF_INPUT: diff, **{k: w.astype(_WEIGHT_DTYPES[k]) for k, w in weights.items()}}
        y = _forward(args)
    with _jax.named_scope("loss_head"):
        err = _jnp.square(y.astype(_jnp.float32) - loss_target)
        return 0.5 * _jnp.sum(_jnp.mean(err, axis=-1)) if err.ndim else 0.5 * err


def _adamw(w, g, m, v):
    m = ADAM_B1 * m + (1.0 - ADAM_B1) * g
    v = ADAM_B2 * v + (1.0 - ADAM_B2) * _jnp.square(g)
    m_hat = m / (1.0 - ADAM_B1 ** ADAM_STEP)
    v_hat = v / (1.0 - ADAM_B2 ** ADAM_STEP)
    delta = -ADAM_LR * (m_hat / (_jnp.sqrt(v_hat) + ADAM_EPS) + ADAM_WD * w)
    return delta, m, v


def reference(x, p, norm_mix_g, w_in, pool_w, pool_b, pool_scale, conv_w, conv_b, gate_a_w, gate_a_b, gate_x_w, gate_x_b, lru_L, w_out, norm_mlp_g, w_up, w_down, norm_ple_g, w_ple_gate, b_ple_gate, w_ple_proj, norm_final_g, loss_target, m_norm_mix_g, m_w_in, m_pool_w, m_pool_b, m_pool_scale, m_conv_w, m_conv_b, m_gate_a_w, m_gate_a_b, m_gate_x_w, m_gate_x_b, m_lru_L, m_w_out, m_norm_mlp_g, m_w_up, m_w_down, m_norm_ple_g, m_w_ple_gate, m_b_ple_gate, m_w_ple_proj, m_norm_final_g, v_norm_mix_g, v_w_in, v_pool_w, v_pool_b, v_pool_scale, v_conv_w, v_conv_b, v_gate_a_w, v_gate_a_b, v_gate_x_w, v_gate_x_b, v_lru_L, v_w_out, v_norm_mlp_g, v_w_up, v_w_down, v_norm_ple_g, v_w_ple_gate, v_b_ple_gate, v_w_ple_proj, v_norm_final_g):
    given = dict(x=x, p=p, norm_mix_g=norm_mix_g, w_in=w_in, pool_w=pool_w, pool_b=pool_b, pool_scale=pool_scale, conv_w=conv_w, conv_b=conv_b, gate_a_w=gate_a_w, gate_a_b=gate_a_b, gate_x_w=gate_x_w, gate_x_b=gate_x_b, lru_L=lru_L, w_out=w_out, norm_mlp_g=norm_mlp_g, w_up=w_up, w_down=w_down, norm_ple_g=norm_ple_g, w_ple_gate=w_ple_gate, b_ple_gate=b_ple_gate, w_ple_proj=w_ple_proj, norm_final_g=norm_final_g, loss_target=loss_target, m_norm_mix_g=m_norm_mix_g, m_w_in=m_w_in, m_pool_w=m_pool_w, m_pool_b=m_pool_b, m_pool_scale=m_pool_scale, m_conv_w=m_conv_w, m_conv_b=m_conv_b, m_gate_a_w=m_gate_a_w, m_gate_a_b=m_gate_a_b, m_gate_x_w=m_gate_x_w, m_gate_x_b=m_gate_x_b, m_lru_L=m_lru_L, m_w_out=m_w_out, m_norm_mlp_g=m_norm_mlp_g, m_w_up=m_w_up, m_w_down=m_w_down, m_norm_ple_g=m_norm_ple_g, m_w_ple_gate=m_w_ple_gate, m_b_ple_gate=m_b_ple_gate, m_w_ple_proj=m_w_ple_proj, m_norm_final_g=m_norm_final_g, v_norm_mix_g=v_norm_mix_g, v_w_in=v_w_in, v_pool_w=v_pool_w, v_pool_b=v_pool_b, v_pool_scale=v_pool_scale, v_conv_w=v_conv_w, v_conv_b=v_conv_b, v_gate_a_w=v_gate_a_w, v_gate_a_b=v_gate_a_b, v_gate_x_w=v_gate_x_w, v_gate_x_b=v_gate_x_b, v_lru_L=v_lru_L, v_w_out=v_w_out, v_norm_mlp_g=v_norm_mlp_g, v_w_up=v_w_up, v_w_down=v_w_down, v_norm_ple_g=v_norm_ple_g, v_w_ple_gate=v_w_ple_gate, v_b_ple_gate=v_b_ple_gate, v_w_ple_proj=v_w_ple_proj, v_norm_final_g=v_norm_final_g)
    weights = {n: given[n] for n in TWIN_WEIGHTS}
    shared = {n: given[n] for n in SHARED_INPUTS}
    per_example = {n: given[n] for n in ['x', 'p']}
    grad_fn = _jax.value_and_grad(_loss, argnums=(0, 1))

    def one_microbatch(ex, loss_target):
        ex = dict(ex)
        diff = ex.pop(TWIN_DIFF_INPUT)
        return grad_fn(weights, diff, {**shared, **ex}, loss_target)

    if N_MICROBATCH == 1:
        loss, (grad_w, grad_x) = one_microbatch(per_example, given["loss_target"])
    else:
        def body(carry, xs):
            loss_sum, grad_sum = carry
            l_k, (gw_k, gx_k) = one_microbatch(xs[0], xs[1])
            with _jax.named_scope("update"):
                return (loss_sum + l_k, _jax.tree.map(_jnp.add, grad_sum, gw_k)), gx_k

        init = (_jnp.zeros((), _jnp.float32), _jax.tree.map(_jnp.zeros_like, weights))
        (loss, grad_w), grad_x = _jax.lax.scan(body, init, (per_example, given["loss_target"]))
    with _jax.named_scope("update"):
        delta_w, new_m, new_v = {}, {}, {}
        for n in TWIN_WEIGHTS:
            delta_w[n], new_m[n], new_v[n] = _adamw(weights[n], grad_w[n], given["m_" + n], given["v_" + n])
    return (loss, grad_x, *[grad_w[n] for n in TWIN_WEIGHTS], *[delta_w[n] for n in TWIN_WEIGHTS],
            *[new_m[n] for n in TWIN_WEIGHTS], *[new_v[n] for n in TWIN_WEIGHTS])
```

```python
import functools

import jax
import jax.numpy as jnp
from jax import lax
from jax.experimental import pallas as pl
from jax.experimental.pallas import tpu as pltpu

F32 = jnp.float32
BF16 = jnp.bfloat16
MESH = pl.DeviceIdType.MESH

N_DEV = 8
RMS_EPS = 1e-6
LRU_C = 8.0
POOL_WINDOWS = (2, 4, 8, 16)
N_POOL_GROUPS = 4
LRU_HEADS = 8
HALO = 16
SUB = 8
GELU_C0 = 0.7978845608028654
GELU_C1 = 0.044715

ADAM_LR = 0.001
ADAM_B1 = 0.9
ADAM_B2 = 0.999
ADAM_EPS = 1e-08
ADAM_WD = 0.01
ADAM_STEP = 10

VMEM_LIMIT = 60 * 1024 * 1024
TIME_BLOCKS = dict(mix_fwd=512, mlp_fwd=512, ple=512, mlp_bwd=512, mix_bwd=256)
MLP_BWD_SPLIT = 2


def _params(n_arbitrary=1):
    return pltpu.CompilerParams(dimension_semantics=("arbitrary",) * n_arbitrary, vmem_limit_bytes=VMEM_LIMIT)


def _dot(a, b):
    return jnp.dot(a, b, preferred_element_type=F32)


def _dot_nt(a, b):
    return lax.dot_general(a, b, (((1,), (1,)), ((), ())), preferred_element_type=F32)


def _dot_tn(a, b):
    return lax.dot_general(a, b, (((0,), (0,)), ((), ())), preferred_element_type=F32)


def _rms_fwd(x, g):
    r = lax.rsqrt(jnp.mean(x * x, axis=-1, keepdims=True) + RMS_EPS)
    xh = x * r
    return xh * g, xh, r


def _rms_bwd(xh, r, g, dz):
    dxh = dz * g
    return r * (dxh - xh * jnp.mean(dxh * xh, axis=-1, keepdims=True))


def _colsum(a):
    return jnp.sum(a, axis=0, keepdims=True)


def _sigmoid(a):
    return 1.0 / (1.0 + jnp.exp(-a))


def _gelu_parts(u):
    u2 = u * u
    th = jnp.tanh(GELU_C0 * (u + GELU_C1 * u * u2))
    gel = 0.5 * u * (1.0 + th)
    dgel = 0.5 * (1.0 + th) + 0.5 * u * (1.0 - th * th) * (GELU_C0 * (1.0 + 3.0 * GELU_C1 * u2))
    return gel, dgel


def _const_spec(shape):
    nd = len(shape)
    return pl.BlockSpec(shape, lambda i: (0,) * nd, pipeline_mode=pl.Buffered(1))


def _pool_windows(up_ext, n, forward):
    sh = (lambda k: k) if forward else (lambda k: n - k)
    s2 = up_ext + pltpu.roll(up_ext, sh(1), 0)
    t4 = s2[:, 128:]
    s4 = t4 + pltpu.roll(t4, sh(2), 0)
    t8 = s4[:, 128:]
    s8 = t8 + pltpu.roll(t8, sh(4), 0)
    t16 = s8[:, 128:]
    s16 = t16 + pltpu.roll(t16, sh(8), 0)
    return jnp.concatenate([s2[:, :128], s4[:, :128], s8[:, :128], s16], axis=1)


def _inv_count(t0, tb):
    t = (t0 + lax.broadcasted_iota(jnp.int32, (tb, 1), 0) + 1).astype(F32)
    cols = [jnp.broadcast_to(1.0 / jnp.minimum(t, float(w)), (tb, 128)) for w in POOL_WINDOWS]
    return jnp.concatenate(cols, axis=1)


def _lru_gates(xb, wg, gb, c_l, first_row):
    gp = _dot(xb.astype(BF16), wg) + gb
    r = _sigmoid(gp[:, :512])
    ig = _sigmoid(gp[:, 512:])
    la = c_l * r
    a = jnp.exp(la)
    a2 = a * a
    m2 = -jnp.tanh(la) * (a2 + 1.0)
    mult = jnp.where(first_row, 1.0, jnp.sqrt(m2))
    return r, ig, a, a2, m2, mult


def _log_sigmoid(v):
    return -(jnp.maximum(-v, 0.0) + jnp.log1p(jnp.exp(-jnp.abs(v))))


def _conv_fwd(ul_ext, cw, cb):
    return (cb + cw[3:4, :] * ul_ext + cw[2:3, :] * pltpu.roll(ul_ext, 1, 0)
            + cw[1:2, :] * pltpu.roll(ul_ext, 2, 0) + cw[0:1, :] * pltpu.roll(ul_ext, 3, 0))


def _mix_fwd(x, g_mix, w_in, wp_bd, pool_b, pool_scale, conv_w, conv_b, wg_bd, gate_b, lru_l, w_out, tb):
    t_len, d = x.shape
    nb = t_len // tb

    def body(x_ref, g_ref, win_ref, wp_ref, pb_ref, ps_ref, cw_ref, cb_ref, wg_ref, gb_ref, l_ref, wout_ref,
             h1_ref, z1_ref, proj_ref, hs_ref, cat_ref, ext_ref, a_ref, b_ref, hc_ref):
        i = pl.program_id(0)

        @pl.when(i == 0)
        def _():
            ext_ref[0:HALO, :] = jnp.zeros((HALO, 1024), F32)
            hc_ref[...] = jnp.zeros_like(hc_ref)

        xv = x_ref[...]
        z, _, _ = _rms_fwd(xv, g_ref[...])
        zb = z.astype(BF16)
        z1_ref[...] = zb
        proj = _dot(zb, win_ref[...])
        proj_ref[...] = proj
        ext_ref[HALO:, :] = proj[:, 0:1024]
        ug = proj[:, 1024:1536]
        n = tb + HALO
        up_ext = ext_ref[:, 0:512]
        win = _pool_windows(up_ext, n, True)[HALO:]
        dpool = win * _inv_count(i * tb, tb) - proj[:, 0:512]
        q = _dot(dpool.astype(BF16), wp_ref[...]) + pb_ref[...]
        y_pool = q * ps_ref[...]
        xb = _conv_fwd(ext_ref[:, 512:1024], cw_ref[...], cb_ref[...])[HALO:]
        first_row = (i * tb + lax.broadcasted_iota(jnp.int32, (tb, 1), 0)) == 0
        c_l = LRU_C * _log_sigmoid(l_ref[...])
        _, ig, a, _, _, mult = _lru_gates(xb, wg_ref[...], gb_ref[...], c_l, first_row)
        a_ref[...] = a
        b_ref[...] = mult * (ig * xb)
        row = lax.broadcasted_iota(jnp.int32, (SUB, 512), 0)

        def group(j, hprev):
            o = pl.multiple_of(j * SUB, SUB)
            a8 = a_ref[pl.ds(o, SUB), :]
            b8 = b_ref[pl.ds(o, SUB), :]
            for sh in (1, 2, 4):
                ash = jnp.where(row >= sh, pltpu.roll(a8, sh, 0), 1.0)
                bsh = jnp.where(row >= sh, pltpu.roll(b8, sh, 0), 0.0)
                b8 = a8 * bsh + b8
                a8 = a8 * ash
            h8 = a8 * hprev + b8
            hs_ref[pl.ds(o, SUB), :] = h8
            return jnp.broadcast_to(h8[SUB - 1:SUB, :], (SUB, 512))

        hc_ref[...] = lax.fori_loop(0, tb // SUB, group, hc_ref[...])
        gel, _ = _gelu_parts(ug)
        y_lru = hs_ref[...] * gel
        catb = jnp.concatenate([y_pool, y_lru], axis=1).astype(BF16)
        cat_ref[...] = catb
        h1_ref[...] = xv + _dot(catb, wout_ref[...])
        ext_ref[0:HALO, :] = ext_ref[tb:tb + HALO, :]

    row_spec = lambda w: pl.BlockSpec((tb, w), lambda i: (i, 0))
    smalls = [g_mix, w_in, wp_bd, pool_b, pool_scale, conv_w, conv_b, wg_bd, gate_b, lru_l, w_out]
    return pl.pallas_call(
        body, name="mix_fwd", grid=(nb,),
        in_specs=[row_spec(d)] + [_const_spec(s.shape) for s in smalls],
        out_specs=[row_spec(d), row_spec(d), row_spec(1536), row_spec(512), row_spec(1024)],
        out_shape=[jax.ShapeDtypeStruct((t_len, d), F32), jax.ShapeDtypeStruct((t_len, d), BF16),
                   jax.ShapeDtypeStruct((t_len, 1536), F32), jax.ShapeDtypeStruct((t_len, 512), F32),
                   jax.ShapeDtypeStruct((t_len, 1024), BF16)],
        scratch_shapes=[pltpu.VMEM((tb + HALO, 1024), F32), pltpu.VMEM((tb, 512), F32), pltpu.VMEM((tb, 512), F32),
                        pltpu.VMEM((SUB, 512), F32)],
        compiler_params=_params(),
    )(x, *smalls)


def _mlp_fwd(h1, g_mlp, w_up, w_down, tb):
    t_len, d = h1.shape
    nb = t_len // tb
    n_chunk, _, fc = w_up.shape

    def body(h1_ref, g_ref, wup_ref, wdn_ref, h2_ref, z2_ref, up_ref):
        xv = h1_ref[...]
        z, _, _ = _rms_fwd(xv, g_ref[...])
        zb = z.astype(BF16)
        z2_ref[...] = zb
        acc = xv
        for c in range(n_chunk):
            u = _dot(zb, wup_ref[c])
            up_ref[:, c * fc:(c + 1) * fc] = u.astype(BF16)
            act = jnp.square(jnp.maximum(u, 0.0)).astype(BF16)
            acc = acc + _dot(act, wdn_ref[c * fc:(c + 1) * fc, :])
        h2_ref[...] = acc

    row_spec = lambda w: pl.BlockSpec((tb, w), lambda i: (i, 0))
    return pl.pallas_call(
        body, name="mlp_fwd", grid=(nb,),
        in_specs=[row_spec(d), _const_spec(g_mlp.shape), _const_spec(w_up.shape), _const_spec(w_down.shape)],
        out_specs=[row_spec(d), row_spec(d), row_spec(n_chunk * fc)],
        out_shape=[jax.ShapeDtypeStruct((t_len, d), F32), jax.ShapeDtypeStruct((t_len, d), BF16),
                   jax.ShapeDtypeStruct((t_len, n_chunk * fc), BF16)],
        compiler_params=_params(),
    )(h1, g_mlp, w_up, w_down)


def _ple(h2, p, target, g_ple, w_gate, b_gate, w_proj, g_final, tb):
    t_len, d = h2.shape
    nb = t_len // tb
    pd = p.shape[1]

    def body(h2_ref, p_ref, tgt_ref, g_ref, wg_ref, bg_ref, wp_ref, gf_ref,
             dh2_ref, vec_ref, dwg_out, dwp_out, dwg_acc, dwp_acc):
        i = pl.program_id(0)

        @pl.when(i == 0)
        def _():
            vec_ref[...] = jnp.zeros_like(vec_ref)
            dwg_acc[...] = jnp.zeros_like(dwg_acc)
            dwp_acc[...] = jnp.zeros_like(dwp_acc)

        h2 = h2_ref[...]
        g2 = g_ref[...]
        z3, xh2, r2 = _rms_fwd(h2, g2)
        z3b = z3.astype(BF16)
        gate = _sigmoid(_dot(z3b, wg_ref[...]) + bg_ref[...])
        pb = p_ref[...].astype(BF16)
        pp = _dot(pb, wp_ref[...])
        h3 = h2 + gate * pp
        gf = gf_ref[...]
        y, xh3, r3 = _rms_fwd(h3, gf)
        err = y - tgt_ref[...]
        loss_rows = jnp.mean(err * err, axis=-1, keepdims=True)
        dy = err * (1.0 / d)
        dh3 = _rms_bwd(xh3, r3, gf, dy)
        dgl = (dh3 * pp) * (gate * (1.0 - gate))
        dpp = dh3 * gate
        dglb = dgl.astype(BF16)
        dwg_acc[...] += _dot_tn(z3b, dglb)
        dwp_acc[...] += _dot_tn(pb, dpp.astype(BF16))
        dz3 = _dot_nt(dglb, wg_ref[...])
        dh2_ref[...] = dh3 + _rms_bwd(xh2, r2, g2, dz3)
        vec_ref[0:1, :] += _colsum(dgl)
        vec_ref[1:2, :] += _colsum(dz3 * xh2)
        vec_ref[2:3, :] += _colsum(dy * xh3)
        vec_ref[3:4, :] += 0.5 * jnp.sum(loss_rows)

        @pl.when(i == nb - 1)
        def _():
            pltpu.sync_copy(dwg_acc, dwg_out)
            pltpu.sync_copy(dwp_acc, dwp_out)

    row_spec = lambda w: pl.BlockSpec((tb, w), lambda i: (i, 0))
    any_spec = pl.BlockSpec(memory_space=pl.ANY)
    smalls = [g_ple, w_gate, b_gate, w_proj, g_final]
    return pl.pallas_call(
        body, name="ple_fwd_bwd", grid=(nb,),
        in_specs=[row_spec(d), row_spec(pd), row_spec(d)] + [_const_spec(s.shape) for s in smalls],
        out_specs=[row_spec(d), pl.BlockSpec((8, d), lambda i: (0, 0)), any_spec, any_spec],
        out_shape=[jax.ShapeDtypeStruct((t_len, d), F32), jax.ShapeDtypeStruct((8, d), F32),
                   jax.ShapeDtypeStruct(w_gate.shape, F32), jax.ShapeDtypeStruct(w_proj.shape, F32)],
        scratch_shapes=[pltpu.VMEM(w_gate.shape, F32), pltpu.VMEM(w_proj.shape, F32)],
        compiler_params=_params(),
    )(h2, p, target, *smalls)


def _mlp_bwd_part(part, n_part, dh2, z2, up, w_up, w_down, dz2_prev, h1, g_mlp, tb):
    t_len, d = dh2.shape
    nb = t_len // tb
    n_chunk_all, _, fc = w_up.shape
    n_chunk = n_chunk_all // n_part
    first, last = part == 0, part == n_part - 1

    def body(*refs):
        refs = list(refs)
        dh2_ref, z2_ref, up_ref, wup_ref, wdn_ref = refs[:5]
        del refs[:5]
        dzp_ref = None if first else refs.pop(0)
        h1_ref, g_ref = (refs.pop(0), refs.pop(0)) if last else (None, None)
        out_ref = refs.pop(0)
        vec_ref = refs.pop(0) if last else None
        dwup_out, dwdn_out, dwup_acc, dwdn_acc = refs
        i = pl.program_id(0)

        @pl.when(i == 0)
        def _():
            dwup_acc[...] = jnp.zeros_like(dwup_acc)
            dwdn_acc[...] = jnp.zeros_like(dwdn_acc)
            if last:
                vec_ref[...] = jnp.zeros_like(vec_ref)

        dh2 = dh2_ref[...]
        dh2b = dh2.astype(BF16)
        z2b = z2_ref[...]
        dz2 = jnp.zeros((tb, d), F32) if first else dzp_ref[...]
        for c in range(n_chunk):
            u = up_ref[:, c * fc:(c + 1) * fc].astype(F32)
            ur = jnp.maximum(u, 0.0)
            dact = _dot_nt(dh2b, wdn_ref[c * fc:(c + 1) * fc, :])
            dupb = (dact * (2.0 * ur)).astype(BF16)
            dwdn_acc[c * fc:(c + 1) * fc, :] += _dot_tn((ur * ur).astype(BF16), dh2b)
            dwup_acc[c] += _dot_tn(z2b, dupb)
            dz2 = dz2 + _dot_nt(dupb, wup_ref[c])
        if last:
            g = g_ref[...]
            _, xh, r = _rms_fwd(h1_ref[...], g)
            out_ref[...] = dh2 + _rms_bwd(xh, r, g, dz2)
            vec_ref[0:1, :] += _colsum(dz2 * xh)
        else:
            out_ref[...] = dz2

        @pl.when(i == nb - 1)
        def _():
            pltpu.sync_copy(dwup_acc, dwup_out)
            pltpu.sync_copy(dwdn_acc, dwdn_out)

    row_spec = lambda w: pl.BlockSpec((tb, w), lambda i: (i, 0))
    any_spec = pl.BlockSpec(memory_space=pl.ANY)
    args = [dh2, z2, up, w_up, w_down]
    in_specs = [row_spec(d), row_spec(d), pl.BlockSpec((tb, n_chunk * fc), lambda i: (i, part)),
                pl.BlockSpec((n_chunk, d, fc), lambda i: (part, 0, 0), pipeline_mode=pl.Buffered(1)),
                pl.BlockSpec((n_chunk * fc, d), lambda i: (part, 0), pipeline_mode=pl.Buffered(1))]
    if not first:
        args.append(dz2_prev)
        in_specs.append(row_spec(d))
    if last:
        args += [h1, g_mlp]
        in_specs += [row_spec(d), _const_spec(g_mlp.shape)]
    out_specs = [row_spec(d)]
    out_shape = [jax.ShapeDtypeStruct((t_len, d), F32)]
    if last:
        out_specs.append(pl.BlockSpec((8, d), lambda i: (0, 0)))
        out_shape.append(jax.ShapeDtypeStruct((8, d), F32))
    out_specs += [any_spec, any_spec]
    out_shape += [jax.ShapeDtypeStruct((n_chunk, d, fc), F32), jax.ShapeDtypeStruct((n_chunk * fc, d), F32)]
    return pl.pallas_call(
        body, name=f"mlp_bwd_{part}", grid=(nb,), in_specs=in_specs, out_specs=out_specs, out_shape=out_shape,
        scratch_shapes=[pltpu.VMEM((n_chunk, d, fc), F32), pltpu.VMEM((n_chunk * fc, d), F32)],
        compiler_params=_params(),
    )(*args)


def _mix_bwd(dh1, x, z1, proj, hs, cat, g_mix, w_in, wp_bd, pool_b, pool_scale, conv_w, conv_b, wg_bd, gate_b, lru_l,
             w_out, tb):
    t_len, d = x.shape
    nb = t_len // tb

    def body(dh1_ref, x_ref, z1_ref, proj_ref, projh_ref, hs_ref, hsh_ref, cat_ref,
             g_ref, win_ref, wp_ref, pb_ref, ps_ref, cw_ref, cb_ref, wg_ref, gb_ref, l_ref, wout_ref,
             dx_ref, v512_ref, v1024_ref, dwin_out, dwout_out, dwp_out, dwg_out,
             dwin_acc, dwout_acc, dwp_acc, dwg_acc, ext_ref, a_ref, b_ref, gs_ref,
             ehead_ref, dxbhead_ref, ahead_ref, gc_ref):
        i = pl.program_id(0)
        tbk = nb - 1 - i

        @pl.when(i == 0)
        def _():
            for ref in (v512_ref, v1024_ref, dwin_acc, dwout_acc, dwp_acc, dwg_acc, ehead_ref, dxbhead_ref, ahead_ref,
                        gc_ref):
                ref[...] = jnp.zeros_like(ref)

        dh1 = dh1_ref[...]
        dh1b = dh1.astype(BF16)
        dcat = _dot_nt(dh1b, wout_ref[...])
        dwout_acc[...] += _dot_tn(cat_ref[...], dh1b)

        proj = proj_ref[...]
        has_prev = (tbk > 0).astype(F32)
        ext_ref[0:HALO, :] = projh_ref[:, 0:1024] * has_prev
        ext_ref[HALO:, :] = proj[:, 0:1024]
        ug = proj[:, 1024:1536]
        n = tb + HALO
        inv = _inv_count(tbk * tb, tb)

        up_ext = ext_ref[:, 0:512]
        win = _pool_windows(up_ext, n, True)[HALO:]
        dpool = win * inv - proj[:, 0:512]
        dpoolb = dpool.astype(BF16)
        q = _dot(dpoolb, wp_ref[...]) + pb_ref[...]
        dyp = dcat[:, 0:512]
        dq = dyp * ps_ref[...]
        dqb = dq.astype(BF16)
        v512_ref[0:1, :] += _colsum(dyp * q)
        v512_ref[1:2, :] += _colsum(dq)
        dwp_acc[...] += _dot_tn(dpoolb, dqb)
        dd = _dot_nt(dqb, wp_ref[...])
        e = dd * inv
        e_ext = jnp.concatenate([e, ehead_ref[...]], axis=0)
        du_pool = _pool_windows(e_ext, n, False)[0:tb] - dd
        ehead_ref[...] = e[0:HALO]

        gel, dgel = _gelu_parts(ug)
        hsv = hs_ref[...]
        dcl = dcat[:, 512:1024]
        dhs = dcl * gel
        dug = dcl * hsv * dgel
        ul_ext = ext_ref[:, 512:1024]
        cw = cw_ref[...]
        xb = _conv_fwd(ul_ext, cw, cb_ref[...])[HALO:]
        first_row = (tbk * tb + lax.broadcasted_iota(jnp.int32, (tb, 1), 0)) == 0
        c_l = LRU_C * _log_sigmoid(l_ref[...])
        r, ig, a, a2, m2, mult = _lru_gates(xb, wg_ref[...], gb_ref[...], c_l, first_row)
        a_ext = jnp.concatenate([a, ahead_ref[...]], axis=0)
        a_ref[...] = pltpu.roll(a_ext, tb + SUB - 1, 0)[0:tb]
        b_ref[...] = dhs
        ahead_ref[...] = a[0:SUB]
        row = lax.broadcasted_iota(jnp.int32, (SUB, 512), 0)

        def group(jj, gnext):
            o = pl.multiple_of((tb // SUB - 1 - jj) * SUB, SUB)
            a8 = a_ref[pl.ds(o, SUB), :]
            b8 = b_ref[pl.ds(o, SUB), :]
            for sh in (1, 2, 4):
                ash = jnp.where(row < SUB - sh, pltpu.roll(a8, SUB - sh, 0), 1.0)
                bsh = jnp.where(row < SUB - sh, pltpu.roll(b8, SUB - sh, 0), 0.0)
                b8 = a8 * bsh + b8
                a8 = a8 * ash
            g8 = a8 * gnext + b8
            gs_ref[pl.ds(o, SUB), :] = g8
            return jnp.broadcast_to(g8[0:1, :], (SUB, 512))

        gc_ref[...] = lax.fori_loop(0, tb // SUB, group, gc_ref[...])
        gsum = gs_ref[...]
        hs_ext = jnp.concatenate([hsh_ref[...] * has_prev, hsv], axis=0)
        hprev = pltpu.roll(hs_ext, 1, 0)[SUB:]
        da = gsum * hprev
        dmult = jnp.where(first_row, 0.0, gsum * (ig * xb))
        di = gsum * mult * xb
        dxb = gsum * mult * ig
        dla = da * a - dmult * a2 * lax.rsqrt(m2)
        dr = dla * c_l
        v512_ref[3:4, :] += _colsum(dla * r)
        dgp = jnp.concatenate([dr * r * (1.0 - r), di * ig * (1.0 - ig)], axis=1)
        dgpb = dgp.astype(BF16)
        v1024_ref[0:1, :] += _colsum(dgp)
        dwg_acc[...] += _dot_tn(xb.astype(BF16), dgpb)
        dxb = dxb + _dot_nt(dgpb, wg_ref[...])
        n8 = tb + SUB
        dxb_ext = jnp.concatenate([dxb, dxbhead_ref[...]], axis=0)
        du_lru = (cw[3:4, :] * dxb + cw[2:3, :] * pltpu.roll(dxb_ext, n8 - 1, 0)[0:tb]
                  + cw[1:2, :] * pltpu.roll(dxb_ext, n8 - 2, 0)[0:tb] + cw[0:1, :] * pltpu.roll(dxb_ext, n8 - 3, 0)[0:tb])
        dxbhead_ref[...] = dxb[0:SUB]
        v512_ref[2:3, :] += _colsum(dxb)
        for j in range(4):
            shifted = ul_ext if j == 0 else pltpu.roll(ul_ext, j, 0)
            v512_ref[4 + (3 - j):5 + (3 - j), :] += _colsum(dxb * shifted[HALO:])

        dprojb = jnp.concatenate([du_pool, du_lru, dug], axis=1).astype(BF16)
        dwin_acc[...] += _dot_tn(z1_ref[...], dprojb)
        dz1 = _dot_nt(dprojb, win_ref[...])
        g = g_ref[...]
        _, xh, rr = _rms_fwd(x_ref[...], g)
        dx_ref[...] = dh1 + _rms_bwd(xh, rr, g, dz1)
        v1024_ref[1:2, :] += _colsum(dz1 * xh)

        @pl.when(i == nb - 1)
        def _():
            v512_ref[3:4, :] = v512_ref[3:4, :] * (LRU_C * _sigmoid(-l_ref[...]))
            pltpu.sync_copy(dwin_acc, dwin_out)
            pltpu.sync_copy(dwout_acc, dwout_out)
            pltpu.sync_copy(dwp_acc, dwp_out)
            pltpu.sync_copy(dwg_acc, dwg_out)

    rev = lambda w: pl.BlockSpec((tb, w), lambda i: (nb - 1 - i, 0))
    halo = lambda rows, w: pl.BlockSpec((rows, w), lambda i: (jnp.maximum((nb - 1 - i) * (tb // rows) - 1, 0), 0))
    any_spec = pl.BlockSpec(memory_space=pl.ANY)
    smalls = [g_mix, w_in, wp_bd, pool_b, pool_scale, conv_w, conv_b, wg_bd, gate_b, lru_l, w_out]
    return pl.pallas_call(
        body, name="mix_bwd", grid=(nb,),
        in_specs=[rev(d), rev(d), rev(d), rev(1536), halo(HALO, 1536), rev(512), halo(SUB, 512), rev(1024)]
        + [_const_spec(s.shape) for s in smalls],
        out_specs=[rev(d), pl.BlockSpec((8, 512), lambda i: (0, 0)), pl.BlockSpec((8, 1024), lambda i: (0, 0)),
                   any_spec, any_spec, any_spec, any_spec],
        out_shape=[jax.ShapeDtypeStruct((t_len, d), F32), jax.ShapeDtypeStruct((8, 512), F32),
                   jax.ShapeDtypeStruct((8, 1024), F32), jax.ShapeDtypeStruct(w_in.shape, F32),
                   jax.ShapeDtypeStruct(w_out.shape, F32), jax.ShapeDtypeStruct(wp_bd.shape, F32),
                   jax.ShapeDtypeStruct(wg_bd.shape, F32)],
        scratch_shapes=[pltpu.VMEM(w_in.shape, F32), pltpu.VMEM(w_out.shape, F32), pltpu.VMEM(wp_bd.shape, F32),
                        pltpu.VMEM(wg_bd.shape, F32), pltpu.VMEM((tb + HALO, 1024), F32), pltpu.VMEM((tb, 512), F32),
                        pltpu.VMEM((tb, 512), F32), pltpu.VMEM((tb, 512), F32), pltpu.VMEM((HALO, 512), F32),
                        pltpu.VMEM((SUB, 512), F32), pltpu.VMEM((SUB, 512), F32), pltpu.VMEM((SUB, 512), F32)],
        compiler_params=_params(),
    )(dh1, x, z1, proj, proj, hs, hs, cat, *smalls)


def _my_index():
    return 4 * lax.axis_index("x") + 2 * lax.axis_index("y") + lax.axis_index("c")


def _exchange(name, srcs, scatter):
    n = len(srcs)
    shapes = [s.shape[1:] if scatter else s.shape for s in srcs]

    def body(*refs):
        ins, outs = refs[:n], refs[n:2 * n]
        send_sems, recv_sems, local_sems = refs[2 * n:]
        me = _my_index()

        def src_of(t, s):
            return ins[t].at[s] if scatter else ins[t]

        def remote(t, s):
            return pltpu.make_async_remote_copy(
                src_ref=src_of(t, s), dst_ref=outs[t].at[me], send_sem=send_sems.at[t, s], recv_sem=recv_sems.at[t, me],
                device_id=(s // 4, (s // 2) % 2, s % 2), device_id_type=MESH)

        def arrival(t, s):
            return pltpu.make_async_remote_copy(
                src_ref=src_of(t, s), dst_ref=outs[t].at[s], send_sem=send_sems.at[t, s], recv_sem=recv_sems.at[t, s],
                device_id=(s // 4, (s // 2) % 2, s % 2), device_id_type=MESH)

        def local(t):
            return pltpu.make_async_copy(src_of(t, me), outs[t].at[me], local_sems.at[t])

        for t in range(n):
            local(t).start()
        for s in range(N_DEV):
            @pl.when(s != me)
            def _():
                for t in range(n):
                    remote(t, s).start()
        for s in range(N_DEV):
            @pl.when(s != me)
            def _():
                for t in range(n):
                    remote(t, s).wait_send()
                    arrival(t, s).wait_recv()
        for t in range(n):
            local(t).wait()

    any_spec = pl.BlockSpec(memory_space=pl.ANY)
    return pl.pallas_call(
        body, name=name, in_specs=[any_spec] * n, out_specs=[any_spec] * n,
        out_shape=[jax.ShapeDtypeStruct((N_DEV,) + tuple(sh), s.dtype) for sh, s in zip(shapes, srcs)],
        scratch_shapes=[pltpu.SemaphoreType.DMA((n, N_DEV)), pltpu.SemaphoreType.DMA((n, N_DEV)),
                        pltpu.SemaphoreType.DMA((n,))],
    )(*srcs)


def _adamw(name, parts, w, m, v, row_block):
    n_src, rows, cols = parts.shape
    rb = min(row_block, rows)

    def body(p_ref, w_ref, m_ref, v_ref, g_out, d_out, m_out, v_out):
        g = p_ref[0]
        for s in range(1, n_src):
            g = g + p_ref[s]
        m_new = ADAM_B1 * m_ref[...] + (1.0 - ADAM_B1) * g
        v_new = ADAM_B2 * v_ref[...] + (1.0 - ADAM_B2) * jnp.square(g)
        m_hat = m_new / (1.0 - ADAM_B1 ** ADAM_STEP)
        v_hat = v_new / (1.0 - ADAM_B2 ** ADAM_STEP)
        g_out[...] = g
        d_out[...] = -ADAM_LR * (m_hat / (jnp.sqrt(v_hat) + ADAM_EPS) + ADAM_WD * w_ref[...])
        m_out[...] = m_new
        v_out[...] = v_new

    spec = pl.BlockSpec((rb, cols), lambda i: (i, 0))
    return pl.pallas_call(
        body, name=name, grid=(rows // rb,),
        in_specs=[pl.BlockSpec((n_src, rb, cols), lambda i: (0, i, 0)), spec, spec, spec],
        out_specs=[spec] * 4, out_shape=[jax.ShapeDtypeStruct((rows, cols), F32)] * 4,
        compiler_params=pltpu.CompilerParams(dimension_semantics=("parallel",), vmem_limit_bytes=VMEM_LIMIT),
    )(parts, w, m, v)


def _sum_parts(name, parts):
    n_src, rows, cols = parts.shape

    def body(p_ref, o_ref):
        g = p_ref[0]
        for s in range(1, n_src):
            g = g + p_ref[s]
        o_ref[...] = g

    return pl.pallas_call(body, name=name, out_shape=jax.ShapeDtypeStruct((rows, cols), F32))(parts)


def _block_diag(blocks):
    g, a, b = blocks.shape
    eye = jnp.eye(g, dtype=blocks.dtype)
    return (eye[:, None, :, None] * blocks[:, :, None, :]).reshape(g * a, g * b)


def _diag_blocks(mat, g):
    a, b = mat.shape[0] // g, mat.shape[1] // g
    m4 = mat.reshape(g, a, g, b)
    return jnp.stack([m4[k, :, k, :] for k in range(g)], axis=0)


def _pack(pieces, width=1024):
    flat = jnp.concatenate([p.reshape(-1) for p in pieces])
    rows = -(-flat.shape[0] // (8 * width)) * 8
    return jnp.pad(flat, (0, rows * width - flat.shape[0])).reshape(rows, width)


def _unpack(packed, shapes):
    flat = packed.reshape(-1)
    out, o = [], 0
    for sh in shapes:
        size = 1
        for k in sh:
            size *= k
        out.append(flat[o:o + size].reshape(sh))
        o += size
    return out


def _local_grads(x2, p2, tgt, norm_mix_g, w_in_f, pool_w, pool_b, pool_scale, conv_w_f, conv_b, gate_a_w, gate_a_b,
                 gate_x_w, gate_x_b, lru_l, w_out_f, norm_mlp_g, w_up_blocks, w_down_f, norm_ple_g, w_gate_f, b_ple_gate,
                 w_proj_f, norm_final_g, tbs):
    wp_bd = _block_diag(pool_w[0]).astype(BF16)
    wg_bd = jnp.concatenate([_block_diag(gate_a_w[0]), _block_diag(gate_x_w[0])], axis=1).astype(BF16)
    pool_b2 = pool_b.reshape(1, -1)
    gate_b2 = jnp.concatenate([gate_a_b.reshape(1, -1), gate_x_b.reshape(1, -1)], axis=1)
    mixer_small = (norm_mix_g, w_in_f, wp_bd, pool_b2, pool_scale, conv_w_f, conv_b, wg_bd, gate_b2, lru_l, w_out_f)

    h1, z1, proj, hs, cat = _mix_fwd(x2, *mixer_small, tbs['mix_fwd'])
    h2, z2, up = _mlp_fwd(h1, norm_mlp_g, w_up_blocks, w_down_f, tbs['mlp_fwd'])
    dh2, ple_vec, dw_gate, dw_proj = _ple(h2, p2, tgt, norm_ple_g, w_gate_f, b_ple_gate, w_proj_f,
                                          norm_final_g.reshape(1, -1), tbs['ple'])
    dz2 = None
    dw_up_parts, dw_down_parts = [], []
    for part in range(MLP_BWD_SPLIT):
        res = _mlp_bwd_part(part, MLP_BWD_SPLIT, dh2, z2, up, w_up_blocks, w_down_f, dz2, h1, norm_mlp_g,
                            tbs['mlp_bwd'])
        dz2 = res[0]
        dw_up_parts.append(res[-2])
        dw_down_parts.append(res[-1])
    dh1, mlp_vec = res[0], res[1]
    dx, v512, v1024, dw_in, dw_out, dwp_bd, dwg_bd = _mix_bwd(dh1, x2, z1, proj, hs, cat, *mixer_small,
                                                              tbs['mix_bwd'])
    big_grads = [dw_in, dw_out, jnp.concatenate(dw_up_parts, axis=0), jnp.concatenate(dw_down_parts, axis=0), dw_gate,
                 dw_proj]
    small_grads = [
        v1024[1:2],
        _diag_blocks(dwp_bd, N_POOL_GROUPS)[None],
        v512[1:2].reshape(pool_b.shape),
        v512[0:1],
        v512[2:3],
        _diag_blocks(dwg_bd[:, :512], LRU_HEADS)[None],
        v1024[0:1, :512].reshape(gate_a_b.shape),
        _diag_blocks(dwg_bd[:, 512:], LRU_HEADS)[None],
        v1024[0:1, 512:].reshape(gate_x_b.shape),
        v512[3:4],
        mlp_vec[0:1],
        ple_vec[1:2],
        ple_vec[0:1],
        ple_vec[2:3].reshape(-1),
        v512[4:8][None],
        ple_vec[3:4, 0:1].reshape(1),
    ]
    return dx, big_grads, small_grads


def kernel(x, p, norm_mix_g, w_in, pool_w, pool_b, pool_scale, conv_w, conv_b, gate_a_w, gate_a_b, gate_x_w, gate_x_b, lru_L, w_out, norm_mlp_g, w_up, w_down, norm_ple_g, w_ple_gate, b_ple_gate, w_ple_proj, norm_final_g, loss_target, m_norm_mix_g, m_w_in, m_pool_w, m_pool_b, m_pool_scale, m_conv_w, m_conv_b, m_gate_a_w, m_gate_a_b, m_gate_x_w, m_gate_x_b, m_lru_L, m_w_out, m_norm_mlp_g, m_w_up, m_w_down, m_norm_ple_g, m_w_ple_gate, m_b_ple_gate, m_w_ple_proj, m_norm_final_g, v_norm_mix_g, v_w_in, v_pool_w, v_pool_b, v_pool_scale, v_conv_w, v_conv_b, v_gate_a_w, v_gate_a_b, v_gate_x_w, v_gate_x_b, v_lru_L, v_w_out, v_norm_mlp_g, v_w_up, v_w_down, v_norm_ple_g, v_w_ple_gate, v_b_ple_gate, v_w_ple_proj, v_norm_final_g):
    t_len, d = x.shape[1], x.shape[2]
    tbs = {k: min(v, t_len) for k, v in TIME_BLOCKS.items()}
    me = _my_index()

    big = [w_in[0], w_out[0], w_up[0], w_down[0], w_ple_gate[0], w_ple_proj[0]]
    gathered = _exchange("gather_weights", [a.astype(BF16) for a in big] + [conv_w[0]], scatter=False)
    win_g, wout_g, wup_g, wdn_g, wgate_g, wproj_g, convw_g = gathered
    w_in_f = jnp.transpose(win_g, (1, 0, 2)).reshape(d, -1)
    w_proj_f = jnp.transpose(wproj_g, (1, 0, 2)).reshape(wproj_g.shape[1], -1)
    conv_w_f = jnp.transpose(convw_g, (1, 0, 2)).reshape(convw_g.shape[1], -1)

    dx, big_grads, small_grads = _local_grads(
        x[0], p[0, 0], loss_target[0], norm_mix_g, w_in_f, pool_w, pool_b, pool_scale, conv_w_f, conv_b, gate_a_w,
        gate_a_b, gate_x_w, gate_x_b, lru_L, wout_g.reshape(-1, d), norm_mlp_g, wup_g, wdn_g.reshape(-1, d), norm_ple_g,
        wgate_g.reshape(-1, d), b_ple_gate, w_proj_f, norm_final_g, tbs)

    dw_in, dw_out, dw_up, dw_down, dw_gate, dw_proj = big_grads
    n_in = w_in.shape[2]
    n_proj = w_ple_proj.shape[2]
    big_parts = [
        jnp.transpose(dw_in.reshape(d, N_DEV, n_in), (1, 0, 2)),
        dw_out.reshape(N_DEV, -1, d),
        dw_up,
        dw_down.reshape(N_DEV, -1, d),
        dw_gate.reshape(N_DEV, -1, d),
        jnp.transpose(dw_proj.reshape(-1, N_DEV, n_proj), (1, 0, 2)),
    ]
    small_w =[norm_mix_g, pool_w, pool_b, pool_scale, conv_b, gate_a_w, gate_a_b, gate_x_w, gate_x_b, lru_L,
               norm_mlp_g, norm_ple_g, b_ple_gate, norm_final_g]
    small_m = [m_norm_mix_g, m_pool_w, m_pool_b, m_pool_scale, m_conv_b, m_gate_a_w, m_gate_a_b, m_gate_x_w, m_gate_x_b,
               m_lru_L, m_norm_mlp_g, m_norm_ple_g, m_b_ple_gate, m_norm_final_g]
    small_v = [v_norm_mix_g, v_pool_w, v_pool_b, v_pool_scale, v_conv_b, v_gate_a_w, v_gate_a_b, v_gate_x_w, v_gate_x_b,
               v_lru_L, v_norm_mlp_g, v_norm_ple_g, v_b_ple_gate, v_norm_final_g]
    small_shapes = [a.shape for a in small_w]
    conv_full_shape = (1,) + conv_w_f.shape
    small_packed = _pack(small_grads)
    small_all = _exchange("gather_small_grads", [small_packed], scatter=False)[0]
    received = _exchange("scatter_grads", big_parts, scatter=True)

    shard_w = [w_in[0], w_out[0], w_up[0], w_down[0], w_ple_gate[0], w_ple_proj[0]]
    shard_m = [m_w_in[0], m_w_out[0], m_w_up[0], m_w_down[0], m_w_ple_gate[0], m_w_ple_proj[0]]
    shard_v = [v_w_in[0], v_w_out[0], v_w_up[0], v_w_down[0], v_w_ple_gate[0], v_w_ple_proj[0]]
    names = ["w_in", "w_out", "w_up", "w_down", "w_ple_gate", "w_ple_proj"]
    big_res = {}
    for nm, parts, w_s, m_s, v_s in zip(names, received, shard_w, shard_m, shard_v):
        big_res[nm] = [r[None] for r in _adamw("adamw_" + nm, parts, w_s, m_s, v_s, 128)]

    small_sum = _sum_parts("sum_small_grads", small_all)
    summed = _unpack(small_sum, small_shapes + [conv_full_shape, (1,)])
    loss = summed[-1][0]
    conv_g = lax.dynamic_slice_in_dim(summed[-2], me * conv_w.shape[2], conv_w.shape[2], axis=2)
    sg = summed[:-2] + [conv_g]
    sw, sm, sv = small_w + [conv_w], small_m + [m_conv_w], small_v + [v_conv_w]
    shapes2 = small_shapes + [conv_w.shape]
    res = _adamw("adamw_small", _pack(sg)[None], _pack(sw), _pack(sm), _pack(sv), 1024)
    sres = [_unpack(r, shapes2) for r in res]
    small_names = ["norm_mix_g", "pool_w", "pool_b", "pool_scale", "conv_b", "gate_a_w", "gate_a_b", "gate_x_w",
                   "gate_x_b", "lru_L", "norm_mlp_g", "norm_ple_g", "b_ple_gate", "norm_final_g", "conv_w"]
    order = ["norm_mix_g", "w_in", "pool_w", "pool_b", "pool_scale", "conv_w", "conv_b", "gate_a_w", "gate_a_b",
             "gate_x_w", "gate_x_b", "lru_L", "w_out", "norm_mlp_g", "w_up", "w_down", "norm_ple_g", "w_ple_gate",
             "b_ple_gate", "w_ple_proj", "norm_final_g"]
    outs = [loss, dx[None]]
    for kind in range(4):
        for nm in order:
            if nm in big_res:
                outs.append(big_res[nm][kind])
            else:
                outs.append(sres[kind][small_names.index(nm)])
    return tuple(outs)
```

```python
import functools

import jax
import jax.numpy as jnp
from jax import lax
from jax.experimental import pallas as pl
from jax.experimental.pallas import tpu as pltpu

F32 = jnp.float32
BF16 = jnp.bfloat16
MESH = pl.DeviceIdType.MESH

N_DEV = 8
RMS_EPS = 1e-6
LRU_C = 8.0
POOL_WINDOWS = (2, 4, 8, 16)
N_POOL_GROUPS = 4
LRU_HEADS = 8
HALO = 16
SUB = 8
GELU_C0 = 0.7978845608028654
GELU_C1 = 0.044715

ADAM_LR = 0.001
ADAM_B1 = 0.9
ADAM_B2 = 0.999
ADAM_EPS = 1e-08
ADAM_WD = 0.01
ADAM_STEP = 10

VMEM_LIMIT = 60 * 1024 * 1024
TIME_BLOCKS = dict(mix_fwd=512, mlp_fwd=512, ple=512, mlp_bwd=512, mix_bwd=256)
MLP_BWD_SPLIT = 2


def _params(n_arbitrary=1):
    return pltpu.CompilerParams(dimension_semantics=("arbitrary",) * n_arbitrary, vmem_limit_bytes=VMEM_LIMIT)


def _dot(a, b):
    return jnp.dot(a, b, preferred_element_type=F32)


def _dot_nt(a, b):
    return lax.dot_general(a, b, (((1,), (1,)), ((), ())), preferred_element_type=F32)


def _dot_tn(a, b):
    return lax.dot_general(a, b, (((0,), (0,)), ((), ())), preferred_element_type=F32)


def _rms_fwd(x, g):
    r = lax.rsqrt(jnp.mean(x * x, axis=-1, keepdims=True) + RMS_EPS)
    xh = x * r
    return xh * g, xh, r


def _rms_bwd(xh, r, g, dz):
    dxh = dz * g
    return r * (dxh - xh * jnp.mean(dxh * xh, axis=-1, keepdims=True))


def _colsum(a):
    return jnp.sum(a, axis=0, keepdims=True)


def _sigmoid(a):
    return 1.0 / (1.0 + jnp.exp(-a))


def _gelu_parts(u):
    u2 = u * u
    th = jnp.tanh(GELU_C0 * (u + GELU_C1 * u * u2))
    gel = 0.5 * u * (1.0 + th)
    dgel = 0.5 * (1.0 + th) + 0.5 * u * (1.0 - th * th) * (GELU_C0 * (1.0 + 3.0 * GELU_C1 * u2))
    return gel, dgel


def _my_index():
    return 4 * lax.axis_index("x") + 2 * lax.axis_index("y") + lax.axis_index("c")


def _all_to_all(srcs_of, dsts, send_sems, recv_sems, local_sems):
    n = len(dsts)
    me = _my_index()

    def remote(t, s):
        return pltpu.make_async_remote_copy(
            src_ref=srcs_of[t](s), dst_ref=dsts[t].at[me], send_sem=send_sems.at[t, s], recv_sem=recv_sems.at[t, me],
            device_id=(s // 4, (s // 2) % 2, s % 2), device_id_type=MESH)

    def arrival(t, s):
        return pltpu.make_async_remote_copy(
            src_ref=srcs_of[t](s), dst_ref=dsts[t].at[s], send_sem=send_sems.at[t, s], recv_sem=recv_sems.at[t, s],
            device_id=(s // 4, (s // 2) % 2, s % 2), device_id_type=MESH)

    def local(t, s):
        return pltpu.make_async_copy(srcs_of[t](s), dsts[t].at[s], local_sems.at[t])

    def start():
        for s in range(N_DEV):
            @pl.when(s == me)
            def _():
                for t in range(n):
                    local(t, s).start()

            @pl.when(s != me)
            def _():
                for t in range(n):
                    remote(t, s).start()

    def wait():
        for s in range(N_DEV):
            @pl.when(s == me)
            def _():
                for t in range(n):
                    local(t, s).wait()

            @pl.when(s != me)
            def _():
                for t in range(n):
                    remote(t, s).wait_send()
                    arrival(t, s).wait_recv()

    return start, wait


def _exchange_scratch(n):
    return [pltpu.SemaphoreType.DMA((n, N_DEV)), pltpu.SemaphoreType.DMA((n, N_DEV)), pltpu.SemaphoreType.DMA((n,))]


def _const_spec(shape):
    nd = len(shape)
    return pl.BlockSpec(shape, lambda i: (0,) * nd, pipeline_mode=pl.Buffered(1))


def _pool_windows(up_ext, n, forward):
    sh = (lambda k: k) if forward else (lambda k: n - k)
    s2 = up_ext + pltpu.roll(up_ext, sh(1), 0)
    t4 = s2[:, 128:]
    s4 = t4 + pltpu.roll(t4, sh(2), 0)
    t8 = s4[:, 128:]
    s8 = t8 + pltpu.roll(t8, sh(4), 0)
    t16 = s8[:, 128:]
    s16 = t16 + pltpu.roll(t16, sh(8), 0)
    return jnp.concatenate([s2[:, :128], s4[:, :128], s8[:, :128], s16], axis=1)


def _inv_count(t0, tb):
    t = (t0 + lax.broadcasted_iota(jnp.int32, (tb, 1), 0) + 1).astype(F32)
    cols = [jnp.broadcast_to(1.0 / jnp.minimum(t, float(w)), (tb, 128)) for w in POOL_WINDOWS]
    return jnp.concatenate(cols, axis=1)


def _lru_gates(xb, wg, gb, c_l, first_row):
    gp = _dot(xb.astype(BF16), wg) + gb
    r = _sigmoid(gp[:, :512])
    ig = _sigmoid(gp[:, 512:])
    la = c_l * r
    a = jnp.exp(la)
    a2 = a * a
    m2 = -jnp.tanh(la) * (a2 + 1.0)
    mult = jnp.where(first_row, 1.0, jnp.sqrt(m2))
    return r, ig, a, a2, m2, mult


def _log_sigmoid(v):
    return -(jnp.maximum(-v, 0.0) + jnp.log1p(jnp.exp(-jnp.abs(v))))


def _conv_fwd(ul_ext, cw, cb):
    return (cb + cw[3:4, :] * ul_ext + cw[2:3, :] * pltpu.roll(ul_ext, 1, 0)
            + cw[1:2, :] * pltpu.roll(ul_ext, 2, 0) + cw[0:1, :] * pltpu.roll(ul_ext, 3, 0))


def _mix_fwd(x, g_mix, w_in, wp_bd, pool_b, pool_scale, conv_w, conv_b, wg_bd, gate_b, lru_l, w_out, gather_srcs, tb):
    t_len, d = x.shape
    nb = t_len // tb
    n_g = len(gather_srcs)

    def body(*refs):
        x_ref, g_ref, win_ref, wp_ref, pb_ref, ps_ref, cw_ref, cb_ref, wg_ref, gb_ref, l_ref, wout_ref = refs[:12]
        gsrc = refs[12:12 + n_g]
        h1_ref, z1_ref, proj_ref, hs_ref, cat_ref = refs[12 + n_g:17 + n_g]
        gdst = refs[17 + n_g:17 + 2 * n_g]
        ext_ref, a_ref, b_ref, hc_ref, send_sems, recv_sems, local_sems = refs[17 + 2 * n_g:]
        i = pl.program_id(0)
        start_gather, wait_gather = _all_to_all([lambda s, r=r: r for r in gsrc], gdst, send_sems, recv_sems, local_sems)

        @pl.when(i == 0)
        def _():
            start_gather()
            ext_ref[0:HALO, :] = jnp.zeros((HALO, 1024), F32)
            hc_ref[...] = jnp.zeros_like(hc_ref)

        xv = x_ref[...]
        z, _, _ = _rms_fwd(xv, g_ref[...])
        zb = z.astype(BF16)
        z1_ref[...] = zb
        proj = _dot(zb, win_ref[...])
        proj_ref[...] = proj
        ext_ref[HALO:, :] = proj[:, 0:1024]
        ug = proj[:, 1024:1536]
        n = tb + HALO
        up_ext = ext_ref[:, 0:512]
        win = _pool_windows(up_ext, n, True)[HALO:]
        dpool = win * _inv_count(i * tb, tb) - proj[:, 0:512]
        q = _dot(dpool.astype(BF16), wp_ref[...]) + pb_ref[...]
        y_pool = q * ps_ref[...]
        xb = _conv_fwd(ext_ref[:, 512:1024], cw_ref[...], cb_ref[...])[HALO:]
        first_row = (i * tb + lax.broadcasted_iota(jnp.int32, (tb, 1), 0)) == 0
        c_l = LRU_C * _log_sigmoid(l_ref[...])
        _, ig, a, _, _, mult = _lru_gates(xb, wg_ref[...], gb_ref[...], c_l, first_row)
        a_ref[...] = a
        b_ref[...] = mult * (ig * xb)
        row = lax.broadcasted_iota(jnp.int32, (SUB, 512), 0)

        def group(j, hprev):
            o = pl.multiple_of(j * SUB, SUB)
            a8 = a_ref[pl.ds(o, SUB), :]
            b8 = b_ref[pl.ds(o, SUB), :]
            for sh in (1, 2, 4):
                ash = jnp.where(row >= sh, pltpu.roll(a8, sh, 0), 1.0)
                bsh = jnp.where(row >= sh, pltpu.roll(b8, sh, 0), 0.0)
                b8 = a8 * bsh + b8
                a8 = a8 * ash
            h8 = a8 * hprev + b8
            hs_ref[pl.ds(o, SUB), :] = h8
            return jnp.broadcast_to(h8[SUB - 1:SUB, :], (SUB, 512))

        hc_ref[...] = lax.fori_loop(0, tb // SUB, group, hc_ref[...])
        gel, _ = _gelu_parts(ug)
        y_lru = hs_ref[...] * gel
        catb = jnp.concatenate([y_pool, y_lru], axis=1).astype(BF16)
        cat_ref[...] = catb
        h1_ref[...] = xv + _dot(catb, wout_ref[...])
        ext_ref[0:HALO, :] = ext_ref[tb:tb + HALO, :]

        @pl.when(i == nb - 1)
        def _():
            wait_gather()

    row_spec = lambda w: pl.BlockSpec((tb, w), lambda i: (i, 0))
    any_spec = pl.BlockSpec(memory_space=pl.ANY)
    smalls = [g_mix, w_in, wp_bd, pool_b, pool_scale, conv_w, conv_b, wg_bd, gate_b, lru_l, w_out]
    return pl.pallas_call(
        body, name="mix_fwd", grid=(nb,),
        in_specs=[row_spec(d)] + [_const_spec(s.shape) for s in smalls] + [any_spec] * n_g,
        out_specs=[row_spec(d), row_spec(d), row_spec(1536), row_spec(512), row_spec(1024)] + [any_spec] * n_g,
        out_shape=[jax.ShapeDtypeStruct((t_len, d), F32), jax.ShapeDtypeStruct((t_len, d), BF16),
                   jax.ShapeDtypeStruct((t_len, 1536), F32), jax.ShapeDtypeStruct((t_len, 512), F32),
                   jax.ShapeDtypeStruct((t_len, 1024), BF16)]
        + [jax.ShapeDtypeStruct((N_DEV,) + s.shape, s.dtype) for s in gather_srcs],
        scratch_shapes=[pltpu.VMEM((tb + HALO, 1024), F32), pltpu.VMEM((tb, 512), F32), pltpu.VMEM((tb, 512), F32),
                        pltpu.VMEM((SUB, 512), F32)] + _exchange_scratch(n_g),
        compiler_params=_params(),
    )(x, *smalls, *gather_srcs)


def _mlp_fwd(h1, g_mlp, w_up, w_down, tb):
    t_len, d = h1.shape
    nb = t_len // tb
    n_chunk, _, fc = w_up.shape

    def body(h1_ref, g_ref, wup_ref, wdn_ref, h2_ref, z2_ref, up_ref):
        xv = h1_ref[...]
        z, _, _ = _rms_fwd(xv, g_ref[...])
        zb = z.astype(BF16)
        z2_ref[...] = zb
        acc = xv
        for c in range(n_chunk):
            u = _dot(zb, wup_ref[c])
            up_ref[:, c * fc:(c + 1) * fc] = u.astype(BF16)
            act = jnp.square(jnp.maximum(u, 0.0)).astype(BF16)
            acc = acc + _dot(act, wdn_ref[c * fc:(c + 1) * fc, :])
        h2_ref[...] = acc

    row_spec = lambda w: pl.BlockSpec((tb, w), lambda i: (i, 0))
    return pl.pallas_call(
        body, name="mlp_fwd", grid=(nb,),
        in_specs=[row_spec(d), _const_spec(g_mlp.shape), _const_spec(w_up.shape), _const_spec(w_down.shape)],
        out_specs=[row_spec(d), row_spec(d), row_spec(n_chunk * fc)],
        out_shape=[jax.ShapeDtypeStruct((t_len, d), F32), jax.ShapeDtypeStruct((t_len, d), BF16),
                   jax.ShapeDtypeStruct((t_len, n_chunk * fc), BF16)],
        compiler_params=_params(),
    )(h1, g_mlp, w_up, w_down)


def _ple(h2, p, target, g_ple, w_gate, b_gate, w_proj, g_final, tb):
    t_len, d = h2.shape
    nb = t_len // tb
    pd = p.shape[1]

    def body(h2_ref, p_ref, tgt_ref, g_ref, wg_ref, bg_ref, wp_ref, gf_ref,
             dh2_ref, vec_ref, dwg_out, dwp_out, dwg_acc, dwp_acc, dwg_stage, dwp_stage):
        i = pl.program_id(0)

        @pl.when(i == 0)
        def _():
            vec_ref[...] = jnp.zeros_like(vec_ref)
            dwg_acc[...] = jnp.zeros_like(dwg_acc)
            dwp_acc[...] = jnp.zeros_like(dwp_acc)

        h2 = h2_ref[...]
        g2 = g_ref[...]
        z3, xh2, r2 = _rms_fwd(h2, g2)
        z3b = z3.astype(BF16)
        gate = _sigmoid(_dot(z3b, wg_ref[...]) + bg_ref[...])
        pb = p_ref[...].astype(BF16)
        pp = _dot(pb, wp_ref[...])
        h3 = h2 + gate * pp
        gf = gf_ref[...]
        y, xh3, r3 = _rms_fwd(h3, gf)
        err = y - tgt_ref[...]
        loss_rows = jnp.mean(err * err, axis=-1, keepdims=True)
        dy = err * (1.0 / d)
        dh3 = _rms_bwd(xh3, r3, gf, dy)
        dgl = (dh3 * pp) * (gate * (1.0 - gate))
        dpp = dh3 * gate
        dglb = dgl.astype(BF16)
        dwg_acc[...] += _dot_tn(z3b, dglb)
        dwp_acc[...] += _dot_tn(pb, dpp.astype(BF16))
        dz3 = _dot_nt(dglb, wg_ref[...])
        dh2_ref[...] = dh3 + _rms_bwd(xh2, r2, g2, dz3)
        vec_ref[0:1, :] += _colsum(dgl)
        vec_ref[1:2, :] += _colsum(dz3 * xh2)
        vec_ref[2:3, :] += _colsum(dy * xh3)
        vec_ref[3:4, :] += 0.5 * jnp.sum(loss_rows)

        @pl.when(i == nb - 1)
        def _():
            dwg_stage[...] = dwg_acc[...].astype(BF16)
            dwp_stage[...] = dwp_acc[...].astype(BF16)
            pltpu.sync_copy(dwg_stage, dwg_out)
            pltpu.sync_copy(dwp_stage, dwp_out)

    row_spec = lambda w: pl.BlockSpec((tb, w), lambda i: (i, 0))
    any_spec = pl.BlockSpec(memory_space=pl.ANY)
    smalls = [g_ple, w_gate, b_gate, w_proj, g_final]
    return pl.pallas_call(
        body, name="ple_fwd_bwd", grid=(nb,),
        in_specs=[row_spec(d), row_spec(pd), row_spec(d)] + [_const_spec(s.shape) for s in smalls],
        out_specs=[row_spec(d), pl.BlockSpec((8, d), lambda i: (0, 0)), any_spec, any_spec],
        out_shape=[jax.ShapeDtypeStruct((t_len, d), F32), jax.ShapeDtypeStruct((8, d), F32),
                   jax.ShapeDtypeStruct(w_gate.shape, BF16), jax.ShapeDtypeStruct(w_proj.shape, BF16)],
        scratch_shapes=[pltpu.VMEM(w_gate.shape, F32), pltpu.VMEM(w_proj.shape, F32), pltpu.VMEM(w_gate.shape, BF16),
                        pltpu.VMEM(w_proj.shape, BF16)],
        compiler_params=_params(),
    )(h2, p, target, *smalls)


def _mlp_bwd_part(part, n_part, dh2, z2, up, w_up, w_down, dz2_prev, h1, g_mlp, tb):
    t_len, d = dh2.shape
    nb = t_len // tb
    n_chunk_all, _, fc = w_up.shape
    n_chunk = n_chunk_all // n_part
    first, last = part == 0, part == n_part - 1

    def body(*refs):
        refs = list(refs)
        dh2_ref, z2_ref, up_ref, wup_ref, wdn_ref = refs[:5]
        del refs[:5]
        dzp_ref = None if first else refs.pop(0)
        h1_ref, g_ref = (refs.pop(0), refs.pop(0)) if last else (None, None)
        out_ref = refs.pop(0)
        vec_ref = refs.pop(0) if last else None
        dwup_out, dwdn_out, dwup_acc, dwdn_acc, up_stage, dn_stage = refs
        i = pl.program_id(0)

        @pl.when(i == 0)
        def _():
            dwup_acc[...] = jnp.zeros_like(dwup_acc)
            dwdn_acc[...] = jnp.zeros_like(dwdn_acc)
            if last:
                vec_ref[...] = jnp.zeros_like(vec_ref)

        dh2 = dh2_ref[...]
        dh2b = dh2.astype(BF16)
        z2b = z2_ref[...]
        dz2 = jnp.zeros((tb, d), F32) if first else dzp_ref[...]
        for c in range(n_chunk):
            u = up_ref[:, c * fc:(c + 1) * fc].astype(F32)
            ur = jnp.maximum(u, 0.0)
            dact = _dot_nt(dh2b, wdn_ref[c * fc:(c + 1) * fc, :])
            dupb = (dact * (2.0 * ur)).astype(BF16)
            dwdn_acc[c * fc:(c + 1) * fc, :] += _dot_tn((ur * ur).astype(BF16), dh2b)
            dwup_acc[c] += _dot_tn(z2b, dupb)
            dz2 = dz2 + _dot_nt(dupb, wup_ref[c])
        if last:
            g = g_ref[...]
            _, xh, r = _rms_fwd(h1_ref[...], g)
            out_ref[...] = dh2 + _rms_bwd(xh, r, g, dz2)
            vec_ref[0:1, :] += _colsum(dz2 * xh)
        else:
            out_ref[...] = dz2

        @pl.when(i == nb - 1)
        def _():
            for c in range(n_chunk):
                up_stage[...] = dwup_acc[c].astype(BF16)
                dn_stage[...] = dwdn_acc[c * fc:(c + 1) * fc, :].astype(BF16)
                pltpu.sync_copy(up_stage, dwup_out.at[c])
                pltpu.sync_copy(dn_stage, dwdn_out.at[c])

    row_spec = lambda w: pl.BlockSpec((tb, w), lambda i: (i, 0))
    any_spec = pl.BlockSpec(memory_space=pl.ANY)
    args = [dh2, z2, up, w_up, w_down]
    in_specs = [row_spec(d), row_spec(d), pl.BlockSpec((tb, n_chunk * fc), lambda i: (i, part)),
                pl.BlockSpec((n_chunk, d, fc), lambda i: (part, 0, 0), pipeline_mode=pl.Buffered(1)),
                pl.BlockSpec((n_chunk * fc, d), lambda i: (part, 0), pipeline_mode=pl.Buffered(1))]
    if not first:
        args.append(dz2_prev)
        in_specs.append(row_spec(d))
    if last:
        args += [h1, g_mlp]
        in_specs += [row_spec(d), _const_spec(g_mlp.shape)]
    out_specs = [row_spec(d)]
    out_shape = [jax.ShapeDtypeStruct((t_len, d), F32)]
    if last:
        out_specs.append(pl.BlockSpec((8, d), lambda i: (0, 0)))
        out_shape.append(jax.ShapeDtypeStruct((8, d), F32))
    out_specs += [any_spec, any_spec]
    out_shape += [jax.ShapeDtypeStruct((n_chunk, d, fc), BF16), jax.ShapeDtypeStruct((n_chunk, fc, d), BF16)]
    return pl.pallas_call(
        body, name=f"mlp_bwd_{part}", grid=(nb,), in_specs=in_specs, out_specs=out_specs, out_shape=out_shape,
        scratch_shapes=[pltpu.VMEM((n_chunk, d, fc), F32), pltpu.VMEM((n_chunk * fc, d), F32),
                        pltpu.VMEM((d, fc), BF16), pltpu.VMEM((fc, d), BF16)],
        compiler_params=_params(),
    )(*args)


def _mix_bwd(dh1, x, z1, proj, hs, cat, g_mix, w_in, wp_bd, pool_b, pool_scale, conv_w, conv_b, wg_bd, gate_b, lru_l,
             w_out, scatter_parts, tb):
    t_len, d = x.shape
    nb = t_len // tb
    n_s = len(scatter_parts)
    flat_parts = [a for parts in scatter_parts for a in parts]

    def body(*refs):
        refs = list(refs)
        (dh1_ref, x_ref, z1_ref, proj_ref, projh_ref, hs_ref, hsh_ref, cat_ref,
         g_ref, win_ref, wp_ref, pb_ref, ps_ref, cw_ref, cb_ref, wg_ref, gb_ref, l_ref, wout_ref) = refs[:19]
        del refs[:19]
        part_refs = []
        for parts in scatter_parts:
            part_refs.append(refs[:len(parts)])
            del refs[:len(parts)]
        dx_ref, v512_ref, v1024_ref, dwin_out, dwout_out, dwp_out, dwg_out = refs[:7]
        recv = refs[7:7 + n_s]
        (dwin_acc, dwout_acc, dwp_acc, dwg_acc, ext_ref, a_ref, b_ref, gs_ref,
         ehead_ref, dxbhead_ref, ahead_ref, gc_ref, send_sems, recv_sems, local_sems) = refs[7 + n_s:]
        i = pl.program_id(0)
        tbk = nb - 1 - i

        def block_of(prefs):
            per = N_DEV // len(prefs)
            return lambda s: prefs[s // per].at[s % per]

        start_scatter, wait_scatter = _all_to_all([block_of(pr) for pr in part_refs], recv, send_sems, recv_sems,
                                                  local_sems)

        @pl.when(i == 0)
        def _():
            start_scatter()
            for ref in (v512_ref, v1024_ref, dwin_acc, dwout_acc, dwp_acc, dwg_acc, ehead_ref, dxbhead_ref, ahead_ref,
                        gc_ref):
                ref[...] = jnp.zeros_like(ref)

        dh1 = dh1_ref[...]
        dh1b = dh1.astype(BF16)
        dcat = _dot_nt(dh1b, wout_ref[...])
        dwout_acc[...] += _dot_tn(cat_ref[...], dh1b)

        proj = proj_ref[...]
        has_prev = (tbk > 0).astype(F32)
        ext_ref[0:HALO, :] = projh_ref[:, 0:1024] * has_prev
        ext_ref[HALO:, :] = proj[:, 0:1024]
        ug = proj[:, 1024:1536]
        n = tb + HALO
        inv = _inv_count(tbk * tb, tb)

        up_ext = ext_ref[:, 0:512]
        win = _pool_windows(up_ext, n, True)[HALO:]
        dpool = win * inv - proj[:, 0:512]
        dpoolb = dpool.astype(BF16)
        q = _dot(dpoolb, wp_ref[...]) + pb_ref[...]
        dyp = dcat[:, 0:512]
        dq = dyp * ps_ref[...]
        dqb = dq.astype(BF16)
        v512_ref[0:1, :] += _colsum(dyp * q)
        v512_ref[1:2, :] += _colsum(dq)
        dwp_acc[...] += _dot_tn(dpoolb, dqb)
        dd = _dot_nt(dqb, wp_ref[...])
        e = dd * inv
        e_ext = jnp.concatenate([e, ehead_ref[...]], axis=0)
        du_pool = _pool_windows(e_ext, n, False)[0:tb] - dd
        ehead_ref[...] = e[0:HALO]

        gel, dgel = _gelu_parts(ug)
        hsv = hs_ref[...]
        dcl = dcat[:, 512:1024]
        dhs = dcl * gel
        dug = dcl * hsv * dgel
        ul_ext = ext_ref[:, 512:1024]
        cw = cw_ref[...]
        xb = _conv_fwd(ul_ext, cw, cb_ref[...])[HALO:]
        first_row = (tbk * tb + lax.broadcasted_iota(jnp.int32, (tb, 1), 0)) == 0
        c_l = LRU_C * _log_sigmoid(l_ref[...])
        r, ig, a, a2, m2, mult = _lru_gates(xb, wg_ref[...], gb_ref[...], c_l, first_row)
        a_ext = jnp.concatenate([a, ahead_ref[...]], axis=0)
        a_ref[...] = pltpu.roll(a_ext, tb + SUB - 1, 0)[0:tb]
        b_ref[...] = dhs
        ahead_ref[...] = a[0:SUB]
        row = lax.broadcasted_iota(jnp.int32, (SUB, 512), 0)

        def group(jj, gnext):
            o = pl.multiple_of((tb // SUB - 1 - jj) * SUB, SUB)
            a8 = a_ref[pl.ds(o, SUB), :]
            b8 = b_ref[pl.ds(o, SUB), :]
            for sh in (1, 2, 4):
                ash = jnp.where(row < SUB - sh, pltpu.roll(a8, SUB - sh, 0), 1.0)
                bsh = jnp.where(row < SUB - sh, pltpu.roll(b8, SUB - sh, 0), 0.0)
                b8 = a8 * bsh + b8
                a8 = a8 * ash
            g8 = a8 * gnext + b8
            gs_ref[pl.ds(o, SUB), :] = g8
            return jnp.broadcast_to(g8[0:1, :], (SUB, 512))

        gc_ref[...] = lax.fori_loop(0, tb // SUB, group, gc_ref[...])
        gsum = gs_ref[...]
        hs_ext = jnp.concatenate([hsh_ref[...] * has_prev, hsv], axis=0)
        hprev = pltpu.roll(hs_ext, 1, 0)[SUB:]
        da = gsum * hprev
        dmult = jnp.where(first_row, 0.0, gsum * (ig * xb))
        di = gsum * mult * xb
        dxb = gsum * mult * ig
        dla = da * a - dmult * a2 * lax.rsqrt(m2)
        dr = dla * c_l
        v512_ref[3:4, :] += _colsum(dla * r)
        dgp = jnp.concatenate([dr * r * (1.0 - r), di * ig * (1.0 - ig)], axis=1)
        dgpb = dgp.astype(BF16)
        v1024_ref[0:1, :] += _colsum(dgp)
        dwg_acc[...] += _dot_tn(xb.astype(BF16), dgpb)
        dxb = dxb + _dot_nt(dgpb, wg_ref[...])
        n8 = tb + SUB
        dxb_ext = jnp.concatenate([dxb, dxbhead_ref[...]], axis=0)
        du_lru = (cw[3:4, :] * dxb + cw[2:3, :] * pltpu.roll(dxb_ext, n8 - 1, 0)[0:tb]
                  + cw[1:2, :] * pltpu.roll(dxb_ext, n8 - 2, 0)[0:tb] + cw[0:1, :] * pltpu.roll(dxb_ext, n8 - 3, 0)[0:tb])
        dxbhead_ref[...] = dxb[0:SUB]
        v512_ref[2:3, :] += _colsum(dxb)
        for j in range(4):
            shifted = ul_ext if j == 0 else pltpu.roll(ul_ext, j, 0)
            v512_ref[4 + (3 - j):5 + (3 - j), :] += _colsum(dxb * shifted[HALO:])

        dprojb = jnp.concatenate([du_pool, du_lru, dug], axis=1).astype(BF16)
        dwin_acc[...] += _dot_tn(z1_ref[...], dprojb)
        dz1 = _dot_nt(dprojb, win_ref[...])
        g = g_ref[...]
        _, xh, rr = _rms_fwd(x_ref[...], g)
        dx_ref[...] = dh1 + _rms_bwd(xh, rr, g, dz1)
        v1024_ref[1:2, :] += _colsum(dz1 * xh)

        @pl.when(i == nb - 1)
        def _():
            v512_ref[3:4, :] = v512_ref[3:4, :] * (LRU_C * _sigmoid(-l_ref[...]))
            pltpu.sync_copy(dwin_acc, dwin_out)
            pltpu.sync_copy(dwout_acc, dwout_out)
            pltpu.sync_copy(dwp_acc, dwp_out)
            pltpu.sync_copy(dwg_acc, dwg_out)
            wait_scatter()

    rev = lambda w: pl.BlockSpec((tb, w), lambda i: (nb - 1 - i, 0))
    halo = lambda rows, w: pl.BlockSpec((rows, w), lambda i: (jnp.maximum((nb - 1 - i) * (tb // rows) - 1, 0), 0))
    any_spec = pl.BlockSpec(memory_space=pl.ANY)
    smalls = [g_mix, w_in, wp_bd, pool_b, pool_scale, conv_w, conv_b, wg_bd, gate_b, lru_l, w_out]
    return pl.pallas_call(
        body, name="mix_bwd", grid=(nb,),
        in_specs=[rev(d), rev(d), rev(d), rev(1536), halo(HALO, 1536), rev(512), halo(SUB, 512), rev(1024)]
        + [_const_spec(s.shape) for s in smalls] + [any_spec] * len(flat_parts),
        out_specs=[rev(d), pl.BlockSpec((8, 512), lambda i: (0, 0)), pl.BlockSpec((8, 1024), lambda i: (0, 0))]
        + [any_spec] * (4 + n_s),
        out_shape=[jax.ShapeDtypeStruct((t_len, d), F32), jax.ShapeDtypeStruct((8, 512), F32),
                   jax.ShapeDtypeStruct((8, 1024), F32), jax.ShapeDtypeStruct(w_in.shape, F32),
                   jax.ShapeDtypeStruct(w_out.shape, F32), jax.ShapeDtypeStruct(wp_bd.shape, F32),
                   jax.ShapeDtypeStruct(wg_bd.shape, F32)]
        + [jax.ShapeDtypeStruct((N_DEV,) + parts[0].shape[1:], parts[0].dtype) for parts in scatter_parts],
        scratch_shapes=[pltpu.VMEM(w_in.shape, F32), pltpu.VMEM(w_out.shape, F32), pltpu.VMEM(wp_bd.shape, F32),
                        pltpu.VMEM(wg_bd.shape, F32), pltpu.VMEM((tb + HALO, 1024), F32), pltpu.VMEM((tb, 512), F32),
                        pltpu.VMEM((tb, 512), F32), pltpu.VMEM((tb, 512), F32), pltpu.VMEM((HALO, 512), F32),
                        pltpu.VMEM((SUB, 512), F32), pltpu.VMEM((SUB, 512), F32), pltpu.VMEM((SUB, 512), F32)]
        + _exchange_scratch(n_s),
        compiler_params=_params(),
    )(dh1, x, z1, proj, proj, hs, hs, cat, *smalls, *flat_parts)


def _exchange(name, gathered, scattered):
    n_g, n = len(gathered), len(gathered) + len(scattered)
    srcs = list(gathered) + list(scattered)
    shapes = [a.shape for a in gathered] + [a.shape[1:] for a in scattered]

    def body(*refs):
        ins, outs = refs[:n], refs[n:2 * n]
        srcs_of = [(lambda s, r=r: r) for r in ins[:n_g]] + [(lambda s, r=r: r.at[s]) for r in ins[n_g:]]
        start, wait = _all_to_all(srcs_of, outs, *refs[2 * n:])
        start()
        wait()

    any_spec = pl.BlockSpec(memory_space=pl.ANY)
    return pl.pallas_call(
        body, name=name, in_specs=[any_spec] * n, out_specs=[any_spec] * n,
        out_shape=[jax.ShapeDtypeStruct((N_DEV,) + tuple(sh), a.dtype) for sh, a in zip(shapes, srcs)],
        scratch_shapes=_exchange_scratch(n),
    )(*srcs)


def _adamw(name, parts, w, m, v, row_block):
    n_src, rows, cols = parts.shape
    rb = min(row_block, rows)

    def body(p_ref, w_ref, m_ref, v_ref, g_out, d_out, m_out, v_out):
        g = p_ref[0].astype(F32)
        for s in range(1, n_src):
            g = g + p_ref[s].astype(F32)
        m_new = ADAM_B1 * m_ref[...] + (1.0 - ADAM_B1) * g
        v_new = ADAM_B2 * v_ref[...] + (1.0 - ADAM_B2) * jnp.square(g)
        m_hat = m_new / (1.0 - ADAM_B1 ** ADAM_STEP)
        v_hat = v_new / (1.0 - ADAM_B2 ** ADAM_STEP)
        g_out[...] = g
        d_out[...] = -ADAM_LR * (m_hat / (jnp.sqrt(v_hat) + ADAM_EPS) + ADAM_WD * w_ref[...])
        m_out[...] = m_new
        v_out[...] = v_new

    spec = pl.BlockSpec((rb, cols), lambda i: (i, 0))
    return pl.pallas_call(
        body, name=name, grid=(rows // rb,),
        in_specs=[pl.BlockSpec((n_src, rb, cols), lambda i: (0, i, 0)), spec, spec, spec],
        out_specs=[spec] * 4, out_shape=[jax.ShapeDtypeStruct((rows, cols), F32)] * 4,
        compiler_params=pltpu.CompilerParams(dimension_semantics=("parallel",), vmem_limit_bytes=VMEM_LIMIT),
    )(parts, w, m, v)


def _sum_parts(name, parts):
    n_src, rows, cols = parts.shape

    def body(p_ref, o_ref):
        g = p_ref[0].astype(F32)
        for s in range(1, n_src):
            g = g + p_ref[s].astype(F32)
        o_ref[...] = g

    return pl.pallas_call(body, name=name, out_shape=jax.ShapeDtypeStruct((rows, cols), F32))(parts)


def _block_diag(blocks):
    g, a, b = blocks.shape
    eye = jnp.eye(g, dtype=blocks.dtype)
    return (eye[:, None, :, None] * blocks[:, :, None, :]).reshape(g * a, g * b)


def _diag_blocks(mat, g):
    a, b = mat.shape[0] // g, mat.shape[1] // g
    m4 = mat.reshape(g, a, g, b)
    return jnp.stack([m4[k, :, k, :] for k in range(g)], axis=0)


def _pack(pieces, width=1024):
    flat = jnp.concatenate([p.reshape(-1) for p in pieces])
    rows = -(-flat.shape[0] // (8 * width)) * 8
    return jnp.pad(flat, (0, rows * width - flat.shape[0])).reshape(rows, width)


def _unpack(packed, shapes):
    flat = packed.reshape(-1)
    out, o = [], 0
    for sh in shapes:
        size = 1
        for k in sh:
            size *= k
        out.append(flat[o:o + size].reshape(sh))
        o += size
    return out


def _small_grads(v512, v1024, mlp_vec, ple_vec, dwp_bd, dwg_bd, pool_b, gate_a_b, gate_x_b):
    return [
        v1024[1:2],
        _diag_blocks(dwp_bd, N_POOL_GROUPS)[None],
        v512[1:2].reshape(pool_b.shape),
        v512[0:1],
        v512[2:3],
        _diag_blocks(dwg_bd[:, :512], LRU_HEADS)[None],
        v1024[0:1, :512].reshape(gate_a_b.shape),
        _diag_blocks(dwg_bd[:, 512:], LRU_HEADS)[None],
        v1024[0:1, 512:].reshape(gate_x_b.shape),
        v512[3:4],
        mlp_vec[0:1],
        ple_vec[1:2],
        ple_vec[0:1],
        ple_vec[2:3].reshape(-1),
        v512[4:8][None],
        ple_vec[3:4, 0:1].reshape(1),
    ]


def kernel(x, p, norm_mix_g, w_in, pool_w, pool_b, pool_scale, conv_w, conv_b, gate_a_w, gate_a_b, gate_x_w, gate_x_b, lru_L, w_out, norm_mlp_g, w_up, w_down, norm_ple_g, w_ple_gate, b_ple_gate, w_ple_proj, norm_final_g, loss_target, m_norm_mix_g, m_w_in, m_pool_w, m_pool_b, m_pool_scale, m_conv_w, m_conv_b, m_gate_a_w, m_gate_a_b, m_gate_x_w, m_gate_x_b, m_lru_L, m_w_out, m_norm_mlp_g, m_w_up, m_w_down, m_norm_ple_g, m_w_ple_gate, m_b_ple_gate, m_w_ple_proj, m_norm_final_g, v_norm_mix_g, v_w_in, v_pool_w, v_pool_b, v_pool_scale, v_conv_w, v_conv_b, v_gate_a_w, v_gate_a_b, v_gate_x_w, v_gate_x_b, v_lru_L, v_w_out, v_norm_mlp_g, v_w_up, v_w_down, v_norm_ple_g, v_w_ple_gate, v_b_ple_gate, v_w_ple_proj, v_norm_final_g):
    t_len, d = x.shape[1], x.shape[2]
    tbs = {k: min(v, t_len) for k, v in TIME_BLOCKS.items()}
    me = _my_index()

    win_g, wout_g, convw_g = _exchange("gather_mixer_weights", [w_in[0].astype(BF16), w_out[0].astype(BF16), conv_w[0]], [])
    w_in_f = jnp.transpose(win_g, (1, 0, 2)).reshape(d, -1)
    conv_w_f = jnp.transpose(convw_g, (1, 0, 2)).reshape(convw_g.shape[1], -1)
    wp_bd = _block_diag(pool_w[0]).astype(BF16)
    wg_bd = jnp.concatenate([_block_diag(gate_a_w[0]), _block_diag(gate_x_w[0])], axis=1).astype(BF16)
    gate_b2 = jnp.concatenate([gate_a_b.reshape(1, -1), gate_x_b.reshape(1, -1)], axis=1)
    mixer_small = (norm_mix_g, w_in_f, wp_bd, pool_b.reshape(1, -1), pool_scale, conv_w_f, conv_b, wg_bd, gate_b2, lru_L,
                   wout_g.reshape(-1, d))

    x2 = x[0]
    later = [w_up[0].astype(BF16), w_down[0].astype(BF16), w_ple_gate[0].astype(BF16), w_ple_proj[0].astype(BF16)]
    h1, z1, proj, hs, cat, wup_g, wdn_g, wgate_g, wproj_g = _mix_fwd(x2, *mixer_small, later, tbs['mix_fwd'])
    w_down_f = wdn_g.reshape(-1, d)
    w_proj_f = jnp.transpose(wproj_g, (1, 0, 2)).reshape(wproj_g.shape[1], -1)
    h2, z2, up = _mlp_fwd(h1, norm_mlp_g, wup_g, w_down_f, tbs['mlp_fwd'])
    dh2, ple_vec, dw_gate, dw_proj = _ple(h2, p[0, 0], loss_target[0], norm_ple_g, wgate_g.reshape(-1, d), b_ple_gate,
                                          w_proj_f, norm_final_g.reshape(1, -1), tbs['ple'])
    dz2 = None
    dw_up_parts, dw_down_parts = [], []
    for part in range(MLP_BWD_SPLIT):
        res = _mlp_bwd_part(part, MLP_BWD_SPLIT, dh2, z2, up, wup_g, w_down_f, dz2, h1, norm_mlp_g, tbs['mlp_bwd'])
        dz2 = res[0]
        dw_up_parts.append(res[-2])
        dw_down_parts.append(res[-1])
    dh1, mlp_vec = res[0], res[1]
    n_proj = w_ple_proj.shape[2]
    early = [dw_up_parts, dw_down_parts, [dw_gate.reshape(N_DEV, -1, d)],
             [jnp.transpose(dw_proj.reshape(-1, N_DEV, n_proj), (1, 0, 2))]]
    (dx, v512, v1024, dw_in, dw_out, dwp_bd, dwg_bd,
     recv_up, recv_down, recv_gate, recv_proj) = _mix_bwd(dh1, x2, z1, proj, hs, cat, *mixer_small, early, tbs['mix_bwd'])

    small_grads = _small_grads(v512, v1024, mlp_vec, ple_vec, dwp_bd, dwg_bd, pool_b, gate_a_b, gate_x_b)
    n_in = w_in.shape[2]
    small_packed = _pack(small_grads)
    small_all, recv_in, recv_out = _exchange(
        "exchange_last_grads", [small_packed],
        [jnp.transpose(dw_in.reshape(d, N_DEV, n_in), (1, 0, 2)), dw_out.reshape(N_DEV, -1, d)])
    received = [recv_in, recv_out, recv_up, recv_down, recv_gate, recv_proj]
    small_w = [norm_mix_g, pool_w, pool_b, pool_scale, conv_b, gate_a_w, gate_a_b, gate_x_w, gate_x_b, lru_L,
               norm_mlp_g, norm_ple_g, b_ple_gate, norm_final_g]
    small_m = [m_norm_mix_g, m_pool_w, m_pool_b, m_pool_scale, m_conv_b, m_gate_a_w, m_gate_a_b, m_gate_x_w, m_gate_x_b,
               m_lru_L, m_norm_mlp_g, m_norm_ple_g, m_b_ple_gate, m_norm_final_g]
    small_v = [v_norm_mix_g, v_pool_w, v_pool_b, v_pool_scale, v_conv_b, v_gate_a_w, v_gate_a_b, v_gate_x_w, v_gate_x_b,
               v_lru_L, v_norm_mlp_g, v_norm_ple_g, v_b_ple_gate, v_norm_final_g]
    small_shapes = [a.shape for a in small_w]
    conv_full_shape = (1,) + conv_w_f.shape

    shard_w = [w_in[0], w_out[0], w_up[0], w_down[0], w_ple_gate[0], w_ple_proj[0]]
    shard_m = [m_w_in[0], m_w_out[0], m_w_up[0], m_w_down[0], m_w_ple_gate[0], m_w_ple_proj[0]]
    shard_v = [v_w_in[0], v_w_out[0], v_w_up[0], v_w_down[0], v_w_ple_gate[0], v_w_ple_proj[0]]
    names = ["w_in", "w_out", "w_up", "w_down", "w_ple_gate", "w_ple_proj"]
    big_res = {}
    for nm, parts, w_s, m_s, v_s in zip(names, received, shard_w, shard_m, shard_v):
        big_res[nm] = [r[None] for r in _adamw("adamw_" + nm, parts, w_s, m_s, v_s, 128)]

    small_sum = _sum_parts("sum_small_grads", small_all)
    summed = _unpack(small_sum, small_shapes + [conv_full_shape, (1,)])
    loss = summed[-1][0]
    conv_g = lax.dynamic_slice_in_dim(summed[-2], me * conv_w.shape[2], conv_w.shape[2], axis=2)
    sg = summed[:-2] + [conv_g]
    sw, sm, sv = small_w + [conv_w], small_m + [m_conv_w], small_v + [v_conv_w]
    shapes2 = small_shapes + [conv_w.shape]
    res = _adamw("adamw_small", _pack(sg)[None], _pack(sw), _pack(sm), _pack(sv), 1024)
    sres = [_unpack(r, shapes2) for r in res]
    small_names = ["norm_mix_g", "pool_w", "pool_b", "pool_scale", "conv_b", "gate_a_w", "gate_a_b", "gate_x_w",
                   "gate_x_b", "lru_L", "norm_mlp_g", "norm_ple_g", "b_ple_gate", "norm_final_g", "conv_w"]
    order = ["norm_mix_g", "w_in", "pool_w", "pool_b", "pool_scale", "conv_w", "conv_b", "gate_a_w", "gate_a_b",
             "gate_x_w", "gate_x_b", "lru_L", "w_out", "norm_mlp_g", "w_up", "w_down", "norm_ple_g", "w_ple_gate",
             "b_ple_gate", "w_ple_proj", "norm_final_g"]
    outs = [loss, dx[None]]
    for kind in range(4):
        for nm in order:
            if nm in big_res:
                outs.append(big_res[nm][kind])
            else:
                outs.append(sres[kind][small_names.index(nm)])
    return tuple(outs)
```

```python
import functools

import jax
import jax.numpy as jnp
from jax import lax
from jax.experimental import pallas as pl
from jax.experimental.pallas import tpu as pltpu

F32 = jnp.float32
BF16 = jnp.bfloat16
MESH = pl.DeviceIdType.MESH

N_DEV = 8
RMS_EPS = 1e-6
LRU_C = 8.0
POOL_WINDOWS = (2, 4, 8, 16)
N_POOL_GROUPS = 4
LRU_HEADS = 8
HALO = 16
SUB = 8
GELU_C0 = 0.7978845608028654
GELU_C1 = 0.044715

ADAM_LR = 0.001
ADAM_B1 = 0.9
ADAM_B2 = 0.999
ADAM_EPS = 1e-08
ADAM_WD = 0.01
ADAM_STEP = 10

VMEM_LIMIT = 60 * 1024 * 1024
TIME_BLOCKS = dict(mix_fwd=512, mlp_fwd=512, ple=512, mlp_bwd=512, mix_bwd=256)
MLP_BWD_SPLIT = 2
GATHER_FORWARD_AT = (0.5, 0.875, 1.0, 1.0)


def _params(n_arbitrary=1):
    return pltpu.CompilerParams(dimension_semantics=("arbitrary",) * n_arbitrary, vmem_limit_bytes=VMEM_LIMIT)


def _dot(a, b):
    return jnp.dot(a, b, preferred_element_type=F32)


def _dot_nt(a, b):
    return lax.dot_general(a, b, (((1,), (1,)), ((), ())), preferred_element_type=F32)


def _dot_tn(a, b):
    return lax.dot_general(a, b, (((0,), (0,)), ((), ())), preferred_element_type=F32)


def _rms_fwd(x, g):
    r = lax.rsqrt(jnp.mean(x * x, axis=-1, keepdims=True) + RMS_EPS)
    xh = x * r
    return xh * g, xh, r


def _rms_bwd(xh, r, g, dz):
    dxh = dz * g
    return r * (dxh - xh * jnp.mean(dxh * xh, axis=-1, keepdims=True))


def _colsum(a):
    return jnp.sum(a, axis=0, keepdims=True)


def _sigmoid(a):
    return 1.0 / (1.0 + jnp.exp(-a))


def _gelu_parts(u):
    u2 = u * u
    th = jnp.tanh(GELU_C0 * (u + GELU_C1 * u * u2))
    gel = 0.5 * u * (1.0 + th)
    dgel = 0.5 * (1.0 + th) + 0.5 * u * (1.0 - th * th) * (GELU_C0 * (1.0 + 3.0 * GELU_C1 * u2))
    return gel, dgel


def _my_index():
    return 4 * lax.axis_index("x") + 2 * lax.axis_index("y") + lax.axis_index("c")


def _all_to_all(srcs_of, dsts, send_sems, recv_sems, local_sems):
    n = len(dsts)
    me = _my_index()

    def remote(t, s):
        return pltpu.make_async_remote_copy(
            src_ref=srcs_of[t](s), dst_ref=dsts[t].at[me], send_sem=send_sems.at[t, s], recv_sem=recv_sems.at[t, me],
            device_id=(s // 4, (s // 2) % 2, s % 2), device_id_type=MESH)

    def arrival(t, s):
        return pltpu.make_async_remote_copy(
            src_ref=srcs_of[t](s), dst_ref=dsts[t].at[s], send_sem=send_sems.at[t, s], recv_sem=recv_sems.at[t, s],
            device_id=(s // 4, (s // 2) % 2, s % 2), device_id_type=MESH)

    def local(t, s):
        return pltpu.make_async_copy(srcs_of[t](s), dsts[t].at[s], local_sems.at[t])

    def start():
        for s in range(N_DEV):
            @pl.when(s == me)
            def _():
                for t in range(n):
                    local(t, s).start()

            @pl.when(s != me)
            def _():
                for t in range(n):
                    remote(t, s).start()

    def wait():
        for s in range(N_DEV):
            @pl.when(s == me)
            def _():
                for t in range(n):
                    local(t, s).wait()

            @pl.when(s != me)
            def _():
                for t in range(n):
                    remote(t, s).wait_send()
                    arrival(t, s).wait_recv()

    return start, wait


N_GATHER_COPIES = 7


def _two_level_gather(srcs, dsts, send_sems, recv_sems, local_sems):
    n = len(dsts)
    x, y, c = lax.axis_index("x"), lax.axis_index("y"), lax.axis_index("c")
    me, sibling = (x, y, c), (x, y, 1 - c)
    chips = [(1 - x, y), (x, 1 - y), (1 - x, 1 - y)]

    def slot(dev):
        return 4 * dev[0] + 2 * dev[1] + dev[2]

    def copy(t, k, block, to, src=None):
        return pltpu.make_async_remote_copy(
            src_ref=dsts[t].at[slot(block)] if src is None else src, dst_ref=dsts[t].at[slot(block)],
            send_sem=send_sems.at[t, k], recv_sem=recv_sems.at[t, k], device_id=to, device_id_type=MESH)

    def local(t):
        return pltpu.make_async_copy(srcs[t], dsts[t].at[slot(me)], local_sems.at[t])

    def start():
        for t in range(n):
            local(t).start()
            for j, chip in enumerate(chips):
                copy(t, 1 + j, me, (*chip, c), src=srcs[t]).start()
            copy(t, 0, me, sibling, src=srcs[t]).start()

    def forward(t):
        for j, chip in enumerate(chips):
            copy(t, 1 + j, (*chip, c), me).wait_recv()
            copy(t, 4 + j, (*chip, c), sibling).start()

    def finish():
        for t in range(n):
            copy(t, 0, sibling, me).wait_recv()
            for j, chip in enumerate(chips):
                copy(t, 4 + j, (*chip, 1 - c), me).wait_recv()
            copy(t, 0, me, sibling, src=srcs[t]).wait_send()
            for j, chip in enumerate(chips):
                copy(t, 1 + j, me, (*chip, c), src=srcs[t]).wait_send()
                copy(t, 4 + j, (*chip, c), sibling).wait_send()
            local(t).wait()

    return start, forward, finish


def _gather_scratch(n):
    return [pltpu.SemaphoreType.DMA((n, N_GATHER_COPIES)), pltpu.SemaphoreType.DMA((n, N_GATHER_COPIES)),
            pltpu.SemaphoreType.DMA((n,))]


def _gather(name, srcs):
    n = len(srcs)

    def body(*refs):
        start, forward, finish = _two_level_gather(refs[:n], refs[n:2 * n], *refs[2 * n:])
        start()
        for t in range(n):
            forward(t)
        finish()

    any_spec = pl.BlockSpec(memory_space=pl.ANY)
    return pl.pallas_call(
        body, name=name, in_specs=[any_spec] * n, out_specs=[any_spec] * n,
        out_shape=[jax.ShapeDtypeStruct((N_DEV,) + a.shape, a.dtype) for a in srcs], scratch_shapes=_gather_scratch(n),
    )(*srcs)


def _exchange_scratch(n):
    return [pltpu.SemaphoreType.DMA((n, N_DEV)), pltpu.SemaphoreType.DMA((n, N_DEV)), pltpu.SemaphoreType.DMA((n,))]


def _const_spec(shape):
    nd = len(shape)
    return pl.BlockSpec(shape, lambda i: (0,) * nd, pipeline_mode=pl.Buffered(1))


def _pool_windows(up_ext, n, forward):
    sh = (lambda k: k) if forward else (lambda k: n - k)
    s2 = up_ext + pltpu.roll(up_ext, sh(1), 0)
    t4 = s2[:, 128:]
    s4 = t4 + pltpu.roll(t4, sh(2), 0)
    t8 = s4[:, 128:]
    s8 = t8 + pltpu.roll(t8, sh(4), 0)
    t16 = s8[:, 128:]
    s16 = t16 + pltpu.roll(t16, sh(8), 0)
    return jnp.concatenate([s2[:, :128], s4[:, :128], s8[:, :128], s16], axis=1)


def _inv_count(t0, tb):
    t = (t0 + lax.broadcasted_iota(jnp.int32, (tb, 1), 0) + 1).astype(F32)
    cols = [jnp.broadcast_to(1.0 / jnp.minimum(t, float(w)), (tb, 128)) for w in POOL_WINDOWS]
    return jnp.concatenate(cols, axis=1)


def _lru_gates(xb, wg, gb, c_l, first_row):
    gp = _dot(xb.astype(BF16), wg) + gb
    r = _sigmoid(gp[:, :512])
    ig = _sigmoid(gp[:, 512:])
    la = c_l * r
    a = jnp.exp(la)
    a2 = a * a
    m2 = -jnp.tanh(la) * (a2 + 1.0)
    mult = jnp.where(first_row, 1.0, jnp.sqrt(m2))
    return r, ig, a, a2, m2, mult


def _log_sigmoid(v):
    return -(jnp.maximum(-v, 0.0) + jnp.log1p(jnp.exp(-jnp.abs(v))))


def _conv_fwd(ul_ext, cw, cb):
    return (cb + cw[3:4, :] * ul_ext + cw[2:3, :] * pltpu.roll(ul_ext, 1, 0)
            + cw[1:2, :] * pltpu.roll(ul_ext, 2, 0) + cw[0:1, :] * pltpu.roll(ul_ext, 3, 0))


def _mix_fwd(x, g_mix, w_in, wp_bd, pool_b, pool_scale, conv_w, conv_b, wg_bd, gate_b, lru_l, w_out, gather_srcs,
             forward_at, tb):
    t_len, d = x.shape
    nb = t_len // tb
    n_g = len(gather_srcs)
    forward_step = [min(nb - 1, int(f * nb)) for f in forward_at]

    def body(*refs):
        x_ref, g_ref, win_ref, wp_ref, pb_ref, ps_ref, cw_ref, cb_ref, wg_ref, gb_ref, l_ref, wout_ref = refs[:12]
        gsrc = refs[12:12 + n_g]
        h1_ref, z1_ref, proj_ref, hs_ref, cat_ref = refs[12 + n_g:17 + n_g]
        gdst = refs[17 + n_g:17 + 2 * n_g]
        ext_ref, a_ref, b_ref, hc_ref, send_sems, recv_sems, local_sems = refs[17 + 2 * n_g:]
        i = pl.program_id(0)
        start_gather, forward_gather, finish_gather = _two_level_gather(gsrc, gdst, send_sems, recv_sems, local_sems)

        @pl.when(i == 0)
        def _():
            start_gather()
            ext_ref[0:HALO, :] = jnp.zeros((HALO, 1024), F32)
            hc_ref[...] = jnp.zeros_like(hc_ref)

        xv = x_ref[...]
        z, _, _ = _rms_fwd(xv, g_ref[...])
        zb = z.astype(BF16)
        z1_ref[...] = zb
        proj = _dot(zb, win_ref[...])
        proj_ref[...] = proj
        ext_ref[HALO:, :] = proj[:, 0:1024]
        ug = proj[:, 1024:1536]
        n = tb + HALO
        up_ext = ext_ref[:, 0:512]
        win = _pool_windows(up_ext, n, True)[HALO:]
        dpool = win * _inv_count(i * tb, tb) - proj[:, 0:512]
        q = _dot(dpool.astype(BF16), wp_ref[...]) + pb_ref[...]
        y_pool = q * ps_ref[...]
        xb = _conv_fwd(ext_ref[:, 512:1024], cw_ref[...], cb_ref[...])[HALO:]
        first_row = (i * tb + lax.broadcasted_iota(jnp.int32, (tb, 1), 0)) == 0
        c_l = LRU_C * _log_sigmoid(l_ref[...])
        _, ig, a, _, _, mult = _lru_gates(xb, wg_ref[...], gb_ref[...], c_l, first_row)
        a_ref[...] = a
        b_ref[...] = mult * (ig * xb)
        row = lax.broadcasted_iota(jnp.int32, (SUB, 512), 0)

        def group(j, hprev):
            o = pl.multiple_of(j * SUB, SUB)
            a8 = a_ref[pl.ds(o, SUB), :]
            b8 = b_ref[pl.ds(o, SUB), :]
            for sh in (1, 2, 4):
                ash = jnp.where(row >= sh, pltpu.roll(a8, sh, 0), 1.0)
                bsh = jnp.where(row >= sh, pltpu.roll(b8, sh, 0), 0.0)
                b8 = a8 * bsh + b8
                a8 = a8 * ash
            h8 = a8 * hprev + b8
            hs_ref[pl.ds(o, SUB), :] = h8
            return jnp.broadcast_to(h8[SUB - 1:SUB, :], (SUB, 512))

        hc_ref[...] = lax.fori_loop(0, tb // SUB, group, hc_ref[...])
        gel, _ = _gelu_parts(ug)
        y_lru = hs_ref[...] * gel
        catb = jnp.concatenate([y_pool, y_lru], axis=1).astype(BF16)
        cat_ref[...] = catb
        h1_ref[...] = xv + _dot(catb, wout_ref[...])
        ext_ref[0:HALO, :] = ext_ref[tb:tb + HALO, :]

        for t in range(n_g):
            @pl.when(i == forward_step[t])
            def _():
                forward_gather(t)

        @pl.when(i == nb - 1)
        def _():
            finish_gather()

    row_spec = lambda w: pl.BlockSpec((tb, w), lambda i: (i, 0))
    any_spec = pl.BlockSpec(memory_space=pl.ANY)
    smalls = [g_mix, w_in, wp_bd, pool_b, pool_scale, conv_w, conv_b, wg_bd, gate_b, lru_l, w_out]
    return pl.pallas_call(
        body, name="mix_fwd", grid=(nb,),
        in_specs=[row_spec(d)] + [_const_spec(s.shape) for s in smalls] + [any_spec] * n_g,
        out_specs=[row_spec(d), row_spec(d), row_spec(1536), row_spec(512), row_spec(1024)] + [any_spec] * n_g,
        out_shape=[jax.ShapeDtypeStruct((t_len, d), F32), jax.ShapeDtypeStruct((t_len, d), BF16),
                   jax.ShapeDtypeStruct((t_len, 1536), F32), jax.ShapeDtypeStruct((t_len, 512), F32),
                   jax.ShapeDtypeStruct((t_len, 1024), BF16)]
        + [jax.ShapeDtypeStruct((N_DEV,) + s.shape, s.dtype) for s in gather_srcs],
        scratch_shapes=[pltpu.VMEM((tb + HALO, 1024), F32), pltpu.VMEM((tb, 512), F32), pltpu.VMEM((tb, 512), F32),
                        pltpu.VMEM((SUB, 512), F32)] + _gather_scratch(n_g),
        compiler_params=_params(),
    )(x, *smalls, *gather_srcs)


def _mlp_fwd(h1, g_mlp, w_up, w_down, tb):
    t_len, d = h1.shape
    nb = t_len // tb
    n_chunk, _, fc = w_up.shape

    def body(h1_ref, g_ref, wup_ref, wdn_ref, h2_ref, z2_ref, up_ref):
        xv = h1_ref[...]
        z, _, _ = _rms_fwd(xv, g_ref[...])
        zb = z.astype(BF16)
        z2_ref[...] = zb
        acc = xv
        for c in range(n_chunk):
            u = _dot(zb, wup_ref[c])
            up_ref[:, c * fc:(c + 1) * fc] = u.astype(BF16)
            act = jnp.square(jnp.maximum(u, 0.0)).astype(BF16)
            acc = acc + _dot(act, wdn_ref[c * fc:(c + 1) * fc, :])
        h2_ref[...] = acc

    row_spec = lambda w: pl.BlockSpec((tb, w), lambda i: (i, 0))
    return pl.pallas_call(
        body, name="mlp_fwd", grid=(nb,),
        in_specs=[row_spec(d), _const_spec(g_mlp.shape), _const_spec(w_up.shape), _const_spec(w_down.shape)],
        out_specs=[row_spec(d), row_spec(d), row_spec(n_chunk * fc)],
        out_shape=[jax.ShapeDtypeStruct((t_len, d), F32), jax.ShapeDtypeStruct((t_len, d), BF16),
                   jax.ShapeDtypeStruct((t_len, n_chunk * fc), BF16)],
        compiler_params=_params(),
    )(h1, g_mlp, w_up, w_down)


def _ple(h2, p, target, g_ple, w_gate, b_gate, w_proj, g_final, tb):
    t_len, d = h2.shape
    nb = t_len // tb
    pd = p.shape[1]

    def body(h2_ref, p_ref, tgt_ref, g_ref, wg_ref, bg_ref, wp_ref, gf_ref,
             dh2_ref, vec_ref, dwg_out, dwp_out, dwg_acc, dwp_acc, dwg_stage, dwp_stage):
        i = pl.program_id(0)

        @pl.when(i == 0)
        def _():
            vec_ref[...] = jnp.zeros_like(vec_ref)
            dwg_acc[...] = jnp.zeros_like(dwg_acc)
            dwp_acc[...] = jnp.zeros_like(dwp_acc)

        h2 = h2_ref[...]
        g2 = g_ref[...]
        z3, xh2, r2 = _rms_fwd(h2, g2)
        z3b = z3.astype(BF16)
        gate = _sigmoid(_dot(z3b, wg_ref[...]) + bg_ref[...])
        pb = p_ref[...].astype(BF16)
        pp = _dot(pb, wp_ref[...])
        h3 = h2 + gate * pp
        gf = gf_ref[...]
        y, xh3, r3 = _rms_fwd(h3, gf)
        err = y - tgt_ref[...]
        loss_rows = jnp.mean(err * err, axis=-1, keepdims=True)
        dy = err * (1.0 / d)
        dh3 = _rms_bwd(xh3, r3, gf, dy)
        dgl = (dh3 * pp) * (gate * (1.0 - gate))
        dpp = dh3 * gate
        dglb = dgl.astype(BF16)
        dwg_acc[...] += _dot_tn(z3b, dglb)
        dwp_acc[...] += _dot_tn(pb, dpp.astype(BF16))
        dz3 = _dot_nt(dglb, wg_ref[...])
        dh2_ref[...] = dh3 + _rms_bwd(xh2, r2, g2, dz3)
        vec_ref[0:1, :] += _colsum(dgl)
        vec_ref[1:2, :] += _colsum(dz3 * xh2)
        vec_ref[2:3, :] += _colsum(dy * xh3)
        vec_ref[3:4, :] += 0.5 * jnp.sum(loss_rows)

        @pl.when(i == nb - 1)
        def _():
            dwg_stage[...] = dwg_acc[...].astype(BF16)
            dwp_stage[...] = dwp_acc[...].astype(BF16)
            pltpu.sync_copy(dwg_stage, dwg_out)
            pltpu.sync_copy(dwp_stage, dwp_out)

    row_spec = lambda w: pl.BlockSpec((tb, w), lambda i: (i, 0))
    any_spec = pl.BlockSpec(memory_space=pl.ANY)
    smalls = [g_ple, w_gate, b_gate, w_proj, g_final]
    return pl.pallas_call(
        body, name="ple_fwd_bwd", grid=(nb,),
        in_specs=[row_spec(d), row_spec(pd), row_spec(d)] + [_const_spec(s.shape) for s in smalls],
        out_specs=[row_spec(d), pl.BlockSpec((8, d), lambda i: (0, 0)), any_spec, any_spec],
        out_shape=[jax.ShapeDtypeStruct((t_len, d), F32), jax.ShapeDtypeStruct((8, d), F32),
                   jax.ShapeDtypeStruct(w_gate.shape, BF16), jax.ShapeDtypeStruct(w_proj.shape, BF16)],
        scratch_shapes=[pltpu.VMEM(w_gate.shape, F32), pltpu.VMEM(w_proj.shape, F32), pltpu.VMEM(w_gate.shape, BF16),
                        pltpu.VMEM(w_proj.shape, BF16)],
        compiler_params=_params(),
    )(h2, p, target, *smalls)


def _mlp_bwd_part(part, n_part, dh2, z2, up, w_up, w_down, dz2_prev, h1, g_mlp, tb):
    t_len, d = dh2.shape
    nb = t_len // tb
    n_chunk_all, _, fc = w_up.shape
    n_chunk = n_chunk_all // n_part
    first, last = part == 0, part == n_part - 1

    def body(*refs):
        refs = list(refs)
        dh2_ref, z2_ref, up_ref, wup_ref, wdn_ref = refs[:5]
        del refs[:5]
        dzp_ref = None if first else refs.pop(0)
        h1_ref, g_ref = (refs.pop(0), refs.pop(0)) if last else (None, None)
        out_ref = refs.pop(0)
        vec_ref = refs.pop(0) if last else None
        dwup_out, dwdn_out, dwup_acc, dwdn_acc, up_stage, dn_stage = refs
        i = pl.program_id(0)

        @pl.when(i == 0)
        def _():
            dwup_acc[...] = jnp.zeros_like(dwup_acc)
            dwdn_acc[...] = jnp.zeros_like(dwdn_acc)
            if last:
                vec_ref[...] = jnp.zeros_like(vec_ref)

        dh2 = dh2_ref[...]
        dh2b = dh2.astype(BF16)
        z2b = z2_ref[...]
        dz2 = jnp.zeros((tb, d), F32) if first else dzp_ref[...]
        for c in range(n_chunk):
            u = up_ref[:, c * fc:(c + 1) * fc].astype(F32)
            ur = jnp.maximum(u, 0.0)
            dact = _dot_nt(dh2b, wdn_ref[c * fc:(c + 1) * fc, :])
            dupb = (dact * (2.0 * ur)).astype(BF16)
            dwdn_acc[c * fc:(c + 1) * fc, :] += _dot_tn((ur * ur).astype(BF16), dh2b)
            dwup_acc[c] += _dot_tn(z2b, dupb)
            dz2 = dz2 + _dot_nt(dupb, wup_ref[c])
        if last:
            g = g_ref[...]
            _, xh, r = _rms_fwd(h1_ref[...], g)
            out_ref[...] = dh2 + _rms_bwd(xh, r, g, dz2)
            vec_ref[0:1, :] += _colsum(dz2 * xh)
        else:
            out_ref[...] = dz2

        @pl.when(i == nb - 1)
        def _():
            for c in range(n_chunk):
                up_stage[...] = dwup_acc[c].astype(BF16)
                dn_stage[...] = dwdn_acc[c * fc:(c + 1) * fc, :].astype(BF16)
                pltpu.sync_copy(up_stage, dwup_out.at[c])
                pltpu.sync_copy(dn_stage, dwdn_out.at[c])

    row_spec = lambda w: pl.BlockSpec((tb, w), lambda i: (i, 0))
    any_spec = pl.BlockSpec(memory_space=pl.ANY)
    args = [dh2, z2, up, w_up, w_down]
    in_specs = [row_spec(d), row_spec(d), pl.BlockSpec((tb, n_chunk * fc), lambda i: (i, part)),
                pl.BlockSpec((n_chunk, d, fc), lambda i: (part, 0, 0), pipeline_mode=pl.Buffered(1)),
                pl.BlockSpec((n_chunk * fc, d), lambda i: (part, 0), pipeline_mode=pl.Buffered(1))]
    if not first:
        args.append(dz2_prev)
        in_specs.append(row_spec(d))
    if last:
        args += [h1, g_mlp]
        in_specs += [row_spec(d), _const_spec(g_mlp.shape)]
    out_specs = [row_spec(d)]
    out_shape = [jax.ShapeDtypeStruct((t_len, d), F32)]
    if last:
        out_specs.append(pl.BlockSpec((8, d), lambda i: (0, 0)))
        out_shape.append(jax.ShapeDtypeStruct((8, d), F32))
    out_specs += [any_spec, any_spec]
    out_shape += [jax.ShapeDtypeStruct((n_chunk, d, fc), BF16), jax.ShapeDtypeStruct((n_chunk, fc, d), BF16)]
    return pl.pallas_call(
        body, name=f"mlp_bwd_{part}", grid=(nb,), in_specs=in_specs, out_specs=out_specs, out_shape=out_shape,
        scratch_shapes=[pltpu.VMEM((n_chunk, d, fc), F32), pltpu.VMEM((n_chunk * fc, d), F32),
                        pltpu.VMEM((d, fc), BF16), pltpu.VMEM((fc, d), BF16)],
        compiler_params=_params(),
    )(*args)


def _mix_bwd(dh1, x, z1, proj, hs, cat, g_mix, w_in, wp_bd, pool_b, pool_scale, conv_w, conv_b, wg_bd, gate_b, lru_l,
             w_out, scatter_parts, tb):
    t_len, d = x.shape
    nb = t_len // tb
    n_s = len(scatter_parts)
    flat_parts = [a for parts in scatter_parts for a in parts]

    def body(*refs):
        refs = list(refs)
        (dh1_ref, x_ref, z1_ref, proj_ref, projh_ref, hs_ref, hsh_ref, cat_ref,
         g_ref, win_ref, wp_ref, pb_ref, ps_ref, cw_ref, cb_ref, wg_ref, gb_ref, l_ref, wout_ref) = refs[:19]
        del refs[:19]
        part_refs = []
        for parts in scatter_parts:
            part_refs.append(refs[:len(parts)])
            del refs[:len(parts)]
        dx_ref, v512_ref, v1024_ref, dwin_out, dwout_out, dwp_out, dwg_out = refs[:7]
        recv = refs[7:7 + n_s]
        (dwin_acc, dwout_acc, dwp_acc, dwg_acc, ext_ref, a_ref, b_ref, gs_ref,
         ehead_ref, dxbhead_ref, ahead_ref, gc_ref, dwin_stage, dwout_stage, send_sems, recv_sems,
         local_sems) = refs[7 + n_s:]
        i = pl.program_id(0)
        tbk = nb - 1 - i

        def block_of(prefs):
            per = N_DEV // len(prefs)
            return lambda s: prefs[s // per].at[s % per]

        start_scatter, wait_scatter = _all_to_all([block_of(pr) for pr in part_refs], recv, send_sems, recv_sems,
                                                  local_sems)

        @pl.when(i == 0)
        def _():
            start_scatter()
            for ref in (v512_ref, v1024_ref, dwin_acc, dwout_acc, dwp_acc, dwg_acc, ehead_ref, dxbhead_ref, ahead_ref,
                        gc_ref):
                ref[...] = jnp.zeros_like(ref)

        dh1 = dh1_ref[...]
        dh1b = dh1.astype(BF16)
        dcat = _dot_nt(dh1b, wout_ref[...])
        dwout_acc[...] += _dot_tn(cat_ref[...], dh1b)

        proj = proj_ref[...]
        has_prev = (tbk > 0).astype(F32)
        ext_ref[0:HALO, :] = projh_ref[:, 0:1024] * has_prev
        ext_ref[HALO:, :] = proj[:, 0:1024]
        ug = proj[:, 1024:1536]
        n = tb + HALO
        inv = _inv_count(tbk * tb, tb)

        up_ext = ext_ref[:, 0:512]
        win = _pool_windows(up_ext, n, True)[HALO:]
        dpool = win * inv - proj[:, 0:512]
        dpoolb = dpool.astype(BF16)
        q = _dot(dpoolb, wp_ref[...]) + pb_ref[...]
        dyp = dcat[:, 0:512]
        dq = dyp * ps_ref[...]
        dqb = dq.astype(BF16)
        v512_ref[0:1, :] += _colsum(dyp * q)
        v512_ref[1:2, :] += _colsum(dq)
        dwp_acc[...] += _dot_tn(dpoolb, dqb)
        dd = _dot_nt(dqb, wp_ref[...])
        e = dd * inv
        e_ext = jnp.concatenate([e, ehead_ref[...]], axis=0)
        du_pool = _pool_windows(e_ext, n, False)[0:tb] - dd
        ehead_ref[...] = e[0:HALO]

        gel, dgel = _gelu_parts(ug)
        hsv = hs_ref[...]
        dcl = dcat[:, 512:1024]
        dhs = dcl * gel
        dug = dcl * hsv * dgel
        ul_ext = ext_ref[:, 512:1024]
        cw = cw_ref[...]
        xb = _conv_fwd(ul_ext, cw, cb_ref[...])[HALO:]
        first_row = (tbk * tb + lax.broadcasted_iota(jnp.int32, (tb, 1), 0)) == 0
        c_l = LRU_C * _log_sigmoid(l_ref[...])
        r, ig, a, a2, m2, mult = _lru_gates(xb, wg_ref[...], gb_ref[...], c_l, first_row)
        a_ext = jnp.concatenate([a, ahead_ref[...]], axis=0)
        a_ref[...] = pltpu.roll(a_ext, tb + SUB - 1, 0)[0:tb]
        b_ref[...] = dhs
        ahead_ref[...] = a[0:SUB]
        row = lax.broadcasted_iota(jnp.int32, (SUB, 512), 0)

        def group(jj, gnext):
            o = pl.multiple_of((tb // SUB - 1 - jj) * SUB, SUB)
            a8 = a_ref[pl.ds(o, SUB), :]
            b8 = b_ref[pl.ds(o, SUB), :]
            for sh in (1, 2, 4):
                ash = jnp.where(row < SUB - sh, pltpu.roll(a8, SUB - sh, 0), 1.0)
                bsh = jnp.where(row < SUB - sh, pltpu.roll(b8, SUB - sh, 0), 0.0)
                b8 = a8 * bsh + b8
                a8 = a8 * ash
            g8 = a8 * gnext + b8
            gs_ref[pl.ds(o, SUB), :] = g8
            return jnp.broadcast_to(g8[0:1, :], (SUB, 512))

        gc_ref[...] = lax.fori_loop(0, tb // SUB, group, gc_ref[...])
        gsum = gs_ref[...]
        hs_ext = jnp.concatenate([hsh_ref[...] * has_prev, hsv], axis=0)
        hprev = pltpu.roll(hs_ext, 1, 0)[SUB:]
        da = gsum * hprev
        dmult = jnp.where(first_row, 0.0, gsum * (ig * xb))
        di = gsum * mult * xb
        dxb = gsum * mult * ig
        dla = da * a - dmult * a2 * lax.rsqrt(m2)
        dr = dla * c_l
        v512_ref[3:4, :] += _colsum(dla * r)
        dgp = jnp.concatenate([dr * r * (1.0 - r), di * ig * (1.0 - ig)], axis=1)
        dgpb = dgp.astype(BF16)
        v1024_ref[0:1, :] += _colsum(dgp)
        dwg_acc[...] += _dot_tn(xb.astype(BF16), dgpb)
        dxb = dxb + _dot_nt(dgpb, wg_ref[...])
        n8 = tb + SUB
        dxb_ext = jnp.concatenate([dxb, dxbhead_ref[...]], axis=0)
        du_lru = (cw[3:4, :] * dxb + cw[2:3, :] * pltpu.roll(dxb_ext, n8 - 1, 0)[0:tb]
                  + cw[1:2, :] * pltpu.roll(dxb_ext, n8 - 2, 0)[0:tb] + cw[0:1, :] * pltpu.roll(dxb_ext, n8 - 3, 0)[0:tb])
        dxbhead_ref[...] = dxb[0:SUB]
        v512_ref[2:3, :] += _colsum(dxb)
        for j in range(4):
            shifted = ul_ext if j == 0 else pltpu.roll(ul_ext, j, 0)
            v512_ref[4 + (3 - j):5 + (3 - j), :] += _colsum(dxb * shifted[HALO:])

        dprojb = jnp.concatenate([du_pool, du_lru, dug], axis=1).astype(BF16)
        dwin_acc[...] += _dot_tn(z1_ref[...], dprojb)
        dz1 = _dot_nt(dprojb, win_ref[...])
        g = g_ref[...]
        _, xh, rr = _rms_fwd(x_ref[...], g)
        dx_ref[...] = dh1 + _rms_bwd(xh, rr, g, dz1)
        v1024_ref[1:2, :] += _colsum(dz1 * xh)

        @pl.when(i == nb - 1)
        def _():
            v512_ref[3:4, :] = v512_ref[3:4, :] * (LRU_C * _sigmoid(-l_ref[...]))
            dwin_stage[...] = dwin_acc[...].astype(BF16)
            dwout_stage[...] = dwout_acc[...].astype(BF16)
            pltpu.sync_copy(dwin_stage, dwin_out)
            pltpu.sync_copy(dwout_stage, dwout_out)
            pltpu.sync_copy(dwp_acc, dwp_out)
            pltpu.sync_copy(dwg_acc, dwg_out)
            wait_scatter()

    rev = lambda w: pl.BlockSpec((tb, w), lambda i: (nb - 1 - i, 0))
    halo = lambda rows, w: pl.BlockSpec((rows, w), lambda i: (jnp.maximum((nb - 1 - i) * (tb // rows) - 1, 0), 0))
    any_spec = pl.BlockSpec(memory_space=pl.ANY)
    smalls = [g_mix, w_in, wp_bd, pool_b, pool_scale, conv_w, conv_b, wg_bd, gate_b, lru_l, w_out]
    return pl.pallas_call(
        body, name="mix_bwd", grid=(nb,),
        in_specs=[rev(d), rev(d), rev(d), rev(1536), halo(HALO, 1536), rev(512), halo(SUB, 512), rev(1024)]
        + [_const_spec(s.shape) for s in smalls] + [any_spec] * len(flat_parts),
        out_specs=[rev(d), pl.BlockSpec((8, 512), lambda i: (0, 0)), pl.BlockSpec((8, 1024), lambda i: (0, 0))]
        + [any_spec] * (4 + n_s),
        out_shape=[jax.ShapeDtypeStruct((t_len, d), F32), jax.ShapeDtypeStruct((8, 512), F32),
                   jax.ShapeDtypeStruct((8, 1024), F32), jax.ShapeDtypeStruct(w_in.shape, BF16),
                   jax.ShapeDtypeStruct(w_out.shape, BF16), jax.ShapeDtypeStruct(wp_bd.shape, F32),
                   jax.ShapeDtypeStruct(wg_bd.shape, F32)]
        + [jax.ShapeDtypeStruct((N_DEV,) + parts[0].shape[1:], parts[0].dtype) for parts in scatter_parts],
        scratch_shapes=[pltpu.VMEM(w_in.shape, F32), pltpu.VMEM(w_out.shape, F32), pltpu.VMEM(wp_bd.shape, F32),
                        pltpu.VMEM(wg_bd.shape, F32), pltpu.VMEM((tb + HALO, 1024), F32), pltpu.VMEM((tb, 512), F32),
                        pltpu.VMEM((tb, 512), F32), pltpu.VMEM((tb, 512), F32), pltpu.VMEM((HALO, 512), F32),
                        pltpu.VMEM((SUB, 512), F32), pltpu.VMEM((SUB, 512), F32), pltpu.VMEM((SUB, 512), F32),
                        pltpu.VMEM(w_in.shape, BF16), pltpu.VMEM(w_out.shape, BF16)]
        + _exchange_scratch(n_s),
        compiler_params=_params(),
    )(dh1, x, z1, proj, proj, hs, hs, cat, *smalls, *flat_parts)


def _exchange(name, gathered, scattered):
    n_g, n = len(gathered), len(gathered) + len(scattered)
    srcs = list(gathered) + list(scattered)
    shapes = [a.shape for a in gathered] + [a.shape[1:] for a in scattered]

    def body(*refs):
        ins, outs = refs[:n], refs[n:2 * n]
        srcs_of = [(lambda s, r=r: r) for r in ins[:n_g]] + [(lambda s, r=r: r.at[s]) for r in ins[n_g:]]
        start, wait = _all_to_all(srcs_of, outs, *refs[2 * n:])
        start()
        wait()

    any_spec = pl.BlockSpec(memory_space=pl.ANY)
    return pl.pallas_call(
        body, name=name, in_specs=[any_spec] * n, out_specs=[any_spec] * n,
        out_shape=[jax.ShapeDtypeStruct((N_DEV,) + tuple(sh), a.dtype) for sh, a in zip(shapes, srcs)],
        scratch_shapes=_exchange_scratch(n),
    )(*srcs)


def _final_exchange(name, small, scattered):
    n = len(scattered)
    rows = small.shape[0]
    per = rows // N_DEV

    def body(*refs):
        small_ref = refs[0]
        ins = refs[1:1 + n]
        sum_ref = refs[1 + n]
        outs = refs[2 + n:2 + 2 * n]
        land_ref, send_sems, recv_sems, local_sems, small_send, small_recv = refs[2 + 2 * n:]
        me = _my_index()
        start, wait = _all_to_all([(lambda s, r=r: r.at[s]) for r in ins], outs, send_sems, recv_sems, local_sems)
        start()
        mine = pl.ds(pl.multiple_of(me * per, 8), per)

        def dev(s):
            return (s // 4, (s // 2) % 2, s % 2)

        def partial_to(s):
            return pltpu.make_async_remote_copy(
                src_ref=small_ref.at[pl.ds(s * per, per)], dst_ref=land_ref.at[me], send_sem=small_send.at[0, s],
                recv_sem=small_recv.at[0, me], device_id=dev(s), device_id_type=MESH)

        def partial_from(s):
            return pltpu.make_async_remote_copy(
                src_ref=small_ref.at[pl.ds(s * per, per)], dst_ref=land_ref.at[s], send_sem=small_send.at[0, s],
                recv_sem=small_recv.at[0, s], device_id=dev(s), device_id_type=MESH)

        def sum_to(s):
            return pltpu.make_async_remote_copy(
                src_ref=sum_ref.at[mine], dst_ref=sum_ref.at[mine], send_sem=small_send.at[1, s],
                recv_sem=small_recv.at[1, me], device_id=dev(s), device_id_type=MESH)

        def sum_from(s):
            rows_s = pl.ds(s * per, per)
            return pltpu.make_async_remote_copy(
                src_ref=sum_ref.at[rows_s], dst_ref=sum_ref.at[rows_s], send_sem=small_send.at[1, s],
                recv_sem=small_recv.at[1, s], device_id=dev(s), device_id_type=MESH)

        for s in range(N_DEV):
            @pl.when(s != me)
            def _():
                partial_to(s).start()
        land_ref[me] = small_ref[mine, :]
        for s in range(N_DEV):
            @pl.when(s != me)
            def _():
                partial_from(s).wait_recv()
        total = land_ref[0]
        for s in range(1, N_DEV):
            total = total + land_ref[s]
        sum_ref[mine, :] = total
        for s in range(N_DEV):
            @pl.when(s != me)
            def _():
                sum_to(s).start()
        for s in range(N_DEV):
            @pl.when(s != me)
            def _():
                sum_from(s).wait_recv()
                partial_to(s).wait_send()
                sum_to(s).wait_send()
        wait()

    any_spec = pl.BlockSpec(memory_space=pl.ANY)
    vmem_spec = pl.BlockSpec(memory_space=pltpu.VMEM)
    res = pl.pallas_call(
        body, name=name, in_specs=[vmem_spec] + [any_spec] * n, out_specs=[vmem_spec] + [any_spec] * n,
        out_shape=[jax.ShapeDtypeStruct(small.shape, F32)]
        + [jax.ShapeDtypeStruct(a.shape, a.dtype) for a in scattered],
        scratch_shapes=[pltpu.VMEM((N_DEV, per, small.shape[1]), F32)] + _exchange_scratch(n)
        + [pltpu.SemaphoreType.DMA((2, N_DEV)), pltpu.SemaphoreType.DMA((2, N_DEV))],
    )(small, *scattered)
    return res[0], res[1:]


def _adamw(name, parts, w, m, v, row_block):
    n_src, rows, cols = parts.shape
    rb = min(row_block, rows)

    def body(p_ref, w_ref, m_ref, v_ref, g_out, d_out, m_out, v_out):
        g = p_ref[0].astype(F32)
        for s in range(1, n_src):
            g = g + p_ref[s].astype(F32)
        m_new = ADAM_B1 * m_ref[...] + (1.0 - ADAM_B1) * g
        v_new = ADAM_B2 * v_ref[...] + (1.0 - ADAM_B2) * jnp.square(g)
        m_hat = m_new / (1.0 - ADAM_B1 ** ADAM_STEP)
        v_hat = v_new / (1.0 - ADAM_B2 ** ADAM_STEP)
        g_out[...] = g
        d_out[...] = -ADAM_LR * (m_hat / (jnp.sqrt(v_hat) + ADAM_EPS) + ADAM_WD * w_ref[...])
        m_out[...] = m_new
        v_out[...] = v_new

    spec = pl.BlockSpec((rb, cols), lambda i: (i, 0))
    return pl.pallas_call(
        body, name=name, grid=(rows // rb,),
        in_specs=[pl.BlockSpec((n_src, rb, cols), lambda i: (0, i, 0)), spec, spec, spec],
        out_specs=[spec] * 4, out_shape=[jax.ShapeDtypeStruct((rows, cols), F32)] * 4,
        compiler_params=pltpu.CompilerParams(dimension_semantics=("parallel",), vmem_limit_bytes=VMEM_LIMIT),
    )(parts, w, m, v)


def _block_diag(blocks):
    g, a, b = blocks.shape
    eye = jnp.eye(g, dtype=blocks.dtype)
    return (eye[:, None, :, None] * blocks[:, :, None, :]).reshape(g * a, g * b)


def _diag_blocks(mat, g):
    a, b = mat.shape[0] // g, mat.shape[1] // g
    m4 = mat.reshape(g, a, g, b)
    return jnp.stack([m4[k, :, k, :] for k in range(g)], axis=0)


def _pack(pieces, row_multiple=8, width=1024):
    flat = jnp.concatenate([p.reshape(-1) for p in pieces])
    rows = -(-flat.shape[0] // (row_multiple * width)) * row_multiple
    return jnp.pad(flat, (0, rows * width - flat.shape[0])).reshape(rows, width)


def _unpack(packed, shapes):
    flat = packed.reshape(-1)
    out, o = [], 0
    for sh in shapes:
        size = 1
        for k in sh:
            size *= k
        out.append(flat[o:o + size].reshape(sh))
        o += size
    return out


def _small_grads(v512, v1024, mlp_vec, ple_vec, dwp_bd, dwg_bd, pool_b, gate_a_b, gate_x_b):
    return [
        v1024[1:2],
        _diag_blocks(dwp_bd, N_POOL_GROUPS)[None],
        v512[1:2].reshape(pool_b.shape),
        v512[0:1],
        v512[2:3],
        _diag_blocks(dwg_bd[:, :512], LRU_HEADS)[None],
        v1024[0:1, :512].reshape(gate_a_b.shape),
        _diag_blocks(dwg_bd[:, 512:], LRU_HEADS)[None],
        v1024[0:1, 512:].reshape(gate_x_b.shape),
        v512[3:4],
        mlp_vec[0:1],
        ple_vec[1:2],
        ple_vec[0:1],
        ple_vec[2:3].reshape(-1),
        v512[4:8][None],
        ple_vec[3:4, 0:1].reshape(1),
    ]


def kernel(x, p, norm_mix_g, w_in, pool_w, pool_b, pool_scale, conv_w, conv_b, gate_a_w, gate_a_b, gate_x_w, gate_x_b, lru_L, w_out, norm_mlp_g, w_up, w_down, norm_ple_g, w_ple_gate, b_ple_gate, w_ple_proj, norm_final_g, loss_target, m_norm_mix_g, m_w_in, m_pool_w, m_pool_b, m_pool_scale, m_conv_w, m_conv_b, m_gate_a_w, m_gate_a_b, m_gate_x_w, m_gate_x_b, m_lru_L, m_w_out, m_norm_mlp_g, m_w_up, m_w_down, m_norm_ple_g, m_w_ple_gate, m_b_ple_gate, m_w_ple_proj, m_norm_final_g, v_norm_mix_g, v_w_in, v_pool_w, v_pool_b, v_pool_scale, v_conv_w, v_conv_b, v_gate_a_w, v_gate_a_b, v_gate_x_w, v_gate_x_b, v_lru_L, v_w_out, v_norm_mlp_g, v_w_up, v_w_down, v_norm_ple_g, v_w_ple_gate, v_b_ple_gate, v_w_ple_proj, v_norm_final_g):
    t_len, d = x.shape[1], x.shape[2]
    tbs = {k: min(v, t_len) for k, v in TIME_BLOCKS.items()}
    me = _my_index()

    win_g, wout_g, convw_g = _gather("gather_mixer_weights", [w_in[0].astype(BF16), w_out[0].astype(BF16), conv_w[0]])
    w_in_f = jnp.transpose(win_g, (1, 0, 2)).reshape(d, -1)
    conv_w_f = jnp.transpose(convw_g, (1, 0, 2)).reshape(convw_g.shape[1], -1)
    wp_bd = _block_diag(pool_w[0]).astype(BF16)
    wg_bd = jnp.concatenate([_block_diag(gate_a_w[0]), _block_diag(gate_x_w[0])], axis=1).astype(BF16)
    gate_b2 = jnp.concatenate([gate_a_b.reshape(1, -1), gate_x_b.reshape(1, -1)], axis=1)
    mixer_small = (norm_mix_g, w_in_f, wp_bd, pool_b.reshape(1, -1), pool_scale, conv_w_f, conv_b, wg_bd, gate_b2, lru_L,
                   wout_g.reshape(-1, d))

    x2 = x[0]
    later = [w_up[0].astype(BF16), w_down[0].astype(BF16), w_ple_gate[0].astype(BF16), w_ple_proj[0].astype(BF16)]
    h1, z1, proj, hs, cat, wup_g, wdn_g, wgate_g, wproj_g = _mix_fwd(x2, *mixer_small, later, GATHER_FORWARD_AT,
                                                                     tbs['mix_fwd'])
    w_down_f = wdn_g.reshape(-1, d)
    w_proj_f = jnp.transpose(wproj_g, (1, 0, 2)).reshape(wproj_g.shape[1], -1)
    h2, z2, up = _mlp_fwd(h1, norm_mlp_g, wup_g, w_down_f, tbs['mlp_fwd'])
    dh2, ple_vec, dw_gate, dw_proj = _ple(h2, p[0, 0], loss_target[0], norm_ple_g, wgate_g.reshape(-1, d), b_ple_gate,
                                          w_proj_f, norm_final_g.reshape(1, -1), tbs['ple'])
    dz2 = None
    dw_up_parts, dw_down_parts = [], []
    for part in range(MLP_BWD_SPLIT):
        res = _mlp_bwd_part(part, MLP_BWD_SPLIT, dh2, z2, up, wup_g, w_down_f, dz2, h1, norm_mlp_g, tbs['mlp_bwd'])
        dz2 = res[0]
        dw_up_parts.append(res[-2])
        dw_down_parts.append(res[-1])
    dh1, mlp_vec = res[0], res[1]
    n_proj = w_ple_proj.shape[2]
    early = [dw_up_parts, dw_down_parts, [dw_gate.reshape(N_DEV, -1, d)],
             [jnp.transpose(dw_proj.reshape(-1, N_DEV, n_proj), (1, 0, 2))]]
    (dx, v512, v1024, dw_in, dw_out, dwp_bd, dwg_bd,
     recv_up, recv_down, recv_gate, recv_proj) = _mix_bwd(dh1, x2, z1, proj, hs, cat, *mixer_small, early, tbs['mix_bwd'])

    small_grads = _small_grads(v512, v1024, mlp_vec, ple_vec, dwp_bd, dwg_bd, pool_b, gate_a_b, gate_x_b)
    n_in = w_in.shape[2]
    small_sum, (recv_in, recv_out) = _final_exchange(
        "exchange_last_grads", _pack(small_grads, row_multiple=8 * N_DEV),
        [jnp.transpose(dw_in.reshape(d, N_DEV, n_in), (1, 0, 2)), dw_out.reshape(N_DEV, -1, d)])
    received = [recv_in, recv_out, recv_up, recv_down, recv_gate, recv_proj]
    small_w = [norm_mix_g, pool_w, pool_b, pool_scale, conv_b, gate_a_w, gate_a_b, gate_x_w, gate_x_b, lru_L,
               norm_mlp_g, norm_ple_g, b_ple_gate, norm_final_g]
    small_m = [m_norm_mix_g, m_pool_w, m_pool_b, m_pool_scale, m_conv_b, m_gate_a_w, m_gate_a_b, m_gate_x_w, m_gate_x_b,
               m_lru_L, m_norm_mlp_g, m_norm_ple_g, m_b_ple_gate, m_norm_final_g]
    small_v = [v_norm_mix_g, v_pool_w, v_pool_b, v_pool_scale, v_conv_b, v_gate_a_w, v_gate_a_b, v_gate_x_w, v_gate_x_b,
               v_lru_L, v_norm_mlp_g, v_norm_ple_g, v_b_ple_gate, v_norm_final_g]
    small_shapes = [a.shape for a in small_w]
    conv_full_shape = (1,) + conv_w_f.shape

    shard_w = [w_in[0], w_out[0], w_up[0], w_down[0], w_ple_gate[0], w_ple_proj[0]]
    shard_m = [m_w_in[0], m_w_out[0], m_w_up[0], m_w_down[0], m_w_ple_gate[0], m_w_ple_proj[0]]
    shard_v = [v_w_in[0], v_w_out[0], v_w_up[0], v_w_down[0], v_w_ple_gate[0], v_w_ple_proj[0]]
    names = ["w_in", "w_out", "w_up", "w_down", "w_ple_gate", "w_ple_proj"]
    big_res = {}
    for nm, parts, w_s, m_s, v_s in zip(names, received, shard_w, shard_m, shard_v):
        big_res[nm] = [r[None] for r in _adamw("adamw_" + nm, parts, w_s, m_s, v_s, 128)]

    summed = _unpack(small_sum, small_shapes + [conv_full_shape, (1,)])
    loss = summed[-1][0]
    conv_g = lax.dynamic_slice_in_dim(summed[-2], me * conv_w.shape[2], conv_w.shape[2], axis=2)
    sg = summed[:-2] + [conv_g]
    sw, sm, sv = small_w + [conv_w], small_m + [m_conv_w], small_v + [v_conv_w]
    shapes2 = small_shapes + [conv_w.shape]
    res = _adamw("adamw_small", _pack(sg)[None], _pack(sw), _pack(sm), _pack(sv), 1024)
    sres = [_unpack(r, shapes2) for r in res]
    small_names = ["norm_mix_g", "pool_w", "pool_b", "pool_scale", "conv_b", "gate_a_w", "gate_a_b", "gate_x_w",
                   "gate_x_b", "lru_L", "norm_mlp_g", "norm_ple_g", "b_ple_gate", "norm_final_g", "conv_w"]
    order = ["norm_mix_g", "w_in", "pool_w", "pool_b", "pool_scale", "conv_w", "conv_b", "gate_a_w", "gate_a_b",
             "gate_x_w", "gate_x_b", "lru_L", "w_out", "norm_mlp_g", "w_up", "w_down", "norm_ple_g", "w_ple_gate",
             "b_ple_gate", "w_ple_proj", "norm_final_g"]
    outs = [loss, dx[None]]
    for kind in range(4):
        for nm in order:
            if nm in big_res:
                outs.append(big_res[nm][kind])
            else:
                outs.append(sres[kind][small_names.index(nm)])
    return tuple(outs)
```

```python
import functools

import jax
import jax.numpy as jnp
from jax import lax
from jax.experimental import pallas as pl
from jax.experimental.pallas import tpu as pltpu

F32 = jnp.float32
BF16 = jnp.bfloat16
MESH = pl.DeviceIdType.MESH

N_DEV = 8
RMS_EPS = 1e-6
LRU_C = 8.0
POOL_WINDOWS = (2, 4, 8, 16)
N_POOL_GROUPS = 4
LRU_HEADS = 8
HALO = 16
SUB = 8
GELU_C0 = 0.7978845608028654
GELU_C1 = 0.044715

ADAM_LR = 0.001
ADAM_B1 = 0.9
ADAM_B2 = 0.999
ADAM_EPS = 1e-08
ADAM_WD = 0.01
ADAM_STEP = 10

VMEM_LIMIT = 60 * 1024 * 1024
TIME_BLOCKS = dict(mix_fwd=512, mlp_fwd=512, ple=512, mlp_bwd=512, wgrad_out=1024, mix_bwd=512, in_bwd=512)
MLP_BWD_SPLIT = 2
GATHER_FORWARD_AT = (0.5, 0.875, 1.0, 1.0)


def _params(n_arbitrary=1):
    return pltpu.CompilerParams(dimension_semantics=("arbitrary",) * n_arbitrary, vmem_limit_bytes=VMEM_LIMIT)


def _dot(a, b):
    return jnp.dot(a, b, preferred_element_type=F32)


def _dot_nt(a, b):
    return lax.dot_general(a, b, (((1,), (1,)), ((), ())), preferred_element_type=F32)


def _dot_tn(a, b):
    return lax.dot_general(a, b, (((0,), (0,)), ((), ())), preferred_element_type=F32)


def _rms_fwd(x, g):
    r = lax.rsqrt(jnp.mean(x * x, axis=-1, keepdims=True) + RMS_EPS)
    xh = x * r
    return xh * g, xh, r


def _rms_bwd(xh, r, g, dz):
    dxh = dz * g
    return r * (dxh - xh * jnp.mean(dxh * xh, axis=-1, keepdims=True))


def _colsum(a):
    return jnp.sum(a, axis=0, keepdims=True)


def _sigmoid(a):
    return 1.0 / (1.0 + jnp.exp(-a))


def _gelu_parts(u):
    u2 = u * u
    th = jnp.tanh(GELU_C0 * (u + GELU_C1 * u * u2))
    gel = 0.5 * u * (1.0 + th)
    dgel = 0.5 * (1.0 + th) + 0.5 * u * (1.0 - th * th) * (GELU_C0 * (1.0 + 3.0 * GELU_C1 * u2))
    return gel, dgel


def _my_index():
    return 4 * lax.axis_index("x") + 2 * lax.axis_index("y") + lax.axis_index("c")


def _all_to_all(srcs_of, dsts, send_sems, recv_sems, local_sems):
    n = len(dsts)
    me = _my_index()

    def remote(t, s):
        return pltpu.make_async_remote_copy(
            src_ref=srcs_of[t](s), dst_ref=dsts[t].at[me], send_sem=send_sems.at[t, s], recv_sem=recv_sems.at[t, me],
            device_id=(s // 4, (s // 2) % 2, s % 2), device_id_type=MESH)

    def arrival(t, s):
        return pltpu.make_async_remote_copy(
            src_ref=srcs_of[t](s), dst_ref=dsts[t].at[s], send_sem=send_sems.at[t, s], recv_sem=recv_sems.at[t, s],
            device_id=(s // 4, (s // 2) % 2, s % 2), device_id_type=MESH)

    def local(t, s):
        return pltpu.make_async_copy(srcs_of[t](s), dsts[t].at[s], local_sems.at[t])

    def start():
        for s in range(N_DEV):
            @pl.when(s == me)
            def _():
                for t in range(n):
                    local(t, s).start()

            @pl.when(s != me)
            def _():
                for t in range(n):
                    remote(t, s).start()

    def wait():
        for s in range(N_DEV):
            @pl.when(s == me)
            def _():
                for t in range(n):
                    local(t, s).wait()

            @pl.when(s != me)
            def _():
                for t in range(n):
                    remote(t, s).wait_send()
                    arrival(t, s).wait_recv()

    return start, wait


N_GATHER_COPIES = 7


def _two_level_gather(srcs, dsts, send_sems, recv_sems, local_sems):
    n = len(dsts)
    x, y, c = lax.axis_index("x"), lax.axis_index("y"), lax.axis_index("c")
    me, sibling = (x, y, c), (x, y, 1 - c)
    chips = [(1 - x, y), (x, 1 - y), (1 - x, 1 - y)]

    def slot(dev):
        return 4 * dev[0] + 2 * dev[1] + dev[2]

    def copy(t, k, block, to, src=None):
        return pltpu.make_async_remote_copy(
            src_ref=dsts[t].at[slot(block)] if src is None else src, dst_ref=dsts[t].at[slot(block)],
            send_sem=send_sems.at[t, k], recv_sem=recv_sems.at[t, k], device_id=to, device_id_type=MESH)

    def local(t):
        return pltpu.make_async_copy(srcs[t], dsts[t].at[slot(me)], local_sems.at[t])

    def start():
        for t in range(n):
            local(t).start()
            for j, chip in enumerate(chips):
                copy(t, 1 + j, me, (*chip, c), src=srcs[t]).start()
            copy(t, 0, me, sibling, src=srcs[t]).start()

    def forward(t):
        for j, chip in enumerate(chips):
            copy(t, 1 + j, (*chip, c), me).wait_recv()
            copy(t, 4 + j, (*chip, c), sibling).start()

    def finish():
        for t in range(n):
            copy(t, 0, sibling, me).wait_recv()
            for j, chip in enumerate(chips):
                copy(t, 4 + j, (*chip, 1 - c), me).wait_recv()
            copy(t, 0, me, sibling, src=srcs[t]).wait_send()
            for j, chip in enumerate(chips):
                copy(t, 1 + j, me, (*chip, c), src=srcs[t]).wait_send()
                copy(t, 4 + j, (*chip, c), sibling).wait_send()
            local(t).wait()

    return start, forward, finish


def _gather_scratch(n):
    return [pltpu.SemaphoreType.DMA((n, N_GATHER_COPIES)), pltpu.SemaphoreType.DMA((n, N_GATHER_COPIES)),
            pltpu.SemaphoreType.DMA((n,))]


def _gather(name, srcs):
    n = len(srcs)

    def body(*refs):
        start, forward, finish = _two_level_gather(refs[:n], refs[n:2 * n], *refs[2 * n:])
        start()
        for t in range(n):
            forward(t)
        finish()

    any_spec = pl.BlockSpec(memory_space=pl.ANY)
    return pl.pallas_call(
        body, name=name, in_specs=[any_spec] * n, out_specs=[any_spec] * n,
        out_shape=[jax.ShapeDtypeStruct((N_DEV,) + a.shape, a.dtype) for a in srcs], scratch_shapes=_gather_scratch(n),
    )(*srcs)


def _exchange_scratch(n):
    return [pltpu.SemaphoreType.DMA((n, N_DEV)), pltpu.SemaphoreType.DMA((n, N_DEV)), pltpu.SemaphoreType.DMA((n,))]


def _const_spec(shape):
    nd = len(shape)
    return pl.BlockSpec(shape, lambda i: (0,) * nd, pipeline_mode=pl.Buffered(1))


def _pool_windows(up_ext, n, forward):
    sh = (lambda k: k) if forward else (lambda k: n - k)
    s2 = up_ext + pltpu.roll(up_ext, sh(1), 0)
    t4 = s2[:, 128:]
    s4 = t4 + pltpu.roll(t4, sh(2), 0)
    t8 = s4[:, 128:]
    s8 = t8 + pltpu.roll(t8, sh(4), 0)
    t16 = s8[:, 128:]
    s16 = t16 + pltpu.roll(t16, sh(8), 0)
    return jnp.concatenate([s2[:, :128], s4[:, :128], s8[:, :128], s16], axis=1)


def _inv_count(t0, tb):
    t = (t0 + lax.broadcasted_iota(jnp.int32, (tb, 1), 0) + 1).astype(F32)
    cols = [jnp.broadcast_to(1.0 / jnp.minimum(t, float(w)), (tb, 128)) for w in POOL_WINDOWS]
    return jnp.concatenate(cols, axis=1)


def _lru_gates(xb, wg, gb, c_l, first_row):
    gp = _dot(xb.astype(BF16), wg) + gb
    r = _sigmoid(gp[:, :512])
    ig = _sigmoid(gp[:, 512:])
    la = c_l * r
    a = jnp.exp(la)
    a2 = a * a
    m2 = -jnp.tanh(la) * (a2 + 1.0)
    mult = jnp.where(first_row, 1.0, jnp.sqrt(m2))
    return r, ig, a, a2, m2, mult


def _log_sigmoid(v):
    return -(jnp.maximum(-v, 0.0) + jnp.log1p(jnp.exp(-jnp.abs(v))))


def _conv_fwd(ul_ext, cw, cb):
    return (cb + cw[3:4, :] * ul_ext + cw[2:3, :] * pltpu.roll(ul_ext, 1, 0)
            + cw[1:2, :] * pltpu.roll(ul_ext, 2, 0) + cw[0:1, :] * pltpu.roll(ul_ext, 3, 0))


def _mix_fwd(x, g_mix, w_in, wp_bd, pool_b, pool_scale, conv_w, conv_b, wg_bd, gate_b, lru_l, w_out, gather_srcs,
             forward_at, tb):
    t_len, d = x.shape
    nb = t_len // tb
    n_g = len(gather_srcs)
    forward_step = [min(nb - 1, int(f * nb)) for f in forward_at]

    def body(*refs):
        x_ref, g_ref, win_ref, wp_ref, pb_ref, ps_ref, cw_ref, cb_ref, wg_ref, gb_ref, l_ref, wout_ref = refs[:12]
        gsrc = refs[12:12 + n_g]
        h1_ref, z1_ref, proj_ref, hs_ref, cat_ref = refs[12 + n_g:17 + n_g]
        gdst = refs[17 + n_g:17 + 2 * n_g]
        ext_ref, a_ref, b_ref, hc_ref, send_sems, recv_sems, local_sems = refs[17 + 2 * n_g:]
        i = pl.program_id(0)
        start_gather, forward_gather, finish_gather = _two_level_gather(gsrc, gdst, send_sems, recv_sems, local_sems)

        @pl.when(i == 0)
        def _():
            start_gather()
            ext_ref[0:HALO, :] = jnp.zeros((HALO, 1024), F32)
            hc_ref[...] = jnp.zeros_like(hc_ref)

        xv = x_ref[...]
        z, _, _ = _rms_fwd(xv, g_ref[...])
        zb = z.astype(BF16)
        z1_ref[...] = zb
        proj = _dot(zb, win_ref[...])
        proj_ref[...] = proj
        ext_ref[HALO:, :] = proj[:, 0:1024]
        ug = proj[:, 1024:1536]
        n = tb + HALO
        up_ext = ext_ref[:, 0:512]
        win = _pool_windows(up_ext, n, True)[HALO:]
        dpool = win * _inv_count(i * tb, tb) - proj[:, 0:512]
        q = _dot(dpool.astype(BF16), wp_ref[...]) + pb_ref[...]
        y_pool = q * ps_ref[...]
        xb = _conv_fwd(ext_ref[:, 512:1024], cw_ref[...], cb_ref[...])[HALO:]
        first_row = (i * tb + lax.broadcasted_iota(jnp.int32, (tb, 1), 0)) == 0
        c_l = LRU_C * _log_sigmoid(l_ref[...])
        _, ig, a, _, _, mult = _lru_gates(xb, wg_ref[...], gb_ref[...], c_l, first_row)
        a_ref[...] = a
        b_ref[...] = mult * (ig * xb)
        row = lax.broadcasted_iota(jnp.int32, (SUB, 512), 0)

        def group(j, hprev):
            o = pl.multiple_of(j * SUB, SUB)
            a8 = a_ref[pl.ds(o, SUB), :]
            b8 = b_ref[pl.ds(o, SUB), :]
            for sh in (1, 2, 4):
                ash = jnp.where(row >= sh, pltpu.roll(a8, sh, 0), 1.0)
                bsh = jnp.where(row >= sh, pltpu.roll(b8, sh, 0), 0.0)
                b8 = a8 * bsh + b8
                a8 = a8 * ash
            h8 = a8 * hprev + b8
            hs_ref[pl.ds(o, SUB), :] = h8
            return jnp.broadcast_to(h8[SUB - 1:SUB, :], (SUB, 512))

        hc_ref[...] = lax.fori_loop(0, tb // SUB, group, hc_ref[...])
        gel, _ = _gelu_parts(ug)
        y_lru = hs_ref[...] * gel
        catb = jnp.concatenate([y_pool, y_lru], axis=1).astype(BF16)
        cat_ref[...] = catb
        h1_ref[...] = xv + _dot(catb, wout_ref[...])
        ext_ref[0:HALO, :] = ext_ref[tb:tb + HALO, :]

        for t in range(n_g):
            @pl.when(i == forward_step[t])
            def _():
                forward_gather(t)

        @pl.when(i == nb - 1)
        def _():
            finish_gather()

    row_spec = lambda w: pl.BlockSpec((tb, w), lambda i: (i, 0))
    any_spec = pl.BlockSpec(memory_space=pl.ANY)
    smalls = [g_mix, w_in, wp_bd, pool_b, pool_scale, conv_w, conv_b, wg_bd, gate_b, lru_l, w_out]
    return pl.pallas_call(
        body, name="mix_fwd", grid=(nb,),
        in_specs=[row_spec(d)] + [_const_spec(s.shape) for s in smalls] + [any_spec] * n_g,
        out_specs=[row_spec(d), row_spec(d), row_spec(1536), row_spec(512), row_spec(1024)] + [any_spec] * n_g,
        out_shape=[jax.ShapeDtypeStruct((t_len, d), F32), jax.ShapeDtypeStruct((t_len, d), BF16),
                   jax.ShapeDtypeStruct((t_len, 1536), F32), jax.ShapeDtypeStruct((t_len, 512), F32),
                   jax.ShapeDtypeStruct((t_len, 1024), BF16)]
        + [jax.ShapeDtypeStruct((N_DEV,) + s.shape, s.dtype) for s in gather_srcs],
        scratch_shapes=[pltpu.VMEM((tb + HALO, 1024), F32), pltpu.VMEM((tb, 512), F32), pltpu.VMEM((tb, 512), F32),
                        pltpu.VMEM((SUB, 512), F32)] + _gather_scratch(n_g),
        compiler_params=_params(),
    )(x, *smalls, *gather_srcs)


def _mlp_fwd(h1, g_mlp, w_up, w_down, tb):
    t_len, d = h1.shape
    nb = t_len // tb
    n_chunk, _, fc = w_up.shape

    def body(h1_ref, g_ref, wup_ref, wdn_ref, h2_ref, z2_ref, up_ref):
        xv = h1_ref[...]
        z, _, _ = _rms_fwd(xv, g_ref[...])
        zb = z.astype(BF16)
        z2_ref[...] = zb
        acc = xv
        for c in range(n_chunk):
            u = _dot(zb, wup_ref[c])
            up_ref[:, c * fc:(c + 1) * fc] = u.astype(BF16)
            act = jnp.square(jnp.maximum(u, 0.0)).astype(BF16)
            acc = acc + _dot(act, wdn_ref[c * fc:(c + 1) * fc, :])
        h2_ref[...] = acc

    row_spec = lambda w: pl.BlockSpec((tb, w), lambda i: (i, 0))
    return pl.pallas_call(
        body, name="mlp_fwd", grid=(nb,),
        in_specs=[row_spec(d), _const_spec(g_mlp.shape), _const_spec(w_up.shape), _const_spec(w_down.shape)],
        out_specs=[row_spec(d), row_spec(d), row_spec(n_chunk * fc)],
        out_shape=[jax.ShapeDtypeStruct((t_len, d), F32), jax.ShapeDtypeStruct((t_len, d), BF16),
                   jax.ShapeDtypeStruct((t_len, n_chunk * fc), BF16)],
        compiler_params=_params(),
    )(h1, g_mlp, w_up, w_down)


def _ple(h2, p, target, g_ple, w_gate, b_gate, w_proj, g_final, tb):
    t_len, d = h2.shape
    nb = t_len // tb
    pd = p.shape[1]

    def body(h2_ref, p_ref, tgt_ref, g_ref, wg_ref, bg_ref, wp_ref, gf_ref,
             dh2_ref, vec_ref, dwg_out, dwp_out, dwg_acc, dwp_acc, dwg_stage, dwp_stage):
        i = pl.program_id(0)

        @pl.when(i == 0)
        def _():
            vec_ref[...] = jnp.zeros_like(vec_ref)
            dwg_acc[...] = jnp.zeros_like(dwg_acc)
            dwp_acc[...] = jnp.zeros_like(dwp_acc)

        h2 = h2_ref[...]
        g2 = g_ref[...]
        z3, xh2, r2 = _rms_fwd(h2, g2)
        z3b = z3.astype(BF16)
        gate = _sigmoid(_dot(z3b, wg_ref[...]) + bg_ref[...])
        pb = p_ref[...].astype(BF16)
        pp = _dot(pb, wp_ref[...])
        h3 = h2 + gate * pp
        gf = gf_ref[...]
        y, xh3, r3 = _rms_fwd(h3, gf)
        err = y - tgt_ref[...]
        loss_rows = jnp.mean(err * err, axis=-1, keepdims=True)
        dy = err * (1.0 / d)
        dh3 = _rms_bwd(xh3, r3, gf, dy)
        dgl = (dh3 * pp) * (gate * (1.0 - gate))
        dpp = dh3 * gate
        dglb = dgl.astype(BF16)
        dwg_acc[...] += _dot_tn(z3b, dglb)
        dwp_acc[...] += _dot_tn(pb, dpp.astype(BF16))
        dz3 = _dot_nt(dglb, wg_ref[...])
        dh2_ref[...] = dh3 + _rms_bwd(xh2, r2, g2, dz3)
        vec_ref[0:1, :] += _colsum(dgl)
        vec_ref[1:2, :] += _colsum(dz3 * xh2)
        vec_ref[2:3, :] += _colsum(dy * xh3)
        vec_ref[3:4, :] += 0.5 * jnp.sum(loss_rows)

        @pl.when(i == nb - 1)
        def _():
            dwg_stage[...] = dwg_acc[...].astype(BF16)
            dwp_stage[...] = dwp_acc[...].astype(BF16)
            pltpu.sync_copy(dwg_stage, dwg_out)
            pltpu.sync_copy(dwp_stage, dwp_out)

    row_spec = lambda w: pl.BlockSpec((tb, w), lambda i: (i, 0))
    any_spec = pl.BlockSpec(memory_space=pl.ANY)
    smalls = [g_ple, w_gate, b_gate, w_proj, g_final]
    return pl.pallas_call(
        body, name="ple_fwd_bwd", grid=(nb,),
        in_specs=[row_spec(d), row_spec(pd), row_spec(d)] + [_const_spec(s.shape) for s in smalls],
        out_specs=[row_spec(d), pl.BlockSpec((8, d), lambda i: (0, 0)), any_spec, any_spec],
        out_shape=[jax.ShapeDtypeStruct((t_len, d), F32), jax.ShapeDtypeStruct((8, d), F32),
                   jax.ShapeDtypeStruct(w_gate.shape, BF16), jax.ShapeDtypeStruct(w_proj.shape, BF16)],
        scratch_shapes=[pltpu.VMEM(w_gate.shape, F32), pltpu.VMEM(w_proj.shape, F32), pltpu.VMEM(w_gate.shape, BF16),
                        pltpu.VMEM(w_proj.shape, BF16)],
        compiler_params=_params(),
    )(h2, p, target, *smalls)


def _mlp_bwd_part(part, n_part, dh2, z2, up, w_up, w_down, dz2_prev, h1, g_mlp, tb):
    t_len, d = dh2.shape
    nb = t_len // tb
    n_chunk_all, _, fc = w_up.shape
    n_chunk = n_chunk_all // n_part
    first, last = part == 0, part == n_part - 1

    def body(*refs):
        refs = list(refs)
        dh2_ref, z2_ref, up_ref, wup_ref, wdn_ref = refs[:5]
        del refs[:5]
        dzp_ref = None if first else refs.pop(0)
        h1_ref, g_ref = (refs.pop(0), refs.pop(0)) if last else (None, None)
        out_ref = refs.pop(0)
        vec_ref = refs.pop(0) if last else None
        dwup_out, dwdn_out, dwup_acc, dwdn_acc, up_stage, dn_stage = refs
        i = pl.program_id(0)

        @pl.when(i == 0)
        def _():
            dwup_acc[...] = jnp.zeros_like(dwup_acc)
            dwdn_acc[...] = jnp.zeros_like(dwdn_acc)
            if last:
                vec_ref[...] = jnp.zeros_like(vec_ref)

        dh2 = dh2_ref[...]
        dh2b = dh2.astype(BF16)
        z2b = z2_ref[...]
        dz2 = jnp.zeros((tb, d), F32) if first else dzp_ref[...]
        for c in range(n_chunk):
            u = up_ref[:, c * fc:(c + 1) * fc].astype(F32)
            ur = jnp.maximum(u, 0.0)
            dact = _dot_nt(dh2b, wdn_ref[c * fc:(c + 1) * fc, :])
            dupb = (dact * (2.0 * ur)).astype(BF16)
            dwdn_acc[c * fc:(c + 1) * fc, :] += _dot_tn((ur * ur).astype(BF16), dh2b)
            dwup_acc[c] += _dot_tn(z2b, dupb)
            dz2 = dz2 + _dot_nt(dupb, wup_ref[c])
        if last:
            g = g_ref[...]
            _, xh, r = _rms_fwd(h1_ref[...], g)
            out_ref[...] = dh2 + _rms_bwd(xh, r, g, dz2)
            vec_ref[0:1, :] += _colsum(dz2 * xh)
        else:
            out_ref[...] = dz2

        @pl.when(i == nb - 1)
        def _():
            for c in range(n_chunk):
                up_stage[...] = dwup_acc[c].astype(BF16)
                dn_stage[...] = dwdn_acc[c * fc:(c + 1) * fc, :].astype(BF16)
                pltpu.sync_copy(up_stage, dwup_out.at[c])
                pltpu.sync_copy(dn_stage, dwdn_out.at[c])

    row_spec = lambda w: pl.BlockSpec((tb, w), lambda i: (i, 0))
    any_spec = pl.BlockSpec(memory_space=pl.ANY)
    args = [dh2, z2, up, w_up, w_down]
    in_specs = [row_spec(d), row_spec(d), pl.BlockSpec((tb, n_chunk * fc), lambda i: (i, part)),
                pl.BlockSpec((n_chunk, d, fc), lambda i: (part, 0, 0), pipeline_mode=pl.Buffered(1)),
                pl.BlockSpec((n_chunk * fc, d), lambda i: (part, 0), pipeline_mode=pl.Buffered(1))]
    if not first:
        args.append(dz2_prev)
        in_specs.append(row_spec(d))
    if last:
        args += [h1, g_mlp]
        in_specs += [row_spec(d), _const_spec(g_mlp.shape)]
    out_specs = [row_spec(d)]
    out_shape = [jax.ShapeDtypeStruct((t_len, d), F32)]
    if last:
        out_specs.append(pl.BlockSpec((8, d), lambda i: (0, 0)))
        out_shape.append(jax.ShapeDtypeStruct((8, d), F32))
    out_specs += [any_spec, any_spec]
    out_shape += [jax.ShapeDtypeStruct((n_chunk, d, fc), BF16), jax.ShapeDtypeStruct((n_chunk, fc, d), BF16)]
    return pl.pallas_call(
        body, name=f"mlp_bwd_{part}", grid=(nb,), in_specs=in_specs, out_specs=out_specs, out_shape=out_shape,
        scratch_shapes=[pltpu.VMEM((n_chunk, d, fc), F32), pltpu.VMEM((n_chunk * fc, d), F32),
                        pltpu.VMEM((d, fc), BF16), pltpu.VMEM((fc, d), BF16)],
        compiler_params=_params(),
    )(*args)


def _mix_bwd(dh1, proj, hs, wp_bd, pool_b, pool_scale, conv_w, conv_b, wg_bd, gate_b, lru_l, w_out, scatter_parts, tb):
    t_len, d = dh1.shape
    nb = t_len // tb
    n_s = len(scatter_parts)
    flat_parts = [a for parts in scatter_parts for a in parts]

    def body(*refs):
        refs = list(refs)
        (dh1_ref, proj_ref, projh_ref, hs_ref, hsh_ref,
         wp_ref, pb_ref, ps_ref, cw_ref, cb_ref, wg_ref, gb_ref, l_ref, wout_ref) = refs[:14]
        del refs[:14]
        part_refs = []
        for parts in scatter_parts:
            part_refs.append(refs[:len(parts)])
            del refs[:len(parts)]
        dproj_ref, v512_ref, v1024_ref, dwp_out, dwg_out = refs[:5]
        recv = refs[5:5 + n_s]
        (dwp_acc, dwg_acc, ext_ref, a_ref, b_ref, gs_ref, ehead_ref, dxbhead_ref, ahead_ref, gc_ref,
         send_sems, recv_sems, local_sems) = refs[5 + n_s:]
        i = pl.program_id(0)
        tbk = nb - 1 - i

        def block_of(prefs):
            per = N_DEV // len(prefs)
            return lambda s: prefs[s // per].at[s % per]

        start_scatter, wait_scatter = _all_to_all([block_of(pr) for pr in part_refs], recv, send_sems, recv_sems,
                                                  local_sems)

        @pl.when(i == 0)
        def _():
            start_scatter()
            for ref in (v512_ref, v1024_ref, dwp_acc, dwg_acc, ehead_ref, dxbhead_ref, ahead_ref, gc_ref):
                ref[...] = jnp.zeros_like(ref)

        dcat = _dot_nt(dh1_ref[...].astype(BF16), wout_ref[...])

        proj = proj_ref[...]
        has_prev = (tbk > 0).astype(F32)
        ext_ref[0:HALO, :] = projh_ref[:, 0:1024] * has_prev
        ext_ref[HALO:, :] = proj[:, 0:1024]
        ug = proj[:, 1024:1536]
        n = tb + HALO
        inv = _inv_count(tbk * tb, tb)

        up_ext = ext_ref[:, 0:512]
        win = _pool_windows(up_ext, n, True)[HALO:]
        dpool = win * inv - proj[:, 0:512]
        dpoolb = dpool.astype(BF16)
        q = _dot(dpoolb, wp_ref[...]) + pb_ref[...]
        dyp = dcat[:, 0:512]
        dq = dyp * ps_ref[...]
        dqb = dq.astype(BF16)
        v512_ref[0:1, :] += _colsum(dyp * q)
        v512_ref[1:2, :] += _colsum(dq)
        dwp_acc[...] += _dot_tn(dpoolb, dqb)
        dd = _dot_nt(dqb, wp_ref[...])
        e = dd * inv
        e_ext = jnp.concatenate([e, ehead_ref[...]], axis=0)
        du_pool = _pool_windows(e_ext, n, False)[0:tb] - dd
        ehead_ref[...] = e[0:HALO]

        gel, dgel = _gelu_parts(ug)
        hsv = hs_ref[...]
        dcl = dcat[:, 512:1024]
        dhs = dcl * gel
        dug = dcl * hsv * dgel
        ul_ext = ext_ref[:, 512:1024]
        cw = cw_ref[...]
        xb = _conv_fwd(ul_ext, cw, cb_ref[...])[HALO:]
        first_row = (tbk * tb + lax.broadcasted_iota(jnp.int32, (tb, 1), 0)) == 0
        c_l = LRU_C * _log_sigmoid(l_ref[...])
        r, ig, a, a2, m2, mult = _lru_gates(xb, wg_ref[...], gb_ref[...], c_l, first_row)
        a_ext = jnp.concatenate([a, ahead_ref[...]], axis=0)
        a_ref[...] = pltpu.roll(a_ext, tb + SUB - 1, 0)[0:tb]
        b_ref[...] = dhs
        ahead_ref[...] = a[0:SUB]
        row = lax.broadcasted_iota(jnp.int32, (SUB, 512), 0)

        def group(jj, gnext):
            o = pl.multiple_of((tb // SUB - 1 - jj) * SUB, SUB)
            a8 = a_ref[pl.ds(o, SUB), :]
            b8 = b_ref[pl.ds(o, SUB), :]
            for sh in (1, 2, 4):
                ash = jnp.where(row < SUB - sh, pltpu.roll(a8, SUB - sh, 0), 1.0)
                bsh = jnp.where(row < SUB - sh, pltpu.roll(b8, SUB - sh, 0), 0.0)
                b8 = a8 * bsh + b8
                a8 = a8 * ash
            g8 = a8 * gnext + b8
            gs_ref[pl.ds(o, SUB), :] = g8
            return jnp.broadcast_to(g8[0:1, :], (SUB, 512))

        gc_ref[...] = lax.fori_loop(0, tb // SUB, group, gc_ref[...])
        gsum = gs_ref[...]
        hs_ext = jnp.concatenate([hsh_ref[...] * has_prev, hsv], axis=0)
        hprev = pltpu.roll(hs_ext, 1, 0)[SUB:]
        da = gsum * hprev
        dmult = jnp.where(first_row, 0.0, gsum * (ig * xb))
        di = gsum * mult * xb
        dxb = gsum * mult * ig
        dla = da * a - dmult * a2 * lax.rsqrt(m2)
        dr = dla * c_l
        v512_ref[3:4, :] += _colsum(dla * r)
        dgp = jnp.concatenate([dr * r * (1.0 - r), di * ig * (1.0 - ig)], axis=1)
        dgpb = dgp.astype(BF16)
        v1024_ref[0:1, :] += _colsum(dgp)
        dwg_acc[...] += _dot_tn(xb.astype(BF16), dgpb)
        dxb = dxb + _dot_nt(dgpb, wg_ref[...])
        n8 = tb + SUB
        dxb_ext = jnp.concatenate([dxb, dxbhead_ref[...]], axis=0)
        du_lru = (cw[3:4, :] * dxb + cw[2:3, :] * pltpu.roll(dxb_ext, n8 - 1, 0)[0:tb]
                  + cw[1:2, :] * pltpu.roll(dxb_ext, n8 - 2, 0)[0:tb] + cw[0:1, :] * pltpu.roll(dxb_ext, n8 - 3, 0)[0:tb])
        dxbhead_ref[...] = dxb[0:SUB]
        v512_ref[2:3, :] += _colsum(dxb)
        for j in range(4):
            shifted = ul_ext if j == 0 else pltpu.roll(ul_ext, j, 0)
            v512_ref[4 + (3 - j):5 + (3 - j), :] += _colsum(dxb * shifted[HALO:])

        dproj_ref[...] = jnp.concatenate([du_pool, du_lru, dug], axis=1).astype(BF16)

        @pl.when(i == nb - 1)
        def _():
            v512_ref[3:4, :] = v512_ref[3:4, :] * (LRU_C * _sigmoid(-l_ref[...]))
            pltpu.sync_copy(dwp_acc, dwp_out)
            pltpu.sync_copy(dwg_acc, dwg_out)
            wait_scatter()

    rev = lambda w: pl.BlockSpec((tb, w), lambda i: (nb - 1 - i, 0))
    halo = lambda rows, w: pl.BlockSpec((rows, w), lambda i: (jnp.maximum((nb - 1 - i) * (tb // rows) - 1, 0), 0))
    any_spec = pl.BlockSpec(memory_space=pl.ANY)
    smalls = [wp_bd, pool_b, pool_scale, conv_w, conv_b, wg_bd, gate_b, lru_l, w_out]
    return pl.pallas_call(
        body, name="mix_bwd", grid=(nb,),
        in_specs=[rev(d), rev(1536), halo(HALO, 1536), rev(512), halo(SUB, 512)]
        + [_const_spec(s.shape) for s in smalls] + [any_spec] * len(flat_parts),
        out_specs=[rev(1536), pl.BlockSpec((8, 512), lambda i: (0, 0)), pl.BlockSpec((8, 1024), lambda i: (0, 0))]
        + [any_spec] * (2 + n_s),
        out_shape=[jax.ShapeDtypeStruct((t_len, 1536), BF16), jax.ShapeDtypeStruct((8, 512), F32),
                   jax.ShapeDtypeStruct((8, 1024), F32), jax.ShapeDtypeStruct(wp_bd.shape, F32),
                   jax.ShapeDtypeStruct(wg_bd.shape, F32)]
        + [jax.ShapeDtypeStruct((N_DEV,) + parts[0].shape[1:], parts[0].dtype) for parts in scatter_parts],
        scratch_shapes=[pltpu.VMEM(wp_bd.shape, F32), pltpu.VMEM(wg_bd.shape, F32), pltpu.VMEM((tb + HALO, 1024), F32),
                        pltpu.VMEM((tb, 512), F32), pltpu.VMEM((tb, 512), F32), pltpu.VMEM((tb, 512), F32),
                        pltpu.VMEM((HALO, 512), F32), pltpu.VMEM((SUB, 512), F32), pltpu.VMEM((SUB, 512), F32),
                        pltpu.VMEM((SUB, 512), F32)]
        + _exchange_scratch(n_s),
        compiler_params=_params(),
    )(dh1, proj, proj, hs, hs, *smalls, *flat_parts)


def _wgrad(name, a, b, tb):
    t_len, m = a.shape
    n = b.shape[1]
    nb = t_len // tb

    def body(a_ref, b_ref, out_ref, acc_ref, stage_ref):
        i = pl.program_id(0)

        @pl.when(i == 0)
        def _():
            acc_ref[...] = jnp.zeros_like(acc_ref)

        acc_ref[...] += _dot_tn(a_ref[...], b_ref[...].astype(BF16))

        @pl.when(i == nb - 1)
        def _():
            stage_ref[...] = acc_ref[...].astype(BF16)
            pltpu.sync_copy(stage_ref, out_ref)

    return pl.pallas_call(
        body, name=name, grid=(nb,),
        in_specs=[pl.BlockSpec((tb, m), lambda i: (i, 0)), pl.BlockSpec((tb, n), lambda i: (i, 0))],
        out_specs=pl.BlockSpec(memory_space=pl.ANY), out_shape=jax.ShapeDtypeStruct((m, n), BF16),
        scratch_shapes=[pltpu.VMEM((m, n), F32), pltpu.VMEM((m, n), BF16)], compiler_params=_params(),
    )(a, b)


def _in_bwd(dproj, z1, x, dh1, g_mix, w_in, tb):
    t_len, d = x.shape
    nb = t_len // tb

    def body(dproj_ref, z1_ref, x_ref, dh1_ref, g_ref, win_ref, dx_ref, vec_ref, dwin_out, dwin_acc, dwin_stage):
        i = pl.program_id(0)

        @pl.when(i == 0)
        def _():
            vec_ref[...] = jnp.zeros_like(vec_ref)
            dwin_acc[...] = jnp.zeros_like(dwin_acc)

        dprojb = dproj_ref[...]
        dwin_acc[...] += _dot_tn(z1_ref[...], dprojb)
        dz1 = _dot_nt(dprojb, win_ref[...])
        g = g_ref[...]
        _, xh, rr = _rms_fwd(x_ref[...], g)
        dx_ref[...] = dh1_ref[...] + _rms_bwd(xh, rr, g, dz1)
        vec_ref[0:1, :] += _colsum(dz1 * xh)

        @pl.when(i == nb - 1)
        def _():
            dwin_stage[...] = dwin_acc[...].astype(BF16)
            pltpu.sync_copy(dwin_stage, dwin_out)

    row_spec = lambda w: pl.BlockSpec((tb, w), lambda i: (i, 0))
    return pl.pallas_call(
        body, name="in_bwd", grid=(nb,),
        in_specs=[row_spec(dproj.shape[1]), row_spec(d), row_spec(d), row_spec(d), _const_spec(g_mix.shape),
                  _const_spec(w_in.shape)],
        out_specs=[row_spec(d), pl.BlockSpec((8, d), lambda i: (0, 0)), pl.BlockSpec(memory_space=pl.ANY)],
        out_shape=[jax.ShapeDtypeStruct((t_len, d), F32), jax.ShapeDtypeStruct((8, d), F32),
                   jax.ShapeDtypeStruct(w_in.shape, BF16)],
        scratch_shapes=[pltpu.VMEM(w_in.shape, F32), pltpu.VMEM(w_in.shape, BF16)], compiler_params=_params(),
    )(dproj, z1, x, dh1, g_mix, w_in)


def _exchange(name, gathered, scattered):
    n_g, n = len(gathered), len(gathered) + len(scattered)
    srcs = list(gathered) + list(scattered)
    shapes = [a.shape for a in gathered] + [a.shape[1:] for a in scattered]

    def body(*refs):
        ins, outs = refs[:n], refs[n:2 * n]
        srcs_of = [(lambda s, r=r: r) for r in ins[:n_g]] + [(lambda s, r=r: r.at[s]) for r in ins[n_g:]]
        start, wait = _all_to_all(srcs_of, outs, *refs[2 * n:])
        start()
        wait()

    any_spec = pl.BlockSpec(memory_space=pl.ANY)
    return pl.pallas_call(
        body, name=name, in_specs=[any_spec] * n, out_specs=[any_spec] * n,
        out_shape=[jax.ShapeDtypeStruct((N_DEV,) + tuple(sh), a.dtype) for sh, a in zip(shapes, srcs)],
        scratch_shapes=_exchange_scratch(n),
    )(*srcs)


def _final_exchange(name, small, scattered):
    n = len(scattered)
    rows = small.shape[0]
    per = rows // N_DEV

    def body(*refs):
        small_ref = refs[0]
        ins = refs[1:1 + n]
        sum_ref = refs[1 + n]
        outs = refs[2 + n:2 + 2 * n]
        land_ref, send_sems, recv_sems, local_sems, small_send, small_recv = refs[2 + 2 * n:]
        me = _my_index()
        start, wait = _all_to_all([(lambda s, r=r: r.at[s]) for r in ins], outs, send_sems, recv_sems, local_sems)
        start()
        mine = pl.ds(pl.multiple_of(me * per, 8), per)

        def dev(s):
            return (s // 4, (s // 2) % 2, s % 2)

        def partial_to(s):
            return pltpu.make_async_remote_copy(
                src_ref=small_ref.at[pl.ds(s * per, per)], dst_ref=land_ref.at[me], send_sem=small_send.at[0, s],
                recv_sem=small_recv.at[0, me], device_id=dev(s), device_id_type=MESH)

        def partial_from(s):
            return pltpu.make_async_remote_copy(
                src_ref=small_ref.at[pl.ds(s * per, per)], dst_ref=land_ref.at[s], send_sem=small_send.at[0, s],
                recv_sem=small_recv.at[0, s], device_id=dev(s), device_id_type=MESH)

        def sum_to(s):
            return pltpu.make_async_remote_copy(
                src_ref=sum_ref.at[mine], dst_ref=sum_ref.at[mine], send_sem=small_send.at[1, s],
                recv_sem=small_recv.at[1, me], device_id=dev(s), device_id_type=MESH)

        def sum_from(s):
            rows_s = pl.ds(s * per, per)
            return pltpu.make_async_remote_copy(
                src_ref=sum_ref.at[rows_s], dst_ref=sum_ref.at[rows_s], send_sem=small_send.at[1, s],
                recv_sem=small_recv.at[1, s], device_id=dev(s), device_id_type=MESH)

        for s in range(N_DEV):
            @pl.when(s != me)
            def _():
                partial_to(s).start()
        land_ref[me] = small_ref[mine, :]
        for s in range(N_DEV):
            @pl.when(s != me)
            def _():
                partial_from(s).wait_recv()
        total = land_ref[0]
        for s in range(1, N_DEV):
            total = total + land_ref[s]
        sum_ref[mine, :] = total
        for s in range(N_DEV):
            @pl.when(s != me)
            def _():
                sum_to(s).start()
        for s in range(N_DEV):
            @pl.when(s != me)
            def _():
                sum_from(s).wait_recv()
                partial_to(s).wait_send()
                sum_to(s).wait_send()
        wait()

    any_spec = pl.BlockSpec(memory_space=pl.ANY)
    vmem_spec = pl.BlockSpec(memory_space=pltpu.VMEM)
    res = pl.pallas_call(
        body, name=name, in_specs=[vmem_spec] + [any_spec] * n, out_specs=[vmem_spec] + [any_spec] * n,
        out_shape=[jax.ShapeDtypeStruct(small.shape, F32)]
        + [jax.ShapeDtypeStruct(a.shape, a.dtype) for a in scattered],
        scratch_shapes=[pltpu.VMEM((N_DEV, per, small.shape[1]), F32)] + _exchange_scratch(n)
        + [pltpu.SemaphoreType.DMA((2, N_DEV)), pltpu.SemaphoreType.DMA((2, N_DEV))],
    )(small, *scattered)
    return res[0], res[1:]


def _adamw(name, parts, w, m, v, row_block):
    n_src, rows, cols = parts.shape
    rb = min(row_block, rows)

    def body(p_ref, w_ref, m_ref, v_ref, g_out, d_out, m_out, v_out):
        g = p_ref[0].astype(F32)
        for s in range(1, n_src):
            g = g + p_ref[s].astype(F32)
        m_new = ADAM_B1 * m_ref[...] + (1.0 - ADAM_B1) * g
        v_new = ADAM_B2 * v_ref[...] + (1.0 - ADAM_B2) * jnp.square(g)
        m_hat = m_new / (1.0 - ADAM_B1 ** ADAM_STEP)
        v_hat = v_new / (1.0 - ADAM_B2 ** ADAM_STEP)
        g_out[...] = g
        d_out[...] = -ADAM_LR * (m_hat / (jnp.sqrt(v_hat) + ADAM_EPS) + ADAM_WD * w_ref[...])
        m_out[...] = m_new
        v_out[...] = v_new

    spec = pl.BlockSpec((rb, cols), lambda i: (i, 0))
    return pl.pallas_call(
        body, name=name, grid=(rows // rb,),
        in_specs=[pl.BlockSpec((n_src, rb, cols), lambda i: (0, i, 0)), spec, spec, spec],
        out_specs=[spec] * 4, out_shape=[jax.ShapeDtypeStruct((rows, cols), F32)] * 4,
        compiler_params=pltpu.CompilerParams(dimension_semantics=("parallel",), vmem_limit_bytes=VMEM_LIMIT),
    )(parts, w, m, v)


def _block_diag(blocks):
    g, a, b = blocks.shape
    eye = jnp.eye(g, dtype=blocks.dtype)
    return (eye[:, None, :, None] * blocks[:, :, None, :]).reshape(g * a, g * b)


def _diag_blocks(mat, g):
    a, b = mat.shape[0] // g, mat.shape[1] // g
    m4 = mat.reshape(g, a, g, b)
    return jnp.stack([m4[k, :, k, :] for k in range(g)], axis=0)


def _pack(pieces, row_multiple=8, width=1024):
    flat = jnp.concatenate([p.reshape(-1) for p in pieces])
    rows = -(-flat.shape[0] // (row_multiple * width)) * row_multiple
    return jnp.pad(flat, (0, rows * width - flat.shape[0])).reshape(rows, width)


def _unpack(packed, shapes):
    flat = packed.reshape(-1)
    out, o = [], 0
    for sh in shapes:
        size = 1
        for k in sh:
            size *= k
        out.append(flat[o:o + size].reshape(sh))
        o += size
    return out


def _small_grads(v512, v1024, in_vec, mlp_vec, ple_vec, dwp_bd, dwg_bd, pool_b, gate_a_b, gate_x_b):
    return [
        in_vec[0:1],
        _diag_blocks(dwp_bd, N_POOL_GROUPS)[None],
        v512[1:2].reshape(pool_b.shape),
        v512[0:1],
        v512[2:3],
        _diag_blocks(dwg_bd[:, :512], LRU_HEADS)[None],
        v1024[0:1, :512].reshape(gate_a_b.shape),
        _diag_blocks(dwg_bd[:, 512:], LRU_HEADS)[None],
        v1024[0:1, 512:].reshape(gate_x_b.shape),
        v512[3:4],
        mlp_vec[0:1],
        ple_vec[1:2],
        ple_vec[0:1],
        ple_vec[2:3].reshape(-1),
        v512[4:8][None],
        ple_vec[3:4, 0:1].reshape(1),
    ]


def kernel(x, p, norm_mix_g, w_in, pool_w, pool_b, pool_scale, conv_w, conv_b, gate_a_w, gate_a_b, gate_x_w, gate_x_b, lru_L, w_out, norm_mlp_g, w_up, w_down, norm_ple_g, w_ple_gate, b_ple_gate, w_ple_proj, norm_final_g, loss_target, m_norm_mix_g, m_w_in, m_pool_w, m_pool_b, m_pool_scale, m_conv_w, m_conv_b, m_gate_a_w, m_gate_a_b, m_gate_x_w, m_gate_x_b, m_lru_L, m_w_out, m_norm_mlp_g, m_w_up, m_w_down, m_norm_ple_g, m_w_ple_gate, m_b_ple_gate, m_w_ple_proj, m_norm_final_g, v_norm_mix_g, v_w_in, v_pool_w, v_pool_b, v_pool_scale, v_conv_w, v_conv_b, v_gate_a_w, v_gate_a_b, v_gate_x_w, v_gate_x_b, v_lru_L, v_w_out, v_norm_mlp_g, v_w_up, v_w_down, v_norm_ple_g, v_w_ple_gate, v_b_ple_gate, v_w_ple_proj, v_norm_final_g):
    t_len, d = x.shape[1], x.shape[2]
    tbs = {k: min(v, t_len) for k, v in TIME_BLOCKS.items()}
    me = _my_index()

    win_g, wout_g, convw_g = _gather("gather_mixer_weights", [w_in[0].astype(BF16), w_out[0].astype(BF16), conv_w[0]])
    w_in_f = jnp.transpose(win_g, (1, 0, 2)).reshape(d, -1)
    conv_w_f = jnp.transpose(convw_g, (1, 0, 2)).reshape(convw_g.shape[1], -1)
    wp_bd = _block_diag(pool_w[0]).astype(BF16)
    wg_bd = jnp.concatenate([_block_diag(gate_a_w[0]), _block_diag(gate_x_w[0])], axis=1).astype(BF16)
    gate_b2 = jnp.concatenate([gate_a_b.reshape(1, -1), gate_x_b.reshape(1, -1)], axis=1)
    mixer_small = (norm_mix_g, w_in_f, wp_bd, pool_b.reshape(1, -1), pool_scale, conv_w_f, conv_b, wg_bd, gate_b2, lru_L,
                   wout_g.reshape(-1, d))

    x2 = x[0]
    later = [w_up[0].astype(BF16), w_down[0].astype(BF16), w_ple_gate[0].astype(BF16), w_ple_proj[0].astype(BF16)]
    h1, z1, proj, hs, cat, wup_g, wdn_g, wgate_g, wproj_g = _mix_fwd(x2, *mixer_small, later, GATHER_FORWARD_AT,
                                                                     tbs['mix_fwd'])
    w_down_f = wdn_g.reshape(-1, d)
    w_proj_f = jnp.transpose(wproj_g, (1, 0, 2)).reshape(wproj_g.shape[1], -1)
    h2, z2, up = _mlp_fwd(h1, norm_mlp_g, wup_g, w_down_f, tbs['mlp_fwd'])
    dh2, ple_vec, dw_gate, dw_proj = _ple(h2, p[0, 0], loss_target[0], norm_ple_g, wgate_g.reshape(-1, d), b_ple_gate,
                                          w_proj_f, norm_final_g.reshape(1, -1), tbs['ple'])
    dz2 = None
    dw_up_parts, dw_down_parts = [], []
    for part in range(MLP_BWD_SPLIT):
        res = _mlp_bwd_part(part, MLP_BWD_SPLIT, dh2, z2, up, wup_g, w_down_f, dz2, h1, norm_mlp_g, tbs['mlp_bwd'])
        dz2 = res[0]
        dw_up_parts.append(res[-2])
        dw_down_parts.append(res[-1])
    dh1, mlp_vec = res[0], res[1]
    n_proj = w_ple_proj.shape[2]
    dw_out = _wgrad("wgrad_out", cat, dh1, tbs['wgrad_out'])
    early = [dw_up_parts, dw_down_parts, [dw_gate.reshape(N_DEV, -1, d)],
             [jnp.transpose(dw_proj.reshape(-1, N_DEV, n_proj), (1, 0, 2))], [dw_out.reshape(N_DEV, -1, d)]]
    (dproj, v512, v1024, dwp_bd, dwg_bd,
     recv_up, recv_down, recv_gate, recv_proj, recv_out) = _mix_bwd(dh1, proj, hs, *mixer_small[2:], early, tbs['mix_bwd'])
    dx, in_vec, dw_in = _in_bwd(dproj, z1, x2, dh1, norm_mix_g, w_in_f, tbs['in_bwd'])

    small_grads = _small_grads(v512, v1024, in_vec, mlp_vec, ple_vec, dwp_bd, dwg_bd, pool_b, gate_a_b, gate_x_b)
    n_in = w_in.shape[2]
    small_sum, (recv_in,) = _final_exchange(
        "exchange_last_grads", _pack(small_grads, row_multiple=8 * N_DEV),
        [jnp.transpose(dw_in.reshape(d, N_DEV, n_in), (1, 0, 2))])
    received = [recv_in, recv_out, recv_up, recv_down, recv_gate, recv_proj]
    small_w = [norm_mix_g, pool_w, pool_b, pool_scale, conv_b, gate_a_w, gate_a_b, gate_x_w, gate_x_b, lru_L,
               norm_mlp_g, norm_ple_g, b_ple_gate, norm_final_g]
    small_m = [m_norm_mix_g, m_pool_w, m_pool_b, m_pool_scale, m_conv_b, m_gate_a_w, m_gate_a_b, m_gate_x_w, m_gate_x_b,
               m_lru_L, m_norm_mlp_g, m_norm_ple_g, m_b_ple_gate, m_norm_final_g]
    small_v = [v_norm_mix_g, v_pool_w, v_pool_b, v_pool_scale, v_conv_b, v_gate_a_w, v_gate_a_b, v_gate_x_w, v_gate_x_b,
               v_lru_L, v_norm_mlp_g, v_norm_ple_g, v_b_ple_gate, v_norm_final_g]
    small_shapes = [a.shape for a in small_w]
    conv_full_shape = (1,) + conv_w_f.shape

    shard_w = [w_in[0], w_out[0], w_up[0], w_down[0], w_ple_gate[0], w_ple_proj[0]]
    shard_m = [m_w_in[0], m_w_out[0], m_w_up[0], m_w_down[0], m_w_ple_gate[0], m_w_ple_proj[0]]
    shard_v = [v_w_in[0], v_w_out[0], v_w_up[0], v_w_down[0], v_w_ple_gate[0], v_w_ple_proj[0]]
    names = ["w_in", "w_out", "w_up", "w_down", "w_ple_gate", "w_ple_proj"]
    big_res = {}
    for nm, parts, w_s, m_s, v_s in zip(names, received, shard_w, shard_m, shard_v):
        big_res[nm] = [r[None] for r in _adamw("adamw_" + nm, parts, w_s, m_s, v_s, 128)]

    summed = _unpack(small_sum, small_shapes + [conv_full_shape, (1,)])
    loss = summed[-1][0]
    conv_g = lax.dynamic_slice_in_dim(summed[-2], me * conv_w.shape[2], conv_w.shape[2], axis=2)
    sg = summed[:-2] + [conv_g]
    sw, sm, sv = small_w + [conv_w], small_m + [m_conv_w], small_v + [v_conv_w]
    shapes2 = small_shapes + [conv_w.shape]
    res = _adamw("adamw_small", _pack(sg)[None], _pack(sw), _pack(sm), _pack(sv), 1024)
    sres = [_unpack(r, shapes2) for r in res]
    small_names = ["norm_mix_g", "pool_w", "pool_b", "pool_scale", "conv_b", "gate_a_w", "gate_a_b", "gate_x_w",
                   "gate_x_b", "lru_L", "norm_mlp_g", "norm_ple_g", "b_ple_gate", "norm_final_g", "conv_w"]
    order = ["norm_mix_g", "w_in", "pool_w", "pool_b", "pool_scale", "conv_w", "conv_b", "gate_a_w", "gate_a_b",
             "gate_x_w", "gate_x_b", "lru_L", "w_out", "norm_mlp_g", "w_up", "w_down", "norm_ple_g", "w_ple_gate",
             "b_ple_gate", "w_ple_proj", "norm_final_g"]
    outs = [loss, dx[None]]
    for kind in range(4):
        for nm in order:
            if nm in big_res:
                outs.append(big_res[nm][kind])
            else:
                outs.append(sres[kind][small_names.index(nm)])
    return tuple(outs)
```

```python
import functools

import jax
import jax.numpy as jnp
from jax import lax
from jax.experimental import pallas as pl
from jax.experimental.pallas import tpu as pltpu

F32 = jnp.float32
BF16 = jnp.bfloat16
MESH = pl.DeviceIdType.MESH

N_DEV = 8
RMS_EPS = 1e-6
LRU_C = 8.0
POOL_WINDOWS = (2, 4, 8, 16)
N_POOL_GROUPS = 4
LRU_HEADS = 8
HALO = 16
SUB = 8
GELU_C0 = 0.7978845608028654
GELU_C1 = 0.044715

ADAM_LR = 0.001
ADAM_B1 = 0.9
ADAM_B2 = 0.999
ADAM_EPS = 1e-08
ADAM_WD = 0.01
ADAM_STEP = 10

VMEM_LIMIT = 60 * 1024 * 1024
TIME_BLOCKS = dict(mix_fwd=512, mlp_fwd=512, ple=512, mlp_bwd=512, wgrad_out=1024, mix_bwd=512, in_bwd=512)
MLP_BWD_SPLIT = 2
GATHER_FORWARD_AT = (0.5, 0.875, 1.0, 1.0)


def _params(n_arbitrary=1):
    return pltpu.CompilerParams(dimension_semantics=("arbitrary",) * n_arbitrary, vmem_limit_bytes=VMEM_LIMIT)


def _dot(a, b):
    return jnp.dot(a, b, preferred_element_type=F32)


def _dot_nt(a, b):
    return lax.dot_general(a, b, (((1,), (1,)), ((), ())), preferred_element_type=F32)


def _dot_tn(a, b):
    return lax.dot_general(a, b, (((0,), (0,)), ((), ())), preferred_element_type=F32)


def _rms_fwd(x, g):
    r = lax.rsqrt(jnp.mean(x * x, axis=-1, keepdims=True) + RMS_EPS)
    xh = x * r
    return xh * g, xh, r


def _rms_bwd(xh, r, g, dz):
    dxh = dz * g
    return r * (dxh - xh * jnp.mean(dxh * xh, axis=-1, keepdims=True))


def _colsum(a):
    return jnp.sum(a, axis=0, keepdims=True)


def _sigmoid(a):
    return 0.5 * jnp.tanh(0.5 * a) + 0.5


def _gelu_parts(u):
    u2 = u * u
    th = jnp.tanh(GELU_C0 * (u + GELU_C1 * u * u2))
    gel = 0.5 * u * (1.0 + th)
    dgel = 0.5 * (1.0 + th) + 0.5 * u * (1.0 - th * th) * (GELU_C0 * (1.0 + 3.0 * GELU_C1 * u2))
    return gel, dgel


def _my_index():
    return 4 * lax.axis_index("x") + 2 * lax.axis_index("y") + lax.axis_index("c")


def _all_to_all(srcs_of, dsts, send_sems, recv_sems, local_sems):
    n = len(dsts)
    me = _my_index()

    def remote(t, s):
        return pltpu.make_async_remote_copy(
            src_ref=srcs_of[t](s), dst_ref=dsts[t].at[me], send_sem=send_sems.at[t, s], recv_sem=recv_sems.at[t, me],
            device_id=(s // 4, (s // 2) % 2, s % 2), device_id_type=MESH)

    def arrival(t, s):
        return pltpu.make_async_remote_copy(
            src_ref=srcs_of[t](s), dst_ref=dsts[t].at[s], send_sem=send_sems.at[t, s], recv_sem=recv_sems.at[t, s],
            device_id=(s // 4, (s // 2) % 2, s % 2), device_id_type=MESH)

    def local(t, s):
        return pltpu.make_async_copy(srcs_of[t](s), dsts[t].at[s], local_sems.at[t])

    def start():
        for s in range(N_DEV):
            @pl.when(s == me)
            def _():
                for t in range(n):
                    local(t, s).start()

            @pl.when(s != me)
            def _():
                for t in range(n):
                    remote(t, s).start()

    def wait():
        for s in range(N_DEV):
            @pl.when(s == me)
            def _():
                for t in range(n):
                    local(t, s).wait()

            @pl.when(s != me)
            def _():
                for t in range(n):
                    remote(t, s).wait_send()
                    arrival(t, s).wait_recv()

    return start, wait


N_GATHER_COPIES = 7


def _two_level_gather(srcs, dsts, send_sems, recv_sems, local_sems):
    n = len(dsts)
    x, y, c = lax.axis_index("x"), lax.axis_index("y"), lax.axis_index("c")
    me, sibling = (x, y, c), (x, y, 1 - c)
    chips = [(1 - x, y), (x, 1 - y), (1 - x, 1 - y)]

    def slot(dev):
        return 4 * dev[0] + 2 * dev[1] + dev[2]

    def copy(t, k, block, to, src=None):
        return pltpu.make_async_remote_copy(
            src_ref=dsts[t].at[slot(block)] if src is None else src, dst_ref=dsts[t].at[slot(block)],
            send_sem=send_sems.at[t, k], recv_sem=recv_sems.at[t, k], device_id=to, device_id_type=MESH)

    def local(t):
        return pltpu.make_async_copy(srcs[t], dsts[t].at[slot(me)], local_sems.at[t])

    def start():
        for t in range(n):
            local(t).start()
            for j, chip in enumerate(chips):
                copy(t, 1 + j, me, (*chip, c), src=srcs[t]).start()
            copy(t, 0, me, sibling, src=srcs[t]).start()

    def forward(t):
        for j, chip in enumerate(chips):
            copy(t, 1 + j, (*chip, c), me).wait_recv()
            copy(t, 4 + j, (*chip, c), sibling).start()

    def finish():
        for t in range(n):
            copy(t, 0, sibling, me).wait_recv()
            for j, chip in enumerate(chips):
                copy(t, 4 + j, (*chip, 1 - c), me).wait_recv()
            copy(t, 0, me, sibling, src=srcs[t]).wait_send()
            for j, chip in enumerate(chips):
                copy(t, 1 + j, me, (*chip, c), src=srcs[t]).wait_send()
                copy(t, 4 + j, (*chip, c), sibling).wait_send()
            local(t).wait()

    return start, forward, finish


def _gather_scratch(n):
    return [pltpu.SemaphoreType.DMA((n, N_GATHER_COPIES)), pltpu.SemaphoreType.DMA((n, N_GATHER_COPIES)),
            pltpu.SemaphoreType.DMA((n,))]


def _gather(name, srcs):
    n = len(srcs)

    def body(*refs):
        start, forward, finish = _two_level_gather(refs[:n], refs[n:2 * n], *refs[2 * n:])
        start()
        for t in range(n):
            forward(t)
        finish()

    any_spec = pl.BlockSpec(memory_space=pl.ANY)
    return pl.pallas_call(
        body, name=name, in_specs=[any_spec] * n, out_specs=[any_spec] * n,
        out_shape=[jax.ShapeDtypeStruct((N_DEV,) + a.shape, a.dtype) for a in srcs], scratch_shapes=_gather_scratch(n),
    )(*srcs)


def _exchange_scratch(n):
    return [pltpu.SemaphoreType.DMA((n, N_DEV)), pltpu.SemaphoreType.DMA((n, N_DEV)), pltpu.SemaphoreType.DMA((n,))]


def _const_spec(shape):
    nd = len(shape)
    return pl.BlockSpec(shape, lambda i: (0,) * nd, pipeline_mode=pl.Buffered(1))


def _pool_windows(up_ext, n, forward):
    sh = (lambda k: k) if forward else (lambda k: n - k)
    s2 = up_ext + pltpu.roll(up_ext, sh(1), 0)
    t4 = s2[:, 128:]
    s4 = t4 + pltpu.roll(t4, sh(2), 0)
    t8 = s4[:, 128:]
    s8 = t8 + pltpu.roll(t8, sh(4), 0)
    t16 = s8[:, 128:]
    s16 = t16 + pltpu.roll(t16, sh(8), 0)
    return jnp.concatenate([s2[:, :128], s4[:, :128], s8[:, :128], s16], axis=1)


def _inv_count_head():
    t = jnp.arange(1, HALO + 1, dtype=F32)[:, None]
    return jnp.concatenate([jnp.broadcast_to(1.0 / jnp.minimum(t, float(w)), (HALO, 128)) for w in POOL_WINDOWS], axis=1)


def _scale_by_inv_count(v, is_first_block, inv_head):
    inv_row = jnp.concatenate([jnp.full((1, 128), 1.0 / w, F32) for w in POOL_WINDOWS], axis=1)
    head = v[0:HALO] * jnp.where(is_first_block, inv_head, inv_row)
    return jnp.concatenate([head, v[HALO:] * inv_row], axis=0)


def _lru_decay(r, a, c_l, first_row):
    a2 = a * a
    m2 = -jnp.tanh(c_l * r) * (a2 + 1.0)
    return a2, m2, jnp.where(first_row, 1.0, jnp.sqrt(m2))


def _log_sigmoid(v):
    return -(jnp.maximum(-v, 0.0) + jnp.log1p(jnp.exp(-jnp.abs(v))))


def _conv_fwd(ul_ext, cw, cb):
    return (cb + cw[3:4, :] * ul_ext + cw[2:3, :] * pltpu.roll(ul_ext, 1, 0)
            + cw[1:2, :] * pltpu.roll(ul_ext, 2, 0) + cw[0:1, :] * pltpu.roll(ul_ext, 3, 0))


def _mix_fwd(x, g_mix, w_in, wp_bd, pool_b, pool_scale, conv_w, conv_b, wg_bd, gate_b, lru_l, w_out, gather_srcs,
             forward_at, tb):
    t_len, d = x.shape
    nb = t_len // tb
    n_g = len(gather_srcs)
    forward_step = [min(nb - 1, int(f * nb)) for f in forward_at]

    def body(*refs):
        (x_ref, g_ref, win_ref, wp_ref, pb_ref, ps_ref, cw_ref, cb_ref, wg_ref, gb_ref, l_ref, wout_ref,
         invh_ref) = refs[:13]
        gsrc = refs[13:13 + n_g]
        h1_ref, z1_ref, proj_ref, hs_ref, cat_ref, lru_ref = refs[13 + n_g:19 + n_g]
        gdst = refs[19 + n_g:19 + 2 * n_g]
        ext_ref, a_ref, b_ref, hc_ref, send_sems, recv_sems, local_sems = refs[19 + 2 * n_g:]
        i = pl.program_id(0)
        start_gather, forward_gather, finish_gather = _two_level_gather(gsrc, gdst, send_sems, recv_sems, local_sems)

        @pl.when(i == 0)
        def _():
            start_gather()
            ext_ref[0:HALO, :] = jnp.zeros((HALO, 1024), F32)
            hc_ref[...] = jnp.zeros_like(hc_ref)

        xv = x_ref[...]
        z, _, _ = _rms_fwd(xv, g_ref[...])
        zb = z.astype(BF16)
        z1_ref[...] = zb
        proj = _dot(zb, win_ref[...])
        proj_ref[...] = proj
        ext_ref[HALO:, :] = proj[:, 0:1024]
        ug = proj[:, 1024:1536]
        n = tb + HALO
        up_ext = ext_ref[:, 0:512]
        win = _pool_windows(up_ext, n, True)[HALO:]
        dpool = _scale_by_inv_count(win, i == 0, invh_ref[...]) - proj[:, 0:512]
        q = _dot(dpool.astype(BF16), wp_ref[...]) + pb_ref[...]
        y_pool = q * ps_ref[...]
        xb = _conv_fwd(ext_ref[:, 512:1024], cw_ref[...], cb_ref[...])[HALO:]
        first_row = (i * tb + lax.broadcasted_iota(jnp.int32, (tb, 1), 0)) == 0
        c_l = LRU_C * _log_sigmoid(l_ref[...])
        gp = _dot(xb.astype(BF16), wg_ref[...]) + gb_ref[...]
        r = _sigmoid(gp[:, :512])
        ig = _sigmoid(gp[:, 512:])
        a = jnp.exp(c_l * r)
        _, _, mult = _lru_decay(r, a, c_l, first_row)
        lru_ref[:, 0:512] = xb
        lru_ref[:, 512:1024] = r
        lru_ref[:, 1024:1536] = ig
        lru_ref[:, 1536:2048] = a
        a_ref[...] = a
        b_ref[...] = mult * (ig * xb)
        row = lax.broadcasted_iota(jnp.int32, (SUB, 512), 0)

        def group(j, hprev):
            o = pl.multiple_of(j * SUB, SUB)
            a8 = a_ref[pl.ds(o, SUB), :]
            b8 = b_ref[pl.ds(o, SUB), :]
            for sh in (1, 2, 4):
                ash = jnp.where(row >= sh, pltpu.roll(a8, sh, 0), 1.0)
                bsh = jnp.where(row >= sh, pltpu.roll(b8, sh, 0), 0.0)
                b8 = a8 * bsh + b8
                a8 = a8 * ash
            h8 = a8 * hprev + b8
            hs_ref[pl.ds(o, SUB), :] = h8
            return jnp.broadcast_to(h8[SUB - 1:SUB, :], (SUB, 512))

        hc_ref[...] = lax.fori_loop(0, tb // SUB, group, hc_ref[...])
        gel, _ = _gelu_parts(ug)
        y_lru = hs_ref[...] * gel
        catb = jnp.concatenate([y_pool, y_lru], axis=1).astype(BF16)
        cat_ref[...] = catb
        h1_ref[...] = xv + _dot(catb, wout_ref[...])
        ext_ref[0:HALO, :] = ext_ref[tb:tb + HALO, :]

        for t in range(n_g):
            @pl.when(i == forward_step[t])
            def _():
                forward_gather(t)

        @pl.when(i == nb - 1)
        def _():
            finish_gather()

    row_spec = lambda w: pl.BlockSpec((tb, w), lambda i: (i, 0))
    any_spec = pl.BlockSpec(memory_space=pl.ANY)
    smalls = [g_mix, w_in, wp_bd, pool_b, pool_scale, conv_w, conv_b, wg_bd, gate_b, lru_l, w_out, _inv_count_head()]
    return pl.pallas_call(
        body, name="mix_fwd", grid=(nb,),
        in_specs=[row_spec(d)] + [_const_spec(s.shape) for s in smalls] + [any_spec] * n_g,
        out_specs=[row_spec(d), row_spec(d), row_spec(1536), row_spec(512), row_spec(1024), row_spec(2048)]
        + [any_spec] * n_g,
        out_shape=[jax.ShapeDtypeStruct((t_len, d), F32), jax.ShapeDtypeStruct((t_len, d), BF16),
                   jax.ShapeDtypeStruct((t_len, 1536), F32), jax.ShapeDtypeStruct((t_len, 512), F32),
                   jax.ShapeDtypeStruct((t_len, 1024), BF16), jax.ShapeDtypeStruct((t_len, 2048), F32)]
        + [jax.ShapeDtypeStruct((N_DEV,) + s.shape, s.dtype) for s in gather_srcs],
        scratch_shapes=[pltpu.VMEM((tb + HALO, 1024), F32), pltpu.VMEM((tb, 512), F32), pltpu.VMEM((tb, 512), F32),
                        pltpu.VMEM((SUB, 512), F32)] + _gather_scratch(n_g),
        compiler_params=_params(),
    )(x, *smalls, *gather_srcs)


def _mlp_fwd(h1, g_mlp, w_up, w_down, tb):
    t_len, d = h1.shape
    nb = t_len // tb
    n_chunk, _, fc = w_up.shape

    def body(h1_ref, g_ref, wup_ref, wdn_ref, h2_ref, z2_ref, up_ref):
        xv = h1_ref[...]
        z, _, _ = _rms_fwd(xv, g_ref[...])
        zb = z.astype(BF16)
        z2_ref[...] = zb
        acc = xv
        for c in range(n_chunk):
            u = _dot(zb, wup_ref[c])
            up_ref[:, c * fc:(c + 1) * fc] = u.astype(BF16)
            act = jnp.square(jnp.maximum(u, 0.0)).astype(BF16)
            acc = acc + _dot(act, wdn_ref[c * fc:(c + 1) * fc, :])
        h2_ref[...] = acc

    row_spec = lambda w: pl.BlockSpec((tb, w), lambda i: (i, 0))
    return pl.pallas_call(
        body, name="mlp_fwd", grid=(nb,),
        in_specs=[row_spec(d), _const_spec(g_mlp.shape), _const_spec(w_up.shape), _const_spec(w_down.shape)],
        out_specs=[row_spec(d), row_spec(d), row_spec(n_chunk * fc)],
        out_shape=[jax.ShapeDtypeStruct((t_len, d), F32), jax.ShapeDtypeStruct((t_len, d), BF16),
                   jax.ShapeDtypeStruct((t_len, n_chunk * fc), BF16)],
        compiler_params=_params(),
    )(h1, g_mlp, w_up, w_down)


def _ple(h2, p, target, g_ple, w_gate, b_gate, w_proj, g_final, tb):
    t_len, d = h2.shape
    nb = t_len // tb
    pd = p.shape[1]

    def body(h2_ref, p_ref, tgt_ref, g_ref, wg_ref, bg_ref, wp_ref, gf_ref,
             dh2_ref, vec_ref, dwg_out, dwp_out, dwg_acc, dwp_acc, dwg_stage, dwp_stage):
        i = pl.program_id(0)

        @pl.when(i == 0)
        def _():
            vec_ref[...] = jnp.zeros_like(vec_ref)
            dwg_acc[...] = jnp.zeros_like(dwg_acc)
            dwp_acc[...] = jnp.zeros_like(dwp_acc)

        h2 = h2_ref[...]
        g2 = g_ref[...]
        z3, xh2, r2 = _rms_fwd(h2, g2)
        z3b = z3.astype(BF16)
        gate = _sigmoid(_dot(z3b, wg_ref[...]) + bg_ref[...])
        pb = p_ref[...].astype(BF16)
        pp = _dot(pb, wp_ref[...])
        h3 = h2 + gate * pp
        gf = gf_ref[...]
        y, xh3, r3 = _rms_fwd(h3, gf)
        err = y - tgt_ref[...]
        loss_rows = jnp.mean(err * err, axis=-1, keepdims=True)
        dy = err * (1.0 / d)
        dh3 = _rms_bwd(xh3, r3, gf, dy)
        dgl = (dh3 * pp) * (gate * (1.0 - gate))
        dpp = dh3 * gate
        dglb = dgl.astype(BF16)
        dwg_acc[...] += _dot_tn(z3b, dglb)
        dwp_acc[...] += _dot_tn(pb, dpp.astype(BF16))
        dz3 = _dot_nt(dglb, wg_ref[...])
        dh2_ref[...] = dh3 + _rms_bwd(xh2, r2, g2, dz3)
        vec_ref[0:1, :] += _colsum(dgl)
        vec_ref[1:2, :] += _colsum(dz3 * xh2)
        vec_ref[2:3, :] += _colsum(dy * xh3)
        vec_ref[3:4, :] += 0.5 * jnp.sum(loss_rows)

        @pl.when(i == nb - 1)
        def _():
            dwg_stage[...] = dwg_acc[...].astype(BF16)
            dwp_stage[...] = dwp_acc[...].astype(BF16)
            pltpu.sync_copy(dwg_stage, dwg_out)
            pltpu.sync_copy(dwp_stage, dwp_out)

    row_spec = lambda w: pl.BlockSpec((tb, w), lambda i: (i, 0))
    any_spec = pl.BlockSpec(memory_space=pl.ANY)
    smalls = [g_ple, w_gate, b_gate, w_proj, g_final]
    return pl.pallas_call(
        body, name="ple_fwd_bwd", grid=(nb,),
        in_specs=[row_spec(d), row_spec(pd), row_spec(d)] + [_const_spec(s.shape) for s in smalls],
        out_specs=[row_spec(d), pl.BlockSpec((8, d), lambda i: (0, 0)), any_spec, any_spec],
        out_shape=[jax.ShapeDtypeStruct((t_len, d), F32), jax.ShapeDtypeStruct((8, d), F32),
                   jax.ShapeDtypeStruct(w_gate.shape, BF16), jax.ShapeDtypeStruct(w_proj.shape, BF16)],
        scratch_shapes=[pltpu.VMEM(w_gate.shape, F32), pltpu.VMEM(w_proj.shape, F32), pltpu.VMEM(w_gate.shape, BF16),
                        pltpu.VMEM(w_proj.shape, BF16)],
        compiler_params=_params(),
    )(h2, p, target, *smalls)


def _mlp_bwd_part(part, n_part, dh2, z2, up, w_up, w_down, dz2_prev, h1, g_mlp, tb):
    t_len, d = dh2.shape
    nb = t_len // tb
    n_chunk_all, _, fc = w_up.shape
    n_chunk = n_chunk_all // n_part
    first, last = part == 0, part == n_part - 1

    def body(*refs):
        refs = list(refs)
        dh2_ref, z2_ref, up_ref, wup_ref, wdn_ref = refs[:5]
        del refs[:5]
        dzp_ref = None if first else refs.pop(0)
        h1_ref, g_ref = (refs.pop(0), refs.pop(0)) if last else (None, None)
        out_ref = refs.pop(0)
        vec_ref = refs.pop(0) if last else None
        dwup_out, dwdn_out, dwup_acc, dwdn_acc, up_stage, dn_stage = refs
        i = pl.program_id(0)

        @pl.when(i == 0)
        def _():
            dwup_acc[...] = jnp.zeros_like(dwup_acc)
            dwdn_acc[...] = jnp.zeros_like(dwdn_acc)
            if last:
                vec_ref[...] = jnp.zeros_like(vec_ref)

        dh2 = dh2_ref[...]
        dh2b = dh2.astype(BF16)
        z2b = z2_ref[...]
        dz2 = jnp.zeros((tb, d), F32) if first else dzp_ref[...]
        for c in range(n_chunk):
            u = up_ref[:, c * fc:(c + 1) * fc].astype(F32)
            ur = jnp.maximum(u, 0.0)
            dact = _dot_nt(dh2b, wdn_ref[c * fc:(c + 1) * fc, :])
            dupb = (dact * (2.0 * ur)).astype(BF16)
            dwdn_acc[c * fc:(c + 1) * fc, :] += _dot_tn((ur * ur).astype(BF16), dh2b)
            dwup_acc[c] += _dot_tn(z2b, dupb)
            dz2 = dz2 + _dot_nt(dupb, wup_ref[c])
        if last:
            g = g_ref[...]
            _, xh, r = _rms_fwd(h1_ref[...], g)
            out_ref[...] = dh2 + _rms_bwd(xh, r, g, dz2)
            vec_ref[0:1, :] += _colsum(dz2 * xh)
        else:
            out_ref[...] = dz2

        @pl.when(i == nb - 1)
        def _():
            for c in range(n_chunk):
                up_stage[...] = dwup_acc[c].astype(BF16)
                dn_stage[...] = dwdn_acc[c * fc:(c + 1) * fc, :].astype(BF16)
                pltpu.sync_copy(up_stage, dwup_out.at[c])
                pltpu.sync_copy(dn_stage, dwdn_out.at[c])

    row_spec = lambda w: pl.BlockSpec((tb, w), lambda i: (i, 0))
    any_spec = pl.BlockSpec(memory_space=pl.ANY)
    args = [dh2, z2, up, w_up, w_down]
    in_specs = [row_spec(d), row_spec(d), pl.BlockSpec((tb, n_chunk * fc), lambda i: (i, part)),
                pl.BlockSpec((n_chunk, d, fc), lambda i: (part, 0, 0), pipeline_mode=pl.Buffered(1)),
                pl.BlockSpec((n_chunk * fc, d), lambda i: (part, 0), pipeline_mode=pl.Buffered(1))]
    if not first:
        args.append(dz2_prev)
        in_specs.append(row_spec(d))
    if last:
        args += [h1, g_mlp]
        in_specs += [row_spec(d), _const_spec(g_mlp.shape)]
    out_specs = [row_spec(d)]
    out_shape = [jax.ShapeDtypeStruct((t_len, d), F32)]
    if last:
        out_specs.append(pl.BlockSpec((8, d), lambda i: (0, 0)))
        out_shape.append(jax.ShapeDtypeStruct((8, d), F32))
    out_specs += [any_spec, any_spec]
    out_shape += [jax.ShapeDtypeStruct((n_chunk, d, fc), BF16), jax.ShapeDtypeStruct((n_chunk, fc, d), BF16)]
    return pl.pallas_call(
        body, name=f"mlp_bwd_{part}", grid=(nb,), in_specs=in_specs, out_specs=out_specs, out_shape=out_shape,
        scratch_shapes=[pltpu.VMEM((n_chunk, d, fc), F32), pltpu.VMEM((n_chunk * fc, d), F32),
                        pltpu.VMEM((d, fc), BF16), pltpu.VMEM((fc, d), BF16)],
        compiler_params=_params(),
    )(*args)


def _mix_bwd(dh1, proj, hs, lru_saved, wp_bd, pool_b, pool_scale, conv_w, wg_bd, lru_l, w_out, scatter_parts, tb):
    t_len, d = dh1.shape
    nb = t_len // tb
    n_s = len(scatter_parts)
    flat_parts = [a for parts in scatter_parts for a in parts]

    def body(*refs):
        refs = list(refs)
        (dh1_ref, proj_ref, projh_ref, hs_ref, hsh_ref, lru_ref,
         wp_ref, pb_ref, ps_ref, cw_ref, wg_ref, l_ref, wout_ref, invh_ref) = refs[:14]
        del refs[:14]
        part_refs = []
        for parts in scatter_parts:
            part_refs.append(refs[:len(parts)])
            del refs[:len(parts)]
        dproj_ref, v512_ref, v1024_ref, dwp_out, dwg_out = refs[:5]
        recv = refs[5:5 + n_s]
        (dwp_acc, dwg_acc, ext_ref, b_ref, gs_ref, ehead_ref, dxbhead_ref, hc_ref,
         send_sems, recv_sems, local_sems) = refs[5 + n_s:]
        i = pl.program_id(0)
        tbk = nb - 1 - i

        def block_of(prefs):
            per = N_DEV // len(prefs)
            return lambda s: prefs[s // per].at[s % per]

        start_scatter, wait_scatter = _all_to_all([block_of(pr) for pr in part_refs], recv, send_sems, recv_sems,
                                                  local_sems)

        @pl.when(i == 0)
        def _():
            start_scatter()
            for ref in (v512_ref, v1024_ref, dwp_acc, dwg_acc, ehead_ref, dxbhead_ref, hc_ref):
                ref[...] = jnp.zeros_like(ref)

        dcat = _dot_nt(dh1_ref[...].astype(BF16), wout_ref[...])

        proj = proj_ref[...]
        has_prev = (tbk > 0).astype(F32)
        ext_ref[0:HALO, :] = projh_ref[:, 0:1024] * has_prev
        ext_ref[HALO:, :] = proj[:, 0:1024]
        ug = proj[:, 1024:1536]
        n = tb + HALO
        inv_head = invh_ref[...]

        up_ext = ext_ref[:, 0:512]
        win = _pool_windows(up_ext, n, True)[HALO:]
        dpool = _scale_by_inv_count(win, tbk == 0, inv_head) - proj[:, 0:512]
        dpoolb = dpool.astype(BF16)
        q = _dot(dpoolb, wp_ref[...]) + pb_ref[...]
        dyp = dcat[:, 0:512]
        dq = dyp * ps_ref[...]
        dqb = dq.astype(BF16)
        v512_ref[0:1, :] += _colsum(dyp * q)
        v512_ref[1:2, :] += _colsum(dq)
        dwp_acc[...] += _dot_tn(dpoolb, dqb)
        dd = _dot_nt(dqb, wp_ref[...])
        e = _scale_by_inv_count(dd, tbk == 0, inv_head)
        e_ext = jnp.concatenate([e, ehead_ref[...]], axis=0)
        du_pool = _pool_windows(e_ext, n, False)[0:tb] - dd
        ehead_ref[...] = e[0:HALO]

        gel, dgel = _gelu_parts(ug)
        hsv = hs_ref[...]
        dcl = dcat[:, 512:1024]
        dhs = dcl * gel
        dug = dcl * hsv * dgel
        ul_ext = ext_ref[:, 512:1024]
        cw = cw_ref[...]
        xb, r, ig, a = lru_ref[:, 0:512], lru_ref[:, 512:1024], lru_ref[:, 1024:1536], lru_ref[:, 1536:2048]
        first_row = (tbk * tb + lax.broadcasted_iota(jnp.int32, (tb, 1), 0)) == 0
        c_l = LRU_C * _log_sigmoid(l_ref[...])
        a2, m2, mult = _lru_decay(r, a, c_l, first_row)
        b_ref[...] = dhs
        row = lax.broadcasted_iota(jnp.int32, (SUB, 512), 0)

        def group(jj, hnext):
            o = pl.multiple_of((tb // SUB - 1 - jj) * SUB, SUB)
            a8 = lru_ref[pl.ds(o, SUB), 1536:2048]
            d8 = b_ref[pl.ds(o, SUB), :]
            b8 = a8 * d8
            for sh in (1, 2, 4):
                ash = jnp.where(row < SUB - sh, pltpu.roll(a8, SUB - sh, 0), 1.0)
                bsh = jnp.where(row < SUB - sh, pltpu.roll(b8, SUB - sh, 0), 0.0)
                b8 = a8 * bsh + b8
                a8 = a8 * ash
            h8 = a8 * hnext + b8
            gs_ref[pl.ds(o, SUB), :] = d8 + jnp.where(row < SUB - 1, pltpu.roll(h8, SUB - 1, 0), hnext)
            return jnp.broadcast_to(h8[0:1, :], (SUB, 512))

        hc_ref[...] = lax.fori_loop(0, tb // SUB, group, hc_ref[...])
        gsum = gs_ref[...]
        hs_ext = jnp.concatenate([hsh_ref[...] * has_prev, hsv], axis=0)
        hprev = pltpu.roll(hs_ext, 1, 0)[SUB:]
        da = gsum * hprev
        dmult = jnp.where(first_row, 0.0, gsum * (ig * xb))
        di = gsum * mult * xb
        dxb = gsum * mult * ig
        dla = da * a - dmult * a2 * lax.rsqrt(m2)
        dr = dla * c_l
        v512_ref[3:4, :] += _colsum(dla * r)
        dgp = jnp.concatenate([dr * r * (1.0 - r), di * ig * (1.0 - ig)], axis=1)
        dgpb = dgp.astype(BF16)
        v1024_ref[0:1, :] += _colsum(dgp)
        dwg_acc[...] += _dot_tn(xb.astype(BF16), dgpb)
        dxb = dxb + _dot_nt(dgpb, wg_ref[...])
        n8 = tb + SUB
        dxb_ext = jnp.concatenate([dxb, dxbhead_ref[...]], axis=0)
        du_lru = (cw[3:4, :] * dxb + cw[2:3, :] * pltpu.roll(dxb_ext, n8 - 1, 0)[0:tb]
                  + cw[1:2, :] * pltpu.roll(dxb_ext, n8 - 2, 0)[0:tb] + cw[0:1, :] * pltpu.roll(dxb_ext, n8 - 3, 0)[0:tb])
        dxbhead_ref[...] = dxb[0:SUB]
        v512_ref[2:3, :] += _colsum(dxb)
        for j in range(4):
            shifted = ul_ext if j == 0 else pltpu.roll(ul_ext, j, 0)
            v512_ref[4 + (3 - j):5 + (3 - j), :] += _colsum(dxb * shifted[HALO:])

        dproj_ref[...] = jnp.concatenate([du_pool, du_lru, dug], axis=1).astype(BF16)

        @pl.when(i == nb - 1)
        def _():
            v512_ref[3:4, :] = v512_ref[3:4, :] * (LRU_C * _sigmoid(-l_ref[...]))
            pltpu.sync_copy(dwp_acc, dwp_out)
            pltpu.sync_copy(dwg_acc, dwg_out)
            wait_scatter()

    rev = lambda w: pl.BlockSpec((tb, w), lambda i: (nb - 1 - i, 0))
    halo = lambda rows, w: pl.BlockSpec((rows, w), lambda i: (jnp.maximum((nb - 1 - i) * (tb // rows) - 1, 0), 0))
    any_spec = pl.BlockSpec(memory_space=pl.ANY)
    smalls = [wp_bd, pool_b, pool_scale, conv_w, wg_bd, lru_l, w_out, _inv_count_head()]
    return pl.pallas_call(
        body, name="mix_bwd", grid=(nb,),
        in_specs=[rev(d), rev(1536), halo(HALO, 1536), rev(512), halo(SUB, 512), rev(2048)]
        + [_const_spec(s.shape) for s in smalls] + [any_spec] * len(flat_parts),
        out_specs=[rev(1536), pl.BlockSpec((8, 512), lambda i: (0, 0)), pl.BlockSpec((8, 1024), lambda i: (0, 0))]
        + [any_spec] * (2 + n_s),
        out_shape=[jax.ShapeDtypeStruct((t_len, 1536), BF16), jax.ShapeDtypeStruct((8, 512), F32),
                   jax.ShapeDtypeStruct((8, 1024), F32), jax.ShapeDtypeStruct(wp_bd.shape, F32),
                   jax.ShapeDtypeStruct(wg_bd.shape, F32)]
        + [jax.ShapeDtypeStruct((N_DEV,) + parts[0].shape[1:], parts[0].dtype) for parts in scatter_parts],
        scratch_shapes=[pltpu.VMEM(wp_bd.shape, F32), pltpu.VMEM(wg_bd.shape, F32), pltpu.VMEM((tb + HALO, 1024), F32),
                        pltpu.VMEM((tb, 512), F32), pltpu.VMEM((tb, 512), F32), pltpu.VMEM((HALO, 512), F32),
                        pltpu.VMEM((SUB, 512), F32), pltpu.VMEM((SUB, 512), F32)]
        + _exchange_scratch(n_s),
        compiler_params=_params(),
    )(dh1, proj, proj, hs, hs, lru_saved, *smalls, *flat_parts)


def _wgrad(name, a, b, tb):
    t_len, m = a.shape
    n = b.shape[1]
    nb = t_len // tb

    def body(a_ref, b_ref, out_ref, acc_ref, stage_ref):
        i = pl.program_id(0)

        @pl.when(i == 0)
        def _():
            acc_ref[...] = jnp.zeros_like(acc_ref)

        acc_ref[...] += _dot_tn(a_ref[...], b_ref[...].astype(BF16))

        @pl.when(i == nb - 1)
        def _():
            stage_ref[...] = acc_ref[...].astype(BF16)
            pltpu.sync_copy(stage_ref, out_ref)

    return pl.pallas_call(
        body, name=name, grid=(nb,),
        in_specs=[pl.BlockSpec((tb, m), lambda i: (i, 0)), pl.BlockSpec((tb, n), lambda i: (i, 0))],
        out_specs=pl.BlockSpec(memory_space=pl.ANY), out_shape=jax.ShapeDtypeStruct((m, n), BF16),
        scratch_shapes=[pltpu.VMEM((m, n), F32), pltpu.VMEM((m, n), BF16)], compiler_params=_params(),
    )(a, b)


def _in_bwd(dproj, z1, x, dh1, g_mix, w_in, tb):
    t_len, d = x.shape
    nb = t_len // tb

    def body(dproj_ref, z1_ref, x_ref, dh1_ref, g_ref, win_ref, dx_ref, vec_ref, dwin_out, dwin_acc, dwin_stage):
        i = pl.program_id(0)

        @pl.when(i == 0)
        def _():
            vec_ref[...] = jnp.zeros_like(vec_ref)
            dwin_acc[...] = jnp.zeros_like(dwin_acc)

        dprojb = dproj_ref[...]
        dwin_acc[...] += _dot_tn(z1_ref[...], dprojb)
        dz1 = _dot_nt(dprojb, win_ref[...])
        g = g_ref[...]
        _, xh, rr = _rms_fwd(x_ref[...], g)
        dx_ref[...] = dh1_ref[...] + _rms_bwd(xh, rr, g, dz1)
        vec_ref[0:1, :] += _colsum(dz1 * xh)

        @pl.when(i == nb - 1)
        def _():
            dwin_stage[...] = dwin_acc[...].astype(BF16)
            pltpu.sync_copy(dwin_stage, dwin_out)

    row_spec = lambda w: pl.BlockSpec((tb, w), lambda i: (i, 0))
    return pl.pallas_call(
        body, name="in_bwd", grid=(nb,),
        in_specs=[row_spec(dproj.shape[1]), row_spec(d), row_spec(d), row_spec(d), _const_spec(g_mix.shape),
                  _const_spec(w_in.shape)],
        out_specs=[row_spec(d), pl.BlockSpec((8, d), lambda i: (0, 0)), pl.BlockSpec(memory_space=pl.ANY)],
        out_shape=[jax.ShapeDtypeStruct((t_len, d), F32), jax.ShapeDtypeStruct((8, d), F32),
                   jax.ShapeDtypeStruct(w_in.shape, BF16)],
        scratch_shapes=[pltpu.VMEM(w_in.shape, F32), pltpu.VMEM(w_in.shape, BF16)], compiler_params=_params(),
    )(dproj, z1, x, dh1, g_mix, w_in)


def _exchange(name, gathered, scattered):
    n_g, n = len(gathered), len(gathered) + len(scattered)
    srcs = list(gathered) + list(scattered)
    shapes = [a.shape for a in gathered] + [a.shape[1:] for a in scattered]

    def body(*refs):
        ins, outs = refs[:n], refs[n:2 * n]
        srcs_of = [(lambda s, r=r: r) for r in ins[:n_g]] + [(lambda s, r=r: r.at[s]) for r in ins[n_g:]]
        start, wait = _all_to_all(srcs_of, outs, *refs[2 * n:])
        start()
        wait()

    any_spec = pl.BlockSpec(memory_space=pl.ANY)
    return pl.pallas_call(
        body, name=name, in_specs=[any_spec] * n, out_specs=[any_spec] * n,
        out_shape=[jax.ShapeDtypeStruct((N_DEV,) + tuple(sh), a.dtype) for sh, a in zip(shapes, srcs)],
        scratch_shapes=_exchange_scratch(n),
    )(*srcs)


def _final_exchange(name, small, scattered):
    n = len(scattered)
    rows = small.shape[0]
    per = rows // N_DEV

    def body(*refs):
        small_ref = refs[0]
        ins = refs[1:1 + n]
        sum_ref = refs[1 + n]
        outs = refs[2 + n:2 + 2 * n]
        land_ref, send_sems, recv_sems, local_sems, small_send, small_recv = refs[2 + 2 * n:]
        me = _my_index()
        start, wait = _all_to_all([(lambda s, r=r: r.at[s]) for r in ins], outs, send_sems, recv_sems, local_sems)
        start()
        mine = pl.ds(pl.multiple_of(me * per, 8), per)

        def dev(s):
            return (s // 4, (s // 2) % 2, s % 2)

        def partial_to(s):
            return pltpu.make_async_remote_copy(
                src_ref=small_ref.at[pl.ds(s * per, per)], dst_ref=land_ref.at[me], send_sem=small_send.at[0, s],
                recv_sem=small_recv.at[0, me], device_id=dev(s), device_id_type=MESH)

        def partial_from(s):
            return pltpu.make_async_remote_copy(
                src_ref=small_ref.at[pl.ds(s * per, per)], dst_ref=land_ref.at[s], send_sem=small_send.at[0, s],
                recv_sem=small_recv.at[0, s], device_id=dev(s), device_id_type=MESH)

        def sum_to(s):
            return pltpu.make_async_remote_copy(
                src_ref=sum_ref.at[mine], dst_ref=sum_ref.at[mine], send_sem=small_send.at[1, s],
                recv_sem=small_recv.at[1, me], device_id=dev(s), device_id_type=MESH)

        def sum_from(s):
            rows_s = pl.ds(s * per, per)
            return pltpu.make_async_remote_copy(
                src_ref=sum_ref.at[rows_s], dst_ref=sum_ref.at[rows_s], send_sem=small_send.at[1, s],
                recv_sem=small_recv.at[1, s], device_id=dev(s), device_id_type=MESH)

        for s in range(N_DEV):
            @pl.when(s != me)
            def _():
                partial_to(s).start()
        land_ref[me] = small_ref[mine, :]
        for s in range(N_DEV):
            @pl.when(s != me)
            def _():
                partial_from(s).wait_recv()
        total = land_ref[0]
        for s in range(1, N_DEV):
            total = total + land_ref[s]
        sum_ref[mine, :] = total
        for s in range(N_DEV):
            @pl.when(s != me)
            def _():
                sum_to(s).start()
        for s in range(N_DEV):
            @pl.when(s != me)
            def _():
                sum_from(s).wait_recv()
                partial_to(s).wait_send()
                sum_to(s).wait_send()
        wait()

    any_spec = pl.BlockSpec(memory_space=pl.ANY)
    vmem_spec = pl.BlockSpec(memory_space=pltpu.VMEM)
    res = pl.pallas_call(
        body, name=name, in_specs=[vmem_spec] + [any_spec] * n, out_specs=[vmem_spec] + [any_spec] * n,
        out_shape=[jax.ShapeDtypeStruct(small.shape, F32)]
        + [jax.ShapeDtypeStruct(a.shape, a.dtype) for a in scattered],
        scratch_shapes=[pltpu.VMEM((N_DEV, per, small.shape[1]), F32)] + _exchange_scratch(n)
        + [pltpu.SemaphoreType.DMA((2, N_DEV)), pltpu.SemaphoreType.DMA((2, N_DEV))],
    )(small, *scattered)
    return res[0], res[1:]


def _adamw(name, parts, w, m, v, row_block):
    n_src, rows, cols = parts.shape
    rb = min(row_block, rows)

    def body(p_ref, w_ref, m_ref, v_ref, g_out, d_out, m_out, v_out):
        g = p_ref[0].astype(F32)
        for s in range(1, n_src):
            g = g + p_ref[s].astype(F32)
        m_new = ADAM_B1 * m_ref[...] + (1.0 - ADAM_B1) * g
        v_new = ADAM_B2 * v_ref[...] + (1.0 - ADAM_B2) * jnp.square(g)
        m_hat = m_new / (1.0 - ADAM_B1 ** ADAM_STEP)
        v_hat = v_new / (1.0 - ADAM_B2 ** ADAM_STEP)
        g_out[...] = g
        d_out[...] = -ADAM_LR * (m_hat / (jnp.sqrt(v_hat) + ADAM_EPS) + ADAM_WD * w_ref[...])
        m_out[...] = m_new
        v_out[...] = v_new

    spec = pl.BlockSpec((rb, cols), lambda i: (i, 0))
    return pl.pallas_call(
        body, name=name, grid=(rows // rb,),
        in_specs=[pl.BlockSpec((n_src, rb, cols), lambda i: (0, i, 0)), spec, spec, spec],
        out_specs=[spec] * 4, out_shape=[jax.ShapeDtypeStruct((rows, cols), F32)] * 4,
        compiler_params=pltpu.CompilerParams(dimension_semantics=("parallel",), vmem_limit_bytes=VMEM_LIMIT),
    )(parts, w, m, v)


def _block_diag(blocks):
    g, a, b = blocks.shape
    eye = jnp.eye(g, dtype=blocks.dtype)
    return (eye[:, None, :, None] * blocks[:, :, None, :]).reshape(g * a, g * b)


def _diag_blocks(mat, g):
    a, b = mat.shape[0] // g, mat.shape[1] // g
    m4 = mat.reshape(g, a, g, b)
    return jnp.stack([m4[k, :, k, :] for k in range(g)], axis=0)


def _pack(pieces, row_multiple=8, width=1024):
    flat = jnp.concatenate([p.reshape(-1) for p in pieces])
    rows = -(-flat.shape[0] // (row_multiple * width)) * row_multiple
    return jnp.pad(flat, (0, rows * width - flat.shape[0])).reshape(rows, width)


def _unpack(packed, shapes):
    flat = packed.reshape(-1)
    out, o = [], 0
    for sh in shapes:
        size = 1
        for k in sh:
            size *= k
        out.append(flat[o:o + size].reshape(sh))
        o += size
    return out


def _small_grads(v512, v1024, in_vec, mlp_vec, ple_vec, dwp_bd, dwg_bd, pool_b, gate_a_b, gate_x_b):
    return [
        in_vec[0:1],
        _diag_blocks(dwp_bd, N_POOL_GROUPS)[None],
        v512[1:2].reshape(pool_b.shape),
        v512[0:1],
        v512[2:3],
        _diag_blocks(dwg_bd[:, :512], LRU_HEADS)[None],
        v1024[0:1, :512].reshape(gate_a_b.shape),
        _diag_blocks(dwg_bd[:, 512:], LRU_HEADS)[None],
        v1024[0:1, 512:].reshape(gate_x_b.shape),
        v512[3:4],
        mlp_vec[0:1],
        ple_vec[1:2],
        ple_vec[0:1],
        ple_vec[2:3].reshape(-1),
        v512[4:8][None],
        ple_vec[3:4, 0:1].reshape(1),
    ]


def kernel(x, p, norm_mix_g, w_in, pool_w, pool_b, pool_scale, conv_w, conv_b, gate_a_w, gate_a_b, gate_x_w, gate_x_b, lru_L, w_out, norm_mlp_g, w_up, w_down, norm_ple_g, w_ple_gate, b_ple_gate, w_ple_proj, norm_final_g, loss_target, m_norm_mix_g, m_w_in, m_pool_w, m_pool_b, m_pool_scale, m_conv_w, m_conv_b, m_gate_a_w, m_gate_a_b, m_gate_x_w, m_gate_x_b, m_lru_L, m_w_out, m_norm_mlp_g, m_w_up, m_w_down, m_norm_ple_g, m_w_ple_gate, m_b_ple_gate, m_w_ple_proj, m_norm_final_g, v_norm_mix_g, v_w_in, v_pool_w, v_pool_b, v_pool_scale, v_conv_w, v_conv_b, v_gate_a_w, v_gate_a_b, v_gate_x_w, v_gate_x_b, v_lru_L, v_w_out, v_norm_mlp_g, v_w_up, v_w_down, v_norm_ple_g, v_w_ple_gate, v_b_ple_gate, v_w_ple_proj, v_norm_final_g):
    t_len, d = x.shape[1], x.shape[2]
    tbs = {k: min(v, t_len) for k, v in TIME_BLOCKS.items()}
    me = _my_index()

    win_g, wout_g, convw_g = _gather("gather_mixer_weights", [w_in[0].astype(BF16), w_out[0].astype(BF16), conv_w[0]])
    w_in_f = jnp.transpose(win_g, (1, 0, 2)).reshape(d, -1)
    conv_w_f = jnp.transpose(convw_g, (1, 0, 2)).reshape(convw_g.shape[1], -1)
    wp_bd = _block_diag(pool_w[0]).astype(BF16)
    wg_bd = jnp.concatenate([_block_diag(gate_a_w[0]), _block_diag(gate_x_w[0])], axis=1).astype(BF16)
    gate_b2 = jnp.concatenate([gate_a_b.reshape(1, -1), gate_x_b.reshape(1, -1)], axis=1)
    mixer_small = (norm_mix_g, w_in_f, wp_bd, pool_b.reshape(1, -1), pool_scale, conv_w_f, conv_b, wg_bd, gate_b2, lru_L,
                   wout_g.reshape(-1, d))

    x2 = x[0]
    later = [w_up[0].astype(BF16), w_down[0].astype(BF16), w_ple_gate[0].astype(BF16), w_ple_proj[0].astype(BF16)]
    h1, z1, proj, hs, cat, lru_saved, wup_g, wdn_g, wgate_g, wproj_g = _mix_fwd(
        x2, *mixer_small, later, GATHER_FORWARD_AT, tbs['mix_fwd'])
    w_down_f = wdn_g.reshape(-1, d)
    w_proj_f = jnp.transpose(wproj_g, (1, 0, 2)).reshape(wproj_g.shape[1], -1)
    h2, z2, up = _mlp_fwd(h1, norm_mlp_g, wup_g, w_down_f, tbs['mlp_fwd'])
    dh2, ple_vec, dw_gate, dw_proj = _ple(h2, p[0, 0], loss_target[0], norm_ple_g, wgate_g.reshape(-1, d), b_ple_gate,
                                          w_proj_f, norm_final_g.reshape(1, -1), tbs['ple'])
    dz2 = None
    dw_up_parts, dw_down_parts = [], []
    for part in range(MLP_BWD_SPLIT):
        res = _mlp_bwd_part(part, MLP_BWD_SPLIT, dh2, z2, up, wup_g, w_down_f, dz2, h1, norm_mlp_g, tbs['mlp_bwd'])
        dz2 = res[0]
        dw_up_parts.append(res[-2])
        dw_down_parts.append(res[-1])
    dh1, mlp_vec = res[0], res[1]
    n_proj = w_ple_proj.shape[2]
    dw_out = _wgrad("wgrad_out", cat, dh1, tbs['wgrad_out'])
    early = [dw_up_parts, dw_down_parts, [dw_gate.reshape(N_DEV, -1, d)],
             [jnp.transpose(dw_proj.reshape(-1, N_DEV, n_proj), (1, 0, 2))], [dw_out.reshape(N_DEV, -1, d)]]
    (dproj, v512, v1024, dwp_bd, dwg_bd,
     recv_up, recv_down, recv_gate, recv_proj, recv_out) = _mix_bwd(
        dh1, proj, hs, lru_saved, wp_bd, pool_b.reshape(1, -1), pool_scale, conv_w_f, wg_bd, lru_L, wout_g.reshape(-1, d),
        early, tbs['mix_bwd'])
    dx, in_vec, dw_in = _in_bwd(dproj, z1, x2, dh1, norm_mix_g, w_in_f, tbs['in_bwd'])

    small_grads = _small_grads(v512, v1024, in_vec, mlp_vec, ple_vec, dwp_bd, dwg_bd, pool_b, gate_a_b, gate_x_b)
    n_in = w_in.shape[2]
    small_sum, (recv_in,) = _final_exchange(
        "exchange_last_grads", _pack(small_grads, row_multiple=8 * N_DEV),
        [jnp.transpose(dw_in.reshape(d, N_DEV, n_in), (1, 0, 2))])
    received = [recv_in, recv_out, recv_up, recv_down, recv_gate, recv_proj]
    small_w = [norm_mix_g, pool_w, pool_b, pool_scale, conv_b, gate_a_w, gate_a_b, gate_x_w, gate_x_b, lru_L,
               norm_mlp_g, norm_ple_g, b_ple_gate, norm_final_g]
    small_m = [m_norm_mix_g, m_pool_w, m_pool_b, m_pool_scale, m_conv_b, m_gate_a_w, m_gate_a_b, m_gate_x_w, m_gate_x_b,
               m_lru_L, m_norm_mlp_g, m_norm_ple_g, m_b_ple_gate, m_norm_final_g]
    small_v = [v_norm_mix_g, v_pool_w, v_pool_b, v_pool_scale, v_conv_b, v_gate_a_w, v_gate_a_b, v_gate_x_w, v_gate_x_b,
               v_lru_L, v_norm_mlp_g, v_norm_ple_g, v_b_ple_gate, v_norm_final_g]
    small_shapes = [a.shape for a in small_w]
    conv_full_shape = (1,) + conv_w_f.shape

    shard_w = [w_in[0], w_out[0], w_up[0], w_down[0], w_ple_gate[0], w_ple_proj[0]]
    shard_m = [m_w_in[0], m_w_out[0], m_w_up[0], m_w_down[0], m_w_ple_gate[0], m_w_ple_proj[0]]
    shard_v = [v_w_in[0], v_w_out[0], v_w_up[0], v_w_down[0], v_w_ple_gate[0], v_w_ple_proj[0]]
    names = ["w_in", "w_out", "w_up", "w_down", "w_ple_gate", "w_ple_proj"]
    big_res = {}
    for nm, parts, w_s, m_s, v_s in zip(names, received, shard_w, shard_m, shard_v):
        big_res[nm] = [r[None] for r in _adamw("adamw_" + nm, parts, w_s, m_s, v_s, 128)]

    summed = _unpack(small_sum, small_shapes + [conv_full_shape, (1,)])
    loss = summed[-1][0]
    conv_g = lax.dynamic_slice_in_dim(summed[-2], me * conv_w.shape[2], conv_w.shape[2], axis=2)
    sg = summed[:-2] + [conv_g]
    sw, sm, sv = small_w + [conv_w], small_m + [m_conv_w], small_v + [v_conv_w]
    shapes2 = small_shapes + [conv_w.shape]
    res = _adamw("adamw_small", _pack(sg)[None], _pack(sw), _pack(sm), _pack(sv), 1024)
    sres = [_unpack(r, shapes2) for r in res]
    small_names = ["norm_mix_g", "pool_w", "pool_b", "pool_scale", "conv_b", "gate_a_w", "gate_a_b", "gate_x_w",
                   "gate_x_b", "lru_L", "norm_mlp_g", "norm_ple_g", "b_ple_gate", "norm_final_g", "conv_w"]
    order = ["norm_mix_g", "w_in", "pool_w", "pool_b", "pool_scale", "conv_w", "conv_b", "gate_a_w", "gate_a_b",
             "gate_x_w", "gate_x_b", "lru_L", "w_out", "norm_mlp_g", "w_up", "w_down", "norm_ple_g", "w_ple_gate",
             "b_ple_gate", "w_ple_proj", "norm_final_g"]
    outs = [loss, dx[None]]
    for kind in range(4):
        for nm in order:
            if nm in big_res:
                outs.append(big_res[nm][kind])
            else:
                outs.append(sres[kind][small_names.index(nm)])
    return tuple(outs)
```

```python
import functools

import jax
import jax.numpy as jnp
from jax import lax
from jax.experimental import pallas as pl
from jax.experimental.pallas import tpu as pltpu

F32 = jnp.float32
BF16 = jnp.bfloat16
MESH = pl.DeviceIdType.MESH

N_DEV = 8
RMS_EPS = 1e-6
LRU_C = 8.0
POOL_WINDOWS = (2, 4, 8, 16)
N_POOL_GROUPS = 4
LRU_HEADS = 8
HALO = 16
SUB = 8
GELU_C0 = 0.7978845608028654
GELU_C1 = 0.044715

ADAM_LR = 0.001
ADAM_B1 = 0.9
ADAM_B2 = 0.999
ADAM_EPS = 1e-08
ADAM_WD = 0.01
ADAM_STEP = 10

VMEM_LIMIT = 60 * 1024 * 1024
TIME_BLOCKS = dict(mix_fwd=512, mlp_fwd=512, ple=512, mlp_bwd=512, wgrad_out=1024, mix_bwd=512, in_bwd=512)
MLP_BWD_SPLIT = 2
GATHER_FORWARD_AT = (0.5, 0.875, 1.0, 1.0)


def _params(n_arbitrary=1):
    return pltpu.CompilerParams(dimension_semantics=("arbitrary",) * n_arbitrary, vmem_limit_bytes=VMEM_LIMIT)


def _dot(a, b):
    return jnp.dot(a, b, preferred_element_type=F32)


def _dot_nt(a, b):
    return lax.dot_general(a, b, (((1,), (1,)), ((), ())), preferred_element_type=F32)


def _dot_tn(a, b):
    return lax.dot_general(a, b, (((0,), (0,)), ((), ())), preferred_element_type=F32)


def _rms_fwd(x, g):
    r = lax.rsqrt(jnp.mean(x * x, axis=-1, keepdims=True) + RMS_EPS)
    xh = x * r
    return xh * g, xh, r


def _rms_bwd(xh, r, g, dz):
    dxh = dz * g
    return r * (dxh - xh * jnp.mean(dxh * xh, axis=-1, keepdims=True))


def _colsum(a):
    return jnp.sum(a, axis=0, keepdims=True)


def _sigmoid(a):
    return 0.5 * jnp.tanh(0.5 * a) + 0.5


def _gelu_parts(u):
    u2 = u * u
    th = jnp.tanh(GELU_C0 * (u + GELU_C1 * u * u2))
    gel = 0.5 * u * (1.0 + th)
    dgel = 0.5 * (1.0 + th) + 0.5 * u * (1.0 - th * th) * (GELU_C0 * (1.0 + 3.0 * GELU_C1 * u2))
    return gel, dgel


def _my_index():
    return 4 * lax.axis_index("x") + 2 * lax.axis_index("y") + lax.axis_index("c")


def _all_to_all(srcs_of, dsts, send_sems, recv_sems, local_sems, dests=None):
    n = len(dsts)
    me = _my_index()
    dests = [list(range(N_DEV))] * n if dests is None else dests

    def remote(t, s):
        return pltpu.make_async_remote_copy(
            src_ref=srcs_of[t](s), dst_ref=dsts[t].at[me], send_sem=send_sems.at[t, s], recv_sem=recv_sems.at[t, me],
            device_id=(s // 4, (s // 2) % 2, s % 2), device_id_type=MESH)

    def arrival(t, s):
        return pltpu.make_async_remote_copy(
            src_ref=srcs_of[t](dests[t][0]), dst_ref=dsts[t].at[s], send_sem=send_sems.at[t, s],
            recv_sem=recv_sems.at[t, s], device_id=(s // 4, (s // 2) % 2, s % 2), device_id_type=MESH)

    def local(t, s):
        return pltpu.make_async_copy(srcs_of[t](s), dsts[t].at[s], local_sems.at[t])

    def start():
        for s in range(N_DEV):
            to_s = [t for t in range(n) if s in dests[t]]

            @pl.when(s == me)
            def _():
                for t in to_s:
                    local(t, s).start()

            @pl.when(s != me)
            def _():
                for t in to_s:
                    remote(t, s).start()

    def wait():
        for s in range(N_DEV):
            to_s = [t for t in range(n) if s in dests[t]]

            @pl.when(s == me)
            def _():
                for t in to_s:
                    local(t, s).wait()
                    for src in range(N_DEV):
                        if src != s:
                            arrival(t, src).wait_recv()

            @pl.when(s != me)
            def _():
                for t in to_s:
                    remote(t, s).wait_send()

    return start, wait


N_GATHER_COPIES = 7


def _core_major_slot(dev):
    return 4 * dev[2] + 2 * dev[0] + dev[1]


def _device_of_core_major_slot(k):
    return (k % 4) * 2 + k // 4


def _two_level_gather(srcs, dsts, send_sems, recv_sems, local_sems, slots=None):
    n = len(dsts)
    x, y, c = lax.axis_index("x"), lax.axis_index("y"), lax.axis_index("c")
    me, sibling = (x, y, c), (x, y, 1 - c)
    chips = [(1 - x, y), (x, 1 - y), (1 - x, 1 - y)]

    def slot(t, dev):
        return 4 * dev[0] + 2 * dev[1] + dev[2] if slots is None or slots[t] is None else slots[t](dev)

    def copy(t, k, block, to, src=None):
        return pltpu.make_async_remote_copy(
            src_ref=dsts[t].at[slot(t, block)] if src is None else src, dst_ref=dsts[t].at[slot(t, block)],
            send_sem=send_sems.at[t, k], recv_sem=recv_sems.at[t, k], device_id=to, device_id_type=MESH)

    def local(t):
        return pltpu.make_async_copy(srcs[t], dsts[t].at[slot(t, me)], local_sems.at[t])

    def start():
        for t in range(n):
            local(t).start()
            for j, chip in enumerate(chips):
                copy(t, 1 + j, me, (*chip, c), src=srcs[t]).start()
            copy(t, 0, me, sibling, src=srcs[t]).start()

    def forward(t):
        for j, chip in enumerate(chips):
            copy(t, 1 + j, (*chip, c), me).wait_recv()
            copy(t, 4 + j, (*chip, c), sibling).start()

    def finish():
        for t in range(n):
            copy(t, 0, sibling, me).wait_recv()
            for j, chip in enumerate(chips):
                copy(t, 4 + j, (*chip, 1 - c), me).wait_recv()
            copy(t, 0, me, sibling, src=srcs[t]).wait_send()
            for j, chip in enumerate(chips):
                copy(t, 1 + j, me, (*chip, c), src=srcs[t]).wait_send()
                copy(t, 4 + j, (*chip, c), sibling).wait_send()
            local(t).wait()

    return start, forward, finish


def _gather_scratch(n):
    return [pltpu.SemaphoreType.DMA((n, N_GATHER_COPIES)), pltpu.SemaphoreType.DMA((n, N_GATHER_COPIES)),
            pltpu.SemaphoreType.DMA((n,))]


def _gather(name, srcs):
    n = len(srcs)

    def body(*refs):
        start, forward, finish = _two_level_gather(refs[:n], refs[n:2 * n], *refs[2 * n:])
        start()
        for t in range(n):
            forward(t)
        finish()

    any_spec = pl.BlockSpec(memory_space=pl.ANY)
    return pl.pallas_call(
        body, name=name, in_specs=[any_spec] * n, out_specs=[any_spec] * n,
        out_shape=[jax.ShapeDtypeStruct((N_DEV,) + a.shape, a.dtype) for a in srcs], scratch_shapes=_gather_scratch(n),
    )(*srcs)


def _scatter_plan(blocks, dests, landing):
    return dict(blocks=list(blocks), dests=[list(dd) for dd in dests], landing=list(landing))


def _scatter_args(plan):
    return plan['blocks'] + [a for a in plan['landing'] if a is not None]


def _scatter_out_shape(plan):
    return [jax.ShapeDtypeStruct((N_DEV,) + b.shape[1:], b.dtype) for b in plan['blocks']]


def _scatter_aliases(plan, first_in, first_out):
    given = [t for t, a in enumerate(plan['landing']) if a is not None]
    return {first_in + len(plan['blocks']) + k: first_out + t for k, t in enumerate(given)}


def _scatter_ops(plan, in_refs, out_refs, sems):
    n = len(plan['blocks'])
    srcs_of = [(lambda s, r=in_refs[t], dd=plan['dests'][t]: r.at[dd.index(s)]) for t in range(n)]
    return _all_to_all(srcs_of, out_refs, *sems, dests=plan['dests'])


def _exchange_scratch(n):
    return [pltpu.SemaphoreType.DMA((n, N_DEV)), pltpu.SemaphoreType.DMA((n, N_DEV)), pltpu.SemaphoreType.DMA((n,))]


def _const_spec(shape):
    nd = len(shape)
    return pl.BlockSpec(shape, lambda i: (0,) * nd, pipeline_mode=pl.Buffered(1))


def _pool_windows(up_ext, n, forward):
    sh = (lambda k: k) if forward else (lambda k: n - k)
    s2 = up_ext + pltpu.roll(up_ext, sh(1), 0)
    t4 = s2[:, 128:]
    s4 = t4 + pltpu.roll(t4, sh(2), 0)
    t8 = s4[:, 128:]
    s8 = t8 + pltpu.roll(t8, sh(4), 0)
    t16 = s8[:, 128:]
    s16 = t16 + pltpu.roll(t16, sh(8), 0)
    return jnp.concatenate([s2[:, :128], s4[:, :128], s8[:, :128], s16], axis=1)


def _inv_count_head():
    t = jnp.arange(1, HALO + 1, dtype=F32)[:, None]
    return jnp.concatenate([jnp.broadcast_to(1.0 / jnp.minimum(t, float(w)), (HALO, 128)) for w in POOL_WINDOWS], axis=1)


def _scale_by_inv_count(v, is_first_block, inv_head):
    inv_row = jnp.concatenate([jnp.full((1, 128), 1.0 / w, F32) for w in POOL_WINDOWS], axis=1)
    head = v[0:HALO] * jnp.where(is_first_block, inv_head, inv_row)
    return jnp.concatenate([head, v[HALO:] * inv_row], axis=0)


def _lru_decay(r, a, c_l, first_row):
    a2 = a * a
    m2 = -jnp.tanh(c_l * r) * (a2 + 1.0)
    return a2, m2, jnp.where(first_row, 1.0, jnp.sqrt(m2))


def _log_sigmoid(v):
    return -(jnp.maximum(-v, 0.0) + jnp.log1p(jnp.exp(-jnp.abs(v))))


def _conv_fwd(ul_ext, cw, cb):
    return (cb + cw[3:4, :] * ul_ext + cw[2:3, :] * pltpu.roll(ul_ext, 1, 0)
            + cw[1:2, :] * pltpu.roll(ul_ext, 2, 0) + cw[0:1, :] * pltpu.roll(ul_ext, 3, 0))


def _mix_fwd(x, g_mix, w_in, wp_bd, pool_b, pool_scale, conv_w, conv_b, wg_bd, gate_b, lru_l, w_out, gather_srcs,
             gather_slots, forward_at, tb):
    t_len, d = x.shape
    nb = t_len // tb
    n_g = len(gather_srcs)
    forward_step = [min(nb - 1, int(f * nb)) for f in forward_at]

    def body(*refs):
        (x_ref, g_ref, win_ref, wp_ref, pb_ref, ps_ref, cw_ref, cb_ref, wg_ref, gb_ref, l_ref, wout_ref,
         invh_ref) = refs[:13]
        gsrc = refs[13:13 + n_g]
        h1_ref, z1_ref, proj_ref, hs_ref, cat_ref, lru_ref = refs[13 + n_g:19 + n_g]
        gdst = refs[19 + n_g:19 + 2 * n_g]
        ext_ref, a_ref, b_ref, hc_ref, send_sems, recv_sems, local_sems = refs[19 + 2 * n_g:]
        i = pl.program_id(0)
        start_gather, forward_gather, finish_gather = _two_level_gather(gsrc, gdst, send_sems, recv_sems, local_sems,
                                                                        gather_slots)

        @pl.when(i == 0)
        def _():
            start_gather()
            ext_ref[0:HALO, :] = jnp.zeros((HALO, 1024), F32)
            hc_ref[...] = jnp.zeros_like(hc_ref)

        xv = x_ref[...]
        z, _, _ = _rms_fwd(xv, g_ref[...])
        zb = z.astype(BF16)
        z1_ref[...] = zb
        proj = _dot(zb, win_ref[...])
        proj_ref[...] = proj
        ext_ref[HALO:, :] = proj[:, 0:1024]
        ug = proj[:, 1024:1536]
        n = tb + HALO
        up_ext = ext_ref[:, 0:512]
        win = _pool_windows(up_ext, n, True)[HALO:]
        dpool = _scale_by_inv_count(win, i == 0, invh_ref[...]) - proj[:, 0:512]
        q = _dot(dpool.astype(BF16), wp_ref[...]) + pb_ref[...]
        y_pool = q * ps_ref[...]
        xb = _conv_fwd(ext_ref[:, 512:1024], cw_ref[...], cb_ref[...])[HALO:]
        first_row = (i * tb + lax.broadcasted_iota(jnp.int32, (tb, 1), 0)) == 0
        c_l = LRU_C * _log_sigmoid(l_ref[...])
        gp = _dot(xb.astype(BF16), wg_ref[...]) + gb_ref[...]
        r = _sigmoid(gp[:, :512])
        ig = _sigmoid(gp[:, 512:])
        a = jnp.exp(c_l * r)
        _, _, mult = _lru_decay(r, a, c_l, first_row)
        lru_ref[:, 0:512] = xb
        lru_ref[:, 512:1024] = r
        lru_ref[:, 1024:1536] = ig
        lru_ref[:, 1536:2048] = a
        a_ref[...] = a
        b_ref[...] = mult * (ig * xb)
        row = lax.broadcasted_iota(jnp.int32, (SUB, 512), 0)

        def group(j, hprev):
            o = pl.multiple_of(j * SUB, SUB)
            a8 = a_ref[pl.ds(o, SUB), :]
            b8 = b_ref[pl.ds(o, SUB), :]
            for sh in (1, 2, 4):
                ash = jnp.where(row >= sh, pltpu.roll(a8, sh, 0), 1.0)
                bsh = jnp.where(row >= sh, pltpu.roll(b8, sh, 0), 0.0)
                b8 = a8 * bsh + b8
                a8 = a8 * ash
            h8 = a8 * hprev + b8
            hs_ref[pl.ds(o, SUB), :] = h8
            return jnp.broadcast_to(h8[SUB - 1:SUB, :], (SUB, 512))

        hc_ref[...] = lax.fori_loop(0, tb // SUB, group, hc_ref[...])
        gel, _ = _gelu_parts(ug)
        y_lru = hs_ref[...] * gel
        catb = jnp.concatenate([y_pool, y_lru], axis=1).astype(BF16)
        cat_ref[...] = catb
        h1_ref[...] = xv + _dot(catb, wout_ref[...])
        ext_ref[0:HALO, :] = ext_ref[tb:tb + HALO, :]

        for t in range(n_g):
            @pl.when(i == forward_step[t])
            def _():
                forward_gather(t)

        @pl.when(i == nb - 1)
        def _():
            finish_gather()

    row_spec = lambda w: pl.BlockSpec((tb, w), lambda i: (i, 0))
    any_spec = pl.BlockSpec(memory_space=pl.ANY)
    smalls = [g_mix, w_in, wp_bd, pool_b, pool_scale, conv_w, conv_b, wg_bd, gate_b, lru_l, w_out, _inv_count_head()]
    return pl.pallas_call(
        body, name="mix_fwd", grid=(nb,),
        in_specs=[row_spec(d)] + [_const_spec(s.shape) for s in smalls] + [any_spec] * n_g,
        out_specs=[row_spec(d), row_spec(d), row_spec(1536), row_spec(512), row_spec(1024), row_spec(2048)]
        + [any_spec] * n_g,
        out_shape=[jax.ShapeDtypeStruct((t_len, d), F32), jax.ShapeDtypeStruct((t_len, d), BF16),
                   jax.ShapeDtypeStruct((t_len, 1536), F32), jax.ShapeDtypeStruct((t_len, 512), F32),
                   jax.ShapeDtypeStruct((t_len, 1024), BF16), jax.ShapeDtypeStruct((t_len, 2048), F32)]
        + [jax.ShapeDtypeStruct((N_DEV,) + s.shape, s.dtype) for s in gather_srcs],
        scratch_shapes=[pltpu.VMEM((tb + HALO, 1024), F32), pltpu.VMEM((tb, 512), F32), pltpu.VMEM((tb, 512), F32),
                        pltpu.VMEM((SUB, 512), F32)] + _gather_scratch(n_g),
        compiler_params=_params(),
    )(x, *smalls, *gather_srcs)


def _mlp_fwd(h1, g_mlp, w_up, w_down, tb):
    t_len, d = h1.shape
    nb = t_len // tb
    n_chunk, _, fc = w_up.shape

    def body(h1_ref, g_ref, wup_ref, wdn_ref, h2_ref, z2_ref, up_ref):
        xv = h1_ref[...]
        z, _, _ = _rms_fwd(xv, g_ref[...])
        zb = z.astype(BF16)
        z2_ref[...] = zb
        acc = xv
        for c in range(n_chunk):
            u = _dot(zb, wup_ref[c])
            up_ref[:, c * fc:(c + 1) * fc] = u.astype(BF16)
            act = jnp.square(jnp.maximum(u, 0.0)).astype(BF16)
            acc = acc + _dot(act, wdn_ref[c * fc:(c + 1) * fc, :])
        h2_ref[...] = acc

    row_spec = lambda w: pl.BlockSpec((tb, w), lambda i: (i, 0))
    return pl.pallas_call(
        body, name="mlp_fwd", grid=(nb,),
        in_specs=[row_spec(d), _const_spec(g_mlp.shape), _const_spec(w_up.shape), _const_spec(w_down.shape)],
        out_specs=[row_spec(d), row_spec(d), row_spec(n_chunk * fc)],
        out_shape=[jax.ShapeDtypeStruct((t_len, d), F32), jax.ShapeDtypeStruct((t_len, d), BF16),
                   jax.ShapeDtypeStruct((t_len, n_chunk * fc), BF16)],
        compiler_params=_params(),
    )(h1, g_mlp, w_up, w_down)


def _ple(h2, p, target, g_ple, w_gate, b_gate, w_proj, g_final, tb):
    t_len, d = h2.shape
    nb = t_len // tb
    pd = p.shape[1]

    def body(h2_ref, p_ref, tgt_ref, g_ref, wg_ref, bg_ref, wp_ref, gf_ref,
             dh2_ref, vec_ref, dwg_out, dwp_out, dwg_acc, dwp_acc, dwg_stage, dwp_stage):
        i = pl.program_id(0)

        @pl.when(i == 0)
        def _():
            vec_ref[...] = jnp.zeros_like(vec_ref)
            dwg_acc[...] = jnp.zeros_like(dwg_acc)
            dwp_acc[...] = jnp.zeros_like(dwp_acc)

        h2 = h2_ref[...]
        g2 = g_ref[...]
        z3, xh2, r2 = _rms_fwd(h2, g2)
        z3b = z3.astype(BF16)
        gate = _sigmoid(_dot(z3b, wg_ref[...]) + bg_ref[...])
        pb = p_ref[...].astype(BF16)
        pp = _dot(pb, wp_ref[...])
        h3 = h2 + gate * pp
        gf = gf_ref[...]
        y, xh3, r3 = _rms_fwd(h3, gf)
        err = y - tgt_ref[...]
        loss_rows = jnp.mean(err * err, axis=-1, keepdims=True)
        dy = err * (1.0 / d)
        dh3 = _rms_bwd(xh3, r3, gf, dy)
        dgl = (dh3 * pp) * (gate * (1.0 - gate))
        dpp = dh3 * gate
        dglb = dgl.astype(BF16)
        dwg_acc[...] += _dot_tn(z3b, dglb)
        dwp_acc[...] += _dot_tn(pb, dpp.astype(BF16))
        dz3 = _dot_nt(dglb, wg_ref[...])
        dh2_ref[...] = dh3 + _rms_bwd(xh2, r2, g2, dz3)
        vec_ref[0:1, :] += _colsum(dgl)
        vec_ref[1:2, :] += _colsum(dz3 * xh2)
        vec_ref[2:3, :] += _colsum(dy * xh3)
        vec_ref[3:4, :] += 0.5 * jnp.sum(loss_rows)

        @pl.when(i == nb - 1)
        def _():
            dwg_stage[...] = dwg_acc[...].astype(BF16)
            dwp_stage[...] = dwp_acc[...].astype(BF16)
            pltpu.sync_copy(dwg_stage, dwg_out)
            pltpu.sync_copy(dwp_stage, dwp_out)

    row_spec = lambda w: pl.BlockSpec((tb, w), lambda i: (i, 0))
    any_spec = pl.BlockSpec(memory_space=pl.ANY)
    smalls = [g_ple, w_gate, b_gate, w_proj, g_final]
    return pl.pallas_call(
        body, name="ple_fwd_bwd", grid=(nb,),
        in_specs=[row_spec(d), row_spec(pd), row_spec(d)] + [_const_spec(s.shape) for s in smalls],
        out_specs=[row_spec(d), pl.BlockSpec((8, d), lambda i: (0, 0)), any_spec, any_spec],
        out_shape=[jax.ShapeDtypeStruct((t_len, d), F32), jax.ShapeDtypeStruct((8, d), F32),
                   jax.ShapeDtypeStruct(w_gate.shape, BF16), jax.ShapeDtypeStruct(w_proj.shape, BF16)],
        scratch_shapes=[pltpu.VMEM(w_gate.shape, F32), pltpu.VMEM(w_proj.shape, F32), pltpu.VMEM(w_gate.shape, BF16),
                        pltpu.VMEM(w_proj.shape, BF16)],
        compiler_params=_params(),
    )(h2, p, target, *smalls)


def _mlp_bwd_part(part, n_part, dh2, z2, up, w_up, w_down, dz2_prev, h1, g_mlp, scatter, tb):
    t_len, d = dh2.shape
    nb = t_len // tb
    n_chunk_all, _, fc = w_up.shape
    n_chunk = n_chunk_all // n_part
    first, last = part == 0, part == n_part - 1

    def body(*refs):
        refs = list(refs)
        dh2_ref, z2_ref, up_ref, wup_ref, wdn_ref = refs[:5]
        del refs[:5]
        dzp_ref = None if first else refs.pop(0)
        h1_ref, g_ref = (refs.pop(0), refs.pop(0)) if last else (None, None)
        scatter_in = [refs.pop(0) for _ in _scatter_args(scatter)]
        out_ref = refs.pop(0)
        vec_ref = refs.pop(0) if last else None
        dwup_out, dwdn_out = refs.pop(0), refs.pop(0)
        scatter_out = [refs.pop(0) for _ in scatter['blocks']]
        dwup_acc, dwdn_acc, up_stage, dn_stage = refs[:4]
        start_scatter, wait_scatter = _scatter_ops(scatter, scatter_in, scatter_out, refs[4:])
        i = pl.program_id(0)

        @pl.when(i == 0)
        def _():
            start_scatter()
            dwup_acc[...] = jnp.zeros_like(dwup_acc)
            dwdn_acc[...] = jnp.zeros_like(dwdn_acc)
            if last:
                vec_ref[...] = jnp.zeros_like(vec_ref)

        dh2 = dh2_ref[...]
        dh2b = dh2.astype(BF16)
        z2b = z2_ref[...]
        dz2 = jnp.zeros((tb, d), F32) if first else dzp_ref[...]
        for c in range(n_chunk):
            u = up_ref[:, c * fc:(c + 1) * fc].astype(F32)
            ur = jnp.maximum(u, 0.0)
            dact = _dot_nt(dh2b, wdn_ref[c * fc:(c + 1) * fc, :])
            dupb = (dact * (2.0 * ur)).astype(BF16)
            dwdn_acc[c * fc:(c + 1) * fc, :] += _dot_tn((ur * ur).astype(BF16), dh2b)
            dwup_acc[c] += _dot_tn(z2b, dupb)
            dz2 = dz2 + _dot_nt(dupb, wup_ref[c])
        if last:
            g = g_ref[...]
            _, xh, r = _rms_fwd(h1_ref[...], g)
            out_ref[...] = dh2 + _rms_bwd(xh, r, g, dz2)
            vec_ref[0:1, :] += _colsum(dz2 * xh)
        else:
            out_ref[...] = dz2

        @pl.when(i == nb - 1)
        def _():
            for c in range(n_chunk):
                up_stage[...] = dwup_acc[c].astype(BF16)
                dn_stage[...] = dwdn_acc[c * fc:(c + 1) * fc, :].astype(BF16)
                pltpu.sync_copy(up_stage, dwup_out.at[c])
                pltpu.sync_copy(dn_stage, dwdn_out.at[c])
            wait_scatter()

    row_spec = lambda w: pl.BlockSpec((tb, w), lambda i: (i, 0))
    any_spec = pl.BlockSpec(memory_space=pl.ANY)
    args = [dh2, z2, up, w_up, w_down]
    in_specs = [row_spec(d), row_spec(d), pl.BlockSpec((tb, n_chunk * fc), lambda i: (i, part)),
                pl.BlockSpec((n_chunk, d, fc), lambda i: (part, 0, 0), pipeline_mode=pl.Buffered(1)),
                pl.BlockSpec((n_chunk * fc, d), lambda i: (part, 0), pipeline_mode=pl.Buffered(1))]
    if not first:
        args.append(dz2_prev)
        in_specs.append(row_spec(d))
    if last:
        args += [h1, g_mlp]
        in_specs += [row_spec(d), _const_spec(g_mlp.shape)]
    n_in = len(args)
    args += _scatter_args(scatter)
    in_specs += [any_spec] * len(_scatter_args(scatter))
    out_specs = [row_spec(d)]
    out_shape = [jax.ShapeDtypeStruct((t_len, d), F32)]
    if last:
        out_specs.append(pl.BlockSpec((8, d), lambda i: (0, 0)))
        out_shape.append(jax.ShapeDtypeStruct((8, d), F32))
    out_specs += [any_spec, any_spec]
    out_shape += [jax.ShapeDtypeStruct((n_chunk, d, fc), BF16), jax.ShapeDtypeStruct((n_chunk, fc, d), BF16)]
    n_out = len(out_shape)
    out_specs += [any_spec] * len(scatter['blocks'])
    out_shape += _scatter_out_shape(scatter)
    return pl.pallas_call(
        body, name=f"mlp_bwd_{part}", grid=(nb,), in_specs=in_specs, out_specs=out_specs, out_shape=out_shape,
        scratch_shapes=[pltpu.VMEM((n_chunk, d, fc), F32), pltpu.VMEM((n_chunk * fc, d), F32),
                        pltpu.VMEM((d, fc), BF16), pltpu.VMEM((fc, d), BF16)] + _exchange_scratch(len(scatter['blocks'])),
        input_output_aliases=_scatter_aliases(scatter, n_in, n_out), compiler_params=_params(),
    )(*args)


def _mix_bwd(dh1, proj, hs, lru_saved, wp_bd, pool_b, pool_scale, conv_w, wg_bd, lru_l, w_out, scatter, tb):
    t_len, d = dh1.shape
    nb = t_len // tb
    n_s = len(scatter['blocks'])
    scatter_args = _scatter_args(scatter)

    def body(*refs):
        refs = list(refs)
        (dh1_ref, proj_ref, projh_ref, hs_ref, hsh_ref, lru_ref,
         wp_ref, pb_ref, ps_ref, cw_ref, wg_ref, l_ref, wout_ref, invh_ref) = refs[:14]
        del refs[:14]
        scatter_in = refs[:len(scatter_args)]
        del refs[:len(scatter_args)]
        dproj_ref, v512_ref, v1024_ref, dwp_out, dwg_out = refs[:5]
        recv = refs[5:5 + n_s]
        (dwp_acc, dwg_acc, ext_ref, b_ref, gs_ref, ehead_ref, dxbhead_ref, hc_ref,
         send_sems, recv_sems, local_sems) = refs[5 + n_s:]
        i = pl.program_id(0)
        tbk = nb - 1 - i

        start_scatter, wait_scatter = _scatter_ops(scatter, scatter_in, recv, (send_sems, recv_sems, local_sems))

        @pl.when(i == 0)
        def _():
            start_scatter()
            for ref in (v512_ref, v1024_ref, dwp_acc, dwg_acc, ehead_ref, dxbhead_ref, hc_ref):
                ref[...] = jnp.zeros_like(ref)

        dcat = _dot_nt(dh1_ref[...].astype(BF16), wout_ref[...])

        proj = proj_ref[...]
        has_prev = (tbk > 0).astype(F32)
        ext_ref[0:HALO, :] = projh_ref[:, 0:1024] * has_prev
        ext_ref[HALO:, :] = proj[:, 0:1024]
        ug = proj[:, 1024:1536]
        n = tb + HALO
        inv_head = invh_ref[...]

        up_ext = ext_ref[:, 0:512]
        win = _pool_windows(up_ext, n, True)[HALO:]
        dpool = _scale_by_inv_count(win, tbk == 0, inv_head) - proj[:, 0:512]
        dpoolb = dpool.astype(BF16)
        q = _dot(dpoolb, wp_ref[...]) + pb_ref[...]
        dyp = dcat[:, 0:512]
        dq = dyp * ps_ref[...]
        dqb = dq.astype(BF16)
        v512_ref[0:1, :] += _colsum(dyp * q)
        v512_ref[1:2, :] += _colsum(dq)
        dwp_acc[...] += _dot_tn(dpoolb, dqb)
        dd = _dot_nt(dqb, wp_ref[...])
        e = _scale_by_inv_count(dd, tbk == 0, inv_head)
        e_ext = jnp.concatenate([e, ehead_ref[...]], axis=0)
        du_pool = _pool_windows(e_ext, n, False)[0:tb] - dd
        ehead_ref[...] = e[0:HALO]

        gel, dgel = _gelu_parts(ug)
        hsv = hs_ref[...]
        dcl = dcat[:, 512:1024]
        dhs = dcl * gel
        dug = dcl * hsv * dgel
        ul_ext = ext_ref[:, 512:1024]
        cw = cw_ref[...]
        xb, r, ig, a = lru_ref[:, 0:512], lru_ref[:, 512:1024], lru_ref[:, 1024:1536], lru_ref[:, 1536:2048]
        first_row = (tbk * tb + lax.broadcasted_iota(jnp.int32, (tb, 1), 0)) == 0
        c_l = LRU_C * _log_sigmoid(l_ref[...])
        a2, m2, mult = _lru_decay(r, a, c_l, first_row)
        b_ref[...] = dhs
        row = lax.broadcasted_iota(jnp.int32, (SUB, 512), 0)

        def group(jj, hnext):
            o = pl.multiple_of((tb // SUB - 1 - jj) * SUB, SUB)
            a8 = lru_ref[pl.ds(o, SUB), 1536:2048]
            d8 = b_ref[pl.ds(o, SUB), :]
            b8 = a8 * d8
            for sh in (1, 2, 4):
                ash = jnp.where(row < SUB - sh, pltpu.roll(a8, SUB - sh, 0), 1.0)
                bsh = jnp.where(row < SUB - sh, pltpu.roll(b8, SUB - sh, 0), 0.0)
                b8 = a8 * bsh + b8
                a8 = a8 * ash
            h8 = a8 * hnext + b8
            gs_ref[pl.ds(o, SUB), :] = d8 + jnp.where(row < SUB - 1, pltpu.roll(h8, SUB - 1, 0), hnext)
            return jnp.broadcast_to(h8[0:1, :], (SUB, 512))

        hc_ref[...] = lax.fori_loop(0, tb // SUB, group, hc_ref[...])
        gsum = gs_ref[...]
        hs_ext = jnp.concatenate([hsh_ref[...] * has_prev, hsv], axis=0)
        hprev = pltpu.roll(hs_ext, 1, 0)[SUB:]
        da = gsum * hprev
        dmult = jnp.where(first_row, 0.0, gsum * (ig * xb))
        di = gsum * mult * xb
        dxb = gsum * mult * ig
        dla = da * a - dmult * a2 * lax.rsqrt(m2)
        dr = dla * c_l
        v512_ref[3:4, :] += _colsum(dla * r)
        dgp = jnp.concatenate([dr * r * (1.0 - r), di * ig * (1.0 - ig)], axis=1)
        dgpb = dgp.astype(BF16)
        v1024_ref[0:1, :] += _colsum(dgp)
        dwg_acc[...] += _dot_tn(xb.astype(BF16), dgpb)
        dxb = dxb + _dot_nt(dgpb, wg_ref[...])
        n8 = tb + SUB
        dxb_ext = jnp.concatenate([dxb, dxbhead_ref[...]], axis=0)
        du_lru = (cw[3:4, :] * dxb + cw[2:3, :] * pltpu.roll(dxb_ext, n8 - 1, 0)[0:tb]
                  + cw[1:2, :] * pltpu.roll(dxb_ext, n8 - 2, 0)[0:tb] + cw[0:1, :] * pltpu.roll(dxb_ext, n8 - 3, 0)[0:tb])
        dxbhead_ref[...] = dxb[0:SUB]
        v512_ref[2:3, :] += _colsum(dxb)
        for j in range(4):
            shifted = ul_ext if j == 0 else pltpu.roll(ul_ext, j, 0)
            v512_ref[4 + (3 - j):5 + (3 - j), :] += _colsum(dxb * shifted[HALO:])

        dproj_ref[...] = jnp.concatenate([du_pool, du_lru, dug], axis=1).astype(BF16)

        @pl.when(i == nb - 1)
        def _():
            v512_ref[3:4, :] = v512_ref[3:4, :] * (LRU_C * _sigmoid(-l_ref[...]))
            pltpu.sync_copy(dwp_acc, dwp_out)
            pltpu.sync_copy(dwg_acc, dwg_out)
            wait_scatter()

    rev = lambda w: pl.BlockSpec((tb, w), lambda i: (nb - 1 - i, 0))
    halo = lambda rows, w: pl.BlockSpec((rows, w), lambda i: (jnp.maximum((nb - 1 - i) * (tb // rows) - 1, 0), 0))
    any_spec = pl.BlockSpec(memory_space=pl.ANY)
    smalls = [wp_bd, pool_b, pool_scale, conv_w, wg_bd, lru_l, w_out, _inv_count_head()]
    return pl.pallas_call(
        body, name="mix_bwd", grid=(nb,),
        in_specs=[rev(d), rev(1536), halo(HALO, 1536), rev(512), halo(SUB, 512), rev(2048)]
        + [_const_spec(s.shape) for s in smalls] + [any_spec] * len(scatter_args),
        out_specs=[rev(1536), pl.BlockSpec((8, 512), lambda i: (0, 0)), pl.BlockSpec((8, 1024), lambda i: (0, 0))]
        + [any_spec] * (2 + n_s),
        out_shape=[jax.ShapeDtypeStruct((t_len, 1536), BF16), jax.ShapeDtypeStruct((8, 512), F32),
                   jax.ShapeDtypeStruct((8, 1024), F32), jax.ShapeDtypeStruct(wp_bd.shape, F32),
                   jax.ShapeDtypeStruct(wg_bd.shape, F32)]
        + _scatter_out_shape(scatter),
        scratch_shapes=[pltpu.VMEM(wp_bd.shape, F32), pltpu.VMEM(wg_bd.shape, F32), pltpu.VMEM((tb + HALO, 1024), F32),
                        pltpu.VMEM((tb, 512), F32), pltpu.VMEM((tb, 512), F32), pltpu.VMEM((HALO, 512), F32),
                        pltpu.VMEM((SUB, 512), F32), pltpu.VMEM((SUB, 512), F32)]
        + _exchange_scratch(n_s),
        input_output_aliases=_scatter_aliases(scatter, 6 + len(smalls), 5), compiler_params=_params(),
    )(dh1, proj, proj, hs, hs, lru_saved, *smalls, *scatter_args)


def _wgrad(name, a, b, tb):
    t_len, m = a.shape
    n = b.shape[1]
    nb = t_len // tb

    def body(a_ref, b_ref, out_ref, acc_ref, stage_ref):
        i = pl.program_id(0)

        @pl.when(i == 0)
        def _():
            acc_ref[...] = jnp.zeros_like(acc_ref)

        acc_ref[...] += _dot_tn(a_ref[...], b_ref[...].astype(BF16))

        @pl.when(i == nb - 1)
        def _():
            stage_ref[...] = acc_ref[...].astype(BF16)
            pltpu.sync_copy(stage_ref, out_ref)

    return pl.pallas_call(
        body, name=name, grid=(nb,),
        in_specs=[pl.BlockSpec((tb, m), lambda i: (i, 0)), pl.BlockSpec((tb, n), lambda i: (i, 0))],
        out_specs=pl.BlockSpec(memory_space=pl.ANY), out_shape=jax.ShapeDtypeStruct((m, n), BF16),
        scratch_shapes=[pltpu.VMEM((m, n), F32), pltpu.VMEM((m, n), BF16)], compiler_params=_params(),
    )(a, b)


def _in_bwd(dproj, z1, x, dh1, g_mix, w_in, tb):
    t_len, d = x.shape
    nb = t_len // tb

    def body(dproj_ref, z1_ref, x_ref, dh1_ref, g_ref, win_ref, dx_ref, vec_ref, dwin_out, dwin_acc, dwin_stage):
        i = pl.program_id(0)

        @pl.when(i == 0)
        def _():
            vec_ref[...] = jnp.zeros_like(vec_ref)
            dwin_acc[...] = jnp.zeros_like(dwin_acc)

        dprojb = dproj_ref[...]
        dwin_acc[...] += _dot_tn(z1_ref[...], dprojb)
        dz1 = _dot_nt(dprojb, win_ref[...])
        g = g_ref[...]
        _, xh, rr = _rms_fwd(x_ref[...], g)
        dx_ref[...] = dh1_ref[...] + _rms_bwd(xh, rr, g, dz1)
        vec_ref[0:1, :] += _colsum(dz1 * xh)

        @pl.when(i == nb - 1)
        def _():
            dwin_stage[...] = dwin_acc[...].astype(BF16)
            pltpu.sync_copy(dwin_stage, dwin_out)

    row_spec = lambda w: pl.BlockSpec((tb, w), lambda i: (i, 0))
    return pl.pallas_call(
        body, name="in_bwd", grid=(nb,),
        in_specs=[row_spec(dproj.shape[1]), row_spec(d), row_spec(d), row_spec(d), _const_spec(g_mix.shape),
                  _const_spec(w_in.shape)],
        out_specs=[row_spec(d), pl.BlockSpec((8, d), lambda i: (0, 0)), pl.BlockSpec(memory_space=pl.ANY)],
        out_shape=[jax.ShapeDtypeStruct((t_len, d), F32), jax.ShapeDtypeStruct((8, d), F32),
                   jax.ShapeDtypeStruct(w_in.shape, BF16)],
        scratch_shapes=[pltpu.VMEM(w_in.shape, F32), pltpu.VMEM(w_in.shape, BF16)], compiler_params=_params(),
    )(dproj, z1, x, dh1, g_mix, w_in)


def _exchange(name, gathered, scattered):
    n_g, n = len(gathered), len(gathered) + len(scattered)
    srcs = list(gathered) + list(scattered)
    shapes = [a.shape for a in gathered] + [a.shape[1:] for a in scattered]

    def body(*refs):
        ins, outs = refs[:n], refs[n:2 * n]
        srcs_of = [(lambda s, r=r: r) for r in ins[:n_g]] + [(lambda s, r=r: r.at[s]) for r in ins[n_g:]]
        start, wait = _all_to_all(srcs_of, outs, *refs[2 * n:])
        start()
        wait()

    any_spec = pl.BlockSpec(memory_space=pl.ANY)
    return pl.pallas_call(
        body, name=name, in_specs=[any_spec] * n, out_specs=[any_spec] * n,
        out_shape=[jax.ShapeDtypeStruct((N_DEV,) + tuple(sh), a.dtype) for sh, a in zip(shapes, srcs)],
        scratch_shapes=_exchange_scratch(n),
    )(*srcs)


def _final_exchange(name, small, scattered):
    n = len(scattered)
    rows = small.shape[0]
    per = rows // N_DEV

    def body(*refs):
        small_ref = refs[0]
        ins = refs[1:1 + n]
        sum_ref = refs[1 + n]
        outs = refs[2 + n:2 + 2 * n]
        land_ref, send_sems, recv_sems, local_sems, small_send, small_recv = refs[2 + 2 * n:]
        me = _my_index()
        start, wait = _all_to_all([(lambda s, r=r: r.at[s]) for r in ins], outs, send_sems, recv_sems, local_sems)
        start()
        mine = pl.ds(pl.multiple_of(me * per, 8), per)

        def dev(s):
            return (s // 4, (s // 2) % 2, s % 2)

        def partial_to(s):
            return pltpu.make_async_remote_copy(
                src_ref=small_ref.at[pl.ds(s * per, per)], dst_ref=land_ref.at[me], send_sem=small_send.at[0, s],
                recv_sem=small_recv.at[0, me], device_id=dev(s), device_id_type=MESH)

        def partial_from(s):
            return pltpu.make_async_remote_copy(
                src_ref=small_ref.at[pl.ds(s * per, per)], dst_ref=land_ref.at[s], send_sem=small_send.at[0, s],
                recv_sem=small_recv.at[0, s], device_id=dev(s), device_id_type=MESH)

        def sum_to(s):
            return pltpu.make_async_remote_copy(
                src_ref=sum_ref.at[mine], dst_ref=sum_ref.at[mine], send_sem=small_send.at[1, s],
                recv_sem=small_recv.at[1, me], device_id=dev(s), device_id_type=MESH)

        def sum_from(s):
            rows_s = pl.ds(s * per, per)
            return pltpu.make_async_remote_copy(
                src_ref=sum_ref.at[rows_s], dst_ref=sum_ref.at[rows_s], send_sem=small_send.at[1, s],
                recv_sem=small_recv.at[1, s], device_id=dev(s), device_id_type=MESH)

        for s in range(N_DEV):
            @pl.when(s != me)
            def _():
                partial_to(s).start()
        land_ref[me] = small_ref[mine, :]
        for s in range(N_DEV):
            @pl.when(s != me)
            def _():
                partial_from(s).wait_recv()
        total = land_ref[0]
        for s in range(1, N_DEV):
            total = total + land_ref[s]
        sum_ref[mine, :] = total
        for s in range(N_DEV):
            @pl.when(s != me)
            def _():
                sum_to(s).start()
        for s in range(N_DEV):
            @pl.when(s != me)
            def _():
                sum_from(s).wait_recv()
                partial_to(s).wait_send()
                sum_to(s).wait_send()
        wait()

    any_spec = pl.BlockSpec(memory_space=pl.ANY)
    vmem_spec = pl.BlockSpec(memory_space=pltpu.VMEM)
    res = pl.pallas_call(
        body, name=name, in_specs=[vmem_spec] + [any_spec] * n, out_specs=[vmem_spec] + [any_spec] * n,
        out_shape=[jax.ShapeDtypeStruct(small.shape, F32)]
        + [jax.ShapeDtypeStruct(a.shape, a.dtype) for a in scattered],
        scratch_shapes=[pltpu.VMEM((N_DEV, per, small.shape[1]), F32)] + _exchange_scratch(n)
        + [pltpu.SemaphoreType.DMA((2, N_DEV)), pltpu.SemaphoreType.DMA((2, N_DEV))],
    )(small, *scattered)
    return res[0], res[1:]


def _adamw(name, parts, w, m, v, row_block):
    n_src, rows, cols = parts.shape
    rb = min(row_block, rows)

    def body(p_ref, w_ref, m_ref, v_ref, g_out, d_out, m_out, v_out):
        g = p_ref[0].astype(F32)
        for s in range(1, n_src):
            g = g + p_ref[s].astype(F32)
        m_new = ADAM_B1 * m_ref[...] + (1.0 - ADAM_B1) * g
        v_new = ADAM_B2 * v_ref[...] + (1.0 - ADAM_B2) * jnp.square(g)
        m_hat = m_new / (1.0 - ADAM_B1 ** ADAM_STEP)
        v_hat = v_new / (1.0 - ADAM_B2 ** ADAM_STEP)
        g_out[...] = g
        d_out[...] = -ADAM_LR * (m_hat / (jnp.sqrt(v_hat) + ADAM_EPS) + ADAM_WD * w_ref[...])
        m_out[...] = m_new
        v_out[...] = v_new

    spec = pl.BlockSpec((rb, cols), lambda i: (i, 0))
    return pl.pallas_call(
        body, name=name, grid=(rows // rb,),
        in_specs=[pl.BlockSpec((n_src, rb, cols), lambda i: (0, i, 0)), spec, spec, spec],
        out_specs=[spec] * 4, out_shape=[jax.ShapeDtypeStruct((rows, cols), F32)] * 4,
        compiler_params=pltpu.CompilerParams(dimension_semantics=("parallel",), vmem_limit_bytes=VMEM_LIMIT),
    )(parts, w, m, v)


def _block_diag(blocks):
    g, a, b = blocks.shape
    eye = jnp.eye(g, dtype=blocks.dtype)
    return (eye[:, None, :, None] * blocks[:, :, None, :]).reshape(g * a, g * b)


def _diag_blocks(mat, g):
    a, b = mat.shape[0] // g, mat.shape[1] // g
    m4 = mat.reshape(g, a, g, b)
    return jnp.stack([m4[k, :, k, :] for k in range(g)], axis=0)


def _pack(pieces, row_multiple=8, width=1024):
    flat = jnp.concatenate([p.reshape(-1) for p in pieces])
    rows = -(-flat.shape[0] // (row_multiple * width)) * row_multiple
    return jnp.pad(flat, (0, rows * width - flat.shape[0])).reshape(rows, width)


def _unpack(packed, shapes):
    flat = packed.reshape(-1)
    out, o = [], 0
    for sh in shapes:
        size = 1
        for k in sh:
            size *= k
        out.append(flat[o:o + size].reshape(sh))
        o += size
    return out


def _small_grads(v512, v1024, in_vec, mlp_vec, ple_vec, dwp_bd, dwg_bd, pool_b, gate_a_b, gate_x_b):
    return [
        in_vec[0:1],
        _diag_blocks(dwp_bd, N_POOL_GROUPS)[None],
        v512[1:2].reshape(pool_b.shape),
        v512[0:1],
        v512[2:3],
        _diag_blocks(dwg_bd[:, :512], LRU_HEADS)[None],
        v1024[0:1, :512].reshape(gate_a_b.shape),
        _diag_blocks(dwg_bd[:, 512:], LRU_HEADS)[None],
        v1024[0:1, 512:].reshape(gate_x_b.shape),
        v512[3:4],
        mlp_vec[0:1],
        ple_vec[1:2],
        ple_vec[0:1],
        ple_vec[2:3].reshape(-1),
        v512[4:8][None],
        ple_vec[3:4, 0:1].reshape(1),
    ]


def kernel(x, p, norm_mix_g, w_in, pool_w, pool_b, pool_scale, conv_w, conv_b, gate_a_w, gate_a_b, gate_x_w, gate_x_b, lru_L, w_out, norm_mlp_g, w_up, w_down, norm_ple_g, w_ple_gate, b_ple_gate, w_ple_proj, norm_final_g, loss_target, m_norm_mix_g, m_w_in, m_pool_w, m_pool_b, m_pool_scale, m_conv_w, m_conv_b, m_gate_a_w, m_gate_a_b, m_gate_x_w, m_gate_x_b, m_lru_L, m_w_out, m_norm_mlp_g, m_w_up, m_w_down, m_norm_ple_g, m_w_ple_gate, m_b_ple_gate, m_w_ple_proj, m_norm_final_g, v_norm_mix_g, v_w_in, v_pool_w, v_pool_b, v_pool_scale, v_conv_w, v_conv_b, v_gate_a_w, v_gate_a_b, v_gate_x_w, v_gate_x_b, v_lru_L, v_w_out, v_norm_mlp_g, v_w_up, v_w_down, v_norm_ple_g, v_w_ple_gate, v_b_ple_gate, v_w_ple_proj, v_norm_final_g):
    t_len, d = x.shape[1], x.shape[2]
    tbs = {k: min(v, t_len) for k, v in TIME_BLOCKS.items()}
    me = _my_index()

    win_g, wout_g, convw_g = _gather("gather_mixer_weights", [w_in[0].astype(BF16), w_out[0].astype(BF16), conv_w[0]])
    w_in_f = jnp.transpose(win_g, (1, 0, 2)).reshape(d, -1)
    conv_w_f = jnp.transpose(convw_g, (1, 0, 2)).reshape(convw_g.shape[1], -1)
    wp_bd = _block_diag(pool_w[0]).astype(BF16)
    wg_bd = jnp.concatenate([_block_diag(gate_a_w[0]), _block_diag(gate_x_w[0])], axis=1).astype(BF16)
    gate_b2 = jnp.concatenate([gate_a_b.reshape(1, -1), gate_x_b.reshape(1, -1)], axis=1)
    mixer_small = (norm_mix_g, w_in_f, wp_bd, pool_b.reshape(1, -1), pool_scale, conv_w_f, conv_b, wg_bd, gate_b2, lru_L,
                   wout_g.reshape(-1, d))

    x2 = x[0]
    later = [w_up[0].astype(BF16), w_down[0].astype(BF16), w_ple_gate[0].astype(BF16), w_ple_proj[0].astype(BF16)]
    h1, z1, proj, hs, cat, lru_saved, wup_g, wdn_g, wgate_g, wproj_g = _mix_fwd(
        x2, *mixer_small, later, [_core_major_slot, _core_major_slot, None, None], GATHER_FORWARD_AT, tbs['mix_fwd'])
    w_down_f = wdn_g.reshape(-1, d)
    w_proj_f = jnp.transpose(wproj_g, (1, 0, 2)).reshape(wproj_g.shape[1], -1)
    h2, z2, up = _mlp_fwd(h1, norm_mlp_g, wup_g, w_down_f, tbs['mlp_fwd'])
    dh2, ple_vec, dw_gate, dw_proj = _ple(h2, p[0, 0], loss_target[0], norm_ple_g, wgate_g.reshape(-1, d), b_ple_gate,
                                          w_proj_f, norm_final_g.reshape(1, -1), tbs['ple'])
    everyone = list(range(N_DEV))
    n_proj = w_ple_proj.shape[2]
    scatter = _scatter_plan([dw_gate.reshape(N_DEV, -1, d), jnp.transpose(dw_proj.reshape(-1, N_DEV, n_proj), (1, 0, 2))],
                            [everyone, everyone], [None, None])
    dz2_0, dw_up_0, dw_down_0, recv_gate, recv_proj = _mlp_bwd_part(
        0, MLP_BWD_SPLIT, dh2, z2, up, wup_g, w_down_f, None, h1, norm_mlp_g, scatter, tbs['mlp_bwd'])
    half = N_DEV // MLP_BWD_SPLIT
    south = [_device_of_core_major_slot(k) for k in range(half)]
    north = [_device_of_core_major_slot(k) for k in range(half, N_DEV)]
    scatter = _scatter_plan([dw_up_0, dw_down_0], [south, south], [None, None])
    dh1, mlp_vec, dw_up_1, dw_down_1, recv_up, recv_down = _mlp_bwd_part(
        1, MLP_BWD_SPLIT, dh2, z2, up, wup_g, w_down_f, dz2_0, h1, norm_mlp_g, scatter, tbs['mlp_bwd'])
    dw_out = _wgrad("wgrad_out", cat, dh1, tbs['wgrad_out'])
    scatter = _scatter_plan([dw_up_1, dw_down_1, dw_out.reshape(N_DEV, -1, d)], [north, north, everyone],
                            [recv_up, recv_down, None])
    dproj, v512, v1024, dwp_bd, dwg_bd, recv_up, recv_down, recv_out = _mix_bwd(
        dh1, proj, hs, lru_saved, wp_bd, pool_b.reshape(1, -1), pool_scale, conv_w_f, wg_bd, lru_L, wout_g.reshape(-1, d),
        scatter, tbs['mix_bwd'])
    dx, in_vec, dw_in = _in_bwd(dproj, z1, x2, dh1, norm_mix_g, w_in_f, tbs['in_bwd'])

    small_grads = _small_grads(v512, v1024, in_vec, mlp_vec, ple_vec, dwp_bd, dwg_bd, pool_b, gate_a_b, gate_x_b)
    n_in = w_in.shape[2]
    small_sum, (recv_in,) = _final_exchange(
        "exchange_last_grads", _pack(small_grads, row_multiple=8 * N_DEV),
        [jnp.transpose(dw_in.reshape(d, N_DEV, n_in), (1, 0, 2))])
    received = [recv_in, recv_out, recv_up, recv_down, recv_gate, recv_proj]
    small_w = [norm_mix_g, pool_w, pool_b, pool_scale, conv_b, gate_a_w, gate_a_b, gate_x_w, gate_x_b, lru_L,
               norm_mlp_g, norm_ple_g, b_ple_gate, norm_final_g]
    small_m = [m_norm_mix_g, m_pool_w, m_pool_b, m_pool_scale, m_conv_b, m_gate_a_w, m_gate_a_b, m_gate_x_w, m_gate_x_b,
               m_lru_L, m_norm_mlp_g, m_norm_ple_g, m_b_ple_gate, m_norm_final_g]
    small_v = [v_norm_mix_g, v_pool_w, v_pool_b, v_pool_scale, v_conv_b, v_gate_a_w, v_gate_a_b, v_gate_x_w, v_gate_x_b,
               v_lru_L, v_norm_mlp_g, v_norm_ple_g, v_b_ple_gate, v_norm_final_g]
    small_shapes = [a.shape for a in small_w]
    conv_full_shape = (1,) + conv_w_f.shape

    shard_w = [w_in[0], w_out[0], w_up[0], w_down[0], w_ple_gate[0], w_ple_proj[0]]
    shard_m = [m_w_in[0], m_w_out[0], m_w_up[0], m_w_down[0], m_w_ple_gate[0], m_w_ple_proj[0]]
    shard_v = [v_w_in[0], v_w_out[0], v_w_up[0], v_w_down[0], v_w_ple_gate[0], v_w_ple_proj[0]]
    names = ["w_in", "w_out", "w_up", "w_down", "w_ple_gate", "w_ple_proj"]
    big_res = {}
    for nm, parts, w_s, m_s, v_s in zip(names, received, shard_w, shard_m, shard_v):
        big_res[nm] = [r[None] for r in _adamw("adamw_" + nm, parts, w_s, m_s, v_s, 128)]

    summed = _unpack(small_sum, small_shapes + [conv_full_shape, (1,)])
    loss = summed[-1][0]
    conv_g = lax.dynamic_slice_in_dim(summed[-2], me * conv_w.shape[2], conv_w.shape[2], axis=2)
    sg = summed[:-2] + [conv_g]
    sw, sm, sv = small_w + [conv_w], small_m + [m_conv_w], small_v + [v_conv_w]
    shapes2 = small_shapes + [conv_w.shape]
    res = _adamw("adamw_small", _pack(sg)[None], _pack(sw), _pack(sm), _pack(sv), 1024)
    sres = [_unpack(r, shapes2) for r in res]
    small_names = ["norm_mix_g", "pool_w", "pool_b", "pool_scale", "conv_b", "gate_a_w", "gate_a_b", "gate_x_w",
                   "gate_x_b", "lru_L", "norm_mlp_g", "norm_ple_g", "b_ple_gate", "norm_final_g", "conv_w"]
    order = ["norm_mix_g", "w_in", "pool_w", "pool_b", "pool_scale", "conv_w", "conv_b", "gate_a_w", "gate_a_b",
             "gate_x_w", "gate_x_b", "lru_L", "w_out", "norm_mlp_g", "w_up", "w_down", "norm_ple_g", "w_ple_gate",
             "b_ple_gate", "w_ple_proj", "norm_final_g"]
    outs = [loss, dx[None]]
    for kind in range(4):
        for nm in order:
            if nm in big_res:
                outs.append(big_res[nm][kind])
            else:
                outs.append(sres[kind][small_names.index(nm)])
    return tuple(outs)
```

```python
import functools

import jax
import jax.numpy as jnp
from jax import lax
from jax.experimental import pallas as pl
from jax.experimental.pallas import tpu as pltpu

F32 = jnp.float32
BF16 = jnp.bfloat16
MESH = pl.DeviceIdType.MESH

N_DEV = 8
RMS_EPS = 1e-6
LRU_C = 8.0
POOL_WINDOWS = (2, 4, 8, 16)
N_POOL_GROUPS = 4
LRU_HEADS = 8
HALO = 16
SUB = 8
GELU_C0 = 0.7978845608028654
GELU_C1 = 0.044715

ADAM_LR = 0.001
ADAM_B1 = 0.9
ADAM_B2 = 0.999
ADAM_EPS = 1e-08
ADAM_WD = 0.01
ADAM_STEP = 10

VMEM_LIMIT = 60 * 1024 * 1024
TIME_BLOCKS = dict(mix_fwd=512, mlp_fwd=512, ple=512, mlp_bwd=512, wgrad_out=1024, mix_bwd=512, in_bwd=512)
MLP_BWD_SPLIT = 2
MLP_GATHER_FORWARD_AT = (0.5, 0.9375)
PLE_GATHER_FORWARD_AT = (0.25, 0.25)


def _params(n_arbitrary=1):
    return pltpu.CompilerParams(dimension_semantics=("arbitrary",) * n_arbitrary, vmem_limit_bytes=VMEM_LIMIT)


def _dot(a, b):
    return jnp.dot(a, b, preferred_element_type=F32)


def _dot_nt(a, b):
    return lax.dot_general(a, b, (((1,), (1,)), ((), ())), preferred_element_type=F32)


def _dot_tn(a, b):
    return lax.dot_general(a, b, (((0,), (0,)), ((), ())), preferred_element_type=F32)


def _rms_fwd(x, g):
    r = lax.rsqrt(jnp.mean(x * x, axis=-1, keepdims=True) + RMS_EPS)
    xh = x * r
    return xh * g, xh, r


def _rms_bwd(xh, r, g, dz):
    dxh = dz * g
    return r * (dxh - xh * jnp.mean(dxh * xh, axis=-1, keepdims=True))


def _colsum(a):
    return jnp.sum(a, axis=0, keepdims=True)


def _sigmoid(a):
    return 0.5 * jnp.tanh(0.5 * a) + 0.5


def _gelu_parts(u):
    u2 = u * u
    th = jnp.tanh(GELU_C0 * (u + GELU_C1 * u * u2))
    gel = 0.5 * u * (1.0 + th)
    dgel = 0.5 * (1.0 + th) + 0.5 * u * (1.0 - th * th) * (GELU_C0 * (1.0 + 3.0 * GELU_C1 * u2))
    return gel, dgel


def _my_index():
    return 4 * lax.axis_index("x") + 2 * lax.axis_index("y") + lax.axis_index("c")


def _all_to_all(srcs_of, dsts, send_sems, recv_sems, local_sems, dests=None):
    n = len(dsts)
    me = _my_index()
    dests = [list(range(N_DEV))] * n if dests is None else dests

    def remote(t, s):
        return pltpu.make_async_remote_copy(
            src_ref=srcs_of[t](s), dst_ref=dsts[t].at[me], send_sem=send_sems.at[t, s], recv_sem=recv_sems.at[t, me],
            device_id=(s // 4, (s // 2) % 2, s % 2), device_id_type=MESH)

    def arrival(t, s):
        return pltpu.make_async_remote_copy(
            src_ref=srcs_of[t](dests[t][0]), dst_ref=dsts[t].at[s], send_sem=send_sems.at[t, s],
            recv_sem=recv_sems.at[t, s], device_id=(s // 4, (s // 2) % 2, s % 2), device_id_type=MESH)

    def local(t, s):
        return pltpu.make_async_copy(srcs_of[t](s), dsts[t].at[s], local_sems.at[t])

    def start():
        for s in range(N_DEV):
            to_s = [t for t in range(n) if s in dests[t]]

            @pl.when(s == me)
            def _():
                for t in to_s:
                    local(t, s).start()

            @pl.when(s != me)
            def _():
                for t in to_s:
                    remote(t, s).start()

    def wait():
        for s in range(N_DEV):
            to_s = [t for t in range(n) if s in dests[t]]

            @pl.when(s == me)
            def _():
                for t in to_s:
                    local(t, s).wait()
                    for src in range(N_DEV):
                        if src != s:
                            arrival(t, src).wait_recv()

            @pl.when(s != me)
            def _():
                for t in to_s:
                    remote(t, s).wait_send()

    return start, wait


N_GATHER_COPIES = 7


def _core_major_slot(dev):
    return 4 * dev[2] + 2 * dev[0] + dev[1]


def _device_of_core_major_slot(k):
    return (k % 4) * 2 + k // 4


def _two_level_gather(srcs, dsts, send_sems, recv_sems, local_sems, slots=None):
    n = len(dsts)
    x, y, c = lax.axis_index("x"), lax.axis_index("y"), lax.axis_index("c")
    me, sibling = (x, y, c), (x, y, 1 - c)
    chips = [(1 - x, y), (x, 1 - y), (1 - x, 1 - y)]

    def slot(t, dev):
        return 4 * dev[0] + 2 * dev[1] + dev[2] if slots is None or slots[t] is None else slots[t](dev)

    def copy(t, k, block, to, src=None):
        return pltpu.make_async_remote_copy(
            src_ref=dsts[t].at[slot(t, block)] if src is None else src, dst_ref=dsts[t].at[slot(t, block)],
            send_sem=send_sems.at[t, k], recv_sem=recv_sems.at[t, k], device_id=to, device_id_type=MESH)

    def local(t):
        return pltpu.make_async_copy(srcs[t], dsts[t].at[slot(t, me)], local_sems.at[t])

    def start():
        for t in range(n):
            local(t).start()
            for j, chip in enumerate(chips):
                copy(t, 1 + j, me, (*chip, c), src=srcs[t]).start()
            copy(t, 0, me, sibling, src=srcs[t]).start()

    def forward(t):
        for j, chip in enumerate(chips):
            copy(t, 1 + j, (*chip, c), me).wait_recv()
            copy(t, 4 + j, (*chip, c), sibling).start()

    def finish():
        for t in range(n):
            copy(t, 0, sibling, me).wait_recv()
            for j, chip in enumerate(chips):
                copy(t, 4 + j, (*chip, 1 - c), me).wait_recv()
            copy(t, 0, me, sibling, src=srcs[t]).wait_send()
            for j, chip in enumerate(chips):
                copy(t, 1 + j, me, (*chip, c), src=srcs[t]).wait_send()
                copy(t, 4 + j, (*chip, c), sibling).wait_send()
            local(t).wait()

    return start, forward, finish


def _hosted_gather(i, nb, forward_at, srcs, dsts, sems, slots=None):
    start, forward, finish = _two_level_gather(srcs, dsts, *sems, slots)

    def after_step():
        for t, f in enumerate(forward_at):
            @pl.when(i == min(nb - 1, int(f * nb)))
            def _():
                forward(t)

        @pl.when(i == nb - 1)
        def _():
            finish()

    return start, after_step


def _gather_scratch(n):
    return [pltpu.SemaphoreType.DMA((n, N_GATHER_COPIES)), pltpu.SemaphoreType.DMA((n, N_GATHER_COPIES)),
            pltpu.SemaphoreType.DMA((n,))]


def _gather(name, srcs):
    n = len(srcs)

    def body(*refs):
        start, forward, finish = _two_level_gather(refs[:n], refs[n:2 * n], *refs[2 * n:])
        start()
        for t in range(n):
            forward(t)
        finish()

    any_spec = pl.BlockSpec(memory_space=pl.ANY)
    return pl.pallas_call(
        body, name=name, in_specs=[any_spec] * n, out_specs=[any_spec] * n,
        out_shape=[jax.ShapeDtypeStruct((N_DEV,) + a.shape, a.dtype) for a in srcs], scratch_shapes=_gather_scratch(n),
    )(*srcs)


def _scatter_plan(blocks, dests, landing):
    return dict(blocks=list(blocks), dests=[list(dd) for dd in dests], landing=list(landing))


def _scatter_args(plan):
    return plan['blocks'] + [a for a in plan['landing'] if a is not None]


def _scatter_out_shape(plan):
    return [jax.ShapeDtypeStruct((N_DEV,) + b.shape[1:], b.dtype) for b in plan['blocks']]


def _scatter_aliases(plan, first_in, first_out):
    given = [t for t, a in enumerate(plan['landing']) if a is not None]
    return {first_in + len(plan['blocks']) + k: first_out + t for k, t in enumerate(given)}


def _scatter_ops(plan, in_refs, out_refs, sems):
    n = len(plan['blocks'])
    srcs_of = [(lambda s, r=in_refs[t], dd=plan['dests'][t]: r.at[dd.index(s)]) for t in range(n)]
    return _all_to_all(srcs_of, out_refs, *sems, dests=plan['dests'])


def _exchange_scratch(n):
    return [pltpu.SemaphoreType.DMA((n, N_DEV)), pltpu.SemaphoreType.DMA((n, N_DEV)), pltpu.SemaphoreType.DMA((n,))]


def _const_spec(shape):
    nd = len(shape)
    return pl.BlockSpec(shape, lambda i: (0,) * nd, pipeline_mode=pl.Buffered(1))


def _pool_windows(up_ext, n, forward):
    sh = (lambda k: k) if forward else (lambda k: n - k)
    s2 = up_ext + pltpu.roll(up_ext, sh(1), 0)
    t4 = s2[:, 128:]
    s4 = t4 + pltpu.roll(t4, sh(2), 0)
    t8 = s4[:, 128:]
    s8 = t8 + pltpu.roll(t8, sh(4), 0)
    t16 = s8[:, 128:]
    s16 = t16 + pltpu.roll(t16, sh(8), 0)
    return jnp.concatenate([s2[:, :128], s4[:, :128], s8[:, :128], s16], axis=1)


def _inv_count_head():
    t = jnp.arange(1, HALO + 1, dtype=F32)[:, None]
    return jnp.concatenate([jnp.broadcast_to(1.0 / jnp.minimum(t, float(w)), (HALO, 128)) for w in POOL_WINDOWS], axis=1)


def _scale_by_inv_count(v, is_first_block, inv_head):
    inv_row = jnp.concatenate([jnp.full((1, 128), 1.0 / w, F32) for w in POOL_WINDOWS], axis=1)
    head = v[0:HALO] * jnp.where(is_first_block, inv_head, inv_row)
    return jnp.concatenate([head, v[HALO:] * inv_row], axis=0)


def _lru_decay(r, a, c_l, first_row):
    a2 = a * a
    m2 = -jnp.tanh(c_l * r) * (a2 + 1.0)
    return a2, m2, jnp.where(first_row, 1.0, jnp.sqrt(m2))


def _log_sigmoid(v):
    return -(jnp.maximum(-v, 0.0) + jnp.log1p(jnp.exp(-jnp.abs(v))))


def _conv_fwd(ul_ext, cw, cb):
    return (cb + cw[3:4, :] * ul_ext + cw[2:3, :] * pltpu.roll(ul_ext, 1, 0)
            + cw[1:2, :] * pltpu.roll(ul_ext, 2, 0) + cw[0:1, :] * pltpu.roll(ul_ext, 3, 0))


def _mix_fwd(x, g_mix, w_in, wp_bd, pool_b, pool_scale, conv_w, conv_b, wg_bd, gate_b, lru_l, w_out, gather_srcs,
             gather_slots, forward_at, tb):
    t_len, d = x.shape
    nb = t_len // tb
    n_g = len(gather_srcs)

    def body(*refs):
        (x_ref, g_ref, win_ref, wp_ref, pb_ref, ps_ref, cw_ref, cb_ref, wg_ref, gb_ref, l_ref, wout_ref,
         invh_ref) = refs[:13]
        gsrc = refs[13:13 + n_g]
        h1_ref, z1_ref, proj_ref, hs_ref, cat_ref, lru_ref = refs[13 + n_g:19 + n_g]
        gdst = refs[19 + n_g:19 + 2 * n_g]
        ext_ref, a_ref, b_ref, hc_ref, send_sems, recv_sems, local_sems = refs[19 + 2 * n_g:]
        i = pl.program_id(0)
        start_gather, after_step = _hosted_gather(i, nb, forward_at, gsrc, gdst, (send_sems, recv_sems, local_sems),
                                                  gather_slots)

        @pl.when(i == 0)
        def _():
            start_gather()
            ext_ref[0:HALO, :] = jnp.zeros((HALO, 1024), F32)
            hc_ref[...] = jnp.zeros_like(hc_ref)

        xv = x_ref[...]
        z, _, _ = _rms_fwd(xv, g_ref[...])
        zb = z.astype(BF16)
        z1_ref[...] = zb
        proj = _dot(zb, win_ref[...])
        proj_ref[...] = proj
        ext_ref[HALO:, :] = proj[:, 0:1024]
        ug = proj[:, 1024:1536]
        n = tb + HALO
        up_ext = ext_ref[:, 0:512]
        win = _pool_windows(up_ext, n, True)[HALO:]
        dpool = _scale_by_inv_count(win, i == 0, invh_ref[...]) - proj[:, 0:512]
        q = _dot(dpool.astype(BF16), wp_ref[...]) + pb_ref[...]
        y_pool = q * ps_ref[...]
        xb = _conv_fwd(ext_ref[:, 512:1024], cw_ref[...], cb_ref[...])[HALO:]
        first_row = (i * tb + lax.broadcasted_iota(jnp.int32, (tb, 1), 0)) == 0
        c_l = LRU_C * _log_sigmoid(l_ref[...])
        gp = _dot(xb.astype(BF16), wg_ref[...]) + gb_ref[...]
        r = _sigmoid(gp[:, :512])
        ig = _sigmoid(gp[:, 512:])
        a = jnp.exp(c_l * r)
        _, _, mult = _lru_decay(r, a, c_l, first_row)
        lru_ref[:, 0:512] = xb
        lru_ref[:, 512:1024] = r
        lru_ref[:, 1024:1536] = ig
        lru_ref[:, 1536:2048] = a
        a_ref[...] = a
        b_ref[...] = mult * (ig * xb)
        row = lax.broadcasted_iota(jnp.int32, (SUB, 512), 0)

        def group(j, hprev):
            o = pl.multiple_of(j * SUB, SUB)
            a8 = a_ref[pl.ds(o, SUB), :]
            b8 = b_ref[pl.ds(o, SUB), :]
            for sh in (1, 2, 4):
                ash = jnp.where(row >= sh, pltpu.roll(a8, sh, 0), 1.0)
                bsh = jnp.where(row >= sh, pltpu.roll(b8, sh, 0), 0.0)
                b8 = a8 * bsh + b8
                a8 = a8 * ash
            h8 = a8 * hprev + b8
            hs_ref[pl.ds(o, SUB), :] = h8
            return jnp.broadcast_to(h8[SUB - 1:SUB, :], (SUB, 512))

        hc_ref[...] = lax.fori_loop(0, tb // SUB, group, hc_ref[...])
        gel, _ = _gelu_parts(ug)
        y_lru = hs_ref[...] * gel
        catb = jnp.concatenate([y_pool, y_lru], axis=1).astype(BF16)
        cat_ref[...] = catb
        h1_ref[...] = xv + _dot(catb, wout_ref[...])
        ext_ref[0:HALO, :] = ext_ref[tb:tb + HALO, :]

        after_step()

    row_spec = lambda w: pl.BlockSpec((tb, w), lambda i: (i, 0))
    any_spec = pl.BlockSpec(memory_space=pl.ANY)
    smalls = [g_mix, w_in, wp_bd, pool_b, pool_scale, conv_w, conv_b, wg_bd, gate_b, lru_l, w_out, _inv_count_head()]
    return pl.pallas_call(
        body, name="mix_fwd", grid=(nb,),
        in_specs=[row_spec(d)] + [_const_spec(s.shape) for s in smalls] + [any_spec] * n_g,
        out_specs=[row_spec(d), row_spec(d), row_spec(1536), row_spec(512), row_spec(1024), row_spec(2048)]
        + [any_spec] * n_g,
        out_shape=[jax.ShapeDtypeStruct((t_len, d), F32), jax.ShapeDtypeStruct((t_len, d), BF16),
                   jax.ShapeDtypeStruct((t_len, 1536), F32), jax.ShapeDtypeStruct((t_len, 512), F32),
                   jax.ShapeDtypeStruct((t_len, 1024), BF16), jax.ShapeDtypeStruct((t_len, 2048), F32)]
        + [jax.ShapeDtypeStruct((N_DEV,) + s.shape, s.dtype) for s in gather_srcs],
        scratch_shapes=[pltpu.VMEM((tb + HALO, 1024), F32), pltpu.VMEM((tb, 512), F32), pltpu.VMEM((tb, 512), F32),
                        pltpu.VMEM((SUB, 512), F32)] + _gather_scratch(n_g),
        compiler_params=_params(),
    )(x, *smalls, *gather_srcs)


def _mlp_fwd(h1, g_mlp, w_up, w_down, gather_srcs, forward_at, tb):
    t_len, d = h1.shape
    nb = t_len // tb
    n_chunk, _, fc = w_up.shape
    n_g = len(gather_srcs)

    def body(*refs):
        h1_ref, g_ref, wup_ref, wdn_ref = refs[:4]
        h2_ref, z2_ref, up_ref = refs[4 + n_g:7 + n_g]
        i = pl.program_id(0)
        start_gather, after_step = _hosted_gather(i, nb, forward_at, refs[4:4 + n_g], refs[7 + n_g:7 + 2 * n_g],
                                                  refs[7 + 2 * n_g:])

        @pl.when(i == 0)
        def _():
            start_gather()

        xv = h1_ref[...]
        z, _, _ = _rms_fwd(xv, g_ref[...])
        zb = z.astype(BF16)
        z2_ref[...] = zb
        acc = xv
        for c in range(n_chunk):
            u = _dot(zb, wup_ref[c])
            up_ref[:, c * fc:(c + 1) * fc] = u.astype(BF16)
            act = jnp.square(jnp.maximum(u, 0.0)).astype(BF16)
            acc = acc + _dot(act, wdn_ref[c * fc:(c + 1) * fc, :])
        h2_ref[...] = acc
        after_step()

    row_spec = lambda w: pl.BlockSpec((tb, w), lambda i: (i, 0))
    any_spec = pl.BlockSpec(memory_space=pl.ANY)
    return pl.pallas_call(
        body, name="mlp_fwd", grid=(nb,),
        in_specs=[row_spec(d), _const_spec(g_mlp.shape), _const_spec(w_up.shape), _const_spec(w_down.shape)]
        + [any_spec] * n_g,
        out_specs=[row_spec(d), row_spec(d), row_spec(n_chunk * fc)] + [any_spec] * n_g,
        out_shape=[jax.ShapeDtypeStruct((t_len, d), F32), jax.ShapeDtypeStruct((t_len, d), BF16),
                   jax.ShapeDtypeStruct((t_len, n_chunk * fc), BF16)]
        + [jax.ShapeDtypeStruct((N_DEV,) + a.shape, a.dtype) for a in gather_srcs],
        scratch_shapes=_gather_scratch(n_g), compiler_params=_params(),
    )(h1, g_mlp, w_up, w_down, *gather_srcs)


def _ple(h2, p, target, g_ple, w_gate, b_gate, w_proj, g_final, tb):
    t_len, d = h2.shape
    nb = t_len // tb
    pd = p.shape[1]

    def body(h2_ref, p_ref, tgt_ref, g_ref, wg_ref, bg_ref, wp_ref, gf_ref,
             dh2_ref, vec_ref, dwg_out, dwp_out, dwg_acc, dwp_acc, dwg_stage, dwp_stage):
        i = pl.program_id(0)

        @pl.when(i == 0)
        def _():
            vec_ref[...] = jnp.zeros_like(vec_ref)
            dwg_acc[...] = jnp.zeros_like(dwg_acc)
            dwp_acc[...] = jnp.zeros_like(dwp_acc)

        h2 = h2_ref[...]
        g2 = g_ref[...]
        z3, xh2, r2 = _rms_fwd(h2, g2)
        z3b = z3.astype(BF16)
        gate = _sigmoid(_dot(z3b, wg_ref[...]) + bg_ref[...])
        pb = p_ref[...].astype(BF16)
        pp = _dot(pb, wp_ref[...])
        h3 = h2 + gate * pp
        gf = gf_ref[...]
        y, xh3, r3 = _rms_fwd(h3, gf)
        err = y - tgt_ref[...]
        loss_rows = jnp.mean(err * err, axis=-1, keepdims=True)
        dy = err * (1.0 / d)
        dh3 = _rms_bwd(xh3, r3, gf, dy)
        dgl = (dh3 * pp) * (gate * (1.0 - gate))
        dpp = dh3 * gate
        dglb = dgl.astype(BF16)
        dwg_acc[...] += _dot_tn(z3b, dglb)
        dwp_acc[...] += _dot_tn(pb, dpp.astype(BF16))
        dz3 = _dot_nt(dglb, wg_ref[...])
        dh2_ref[...] = dh3 + _rms_bwd(xh2, r2, g2, dz3)
        vec_ref[0:1, :] += _colsum(dgl)
        vec_ref[1:2, :] += _colsum(dz3 * xh2)
        vec_ref[2:3, :] += _colsum(dy * xh3)
        vec_ref[3:4, :] += 0.5 * jnp.sum(loss_rows)

        @pl.when(i == nb - 1)
        def _():
            dwg_stage[...] = dwg_acc[...].astype(BF16)
            dwp_stage[...] = dwp_acc[...].astype(BF16)
            pltpu.sync_copy(dwg_stage, dwg_out)
            pltpu.sync_copy(dwp_stage, dwp_out)

    row_spec = lambda w: pl.BlockSpec((tb, w), lambda i: (i, 0))
    any_spec = pl.BlockSpec(memory_space=pl.ANY)
    smalls = [g_ple, w_gate, b_gate, w_proj, g_final]
    return pl.pallas_call(
        body, name="ple_fwd_bwd", grid=(nb,),
        in_specs=[row_spec(d), row_spec(pd), row_spec(d)] + [_const_spec(s.shape) for s in smalls],
        out_specs=[row_spec(d), pl.BlockSpec((8, d), lambda i: (0, 0)), any_spec, any_spec],
        out_shape=[jax.ShapeDtypeStruct((t_len, d), F32), jax.ShapeDtypeStruct((8, d), F32),
                   jax.ShapeDtypeStruct(w_gate.shape, BF16), jax.ShapeDtypeStruct(w_proj.shape, BF16)],
        scratch_shapes=[pltpu.VMEM(w_gate.shape, F32), pltpu.VMEM(w_proj.shape, F32), pltpu.VMEM(w_gate.shape, BF16),
                        pltpu.VMEM(w_proj.shape, BF16)],
        compiler_params=_params(),
    )(h2, p, target, *smalls)


def _mlp_bwd_part(part, n_part, dh2, z2, up, w_up, w_down, dz2_prev, h1, g_mlp, scatter, tb):
    t_len, d = dh2.shape
    nb = t_len // tb
    n_chunk_all, _, fc = w_up.shape
    n_chunk = n_chunk_all // n_part
    first, last = part == 0, part == n_part - 1

    def body(*refs):
        refs = list(refs)
        dh2_ref, z2_ref, up_ref, wup_ref, wdn_ref = refs[:5]
        del refs[:5]
        dzp_ref = None if first else refs.pop(0)
        h1_ref, g_ref = (refs.pop(0), refs.pop(0)) if last else (None, None)
        scatter_in = [refs.pop(0) for _ in _scatter_args(scatter)]
        out_ref = refs.pop(0)
        vec_ref = refs.pop(0) if last else None
        dwup_out, dwdn_out = refs.pop(0), refs.pop(0)
        scatter_out = [refs.pop(0) for _ in scatter['blocks']]
        dwup_acc, dwdn_acc, up_stage, dn_stage = refs[:4]
        start_scatter, wait_scatter = _scatter_ops(scatter, scatter_in, scatter_out, refs[4:])
        i = pl.program_id(0)

        @pl.when(i == 0)
        def _():
            start_scatter()
            dwup_acc[...] = jnp.zeros_like(dwup_acc)
            dwdn_acc[...] = jnp.zeros_like(dwdn_acc)
            if last:
                vec_ref[...] = jnp.zeros_like(vec_ref)

        dh2 = dh2_ref[...]
        dh2b = dh2.astype(BF16)
        z2b = z2_ref[...]
        dz2 = jnp.zeros((tb, d), F32) if first else dzp_ref[...]
        for c in range(n_chunk):
            u = up_ref[:, c * fc:(c + 1) * fc].astype(F32)
            ur = jnp.maximum(u, 0.0)
            dact = _dot_nt(dh2b, wdn_ref[c * fc:(c + 1) * fc, :])
            dupb = (dact * (2.0 * ur)).astype(BF16)
            dwdn_acc[c * fc:(c + 1) * fc, :] += _dot_tn((ur * ur).astype(BF16), dh2b)
            dwup_acc[c] += _dot_tn(z2b, dupb)
            dz2 = dz2 + _dot_nt(dupb, wup_ref[c])
        if last:
            g = g_ref[...]
            _, xh, r = _rms_fwd(h1_ref[...], g)
            out_ref[...] = dh2 + _rms_bwd(xh, r, g, dz2)
            vec_ref[0:1, :] += _colsum(dz2 * xh)
        else:
            out_ref[...] = dz2

        @pl.when(i == nb - 1)
        def _():
            for c in range(n_chunk):
                up_stage[...] = dwup_acc[c].astype(BF16)
                dn_stage[...] = dwdn_acc[c * fc:(c + 1) * fc, :].astype(BF16)
                pltpu.sync_copy(up_stage, dwup_out.at[c])
                pltpu.sync_copy(dn_stage, dwdn_out.at[c])
            wait_scatter()

    row_spec = lambda w: pl.BlockSpec((tb, w), lambda i: (i, 0))
    any_spec = pl.BlockSpec(memory_space=pl.ANY)
    args = [dh2, z2, up, w_up, w_down]
    in_specs = [row_spec(d), row_spec(d), pl.BlockSpec((tb, n_chunk * fc), lambda i: (i, part)),
                pl.BlockSpec((n_chunk, d, fc), lambda i: (part, 0, 0), pipeline_mode=pl.Buffered(1)),
                pl.BlockSpec((n_chunk * fc, d), lambda i: (part, 0), pipeline_mode=pl.Buffered(1))]
    if not first:
        args.append(dz2_prev)
        in_specs.append(row_spec(d))
    if last:
        args += [h1, g_mlp]
        in_specs += [row_spec(d), _const_spec(g_mlp.shape)]
    n_in = len(args)
    args += _scatter_args(scatter)
    in_specs += [any_spec] * len(_scatter_args(scatter))
    out_specs = [row_spec(d)]
    out_shape = [jax.ShapeDtypeStruct((t_len, d), F32)]
    if last:
        out_specs.append(pl.BlockSpec((8, d), lambda i: (0, 0)))
        out_shape.append(jax.ShapeDtypeStruct((8, d), F32))
    out_specs += [any_spec, any_spec]
    out_shape += [jax.ShapeDtypeStruct((n_chunk, d, fc), BF16), jax.ShapeDtypeStruct((n_chunk, fc, d), BF16)]
    n_out = len(out_shape)
    out_specs += [any_spec] * len(scatter['blocks'])
    out_shape += _scatter_out_shape(scatter)
    return pl.pallas_call(
        body, name=f"mlp_bwd_{part}", grid=(nb,), in_specs=in_specs, out_specs=out_specs, out_shape=out_shape,
        scratch_shapes=[pltpu.VMEM((n_chunk, d, fc), F32), pltpu.VMEM((n_chunk * fc, d), F32),
                        pltpu.VMEM((d, fc), BF16), pltpu.VMEM((fc, d), BF16)] + _exchange_scratch(len(scatter['blocks'])),
        input_output_aliases=_scatter_aliases(scatter, n_in, n_out), compiler_params=_params(),
    )(*args)


def _mix_bwd(dh1, proj, hs, lru_saved, wp_bd, pool_b, pool_scale, conv_w, wg_bd, lru_l, w_out, scatter, tb):
    t_len, d = dh1.shape
    nb = t_len // tb
    n_s = len(scatter['blocks'])
    scatter_args = _scatter_args(scatter)

    def body(*refs):
        refs = list(refs)
        (dh1_ref, proj_ref, projh_ref, hs_ref, hsh_ref, lru_ref,
         wp_ref, pb_ref, ps_ref, cw_ref, wg_ref, l_ref, wout_ref, invh_ref) = refs[:14]
        del refs[:14]
        scatter_in = refs[:len(scatter_args)]
        del refs[:len(scatter_args)]
        dproj_ref, v512_ref, v1024_ref, dwp_out, dwg_out = refs[:5]
        recv = refs[5:5 + n_s]
        (dwp_acc, dwg_acc, ext_ref, b_ref, gs_ref, ehead_ref, dxbhead_ref, hc_ref,
         send_sems, recv_sems, local_sems) = refs[5 + n_s:]
        i = pl.program_id(0)
        tbk = nb - 1 - i

        start_scatter, wait_scatter = _scatter_ops(scatter, scatter_in, recv, (send_sems, recv_sems, local_sems))

        @pl.when(i == 0)
        def _():
            start_scatter()
            for ref in (v512_ref, v1024_ref, dwp_acc, dwg_acc, ehead_ref, dxbhead_ref, hc_ref):
                ref[...] = jnp.zeros_like(ref)

        dcat = _dot_nt(dh1_ref[...].astype(BF16), wout_ref[...])

        proj = proj_ref[...]
        has_prev = (tbk > 0).astype(F32)
        ext_ref[0:HALO, :] = projh_ref[:, 0:1024] * has_prev
        ext_ref[HALO:, :] = proj[:, 0:1024]
        ug = proj[:, 1024:1536]
        n = tb + HALO
        inv_head = invh_ref[...]

        up_ext = ext_ref[:, 0:512]
        win = _pool_windows(up_ext, n, True)[HALO:]
        dpool = _scale_by_inv_count(win, tbk == 0, inv_head) - proj[:, 0:512]
        dpoolb = dpool.astype(BF16)
        q = _dot(dpoolb, wp_ref[...]) + pb_ref[...]
        dyp = dcat[:, 0:512]
        dq = dyp * ps_ref[...]
        dqb = dq.astype(BF16)
        v512_ref[0:1, :] += _colsum(dyp * q)
        v512_ref[1:2, :] += _colsum(dq)
        dwp_acc[...] += _dot_tn(dpoolb, dqb)
        dd = _dot_nt(dqb, wp_ref[...])
        e = _scale_by_inv_count(dd, tbk == 0, inv_head)
        e_ext = jnp.concatenate([e, ehead_ref[...]], axis=0)
        du_pool = _pool_windows(e_ext, n, False)[0:tb] - dd
        ehead_ref[...] = e[0:HALO]

        gel, dgel = _gelu_parts(ug)
        hsv = hs_ref[...]
        dcl = dcat[:, 512:1024]
        dhs = dcl * gel
        dug = dcl * hsv * dgel
        ul_ext = ext_ref[:, 512:1024]
        cw = cw_ref[...]
        xb, r, ig, a = lru_ref[:, 0:512], lru_ref[:, 512:1024], lru_ref[:, 1024:1536], lru_ref[:, 1536:2048]
        first_row = (tbk * tb + lax.broadcasted_iota(jnp.int32, (tb, 1), 0)) == 0
        c_l = LRU_C * _log_sigmoid(l_ref[...])
        a2, m2, mult = _lru_decay(r, a, c_l, first_row)
        b_ref[...] = dhs
        row = lax.broadcasted_iota(jnp.int32, (SUB, 512), 0)

        def group(jj, hnext):
            o = pl.multiple_of((tb // SUB - 1 - jj) * SUB, SUB)
            a8 = lru_ref[pl.ds(o, SUB), 1536:2048]
            d8 = b_ref[pl.ds(o, SUB), :]
            b8 = a8 * d8
            for sh in (1, 2, 4):
                ash = jnp.where(row < SUB - sh, pltpu.roll(a8, SUB - sh, 0), 1.0)
                bsh = jnp.where(row < SUB - sh, pltpu.roll(b8, SUB - sh, 0), 0.0)
                b8 = a8 * bsh + b8
                a8 = a8 * ash
            h8 = a8 * hnext + b8
            gs_ref[pl.ds(o, SUB), :] = d8 + jnp.where(row < SUB - 1, pltpu.roll(h8, SUB - 1, 0), hnext)
            return jnp.broadcast_to(h8[0:1, :], (SUB, 512))

        hc_ref[...] = lax.fori_loop(0, tb // SUB, group, hc_ref[...])
        gsum = gs_ref[...]
        hs_ext = jnp.concatenate([hsh_ref[...] * has_prev, hsv], axis=0)
        hprev = pltpu.roll(hs_ext, 1, 0)[SUB:]
        da = gsum * hprev
        dmult = jnp.where(first_row, 0.0, gsum * (ig * xb))
        di = gsum * mult * xb
        dxb = gsum * mult * ig
        dla = da * a - dmult * a2 * lax.rsqrt(m2)
        dr = dla * c_l
        v512_ref[3:4, :] += _colsum(dla * r)
        dgp = jnp.concatenate([dr * r * (1.0 - r), di * ig * (1.0 - ig)], axis=1)
        dgpb = dgp.astype(BF16)
        v1024_ref[0:1, :] += _colsum(dgp)
        dwg_acc[...] += _dot_tn(xb.astype(BF16), dgpb)
        dxb = dxb + _dot_nt(dgpb, wg_ref[...])
        n8 = tb + SUB
        dxb_ext = jnp.concatenate([dxb, dxbhead_ref[...]], axis=0)
        du_lru = (cw[3:4, :] * dxb + cw[2:3, :] * pltpu.roll(dxb_ext, n8 - 1, 0)[0:tb]
                  + cw[1:2, :] * pltpu.roll(dxb_ext, n8 - 2, 0)[0:tb] + cw[0:1, :] * pltpu.roll(dxb_ext, n8 - 3, 0)[0:tb])
        dxbhead_ref[...] = dxb[0:SUB]
        v512_ref[2:3, :] += _colsum(dxb)
        for j in range(4):
            shifted = ul_ext if j == 0 else pltpu.roll(ul_ext, j, 0)
            v512_ref[4 + (3 - j):5 + (3 - j), :] += _colsum(dxb * shifted[HALO:])

        dproj_ref[...] = jnp.concatenate([du_pool, du_lru, dug], axis=1).astype(BF16)

        @pl.when(i == nb - 1)
        def _():
            v512_ref[3:4, :] = v512_ref[3:4, :] * (LRU_C * _sigmoid(-l_ref[...]))
            pltpu.sync_copy(dwp_acc, dwp_out)
            pltpu.sync_copy(dwg_acc, dwg_out)
            wait_scatter()

    rev = lambda w: pl.BlockSpec((tb, w), lambda i: (nb - 1 - i, 0))
    halo = lambda rows, w: pl.BlockSpec((rows, w), lambda i: (jnp.maximum((nb - 1 - i) * (tb // rows) - 1, 0), 0))
    any_spec = pl.BlockSpec(memory_space=pl.ANY)
    smalls = [wp_bd, pool_b, pool_scale, conv_w, wg_bd, lru_l, w_out, _inv_count_head()]
    return pl.pallas_call(
        body, name="mix_bwd", grid=(nb,),
        in_specs=[rev(d), rev(1536), halo(HALO, 1536), rev(512), halo(SUB, 512), rev(2048)]
        + [_const_spec(s.shape) for s in smalls] + [any_spec] * len(scatter_args),
        out_specs=[rev(1536), pl.BlockSpec((8, 512), lambda i: (0, 0)), pl.BlockSpec((8, 1024), lambda i: (0, 0))]
        + [any_spec] * (2 + n_s),
        out_shape=[jax.ShapeDtypeStruct((t_len, 1536), BF16), jax.ShapeDtypeStruct((8, 512), F32),
                   jax.ShapeDtypeStruct((8, 1024), F32), jax.ShapeDtypeStruct(wp_bd.shape, F32),
                   jax.ShapeDtypeStruct(wg_bd.shape, F32)]
        + _scatter_out_shape(scatter),
        scratch_shapes=[pltpu.VMEM(wp_bd.shape, F32), pltpu.VMEM(wg_bd.shape, F32), pltpu.VMEM((tb + HALO, 1024), F32),
                        pltpu.VMEM((tb, 512), F32), pltpu.VMEM((tb, 512), F32), pltpu.VMEM((HALO, 512), F32),
                        pltpu.VMEM((SUB, 512), F32), pltpu.VMEM((SUB, 512), F32)]
        + _exchange_scratch(n_s),
        input_output_aliases=_scatter_aliases(scatter, 6 + len(smalls), 5), compiler_params=_params(),
    )(dh1, proj, proj, hs, hs, lru_saved, *smalls, *scatter_args)


def _wgrad(name, a, b, tb):
    t_len, m = a.shape
    n = b.shape[1]
    nb = t_len // tb

    def body(a_ref, b_ref, out_ref, acc_ref, stage_ref):
        i = pl.program_id(0)

        @pl.when(i == 0)
        def _():
            acc_ref[...] = jnp.zeros_like(acc_ref)

        acc_ref[...] += _dot_tn(a_ref[...], b_ref[...].astype(BF16))

        @pl.when(i == nb - 1)
        def _():
            stage_ref[...] = acc_ref[...].astype(BF16)
            pltpu.sync_copy(stage_ref, out_ref)

    return pl.pallas_call(
        body, name=name, grid=(nb,),
        in_specs=[pl.BlockSpec((tb, m), lambda i: (i, 0)), pl.BlockSpec((tb, n), lambda i: (i, 0))],
        out_specs=pl.BlockSpec(memory_space=pl.ANY), out_shape=jax.ShapeDtypeStruct((m, n), BF16),
        scratch_shapes=[pltpu.VMEM((m, n), F32), pltpu.VMEM((m, n), BF16)], compiler_params=_params(),
    )(a, b)


def _in_bwd(dproj, z1, x, dh1, g_mix, w_in, tb):
    t_len, d = x.shape
    nb = t_len // tb

    def body(dproj_ref, z1_ref, x_ref, dh1_ref, g_ref, win_ref, dx_ref, vec_ref, dwin_out, dwin_acc, dwin_stage):
        i = pl.program_id(0)

        @pl.when(i == 0)
        def _():
            vec_ref[...] = jnp.zeros_like(vec_ref)
            dwin_acc[...] = jnp.zeros_like(dwin_acc)

        dprojb = dproj_ref[...]
        dwin_acc[...] += _dot_tn(z1_ref[...], dprojb)
        dz1 = _dot_nt(dprojb, win_ref[...])
        g = g_ref[...]
        _, xh, rr = _rms_fwd(x_ref[...], g)
        dx_ref[...] = dh1_ref[...] + _rms_bwd(xh, rr, g, dz1)
        vec_ref[0:1, :] += _colsum(dz1 * xh)

        @pl.when(i == nb - 1)
        def _():
            dwin_stage[...] = dwin_acc[...].astype(BF16)
            pltpu.sync_copy(dwin_stage, dwin_out)

    row_spec = lambda w: pl.BlockSpec((tb, w), lambda i: (i, 0))
    return pl.pallas_call(
        body, name="in_bwd", grid=(nb,),
        in_specs=[row_spec(dproj.shape[1]), row_spec(d), row_spec(d), row_spec(d), _const_spec(g_mix.shape),
                  _const_spec(w_in.shape)],
        out_specs=[row_spec(d), pl.BlockSpec((8, d), lambda i: (0, 0)), pl.BlockSpec(memory_space=pl.ANY)],
        out_shape=[jax.ShapeDtypeStruct((t_len, d), F32), jax.ShapeDtypeStruct((8, d), F32),
                   jax.ShapeDtypeStruct(w_in.shape, BF16)],
        scratch_shapes=[pltpu.VMEM(w_in.shape, F32), pltpu.VMEM(w_in.shape, BF16)], compiler_params=_params(),
    )(dproj, z1, x, dh1, g_mix, w_in)


def _exchange(name, gathered, scattered):
    n_g, n = len(gathered), len(gathered) + len(scattered)
    srcs = list(gathered) + list(scattered)
    shapes = [a.shape for a in gathered] + [a.shape[1:] for a in scattered]

    def body(*refs):
        ins, outs = refs[:n], refs[n:2 * n]
        srcs_of = [(lambda s, r=r: r) for r in ins[:n_g]] + [(lambda s, r=r: r.at[s]) for r in ins[n_g:]]
        start, wait = _all_to_all(srcs_of, outs, *refs[2 * n:])
        start()
        wait()

    any_spec = pl.BlockSpec(memory_space=pl.ANY)
    return pl.pallas_call(
        body, name=name, in_specs=[any_spec] * n, out_specs=[any_spec] * n,
        out_shape=[jax.ShapeDtypeStruct((N_DEV,) + tuple(sh), a.dtype) for sh, a in zip(shapes, srcs)],
        scratch_shapes=_exchange_scratch(n),
    )(*srcs)


def _final_exchange(name, small, scattered):
    n = len(scattered)
    rows = small.shape[0]
    per = rows // N_DEV

    def body(*refs):
        small_ref = refs[0]
        ins = refs[1:1 + n]
        sum_ref = refs[1 + n]
        outs = refs[2 + n:2 + 2 * n]
        land_ref, send_sems, recv_sems, local_sems, small_send, small_recv = refs[2 + 2 * n:]
        me = _my_index()
        start, wait = _all_to_all([(lambda s, r=r: r.at[s]) for r in ins], outs, send_sems, recv_sems, local_sems)
        start()
        mine = pl.ds(pl.multiple_of(me * per, 8), per)

        def dev(s):
            return (s // 4, (s // 2) % 2, s % 2)

        def partial_to(s):
            return pltpu.make_async_remote_copy(
                src_ref=small_ref.at[pl.ds(s * per, per)], dst_ref=land_ref.at[me], send_sem=small_send.at[0, s],
                recv_sem=small_recv.at[0, me], device_id=dev(s), device_id_type=MESH)

        def partial_from(s):
            return pltpu.make_async_remote_copy(
                src_ref=small_ref.at[pl.ds(s * per, per)], dst_ref=land_ref.at[s], send_sem=small_send.at[0, s],
                recv_sem=small_recv.at[0, s], device_id=dev(s), device_id_type=MESH)

        def sum_to(s):
            return pltpu.make_async_remote_copy(
                src_ref=sum_ref.at[mine], dst_ref=sum_ref.at[mine], send_sem=small_send.at[1, s],
                recv_sem=small_recv.at[1, me], device_id=dev(s), device_id_type=MESH)

        def sum_from(s):
            rows_s = pl.ds(s * per, per)
            return pltpu.make_async_remote_copy(
                src_ref=sum_ref.at[rows_s], dst_ref=sum_ref.at[rows_s], send_sem=small_send.at[1, s],
                recv_sem=small_recv.at[1, s], device_id=dev(s), device_id_type=MESH)

        for s in range(N_DEV):
            @pl.when(s != me)
            def _():
                partial_to(s).start()
        land_ref[me] = small_ref[mine, :]
        for s in range(N_DEV):
            @pl.when(s != me)
            def _():
                partial_from(s).wait_recv()
        total = land_ref[0]
        for s in range(1, N_DEV):
            total = total + land_ref[s]
        sum_ref[mine, :] = total
        for s in range(N_DEV):
            @pl.when(s != me)
            def _():
                sum_to(s).start()
        for s in range(N_DEV):
            @pl.when(s != me)
            def _():
                sum_from(s).wait_recv()
                partial_to(s).wait_send()
                sum_to(s).wait_send()
        wait()

    any_spec = pl.BlockSpec(memory_space=pl.ANY)
    vmem_spec = pl.BlockSpec(memory_space=pltpu.VMEM)
    res = pl.pallas_call(
        body, name=name, in_specs=[vmem_spec] + [any_spec] * n, out_specs=[vmem_spec] + [any_spec] * n,
        out_shape=[jax.ShapeDtypeStruct(small.shape, F32)]
        + [jax.ShapeDtypeStruct(a.shape, a.dtype) for a in scattered],
        scratch_shapes=[pltpu.VMEM((N_DEV, per, small.shape[1]), F32)] + _exchange_scratch(n)
        + [pltpu.SemaphoreType.DMA((2, N_DEV)), pltpu.SemaphoreType.DMA((2, N_DEV))],
    )(small, *scattered)
    return res[0], res[1:]


def _adamw(name, parts, w, m, v, row_block):
    n_src, rows, cols = parts.shape
    rb = min(row_block, rows)

    def body(p_ref, w_ref, m_ref, v_ref, g_out, d_out, m_out, v_out):
        g = p_ref[0].astype(F32)
        for s in range(1, n_src):
            g = g + p_ref[s].astype(F32)
        m_new = ADAM_B1 * m_ref[...] + (1.0 - ADAM_B1) * g
        v_new = ADAM_B2 * v_ref[...] + (1.0 - ADAM_B2) * jnp.square(g)
        m_hat = m_new / (1.0 - ADAM_B1 ** ADAM_STEP)
        v_hat = v_new / (1.0 - ADAM_B2 ** ADAM_STEP)
        g_out[...] = g
        d_out[...] = -ADAM_LR * (m_hat / (jnp.sqrt(v_hat) + ADAM_EPS) + ADAM_WD * w_ref[...])
        m_out[...] = m_new
        v_out[...] = v_new

    spec = pl.BlockSpec((rb, cols), lambda i: (i, 0))
    return pl.pallas_call(
        body, name=name, grid=(rows // rb,),
        in_specs=[pl.BlockSpec((n_src, rb, cols), lambda i: (0, i, 0)), spec, spec, spec],
        out_specs=[spec] * 4, out_shape=[jax.ShapeDtypeStruct((rows, cols), F32)] * 4,
        compiler_params=pltpu.CompilerParams(dimension_semantics=("parallel",), vmem_limit_bytes=VMEM_LIMIT),
    )(parts, w, m, v)


def _block_diag(blocks):
    g, a, b = blocks.shape
    eye = jnp.eye(g, dtype=blocks.dtype)
    return (eye[:, None, :, None] * blocks[:, :, None, :]).reshape(g * a, g * b)


def _diag_blocks(mat, g):
    a, b = mat.shape[0] // g, mat.shape[1] // g
    m4 = mat.reshape(g, a, g, b)
    return jnp.stack([m4[k, :, k, :] for k in range(g)], axis=0)


def _pack(pieces, row_multiple=8, width=1024):
    flat = jnp.concatenate([p.reshape(-1) for p in pieces])
    rows = -(-flat.shape[0] // (row_multiple * width)) * row_multiple
    return jnp.pad(flat, (0, rows * width - flat.shape[0])).reshape(rows, width)


def _unpack(packed, shapes):
    flat = packed.reshape(-1)
    out, o = [], 0
    for sh in shapes:
        size = 1
        for k in sh:
            size *= k
        out.append(flat[o:o + size].reshape(sh))
        o += size
    return out


def _small_grads(v512, v1024, in_vec, mlp_vec, ple_vec, dwp_bd, dwg_bd, pool_b, gate_a_b, gate_x_b):
    return [
        in_vec[0:1],
        _diag_blocks(dwp_bd, N_POOL_GROUPS)[None],
        v512[1:2].reshape(pool_b.shape),
        v512[0:1],
        v512[2:3],
        _diag_blocks(dwg_bd[:, :512], LRU_HEADS)[None],
        v1024[0:1, :512].reshape(gate_a_b.shape),
        _diag_blocks(dwg_bd[:, 512:], LRU_HEADS)[None],
        v1024[0:1, 512:].reshape(gate_x_b.shape),
        v512[3:4],
        mlp_vec[0:1],
        ple_vec[1:2],
        ple_vec[0:1],
        ple_vec[2:3].reshape(-1),
        v512[4:8][None],
        ple_vec[3:4, 0:1].reshape(1),
    ]


def kernel(x, p, norm_mix_g, w_in, pool_w, pool_b, pool_scale, conv_w, conv_b, gate_a_w, gate_a_b, gate_x_w, gate_x_b, lru_L, w_out, norm_mlp_g, w_up, w_down, norm_ple_g, w_ple_gate, b_ple_gate, w_ple_proj, norm_final_g, loss_target, m_norm_mix_g, m_w_in, m_pool_w, m_pool_b, m_pool_scale, m_conv_w, m_conv_b, m_gate_a_w, m_gate_a_b, m_gate_x_w, m_gate_x_b, m_lru_L, m_w_out, m_norm_mlp_g, m_w_up, m_w_down, m_norm_ple_g, m_w_ple_gate, m_b_ple_gate, m_w_ple_proj, m_norm_final_g, v_norm_mix_g, v_w_in, v_pool_w, v_pool_b, v_pool_scale, v_conv_w, v_conv_b, v_gate_a_w, v_gate_a_b, v_gate_x_w, v_gate_x_b, v_lru_L, v_w_out, v_norm_mlp_g, v_w_up, v_w_down, v_norm_ple_g, v_w_ple_gate, v_b_ple_gate, v_w_ple_proj, v_norm_final_g):
    t_len, d = x.shape[1], x.shape[2]
    tbs = {k: min(v, t_len) for k, v in TIME_BLOCKS.items()}
    me = _my_index()

    win_g, wout_g, convw_g = _gather("gather_mixer_weights", [w_in[0].astype(BF16), w_out[0].astype(BF16), conv_w[0]])
    w_in_f = jnp.transpose(win_g, (1, 0, 2)).reshape(d, -1)
    conv_w_f = jnp.transpose(convw_g, (1, 0, 2)).reshape(convw_g.shape[1], -1)
    wp_bd = _block_diag(pool_w[0]).astype(BF16)
    wg_bd = jnp.concatenate([_block_diag(gate_a_w[0]), _block_diag(gate_x_w[0])], axis=1).astype(BF16)
    gate_b2 = jnp.concatenate([gate_a_b.reshape(1, -1), gate_x_b.reshape(1, -1)], axis=1)
    mixer_small = (norm_mix_g, w_in_f, wp_bd, pool_b.reshape(1, -1), pool_scale, conv_w_f, conv_b, wg_bd, gate_b2, lru_L,
                   wout_g.reshape(-1, d))

    x2 = x[0]
    later = [w_up[0].astype(BF16), w_down[0].astype(BF16)]
    h1, z1, proj, hs, cat, lru_saved, wup_g, wdn_g = _mix_fwd(
        x2, *mixer_small, later, [_core_major_slot, _core_major_slot], MLP_GATHER_FORWARD_AT, tbs['mix_fwd'])
    w_down_f = wdn_g.reshape(-1, d)
    h2, z2, up, wgate_g, wproj_g = _mlp_fwd(
        h1, norm_mlp_g, wup_g, w_down_f, [w_ple_gate[0].astype(BF16), w_ple_proj[0].astype(BF16)], PLE_GATHER_FORWARD_AT,
        tbs['mlp_fwd'])
    w_proj_f = jnp.transpose(wproj_g, (1, 0, 2)).reshape(wproj_g.shape[1], -1)
    dh2, ple_vec, dw_gate, dw_proj = _ple(h2, p[0, 0], loss_target[0], norm_ple_g, wgate_g.reshape(-1, d), b_ple_gate,
                                          w_proj_f, norm_final_g.reshape(1, -1), tbs['ple'])
    everyone = list(range(N_DEV))
    n_proj = w_ple_proj.shape[2]
    scatter = _scatter_plan([dw_gate.reshape(N_DEV, -1, d), jnp.transpose(dw_proj.reshape(-1, N_DEV, n_proj), (1, 0, 2))],
                            [everyone, everyone], [None, None])
    dz2_0, dw_up_0, dw_down_0, recv_gate, recv_proj = _mlp_bwd_part(
        0, MLP_BWD_SPLIT, dh2, z2, up, wup_g, w_down_f, None, h1, norm_mlp_g, scatter, tbs['mlp_bwd'])
    half = N_DEV // MLP_BWD_SPLIT
    south = [_device_of_core_major_slot(k) for k in range(half)]
    north = [_device_of_core_major_slot(k) for k in range(half, N_DEV)]
    scatter = _scatter_plan([dw_up_0, dw_down_0], [south, south], [None, None])
    dh1, mlp_vec, dw_up_1, dw_down_1, recv_up, recv_down = _mlp_bwd_part(
        1, MLP_BWD_SPLIT, dh2, z2, up, wup_g, w_down_f, dz2_0, h1, norm_mlp_g, scatter, tbs['mlp_bwd'])
    dw_out = _wgrad("wgrad_out", cat, dh1, tbs['wgrad_out'])
    scatter = _scatter_plan([dw_up_1, dw_down_1, dw_out.reshape(N_DEV, -1, d)], [north, north, everyone],
                            [recv_up, recv_down, None])
    dproj, v512, v1024, dwp_bd, dwg_bd, recv_up, recv_down, recv_out = _mix_bwd(
        dh1, proj, hs, lru_saved, wp_bd, pool_b.reshape(1, -1), pool_scale, conv_w_f, wg_bd, lru_L, wout_g.reshape(-1, d),
        scatter, tbs['mix_bwd'])
    dx, in_vec, dw_in = _in_bwd(dproj, z1, x2, dh1, norm_mix_g, w_in_f, tbs['in_bwd'])

    small_grads = _small_grads(v512, v1024, in_vec, mlp_vec, ple_vec, dwp_bd, dwg_bd, pool_b, gate_a_b, gate_x_b)
    n_in = w_in.shape[2]
    small_sum, (recv_in,) = _final_exchange(
        "exchange_last_grads", _pack(small_grads, row_multiple=8 * N_DEV),
        [jnp.transpose(dw_in.reshape(d, N_DEV, n_in), (1, 0, 2))])
    received = [recv_in, recv_out, recv_up, recv_down, recv_gate, recv_proj]
    small_w = [norm_mix_g, pool_w, pool_b, pool_scale, conv_b, gate_a_w, gate_a_b, gate_x_w, gate_x_b, lru_L,
               norm_mlp_g, norm_ple_g, b_ple_gate, norm_final_g]
    small_m = [m_norm_mix_g, m_pool_w, m_pool_b, m_pool_scale, m_conv_b, m_gate_a_w, m_gate_a_b, m_gate_x_w, m_gate_x_b,
               m_lru_L, m_norm_mlp_g, m_norm_ple_g, m_b_ple_gate, m_norm_final_g]
    small_v = [v_norm_mix_g, v_pool_w, v_pool_b, v_pool_scale, v_conv_b, v_gate_a_w, v_gate_a_b, v_gate_x_w, v_gate_x_b,
               v_lru_L, v_norm_mlp_g, v_norm_ple_g, v_b_ple_gate, v_norm_final_g]
    small_shapes = [a.shape for a in small_w]
    conv_full_shape = (1,) + conv_w_f.shape

    shard_w = [w_in[0], w_out[0], w_up[0], w_down[0], w_ple_gate[0], w_ple_proj[0]]
    shard_m = [m_w_in[0], m_w_out[0], m_w_up[0], m_w_down[0], m_w_ple_gate[0], m_w_ple_proj[0]]
    shard_v = [v_w_in[0], v_w_out[0], v_w_up[0], v_w_down[0], v_w_ple_gate[0], v_w_ple_proj[0]]
    names = ["w_in", "w_out", "w_up", "w_down", "w_ple_gate", "w_ple_proj"]
    big_res = {}
    for nm, parts, w_s, m_s, v_s in zip(names, received, shard_w, shard_m, shard_v):
        big_res[nm] = [r[None] for r in _adamw("adamw_" + nm, parts, w_s, m_s, v_s, 128)]

    summed = _unpack(small_sum, small_shapes + [conv_full_shape, (1,)])
    loss = summed[-1][0]
    conv_g = lax.dynamic_slice_in_dim(summed[-2], me * conv_w.shape[2], conv_w.shape[2], axis=2)
    sg = summed[:-2] + [conv_g]
    sw, sm, sv = small_w + [conv_w], small_m + [m_conv_w], small_v + [v_conv_w]
    shapes2 = small_shapes + [conv_w.shape]
    res = _adamw("adamw_small", _pack(sg)[None], _pack(sw), _pack(sm), _pack(sv), 1024)
    sres = [_unpack(r, shapes2) for r in res]
    small_names = ["norm_mix_g", "pool_w", "pool_b", "pool_scale", "conv_b", "gate_a_w", "gate_a_b", "gate_x_w",
                   "gate_x_b", "lru_L", "norm_mlp_g", "norm_ple_g", "b_ple_gate", "norm_final_g", "conv_w"]
    order = ["norm_mix_g", "w_in", "pool_w", "pool_b", "pool_scale", "conv_w", "conv_b", "gate_a_w", "gate_a_b",
             "gate_x_w", "gate_x_b", "lru_L", "w_out", "norm_mlp_g", "w_up", "w_down", "norm_ple_g", "w_ple_gate",
             "b_ple_gate", "w_ple_proj", "norm_final_g"]
    outs = [loss, dx[None]]
    for kind in range(4):
        for nm in order:
            if nm in big_res:
                outs.append(big_res[nm][kind])
            else:
                outs.append(sres[kind][small_names.index(nm)])
    return tuple(outs)
```

```python
import functools

import jax
import jax.numpy as jnp
from jax import lax
from jax.experimental import pallas as pl
from jax.experimental.pallas import tpu as pltpu

F32 = jnp.float32
BF16 = jnp.bfloat16
MESH = pl.DeviceIdType.MESH

N_DEV = 8
RMS_EPS = 1e-6
LRU_C = 8.0
POOL_WINDOWS = (2, 4, 8, 16)
N_POOL_GROUPS = 4
LRU_HEADS = 8
HALO = 16
SUB = 8
GELU_C0 = 0.7978845608028654
GELU_C1 = 0.044715

ADAM_LR = 0.001
ADAM_B1 = 0.9
ADAM_B2 = 0.999
ADAM_EPS = 1e-08
ADAM_WD = 0.01
ADAM_STEP = 10

VMEM_LIMIT = 60 * 1024 * 1024
TIME_BLOCKS = dict(mix_fwd=512, mlp_fwd=512, ple=512, mlp_bwd=512, wgrad_out=1024, mix_bwd=512, in_bwd=512)
MLP_BWD_SPLIT = 2
GATHER_FORWARD_AT = (0.5, 0.875, 1.0, 1.0)


def _params(n_arbitrary=1):
    return pltpu.CompilerParams(dimension_semantics=("arbitrary",) * n_arbitrary, vmem_limit_bytes=VMEM_LIMIT)


def _dot(a, b):
    return jnp.dot(a, b, preferred_element_type=F32)


def _dot_nt(a, b):
    return lax.dot_general(a, b, (((1,), (1,)), ((), ())), preferred_element_type=F32)


def _dot_tn(a, b):
    return lax.dot_general(a, b, (((0,), (0,)), ((), ())), preferred_element_type=F32)


def _rms_fwd(x, g):
    r = lax.rsqrt(jnp.mean(x * x, axis=-1, keepdims=True) + RMS_EPS)
    xh = x * r
    return xh * g, xh, r


def _rms_bwd(xh, r, g, dz):
    dxh = dz * g
    return r * (dxh - xh * jnp.mean(dxh * xh, axis=-1, keepdims=True))


def _colsum(a):
    return jnp.sum(a, axis=0, keepdims=True)


def _sigmoid(a):
    return 0.5 * jnp.tanh(0.5 * a) + 0.5


def _gelu_parts(u):
    u2 = u * u
    th = jnp.tanh(GELU_C0 * (u + GELU_C1 * u * u2))
    gel = 0.5 * u * (1.0 + th)
    dgel = 0.5 * (1.0 + th) + 0.5 * u * (1.0 - th * th) * (GELU_C0 * (1.0 + 3.0 * GELU_C1 * u2))
    return gel, dgel


def _my_index():
    return 4 * lax.axis_index("x") + 2 * lax.axis_index("y") + lax.axis_index("c")


def _all_to_all(srcs_of, dsts, send_sems, recv_sems, local_sems, dests=None):
    n = len(dsts)
    me = _my_index()
    dests = [list(range(N_DEV))] * n if dests is None else dests

    def remote(t, s):
        return pltpu.make_async_remote_copy(
            src_ref=srcs_of[t](s), dst_ref=dsts[t].at[me], send_sem=send_sems.at[t, s], recv_sem=recv_sems.at[t, me],
            device_id=(s // 4, (s // 2) % 2, s % 2), device_id_type=MESH)

    def arrival(t, s):
        return pltpu.make_async_remote_copy(
            src_ref=srcs_of[t](dests[t][0]), dst_ref=dsts[t].at[s], send_sem=send_sems.at[t, s],
            recv_sem=recv_sems.at[t, s], device_id=(s // 4, (s // 2) % 2, s % 2), device_id_type=MESH)

    def local(t, s):
        return pltpu.make_async_copy(srcs_of[t](s), dsts[t].at[s], local_sems.at[t])

    def start():
        for s in range(N_DEV):
            to_s = [t for t in range(n) if s in dests[t]]

            @pl.when(s == me)
            def _():
                for t in to_s:
                    local(t, s).start()

            @pl.when(s != me)
            def _():
                for t in to_s:
                    remote(t, s).start()

    def wait():
        for s in range(N_DEV):
            to_s = [t for t in range(n) if s in dests[t]]

            @pl.when(s == me)
            def _():
                for t in to_s:
                    local(t, s).wait()
                    for src in range(N_DEV):
                        if src != s:
                            arrival(t, src).wait_recv()

            @pl.when(s != me)
            def _():
                for t in to_s:
                    remote(t, s).wait_send()

    return start, wait


N_GATHER_COPIES = 7


def _core_major_slot(dev):
    return 4 * dev[2] + 2 * dev[0] + dev[1]


def _device_of_core_major_slot(k):
    return (k % 4) * 2 + k // 4


def _two_level_gather(srcs, dsts, send_sems, recv_sems, local_sems, slots=None):
    n = len(dsts)
    x, y, c = lax.axis_index("x"), lax.axis_index("y"), lax.axis_index("c")
    me, sibling = (x, y, c), (x, y, 1 - c)
    chips = [(1 - x, y), (x, 1 - y), (1 - x, 1 - y)]

    def slot(t, dev):
        return 4 * dev[0] + 2 * dev[1] + dev[2] if slots is None or slots[t] is None else slots[t](dev)

    def copy(t, k, block, to, src=None):
        return pltpu.make_async_remote_copy(
            src_ref=dsts[t].at[slot(t, block)] if src is None else src, dst_ref=dsts[t].at[slot(t, block)],
            send_sem=send_sems.at[t, k], recv_sem=recv_sems.at[t, k], device_id=to, device_id_type=MESH)

    def local(t):
        return pltpu.make_async_copy(srcs[t], dsts[t].at[slot(t, me)], local_sems.at[t])

    def start():
        for t in range(n):
            local(t).start()
            for j, chip in enumerate(chips):
                copy(t, 1 + j, me, (*chip, c), src=srcs[t]).start()
            copy(t, 0, me, sibling, src=srcs[t]).start()

    def forward(t):
        for j, chip in enumerate(chips):
            copy(t, 1 + j, (*chip, c), me).wait_recv()
            copy(t, 4 + j, (*chip, c), sibling).start()

    def finish():
        for t in range(n):
            copy(t, 0, sibling, me).wait_recv()
            for j, chip in enumerate(chips):
                copy(t, 4 + j, (*chip, 1 - c), me).wait_recv()
            copy(t, 0, me, sibling, src=srcs[t]).wait_send()
            for j, chip in enumerate(chips):
                copy(t, 1 + j, me, (*chip, c), src=srcs[t]).wait_send()
                copy(t, 4 + j, (*chip, c), sibling).wait_send()
            local(t).wait()

    return start, forward, finish


def _hosted_gather(i, nb, forward_at, srcs, dsts, sems, slots=None):
    start, forward, finish = _two_level_gather(srcs, dsts, *sems, slots)

    def after_step():
        for t, f in enumerate(forward_at):
            @pl.when(i == min(nb - 1, int(f * nb)))
            def _():
                forward(t)

        @pl.when(i == nb - 1)
        def _():
            finish()

    return start, after_step


def _gather_scratch(n):
    return [pltpu.SemaphoreType.DMA((n, N_GATHER_COPIES)), pltpu.SemaphoreType.DMA((n, N_GATHER_COPIES)),
            pltpu.SemaphoreType.DMA((n,))]


def _gather(name, srcs):
    n = len(srcs)

    def body(*refs):
        start, forward, finish = _two_level_gather(refs[:n], refs[n:2 * n], *refs[2 * n:])
        start()
        for t in range(n):
            forward(t)
        finish()

    any_spec = pl.BlockSpec(memory_space=pl.ANY)
    return pl.pallas_call(
        body, name=name, in_specs=[any_spec] * n, out_specs=[any_spec] * n,
        out_shape=[jax.ShapeDtypeStruct((N_DEV,) + a.shape, a.dtype) for a in srcs], scratch_shapes=_gather_scratch(n),
    )(*srcs)


def _scatter_plan(blocks, dests, landing):
    return dict(blocks=list(blocks), dests=[list(dd) for dd in dests], landing=list(landing))


def _scatter_args(plan):
    return plan['blocks'] + [a for a in plan['landing'] if a is not None]


def _scatter_out_shape(plan):
    return [jax.ShapeDtypeStruct((N_DEV,) + b.shape[1:], b.dtype) for b in plan['blocks']]


def _scatter_aliases(plan, first_in, first_out):
    given = [t for t, a in enumerate(plan['landing']) if a is not None]
    return {first_in + len(plan['blocks']) + k: first_out + t for k, t in enumerate(given)}


def _scatter_ops(plan, in_refs, out_refs, sems):
    n = len(plan['blocks'])
    srcs_of = [(lambda s, r=in_refs[t], dd=plan['dests'][t]: r.at[dd.index(s)]) for t in range(n)]
    return _all_to_all(srcs_of, out_refs, *sems, dests=plan['dests'])


def _exchange_scratch(n):
    return [pltpu.SemaphoreType.DMA((n, N_DEV)), pltpu.SemaphoreType.DMA((n, N_DEV)), pltpu.SemaphoreType.DMA((n,))]


def _const_spec(shape):
    nd = len(shape)
    return pl.BlockSpec(shape, lambda i: (0,) * nd, pipeline_mode=pl.Buffered(1))


def _pool_windows(up_ext, n, forward):
    sh = (lambda k: k) if forward else (lambda k: n - k)
    s2 = up_ext + pltpu.roll(up_ext, sh(1), 0)
    t4 = s2[:, 128:]
    s4 = t4 + pltpu.roll(t4, sh(2), 0)
    t8 = s4[:, 128:]
    s8 = t8 + pltpu.roll(t8, sh(4), 0)
    t16 = s8[:, 128:]
    s16 = t16 + pltpu.roll(t16, sh(8), 0)
    return jnp.concatenate([s2[:, :128], s4[:, :128], s8[:, :128], s16], axis=1)


def _inv_count_head():
    t = jnp.arange(1, HALO + 1, dtype=F32)[:, None]
    return jnp.concatenate([jnp.broadcast_to(1.0 / jnp.minimum(t, float(w)), (HALO, 128)) for w in POOL_WINDOWS], axis=1)


def _scale_by_inv_count(v, is_first_block, inv_head):
    inv_row = jnp.concatenate([jnp.full((1, 128), 1.0 / w, F32) for w in POOL_WINDOWS], axis=1)
    head = v[0:HALO] * jnp.where(is_first_block, inv_head, inv_row)
    return jnp.concatenate([head, v[HALO:] * inv_row], axis=0)


def _lru_decay(r, a, c_l, first_row):
    a2 = a * a
    m2 = -jnp.tanh(c_l * r) * (a2 + 1.0)
    return a2, m2, jnp.where(first_row, 1.0, jnp.sqrt(m2))


def _log_sigmoid(v):
    return -(jnp.maximum(-v, 0.0) + jnp.log1p(jnp.exp(-jnp.abs(v))))


def _conv_fwd(ul_ext, cw, cb):
    return (cb + cw[3:4, :] * ul_ext + cw[2:3, :] * pltpu.roll(ul_ext, 1, 0)
            + cw[1:2, :] * pltpu.roll(ul_ext, 2, 0) + cw[0:1, :] * pltpu.roll(ul_ext, 3, 0))


def _mix_fwd(x, g_mix, w_in, wp_bd, pool_b, pool_scale, conv_w, conv_b, wg_bd, gate_b, lru_l, w_out, gather_srcs,
             gather_slots, forward_at, tb):
    t_len, d = x.shape
    nb = t_len // tb
    n_g = len(gather_srcs)

    def body(*refs):
        (x_ref, g_ref, win_ref, wp_ref, pb_ref, ps_ref, cw_ref, cb_ref, wg_ref, gb_ref, l_ref, wout_ref,
         invh_ref) = refs[:13]
        gsrc = refs[13:13 + n_g]
        h1_ref, z1_ref, proj_ref, hs_ref, cat_ref, lru_ref = refs[13 + n_g:19 + n_g]
        gdst = refs[19 + n_g:19 + 2 * n_g]
        ext_ref, a_ref, b_ref, hc_ref, send_sems, recv_sems, local_sems = refs[19 + 2 * n_g:]
        i = pl.program_id(0)
        start_gather, after_step = _hosted_gather(i, nb, forward_at, gsrc, gdst, (send_sems, recv_sems, local_sems),
                                                  gather_slots)

        @pl.when(i == 0)
        def _():
            start_gather()
            ext_ref[0:HALO, :] = jnp.zeros((HALO, 1024), F32)
            hc_ref[...] = jnp.zeros_like(hc_ref)

        xv = x_ref[...]
        z, _, _ = _rms_fwd(xv, g_ref[...])
        zb = z.astype(BF16)
        z1_ref[...] = zb
        proj = _dot(zb, win_ref[...])
        proj_ref[...] = proj
        ext_ref[HALO:, :] = proj[:, 0:1024]
        ug = proj[:, 1024:1536]
        n = tb + HALO
        up_ext = ext_ref[:, 0:512]
        win = _pool_windows(up_ext, n, True)[HALO:]
        dpool = _scale_by_inv_count(win, i == 0, invh_ref[...]) - proj[:, 0:512]
        q = _dot(dpool.astype(BF16), wp_ref[...]) + pb_ref[...]
        y_pool = q * ps_ref[...]
        xb = _conv_fwd(ext_ref[:, 512:1024], cw_ref[...], cb_ref[...])[HALO:]
        first_row = (i * tb + lax.broadcasted_iota(jnp.int32, (tb, 1), 0)) == 0
        c_l = LRU_C * _log_sigmoid(l_ref[...])
        gp = _dot(xb.astype(BF16), wg_ref[...]) + gb_ref[...]
        r = _sigmoid(gp[:, :512])
        ig = _sigmoid(gp[:, 512:])
        a = jnp.exp(c_l * r)
        _, _, mult = _lru_decay(r, a, c_l, first_row)
        lru_ref[:, 0:512] = xb
        lru_ref[:, 512:1024] = r
        lru_ref[:, 1024:1536] = ig
        lru_ref[:, 1536:2048] = a
        a_ref[...] = a
        b_ref[...] = mult * (ig * xb)
        row = lax.broadcasted_iota(jnp.int32, (SUB, 512), 0)

        def group(j, hprev):
            o = pl.multiple_of(j * SUB, SUB)
            a8 = a_ref[pl.ds(o, SUB), :]
            b8 = b_ref[pl.ds(o, SUB), :]
            for sh in (1, 2, 4):
                ash = jnp.where(row >= sh, pltpu.roll(a8, sh, 0), 1.0)
                bsh = jnp.where(row >= sh, pltpu.roll(b8, sh, 0), 0.0)
                b8 = a8 * bsh + b8
                a8 = a8 * ash
            h8 = a8 * hprev + b8
            hs_ref[pl.ds(o, SUB), :] = h8
            return jnp.broadcast_to(h8[SUB - 1:SUB, :], (SUB, 512))

        hc_ref[...] = lax.fori_loop(0, tb // SUB, group, hc_ref[...])
        gel, _ = _gelu_parts(ug)
        y_lru = hs_ref[...] * gel
        catb = jnp.concatenate([y_pool, y_lru], axis=1).astype(BF16)
        cat_ref[...] = catb
        h1_ref[...] = xv + _dot(catb, wout_ref[...])
        ext_ref[0:HALO, :] = ext_ref[tb:tb + HALO, :]

        after_step()

    row_spec = lambda w: pl.BlockSpec((tb, w), lambda i: (i, 0))
    any_spec = pl.BlockSpec(memory_space=pl.ANY)
    smalls = [g_mix, w_in, wp_bd, pool_b, pool_scale, conv_w, conv_b, wg_bd, gate_b, lru_l, w_out, _inv_count_head()]
    return pl.pallas_call(
        body, name="mix_fwd", grid=(nb,),
        in_specs=[row_spec(d)] + [_const_spec(s.shape) for s in smalls] + [any_spec] * n_g,
        out_specs=[row_spec(d), row_spec(d), row_spec(1536), row_spec(512), row_spec(1024), row_spec(2048)]
        + [any_spec] * n_g,
        out_shape=[jax.ShapeDtypeStruct((t_len, d), F32), jax.ShapeDtypeStruct((t_len, d), BF16),
                   jax.ShapeDtypeStruct((t_len, 1536), F32), jax.ShapeDtypeStruct((t_len, 512), F32),
                   jax.ShapeDtypeStruct((t_len, 1024), BF16), jax.ShapeDtypeStruct((t_len, 2048), F32)]
        + [jax.ShapeDtypeStruct((N_DEV,) + s.shape, s.dtype) for s in gather_srcs],
        scratch_shapes=[pltpu.VMEM((tb + HALO, 1024), F32), pltpu.VMEM((tb, 512), F32), pltpu.VMEM((tb, 512), F32),
                        pltpu.VMEM((SUB, 512), F32)] + _gather_scratch(n_g),
        compiler_params=_params(),
    )(x, *smalls, *gather_srcs)


def _mlp_fwd(h1, g_mlp, w_up, w_down, tb):
    t_len, d = h1.shape
    nb = t_len // tb
    n_chunk, _, fc = w_up.shape

    def body(h1_ref, g_ref, wup_ref, wdn_ref, h2_ref, z2_ref, up_ref):
        xv = h1_ref[...]
        z, _, _ = _rms_fwd(xv, g_ref[...])
        zb = z.astype(BF16)
        z2_ref[...] = zb
        acc = xv
        for c in range(n_chunk):
            u = _dot(zb, wup_ref[c])
            up_ref[:, c * fc:(c + 1) * fc] = u.astype(BF16)
            act = jnp.square(jnp.maximum(u, 0.0)).astype(BF16)
            acc = acc + _dot(act, wdn_ref[c * fc:(c + 1) * fc, :])
        h2_ref[...] = acc

    row_spec = lambda w: pl.BlockSpec((tb, w), lambda i: (i, 0))
    return pl.pallas_call(
        body, name="mlp_fwd", grid=(nb,),
        in_specs=[row_spec(d), _const_spec(g_mlp.shape), _const_spec(w_up.shape), _const_spec(w_down.shape)],
        out_specs=[row_spec(d), row_spec(d), row_spec(n_chunk * fc)],
        out_shape=[jax.ShapeDtypeStruct((t_len, d), F32), jax.ShapeDtypeStruct((t_len, d), BF16),
                   jax.ShapeDtypeStruct((t_len, n_chunk * fc), BF16)],
        compiler_params=_params(),
    )(h1, g_mlp, w_up, w_down)


def _ple(h2, p, target, g_ple, w_gate, b_gate, w_proj, g_final, tb):
    t_len, d = h2.shape
    nb = t_len // tb
    pd = p.shape[1]

    def body(h2_ref, p_ref, tgt_ref, g_ref, wg_ref, bg_ref, wp_ref, gf_ref,
             dh2_ref, vec_ref, dwg_out, dwp_out, dwg_acc, dwp_acc, dwg_stage, dwp_stage):
        i = pl.program_id(0)

        @pl.when(i == 0)
        def _():
            vec_ref[...] = jnp.zeros_like(vec_ref)
            dwg_acc[...] = jnp.zeros_like(dwg_acc)
            dwp_acc[...] = jnp.zeros_like(dwp_acc)

        h2 = h2_ref[...]
        g2 = g_ref[...]
        z3, xh2, r2 = _rms_fwd(h2, g2)
        z3b = z3.astype(BF16)
        gate = _sigmoid(_dot(z3b, wg_ref[...]) + bg_ref[...])
        pb = p_ref[...].astype(BF16)
        pp = _dot(pb, wp_ref[...])
        h3 = h2 + gate * pp
        gf = gf_ref[...]
        y, xh3, r3 = _rms_fwd(h3, gf)
        err = y - tgt_ref[...]
        loss_rows = jnp.mean(err * err, axis=-1, keepdims=True)
        dy = err * (1.0 / d)
        dh3 = _rms_bwd(xh3, r3, gf, dy)
        dgl = (dh3 * pp) * (gate * (1.0 - gate))
        dpp = dh3 * gate
        dglb = dgl.astype(BF16)
        dwg_acc[...] += _dot_tn(z3b, dglb)
        dwp_acc[...] += _dot_tn(pb, dpp.astype(BF16))
        dz3 = _dot_nt(dglb, wg_ref[...])
        dh2_ref[...] = dh3 + _rms_bwd(xh2, r2, g2, dz3)
        vec_ref[0:1, :] += _colsum(dgl)
        vec_ref[1:2, :] += _colsum(dz3 * xh2)
        vec_ref[2:3, :] += _colsum(dy * xh3)
        vec_ref[3:4, :] += 0.5 * jnp.sum(loss_rows)

        @pl.when(i == nb - 1)
        def _():
            dwg_stage[...] = dwg_acc[...].astype(BF16)
            dwp_stage[...] = dwp_acc[...].astype(BF16)
            pltpu.sync_copy(dwg_stage, dwg_out)
            pltpu.sync_copy(dwp_stage, dwp_out)

    row_spec = lambda w: pl.BlockSpec((tb, w), lambda i: (i, 0))
    any_spec = pl.BlockSpec(memory_space=pl.ANY)
    smalls = [g_ple, w_gate, b_gate, w_proj, g_final]
    return pl.pallas_call(
        body, name="ple_fwd_bwd", grid=(nb,),
        in_specs=[row_spec(d), row_spec(pd), row_spec(d)] + [_const_spec(s.shape) for s in smalls],
        out_specs=[row_spec(d), pl.BlockSpec((8, d), lambda i: (0, 0)), any_spec, any_spec],
        out_shape=[jax.ShapeDtypeStruct((t_len, d), F32), jax.ShapeDtypeStruct((8, d), F32),
                   jax.ShapeDtypeStruct(w_gate.shape, BF16), jax.ShapeDtypeStruct(w_proj.shape, BF16)],
        scratch_shapes=[pltpu.VMEM(w_gate.shape, F32), pltpu.VMEM(w_proj.shape, F32), pltpu.VMEM(w_gate.shape, BF16),
                        pltpu.VMEM(w_proj.shape, BF16)],
        compiler_params=_params(),
    )(h2, p, target, *smalls)


def _mlp_bwd_part(part, n_part, dh2, z2, up, w_up, w_down, dz2_prev, h1, g_mlp, scatter, tb):
    t_len, d = dh2.shape
    nb = t_len // tb
    n_chunk_all, _, fc = w_up.shape
    n_chunk = n_chunk_all // n_part
    first, last = part == 0, part == n_part - 1

    def body(*refs):
        refs = list(refs)
        dh2_ref, z2_ref, up_ref, wup_ref, wdn_ref = refs[:5]
        del refs[:5]
        dzp_ref = None if first else refs.pop(0)
        h1_ref, g_ref = (refs.pop(0), refs.pop(0)) if last else (None, None)
        scatter_in = [refs.pop(0) for _ in _scatter_args(scatter)]
        out_ref = refs.pop(0)
        vec_ref = refs.pop(0) if last else None
        dwup_out, dwdn_out = refs.pop(0), refs.pop(0)
        scatter_out = [refs.pop(0) for _ in scatter['blocks']]
        dwup_acc, dwdn_acc, up_stage, dn_stage = refs[:4]
        start_scatter, wait_scatter = _scatter_ops(scatter, scatter_in, scatter_out, refs[4:])
        i = pl.program_id(0)

        @pl.when(i == 0)
        def _():
            start_scatter()
            dwup_acc[...] = jnp.zeros_like(dwup_acc)
            dwdn_acc[...] = jnp.zeros_like(dwdn_acc)
            if last:
                vec_ref[...] = jnp.zeros_like(vec_ref)

        dh2 = dh2_ref[...]
        dh2b = dh2.astype(BF16)
        z2b = z2_ref[...]
        dz2 = jnp.zeros((tb, d), F32) if first else dzp_ref[...]
        for c in range(n_chunk):
            u = up_ref[:, c * fc:(c + 1) * fc].astype(F32)
            ur = jnp.maximum(u, 0.0)
            dact = _dot_nt(dh2b, wdn_ref[c * fc:(c + 1) * fc, :])
            dupb = (dact * (2.0 * ur)).astype(BF16)
            dwdn_acc[c * fc:(c + 1) * fc, :] += _dot_tn((ur * ur).astype(BF16), dh2b)
            dwup_acc[c] += _dot_tn(z2b, dupb)
            dz2 = dz2 + _dot_nt(dupb, wup_ref[c])
        if last:
            g = g_ref[...]
            _, xh, r = _rms_fwd(h1_ref[...], g)
            out_ref[...] = dh2 + _rms_bwd(xh, r, g, dz2)
            vec_ref[0:1, :] += _colsum(dz2 * xh)
        else:
            out_ref[...] = dz2

        @pl.when(i == nb - 1)
        def _():
            for c in range(n_chunk):
                up_stage[...] = dwup_acc[c].astype(BF16)
                dn_stage[...] = dwdn_acc[c * fc:(c + 1) * fc, :].astype(BF16)
                pltpu.sync_copy(up_stage, dwup_out.at[c])
                pltpu.sync_copy(dn_stage, dwdn_out.at[c])
            wait_scatter()

    row_spec = lambda w: pl.BlockSpec((tb, w), lambda i: (i, 0))
    any_spec = pl.BlockSpec(memory_space=pl.ANY)
    args = [dh2, z2, up, w_up, w_down]
    in_specs = [row_spec(d), row_spec(d), pl.BlockSpec((tb, n_chunk * fc), lambda i: (i, part)),
                pl.BlockSpec((n_chunk, d, fc), lambda i: (part, 0, 0), pipeline_mode=pl.Buffered(1)),
                pl.BlockSpec((n_chunk * fc, d), lambda i: (part, 0), pipeline_mode=pl.Buffered(1))]
    if not first:
        args.append(dz2_prev)
        in_specs.append(row_spec(d))
    if last:
        args += [h1, g_mlp]
        in_specs += [row_spec(d), _const_spec(g_mlp.shape)]
    n_in = len(args)
    args += _scatter_args(scatter)
    in_specs += [any_spec] * len(_scatter_args(scatter))
    out_specs = [row_spec(d)]
    out_shape = [jax.ShapeDtypeStruct((t_len, d), F32)]
    if last:
        out_specs.append(pl.BlockSpec((8, d), lambda i: (0, 0)))
        out_shape.append(jax.ShapeDtypeStruct((8, d), F32))
    out_specs += [any_spec, any_spec]
    out_shape += [jax.ShapeDtypeStruct((n_chunk, d, fc), BF16), jax.ShapeDtypeStruct((n_chunk, fc, d), BF16)]
    n_out = len(out_shape)
    out_specs += [any_spec] * len(scatter['blocks'])
    out_shape += _scatter_out_shape(scatter)
    return pl.pallas_call(
        body, name=f"mlp_bwd_{part}", grid=(nb,), in_specs=in_specs, out_specs=out_specs, out_shape=out_shape,
        scratch_shapes=[pltpu.VMEM((n_chunk, d, fc), F32), pltpu.VMEM((n_chunk * fc, d), F32),
                        pltpu.VMEM((d, fc), BF16), pltpu.VMEM((fc, d), BF16)] + _exchange_scratch(len(scatter['blocks'])),
        input_output_aliases=_scatter_aliases(scatter, n_in, n_out), compiler_params=_params(),
    )(*args)


def _mix_bwd(dh1, proj, hs, lru_saved, wp_bd, pool_b, pool_scale, conv_w, wg_bd, lru_l, w_out, scatter, tb):
    t_len, d = dh1.shape
    nb = t_len // tb
    n_s = len(scatter['blocks'])
    scatter_args = _scatter_args(scatter)

    def body(*refs):
        refs = list(refs)
        (dh1_ref, proj_ref, projh_ref, hs_ref, hsh_ref, lru_ref,
         wp_ref, pb_ref, ps_ref, cw_ref, wg_ref, l_ref, wout_ref, invh_ref) = refs[:14]
        del refs[:14]
        scatter_in = refs[:len(scatter_args)]
        del refs[:len(scatter_args)]
        dproj_ref, v512_ref, dpw_ref, dga_ref, dgx_ref = refs[:5]
        recv = refs[5:5 + n_s]
        (dwp_acc, dwg_acc, v1024_ref, ext_ref, b_ref, gs_ref, ehead_ref, dxbhead_ref, hc_ref,
         send_sems, recv_sems, local_sems) = refs[5 + n_s:]
        i = pl.program_id(0)
        tbk = nb - 1 - i

        start_scatter, wait_scatter = _scatter_ops(scatter, scatter_in, recv, (send_sems, recv_sems, local_sems))

        @pl.when(i == 0)
        def _():
            start_scatter()
            for ref in (v512_ref, v1024_ref, dwp_acc, dwg_acc, ehead_ref, dxbhead_ref, hc_ref):
                ref[...] = jnp.zeros_like(ref)

        dcat = _dot_nt(dh1_ref[...].astype(BF16), wout_ref[...])

        proj = proj_ref[...]
        has_prev = (tbk > 0).astype(F32)
        ext_ref[0:HALO, :] = projh_ref[:, 0:1024] * has_prev
        ext_ref[HALO:, :] = proj[:, 0:1024]
        ug = proj[:, 1024:1536]
        n = tb + HALO
        inv_head = invh_ref[...]

        up_ext = ext_ref[:, 0:512]
        win = _pool_windows(up_ext, n, True)[HALO:]
        dpool = _scale_by_inv_count(win, tbk == 0, inv_head) - proj[:, 0:512]
        dpoolb = dpool.astype(BF16)
        q = _dot(dpoolb, wp_ref[...]) + pb_ref[...]
        dyp = dcat[:, 0:512]
        dq = dyp * ps_ref[...]
        dqb = dq.astype(BF16)
        v512_ref[0:1, :] += _colsum(dyp * q)
        v512_ref[1:2, :] += _colsum(dq)
        dwp_acc[...] += _dot_tn(dpoolb, dqb)
        dd = _dot_nt(dqb, wp_ref[...])
        e = _scale_by_inv_count(dd, tbk == 0, inv_head)
        e_ext = jnp.concatenate([e, ehead_ref[...]], axis=0)
        du_pool = _pool_windows(e_ext, n, False)[0:tb] - dd
        ehead_ref[...] = e[0:HALO]

        gel, dgel = _gelu_parts(ug)
        hsv = hs_ref[...]
        dcl = dcat[:, 512:1024]
        dhs = dcl * gel
        dug = dcl * hsv * dgel
        ul_ext = ext_ref[:, 512:1024]
        cw = cw_ref[...]
        xb, r, ig, a = lru_ref[:, 0:512], lru_ref[:, 512:1024], lru_ref[:, 1024:1536], lru_ref[:, 1536:2048]
        first_row = (tbk * tb + lax.broadcasted_iota(jnp.int32, (tb, 1), 0)) == 0
        c_l = LRU_C * _log_sigmoid(l_ref[...])
        a2, m2, mult = _lru_decay(r, a, c_l, first_row)
        b_ref[...] = dhs
        row = lax.broadcasted_iota(jnp.int32, (SUB, 512), 0)

        def group(jj, hnext):
            o = pl.multiple_of((tb // SUB - 1 - jj) * SUB, SUB)
            a8 = lru_ref[pl.ds(o, SUB), 1536:2048]
            d8 = b_ref[pl.ds(o, SUB), :]
            b8 = a8 * d8
            for sh in (1, 2, 4):
                ash = jnp.where(row < SUB - sh, pltpu.roll(a8, SUB - sh, 0), 1.0)
                bsh = jnp.where(row < SUB - sh, pltpu.roll(b8, SUB - sh, 0), 0.0)
                b8 = a8 * bsh + b8
                a8 = a8 * ash
            h8 = a8 * hnext + b8
            gs_ref[pl.ds(o, SUB), :] = d8 + jnp.where(row < SUB - 1, pltpu.roll(h8, SUB - 1, 0), hnext)
            return jnp.broadcast_to(h8[0:1, :], (SUB, 512))

        hc_ref[...] = lax.fori_loop(0, tb // SUB, group, hc_ref[...])
        gsum = gs_ref[...]
        hs_ext = jnp.concatenate([hsh_ref[...] * has_prev, hsv], axis=0)
        hprev = pltpu.roll(hs_ext, 1, 0)[SUB:]
        da = gsum * hprev
        dmult = jnp.where(first_row, 0.0, gsum * (ig * xb))
        di = gsum * mult * xb
        dxb = gsum * mult * ig
        dla = da * a - dmult * a2 * lax.rsqrt(m2)
        dr = dla * c_l
        v512_ref[3:4, :] += _colsum(dla * r)
        dgp = jnp.concatenate([dr * r * (1.0 - r), di * ig * (1.0 - ig)], axis=1)
        dgpb = dgp.astype(BF16)
        v1024_ref[0:1, :] += _colsum(dgp)
        dwg_acc[...] += _dot_tn(xb.astype(BF16), dgpb)
        dxb = dxb + _dot_nt(dgpb, wg_ref[...])
        n8 = tb + SUB
        dxb_ext = jnp.concatenate([dxb, dxbhead_ref[...]], axis=0)
        du_lru = (cw[3:4, :] * dxb + cw[2:3, :] * pltpu.roll(dxb_ext, n8 - 1, 0)[0:tb]
                  + cw[1:2, :] * pltpu.roll(dxb_ext, n8 - 2, 0)[0:tb] + cw[0:1, :] * pltpu.roll(dxb_ext, n8 - 3, 0)[0:tb])
        dxbhead_ref[...] = dxb[0:SUB]
        v512_ref[2:3, :] += _colsum(dxb)
        for j in range(4):
            shifted = ul_ext if j == 0 else pltpu.roll(ul_ext, j, 0)
            v512_ref[4 + (3 - j):5 + (3 - j), :] += _colsum(dxb * shifted[HALO:])

        dproj_ref[...] = jnp.concatenate([du_pool, du_lru, dug], axis=1).astype(BF16)

        @pl.when(i == nb - 1)
        def _():
            v512_ref[3:4, :] = v512_ref[3:4, :] * (LRU_C * _sigmoid(-l_ref[...]))
            v512_ref[8:9, :] = v1024_ref[0:1, 0:512]
            v512_ref[9:10, :] = v1024_ref[0:1, 512:1024]
            for g in range(N_POOL_GROUPS):
                dpw_ref[g * 128:(g + 1) * 128, :] = dwp_acc[g * 128:(g + 1) * 128, g * 128:(g + 1) * 128]
            odd_head = (lax.broadcasted_iota(jnp.int32, (512, 128), 0) // 64) % 2 == 1
            for out_ref, col0 in ((dga_ref, 0), (dgx_ref, 512)):
                pairs = jnp.concatenate([dwg_acc[128 * k:128 * (k + 1), col0 + 128 * k:col0 + 128 * (k + 1)]
                                         for k in range(LRU_HEADS // 2)], axis=0)
                out_ref[...] = jnp.where(odd_head, pltpu.roll(pairs, 64, 1), pairs)[:, 0:64]
            wait_scatter()

    rev = lambda w: pl.BlockSpec((tb, w), lambda i: (nb - 1 - i, 0))
    halo = lambda rows, w: pl.BlockSpec((rows, w), lambda i: (jnp.maximum((nb - 1 - i) * (tb // rows) - 1, 0), 0))
    any_spec = pl.BlockSpec(memory_space=pl.ANY)
    smalls = [wp_bd, pool_b, pool_scale, conv_w, wg_bd, lru_l, w_out, _inv_count_head()]
    return pl.pallas_call(
        body, name="mix_bwd", grid=(nb,),
        in_specs=[rev(d), rev(1536), halo(HALO, 1536), rev(512), halo(SUB, 512), rev(2048)]
        + [_const_spec(s.shape) for s in smalls] + [any_spec] * len(scatter_args),
        out_specs=[rev(1536), pl.BlockSpec((16, 512), lambda i: (0, 0)), pl.BlockSpec((512, 128), lambda i: (0, 0)),
                   pl.BlockSpec((512, 64), lambda i: (0, 0)), pl.BlockSpec((512, 64), lambda i: (0, 0))]
        + [any_spec] * n_s,
        out_shape=[jax.ShapeDtypeStruct((t_len, 1536), BF16), jax.ShapeDtypeStruct((16, 512), F32),
                   jax.ShapeDtypeStruct((512, 128), F32), jax.ShapeDtypeStruct((512, 64), F32),
                   jax.ShapeDtypeStruct((512, 64), F32)]
        + _scatter_out_shape(scatter),
        scratch_shapes=[pltpu.VMEM(wp_bd.shape, F32), pltpu.VMEM(wg_bd.shape, F32), pltpu.VMEM((8, 1024), F32),
                        pltpu.VMEM((tb + HALO, 1024), F32),
                        pltpu.VMEM((tb, 512), F32), pltpu.VMEM((tb, 512), F32), pltpu.VMEM((HALO, 512), F32),
                        pltpu.VMEM((SUB, 512), F32), pltpu.VMEM((SUB, 512), F32)]
        + _exchange_scratch(n_s),
        input_output_aliases=_scatter_aliases(scatter, 6 + len(smalls), 5), compiler_params=_params(),
    )(dh1, proj, proj, hs, hs, lru_saved, *smalls, *scatter_args)


def _wgrad(name, a, b, tb):
    t_len, m = a.shape
    n = b.shape[1]
    nb = t_len // tb

    def body(a_ref, b_ref, out_ref, acc_ref, stage_ref):
        i = pl.program_id(0)

        @pl.when(i == 0)
        def _():
            acc_ref[...] = jnp.zeros_like(acc_ref)

        acc_ref[...] += _dot_tn(a_ref[...], b_ref[...].astype(BF16))

        @pl.when(i == nb - 1)
        def _():
            stage_ref[...] = acc_ref[...].astype(BF16)
            pltpu.sync_copy(stage_ref, out_ref)

    return pl.pallas_call(
        body, name=name, grid=(nb,),
        in_specs=[pl.BlockSpec((tb, m), lambda i: (i, 0)), pl.BlockSpec((tb, n), lambda i: (i, 0))],
        out_specs=pl.BlockSpec(memory_space=pl.ANY), out_shape=jax.ShapeDtypeStruct((m, n), BF16),
        scratch_shapes=[pltpu.VMEM((m, n), F32), pltpu.VMEM((m, n), BF16)], compiler_params=_params(),
    )(a, b)


def _in_bwd(dproj, z1, x, dh1, g_mix, w_in, tb):
    t_len, d = x.shape
    nb = t_len // tb

    def body(dproj_ref, z1_ref, x_ref, dh1_ref, g_ref, win_ref, dx_ref, vec_ref, dwin_out, dwin_acc, dwin_stage):
        i = pl.program_id(0)

        @pl.when(i == 0)
        def _():
            vec_ref[...] = jnp.zeros_like(vec_ref)
            dwin_acc[...] = jnp.zeros_like(dwin_acc)

        dprojb = dproj_ref[...]
        dwin_acc[...] += _dot_tn(z1_ref[...], dprojb)
        dz1 = _dot_nt(dprojb, win_ref[...])
        g = g_ref[...]
        _, xh, rr = _rms_fwd(x_ref[...], g)
        dx_ref[...] = dh1_ref[...] + _rms_bwd(xh, rr, g, dz1)
        vec_ref[0:1, :] += _colsum(dz1 * xh)

        @pl.when(i == nb - 1)
        def _():
            dwin_stage[...] = dwin_acc[...].astype(BF16)
            pltpu.sync_copy(dwin_stage, dwin_out)

    row_spec = lambda w: pl.BlockSpec((tb, w), lambda i: (i, 0))
    return pl.pallas_call(
        body, name="in_bwd", grid=(nb,),
        in_specs=[row_spec(dproj.shape[1]), row_spec(d), row_spec(d), row_spec(d), _const_spec(g_mix.shape),
                  _const_spec(w_in.shape)],
        out_specs=[row_spec(d), pl.BlockSpec((8, d), lambda i: (0, 0)), pl.BlockSpec(memory_space=pl.ANY)],
        out_shape=[jax.ShapeDtypeStruct((t_len, d), F32), jax.ShapeDtypeStruct((8, d), F32),
                   jax.ShapeDtypeStruct(w_in.shape, BF16)],
        scratch_shapes=[pltpu.VMEM(w_in.shape, F32), pltpu.VMEM(w_in.shape, BF16)], compiler_params=_params(),
    )(dproj, z1, x, dh1, g_mix, w_in)


def _exchange(name, gathered, scattered):
    n_g, n = len(gathered), len(gathered) + len(scattered)
    srcs = list(gathered) + list(scattered)
    shapes = [a.shape for a in gathered] + [a.shape[1:] for a in scattered]

    def body(*refs):
        ins, outs = refs[:n], refs[n:2 * n]
        srcs_of = [(lambda s, r=r: r) for r in ins[:n_g]] + [(lambda s, r=r: r.at[s]) for r in ins[n_g:]]
        start, wait = _all_to_all(srcs_of, outs, *refs[2 * n:])
        start()
        wait()

    any_spec = pl.BlockSpec(memory_space=pl.ANY)
    return pl.pallas_call(
        body, name=name, in_specs=[any_spec] * n, out_specs=[any_spec] * n,
        out_shape=[jax.ShapeDtypeStruct((N_DEV,) + tuple(sh), a.dtype) for sh, a in zip(shapes, srcs)],
        scratch_shapes=_exchange_scratch(n),
    )(*srcs)


def _final_exchange(name, whole, by_rows, scattered):
    n_w, n_r, n = len(whole), len(by_rows), len(scattered)
    per = [a.shape[0] // N_DEV for a in by_rows]

    def body(*refs):
        refs = list(refs)
        whole_in, rows_in, ins = refs[:n_w], refs[n_w:n_w + n_r], refs[n_w + n_r:n_w + n_r + n]
        del refs[:n_w + n_r + n]
        whole_out, rows_out, outs = refs[:n_w], refs[n_w:n_w + n_r], refs[n_w + n_r:n_w + n_r + n]
        del refs[:n_w + n_r + n]
        whole_land, rows_land = refs[:n_w], refs[n_w:n_w + n_r]
        send_sems, recv_sems, local_sems, small_send, small_recv = refs[n_w + n_r:]
        me = _my_index()
        start, wait = _all_to_all([(lambda s, r=r: r.at[s]) for r in ins], outs, send_sems, recv_sems, local_sems)
        start()

        def dev(s):
            return (s // 4, (s // 2) % 2, s % 2)

        def rows_of(t, s):
            return pl.ds(s * per[t], per[t])

        def mine(t):
            return pl.ds(pl.multiple_of(me * per[t], 8), per[t])

        def partial(t, s, slot_sem):
            if t < n_w:
                src, dst = whole_in[t], whole_land[t]
            else:
                src, dst = rows_in[t - n_w].at[rows_of(t - n_w, s)], rows_land[t - n_w]
            return pltpu.make_async_remote_copy(
                src_ref=src, dst_ref=dst.at[slot_sem], send_sem=small_send.at[t, s], recv_sem=small_recv.at[t, slot_sem],
                device_id=dev(s), device_id_type=MESH)

        def summed(t, s, rows, sem_slot):
            return pltpu.make_async_remote_copy(
                src_ref=rows_out[t].at[rows], dst_ref=rows_out[t].at[rows], send_sem=small_send.at[n_w + n_r + t, s],
                recv_sem=small_recv.at[n_w + n_r + t, sem_slot], device_id=dev(s), device_id_type=MESH)

        for s in range(N_DEV):
            @pl.when(s != me)
            def _():
                for t in range(n_w + n_r):
                    partial(t, s, me).start()
        for t in range(n_w):
            whole_land[t][me] = whole_in[t][...]
        for t in range(n_r):
            rows_land[t][me] = rows_in[t][mine(t), :]
        for s in range(N_DEV):
            @pl.when(s != me)
            def _():
                for t in range(n_w + n_r):
                    partial(t, s, s).wait_recv()
        for t in range(n_w):
            total = whole_land[t][0]
            for s in range(1, N_DEV):
                total = total + whole_land[t][s]
            whole_out[t][...] = total
        for t in range(n_r):
            total = rows_land[t][0]
            for s in range(1, N_DEV):
                total = total + rows_land[t][s]
            rows_out[t][mine(t), :] = total
        for s in range(N_DEV):
            @pl.when(s != me)
            def _():
                for t in range(n_r):
                    summed(t, s, mine(t), me).start()
        for s in range(N_DEV):
            @pl.when(s != me)
            def _():
                for t in range(n_r):
                    summed(t, s, rows_of(t, s), s).wait_recv()
                    summed(t, s, mine(t), me).wait_send()
                for t in range(n_w + n_r):
                    partial(t, s, me).wait_send()
        wait()

    any_spec = pl.BlockSpec(memory_space=pl.ANY)
    vmem_spec = pl.BlockSpec(memory_space=pltpu.VMEM)
    small = list(whole) + list(by_rows)
    n_sem = n_w + 2 * n_r
    res = pl.pallas_call(
        body, name=name, in_specs=[vmem_spec] * len(small) + [any_spec] * n,
        out_specs=[vmem_spec] * len(small) + [any_spec] * n,
        out_shape=[jax.ShapeDtypeStruct(a.shape, F32) for a in small]
        + [jax.ShapeDtypeStruct(a.shape, a.dtype) for a in scattered],
        scratch_shapes=[pltpu.VMEM((N_DEV,) + a.shape, F32) for a in whole]
        + [pltpu.VMEM((N_DEV, a.shape[0] // N_DEV, a.shape[1]), F32) for a in by_rows] + _exchange_scratch(n)
        + [pltpu.SemaphoreType.DMA((n_sem, N_DEV)), pltpu.SemaphoreType.DMA((n_sem, N_DEV))],
    )(*small, *scattered)
    return res[:n_w], res[n_w:n_w + n_r], res[n_w + n_r:]


def _adam_update(g, w, m, v):
    m_new = ADAM_B1 * m + (1.0 - ADAM_B1) * g
    v_new = ADAM_B2 * v + (1.0 - ADAM_B2) * jnp.square(g)
    m_hat = m_new / (1.0 - ADAM_B1 ** ADAM_STEP)
    v_hat = v_new / (1.0 - ADAM_B2 ** ADAM_STEP)
    return -ADAM_LR * (m_hat / (jnp.sqrt(v_hat) + ADAM_EPS) + ADAM_WD * w), m_new, v_new


def _adamw_groups(name, groups):
    n = len(groups)

    def body(*refs):
        for k in range(n):
            g_ref, w_ref, m_ref, v_ref = refs[4 * k:4 * k + 4]
            g_out, d_out, m_out, v_out = refs[4 * n + 4 * k:4 * n + 4 * k + 4]
            g = g_ref[...]
            g_out[...] = g
            d_out[...], m_out[...], v_out[...] = _adam_update(g, w_ref[...], m_ref[...], v_ref[...])

    flat = [a for grp in groups for a in grp]
    out = pl.pallas_call(body, name=name,
                         out_shape=[jax.ShapeDtypeStruct(grp[0].shape, F32) for grp in groups for _ in range(4)])(*flat)
    return [out[4 * k:4 * k + 4] for k in range(n)]


def _adamw(name, parts, w, m, v, row_block):
    n_src, rows, cols = parts.shape
    rb = min(row_block, rows)

    def body(p_ref, w_ref, m_ref, v_ref, g_out, d_out, m_out, v_out):
        g = p_ref[0].astype(F32)
        for s in range(1, n_src):
            g = g + p_ref[s].astype(F32)
        g_out[...] = g
        d_out[...], m_out[...], v_out[...] = _adam_update(g, w_ref[...], m_ref[...], v_ref[...])

    spec = pl.BlockSpec((rb, cols), lambda i: (i, 0))
    return pl.pallas_call(
        body, name=name, grid=(rows // rb,),
        in_specs=[pl.BlockSpec((n_src, rb, cols), lambda i: (0, i, 0)), spec, spec, spec],
        out_specs=[spec] * 4, out_shape=[jax.ShapeDtypeStruct((rows, cols), F32)] * 4,
        compiler_params=pltpu.CompilerParams(dimension_semantics=("parallel",), vmem_limit_bytes=VMEM_LIMIT),
    )(parts, w, m, v)


def _block_diag(blocks):
    g, a, b = blocks.shape
    eye = jnp.eye(g, dtype=blocks.dtype)
    return (eye[:, None, :, None] * blocks[:, :, None, :]).reshape(g * a, g * b)


def kernel(x, p, norm_mix_g, w_in, pool_w, pool_b, pool_scale, conv_w, conv_b, gate_a_w, gate_a_b, gate_x_w, gate_x_b, lru_L, w_out, norm_mlp_g, w_up, w_down, norm_ple_g, w_ple_gate, b_ple_gate, w_ple_proj, norm_final_g, loss_target, m_norm_mix_g, m_w_in, m_pool_w, m_pool_b, m_pool_scale, m_conv_w, m_conv_b, m_gate_a_w, m_gate_a_b, m_gate_x_w, m_gate_x_b, m_lru_L, m_w_out, m_norm_mlp_g, m_w_up, m_w_down, m_norm_ple_g, m_w_ple_gate, m_b_ple_gate, m_w_ple_proj, m_norm_final_g, v_norm_mix_g, v_w_in, v_pool_w, v_pool_b, v_pool_scale, v_conv_w, v_conv_b, v_gate_a_w, v_gate_a_b, v_gate_x_w, v_gate_x_b, v_lru_L, v_w_out, v_norm_mlp_g, v_w_up, v_w_down, v_norm_ple_g, v_w_ple_gate, v_b_ple_gate, v_w_ple_proj, v_norm_final_g):
    t_len, d = x.shape[1], x.shape[2]
    tbs = {k: min(v, t_len) for k, v in TIME_BLOCKS.items()}
    me = _my_index()

    win_g, wout_g, convw_g = _gather("gather_mixer_weights", [w_in[0].astype(BF16), w_out[0].astype(BF16), conv_w[0]])
    w_in_f = jnp.transpose(win_g, (1, 0, 2)).reshape(d, -1)
    conv_w_f = jnp.transpose(convw_g, (1, 0, 2)).reshape(convw_g.shape[1], -1)
    wp_bd = _block_diag(pool_w[0]).astype(BF16)
    wg_bd = jnp.concatenate([_block_diag(gate_a_w[0]), _block_diag(gate_x_w[0])], axis=1).astype(BF16)
    gate_b2 = jnp.concatenate([gate_a_b.reshape(1, -1), gate_x_b.reshape(1, -1)], axis=1)
    mixer_small = (norm_mix_g, w_in_f, wp_bd, pool_b.reshape(1, -1), pool_scale, conv_w_f, conv_b, wg_bd, gate_b2, lru_L,
                   wout_g.reshape(-1, d))

    x2 = x[0]
    later = [w_up[0].astype(BF16), w_down[0].astype(BF16), w_ple_gate[0].astype(BF16), w_ple_proj[0].astype(BF16)]
    h1, z1, proj, hs, cat, lru_saved, wup_g, wdn_g, wgate_g, wproj_g = _mix_fwd(
        x2, *mixer_small, later, [_core_major_slot, _core_major_slot, None, None], GATHER_FORWARD_AT, tbs['mix_fwd'])
    w_down_f = wdn_g.reshape(-1, d)
    w_proj_f = jnp.transpose(wproj_g, (1, 0, 2)).reshape(wproj_g.shape[1], -1)
    h2, z2, up = _mlp_fwd(h1, norm_mlp_g, wup_g, w_down_f, tbs['mlp_fwd'])
    dh2, ple_vec, dw_gate, dw_proj = _ple(h2, p[0, 0], loss_target[0], norm_ple_g, wgate_g.reshape(-1, d), b_ple_gate,
                                          w_proj_f, norm_final_g.reshape(1, -1), tbs['ple'])
    everyone = list(range(N_DEV))
    n_proj = w_ple_proj.shape[2]
    scatter = _scatter_plan([dw_gate.reshape(N_DEV, -1, d), jnp.transpose(dw_proj.reshape(-1, N_DEV, n_proj), (1, 0, 2))],
                            [everyone, everyone], [None, None])
    dz2_0, dw_up_0, dw_down_0, recv_gate, recv_proj = _mlp_bwd_part(
        0, MLP_BWD_SPLIT, dh2, z2, up, wup_g, w_down_f, None, h1, norm_mlp_g, scatter, tbs['mlp_bwd'])
    half = N_DEV // MLP_BWD_SPLIT
    south = [_device_of_core_major_slot(k) for k in range(half)]
    north = [_device_of_core_major_slot(k) for k in range(half, N_DEV)]
    scatter = _scatter_plan([dw_up_0, dw_down_0], [south, south], [None, None])
    dh1, mlp_vec, dw_up_1, dw_down_1, recv_up, recv_down = _mlp_bwd_part(
        1, MLP_BWD_SPLIT, dh2, z2, up, wup_g, w_down_f, dz2_0, h1, norm_mlp_g, scatter, tbs['mlp_bwd'])
    dw_out = _wgrad("wgrad_out", cat, dh1, tbs['wgrad_out'])
    scatter = _scatter_plan([dw_up_1, dw_down_1, dw_out.reshape(N_DEV, -1, d)], [north, north, everyone],
                            [recv_up, recv_down, None])
    dproj, v512, dpw, dga, dgx, recv_up, recv_down, recv_out = _mix_bwd(
        dh1, proj, hs, lru_saved, wp_bd, pool_b.reshape(1, -1), pool_scale, conv_w_f, wg_bd, lru_L, wout_g.reshape(-1, d),
        scatter, tbs['mix_bwd'])
    dx, in_vec, dw_in = _in_bwd(dproj, z1, x2, dh1, norm_mix_g, w_in_f, tbs['in_bwd'])

    rows1024 = jnp.concatenate([in_vec[0:1], mlp_vec[0:1], ple_vec[1:2], ple_vec[0:1], ple_vec[2:4],
                                jnp.zeros((2, d), F32)], axis=0)
    n_in = w_in.shape[2]
    (rows1024, rows512), (g_pool_w, g_gate_a_w, g_gate_x_w), (recv_in,) = _final_exchange(
        "exchange_last_grads", [rows1024, v512], [dpw, dga, dgx],
        [jnp.transpose(dw_in.reshape(d, N_DEV, n_in), (1, 0, 2))])
    received = [recv_in, recv_out, recv_up, recv_down, recv_gate, recv_proj]

    shard_w = [w_in[0], w_out[0], w_up[0], w_down[0], w_ple_gate[0], w_ple_proj[0]]
    shard_m = [m_w_in[0], m_w_out[0], m_w_up[0], m_w_down[0], m_w_ple_gate[0], m_w_ple_proj[0]]
    shard_v = [v_w_in[0], v_w_out[0], v_w_up[0], v_w_down[0], v_w_ple_gate[0], v_w_ple_proj[0]]
    names = ["w_in", "w_out", "w_up", "w_down", "w_ple_gate", "w_ple_proj"]
    res = {}
    for nm, parts, w_s, m_s, v_s in zip(names, received, shard_w, shard_m, shard_v):
        res[nm] = [r[None] for r in _adamw("adamw_" + nm, parts, w_s, m_s, v_s, 128)]

    def rows_of_1024(a, b, c, e, f):
        return jnp.concatenate([a, b, c, e, f.reshape(1, -1), jnp.zeros((3, d), F32)], axis=0)

    def rows_of_512(scale, bias, cb, lru, ga, gx):
        z = jnp.zeros((1, 512), F32)
        return jnp.concatenate([scale, bias.reshape(1, -1), cb, lru, z, z, z, z, ga.reshape(1, -1), gx.reshape(1, -1),
                                z, z, z, z, z, z], axis=0)

    n_conv = conv_w.shape[2]
    groups = [
        (rows1024, *[rows_of_1024(*t) for t in (
            (norm_mix_g, norm_mlp_g, norm_ple_g, b_ple_gate, norm_final_g),
            (m_norm_mix_g, m_norm_mlp_g, m_norm_ple_g, m_b_ple_gate, m_norm_final_g),
            (v_norm_mix_g, v_norm_mlp_g, v_norm_ple_g, v_b_ple_gate, v_norm_final_g))]),
        (rows512, *[rows_of_512(*t) for t in (
            (pool_scale, pool_b, conv_b, lru_L, gate_a_b, gate_x_b),
            (m_pool_scale, m_pool_b, m_conv_b, m_lru_L, m_gate_a_b, m_gate_x_b),
            (v_pool_scale, v_pool_b, v_conv_b, v_lru_L, v_gate_a_b, v_gate_x_b))]),
        (g_pool_w, *[a.reshape(-1, a.shape[-1]) for a in (pool_w, m_pool_w, v_pool_w)]),
        (g_gate_a_w, *[a.reshape(-1, a.shape[-1]) for a in (gate_a_w, m_gate_a_w, v_gate_a_w)]),
        (g_gate_x_w, *[a.reshape(-1, a.shape[-1]) for a in (gate_x_w, m_gate_x_w, v_gate_x_w)]),
        (lax.dynamic_slice_in_dim(rows512[4:8], me * n_conv, n_conv, axis=1), conv_w[0], m_conv_w[0], v_conv_w[0]),
    ]
    r1024, r512, r_pool, r_ga, r_gx, r_conv = _adamw_groups("adamw_small", groups)
    loss = rows1024[5, 0]
    for k, nm in enumerate(["norm_mix_g", "norm_mlp_g", "norm_ple_g", "b_ple_gate"]):
        res[nm] = [a[k:k + 1] for a in r1024]
    res["norm_final_g"] = [a[4] for a in r1024]
    res["pool_scale"] = [a[0:1] for a in r512]
    res["pool_b"] = [a[1:2].reshape(pool_b.shape) for a in r512]
    res["conv_b"] = [a[2:3] for a in r512]
    res["lru_L"] = [a[3:4] for a in r512]
    res["gate_a_b"] = [a[8:9].reshape(gate_a_b.shape) for a in r512]
    res["gate_x_b"] = [a[9:10].reshape(gate_x_b.shape) for a in r512]
    res["pool_w"] = [a.reshape(pool_w.shape) for a in r_pool]
    res["gate_a_w"] = [a.reshape(gate_a_w.shape) for a in r_ga]
    res["gate_x_w"] = [a.reshape(gate_x_w.shape) for a in r_gx]
    res["conv_w"] = [a[None] for a in r_conv]
    order = ["norm_mix_g", "w_in", "pool_w", "pool_b", "pool_scale", "conv_w", "conv_b", "gate_a_w", "gate_a_b",
             "gate_x_w", "gate_x_b", "lru_L", "w_out", "norm_mlp_g", "w_up", "w_down", "norm_ple_g", "w_ple_gate",
             "b_ple_gate", "w_ple_proj", "norm_final_g"]
    return (loss, dx[None], *[res[nm][kind] for kind in range(4) for nm in order])
```

```python
import functools

import jax
import jax.numpy as jnp
from jax import lax
from jax.experimental import pallas as pl
from jax.experimental.pallas import tpu as pltpu

F32 = jnp.float32
BF16 = jnp.bfloat16
MESH = pl.DeviceIdType.MESH

N_DEV = 8
RMS_EPS = 1e-6
LRU_C = 8.0
POOL_WINDOWS = (2, 4, 8, 16)
N_POOL_GROUPS = 4
LRU_HEADS = 8
HALO = 16
SUB = 8
GELU_C0 = 0.7978845608028654
GELU_C1 = 0.044715

ADAM_LR = 0.001
ADAM_B1 = 0.9
ADAM_B2 = 0.999
ADAM_EPS = 1e-08
ADAM_WD = 0.01
ADAM_STEP = 10

VMEM_LIMIT = 60 * 1024 * 1024
TIME_BLOCKS = dict(mix_fwd=512, mlp_fwd=512, ple=512, mlp_bwd=512, wgrad_out=1024, mix_bwd=512, wgrad_in=1024, in_bwd=512)
MLP_BWD_SPLIT = 2
GATHER_FORWARD_AT = (0.5, 0.875, 1.0, 1.0)


def _params(n_arbitrary=1):
    return pltpu.CompilerParams(dimension_semantics=("arbitrary",) * n_arbitrary, vmem_limit_bytes=VMEM_LIMIT)


def _dot(a, b):
    return jnp.dot(a, b, preferred_element_type=F32)


def _dot_nt(a, b):
    return lax.dot_general(a, b, (((1,), (1,)), ((), ())), preferred_element_type=F32)


def _dot_tn(a, b):
    return lax.dot_general(a, b, (((0,), (0,)), ((), ())), preferred_element_type=F32)


def _rms_fwd(x, g):
    r = lax.rsqrt(jnp.mean(x * x, axis=-1, keepdims=True) + RMS_EPS)
    xh = x * r
    return xh * g, xh, r


def _rms_bwd(xh, r, g, dz):
    dxh = dz * g
    return r * (dxh - xh * jnp.mean(dxh * xh, axis=-1, keepdims=True))


def _colsum(a):
    return jnp.sum(a, axis=0, keepdims=True)


def _sigmoid(a):
    return 0.5 * jnp.tanh(0.5 * a) + 0.5


def _gelu_parts(u):
    u2 = u * u
    th = jnp.tanh(GELU_C0 * (u + GELU_C1 * u * u2))
    gel = 0.5 * u * (1.0 + th)
    dgel = 0.5 * (1.0 + th) + 0.5 * u * (1.0 - th * th) * (GELU_C0 * (1.0 + 3.0 * GELU_C1 * u2))
    return gel, dgel


def _my_index():
    return 4 * lax.axis_index("x") + 2 * lax.axis_index("y") + lax.axis_index("c")


def _all_to_all(srcs_of, dsts, send_sems, recv_sems, local_sems, dests=None):
    n = len(dsts)
    me = _my_index()
    dests = [list(range(N_DEV))] * n if dests is None else dests

    def remote(t, s):
        return pltpu.make_async_remote_copy(
            src_ref=srcs_of[t](s), dst_ref=dsts[t].at[me], send_sem=send_sems.at[t, s], recv_sem=recv_sems.at[t, me],
            device_id=(s // 4, (s // 2) % 2, s % 2), device_id_type=MESH)

    def arrival(t, s):
        return pltpu.make_async_remote_copy(
            src_ref=srcs_of[t](dests[t][0]), dst_ref=dsts[t].at[s], send_sem=send_sems.at[t, s],
            recv_sem=recv_sems.at[t, s], device_id=(s // 4, (s // 2) % 2, s % 2), device_id_type=MESH)

    def local(t, s):
        return pltpu.make_async_copy(srcs_of[t](s), dsts[t].at[s], local_sems.at[t])

    def start():
        for s in range(N_DEV):
            to_s = [t for t in range(n) if s in dests[t]]

            @pl.when(s == me)
            def _():
                for t in to_s:
                    local(t, s).start()

            @pl.when(s != me)
            def _():
                for t in to_s:
                    remote(t, s).start()

    def wait():
        for s in range(N_DEV):
            to_s = [t for t in range(n) if s in dests[t]]

            @pl.when(s == me)
            def _():
                for t in to_s:
                    local(t, s).wait()
                    for src in range(N_DEV):
                        if src != s:
                            arrival(t, src).wait_recv()

            @pl.when(s != me)
            def _():
                for t in to_s:
                    remote(t, s).wait_send()

    return start, wait


N_GATHER_COPIES = 7


def _core_major_slot(dev):
    return 4 * dev[2] + 2 * dev[0] + dev[1]


def _device_of_core_major_slot(k):
    return (k % 4) * 2 + k // 4


def _two_level_gather(srcs, dsts, send_sems, recv_sems, local_sems, slots=None):
    n = len(dsts)
    x, y, c = lax.axis_index("x"), lax.axis_index("y"), lax.axis_index("c")
    me, sibling = (x, y, c), (x, y, 1 - c)
    chips = [(1 - x, y), (x, 1 - y), (1 - x, 1 - y)]

    def slot(t, dev):
        return 4 * dev[0] + 2 * dev[1] + dev[2] if slots is None or slots[t] is None else slots[t](dev)

    def copy(t, k, block, to, src=None):
        return pltpu.make_async_remote_copy(
            src_ref=dsts[t].at[slot(t, block)] if src is None else src, dst_ref=dsts[t].at[slot(t, block)],
            send_sem=send_sems.at[t, k], recv_sem=recv_sems.at[t, k], device_id=to, device_id_type=MESH)

    def local(t):
        return pltpu.make_async_copy(srcs[t], dsts[t].at[slot(t, me)], local_sems.at[t])

    def start():
        for t in range(n):
            local(t).start()
            for j, chip in enumerate(chips):
                copy(t, 1 + j, me, (*chip, c), src=srcs[t]).start()
            copy(t, 0, me, sibling, src=srcs[t]).start()

    def forward(t):
        for j, chip in enumerate(chips):
            copy(t, 1 + j, (*chip, c), me).wait_recv()
            copy(t, 4 + j, (*chip, c), sibling).start()

    def finish():
        for t in range(n):
            copy(t, 0, sibling, me).wait_recv()
            for j, chip in enumerate(chips):
                copy(t, 4 + j, (*chip, 1 - c), me).wait_recv()
            copy(t, 0, me, sibling, src=srcs[t]).wait_send()
            for j, chip in enumerate(chips):
                copy(t, 1 + j, me, (*chip, c), src=srcs[t]).wait_send()
                copy(t, 4 + j, (*chip, c), sibling).wait_send()
            local(t).wait()

    return start, forward, finish


def _hosted_gather(i, nb, forward_at, srcs, dsts, sems, slots=None):
    start, forward, finish = _two_level_gather(srcs, dsts, *sems, slots)

    def after_step():
        for t, f in enumerate(forward_at):
            @pl.when(i == min(nb - 1, int(f * nb)))
            def _():
                forward(t)

        @pl.when(i == nb - 1)
        def _():
            finish()

    return start, after_step


def _gather_scratch(n):
    return [pltpu.SemaphoreType.DMA((n, N_GATHER_COPIES)), pltpu.SemaphoreType.DMA((n, N_GATHER_COPIES)),
            pltpu.SemaphoreType.DMA((n,))]


def _gather(name, srcs):
    n = len(srcs)

    def body(*refs):
        start, forward, finish = _two_level_gather(refs[:n], refs[n:2 * n], *refs[2 * n:])
        start()
        for t in range(n):
            forward(t)
        finish()

    any_spec = pl.BlockSpec(memory_space=pl.ANY)
    return pl.pallas_call(
        body, name=name, in_specs=[any_spec] * n, out_specs=[any_spec] * n,
        out_shape=[jax.ShapeDtypeStruct((N_DEV,) + a.shape, a.dtype) for a in srcs], scratch_shapes=_gather_scratch(n),
    )(*srcs)


def _scatter_plan(blocks, dests, landing):
    return dict(blocks=list(blocks), dests=[list(dd) for dd in dests], landing=list(landing))


def _scatter_args(plan):
    return plan['blocks'] + [a for a in plan['landing'] if a is not None]


def _scatter_out_shape(plan):
    return [jax.ShapeDtypeStruct((N_DEV,) + b.shape[1:], b.dtype) for b in plan['blocks']]


def _scatter_aliases(plan, first_in, first_out):
    given = [t for t, a in enumerate(plan['landing']) if a is not None]
    return {first_in + len(plan['blocks']) + k: first_out + t for k, t in enumerate(given)}


def _scatter_ops(plan, in_refs, out_refs, sems):
    n = len(plan['blocks'])
    srcs_of = [(lambda s, r=in_refs[t], dd=plan['dests'][t]: r.at[dd.index(s)]) for t in range(n)]
    return _all_to_all(srcs_of, out_refs, *sems, dests=plan['dests'])


def _exchange_scratch(n):
    return [pltpu.SemaphoreType.DMA((n, N_DEV)), pltpu.SemaphoreType.DMA((n, N_DEV)), pltpu.SemaphoreType.DMA((n,))]


def _const_spec(shape):
    nd = len(shape)
    return pl.BlockSpec(shape, lambda i: (0,) * nd, pipeline_mode=pl.Buffered(1))


def _pool_windows(up_ext, n, forward):
    sh = (lambda k: k) if forward else (lambda k: n - k)
    s2 = up_ext + pltpu.roll(up_ext, sh(1), 0)
    t4 = s2[:, 128:]
    s4 = t4 + pltpu.roll(t4, sh(2), 0)
    t8 = s4[:, 128:]
    s8 = t8 + pltpu.roll(t8, sh(4), 0)
    t16 = s8[:, 128:]
    s16 = t16 + pltpu.roll(t16, sh(8), 0)
    return jnp.concatenate([s2[:, :128], s4[:, :128], s8[:, :128], s16], axis=1)


def _inv_count_head():
    t = jnp.arange(1, HALO + 1, dtype=F32)[:, None]
    return jnp.concatenate([jnp.broadcast_to(1.0 / jnp.minimum(t, float(w)), (HALO, 128)) for w in POOL_WINDOWS], axis=1)


def _scale_by_inv_count(v, is_first_block, inv_head):
    inv_row = jnp.concatenate([jnp.full((1, 128), 1.0 / w, F32) for w in POOL_WINDOWS], axis=1)
    head = v[0:HALO] * jnp.where(is_first_block, inv_head, inv_row)
    return jnp.concatenate([head, v[HALO:] * inv_row], axis=0)


def _lru_decay(r, a, c_l, first_row):
    a2 = a * a
    m2 = -jnp.tanh(c_l * r) * (a2 + 1.0)
    return a2, m2, jnp.where(first_row, 1.0, jnp.sqrt(m2))


def _log_sigmoid(v):
    return -(jnp.maximum(-v, 0.0) + jnp.log1p(jnp.exp(-jnp.abs(v))))


def _conv_fwd(ul_ext, cw, cb):
    return (cb + cw[3:4, :] * ul_ext + cw[2:3, :] * pltpu.roll(ul_ext, 1, 0)
            + cw[1:2, :] * pltpu.roll(ul_ext, 2, 0) + cw[0:1, :] * pltpu.roll(ul_ext, 3, 0))


def _mix_fwd(x, g_mix, w_in, wp_bd, pool_b, pool_scale, conv_w, conv_b, wg_bd, gate_b, lru_l, w_out, gather_srcs,
             gather_slots, forward_at, tb):
    t_len, d = x.shape
    nb = t_len // tb
    n_g = len(gather_srcs)

    def body(*refs):
        (x_ref, g_ref, win_ref, wp_ref, pb_ref, ps_ref, cw_ref, cb_ref, wg_ref, gb_ref, l_ref, wout_ref,
         invh_ref) = refs[:13]
        gsrc = refs[13:13 + n_g]
        h1_ref, z1_ref, proj_ref, hs_ref, cat_ref, lru_ref = refs[13 + n_g:19 + n_g]
        gdst = refs[19 + n_g:19 + 2 * n_g]
        ext_ref, a_ref, b_ref, hc_ref, send_sems, recv_sems, local_sems = refs[19 + 2 * n_g:]
        i = pl.program_id(0)
        start_gather, after_step = _hosted_gather(i, nb, forward_at, gsrc, gdst, (send_sems, recv_sems, local_sems),
                                                  gather_slots)

        @pl.when(i == 0)
        def _():
            start_gather()
            ext_ref[0:HALO, :] = jnp.zeros((HALO, 1024), F32)
            hc_ref[...] = jnp.zeros_like(hc_ref)

        xv = x_ref[...]
        z, _, _ = _rms_fwd(xv, g_ref[...])
        zb = z.astype(BF16)
        z1_ref[...] = zb
        proj = _dot(zb, win_ref[...])
        proj_ref[...] = proj
        ext_ref[HALO:, :] = proj[:, 0:1024]
        ug = proj[:, 1024:1536]
        n = tb + HALO
        up_ext = ext_ref[:, 0:512]
        win = _pool_windows(up_ext, n, True)[HALO:]
        dpool = _scale_by_inv_count(win, i == 0, invh_ref[...]) - proj[:, 0:512]
        q = _dot(dpool.astype(BF16), wp_ref[...]) + pb_ref[...]
        y_pool = q * ps_ref[...]
        xb = _conv_fwd(ext_ref[:, 512:1024], cw_ref[...], cb_ref[...])[HALO:]
        first_row = (i * tb + lax.broadcasted_iota(jnp.int32, (tb, 1), 0)) == 0
        c_l = LRU_C * _log_sigmoid(l_ref[...])
        gp = _dot(xb.astype(BF16), wg_ref[...]) + gb_ref[...]
        r = _sigmoid(gp[:, :512])
        ig = _sigmoid(gp[:, 512:])
        a = jnp.exp(c_l * r)
        _, _, mult = _lru_decay(r, a, c_l, first_row)
        lru_ref[:, 0:512] = xb
        lru_ref[:, 512:1024] = r
        lru_ref[:, 1024:1536] = ig
        lru_ref[:, 1536:2048] = a
        a_ref[...] = a
        b_ref[...] = mult * (ig * xb)
        row = lax.broadcasted_iota(jnp.int32, (SUB, 512), 0)

        def group(j, hprev):
            o = pl.multiple_of(j * SUB, SUB)
            a8 = a_ref[pl.ds(o, SUB), :]
            b8 = b_ref[pl.ds(o, SUB), :]
            for sh in (1, 2, 4):
                ash = jnp.where(row >= sh, pltpu.roll(a8, sh, 0), 1.0)
                bsh = jnp.where(row >= sh, pltpu.roll(b8, sh, 0), 0.0)
                b8 = a8 * bsh + b8
                a8 = a8 * ash
            h8 = a8 * hprev + b8
            hs_ref[pl.ds(o, SUB), :] = h8
            return jnp.broadcast_to(h8[SUB - 1:SUB, :], (SUB, 512))

        hc_ref[...] = lax.fori_loop(0, tb // SUB, group, hc_ref[...])
        gel, _ = _gelu_parts(ug)
        y_lru = hs_ref[...] * gel
        catb = jnp.concatenate([y_pool, y_lru], axis=1).astype(BF16)
        cat_ref[...] = catb
        h1_ref[...] = xv + _dot(catb, wout_ref[...])
        ext_ref[0:HALO, :] = ext_ref[tb:tb + HALO, :]

        after_step()

    row_spec = lambda w: pl.BlockSpec((tb, w), lambda i: (i, 0))
    any_spec = pl.BlockSpec(memory_space=pl.ANY)
    smalls = [g_mix, w_in, wp_bd, pool_b, pool_scale, conv_w, conv_b, wg_bd, gate_b, lru_l, w_out, _inv_count_head()]
    return pl.pallas_call(
        body, name="mix_fwd", grid=(nb,),
        in_specs=[row_spec(d)] + [_const_spec(s.shape) for s in smalls] + [any_spec] * n_g,
        out_specs=[row_spec(d), row_spec(d), row_spec(1536), row_spec(512), row_spec(1024), row_spec(2048)]
        + [any_spec] * n_g,
        out_shape=[jax.ShapeDtypeStruct((t_len, d), F32), jax.ShapeDtypeStruct((t_len, d), BF16),
                   jax.ShapeDtypeStruct((t_len, 1536), F32), jax.ShapeDtypeStruct((t_len, 512), F32),
                   jax.ShapeDtypeStruct((t_len, 1024), BF16), jax.ShapeDtypeStruct((t_len, 2048), F32)]
        + [jax.ShapeDtypeStruct((N_DEV,) + s.shape, s.dtype) for s in gather_srcs],
        scratch_shapes=[pltpu.VMEM((tb + HALO, 1024), F32), pltpu.VMEM((tb, 512), F32), pltpu.VMEM((tb, 512), F32),
                        pltpu.VMEM((SUB, 512), F32)] + _gather_scratch(n_g),
        compiler_params=_params(),
    )(x, *smalls, *gather_srcs)


def _mlp_fwd(h1, g_mlp, w_up, w_down, tb):
    t_len, d = h1.shape
    nb = t_len // tb
    n_chunk, _, fc = w_up.shape

    def body(h1_ref, g_ref, wup_ref, wdn_ref, h2_ref, z2_ref, up_ref):
        xv = h1_ref[...]
        z, _, _ = _rms_fwd(xv, g_ref[...])
        zb = z.astype(BF16)
        z2_ref[...] = zb
        acc = xv
        for c in range(n_chunk):
            u = _dot(zb, wup_ref[c])
            up_ref[:, c * fc:(c + 1) * fc] = u.astype(BF16)
            act = jnp.square(jnp.maximum(u, 0.0)).astype(BF16)
            acc = acc + _dot(act, wdn_ref[c * fc:(c + 1) * fc, :])
        h2_ref[...] = acc

    row_spec = lambda w: pl.BlockSpec((tb, w), lambda i: (i, 0))
    return pl.pallas_call(
        body, name="mlp_fwd", grid=(nb,),
        in_specs=[row_spec(d), _const_spec(g_mlp.shape), _const_spec(w_up.shape), _const_spec(w_down.shape)],
        out_specs=[row_spec(d), row_spec(d), row_spec(n_chunk * fc)],
        out_shape=[jax.ShapeDtypeStruct((t_len, d), F32), jax.ShapeDtypeStruct((t_len, d), BF16),
                   jax.ShapeDtypeStruct((t_len, n_chunk * fc), BF16)],
        compiler_params=_params(),
    )(h1, g_mlp, w_up, w_down)


def _ple(h2, p, target, g_ple, w_gate, b_gate, w_proj, g_final, tb):
    t_len, d = h2.shape
    nb = t_len // tb
    pd = p.shape[1]

    def body(h2_ref, p_ref, tgt_ref, g_ref, wg_ref, bg_ref, wp_ref, gf_ref,
             dh2_ref, vec_ref, dwg_out, dwp_out, dwg_acc, dwp_acc, dwg_stage, dwp_stage):
        i = pl.program_id(0)

        @pl.when(i == 0)
        def _():
            vec_ref[...] = jnp.zeros_like(vec_ref)
            dwg_acc[...] = jnp.zeros_like(dwg_acc)
            dwp_acc[...] = jnp.zeros_like(dwp_acc)

        h2 = h2_ref[...]
        g2 = g_ref[...]
        z3, xh2, r2 = _rms_fwd(h2, g2)
        z3b = z3.astype(BF16)
        gate = _sigmoid(_dot(z3b, wg_ref[...]) + bg_ref[...])
        pb = p_ref[...].astype(BF16)
        pp = _dot(pb, wp_ref[...])
        h3 = h2 + gate * pp
        gf = gf_ref[...]
        y, xh3, r3 = _rms_fwd(h3, gf)
        err = y - tgt_ref[...]
        loss_rows = jnp.mean(err * err, axis=-1, keepdims=True)
        dy = err * (1.0 / d)
        dh3 = _rms_bwd(xh3, r3, gf, dy)
        dgl = (dh3 * pp) * (gate * (1.0 - gate))
        dpp = dh3 * gate
        dglb = dgl.astype(BF16)
        dwg_acc[...] += _dot_tn(z3b, dglb)
        dwp_acc[...] += _dot_tn(pb, dpp.astype(BF16))
        dz3 = _dot_nt(dglb, wg_ref[...])
        dh2_ref[...] = dh3 + _rms_bwd(xh2, r2, g2, dz3)
        vec_ref[0:1, :] += _colsum(dgl)
        vec_ref[1:2, :] += _colsum(dz3 * xh2)
        vec_ref[2:3, :] += _colsum(dy * xh3)
        vec_ref[3:4, :] += 0.5 * jnp.sum(loss_rows)

        @pl.when(i == nb - 1)
        def _():
            dwg_stage[...] = dwg_acc[...].astype(BF16)
            dwp_stage[...] = dwp_acc[...].astype(BF16)
            pltpu.sync_copy(dwg_stage, dwg_out)
            pltpu.sync_copy(dwp_stage, dwp_out)

    row_spec = lambda w: pl.BlockSpec((tb, w), lambda i: (i, 0))
    any_spec = pl.BlockSpec(memory_space=pl.ANY)
    smalls = [g_ple, w_gate, b_gate, w_proj, g_final]
    return pl.pallas_call(
        body, name="ple_fwd_bwd", grid=(nb,),
        in_specs=[row_spec(d), row_spec(pd), row_spec(d)] + [_const_spec(s.shape) for s in smalls],
        out_specs=[row_spec(d), pl.BlockSpec((8, d), lambda i: (0, 0)), any_spec, any_spec],
        out_shape=[jax.ShapeDtypeStruct((t_len, d), F32), jax.ShapeDtypeStruct((8, d), F32),
                   jax.ShapeDtypeStruct(w_gate.shape, BF16), jax.ShapeDtypeStruct(w_proj.shape, BF16)],
        scratch_shapes=[pltpu.VMEM(w_gate.shape, F32), pltpu.VMEM(w_proj.shape, F32), pltpu.VMEM(w_gate.shape, BF16),
                        pltpu.VMEM(w_proj.shape, BF16)],
        compiler_params=_params(),
    )(h2, p, target, *smalls)


def _mlp_bwd_part(part, n_part, dh2, z2, up, w_up, w_down, dz2_prev, h1, g_mlp, scatter, tb):
    t_len, d = dh2.shape
    nb = t_len // tb
    n_chunk_all, _, fc = w_up.shape
    n_chunk = n_chunk_all // n_part
    first, last = part == 0, part == n_part - 1

    def body(*refs):
        refs = list(refs)
        dh2_ref, z2_ref, up_ref, wup_ref, wdn_ref = refs[:5]
        del refs[:5]
        dzp_ref = None if first else refs.pop(0)
        h1_ref, g_ref = (refs.pop(0), refs.pop(0)) if last else (None, None)
        scatter_in = [refs.pop(0) for _ in _scatter_args(scatter)]
        out_ref = refs.pop(0)
        vec_ref = refs.pop(0) if last else None
        dwup_out, dwdn_out = refs.pop(0), refs.pop(0)
        scatter_out = [refs.pop(0) for _ in scatter['blocks']]
        dwup_acc, dwdn_acc, up_stage, dn_stage = refs[:4]
        start_scatter, wait_scatter = _scatter_ops(scatter, scatter_in, scatter_out, refs[4:])
        i = pl.program_id(0)

        @pl.when(i == 0)
        def _():
            start_scatter()
            dwup_acc[...] = jnp.zeros_like(dwup_acc)
            dwdn_acc[...] = jnp.zeros_like(dwdn_acc)
            if last:
                vec_ref[...] = jnp.zeros_like(vec_ref)

        dh2 = dh2_ref[...]
        dh2b = dh2.astype(BF16)
        z2b = z2_ref[...]
        dz2 = jnp.zeros((tb, d), F32) if first else dzp_ref[...]
        for c in range(n_chunk):
            u = up_ref[:, c * fc:(c + 1) * fc].astype(F32)
            ur = jnp.maximum(u, 0.0)
            dact = _dot_nt(dh2b, wdn_ref[c * fc:(c + 1) * fc, :])
            dupb = (dact * (2.0 * ur)).astype(BF16)
            dwdn_acc[c * fc:(c + 1) * fc, :] += _dot_tn((ur * ur).astype(BF16), dh2b)
            dwup_acc[c] += _dot_tn(z2b, dupb)
            dz2 = dz2 + _dot_nt(dupb, wup_ref[c])
        if last:
            g = g_ref[...]
            _, xh, r = _rms_fwd(h1_ref[...], g)
            out_ref[...] = dh2 + _rms_bwd(xh, r, g, dz2)
            vec_ref[0:1, :] += _colsum(dz2 * xh)
        else:
            out_ref[...] = dz2

        @pl.when(i == nb - 1)
        def _():
            for c in range(n_chunk):
                up_stage[...] = dwup_acc[c].astype(BF16)
                dn_stage[...] = dwdn_acc[c * fc:(c + 1) * fc, :].astype(BF16)
                pltpu.sync_copy(up_stage, dwup_out.at[c])
                pltpu.sync_copy(dn_stage, dwdn_out.at[c])
            wait_scatter()

    row_spec = lambda w: pl.BlockSpec((tb, w), lambda i: (i, 0))
    any_spec = pl.BlockSpec(memory_space=pl.ANY)
    args = [dh2, z2, up, w_up, w_down]
    in_specs = [row_spec(d), row_spec(d), pl.BlockSpec((tb, n_chunk * fc), lambda i: (i, part)),
                pl.BlockSpec((n_chunk, d, fc), lambda i: (part, 0, 0), pipeline_mode=pl.Buffered(1)),
                pl.BlockSpec((n_chunk * fc, d), lambda i: (part, 0), pipeline_mode=pl.Buffered(1))]
    if not first:
        args.append(dz2_prev)
        in_specs.append(row_spec(d))
    if last:
        args += [h1, g_mlp]
        in_specs += [row_spec(d), _const_spec(g_mlp.shape)]
    n_in = len(args)
    args += _scatter_args(scatter)
    in_specs += [any_spec] * len(_scatter_args(scatter))
    out_specs = [row_spec(d)]
    out_shape = [jax.ShapeDtypeStruct((t_len, d), F32)]
    if last:
        out_specs.append(pl.BlockSpec((8, d), lambda i: (0, 0)))
        out_shape.append(jax.ShapeDtypeStruct((8, d), F32))
    out_specs += [any_spec, any_spec]
    out_shape += [jax.ShapeDtypeStruct((n_chunk, d, fc), BF16), jax.ShapeDtypeStruct((n_chunk, fc, d), BF16)]
    n_out = len(out_shape)
    out_specs += [any_spec] * len(scatter['blocks'])
    out_shape += _scatter_out_shape(scatter)
    return pl.pallas_call(
        body, name=f"mlp_bwd_{part}", grid=(nb,), in_specs=in_specs, out_specs=out_specs, out_shape=out_shape,
        scratch_shapes=[pltpu.VMEM((n_chunk, d, fc), F32), pltpu.VMEM((n_chunk * fc, d), F32),
                        pltpu.VMEM((d, fc), BF16), pltpu.VMEM((fc, d), BF16)] + _exchange_scratch(len(scatter['blocks'])),
        input_output_aliases=_scatter_aliases(scatter, n_in, n_out), compiler_params=_params(),
    )(*args)


def _mix_bwd(dh1, proj, hs, lru_saved, wp_bd, pool_b, pool_scale, conv_w, wg_bd, lru_l, w_out, scatter, tb):
    t_len, d = dh1.shape
    nb = t_len // tb
    n_s = len(scatter['blocks'])
    scatter_args = _scatter_args(scatter)

    def body(*refs):
        refs = list(refs)
        (dh1_ref, proj_ref, projh_ref, hs_ref, hsh_ref, lru_ref,
         wp_ref, pb_ref, ps_ref, cw_ref, wg_ref, l_ref, wout_ref, invh_ref) = refs[:14]
        del refs[:14]
        scatter_in = refs[:len(scatter_args)]
        del refs[:len(scatter_args)]
        dproj_ref, v512_ref, dpw_ref, dga_ref, dgx_ref = refs[:5]
        recv = refs[5:5 + n_s]
        (dwp_acc, dwg_acc, v1024_ref, ext_ref, b_ref, gs_ref, ehead_ref, dxbhead_ref, hc_ref,
         send_sems, recv_sems, local_sems) = refs[5 + n_s:]
        i = pl.program_id(0)
        tbk = nb - 1 - i

        start_scatter, wait_scatter = _scatter_ops(scatter, scatter_in, recv, (send_sems, recv_sems, local_sems))

        @pl.when(i == 0)
        def _():
            start_scatter()
            for ref in (v512_ref, v1024_ref, dwp_acc, dwg_acc, ehead_ref, dxbhead_ref, hc_ref):
                ref[...] = jnp.zeros_like(ref)

        dcat = _dot_nt(dh1_ref[...].astype(BF16), wout_ref[...])

        proj = proj_ref[...]
        has_prev = (tbk > 0).astype(F32)
        ext_ref[0:HALO, :] = projh_ref[:, 0:1024] * has_prev
        ext_ref[HALO:, :] = proj[:, 0:1024]
        ug = proj[:, 1024:1536]
        n = tb + HALO
        inv_head = invh_ref[...]

        up_ext = ext_ref[:, 0:512]
        win = _pool_windows(up_ext, n, True)[HALO:]
        dpool = _scale_by_inv_count(win, tbk == 0, inv_head) - proj[:, 0:512]
        dpoolb = dpool.astype(BF16)
        q = _dot(dpoolb, wp_ref[...]) + pb_ref[...]
        dyp = dcat[:, 0:512]
        dq = dyp * ps_ref[...]
        dqb = dq.astype(BF16)
        v512_ref[0:1, :] += _colsum(dyp * q)
        v512_ref[1:2, :] += _colsum(dq)
        dwp_acc[...] += _dot_tn(dpoolb, dqb)
        dd = _dot_nt(dqb, wp_ref[...])
        e = _scale_by_inv_count(dd, tbk == 0, inv_head)
        e_ext = jnp.concatenate([e, ehead_ref[...]], axis=0)
        du_pool = _pool_windows(e_ext, n, False)[0:tb] - dd
        ehead_ref[...] = e[0:HALO]

        gel, dgel = _gelu_parts(ug)
        hsv = hs_ref[...]
        dcl = dcat[:, 512:1024]
        dhs = dcl * gel
        dug = dcl * hsv * dgel
        ul_ext = ext_ref[:, 512:1024]
        cw = cw_ref[...]
        xb, r, ig, a = lru_ref[:, 0:512], lru_ref[:, 512:1024], lru_ref[:, 1024:1536], lru_ref[:, 1536:2048]
        first_row = (tbk * tb + lax.broadcasted_iota(jnp.int32, (tb, 1), 0)) == 0
        c_l = LRU_C * _log_sigmoid(l_ref[...])
        a2, m2, mult = _lru_decay(r, a, c_l, first_row)
        b_ref[...] = dhs
        row = lax.broadcasted_iota(jnp.int32, (SUB, 512), 0)

        def group(jj, hnext):
            o = pl.multiple_of((tb // SUB - 1 - jj) * SUB, SUB)
            a8 = lru_ref[pl.ds(o, SUB), 1536:2048]
            d8 = b_ref[pl.ds(o, SUB), :]
            b8 = a8 * d8
            for sh in (1, 2, 4):
                ash = jnp.where(row < SUB - sh, pltpu.roll(a8, SUB - sh, 0), 1.0)
                bsh = jnp.where(row < SUB - sh, pltpu.roll(b8, SUB - sh, 0), 0.0)
                b8 = a8 * bsh + b8
                a8 = a8 * ash
            h8 = a8 * hnext + b8
            gs_ref[pl.ds(o, SUB), :] = d8 + jnp.where(row < SUB - 1, pltpu.roll(h8, SUB - 1, 0), hnext)
            return jnp.broadcast_to(h8[0:1, :], (SUB, 512))

        hc_ref[...] = lax.fori_loop(0, tb // SUB, group, hc_ref[...])
        gsum = gs_ref[...]
        hs_ext = jnp.concatenate([hsh_ref[...] * has_prev, hsv], axis=0)
        hprev = pltpu.roll(hs_ext, 1, 0)[SUB:]
        da = gsum * hprev
        dmult = jnp.where(first_row, 0.0, gsum * (ig * xb))
        di = gsum * mult * xb
        dxb = gsum * mult * ig
        dla = da * a - dmult * a2 * lax.rsqrt(m2)
        dr = dla * c_l
        v512_ref[3:4, :] += _colsum(dla * r)
        dgp = jnp.concatenate([dr * r * (1.0 - r), di * ig * (1.0 - ig)], axis=1)
        dgpb = dgp.astype(BF16)
        v1024_ref[0:1, :] += _colsum(dgp)
        dwg_acc[...] += _dot_tn(xb.astype(BF16), dgpb)
        dxb = dxb + _dot_nt(dgpb, wg_ref[...])
        n8 = tb + SUB
        dxb_ext = jnp.concatenate([dxb, dxbhead_ref[...]], axis=0)
        du_lru = (cw[3:4, :] * dxb + cw[2:3, :] * pltpu.roll(dxb_ext, n8 - 1, 0)[0:tb]
                  + cw[1:2, :] * pltpu.roll(dxb_ext, n8 - 2, 0)[0:tb] + cw[0:1, :] * pltpu.roll(dxb_ext, n8 - 3, 0)[0:tb])
        dxbhead_ref[...] = dxb[0:SUB]
        v512_ref[2:3, :] += _colsum(dxb)
        for j in range(4):
            shifted = ul_ext if j == 0 else pltpu.roll(ul_ext, j, 0)
            v512_ref[4 + (3 - j):5 + (3 - j), :] += _colsum(dxb * shifted[HALO:])

        dproj_ref[...] = jnp.concatenate([du_pool, du_lru, dug], axis=1).astype(BF16)

        @pl.when(i == nb - 1)
        def _():
            v512_ref[3:4, :] = v512_ref[3:4, :] * (LRU_C * _sigmoid(-l_ref[...]))
            v512_ref[8:9, :] = v1024_ref[0:1, 0:512]
            v512_ref[9:10, :] = v1024_ref[0:1, 512:1024]
            for g in range(N_POOL_GROUPS):
                dpw_ref[g * 128:(g + 1) * 128, :] = dwp_acc[g * 128:(g + 1) * 128, g * 128:(g + 1) * 128]
            odd_head = (lax.broadcasted_iota(jnp.int32, (512, 128), 0) // 64) % 2 == 1
            for out_ref, col0 in ((dga_ref, 0), (dgx_ref, 512)):
                pairs = jnp.concatenate([dwg_acc[128 * k:128 * (k + 1), col0 + 128 * k:col0 + 128 * (k + 1)]
                                         for k in range(LRU_HEADS // 2)], axis=0)
                out_ref[...] = jnp.where(odd_head, pltpu.roll(pairs, 64, 1), pairs)[:, 0:64]
            wait_scatter()

    rev = lambda w: pl.BlockSpec((tb, w), lambda i: (nb - 1 - i, 0))
    halo = lambda rows, w: pl.BlockSpec((rows, w), lambda i: (jnp.maximum((nb - 1 - i) * (tb // rows) - 1, 0), 0))
    any_spec = pl.BlockSpec(memory_space=pl.ANY)
    smalls = [wp_bd, pool_b, pool_scale, conv_w, wg_bd, lru_l, w_out, _inv_count_head()]
    return pl.pallas_call(
        body, name="mix_bwd", grid=(nb,),
        in_specs=[rev(d), rev(1536), halo(HALO, 1536), rev(512), halo(SUB, 512), rev(2048)]
        + [_const_spec(s.shape) for s in smalls] + [any_spec] * len(scatter_args),
        out_specs=[rev(1536), pl.BlockSpec((16, 512), lambda i: (0, 0)), pl.BlockSpec((512, 128), lambda i: (0, 0)),
                   pl.BlockSpec((512, 64), lambda i: (0, 0)), pl.BlockSpec((512, 64), lambda i: (0, 0))]
        + [any_spec] * n_s,
        out_shape=[jax.ShapeDtypeStruct((t_len, 1536), BF16), jax.ShapeDtypeStruct((16, 512), F32),
                   jax.ShapeDtypeStruct((512, 128), F32), jax.ShapeDtypeStruct((512, 64), F32),
                   jax.ShapeDtypeStruct((512, 64), F32)]
        + _scatter_out_shape(scatter),
        scratch_shapes=[pltpu.VMEM(wp_bd.shape, F32), pltpu.VMEM(wg_bd.shape, F32), pltpu.VMEM((8, 1024), F32),
                        pltpu.VMEM((tb + HALO, 1024), F32),
                        pltpu.VMEM((tb, 512), F32), pltpu.VMEM((tb, 512), F32), pltpu.VMEM((HALO, 512), F32),
                        pltpu.VMEM((SUB, 512), F32), pltpu.VMEM((SUB, 512), F32)]
        + _exchange_scratch(n_s),
        input_output_aliases=_scatter_aliases(scatter, 6 + len(smalls), 5), compiler_params=_params(),
    )(dh1, proj, proj, hs, hs, lru_saved, *smalls, *scatter_args)


def _wgrad(name, a, b, whole, by_rows, tb):
    t_len, m = a.shape
    n = b.shape[1]
    nb = t_len // tb
    n_w, n_r = len(whole), len(by_rows)
    n_small = n_w + n_r

    def body(*refs):
        a_ref, b_ref = refs[:2]
        small_in = refs[2:2 + n_small]
        out_ref = refs[2 + n_small]
        small_out = refs[3 + n_small:3 + 2 * n_small]
        acc_ref, stage_ref = refs[3 + 2 * n_small:5 + 2 * n_small]
        rest = refs[5 + 2 * n_small:]
        if n_small:
            send_partials, reduce_and_send_sums, finish_small = _small_allreduce(
                small_in[:n_w], small_in[n_w:], small_out[:n_w], small_out[n_w:], rest[:n_w], rest[n_w:n_small],
                rest[n_small:n_small + n_r], *rest[n_small + n_r:])
        i = pl.program_id(0)

        @pl.when(i == 0)
        def _():
            if n_small:
                send_partials()
            acc_ref[...] = jnp.zeros_like(acc_ref)

        acc_ref[...] += _dot_tn(a_ref[...], b_ref[...].astype(BF16))

        if n_small:
            @pl.when(i == nb // 2)
            def _():
                reduce_and_send_sums()

        @pl.when(i == nb - 1)
        def _():
            stage_ref[...] = acc_ref[...].astype(BF16)
            pltpu.sync_copy(stage_ref, out_ref)
            if n_small:
                finish_small()

    small = list(whole) + list(by_rows)
    vmem_spec = pl.BlockSpec(memory_space=pltpu.VMEM)
    res = pl.pallas_call(
        body, name=name, grid=(nb,),
        in_specs=[pl.BlockSpec((tb, m), lambda i: (i, 0)), pl.BlockSpec((tb, n), lambda i: (i, 0))] + [vmem_spec] * n_small,
        out_specs=[pl.BlockSpec(memory_space=pl.ANY)] + [vmem_spec] * n_small,
        out_shape=[jax.ShapeDtypeStruct((m, n), BF16)] + [jax.ShapeDtypeStruct(s_.shape, F32) for s_ in small],
        scratch_shapes=[pltpu.VMEM((m, n), F32), pltpu.VMEM((m, n), BF16)]
        + (_small_allreduce_scratch(whole, by_rows) if n_small else []),
        compiler_params=_params(),
    )(a, b, *small)
    return res[0], res[1:1 + n_w], res[1 + n_w:]


def _in_bwd(dproj, x, dh1, g_mix, w_in, tb):
    t_len, d = x.shape
    nb = t_len // tb

    def body(dproj_ref, x_ref, dh1_ref, g_ref, win_ref, dx_ref, vec_ref):
        i = pl.program_id(0)

        @pl.when(i == 0)
        def _():
            vec_ref[...] = jnp.zeros_like(vec_ref)

        dz1 = _dot_nt(dproj_ref[...], win_ref[...])
        g = g_ref[...]
        _, xh, rr = _rms_fwd(x_ref[...], g)
        dx_ref[...] = dh1_ref[...] + _rms_bwd(xh, rr, g, dz1)
        vec_ref[0:1, :] += _colsum(dz1 * xh)

    row_spec = lambda w: pl.BlockSpec((tb, w), lambda i: (i, 0))
    return pl.pallas_call(
        body, name="in_bwd", grid=(nb,),
        in_specs=[row_spec(dproj.shape[1]), row_spec(d), row_spec(d), _const_spec(g_mix.shape), _const_spec(w_in.shape)],
        out_specs=[row_spec(d), pl.BlockSpec((8, d), lambda i: (0, 0))],
        out_shape=[jax.ShapeDtypeStruct((t_len, d), F32), jax.ShapeDtypeStruct((8, d), F32)],
        compiler_params=_params(),
    )(dproj, x, dh1, g_mix, w_in)


def _exchange(name, gathered, scattered):
    n_g, n = len(gathered), len(gathered) + len(scattered)
    srcs = list(gathered) + list(scattered)
    shapes = [a.shape for a in gathered] + [a.shape[1:] for a in scattered]

    def body(*refs):
        ins, outs = refs[:n], refs[n:2 * n]
        srcs_of = [(lambda s, r=r: r) for r in ins[:n_g]] + [(lambda s, r=r: r.at[s]) for r in ins[n_g:]]
        start, wait = _all_to_all(srcs_of, outs, *refs[2 * n:])
        start()
        wait()

    any_spec = pl.BlockSpec(memory_space=pl.ANY)
    return pl.pallas_call(
        body, name=name, in_specs=[any_spec] * n, out_specs=[any_spec] * n,
        out_shape=[jax.ShapeDtypeStruct((N_DEV,) + tuple(sh), a.dtype) for sh, a in zip(shapes, srcs)],
        scratch_shapes=_exchange_scratch(n),
    )(*srcs)


def _small_allreduce(whole_in, rows_in, whole_out, rows_out, whole_land, rows_land, rows_sum, send_sems, recv_sems):
    n_w, n_r = len(whole_in), len(rows_in)
    per = [r.shape[0] // N_DEV for r in rows_in]
    me = _my_index()

    def dev(s):
        return (s // 4, (s // 2) % 2, s % 2)

    def rows_of(t, s):
        return pl.ds(s * per[t], per[t])

    def mine(t):
        return pl.ds(pl.multiple_of(me * per[t], 8), per[t])

    def partial(t, s, slot):
        if t < n_w:
            src, dst = whole_in[t], whole_land[t]
        else:
            src, dst = rows_in[t - n_w].at[rows_of(t - n_w, s)], rows_land[t - n_w]
        return pltpu.make_async_remote_copy(
            src_ref=src, dst_ref=dst.at[slot], send_sem=send_sems.at[t, s], recv_sem=recv_sems.at[t, slot],
            device_id=dev(s), device_id_type=MESH)

    def summed(t, s, rows, slot):
        return pltpu.make_async_remote_copy(
            src_ref=rows_sum[t].at[rows], dst_ref=rows_sum[t].at[rows], send_sem=send_sems.at[n_w + n_r + t, s],
            recv_sem=recv_sems.at[n_w + n_r + t, slot], device_id=dev(s), device_id_type=MESH)

    def send_partials():
        for s in range(N_DEV):
            @pl.when(s != me)
            def _():
                for t in range(n_w + n_r):
                    partial(t, s, me).start()
        for t in range(n_w):
            whole_land[t][me] = whole_in[t][...]
        for t in range(n_r):
            rows_land[t][me] = rows_in[t][mine(t), :]

    def reduce_and_send_sums():
        for s in range(N_DEV):
            @pl.when(s != me)
            def _():
                for t in range(n_w + n_r):
                    partial(t, s, s).wait_recv()
        for t in range(n_w):
            total = whole_land[t][0]
            for s in range(1, N_DEV):
                total = total + whole_land[t][s]
            whole_out[t][...] = total
        for t in range(n_r):
            total = rows_land[t][0]
            for s in range(1, N_DEV):
                total = total + rows_land[t][s]
            rows_sum[t][mine(t), :] = total
        for s in range(N_DEV):
            @pl.when(s != me)
            def _():
                for t in range(n_r):
                    summed(t, s, mine(t), me).start()

    def finish():
        for s in range(N_DEV):
            @pl.when(s != me)
            def _():
                for t in range(n_r):
                    summed(t, s, rows_of(t, s), s).wait_recv()
                    summed(t, s, mine(t), me).wait_send()
                for t in range(n_w + n_r):
                    partial(t, s, me).wait_send()
        for t in range(n_r):
            rows_out[t][...] = rows_sum[t][...]

    return send_partials, reduce_and_send_sums, finish


def _small_allreduce_scratch(whole, by_rows):
    n_sem = len(whole) + 2 * len(by_rows)
    return ([pltpu.VMEM((N_DEV,) + a.shape, F32) for a in whole]
            + [pltpu.VMEM((N_DEV, a.shape[0] // N_DEV, a.shape[1]), F32) for a in by_rows]
            + [pltpu.VMEM(a.shape, F32) for a in by_rows]
            + [pltpu.SemaphoreType.DMA((n_sem, N_DEV)), pltpu.SemaphoreType.DMA((n_sem, N_DEV))])


def _final_exchange(name, whole, by_rows, scattered):
    n_w, n_r, n = len(whole), len(by_rows), len(scattered)

    def body(*refs):
        refs = list(refs)
        whole_in, rows_in, ins = refs[:n_w], refs[n_w:n_w + n_r], refs[n_w + n_r:n_w + n_r + n]
        del refs[:n_w + n_r + n]
        whole_out, rows_out, outs = refs[:n_w], refs[n_w:n_w + n_r], refs[n_w + n_r:n_w + n_r + n]
        del refs[:n_w + n_r + n]
        send_sems, recv_sems, local_sems = refs[:3]
        whole_land, rows_land = refs[3:3 + n_w], refs[3 + n_w:3 + n_w + n_r]
        rows_sum = refs[3 + n_w + n_r:3 + n_w + 2 * n_r]
        small_send, small_recv = refs[3 + n_w + 2 * n_r:]
        start, wait = _all_to_all([(lambda s, r=r: r.at[s]) for r in ins], outs, send_sems, recv_sems, local_sems)
        start()
        send_partials, reduce_and_send_sums, finish_small = _small_allreduce(
            whole_in, rows_in, whole_out, rows_out, whole_land, rows_land, rows_sum, small_send, small_recv)
        send_partials()
        reduce_and_send_sums()
        finish_small()
        wait()

    any_spec = pl.BlockSpec(memory_space=pl.ANY)
    vmem_spec = pl.BlockSpec(memory_space=pltpu.VMEM)
    small = list(whole) + list(by_rows)
    res = pl.pallas_call(
        body, name=name, in_specs=[vmem_spec] * len(small) + [any_spec] * n,
        out_specs=[vmem_spec] * len(small) + [any_spec] * n,
        out_shape=[jax.ShapeDtypeStruct(a.shape, F32) for a in small]
        + [jax.ShapeDtypeStruct(a.shape, a.dtype) for a in scattered],
        scratch_shapes=_exchange_scratch(n) + _small_allreduce_scratch(whole, by_rows),
    )(*small, *scattered)
    return res[:n_w], res[n_w:n_w + n_r], res[n_w + n_r:]


def _adam_update(g, w, m, v):
    m_new = ADAM_B1 * m + (1.0 - ADAM_B1) * g
    v_new = ADAM_B2 * v + (1.0 - ADAM_B2) * jnp.square(g)
    m_hat = m_new / (1.0 - ADAM_B1 ** ADAM_STEP)
    v_hat = v_new / (1.0 - ADAM_B2 ** ADAM_STEP)
    return -ADAM_LR * (m_hat / (jnp.sqrt(v_hat) + ADAM_EPS) + ADAM_WD * w), m_new, v_new


def _adamw_groups(name, groups):
    n = len(groups)

    def body(*refs):
        for k in range(n):
            g_ref, w_ref, m_ref, v_ref = refs[4 * k:4 * k + 4]
            g_out, d_out, m_out, v_out = refs[4 * n + 4 * k:4 * n + 4 * k + 4]
            g = g_ref[...]
            g_out[...] = g
            d_out[...], m_out[...], v_out[...] = _adam_update(g, w_ref[...], m_ref[...], v_ref[...])

    flat = [a for grp in groups for a in grp]
    out = pl.pallas_call(body, name=name,
                         out_shape=[jax.ShapeDtypeStruct(grp[0].shape, F32) for grp in groups for _ in range(4)])(*flat)
    return [out[4 * k:4 * k + 4] for k in range(n)]


def _adamw(name, parts, w, m, v, row_block):
    n_src, rows, cols = parts.shape
    rb = min(row_block, rows)

    def body(p_ref, w_ref, m_ref, v_ref, g_out, d_out, m_out, v_out):
        g = p_ref[0].astype(F32)
        for s in range(1, n_src):
            g = g + p_ref[s].astype(F32)
        g_out[...] = g
        d_out[...], m_out[...], v_out[...] = _adam_update(g, w_ref[...], m_ref[...], v_ref[...])

    spec = pl.BlockSpec((rb, cols), lambda i: (i, 0))
    return pl.pallas_call(
        body, name=name, grid=(rows // rb,),
        in_specs=[pl.BlockSpec((n_src, rb, cols), lambda i: (0, i, 0)), spec, spec, spec],
        out_specs=[spec] * 4, out_shape=[jax.ShapeDtypeStruct((rows, cols), F32)] * 4,
        compiler_params=pltpu.CompilerParams(dimension_semantics=("parallel",), vmem_limit_bytes=VMEM_LIMIT),
    )(parts, w, m, v)


def _block_diag(blocks):
    g, a, b = blocks.shape
    eye = jnp.eye(g, dtype=blocks.dtype)
    return (eye[:, None, :, None] * blocks[:, :, None, :]).reshape(g * a, g * b)


def kernel(x, p, norm_mix_g, w_in, pool_w, pool_b, pool_scale, conv_w, conv_b, gate_a_w, gate_a_b, gate_x_w, gate_x_b, lru_L, w_out, norm_mlp_g, w_up, w_down, norm_ple_g, w_ple_gate, b_ple_gate, w_ple_proj, norm_final_g, loss_target, m_norm_mix_g, m_w_in, m_pool_w, m_pool_b, m_pool_scale, m_conv_w, m_conv_b, m_gate_a_w, m_gate_a_b, m_gate_x_w, m_gate_x_b, m_lru_L, m_w_out, m_norm_mlp_g, m_w_up, m_w_down, m_norm_ple_g, m_w_ple_gate, m_b_ple_gate, m_w_ple_proj, m_norm_final_g, v_norm_mix_g, v_w_in, v_pool_w, v_pool_b, v_pool_scale, v_conv_w, v_conv_b, v_gate_a_w, v_gate_a_b, v_gate_x_w, v_gate_x_b, v_lru_L, v_w_out, v_norm_mlp_g, v_w_up, v_w_down, v_norm_ple_g, v_w_ple_gate, v_b_ple_gate, v_w_ple_proj, v_norm_final_g):
    t_len, d = x.shape[1], x.shape[2]
    tbs = {k: min(v, t_len) for k, v in TIME_BLOCKS.items()}
    me = _my_index()

    win_g, wout_g, convw_g = _gather("gather_mixer_weights", [w_in[0].astype(BF16), w_out[0].astype(BF16), conv_w[0]])
    w_in_f = jnp.transpose(win_g, (1, 0, 2)).reshape(d, -1)
    conv_w_f = jnp.transpose(convw_g, (1, 0, 2)).reshape(convw_g.shape[1], -1)
    wp_bd = _block_diag(pool_w[0]).astype(BF16)
    wg_bd = jnp.concatenate([_block_diag(gate_a_w[0]), _block_diag(gate_x_w[0])], axis=1).astype(BF16)
    gate_b2 = jnp.concatenate([gate_a_b.reshape(1, -1), gate_x_b.reshape(1, -1)], axis=1)
    mixer_small = (norm_mix_g, w_in_f, wp_bd, pool_b.reshape(1, -1), pool_scale, conv_w_f, conv_b, wg_bd, gate_b2, lru_L,
                   wout_g.reshape(-1, d))

    x2 = x[0]
    later = [w_up[0].astype(BF16), w_down[0].astype(BF16), w_ple_gate[0].astype(BF16), w_ple_proj[0].astype(BF16)]
    h1, z1, proj, hs, cat, lru_saved, wup_g, wdn_g, wgate_g, wproj_g = _mix_fwd(
        x2, *mixer_small, later, [_core_major_slot, _core_major_slot, None, None], GATHER_FORWARD_AT, tbs['mix_fwd'])
    w_down_f = wdn_g.reshape(-1, d)
    w_proj_f = jnp.transpose(wproj_g, (1, 0, 2)).reshape(wproj_g.shape[1], -1)
    h2, z2, up = _mlp_fwd(h1, norm_mlp_g, wup_g, w_down_f, tbs['mlp_fwd'])
    dh2, ple_vec, dw_gate, dw_proj = _ple(h2, p[0, 0], loss_target[0], norm_ple_g, wgate_g.reshape(-1, d), b_ple_gate,
                                          w_proj_f, norm_final_g.reshape(1, -1), tbs['ple'])
    everyone = list(range(N_DEV))
    n_proj = w_ple_proj.shape[2]
    scatter = _scatter_plan([dw_gate.reshape(N_DEV, -1, d), jnp.transpose(dw_proj.reshape(-1, N_DEV, n_proj), (1, 0, 2))],
                            [everyone, everyone], [None, None])
    dz2_0, dw_up_0, dw_down_0, recv_gate, recv_proj = _mlp_bwd_part(
        0, MLP_BWD_SPLIT, dh2, z2, up, wup_g, w_down_f, None, h1, norm_mlp_g, scatter, tbs['mlp_bwd'])
    half = N_DEV // MLP_BWD_SPLIT
    south = [_device_of_core_major_slot(k) for k in range(half)]
    north = [_device_of_core_major_slot(k) for k in range(half, N_DEV)]
    scatter = _scatter_plan([dw_up_0, dw_down_0], [south, south], [None, None])
    dh1, mlp_vec, dw_up_1, dw_down_1, recv_up, recv_down = _mlp_bwd_part(
        1, MLP_BWD_SPLIT, dh2, z2, up, wup_g, w_down_f, dz2_0, h1, norm_mlp_g, scatter, tbs['mlp_bwd'])
    dw_out, _, _ = _wgrad("wgrad_out", cat, dh1, [], [], tbs['wgrad_out'])
    scatter = _scatter_plan([dw_up_1, dw_down_1, dw_out.reshape(N_DEV, -1, d)], [north, north, everyone],
                            [recv_up, recv_down, None])
    dproj, v512, dpw, dga, dgx, recv_up, recv_down, recv_out = _mix_bwd(
        dh1, proj, hs, lru_saved, wp_bd, pool_b.reshape(1, -1), pool_scale, conv_w_f, wg_bd, lru_L, wout_g.reshape(-1, d),
        scatter, tbs['mix_bwd'])
    rows1024 = jnp.concatenate([jnp.zeros((1, d), F32), mlp_vec[0:1], ple_vec[1:2], ple_vec[0:1], ple_vec[2:4],
                                jnp.zeros((2, d), F32)], axis=0)
    dw_in, (rows1024, rows512), (g_pool_w, g_gate_a_w, g_gate_x_w) = _wgrad(
        "wgrad_in", z1, dproj, [rows1024, v512], [dpw, dga, dgx], tbs['wgrad_in'])
    dx, in_vec = _in_bwd(dproj, x2, dh1, norm_mix_g, w_in_f, tbs['in_bwd'])

    n_in = w_in.shape[2]
    (in_vec,), _, (recv_in,) = _final_exchange(
        "exchange_last_grads", [in_vec], [], [jnp.transpose(dw_in.reshape(d, N_DEV, n_in), (1, 0, 2))])
    rows1024 = rows1024.at[0:1].set(in_vec[0:1])
    received = [recv_in, recv_out, recv_up, recv_down, recv_gate, recv_proj]

    shard_w = [w_in[0], w_out[0], w_up[0], w_down[0], w_ple_gate[0], w_ple_proj[0]]
    shard_m = [m_w_in[0], m_w_out[0], m_w_up[0], m_w_down[0], m_w_ple_gate[0], m_w_ple_proj[0]]
    shard_v = [v_w_in[0], v_w_out[0], v_w_up[0], v_w_down[0], v_w_ple_gate[0], v_w_ple_proj[0]]
    names = ["w_in", "w_out", "w_up", "w_down", "w_ple_gate", "w_ple_proj"]
    res = {}
    for nm, parts, w_s, m_s, v_s in zip(names, received, shard_w, shard_m, shard_v):
        res[nm] = [r[None] for r in _adamw("adamw_" + nm, parts, w_s, m_s, v_s, 128)]

    def rows_of_1024(a, b, c, e, f):
        return jnp.concatenate([a, b, c, e, f.reshape(1, -1), jnp.zeros((3, d), F32)], axis=0)

    def rows_of_512(scale, bias, cb, lru, ga, gx):
        z = jnp.zeros((1, 512), F32)
        return jnp.concatenate([scale, bias.reshape(1, -1), cb, lru, z, z, z, z, ga.reshape(1, -1), gx.reshape(1, -1),
                                z, z, z, z, z, z], axis=0)

    n_conv = conv_w.shape[2]
    groups = [
        (rows1024, *[rows_of_1024(*t) for t in (
            (norm_mix_g, norm_mlp_g, norm_ple_g, b_ple_gate, norm_final_g),
            (m_norm_mix_g, m_norm_mlp_g, m_norm_ple_g, m_b_ple_gate, m_norm_final_g),
            (v_norm_mix_g, v_norm_mlp_g, v_norm_ple_g, v_b_ple_gate, v_norm_final_g))]),
        (rows512, *[rows_of_512(*t) for t in (
            (pool_scale, pool_b, conv_b, lru_L, gate_a_b, gate_x_b),
            (m_pool_scale, m_pool_b, m_conv_b, m_lru_L, m_gate_a_b, m_gate_x_b),
            (v_pool_scale, v_pool_b, v_conv_b, v_lru_L, v_gate_a_b, v_gate_x_b))]),
        (g_pool_w, *[a.reshape(-1, a.shape[-1]) for a in (pool_w, m_pool_w, v_pool_w)]),
        (g_gate_a_w, *[a.reshape(-1, a.shape[-1]) for a in (gate_a_w, m_gate_a_w, v_gate_a_w)]),
        (g_gate_x_w, *[a.reshape(-1, a.shape[-1]) for a in (gate_x_w, m_gate_x_w, v_gate_x_w)]),
        (lax.dynamic_slice_in_dim(rows512[4:8], me * n_conv, n_conv, axis=1), conv_w[0], m_conv_w[0], v_conv_w[0]),
    ]
    r1024, r512, r_pool, r_ga, r_gx, r_conv = _adamw_groups("adamw_small", groups)
    loss = rows1024[5, 0]
    for k, nm in enumerate(["norm_mix_g", "norm_mlp_g", "norm_ple_g", "b_ple_gate"]):
        res[nm] = [a[k:k + 1] for a in r1024]
    res["norm_final_g"] = [a[4] for a in r1024]
    res["pool_scale"] = [a[0:1] for a in r512]
    res["pool_b"] = [a[1:2].reshape(pool_b.shape) for a in r512]
    res["conv_b"] = [a[2:3] for a in r512]
    res["lru_L"] = [a[3:4] for a in r512]
    res["gate_a_b"] = [a[8:9].reshape(gate_a_b.shape) for a in r512]
    res["gate_x_b"] = [a[9:10].reshape(gate_x_b.shape) for a in r512]
    res["pool_w"] = [a.reshape(pool_w.shape) for a in r_pool]
    res["gate_a_w"] = [a.reshape(gate_a_w.shape) for a in r_ga]
    res["gate_x_w"] = [a.reshape(gate_x_w.shape) for a in r_gx]
    res["conv_w"] = [a[None] for a in r_conv]
    order = ["norm_mix_g", "w_in", "pool_w", "pool_b", "pool_scale", "conv_w", "conv_b", "gate_a_w", "gate_a_b",
             "gate_x_w", "gate_x_b", "lru_L", "w_out", "norm_mlp_g", "w_up", "w_down", "norm_ple_g", "w_ple_gate",
             "b_ple_gate", "w_ple_proj", "norm_final_g"]
    return (loss, dx[None], *[res[nm][kind] for kind in range(4) for nm in order])
```

```python
import functools

import jax
import jax.numpy as jnp
from jax import lax
from jax.experimental import pallas as pl
from jax.experimental.pallas import tpu as pltpu

F32 = jnp.float32
BF16 = jnp.bfloat16
MESH = pl.DeviceIdType.MESH

N_DEV = 8
RMS_EPS = 1e-6
LRU_C = 8.0
POOL_WINDOWS = (2, 4, 8, 16)
N_POOL_GROUPS = 4
LRU_HEADS = 8
HALO = 16
SUB = 8
GELU_C0 = 0.7978845608028654
GELU_C1 = 0.044715

ADAM_LR = 0.001
ADAM_B1 = 0.9
ADAM_B2 = 0.999
ADAM_EPS = 1e-08
ADAM_WD = 0.01
ADAM_STEP = 10

VMEM_LIMIT = 60 * 1024 * 1024
TIME_BLOCKS = dict(mix_fwd=512, mlp_fwd=512, ple=512, mlp_bwd=512, wgrad_out=1024, mix_bwd=512, wgrad_in=1024, in_bwd=512)
ADAM_ROW_BLOCK = 512
MLP_BWD_SPLIT = 2
GATHER_FORWARD_AT = (0.5, 0.875, 1.0, 1.0)


def _params(n_arbitrary=1):
    return pltpu.CompilerParams(dimension_semantics=("arbitrary",) * n_arbitrary, vmem_limit_bytes=VMEM_LIMIT)


def _dot(a, b):
    return jnp.dot(a, b, preferred_element_type=F32)


def _dot_nt(a, b):
    return lax.dot_general(a, b, (((1,), (1,)), ((), ())), preferred_element_type=F32)


def _dot_tn(a, b):
    return lax.dot_general(a, b, (((0,), (0,)), ((), ())), preferred_element_type=F32)


def _rms_fwd(x, g):
    r = lax.rsqrt(jnp.mean(x * x, axis=-1, keepdims=True) + RMS_EPS)
    xh = x * r
    return xh * g, xh, r


def _rms_bwd(xh, r, g, dz):
    dxh = dz * g
    return r * (dxh - xh * jnp.mean(dxh * xh, axis=-1, keepdims=True))


def _colsum(a):
    return jnp.sum(a, axis=0, keepdims=True)


def _sigmoid(a):
    return 0.5 * jnp.tanh(0.5 * a) + 0.5


def _gelu_parts(u):
    u2 = u * u
    th = jnp.tanh(GELU_C0 * (u + GELU_C1 * u * u2))
    gel = 0.5 * u * (1.0 + th)
    dgel = 0.5 * (1.0 + th) + 0.5 * u * (1.0 - th * th) * (GELU_C0 * (1.0 + 3.0 * GELU_C1 * u2))
    return gel, dgel


def _my_index():
    return 4 * lax.axis_index("x") + 2 * lax.axis_index("y") + lax.axis_index("c")


def _all_to_all(srcs_of, dsts, send_sems, recv_sems, local_sems, dests=None):
    n = len(dsts)
    me = _my_index()
    dests = [list(range(N_DEV))] * n if dests is None else dests

    def remote(t, s):
        return pltpu.make_async_remote_copy(
            src_ref=srcs_of[t](s), dst_ref=dsts[t].at[me], send_sem=send_sems.at[t, s], recv_sem=recv_sems.at[t, me],
            device_id=(s // 4, (s // 2) % 2, s % 2), device_id_type=MESH)

    def arrival(t, s):
        return pltpu.make_async_remote_copy(
            src_ref=srcs_of[t](dests[t][0]), dst_ref=dsts[t].at[s], send_sem=send_sems.at[t, s],
            recv_sem=recv_sems.at[t, s], device_id=(s // 4, (s // 2) % 2, s % 2), device_id_type=MESH)

    def local(t, s):
        return pltpu.make_async_copy(srcs_of[t](s), dsts[t].at[s], local_sems.at[t])

    def start():
        for s in range(N_DEV):
            to_s = [t for t in range(n) if s in dests[t]]

            @pl.when(s == me)
            def _():
                for t in to_s:
                    local(t, s).start()

            @pl.when(s != me)
            def _():
                for t in to_s:
                    remote(t, s).start()

    def wait():
        for s in range(N_DEV):
            to_s = [t for t in range(n) if s in dests[t]]

            @pl.when(s == me)
            def _():
                for t in to_s:
                    local(t, s).wait()
                    for src in range(N_DEV):
                        if src != s:
                            arrival(t, src).wait_recv()

            @pl.when(s != me)
            def _():
                for t in to_s:
                    remote(t, s).wait_send()

    return start, wait


N_GATHER_COPIES = 7


def _core_major_slot(dev):
    return 4 * dev[2] + 2 * dev[0] + dev[1]


def _device_of_core_major_slot(k):
    return (k % 4) * 2 + k // 4


def _two_level_gather(srcs, dsts, send_sems, recv_sems, local_sems, slots=None):
    n = len(dsts)
    x, y, c = lax.axis_index("x"), lax.axis_index("y"), lax.axis_index("c")
    me, sibling = (x, y, c), (x, y, 1 - c)
    chips = [(1 - x, y), (x, 1 - y), (1 - x, 1 - y)]

    def slot(t, dev):
        return 4 * dev[0] + 2 * dev[1] + dev[2] if slots is None or slots[t] is None else slots[t](dev)

    def copy(t, k, block, to, src=None):
        return pltpu.make_async_remote_copy(
            src_ref=dsts[t].at[slot(t, block)] if src is None else src, dst_ref=dsts[t].at[slot(t, block)],
            send_sem=send_sems.at[t, k], recv_sem=recv_sems.at[t, k], device_id=to, device_id_type=MESH)

    def local(t):
        return pltpu.make_async_copy(srcs[t], dsts[t].at[slot(t, me)], local_sems.at[t])

    def start():
        for t in range(n):
            local(t).start()
            for j, chip in enumerate(chips):
                copy(t, 1 + j, me, (*chip, c), src=srcs[t]).start()
            copy(t, 0, me, sibling, src=srcs[t]).start()

    def forward(t):
        for j, chip in enumerate(chips):
            copy(t, 1 + j, (*chip, c), me).wait_recv()
            copy(t, 4 + j, (*chip, c), sibling).start()

    def finish():
        for t in range(n):
            copy(t, 0, sibling, me).wait_recv()
            for j, chip in enumerate(chips):
                copy(t, 4 + j, (*chip, 1 - c), me).wait_recv()
            copy(t, 0, me, sibling, src=srcs[t]).wait_send()
            for j, chip in enumerate(chips):
                copy(t, 1 + j, me, (*chip, c), src=srcs[t]).wait_send()
                copy(t, 4 + j, (*chip, c), sibling).wait_send()
            local(t).wait()

    return start, forward, finish


def _hosted_gather(i, nb, forward_at, srcs, dsts, sems, slots=None):
    start, forward, finish = _two_level_gather(srcs, dsts, *sems, slots)

    def after_step():
        for t, f in enumerate(forward_at):
            @pl.when(i == min(nb - 1, int(f * nb)))
            def _():
                forward(t)

        @pl.when(i == nb - 1)
        def _():
            finish()

    return start, after_step


def _gather_scratch(n):
    return [pltpu.SemaphoreType.DMA((n, N_GATHER_COPIES)), pltpu.SemaphoreType.DMA((n, N_GATHER_COPIES)),
            pltpu.SemaphoreType.DMA((n,))]


def _gather(name, srcs):
    n = len(srcs)

    def body(*refs):
        start, forward, finish = _two_level_gather(refs[:n], refs[n:2 * n], *refs[2 * n:])
        start()
        for t in range(n):
            forward(t)
        finish()

    any_spec = pl.BlockSpec(memory_space=pl.ANY)
    return pl.pallas_call(
        body, name=name, in_specs=[any_spec] * n, out_specs=[any_spec] * n,
        out_shape=[jax.ShapeDtypeStruct((N_DEV,) + a.shape, a.dtype) for a in srcs], scratch_shapes=_gather_scratch(n),
    )(*srcs)


def _scatter_plan(blocks, dests, landing):
    return dict(blocks=list(blocks), dests=[list(dd) for dd in dests], landing=list(landing))


def _scatter_args(plan):
    return plan['blocks'] + [a for a in plan['landing'] if a is not None]


def _scatter_out_shape(plan):
    return [jax.ShapeDtypeStruct((N_DEV,) + b.shape[1:], b.dtype) for b in plan['blocks']]


def _scatter_aliases(plan, first_in, first_out):
    given = [t for t, a in enumerate(plan['landing']) if a is not None]
    return {first_in + len(plan['blocks']) + k: first_out + t for k, t in enumerate(given)}


def _scatter_ops(plan, in_refs, out_refs, sems):
    n = len(plan['blocks'])
    srcs_of = [(lambda s, r=in_refs[t], dd=plan['dests'][t]: r.at[dd.index(s)]) for t in range(n)]
    return _all_to_all(srcs_of, out_refs, *sems, dests=plan['dests'])


def _exchange_scratch(n):
    return [pltpu.SemaphoreType.DMA((n, N_DEV)), pltpu.SemaphoreType.DMA((n, N_DEV)), pltpu.SemaphoreType.DMA((n,))]


def _const_spec(shape):
    nd = len(shape)
    return pl.BlockSpec(shape, lambda i: (0,) * nd, pipeline_mode=pl.Buffered(1))


def _pool_windows(up_ext, n, forward):
    sh = (lambda k: k) if forward else (lambda k: n - k)
    s2 = up_ext + pltpu.roll(up_ext, sh(1), 0)
    t4 = s2[:, 128:]
    s4 = t4 + pltpu.roll(t4, sh(2), 0)
    t8 = s4[:, 128:]
    s8 = t8 + pltpu.roll(t8, sh(4), 0)
    t16 = s8[:, 128:]
    s16 = t16 + pltpu.roll(t16, sh(8), 0)
    return jnp.concatenate([s2[:, :128], s4[:, :128], s8[:, :128], s16], axis=1)


def _inv_count_head():
    t = jnp.arange(1, HALO + 1, dtype=F32)[:, None]
    return jnp.concatenate([jnp.broadcast_to(1.0 / jnp.minimum(t, float(w)), (HALO, 128)) for w in POOL_WINDOWS], axis=1)


def _scale_by_inv_count(v, is_first_block, inv_head):
    inv_row = jnp.concatenate([jnp.full((1, 128), 1.0 / w, F32) for w in POOL_WINDOWS], axis=1)
    head = v[0:HALO] * jnp.where(is_first_block, inv_head, inv_row)
    return jnp.concatenate([head, v[HALO:] * inv_row], axis=0)


def _lru_decay(r, a, c_l, first_row):
    a2 = a * a
    m2 = -jnp.tanh(c_l * r) * (a2 + 1.0)
    return a2, m2, jnp.where(first_row, 1.0, jnp.sqrt(m2))


def _log_sigmoid(v):
    return -(jnp.maximum(-v, 0.0) + jnp.log1p(jnp.exp(-jnp.abs(v))))


def _conv_fwd(ul_ext, cw, cb):
    return (cb + cw[3:4, :] * ul_ext + cw[2:3, :] * pltpu.roll(ul_ext, 1, 0)
            + cw[1:2, :] * pltpu.roll(ul_ext, 2, 0) + cw[0:1, :] * pltpu.roll(ul_ext, 3, 0))


def _mix_fwd(x, g_mix, w_in, wp_bd, pool_b, pool_scale, conv_w, conv_b, wg_bd, gate_b, lru_l, w_out, gather_srcs,
             gather_slots, forward_at, tb):
    t_len, d = x.shape
    nb = t_len // tb
    n_g = len(gather_srcs)

    def body(*refs):
        (x_ref, g_ref, win_ref, wp_ref, pb_ref, ps_ref, cw_ref, cb_ref, wg_ref, gb_ref, l_ref, wout_ref,
         invh_ref) = refs[:13]
        gsrc = refs[13:13 + n_g]
        h1_ref, z1_ref, proj_ref, hs_ref, cat_ref, lru_ref = refs[13 + n_g:19 + n_g]
        gdst = refs[19 + n_g:19 + 2 * n_g]
        ext_ref, a_ref, b_ref, hc_ref, send_sems, recv_sems, local_sems = refs[19 + 2 * n_g:]
        i = pl.program_id(0)
        start_gather, after_step = _hosted_gather(i, nb, forward_at, gsrc, gdst, (send_sems, recv_sems, local_sems),
                                                  gather_slots)

        @pl.when(i == 0)
        def _():
            start_gather()
            ext_ref[0:HALO, :] = jnp.zeros((HALO, 1024), F32)
            hc_ref[...] = jnp.zeros_like(hc_ref)

        xv = x_ref[...]
        z, _, _ = _rms_fwd(xv, g_ref[...])
        zb = z.astype(BF16)
        z1_ref[...] = zb
        proj = _dot(zb, win_ref[...])
        proj_ref[...] = proj
        ext_ref[HALO:, :] = proj[:, 0:1024]
        ug = proj[:, 1024:1536]
        n = tb + HALO
        up_ext = ext_ref[:, 0:512]
        win = _pool_windows(up_ext, n, True)[HALO:]
        dpool = _scale_by_inv_count(win, i == 0, invh_ref[...]) - proj[:, 0:512]
        q = _dot(dpool.astype(BF16), wp_ref[...]) + pb_ref[...]
        y_pool = q * ps_ref[...]
        xb = _conv_fwd(ext_ref[:, 512:1024], cw_ref[...], cb_ref[...])[HALO:]
        first_row = (i * tb + lax.broadcasted_iota(jnp.int32, (tb, 1), 0)) == 0
        c_l = LRU_C * _log_sigmoid(l_ref[...])
        gp = _dot(xb.astype(BF16), wg_ref[...]) + gb_ref[...]
        r = _sigmoid(gp[:, :512])
        ig = _sigmoid(gp[:, 512:])
        a = jnp.exp(c_l * r)
        _, _, mult = _lru_decay(r, a, c_l, first_row)
        lru_ref[:, 0:512] = xb
        lru_ref[:, 512:1024] = r
        lru_ref[:, 1024:1536] = ig
        lru_ref[:, 1536:2048] = a
        a_ref[...] = a
        b_ref[...] = mult * (ig * xb)
        row = lax.broadcasted_iota(jnp.int32, (SUB, 512), 0)

        def group(j, hprev):
            o = pl.multiple_of(j * SUB, SUB)
            a8 = a_ref[pl.ds(o, SUB), :]
            b8 = b_ref[pl.ds(o, SUB), :]
            for sh in (1, 2, 4):
                ash = jnp.where(row >= sh, pltpu.roll(a8, sh, 0), 1.0)
                bsh = jnp.where(row >= sh, pltpu.roll(b8, sh, 0), 0.0)
                b8 = a8 * bsh + b8
                a8 = a8 * ash
            h8 = a8 * hprev + b8
            hs_ref[pl.ds(o, SUB), :] = h8
            return jnp.broadcast_to(h8[SUB - 1:SUB, :], (SUB, 512))

        hc_ref[...] = lax.fori_loop(0, tb // SUB, group, hc_ref[...])
        gel, _ = _gelu_parts(ug)
        y_lru = hs_ref[...] * gel
        catb = jnp.concatenate([y_pool, y_lru], axis=1).astype(BF16)
        cat_ref[...] = catb
        h1_ref[...] = xv + _dot(catb, wout_ref[...])
        ext_ref[0:HALO, :] = ext_ref[tb:tb + HALO, :]

        after_step()

    row_spec = lambda w: pl.BlockSpec((tb, w), lambda i: (i, 0))
    any_spec = pl.BlockSpec(memory_space=pl.ANY)
    smalls = [g_mix, w_in, wp_bd, pool_b, pool_scale, conv_w, conv_b, wg_bd, gate_b, lru_l, w_out, _inv_count_head()]
    return pl.pallas_call(
        body, name="mix_fwd", grid=(nb,),
        in_specs=[row_spec(d)] + [_const_spec(s.shape) for s in smalls] + [any_spec] * n_g,
        out_specs=[row_spec(d), row_spec(d), row_spec(1536), row_spec(512), row_spec(1024), row_spec(2048)]
        + [any_spec] * n_g,
        out_shape=[jax.ShapeDtypeStruct((t_len, d), F32), jax.ShapeDtypeStruct((t_len, d), BF16),
                   jax.ShapeDtypeStruct((t_len, 1536), F32), jax.ShapeDtypeStruct((t_len, 512), F32),
                   jax.ShapeDtypeStruct((t_len, 1024), BF16), jax.ShapeDtypeStruct((t_len, 2048), F32)]
        + [jax.ShapeDtypeStruct((N_DEV,) + s.shape, s.dtype) for s in gather_srcs],
        scratch_shapes=[pltpu.VMEM((tb + HALO, 1024), F32), pltpu.VMEM((tb, 512), F32), pltpu.VMEM((tb, 512), F32),
                        pltpu.VMEM((SUB, 512), F32)] + _gather_scratch(n_g),
        compiler_params=_params(),
    )(x, *smalls, *gather_srcs)


def _mlp_fwd(h1, g_mlp, w_up, w_down, tb):
    t_len, d = h1.shape
    nb = t_len // tb
    n_chunk, _, fc = w_up.shape

    def body(h1_ref, g_ref, wup_ref, wdn_ref, h2_ref, z2_ref, up_ref):
        xv = h1_ref[...]
        z, _, _ = _rms_fwd(xv, g_ref[...])
        zb = z.astype(BF16)
        z2_ref[...] = zb
        acc = xv
        for c in range(n_chunk):
            u = _dot(zb, wup_ref[c])
            up_ref[:, c * fc:(c + 1) * fc] = u.astype(BF16)
            act = jnp.square(jnp.maximum(u, 0.0)).astype(BF16)
            acc = acc + _dot(act, wdn_ref[c * fc:(c + 1) * fc, :])
        h2_ref[...] = acc

    row_spec = lambda w: pl.BlockSpec((tb, w), lambda i: (i, 0))
    return pl.pallas_call(
        body, name="mlp_fwd", grid=(nb,),
        in_specs=[row_spec(d), _const_spec(g_mlp.shape), _const_spec(w_up.shape), _const_spec(w_down.shape)],
        out_specs=[row_spec(d), row_spec(d), row_spec(n_chunk * fc)],
        out_shape=[jax.ShapeDtypeStruct((t_len, d), F32), jax.ShapeDtypeStruct((t_len, d), BF16),
                   jax.ShapeDtypeStruct((t_len, n_chunk * fc), BF16)],
        compiler_params=_params(),
    )(h1, g_mlp, w_up, w_down)


def _ple(h2, p, target, g_ple, w_gate, b_gate, w_proj, g_final, tb):
    t_len, d = h2.shape
    nb = t_len // tb
    pd = p.shape[1]

    def body(h2_ref, p_ref, tgt_ref, g_ref, wg_ref, bg_ref, wp_ref, gf_ref,
             dh2_ref, vec_ref, dwg_out, dwp_out, dwg_acc, dwp_acc, dwg_stage, dwp_stage):
        i = pl.program_id(0)

        @pl.when(i == 0)
        def _():
            vec_ref[...] = jnp.zeros_like(vec_ref)
            dwg_acc[...] = jnp.zeros_like(dwg_acc)
            dwp_acc[...] = jnp.zeros_like(dwp_acc)

        h2 = h2_ref[...]
        g2 = g_ref[...]
        z3, xh2, r2 = _rms_fwd(h2, g2)
        z3b = z3.astype(BF16)
        gate = _sigmoid(_dot(z3b, wg_ref[...]) + bg_ref[...])
        pb = p_ref[...].astype(BF16)
        pp = _dot(pb, wp_ref[...])
        h3 = h2 + gate * pp
        gf = gf_ref[...]
        y, xh3, r3 = _rms_fwd(h3, gf)
        err = y - tgt_ref[...]
        loss_rows = jnp.mean(err * err, axis=-1, keepdims=True)
        dy = err * (1.0 / d)
        dh3 = _rms_bwd(xh3, r3, gf, dy)
        dgl = (dh3 * pp) * (gate * (1.0 - gate))
        dpp = dh3 * gate
        dglb = dgl.astype(BF16)
        dwg_acc[...] += _dot_tn(z3b, dglb)
        dwp_acc[...] += _dot_tn(pb, dpp.astype(BF16))
        dz3 = _dot_nt(dglb, wg_ref[...])
        dh2_ref[...] = dh3 + _rms_bwd(xh2, r2, g2, dz3)
        vec_ref[0:1, :] += _colsum(dgl)
        vec_ref[1:2, :] += _colsum(dz3 * xh2)
        vec_ref[2:3, :] += _colsum(dy * xh3)
        vec_ref[3:4, :] += 0.5 * jnp.sum(loss_rows)

        @pl.when(i == nb - 1)
        def _():
            dwg_stage[...] = dwg_acc[...].astype(BF16)
            dwp_stage[...] = dwp_acc[...].astype(BF16)
            pltpu.sync_copy(dwg_stage, dwg_out)
            pltpu.sync_copy(dwp_stage, dwp_out)

    row_spec = lambda w: pl.BlockSpec((tb, w), lambda i: (i, 0))
    any_spec = pl.BlockSpec(memory_space=pl.ANY)
    smalls = [g_ple, w_gate, b_gate, w_proj, g_final]
    return pl.pallas_call(
        body, name="ple_fwd_bwd", grid=(nb,),
        in_specs=[row_spec(d), row_spec(pd), row_spec(d)] + [_const_spec(s.shape) for s in smalls],
        out_specs=[row_spec(d), pl.BlockSpec((8, d), lambda i: (0, 0)), any_spec, any_spec],
        out_shape=[jax.ShapeDtypeStruct((t_len, d), F32), jax.ShapeDtypeStruct((8, d), F32),
                   jax.ShapeDtypeStruct(w_gate.shape, BF16), jax.ShapeDtypeStruct(w_proj.shape, BF16)],
        scratch_shapes=[pltpu.VMEM(w_gate.shape, F32), pltpu.VMEM(w_proj.shape, F32), pltpu.VMEM(w_gate.shape, BF16),
                        pltpu.VMEM(w_proj.shape, BF16)],
        compiler_params=_params(),
    )(h2, p, target, *smalls)


def _mlp_bwd_part(part, n_part, dh2, z2, up, w_up, w_down, dz2_prev, h1, g_mlp, scatter, tb):
    t_len, d = dh2.shape
    nb = t_len // tb
    n_chunk_all, _, fc = w_up.shape
    n_chunk = n_chunk_all // n_part
    first, last = part == 0, part == n_part - 1

    def body(*refs):
        refs = list(refs)
        dh2_ref, z2_ref, up_ref, wup_ref, wdn_ref = refs[:5]
        del refs[:5]
        dzp_ref = None if first else refs.pop(0)
        h1_ref, g_ref = (refs.pop(0), refs.pop(0)) if last else (None, None)
        scatter_in = [refs.pop(0) for _ in _scatter_args(scatter)]
        out_ref = refs.pop(0)
        vec_ref = refs.pop(0) if last else None
        dwup_out, dwdn_out = refs.pop(0), refs.pop(0)
        scatter_out = [refs.pop(0) for _ in scatter['blocks']]
        dwup_acc, dwdn_acc, up_stage, dn_stage = refs[:4]
        start_scatter, wait_scatter = _scatter_ops(scatter, scatter_in, scatter_out, refs[4:])
        i = pl.program_id(0)

        @pl.when(i == 0)
        def _():
            start_scatter()
            dwup_acc[...] = jnp.zeros_like(dwup_acc)
            dwdn_acc[...] = jnp.zeros_like(dwdn_acc)
            if last:
                vec_ref[...] = jnp.zeros_like(vec_ref)

        dh2 = dh2_ref[...]
        dh2b = dh2.astype(BF16)
        z2b = z2_ref[...]
        dz2 = jnp.zeros((tb, d), F32) if first else dzp_ref[...]
        for c in range(n_chunk):
            u = up_ref[:, c * fc:(c + 1) * fc].astype(F32)
            ur = jnp.maximum(u, 0.0)
            dact = _dot_nt(dh2b, wdn_ref[c * fc:(c + 1) * fc, :])
            dupb = (dact * (2.0 * ur)).astype(BF16)
            dwdn_acc[c * fc:(c + 1) * fc, :] += _dot_tn((ur * ur).astype(BF16), dh2b)
            dwup_acc[c] += _dot_tn(z2b, dupb)
            dz2 = dz2 + _dot_nt(dupb, wup_ref[c])
        if last:
            g = g_ref[...]
            _, xh, r = _rms_fwd(h1_ref[...], g)
            out_ref[...] = dh2 + _rms_bwd(xh, r, g, dz2)
            vec_ref[0:1, :] += _colsum(dz2 * xh)
        else:
            out_ref[...] = dz2

        @pl.when(i == nb - 1)
        def _():
            for c in range(n_chunk):
                up_stage[...] = dwup_acc[c].astype(BF16)
                dn_stage[...] = dwdn_acc[c * fc:(c + 1) * fc, :].astype(BF16)
                pltpu.sync_copy(up_stage, dwup_out.at[c])
                pltpu.sync_copy(dn_stage, dwdn_out.at[c])
            wait_scatter()

    row_spec = lambda w: pl.BlockSpec((tb, w), lambda i: (i, 0))
    any_spec = pl.BlockSpec(memory_space=pl.ANY)
    args = [dh2, z2, up, w_up, w_down]
    in_specs = [row_spec(d), row_spec(d), pl.BlockSpec((tb, n_chunk * fc), lambda i: (i, part)),
                pl.BlockSpec((n_chunk, d, fc), lambda i: (part, 0, 0), pipeline_mode=pl.Buffered(1)),
                pl.BlockSpec((n_chunk * fc, d), lambda i: (part, 0), pipeline_mode=pl.Buffered(1))]
    if not first:
        args.append(dz2_prev)
        in_specs.append(row_spec(d))
    if last:
        args += [h1, g_mlp]
        in_specs += [row_spec(d), _const_spec(g_mlp.shape)]
    n_in = len(args)
    args += _scatter_args(scatter)
    in_specs += [any_spec] * len(_scatter_args(scatter))
    out_specs = [row_spec(d)]
    out_shape = [jax.ShapeDtypeStruct((t_len, d), F32)]
    if last:
        out_specs.append(pl.BlockSpec((8, d), lambda i: (0, 0)))
        out_shape.append(jax.ShapeDtypeStruct((8, d), F32))
    out_specs += [any_spec, any_spec]
    out_shape += [jax.ShapeDtypeStruct((n_chunk, d, fc), BF16), jax.ShapeDtypeStruct((n_chunk, fc, d), BF16)]
    n_out = len(out_shape)
    out_specs += [any_spec] * len(scatter['blocks'])
    out_shape += _scatter_out_shape(scatter)
    return pl.pallas_call(
        body, name=f"mlp_bwd_{part}", grid=(nb,), in_specs=in_specs, out_specs=out_specs, out_shape=out_shape,
        scratch_shapes=[pltpu.VMEM((n_chunk, d, fc), F32), pltpu.VMEM((n_chunk * fc, d), F32),
                        pltpu.VMEM((d, fc), BF16), pltpu.VMEM((fc, d), BF16)] + _exchange_scratch(len(scatter['blocks'])),
        input_output_aliases=_scatter_aliases(scatter, n_in, n_out), compiler_params=_params(),
    )(*args)


def _mix_bwd(dh1, proj, hs, lru_saved, wp_bd, pool_b, pool_scale, conv_w, wg_bd, lru_l, w_out, scatter, tb):
    t_len, d = dh1.shape
    nb = t_len // tb
    n_s = len(scatter['blocks'])
    scatter_args = _scatter_args(scatter)

    def body(*refs):
        refs = list(refs)
        (dh1_ref, proj_ref, projh_ref, hs_ref, hsh_ref, lru_ref,
         wp_ref, pb_ref, ps_ref, cw_ref, wg_ref, l_ref, wout_ref, invh_ref) = refs[:14]
        del refs[:14]
        scatter_in = refs[:len(scatter_args)]
        del refs[:len(scatter_args)]
        dproj_ref, v512_ref, dpw_ref, dga_ref, dgx_ref = refs[:5]
        recv = refs[5:5 + n_s]
        (dwp_acc, dwg_acc, v1024_ref, ext_ref, b_ref, gs_ref, ehead_ref, dxbhead_ref, hc_ref,
         send_sems, recv_sems, local_sems) = refs[5 + n_s:]
        i = pl.program_id(0)
        tbk = nb - 1 - i

        start_scatter, wait_scatter = _scatter_ops(scatter, scatter_in, recv, (send_sems, recv_sems, local_sems))

        @pl.when(i == 0)
        def _():
            start_scatter()
            for ref in (v512_ref, v1024_ref, dwp_acc, dwg_acc, ehead_ref, dxbhead_ref, hc_ref):
                ref[...] = jnp.zeros_like(ref)

        dcat = _dot_nt(dh1_ref[...].astype(BF16), wout_ref[...])

        proj = proj_ref[...]
        has_prev = (tbk > 0).astype(F32)
        ext_ref[0:HALO, :] = projh_ref[:, 0:1024] * has_prev
        ext_ref[HALO:, :] = proj[:, 0:1024]
        ug = proj[:, 1024:1536]
        n = tb + HALO
        inv_head = invh_ref[...]

        up_ext = ext_ref[:, 0:512]
        win = _pool_windows(up_ext, n, True)[HALO:]
        dpool = _scale_by_inv_count(win, tbk == 0, inv_head) - proj[:, 0:512]
        dpoolb = dpool.astype(BF16)
        q = _dot(dpoolb, wp_ref[...]) + pb_ref[...]
        dyp = dcat[:, 0:512]
        dq = dyp * ps_ref[...]
        dqb = dq.astype(BF16)
        v512_ref[0:1, :] += _colsum(dyp * q)
        v512_ref[1:2, :] += _colsum(dq)
        dwp_acc[...] += _dot_tn(dpoolb, dqb)
        dd = _dot_nt(dqb, wp_ref[...])
        e = _scale_by_inv_count(dd, tbk == 0, inv_head)
        e_ext = jnp.concatenate([e, ehead_ref[...]], axis=0)
        du_pool = _pool_windows(e_ext, n, False)[0:tb] - dd
        ehead_ref[...] = e[0:HALO]

        gel, dgel = _gelu_parts(ug)
        hsv = hs_ref[...]
        dcl = dcat[:, 512:1024]
        dhs = dcl * gel
        dug = dcl * hsv * dgel
        ul_ext = ext_ref[:, 512:1024]
        cw = cw_ref[...]
        xb, r, ig, a = lru_ref[:, 0:512], lru_ref[:, 512:1024], lru_ref[:, 1024:1536], lru_ref[:, 1536:2048]
        first_row = (tbk * tb + lax.broadcasted_iota(jnp.int32, (tb, 1), 0)) == 0
        c_l = LRU_C * _log_sigmoid(l_ref[...])
        a2, m2, mult = _lru_decay(r, a, c_l, first_row)
        b_ref[...] = dhs
        row = lax.broadcasted_iota(jnp.int32, (SUB, 512), 0)

        def group(jj, hnext):
            o = pl.multiple_of((tb // SUB - 1 - jj) * SUB, SUB)
            a8 = lru_ref[pl.ds(o, SUB), 1536:2048]
            d8 = b_ref[pl.ds(o, SUB), :]
            b8 = a8 * d8
            for sh in (1, 2, 4):
                ash = jnp.where(row < SUB - sh, pltpu.roll(a8, SUB - sh, 0), 1.0)
                bsh = jnp.where(row < SUB - sh, pltpu.roll(b8, SUB - sh, 0), 0.0)
                b8 = a8 * bsh + b8
                a8 = a8 * ash
            h8 = a8 * hnext + b8
            gs_ref[pl.ds(o, SUB), :] = d8 + jnp.where(row < SUB - 1, pltpu.roll(h8, SUB - 1, 0), hnext)
            return jnp.broadcast_to(h8[0:1, :], (SUB, 512))

        hc_ref[...] = lax.fori_loop(0, tb // SUB, group, hc_ref[...])
        gsum = gs_ref[...]
        hs_ext = jnp.concatenate([hsh_ref[...] * has_prev, hsv], axis=0)
        hprev = pltpu.roll(hs_ext, 1, 0)[SUB:]
        da = gsum * hprev
        dmult = jnp.where(first_row, 0.0, gsum * (ig * xb))
        di = gsum * mult * xb
        dxb = gsum * mult * ig
        dla = da * a - dmult * a2 * lax.rsqrt(m2)
        dr = dla * c_l
        v512_ref[3:4, :] += _colsum(dla * r)
        dgp = jnp.concatenate([dr * r * (1.0 - r), di * ig * (1.0 - ig)], axis=1)
        dgpb = dgp.astype(BF16)
        v1024_ref[0:1, :] += _colsum(dgp)
        dwg_acc[...] += _dot_tn(xb.astype(BF16), dgpb)
        dxb = dxb + _dot_nt(dgpb, wg_ref[...])
        n8 = tb + SUB
        dxb_ext = jnp.concatenate([dxb, dxbhead_ref[...]], axis=0)
        du_lru = (cw[3:4, :] * dxb + cw[2:3, :] * pltpu.roll(dxb_ext, n8 - 1, 0)[0:tb]
                  + cw[1:2, :] * pltpu.roll(dxb_ext, n8 - 2, 0)[0:tb] + cw[0:1, :] * pltpu.roll(dxb_ext, n8 - 3, 0)[0:tb])
        dxbhead_ref[...] = dxb[0:SUB]
        v512_ref[2:3, :] += _colsum(dxb)
        for j in range(4):
            shifted = ul_ext if j == 0 else pltpu.roll(ul_ext, j, 0)
            v512_ref[4 + (3 - j):5 + (3 - j), :] += _colsum(dxb * shifted[HALO:])

        dproj_ref[...] = jnp.concatenate([du_pool, du_lru, dug], axis=1).astype(BF16)

        @pl.when(i == nb - 1)
        def _():
            v512_ref[3:4, :] = v512_ref[3:4, :] * (LRU_C * _sigmoid(-l_ref[...]))
            v512_ref[8:9, :] = v1024_ref[0:1, 0:512]
            v512_ref[9:10, :] = v1024_ref[0:1, 512:1024]
            for g in range(N_POOL_GROUPS):
                dpw_ref[g * 128:(g + 1) * 128, :] = dwp_acc[g * 128:(g + 1) * 128, g * 128:(g + 1) * 128]
            odd_head = (lax.broadcasted_iota(jnp.int32, (512, 128), 0) // 64) % 2 == 1
            for out_ref, col0 in ((dga_ref, 0), (dgx_ref, 512)):
                pairs = jnp.concatenate([dwg_acc[128 * k:128 * (k + 1), col0 + 128 * k:col0 + 128 * (k + 1)]
                                         for k in range(LRU_HEADS // 2)], axis=0)
                out_ref[...] = jnp.where(odd_head, pltpu.roll(pairs, 64, 1), pairs)[:, 0:64]
            wait_scatter()

    rev = lambda w: pl.BlockSpec((tb, w), lambda i: (nb - 1 - i, 0))
    halo = lambda rows, w: pl.BlockSpec((rows, w), lambda i: (jnp.maximum((nb - 1 - i) * (tb // rows) - 1, 0), 0))
    any_spec = pl.BlockSpec(memory_space=pl.ANY)
    smalls = [wp_bd, pool_b, pool_scale, conv_w, wg_bd, lru_l, w_out, _inv_count_head()]
    return pl.pallas_call(
        body, name="mix_bwd", grid=(nb,),
        in_specs=[rev(d), rev(1536), halo(HALO, 1536), rev(512), halo(SUB, 512), rev(2048)]
        + [_const_spec(s.shape) for s in smalls] + [any_spec] * len(scatter_args),
        out_specs=[rev(1536), pl.BlockSpec((16, 512), lambda i: (0, 0)), pl.BlockSpec((512, 128), lambda i: (0, 0)),
                   pl.BlockSpec((512, 64), lambda i: (0, 0)), pl.BlockSpec((512, 64), lambda i: (0, 0))]
        + [any_spec] * n_s,
        out_shape=[jax.ShapeDtypeStruct((t_len, 1536), BF16), jax.ShapeDtypeStruct((16, 512), F32),
                   jax.ShapeDtypeStruct((512, 128), F32), jax.ShapeDtypeStruct((512, 64), F32),
                   jax.ShapeDtypeStruct((512, 64), F32)]
        + _scatter_out_shape(scatter),
        scratch_shapes=[pltpu.VMEM(wp_bd.shape, F32), pltpu.VMEM(wg_bd.shape, F32), pltpu.VMEM((8, 1024), F32),
                        pltpu.VMEM((tb + HALO, 1024), F32),
                        pltpu.VMEM((tb, 512), F32), pltpu.VMEM((tb, 512), F32), pltpu.VMEM((HALO, 512), F32),
                        pltpu.VMEM((SUB, 512), F32), pltpu.VMEM((SUB, 512), F32)]
        + _exchange_scratch(n_s),
        input_output_aliases=_scatter_aliases(scatter, 6 + len(smalls), 5), compiler_params=_params(),
    )(dh1, proj, proj, hs, hs, lru_saved, *smalls, *scatter_args)


def _wgrad(name, a, b, whole, by_rows, tb):
    t_len, m = a.shape
    n = b.shape[1]
    nb = t_len // tb
    n_w, n_r = len(whole), len(by_rows)
    n_small = n_w + n_r

    def body(*refs):
        a_ref, b_ref = refs[:2]
        small_in = refs[2:2 + n_small]
        out_ref = refs[2 + n_small]
        small_out = refs[3 + n_small:3 + 2 * n_small]
        acc_ref, stage_ref = refs[3 + 2 * n_small:5 + 2 * n_small]
        rest = refs[5 + 2 * n_small:]
        if n_small:
            send_partials, reduce_and_send_sums, finish_small = _small_allreduce(
                small_in[:n_w], small_in[n_w:], small_out[:n_w], small_out[n_w:], rest[:n_w], rest[n_w:n_small],
                rest[n_small:n_small + n_r], *rest[n_small + n_r:])
        i = pl.program_id(0)

        @pl.when(i == 0)
        def _():
            if n_small:
                send_partials()
            acc_ref[...] = jnp.zeros_like(acc_ref)

        acc_ref[...] += _dot_tn(a_ref[...], b_ref[...].astype(BF16))

        if n_small:
            @pl.when(i == nb // 2)
            def _():
                reduce_and_send_sums()

        @pl.when(i == nb - 1)
        def _():
            stage_ref[...] = acc_ref[...].astype(BF16)
            pltpu.sync_copy(stage_ref, out_ref)
            if n_small:
                finish_small()

    small = list(whole) + list(by_rows)
    vmem_spec = pl.BlockSpec(memory_space=pltpu.VMEM)
    res = pl.pallas_call(
        body, name=name, grid=(nb,),
        in_specs=[pl.BlockSpec((tb, m), lambda i: (i, 0)), pl.BlockSpec((tb, n), lambda i: (i, 0))] + [vmem_spec] * n_small,
        out_specs=[pl.BlockSpec(memory_space=pl.ANY)] + [vmem_spec] * n_small,
        out_shape=[jax.ShapeDtypeStruct((m, n), BF16)] + [jax.ShapeDtypeStruct(s_.shape, F32) for s_ in small],
        scratch_shapes=[pltpu.VMEM((m, n), F32), pltpu.VMEM((m, n), BF16)]
        + (_small_allreduce_scratch(whole, by_rows) if n_small else []),
        compiler_params=_params(),
    )(a, b, *small)
    return res[0], res[1:1 + n_w], res[1 + n_w:]


def _in_bwd(dproj, x, dh1, g_mix, w_in, scatter, tb):
    t_len, d = x.shape
    nb = t_len // tb
    n_s = len(scatter['blocks'])
    scatter_args = _scatter_args(scatter)

    def body(*refs):
        dproj_ref, x_ref, dh1_ref, g_ref, win_ref = refs[:5]
        scatter_in = refs[5:5 + len(scatter_args)]
        dx_ref, vec_ref = refs[5 + len(scatter_args):7 + len(scatter_args)]
        recv = refs[7 + len(scatter_args):7 + len(scatter_args) + n_s]
        vec_acc, send_sems, recv_sems, local_sems, vec_land, small_send, small_recv = refs[7 + len(scatter_args) + n_s:]
        start_scatter, wait_scatter = _scatter_ops(scatter, scatter_in, recv, (send_sems, recv_sems, local_sems))
        send_partials, reduce_and_send_sums, finish_small = _small_allreduce(
            [vec_acc], [], [vec_ref], [], [vec_land], [], [], small_send, small_recv)
        i = pl.program_id(0)

        @pl.when(i == 0)
        def _():
            start_scatter()
            vec_acc[...] = jnp.zeros_like(vec_acc)

        dz1 = _dot_nt(dproj_ref[...], win_ref[...])
        g = g_ref[...]
        _, xh, rr = _rms_fwd(x_ref[...], g)
        dx_ref[...] = dh1_ref[...] + _rms_bwd(xh, rr, g, dz1)
        vec_acc[0:1, :] += _colsum(dz1 * xh)

        @pl.when(i == nb - 1)
        def _():
            send_partials()
            reduce_and_send_sums()
            finish_small()
            wait_scatter()

    row_spec = lambda w: pl.BlockSpec((tb, w), lambda i: (i, 0))
    any_spec = pl.BlockSpec(memory_space=pl.ANY)
    return pl.pallas_call(
        body, name="in_bwd", grid=(nb,),
        in_specs=[row_spec(dproj.shape[1]), row_spec(d), row_spec(d), _const_spec(g_mix.shape), _const_spec(w_in.shape)]
        + [any_spec] * len(scatter_args),
        out_specs=[row_spec(d), pl.BlockSpec((8, d), lambda i: (0, 0))] + [any_spec] * n_s,
        out_shape=[jax.ShapeDtypeStruct((t_len, d), F32), jax.ShapeDtypeStruct((8, d), F32)] + _scatter_out_shape(scatter),
        scratch_shapes=[pltpu.VMEM((8, d), F32)] + _exchange_scratch(n_s)
        + _small_allreduce_scratch([jax.ShapeDtypeStruct((8, d), F32)], []),
        input_output_aliases=_scatter_aliases(scatter, 5, 2), compiler_params=_params(),
    )(dproj, x, dh1, g_mix, w_in, *scatter_args)


def _exchange(name, gathered, scattered):
    n_g, n = len(gathered), len(gathered) + len(scattered)
    srcs = list(gathered) + list(scattered)
    shapes = [a.shape for a in gathered] + [a.shape[1:] for a in scattered]

    def body(*refs):
        ins, outs = refs[:n], refs[n:2 * n]
        srcs_of = [(lambda s, r=r: r) for r in ins[:n_g]] + [(lambda s, r=r: r.at[s]) for r in ins[n_g:]]
        start, wait = _all_to_all(srcs_of, outs, *refs[2 * n:])
        start()
        wait()

    any_spec = pl.BlockSpec(memory_space=pl.ANY)
    return pl.pallas_call(
        body, name=name, in_specs=[any_spec] * n, out_specs=[any_spec] * n,
        out_shape=[jax.ShapeDtypeStruct((N_DEV,) + tuple(sh), a.dtype) for sh, a in zip(shapes, srcs)],
        scratch_shapes=_exchange_scratch(n),
    )(*srcs)


def _small_allreduce(whole_in, rows_in, whole_out, rows_out, whole_land, rows_land, rows_sum, send_sems, recv_sems):
    n_w, n_r = len(whole_in), len(rows_in)
    per = [r.shape[0] // N_DEV for r in rows_in]
    me = _my_index()

    def dev(s):
        return (s // 4, (s // 2) % 2, s % 2)

    def rows_of(t, s):
        return pl.ds(s * per[t], per[t])

    def mine(t):
        return pl.ds(pl.multiple_of(me * per[t], 8), per[t])

    def partial(t, s, slot):
        if t < n_w:
            src, dst = whole_in[t], whole_land[t]
        else:
            src, dst = rows_in[t - n_w].at[rows_of(t - n_w, s)], rows_land[t - n_w]
        return pltpu.make_async_remote_copy(
            src_ref=src, dst_ref=dst.at[slot], send_sem=send_sems.at[t, s], recv_sem=recv_sems.at[t, slot],
            device_id=dev(s), device_id_type=MESH)

    def summed(t, s, rows, slot):
        return pltpu.make_async_remote_copy(
            src_ref=rows_sum[t].at[rows], dst_ref=rows_sum[t].at[rows], send_sem=send_sems.at[n_w + n_r + t, s],
            recv_sem=recv_sems.at[n_w + n_r + t, slot], device_id=dev(s), device_id_type=MESH)

    def send_partials():
        for s in range(N_DEV):
            @pl.when(s != me)
            def _():
                for t in range(n_w + n_r):
                    partial(t, s, me).start()
        for t in range(n_w):
            whole_land[t][me] = whole_in[t][...]
        for t in range(n_r):
            rows_land[t][me] = rows_in[t][mine(t), :]

    def reduce_and_send_sums():
        for s in range(N_DEV):
            @pl.when(s != me)
            def _():
                for t in range(n_w + n_r):
                    partial(t, s, s).wait_recv()
        for t in range(n_w):
            total = whole_land[t][0]
            for s in range(1, N_DEV):
                total = total + whole_land[t][s]
            whole_out[t][...] = total
        for t in range(n_r):
            total = rows_land[t][0]
            for s in range(1, N_DEV):
                total = total + rows_land[t][s]
            rows_sum[t][mine(t), :] = total
        for s in range(N_DEV):
            @pl.when(s != me)
            def _():
                for t in range(n_r):
                    summed(t, s, mine(t), me).start()

    def finish():
        for s in range(N_DEV):
            @pl.when(s != me)
            def _():
                for t in range(n_r):
                    summed(t, s, rows_of(t, s), s).wait_recv()
                    summed(t, s, mine(t), me).wait_send()
                for t in range(n_w + n_r):
                    partial(t, s, me).wait_send()
        for t in range(n_r):
            rows_out[t][...] = rows_sum[t][...]

    return send_partials, reduce_and_send_sums, finish


def _small_allreduce_scratch(whole, by_rows):
    n_sem = len(whole) + 2 * len(by_rows)
    return ([pltpu.VMEM((N_DEV,) + a.shape, F32) for a in whole]
            + [pltpu.VMEM((N_DEV, a.shape[0] // N_DEV, a.shape[1]), F32) for a in by_rows]
            + [pltpu.VMEM(a.shape, F32) for a in by_rows]
            + [pltpu.SemaphoreType.DMA((n_sem, N_DEV)), pltpu.SemaphoreType.DMA((n_sem, N_DEV))])


def _final_exchange(name, whole, by_rows, scattered):
    n_w, n_r, n = len(whole), len(by_rows), len(scattered)

    def body(*refs):
        refs = list(refs)
        whole_in, rows_in, ins = refs[:n_w], refs[n_w:n_w + n_r], refs[n_w + n_r:n_w + n_r + n]
        del refs[:n_w + n_r + n]
        whole_out, rows_out, outs = refs[:n_w], refs[n_w:n_w + n_r], refs[n_w + n_r:n_w + n_r + n]
        del refs[:n_w + n_r + n]
        send_sems, recv_sems, local_sems = refs[:3]
        whole_land, rows_land = refs[3:3 + n_w], refs[3 + n_w:3 + n_w + n_r]
        rows_sum = refs[3 + n_w + n_r:3 + n_w + 2 * n_r]
        small_send, small_recv = refs[3 + n_w + 2 * n_r:]
        start, wait = _all_to_all([(lambda s, r=r: r.at[s]) for r in ins], outs, send_sems, recv_sems, local_sems)
        start()
        send_partials, reduce_and_send_sums, finish_small = _small_allreduce(
            whole_in, rows_in, whole_out, rows_out, whole_land, rows_land, rows_sum, small_send, small_recv)
        send_partials()
        reduce_and_send_sums()
        finish_small()
        wait()

    any_spec = pl.BlockSpec(memory_space=pl.ANY)
    vmem_spec = pl.BlockSpec(memory_space=pltpu.VMEM)
    small = list(whole) + list(by_rows)
    res = pl.pallas_call(
        body, name=name, in_specs=[vmem_spec] * len(small) + [any_spec] * n,
        out_specs=[vmem_spec] * len(small) + [any_spec] * n,
        out_shape=[jax.ShapeDtypeStruct(a.shape, F32) for a in small]
        + [jax.ShapeDtypeStruct(a.shape, a.dtype) for a in scattered],
        scratch_shapes=_exchange_scratch(n) + _small_allreduce_scratch(whole, by_rows),
    )(*small, *scattered)
    return res[:n_w], res[n_w:n_w + n_r], res[n_w + n_r:]


def _adam_update(g, w, m, v):
    m_new = ADAM_B1 * m + (1.0 - ADAM_B1) * g
    v_new = ADAM_B2 * v + (1.0 - ADAM_B2) * jnp.square(g)
    m_hat = m_new / (1.0 - ADAM_B1 ** ADAM_STEP)
    v_hat = v_new / (1.0 - ADAM_B2 ** ADAM_STEP)
    return -ADAM_LR * (m_hat / (jnp.sqrt(v_hat) + ADAM_EPS) + ADAM_WD * w), m_new, v_new


def _adamw_groups(name, groups):
    n = len(groups)

    def body(*refs):
        for k in range(n):
            g_ref, w_ref, m_ref, v_ref = refs[4 * k:4 * k + 4]
            g_out, d_out, m_out, v_out = refs[4 * n + 4 * k:4 * n + 4 * k + 4]
            g = g_ref[...]
            g_out[...] = g
            d_out[...], m_out[...], v_out[...] = _adam_update(g, w_ref[...], m_ref[...], v_ref[...])

    flat = [a for grp in groups for a in grp]
    out = pl.pallas_call(body, name=name,
                         out_shape=[jax.ShapeDtypeStruct(grp[0].shape, F32) for grp in groups for _ in range(4)])(*flat)
    return [out[4 * k:4 * k + 4] for k in range(n)]


def _adamw(name, parts, w, m, v, row_block):
    n_src, rows, cols = parts.shape
    rb = min(row_block, rows)

    def body(p_ref, w_ref, m_ref, v_ref, g_out, d_out, m_out, v_out):
        g = p_ref[0].astype(F32)
        for s in range(1, n_src):
            g = g + p_ref[s].astype(F32)
        g_out[...] = g
        d_out[...], m_out[...], v_out[...] = _adam_update(g, w_ref[...], m_ref[...], v_ref[...])

    spec = pl.BlockSpec((rb, cols), lambda i: (i, 0))
    return pl.pallas_call(
        body, name=name, grid=(rows // rb,),
        in_specs=[pl.BlockSpec((n_src, rb, cols), lambda i: (0, i, 0)), spec, spec, spec],
        out_specs=[spec] * 4, out_shape=[jax.ShapeDtypeStruct((rows, cols), F32)] * 4,
        compiler_params=pltpu.CompilerParams(dimension_semantics=("parallel",), vmem_limit_bytes=VMEM_LIMIT),
    )(parts, w, m, v)


def _block_diag(blocks):
    g, a, b = blocks.shape
    eye = jnp.eye(g, dtype=blocks.dtype)
    return (eye[:, None, :, None] * blocks[:, :, None, :]).reshape(g * a, g * b)


def kernel(x, p, norm_mix_g, w_in, pool_w, pool_b, pool_scale, conv_w, conv_b, gate_a_w, gate_a_b, gate_x_w, gate_x_b, lru_L, w_out, norm_mlp_g, w_up, w_down, norm_ple_g, w_ple_gate, b_ple_gate, w_ple_proj, norm_final_g, loss_target, m_norm_mix_g, m_w_in, m_pool_w, m_pool_b, m_pool_scale, m_conv_w, m_conv_b, m_gate_a_w, m_gate_a_b, m_gate_x_w, m_gate_x_b, m_lru_L, m_w_out, m_norm_mlp_g, m_w_up, m_w_down, m_norm_ple_g, m_w_ple_gate, m_b_ple_gate, m_w_ple_proj, m_norm_final_g, v_norm_mix_g, v_w_in, v_pool_w, v_pool_b, v_pool_scale, v_conv_w, v_conv_b, v_gate_a_w, v_gate_a_b, v_gate_x_w, v_gate_x_b, v_lru_L, v_w_out, v_norm_mlp_g, v_w_up, v_w_down, v_norm_ple_g, v_w_ple_gate, v_b_ple_gate, v_w_ple_proj, v_norm_final_g):
    t_len, d = x.shape[1], x.shape[2]
    tbs = {k: min(v, t_len) for k, v in TIME_BLOCKS.items()}
    me = _my_index()

    win_g, wout_g, convw_g = _gather("gather_mixer_weights", [w_in[0].astype(BF16), w_out[0].astype(BF16), conv_w[0]])
    w_in_f = jnp.transpose(win_g, (1, 0, 2)).reshape(d, -1)
    conv_w_f = jnp.transpose(convw_g, (1, 0, 2)).reshape(convw_g.shape[1], -1)
    wp_bd = _block_diag(pool_w[0]).astype(BF16)
    wg_bd = jnp.concatenate([_block_diag(gate_a_w[0]), _block_diag(gate_x_w[0])], axis=1).astype(BF16)
    gate_b2 = jnp.concatenate([gate_a_b.reshape(1, -1), gate_x_b.reshape(1, -1)], axis=1)
    mixer_small = (norm_mix_g, w_in_f, wp_bd, pool_b.reshape(1, -1), pool_scale, conv_w_f, conv_b, wg_bd, gate_b2, lru_L,
                   wout_g.reshape(-1, d))

    x2 = x[0]
    later = [w_up[0].astype(BF16), w_down[0].astype(BF16), w_ple_gate[0].astype(BF16), w_ple_proj[0].astype(BF16)]
    h1, z1, proj, hs, cat, lru_saved, wup_g, wdn_g, wgate_g, wproj_g = _mix_fwd(
        x2, *mixer_small, later, [_core_major_slot, _core_major_slot, None, None], GATHER_FORWARD_AT, tbs['mix_fwd'])
    w_down_f = wdn_g.reshape(-1, d)
    w_proj_f = jnp.transpose(wproj_g, (1, 0, 2)).reshape(wproj_g.shape[1], -1)
    h2, z2, up = _mlp_fwd(h1, norm_mlp_g, wup_g, w_down_f, tbs['mlp_fwd'])
    dh2, ple_vec, dw_gate, dw_proj = _ple(h2, p[0, 0], loss_target[0], norm_ple_g, wgate_g.reshape(-1, d), b_ple_gate,
                                          w_proj_f, norm_final_g.reshape(1, -1), tbs['ple'])
    everyone = list(range(N_DEV))
    n_proj = w_ple_proj.shape[2]
    scatter = _scatter_plan([dw_gate.reshape(N_DEV, -1, d), jnp.transpose(dw_proj.reshape(-1, N_DEV, n_proj), (1, 0, 2))],
                            [everyone, everyone], [None, None])
    dz2_0, dw_up_0, dw_down_0, recv_gate, recv_proj = _mlp_bwd_part(
        0, MLP_BWD_SPLIT, dh2, z2, up, wup_g, w_down_f, None, h1, norm_mlp_g, scatter, tbs['mlp_bwd'])
    half = N_DEV // MLP_BWD_SPLIT
    south = [_device_of_core_major_slot(k) for k in range(half)]
    north = [_device_of_core_major_slot(k) for k in range(half, N_DEV)]
    scatter = _scatter_plan([dw_up_0, dw_down_0], [south, south], [None, None])
    dh1, mlp_vec, dw_up_1, dw_down_1, recv_up, recv_down = _mlp_bwd_part(
        1, MLP_BWD_SPLIT, dh2, z2, up, wup_g, w_down_f, dz2_0, h1, norm_mlp_g, scatter, tbs['mlp_bwd'])
    dw_out, _, _ = _wgrad("wgrad_out", cat, dh1, [], [], tbs['wgrad_out'])
    scatter = _scatter_plan([dw_up_1, dw_down_1, dw_out.reshape(N_DEV, -1, d)], [north, north, everyone],
                            [recv_up, recv_down, None])
    dproj, v512, dpw, dga, dgx, recv_up, recv_down, recv_out = _mix_bwd(
        dh1, proj, hs, lru_saved, wp_bd, pool_b.reshape(1, -1), pool_scale, conv_w_f, wg_bd, lru_L, wout_g.reshape(-1, d),
        scatter, tbs['mix_bwd'])
    rows1024 = jnp.concatenate([jnp.zeros((1, d), F32), mlp_vec[0:1], ple_vec[1:2], ple_vec[0:1], ple_vec[2:4],
                                jnp.zeros((2, d), F32)], axis=0)
    dw_in, (rows1024, rows512), (g_pool_w, g_gate_a_w, g_gate_x_w) = _wgrad(
        "wgrad_in", z1, dproj, [rows1024, v512], [dpw, dga, dgx], tbs['wgrad_in'])
    n_in = w_in.shape[2]
    scatter = _scatter_plan([jnp.transpose(dw_in.reshape(d, N_DEV, n_in), (1, 0, 2))], [everyone], [None])
    dx, in_vec, recv_in = _in_bwd(dproj, x2, dh1, norm_mix_g, w_in_f, scatter, tbs['in_bwd'])
    rows1024 = jnp.concatenate([in_vec[0:1], rows1024[1:]], axis=0)
    received = [recv_in, recv_out, recv_up, recv_down, recv_gate, recv_proj]

    shard_w = [w_in[0], w_out[0], w_up[0], w_down[0], w_ple_gate[0], w_ple_proj[0]]
    shard_m = [m_w_in[0], m_w_out[0], m_w_up[0], m_w_down[0], m_w_ple_gate[0], m_w_ple_proj[0]]
    shard_v = [v_w_in[0], v_w_out[0], v_w_up[0], v_w_down[0], v_w_ple_gate[0], v_w_ple_proj[0]]
    names = ["w_in", "w_out", "w_up", "w_down", "w_ple_gate", "w_ple_proj"]
    res = {}
    for nm, parts, w_s, m_s, v_s in zip(names, received, shard_w, shard_m, shard_v):
        res[nm] = [r[None] for r in _adamw("adamw_" + nm, parts, w_s, m_s, v_s, ADAM_ROW_BLOCK)]

    def rows_of_1024(a, b, c, e, f):
        return jnp.concatenate([a, b, c, e, f.reshape(1, -1), jnp.zeros((3, d), F32)], axis=0)

    def rows_of_512(scale, bias, cb, lru, ga, gx):
        z = jnp.zeros((1, 512), F32)
        return jnp.concatenate([scale, bias.reshape(1, -1), cb, lru, z, z, z, z, ga.reshape(1, -1), gx.reshape(1, -1),
                                z, z, z, z, z, z], axis=0)

    n_conv = conv_w.shape[2]
    groups = [
        (rows1024, *[rows_of_1024(*t) for t in (
            (norm_mix_g, norm_mlp_g, norm_ple_g, b_ple_gate, norm_final_g),
            (m_norm_mix_g, m_norm_mlp_g, m_norm_ple_g, m_b_ple_gate, m_norm_final_g),
            (v_norm_mix_g, v_norm_mlp_g, v_norm_ple_g, v_b_ple_gate, v_norm_final_g))]),
        (rows512, *[rows_of_512(*t) for t in (
            (pool_scale, pool_b, conv_b, lru_L, gate_a_b, gate_x_b),
            (m_pool_scale, m_pool_b, m_conv_b, m_lru_L, m_gate_a_b, m_gate_x_b),
            (v_pool_scale, v_pool_b, v_conv_b, v_lru_L, v_gate_a_b, v_gate_x_b))]),
        (g_pool_w, *[a.reshape(-1, a.shape[-1]) for a in (pool_w, m_pool_w, v_pool_w)]),
        (g_gate_a_w, *[a.reshape(-1, a.shape[-1]) for a in (gate_a_w, m_gate_a_w, v_gate_a_w)]),
        (g_gate_x_w, *[a.reshape(-1, a.shape[-1]) for a in (gate_x_w, m_gate_x_w, v_gate_x_w)]),
        (lax.dynamic_slice_in_dim(rows512[4:8], me * n_conv, n_conv, axis=1), conv_w[0], m_conv_w[0], v_conv_w[0]),
    ]
    r1024, r512, r_pool, r_ga, r_gx, r_conv = _adamw_groups("adamw_small", groups)
    loss = rows1024[5, 0]
    for k, nm in enumerate(["norm_mix_g", "norm_mlp_g", "norm_ple_g", "b_ple_gate"]):
        res[nm] = [a[k:k + 1] for a in r1024]
    res["norm_final_g"] = [a[4] for a in r1024]
    res["pool_scale"] = [a[0:1] for a in r512]
    res["pool_b"] = [a[1:2].reshape(pool_b.shape) for a in r512]
    res["conv_b"] = [a[2:3] for a in r512]
    res["lru_L"] = [a[3:4] for a in r512]
    res["gate_a_b"] = [a[8:9].reshape(gate_a_b.shape) for a in r512]
    res["gate_x_b"] = [a[9:10].reshape(gate_x_b.shape) for a in r512]
    res["pool_w"] = [a.reshape(pool_w.shape) for a in r_pool]
    res["gate_a_w"] = [a.reshape(gate_a_w.shape) for a in r_ga]
    res["gate_x_w"] = [a.reshape(gate_x_w.shape) for a in r_gx]
    res["conv_w"] = [a[None] for a in r_conv]
    order = ["norm_mix_g", "w_in", "pool_w", "pool_b", "pool_scale", "conv_w", "conv_b", "gate_a_w", "gate_a_b",
             "gate_x_w", "gate_x_b", "lru_L", "w_out", "norm_mlp_g", "w_up", "w_down", "norm_ple_g", "w_ple_gate",
             "b_ple_gate", "w_ple_proj", "norm_final_g"]
    return (loss, dx[None], *[res[nm][kind] for kind in range(4) for nm in order])
```

```python
import jax
import jax.numpy as jnp
from jax import lax
from jax.experimental import pallas as pl
from jax.experimental.pallas import tpu as pltpu

F32 = jnp.float32
BF16 = jnp.bfloat16
MESH = pl.DeviceIdType.MESH

N_DEV = 8
RMS_EPS = 1e-6
LRU_C = 8.0
POOL_WINDOWS = (2, 4, 8, 16)
N_POOL_GROUPS = 4
LRU_HEADS = 8
HALO = 16
SUB = 8
GELU_C0 = 0.7978845608028654
GELU_C1 = 0.044715

ADAM_LR = 0.001
ADAM_B1 = 0.9
ADAM_B2 = 0.999
ADAM_EPS = 1e-08
ADAM_WD = 0.01
ADAM_STEP = 10

VMEM_LIMIT = 60 * 1024 * 1024
TIME_BLOCKS = dict(mix_fwd=512, mlp_fwd=512, ple=512, mlp_bwd=512, wgrad_out=1024, mix_bwd=512, wgrad_in=1024, in_bwd=512)
ADAM_ROW_BLOCK = 256
MLP_BWD_SPLIT = 2
GATHER_FORWARD_AT = (0.5, 0.875, 1.0, 1.0)


def _params(n_arbitrary=1):
    return pltpu.CompilerParams(dimension_semantics=("arbitrary",) * n_arbitrary, vmem_limit_bytes=VMEM_LIMIT)


def _dot(a, b):
    return jnp.dot(a, b, preferred_element_type=F32)


def _dot_nt(a, b):
    return lax.dot_general(a, b, (((1,), (1,)), ((), ())), preferred_element_type=F32)


def _dot_tn(a, b):
    return lax.dot_general(a, b, (((0,), (0,)), ((), ())), preferred_element_type=F32)


def _rms_fwd(x, g):
    r = lax.rsqrt(jnp.mean(x * x, axis=-1, keepdims=True) + RMS_EPS)
    xh = x * r
    return xh * g, xh, r


def _rms_bwd(xh, r, g, dz):
    dxh = dz * g
    return r * (dxh - xh * jnp.mean(dxh * xh, axis=-1, keepdims=True))


def _colsum(a):
    return jnp.sum(a, axis=0, keepdims=True)


def _sigmoid(a):
    return 0.5 * jnp.tanh(0.5 * a) + 0.5


def _gelu_parts(u):
    u2 = u * u
    th = jnp.tanh(GELU_C0 * (u + GELU_C1 * u * u2))
    gel = 0.5 * u * (1.0 + th)
    dgel = 0.5 * (1.0 + th) + 0.5 * u * (1.0 - th * th) * (GELU_C0 * (1.0 + 3.0 * GELU_C1 * u2))
    return gel, dgel


def _my_index():
    return 4 * lax.axis_index("x") + 2 * lax.axis_index("y") + lax.axis_index("c")


def _all_to_all(srcs_of, dsts, send_sems, recv_sems, local_sems, dests=None):
    n = len(dsts)
    me = _my_index()
    dests = [list(range(N_DEV))] * n if dests is None else dests

    def remote(t, s):
        return pltpu.make_async_remote_copy(
            src_ref=srcs_of[t](s), dst_ref=dsts[t].at[me], send_sem=send_sems.at[t, s], recv_sem=recv_sems.at[t, me],
            device_id=(s // 4, (s // 2) % 2, s % 2), device_id_type=MESH)

    def arrival(t, s):
        return pltpu.make_async_remote_copy(
            src_ref=srcs_of[t](dests[t][0]), dst_ref=dsts[t].at[s], send_sem=send_sems.at[t, s],
            recv_sem=recv_sems.at[t, s], device_id=(s // 4, (s // 2) % 2, s % 2), device_id_type=MESH)

    def local(t, s):
        return pltpu.make_async_copy(srcs_of[t](s), dsts[t].at[s], local_sems.at[t])

    def start():
        for s in range(N_DEV):
            to_s = [t for t in range(n) if s in dests[t]]

            @pl.when(s == me)
            def _():
                for t in to_s:
                    local(t, s).start()

            @pl.when(s != me)
            def _():
                for t in to_s:
                    remote(t, s).start()

    def wait():
        for s in range(N_DEV):
            to_s = [t for t in range(n) if s in dests[t]]

            @pl.when(s == me)
            def _():
                for t in to_s:
                    local(t, s).wait()
                    for src in range(N_DEV):
                        if src != s:
                            arrival(t, src).wait_recv()

            @pl.when(s != me)
            def _():
                for t in to_s:
                    remote(t, s).wait_send()

    return start, wait


N_GATHER_COPIES = 7


def _core_major_slot(dev):
    return 4 * dev[2] + 2 * dev[0] + dev[1]


def _device_of_core_major_slot(k):
    return (k % 4) * 2 + k // 4


def _two_level_gather(srcs, dsts, send_sems, recv_sems, local_sems, slots=None):
    n = len(dsts)
    x, y, c = lax.axis_index("x"), lax.axis_index("y"), lax.axis_index("c")
    me, sibling = (x, y, c), (x, y, 1 - c)
    chips = [(1 - x, y), (x, 1 - y), (1 - x, 1 - y)]

    def slot(t, dev):
        return 4 * dev[0] + 2 * dev[1] + dev[2] if slots is None or slots[t] is None else slots[t](dev)

    def copy(t, k, block, to, src=None):
        return pltpu.make_async_remote_copy(
            src_ref=dsts[t].at[slot(t, block)] if src is None else src, dst_ref=dsts[t].at[slot(t, block)],
            send_sem=send_sems.at[t, k], recv_sem=recv_sems.at[t, k], device_id=to, device_id_type=MESH)

    def local(t):
        return pltpu.make_async_copy(srcs[t], dsts[t].at[slot(t, me)], local_sems.at[t])

    def start():
        for t in range(n):
            local(t).start()
            for j, chip in enumerate(chips):
                copy(t, 1 + j, me, (*chip, c), src=srcs[t]).start()
            copy(t, 0, me, sibling, src=srcs[t]).start()

    def forward(t):
        for j, chip in enumerate(chips):
            copy(t, 1 + j, (*chip, c), me).wait_recv()
            copy(t, 4 + j, (*chip, c), sibling).start()

    def finish():
        for t in range(n):
            copy(t, 0, sibling, me).wait_recv()
            for j, chip in enumerate(chips):
                copy(t, 4 + j, (*chip, 1 - c), me).wait_recv()
            copy(t, 0, me, sibling, src=srcs[t]).wait_send()
            for j, chip in enumerate(chips):
                copy(t, 1 + j, me, (*chip, c), src=srcs[t]).wait_send()
                copy(t, 4 + j, (*chip, c), sibling).wait_send()
            local(t).wait()

    return start, forward, finish


def _hosted_gather(i, nb, forward_at, srcs, dsts, sems, slots=None):
    start, forward, finish = _two_level_gather(srcs, dsts, *sems, slots)

    def after_step():
        for t, f in enumerate(forward_at):
            @pl.when(i == min(nb - 1, int(f * nb)))
            def _():
                forward(t)

        @pl.when(i == nb - 1)
        def _():
            finish()

    return start, after_step


def _gather_scratch(n):
    return [pltpu.SemaphoreType.DMA((n, N_GATHER_COPIES)), pltpu.SemaphoreType.DMA((n, N_GATHER_COPIES)),
            pltpu.SemaphoreType.DMA((n,))]


def _gather(name, srcs):
    n = len(srcs)

    def body(*refs):
        start, forward, finish = _two_level_gather(refs[:n], refs[n:2 * n], *refs[2 * n:])
        start()
        for t in range(n):
            forward(t)
        finish()

    any_spec = pl.BlockSpec(memory_space=pl.ANY)
    return pl.pallas_call(
        body, name=name, in_specs=[any_spec] * n, out_specs=[any_spec] * n,
        out_shape=[jax.ShapeDtypeStruct((N_DEV,) + a.shape, a.dtype) for a in srcs], scratch_shapes=_gather_scratch(n),
    )(*srcs)


def _scatter_plan(blocks, dests, landing):
    return dict(blocks=list(blocks), dests=[list(dd) for dd in dests], landing=list(landing))


def _scatter_args(plan):
    return plan['blocks'] + [a for a in plan['landing'] if a is not None]


def _scatter_out_shape(plan):
    return [jax.ShapeDtypeStruct((N_DEV,) + b.shape[1:], b.dtype) for b in plan['blocks']]


def _scatter_aliases(plan, first_in, first_out):
    given = [t for t, a in enumerate(plan['landing']) if a is not None]
    return {first_in + len(plan['blocks']) + k: first_out + t for k, t in enumerate(given)}


def _scatter_ops(plan, in_refs, out_refs, sems):
    n = len(plan['blocks'])
    srcs_of = [(lambda s, r=in_refs[t], dd=plan['dests'][t]: r.at[dd.index(s)]) for t in range(n)]
    return _all_to_all(srcs_of, out_refs, *sems, dests=plan['dests'])


def _exchange_scratch(n):
    return [pltpu.SemaphoreType.DMA((n, N_DEV)), pltpu.SemaphoreType.DMA((n, N_DEV)), pltpu.SemaphoreType.DMA((n,))]


def _const_spec(shape):
    nd = len(shape)
    return pl.BlockSpec(shape, lambda i: (0,) * nd, pipeline_mode=pl.Buffered(1))


def _pool_windows(up_ext, n, forward):
    sh = (lambda k: k) if forward else (lambda k: n - k)
    s2 = up_ext + pltpu.roll(up_ext, sh(1), 0)
    t4 = s2[:, 128:]
    s4 = t4 + pltpu.roll(t4, sh(2), 0)
    t8 = s4[:, 128:]
    s8 = t8 + pltpu.roll(t8, sh(4), 0)
    t16 = s8[:, 128:]
    s16 = t16 + pltpu.roll(t16, sh(8), 0)
    return jnp.concatenate([s2[:, :128], s4[:, :128], s8[:, :128], s16], axis=1)


def _inv_count_head():
    t = jnp.arange(1, HALO + 1, dtype=F32)[:, None]
    return jnp.concatenate([jnp.broadcast_to(1.0 / jnp.minimum(t, float(w)), (HALO, 128)) for w in POOL_WINDOWS], axis=1)


def _scale_by_inv_count(v, is_first_block, inv_head):
    inv_row = jnp.concatenate([jnp.full((1, 128), 1.0 / w, F32) for w in POOL_WINDOWS], axis=1)
    head = v[0:HALO] * jnp.where(is_first_block, inv_head, inv_row)
    return jnp.concatenate([head, v[HALO:] * inv_row], axis=0)


def _lru_decay(r, a, c_l, first_row):
    a2 = a * a
    m2 = -jnp.tanh(c_l * r) * (a2 + 1.0)
    return a2, m2, jnp.where(first_row, 1.0, jnp.sqrt(m2))


def _log_sigmoid(v):
    return -(jnp.maximum(-v, 0.0) + jnp.log1p(jnp.exp(-jnp.abs(v))))


def _conv_fwd(ul_ext, cw, cb):
    return (cb + cw[3:4, :] * ul_ext + cw[2:3, :] * pltpu.roll(ul_ext, 1, 0)
            + cw[1:2, :] * pltpu.roll(ul_ext, 2, 0) + cw[0:1, :] * pltpu.roll(ul_ext, 3, 0))


def _mix_fwd(x, g_mix, w_in, wp_bd, pool_b, pool_scale, conv_w, conv_b, wg_bd, gate_b, lru_l, w_out, gather_srcs,
             gather_slots, forward_at, tb):
    t_len, d = x.shape
    nb = t_len // tb
    n_g = len(gather_srcs)

    def body(*refs):
        (x_ref, g_ref, win_ref, wp_ref, pb_ref, ps_ref, cw_ref, cb_ref, wg_ref, gb_ref, l_ref, wout_ref,
         invh_ref) = refs[:13]
        gsrc = refs[13:13 + n_g]
        h1_ref, z1_ref, proj_ref, hs_ref, cat_ref, lru_ref = refs[13 + n_g:19 + n_g]
        gdst = refs[19 + n_g:19 + 2 * n_g]
        ext_ref, a_ref, b_ref, hc_ref, send_sems, recv_sems, local_sems = refs[19 + 2 * n_g:]
        i = pl.program_id(0)
        start_gather, after_step = _hosted_gather(i, nb, forward_at, gsrc, gdst, (send_sems, recv_sems, local_sems),
                                                  gather_slots)

        @pl.when(i == 0)
        def _():
            start_gather()
            ext_ref[0:HALO, :] = jnp.zeros((HALO, 1024), F32)
            hc_ref[...] = jnp.zeros_like(hc_ref)

        xv = x_ref[...]
        z, _, _ = _rms_fwd(xv, g_ref[...])
        zb = z.astype(BF16)
        z1_ref[...] = zb
        proj = _dot(zb, win_ref[...])
        proj_ref[...] = proj
        ext_ref[HALO:, :] = proj[:, 0:1024]
        ug = proj[:, 1024:1536]
        n = tb + HALO
        up_ext = ext_ref[:, 0:512]
        win = _pool_windows(up_ext, n, True)[HALO:]
        dpool = _scale_by_inv_count(win, i == 0, invh_ref[...]) - proj[:, 0:512]
        q = _dot(dpool.astype(BF16), wp_ref[...]) + pb_ref[...]
        y_pool = q * ps_ref[...]
        xb = _conv_fwd(ext_ref[:, 512:1024], cw_ref[...], cb_ref[...])[HALO:]
        first_row = (i * tb + lax.broadcasted_iota(jnp.int32, (tb, 1), 0)) == 0
        c_l = LRU_C * _log_sigmoid(l_ref[...])
        gp = _dot(xb.astype(BF16), wg_ref[...]) + gb_ref[...]
        r = _sigmoid(gp[:, :512])
        ig = _sigmoid(gp[:, 512:])
        a = jnp.exp(c_l * r)
        _, _, mult = _lru_decay(r, a, c_l, first_row)
        lru_ref[:, 0:512] = xb
        lru_ref[:, 512:1024] = r
        lru_ref[:, 1024:1536] = ig
        lru_ref[:, 1536:2048] = a
        a_ref[...] = a
        b_ref[...] = mult * (ig * xb)
        row = lax.broadcasted_iota(jnp.int32, (SUB, 512), 0)

        def group(j, hprev):
            o = pl.multiple_of(j * SUB, SUB)
            a8 = a_ref[pl.ds(o, SUB), :]
            b8 = b_ref[pl.ds(o, SUB), :]
            for sh in (1, 2, 4):
                ash = jnp.where(row >= sh, pltpu.roll(a8, sh, 0), 1.0)
                bsh = jnp.where(row >= sh, pltpu.roll(b8, sh, 0), 0.0)
                b8 = a8 * bsh + b8
                a8 = a8 * ash
            h8 = a8 * hprev + b8
            hs_ref[pl.ds(o, SUB), :] = h8
            return jnp.broadcast_to(h8[SUB - 1:SUB, :], (SUB, 512))

        hc_ref[...] = lax.fori_loop(0, tb // SUB, group, hc_ref[...])
        gel, _ = _gelu_parts(ug)
        y_lru = hs_ref[...] * gel
        catb = jnp.concatenate([y_pool, y_lru], axis=1).astype(BF16)
        cat_ref[...] = catb
        h1_ref[...] = xv + _dot(catb, wout_ref[...])
        ext_ref[0:HALO, :] = ext_ref[tb:tb + HALO, :]

        after_step()

    row_spec = lambda w: pl.BlockSpec((tb, w), lambda i: (i, 0))
    any_spec = pl.BlockSpec(memory_space=pl.ANY)
    smalls = [g_mix, w_in, wp_bd, pool_b, pool_scale, conv_w, conv_b, wg_bd, gate_b, lru_l, w_out, _inv_count_head()]
    return pl.pallas_call(
        body, name="mix_fwd", grid=(nb,),
        in_specs=[row_spec(d)] + [_const_spec(s.shape) for s in smalls] + [any_spec] * n_g,
        out_specs=[row_spec(d), row_spec(d), row_spec(1536), row_spec(512), row_spec(1024), row_spec(2048)]
        + [any_spec] * n_g,
        out_shape=[jax.ShapeDtypeStruct((t_len, d), F32), jax.ShapeDtypeStruct((t_len, d), BF16),
                   jax.ShapeDtypeStruct((t_len, 1536), F32), jax.ShapeDtypeStruct((t_len, 512), F32),
                   jax.ShapeDtypeStruct((t_len, 1024), BF16), jax.ShapeDtypeStruct((t_len, 2048), F32)]
        + [jax.ShapeDtypeStruct((N_DEV,) + s.shape, s.dtype) for s in gather_srcs],
        scratch_shapes=[pltpu.VMEM((tb + HALO, 1024), F32), pltpu.VMEM((tb, 512), F32), pltpu.VMEM((tb, 512), F32),
                        pltpu.VMEM((SUB, 512), F32)] + _gather_scratch(n_g),
        compiler_params=_params(),
    )(x, *smalls, *gather_srcs)


def _mlp_fwd(h1, g_mlp, w_up, w_down, tb):
    t_len, d = h1.shape
    nb = t_len // tb
    n_chunk, _, fc = w_up.shape

    def body(h1_ref, g_ref, wup_ref, wdn_ref, h2_ref, z2_ref, up_ref):
        xv = h1_ref[...]
        z, _, _ = _rms_fwd(xv, g_ref[...])
        zb = z.astype(BF16)
        z2_ref[...] = zb
        acc = xv
        for c in range(n_chunk):
            u = _dot(zb, wup_ref[c])
            up_ref[:, c * fc:(c + 1) * fc] = u.astype(BF16)
            act = jnp.square(jnp.maximum(u, 0.0)).astype(BF16)
            acc = acc + _dot(act, wdn_ref[c * fc:(c + 1) * fc, :])
        h2_ref[...] = acc

    row_spec = lambda w: pl.BlockSpec((tb, w), lambda i: (i, 0))
    return pl.pallas_call(
        body, name="mlp_fwd", grid=(nb,),
        in_specs=[row_spec(d), _const_spec(g_mlp.shape), _const_spec(w_up.shape), _const_spec(w_down.shape)],
        out_specs=[row_spec(d), row_spec(d), row_spec(n_chunk * fc)],
        out_shape=[jax.ShapeDtypeStruct((t_len, d), F32), jax.ShapeDtypeStruct((t_len, d), BF16),
                   jax.ShapeDtypeStruct((t_len, n_chunk * fc), BF16)],
        compiler_params=_params(),
    )(h1, g_mlp, w_up, w_down)


def _ple(h2, p, target, g_ple, w_gate, b_gate, w_proj, g_final, tb):
    t_len, d = h2.shape
    nb = t_len // tb
    pd = p.shape[1]

    def body(h2_ref, p_ref, tgt_ref, g_ref, wg_ref, bg_ref, wp_ref, gf_ref,
             dh2_ref, vec_ref, dwg_out, dwp_out, dwg_acc, dwp_acc, dwg_stage, dwp_stage):
        i = pl.program_id(0)

        @pl.when(i == 0)
        def _():
            vec_ref[...] = jnp.zeros_like(vec_ref)
            dwg_acc[...] = jnp.zeros_like(dwg_acc)
            dwp_acc[...] = jnp.zeros_like(dwp_acc)

        h2 = h2_ref[...]
        g2 = g_ref[...]
        z3, xh2, r2 = _rms_fwd(h2, g2)
        z3b = z3.astype(BF16)
        gate = _sigmoid(_dot(z3b, wg_ref[...]) + bg_ref[...])
        pb = p_ref[...].astype(BF16)
        pp = _dot(pb, wp_ref[...])
        h3 = h2 + gate * pp
        gf = gf_ref[...]
        y, xh3, r3 = _rms_fwd(h3, gf)
        err = y - tgt_ref[...]
        loss_rows = jnp.mean(err * err, axis=-1, keepdims=True)
        dy = err * (1.0 / d)
        dh3 = _rms_bwd(xh3, r3, gf, dy)
        dgl = (dh3 * pp) * (gate * (1.0 - gate))
        dpp = dh3 * gate
        dglb = dgl.astype(BF16)
        dwg_acc[...] += _dot_tn(z3b, dglb)
        dwp_acc[...] += _dot_tn(pb, dpp.astype(BF16))
        dz3 = _dot_nt(dglb, wg_ref[...])
        dh2_ref[...] = dh3 + _rms_bwd(xh2, r2, g2, dz3)
        vec_ref[0:1, :] += _colsum(dgl)
        vec_ref[1:2, :] += _colsum(dz3 * xh2)
        vec_ref[2:3, :] += _colsum(dy * xh3)
        vec_ref[3:4, :] += 0.5 * jnp.sum(loss_rows)

        @pl.when(i == nb - 1)
        def _():
            dwg_stage[...] = dwg_acc[...].astype(BF16)
            dwp_stage[...] = dwp_acc[...].astype(BF16)
            pltpu.sync_copy(dwg_stage, dwg_out)
            pltpu.sync_copy(dwp_stage, dwp_out)

    row_spec = lambda w: pl.BlockSpec((tb, w), lambda i: (i, 0))
    any_spec = pl.BlockSpec(memory_space=pl.ANY)
    smalls = [g_ple, w_gate, b_gate, w_proj, g_final]
    return pl.pallas_call(
        body, name="ple_fwd_bwd", grid=(nb,),
        in_specs=[row_spec(d), row_spec(pd), row_spec(d)] + [_const_spec(s.shape) for s in smalls],
        out_specs=[row_spec(d), pl.BlockSpec((8, d), lambda i: (0, 0)), any_spec, any_spec],
        out_shape=[jax.ShapeDtypeStruct((t_len, d), F32), jax.ShapeDtypeStruct((8, d), F32),
                   jax.ShapeDtypeStruct(w_gate.shape, BF16), jax.ShapeDtypeStruct(w_proj.shape, BF16)],
        scratch_shapes=[pltpu.VMEM(w_gate.shape, F32), pltpu.VMEM(w_proj.shape, F32), pltpu.VMEM(w_gate.shape, BF16),
                        pltpu.VMEM(w_proj.shape, BF16)],
        compiler_params=_params(),
    )(h2, p, target, *smalls)


def _mlp_bwd_part(part, n_part, dh2, z2, up, w_up, w_down, dz2_prev, h1, g_mlp, scatter, tb):
    t_len, d = dh2.shape
    nb = t_len // tb
    n_chunk_all, _, fc = w_up.shape
    n_chunk = n_chunk_all // n_part
    first, last = part == 0, part == n_part - 1

    def body(*refs):
        refs = list(refs)
        dh2_ref, z2_ref, up_ref, wup_ref, wdn_ref = refs[:5]
        del refs[:5]
        dzp_ref = None if first else refs.pop(0)
        h1_ref, g_ref = (refs.pop(0), refs.pop(0)) if last else (None, None)
        scatter_in = [refs.pop(0) for _ in _scatter_args(scatter)]
        out_ref = refs.pop(0)
        vec_ref = refs.pop(0) if last else None
        dwup_out, dwdn_out = refs.pop(0), refs.pop(0)
        scatter_out = [refs.pop(0) for _ in scatter['blocks']]
        dwup_acc, dwdn_acc, up_stage, dn_stage = refs[:4]
        start_scatter, wait_scatter = _scatter_ops(scatter, scatter_in, scatter_out, refs[4:])
        i = pl.program_id(0)

        @pl.when(i == 0)
        def _():
            start_scatter()
            dwup_acc[...] = jnp.zeros_like(dwup_acc)
            dwdn_acc[...] = jnp.zeros_like(dwdn_acc)
            if last:
                vec_ref[...] = jnp.zeros_like(vec_ref)

        dh2 = dh2_ref[...]
        dh2b = dh2.astype(BF16)
        z2b = z2_ref[...]
        dz2 = jnp.zeros((tb, d), F32) if first else dzp_ref[...]
        for c in range(n_chunk):
            u = up_ref[:, c * fc:(c + 1) * fc].astype(F32)
            ur = jnp.maximum(u, 0.0)
            dact = _dot_nt(dh2b, wdn_ref[c * fc:(c + 1) * fc, :])
            dupb = (dact * (2.0 * ur)).astype(BF16)
            dwdn_acc[c * fc:(c + 1) * fc, :] += _dot_tn((ur * ur).astype(BF16), dh2b)
            dwup_acc[c] += _dot_tn(z2b, dupb)
            dz2 = dz2 + _dot_nt(dupb, wup_ref[c])
        if last:
            g = g_ref[...]
            _, xh, r = _rms_fwd(h1_ref[...], g)
            out_ref[...] = dh2 + _rms_bwd(xh, r, g, dz2)
            vec_ref[0:1, :] += _colsum(dz2 * xh)
        else:
            out_ref[...] = dz2

        @pl.when(i == nb - 1)
        def _():
            for c in range(n_chunk):
                up_stage[...] = dwup_acc[c].astype(BF16)
                dn_stage[...] = dwdn_acc[c * fc:(c + 1) * fc, :].astype(BF16)
                pltpu.sync_copy(up_stage, dwup_out.at[c])
                pltpu.sync_copy(dn_stage, dwdn_out.at[c])
            wait_scatter()

    row_spec = lambda w: pl.BlockSpec((tb, w), lambda i: (i, 0))
    any_spec = pl.BlockSpec(memory_space=pl.ANY)
    args = [dh2, z2, up, w_up, w_down]
    in_specs = [row_spec(d), row_spec(d), pl.BlockSpec((tb, n_chunk * fc), lambda i: (i, part)),
                pl.BlockSpec((n_chunk, d, fc), lambda i: (part, 0, 0), pipeline_mode=pl.Buffered(1)),
                pl.BlockSpec((n_chunk * fc, d), lambda i: (part, 0), pipeline_mode=pl.Buffered(1))]
    if not first:
        args.append(dz2_prev)
        in_specs.append(row_spec(d))
    if last:
        args += [h1, g_mlp]
        in_specs += [row_spec(d), _const_spec(g_mlp.shape)]
    n_in = len(args)
    args += _scatter_args(scatter)
    in_specs += [any_spec] * len(_scatter_args(scatter))
    out_specs = [row_spec(d)]
    out_shape = [jax.ShapeDtypeStruct((t_len, d), F32)]
    if last:
        out_specs.append(pl.BlockSpec((8, d), lambda i: (0, 0)))
        out_shape.append(jax.ShapeDtypeStruct((8, d), F32))
    out_specs += [any_spec, any_spec]
    out_shape += [jax.ShapeDtypeStruct((n_chunk, d, fc), BF16), jax.ShapeDtypeStruct((n_chunk, fc, d), BF16)]
    n_out = len(out_shape)
    out_specs += [any_spec] * len(scatter['blocks'])
    out_shape += _scatter_out_shape(scatter)
    return pl.pallas_call(
        body, name=f"mlp_bwd_{part}", grid=(nb,), in_specs=in_specs, out_specs=out_specs, out_shape=out_shape,
        scratch_shapes=[pltpu.VMEM((n_chunk, d, fc), F32), pltpu.VMEM((n_chunk * fc, d), F32),
                        pltpu.VMEM((d, fc), BF16), pltpu.VMEM((fc, d), BF16)] + _exchange_scratch(len(scatter['blocks'])),
        input_output_aliases=_scatter_aliases(scatter, n_in, n_out), compiler_params=_params(),
    )(*args)


def _mix_bwd(dh1, proj, hs, lru_saved, wp_bd, pool_b, pool_scale, conv_w, wg_bd, lru_l, w_out, scatter, tb):
    t_len, d = dh1.shape
    nb = t_len // tb
    n_s = len(scatter['blocks'])
    scatter_args = _scatter_args(scatter)

    def body(*refs):
        refs = list(refs)
        (dh1_ref, proj_ref, projh_ref, hs_ref, hsh_ref, lru_ref,
         wp_ref, pb_ref, ps_ref, cw_ref, wg_ref, l_ref, wout_ref, invh_ref) = refs[:14]
        del refs[:14]
        scatter_in = refs[:len(scatter_args)]
        del refs[:len(scatter_args)]
        dproj_ref, v512_ref, dpw_ref, dga_ref, dgx_ref = refs[:5]
        recv = refs[5:5 + n_s]
        (dwp_acc, dwg_acc, v1024_ref, ext_ref, b_ref, gs_ref, ehead_ref, dxbhead_ref, hc_ref,
         send_sems, recv_sems, local_sems) = refs[5 + n_s:]
        i = pl.program_id(0)
        tbk = nb - 1 - i

        start_scatter, wait_scatter = _scatter_ops(scatter, scatter_in, recv, (send_sems, recv_sems, local_sems))

        @pl.when(i == 0)
        def _():
            start_scatter()
            for ref in (v512_ref, v1024_ref, dwp_acc, dwg_acc, ehead_ref, dxbhead_ref, hc_ref):
                ref[...] = jnp.zeros_like(ref)

        dcat = _dot_nt(dh1_ref[...].astype(BF16), wout_ref[...])

        proj = proj_ref[...]
        has_prev = (tbk > 0).astype(F32)
        ext_ref[0:HALO, :] = projh_ref[:, 0:1024] * has_prev
        ext_ref[HALO:, :] = proj[:, 0:1024]
        ug = proj[:, 1024:1536]
        n = tb + HALO
        inv_head = invh_ref[...]

        up_ext = ext_ref[:, 0:512]
        win = _pool_windows(up_ext, n, True)[HALO:]
        dpool = _scale_by_inv_count(win, tbk == 0, inv_head) - proj[:, 0:512]
        dpoolb = dpool.astype(BF16)
        q = _dot(dpoolb, wp_ref[...]) + pb_ref[...]
        dyp = dcat[:, 0:512]
        dq = dyp * ps_ref[...]
        dqb = dq.astype(BF16)
        v512_ref[0:1, :] += _colsum(dyp * q)
        v512_ref[1:2, :] += _colsum(dq)
        dwp_acc[...] += _dot_tn(dpoolb, dqb)
        dd = _dot_nt(dqb, wp_ref[...])
        e = _scale_by_inv_count(dd, tbk == 0, inv_head)
        e_ext = jnp.concatenate([e, ehead_ref[...]], axis=0)
        du_pool = _pool_windows(e_ext, n, False)[0:tb] - dd
        ehead_ref[...] = e[0:HALO]

        gel, dgel = _gelu_parts(ug)
        hsv = hs_ref[...]
        dcl = dcat[:, 512:1024]
        dhs = dcl * gel
        dug = dcl * hsv * dgel
        ul_ext = ext_ref[:, 512:1024]
        cw = cw_ref[...]
        xb, r, ig, a = lru_ref[:, 0:512], lru_ref[:, 512:1024], lru_ref[:, 1024:1536], lru_ref[:, 1536:2048]
        first_row = (tbk * tb + lax.broadcasted_iota(jnp.int32, (tb, 1), 0)) == 0
        c_l = LRU_C * _log_sigmoid(l_ref[...])
        a2, m2, mult = _lru_decay(r, a, c_l, first_row)
        b_ref[...] = dhs
        row = lax.broadcasted_iota(jnp.int32, (SUB, 512), 0)

        def group(jj, hnext):
            o = pl.multiple_of((tb // SUB - 1 - jj) * SUB, SUB)
            a8 = lru_ref[pl.ds(o, SUB), 1536:2048]
            d8 = b_ref[pl.ds(o, SUB), :]
            b8 = a8 * d8
            for sh in (1, 2, 4):
                ash = jnp.where(row < SUB - sh, pltpu.roll(a8, SUB - sh, 0), 1.0)
                bsh = jnp.where(row < SUB - sh, pltpu.roll(b8, SUB - sh, 0), 0.0)
                b8 = a8 * bsh + b8
                a8 = a8 * ash
            h8 = a8 * hnext + b8
            gs_ref[pl.ds(o, SUB), :] = d8 + jnp.where(row < SUB - 1, pltpu.roll(h8, SUB - 1, 0), hnext)
            return jnp.broadcast_to(h8[0:1, :], (SUB, 512))

        hc_ref[...] = lax.fori_loop(0, tb // SUB, group, hc_ref[...])
        gsum = gs_ref[...]
        hs_ext = jnp.concatenate([hsh_ref[...] * has_prev, hsv], axis=0)
        hprev = pltpu.roll(hs_ext, 1, 0)[SUB:]
        da = gsum * hprev
        dmult = jnp.where(first_row, 0.0, gsum * (ig * xb))
        di = gsum * mult * xb
        dxb = gsum * mult * ig
        dla = da * a - dmult * a2 * lax.rsqrt(m2)
        dr = dla * c_l
        v512_ref[3:4, :] += _colsum(dla * r)
        dgp = jnp.concatenate([dr * r * (1.0 - r), di * ig * (1.0 - ig)], axis=1)
        dgpb = dgp.astype(BF16)
        v1024_ref[0:1, :] += _colsum(dgp)
        dwg_acc[...] += _dot_tn(xb.astype(BF16), dgpb)
        dxb = dxb + _dot_nt(dgpb, wg_ref[...])
        n8 = tb + SUB
        dxb_ext = jnp.concatenate([dxb, dxbhead_ref[...]], axis=0)
        du_lru = (cw[3:4, :] * dxb + cw[2:3, :] * pltpu.roll(dxb_ext, n8 - 1, 0)[0:tb]
                  + cw[1:2, :] * pltpu.roll(dxb_ext, n8 - 2, 0)[0:tb] + cw[0:1, :] * pltpu.roll(dxb_ext, n8 - 3, 0)[0:tb])
        dxbhead_ref[...] = dxb[0:SUB]
        v512_ref[2:3, :] += _colsum(dxb)
        for j in range(4):
            shifted = ul_ext if j == 0 else pltpu.roll(ul_ext, j, 0)
            v512_ref[4 + (3 - j):5 + (3 - j), :] += _colsum(dxb * shifted[HALO:])

        dproj_ref[...] = jnp.concatenate([du_pool, du_lru, dug], axis=1).astype(BF16)

        @pl.when(i == nb - 1)
        def _():
            v512_ref[3:4, :] = v512_ref[3:4, :] * (LRU_C * _sigmoid(-l_ref[...]))
            v512_ref[8:9, :] = v1024_ref[0:1, 0:512]
            v512_ref[9:10, :] = v1024_ref[0:1, 512:1024]
            for g in range(N_POOL_GROUPS):
                dpw_ref[g * 128:(g + 1) * 128, :] = dwp_acc[g * 128:(g + 1) * 128, g * 128:(g + 1) * 128]
            odd_head = (lax.broadcasted_iota(jnp.int32, (512, 128), 0) // 64) % 2 == 1
            for out_ref, col0 in ((dga_ref, 0), (dgx_ref, 512)):
                pairs = jnp.concatenate([dwg_acc[128 * k:128 * (k + 1), col0 + 128 * k:col0 + 128 * (k + 1)]
                                         for k in range(LRU_HEADS // 2)], axis=0)
                out_ref[...] = jnp.where(odd_head, pltpu.roll(pairs, 64, 1), pairs)[:, 0:64]
            wait_scatter()

    rev = lambda w: pl.BlockSpec((tb, w), lambda i: (nb - 1 - i, 0))
    halo = lambda rows, w: pl.BlockSpec((rows, w), lambda i: (jnp.maximum((nb - 1 - i) * (tb // rows) - 1, 0), 0))
    any_spec = pl.BlockSpec(memory_space=pl.ANY)
    smalls = [wp_bd, pool_b, pool_scale, conv_w, wg_bd, lru_l, w_out, _inv_count_head()]
    return pl.pallas_call(
        body, name="mix_bwd", grid=(nb,),
        in_specs=[rev(d), rev(1536), halo(HALO, 1536), rev(512), halo(SUB, 512), rev(2048)]
        + [_const_spec(s.shape) for s in smalls] + [any_spec] * len(scatter_args),
        out_specs=[rev(1536), pl.BlockSpec((16, 512), lambda i: (0, 0)), pl.BlockSpec((512, 128), lambda i: (0, 0)),
                   pl.BlockSpec((512, 64), lambda i: (0, 0)), pl.BlockSpec((512, 64), lambda i: (0, 0))]
        + [any_spec] * n_s,
        out_shape=[jax.ShapeDtypeStruct((t_len, 1536), BF16), jax.ShapeDtypeStruct((16, 512), F32),
                   jax.ShapeDtypeStruct((512, 128), F32), jax.ShapeDtypeStruct((512, 64), F32),
                   jax.ShapeDtypeStruct((512, 64), F32)]
        + _scatter_out_shape(scatter),
        scratch_shapes=[pltpu.VMEM(wp_bd.shape, F32), pltpu.VMEM(wg_bd.shape, F32), pltpu.VMEM((8, 1024), F32),
                        pltpu.VMEM((tb + HALO, 1024), F32),
                        pltpu.VMEM((tb, 512), F32), pltpu.VMEM((tb, 512), F32), pltpu.VMEM((HALO, 512), F32),
                        pltpu.VMEM((SUB, 512), F32), pltpu.VMEM((SUB, 512), F32)]
        + _exchange_scratch(n_s),
        input_output_aliases=_scatter_aliases(scatter, 6 + len(smalls), 5), compiler_params=_params(),
    )(dh1, proj, proj, hs, hs, lru_saved, *smalls, *scatter_args)


def _wgrad(name, a, b, whole, by_rows, tb):
    t_len, m = a.shape
    n = b.shape[1]
    nb = t_len // tb
    n_w, n_r = len(whole), len(by_rows)
    n_small = n_w + n_r

    def body(*refs):
        a_ref, b_ref = refs[:2]
        small_in = refs[2:2 + n_small]
        out_ref = refs[2 + n_small]
        small_out = refs[3 + n_small:3 + 2 * n_small]
        acc_ref, stage_ref = refs[3 + 2 * n_small:5 + 2 * n_small]
        rest = refs[5 + 2 * n_small:]
        if n_small:
            send_partials, reduce_and_send_sums, finish_small = _small_allreduce(
                small_in[:n_w], small_in[n_w:], small_out[:n_w], small_out[n_w:], rest[:n_w], rest[n_w:n_small],
                rest[n_small:n_small + n_r], *rest[n_small + n_r:])
        i = pl.program_id(0)

        @pl.when(i == 0)
        def _():
            if n_small:
                send_partials()
            acc_ref[...] = jnp.zeros_like(acc_ref)

        acc_ref[...] += _dot_tn(a_ref[...], b_ref[...].astype(BF16))

        if n_small:
            @pl.when(i == nb // 2)
            def _():
                reduce_and_send_sums()

        @pl.when(i == nb - 1)
        def _():
            stage_ref[...] = acc_ref[...].astype(BF16)
            pltpu.sync_copy(stage_ref, out_ref)
            if n_small:
                finish_small()

    small = list(whole) + list(by_rows)
    vmem_spec = pl.BlockSpec(memory_space=pltpu.VMEM)
    res = pl.pallas_call(
        body, name=name, grid=(nb,),
        in_specs=[pl.BlockSpec((tb, m), lambda i: (i, 0)), pl.BlockSpec((tb, n), lambda i: (i, 0))] + [vmem_spec] * n_small,
        out_specs=[pl.BlockSpec(memory_space=pl.ANY)] + [vmem_spec] * n_small,
        out_shape=[jax.ShapeDtypeStruct((m, n), BF16)] + [jax.ShapeDtypeStruct(s_.shape, F32) for s_ in small],
        scratch_shapes=[pltpu.VMEM((m, n), F32), pltpu.VMEM((m, n), BF16)]
        + (_small_allreduce_scratch(whole, by_rows) if n_small else []),
        compiler_params=_params(),
    )(a, b, *small)
    return res[0], res[1:1 + n_w], res[1 + n_w:]


def _in_bwd(dproj, x, dh1, g_mix, w_in, scatter, tb):
    t_len, d = x.shape
    nb = t_len // tb
    n_s = len(scatter['blocks'])
    scatter_args = _scatter_args(scatter)

    def body(*refs):
        dproj_ref, x_ref, dh1_ref, g_ref, win_ref = refs[:5]
        scatter_in = refs[5:5 + len(scatter_args)]
        dx_ref, vec_ref = refs[5 + len(scatter_args):7 + len(scatter_args)]
        recv = refs[7 + len(scatter_args):7 + len(scatter_args) + n_s]
        vec_acc, send_sems, recv_sems, local_sems, vec_land, small_send, small_recv = refs[7 + len(scatter_args) + n_s:]
        start_scatter, wait_scatter = _scatter_ops(scatter, scatter_in, recv, (send_sems, recv_sems, local_sems))
        send_partials, reduce_and_send_sums, finish_small = _small_allreduce(
            [vec_acc], [], [vec_ref], [], [vec_land], [], [], small_send, small_recv)
        i = pl.program_id(0)

        @pl.when(i == 0)
        def _():
            start_scatter()
            vec_acc[...] = jnp.zeros_like(vec_acc)

        dz1 = _dot_nt(dproj_ref[...], win_ref[...])
        g = g_ref[...]
        _, xh, rr = _rms_fwd(x_ref[...], g)
        dx_ref[...] = dh1_ref[...] + _rms_bwd(xh, rr, g, dz1)
        vec_acc[0:1, :] += _colsum(dz1 * xh)

        @pl.when(i == nb - 1)
        def _():
            send_partials()
            reduce_and_send_sums()
            finish_small()
            wait_scatter()

    row_spec = lambda w: pl.BlockSpec((tb, w), lambda i: (i, 0))
    any_spec = pl.BlockSpec(memory_space=pl.ANY)
    return pl.pallas_call(
        body, name="in_bwd", grid=(nb,),
        in_specs=[row_spec(dproj.shape[1]), row_spec(d), row_spec(d), _const_spec(g_mix.shape), _const_spec(w_in.shape)]
        + [any_spec] * len(scatter_args),
        out_specs=[row_spec(d), pl.BlockSpec((8, d), lambda i: (0, 0))] + [any_spec] * n_s,
        out_shape=[jax.ShapeDtypeStruct((t_len, d), F32), jax.ShapeDtypeStruct((8, d), F32)] + _scatter_out_shape(scatter),
        scratch_shapes=[pltpu.VMEM((8, d), F32)] + _exchange_scratch(n_s)
        + _small_allreduce_scratch([jax.ShapeDtypeStruct((8, d), F32)], []),
        input_output_aliases=_scatter_aliases(scatter, 5, 2), compiler_params=_params(),
    )(dproj, x, dh1, g_mix, w_in, *scatter_args)


def _small_allreduce(whole_in, rows_in, whole_out, rows_out, whole_land, rows_land, rows_sum, send_sems, recv_sems):
    n_w, n_r = len(whole_in), len(rows_in)
    per = [r.shape[0] // N_DEV for r in rows_in]
    me = _my_index()

    def dev(s):
        return (s // 4, (s // 2) % 2, s % 2)

    def rows_of(t, s):
        return pl.ds(s * per[t], per[t])

    def mine(t):
        return pl.ds(pl.multiple_of(me * per[t], 8), per[t])

    def partial(t, s, slot):
        if t < n_w:
            src, dst = whole_in[t], whole_land[t]
        else:
            src, dst = rows_in[t - n_w].at[rows_of(t - n_w, s)], rows_land[t - n_w]
        return pltpu.make_async_remote_copy(
            src_ref=src, dst_ref=dst.at[slot], send_sem=send_sems.at[t, s], recv_sem=recv_sems.at[t, slot],
            device_id=dev(s), device_id_type=MESH)

    def summed(t, s, rows, slot):
        return pltpu.make_async_remote_copy(
            src_ref=rows_sum[t].at[rows], dst_ref=rows_sum[t].at[rows], send_sem=send_sems.at[n_w + n_r + t, s],
            recv_sem=recv_sems.at[n_w + n_r + t, slot], device_id=dev(s), device_id_type=MESH)

    def send_partials():
        for s in range(N_DEV):
            @pl.when(s != me)
            def _():
                for t in range(n_w + n_r):
                    partial(t, s, me).start()
        for t in range(n_w):
            whole_land[t][me] = whole_in[t][...]
        for t in range(n_r):
            rows_land[t][me] = rows_in[t][mine(t), :]

    def reduce_and_send_sums():
        for s in range(N_DEV):
            @pl.when(s != me)
            def _():
                for t in range(n_w + n_r):
                    partial(t, s, s).wait_recv()
        for t in range(n_w):
            total = whole_land[t][0]
            for s in range(1, N_DEV):
                total = total + whole_land[t][s]
            whole_out[t][...] = total
        for t in range(n_r):
            total = rows_land[t][0]
            for s in range(1, N_DEV):
                total = total + rows_land[t][s]
            rows_sum[t][mine(t), :] = total
        for s in range(N_DEV):
            @pl.when(s != me)
            def _():
                for t in range(n_r):
                    summed(t, s, mine(t), me).start()

    def finish():
        for s in range(N_DEV):
            @pl.when(s != me)
            def _():
                for t in range(n_r):
                    summed(t, s, rows_of(t, s), s).wait_recv()
                    summed(t, s, mine(t), me).wait_send()
                for t in range(n_w + n_r):
                    partial(t, s, me).wait_send()
        for t in range(n_r):
            rows_out[t][...] = rows_sum[t][...]

    return send_partials, reduce_and_send_sums, finish


def _small_allreduce_scratch(whole, by_rows):
    n_sem = len(whole) + 2 * len(by_rows)
    return ([pltpu.VMEM((N_DEV,) + a.shape, F32) for a in whole]
            + [pltpu.VMEM((N_DEV, a.shape[0] // N_DEV, a.shape[1]), F32) for a in by_rows]
            + [pltpu.VMEM(a.shape, F32) for a in by_rows]
            + [pltpu.SemaphoreType.DMA((n_sem, N_DEV)), pltpu.SemaphoreType.DMA((n_sem, N_DEV))])


def _adam_update(g, w, m, v):
    m_new = ADAM_B1 * m + (1.0 - ADAM_B1) * g
    v_new = ADAM_B2 * v + (1.0 - ADAM_B2) * jnp.square(g)
    m_hat = m_new / (1.0 - ADAM_B1 ** ADAM_STEP)
    v_hat = v_new / (1.0 - ADAM_B2 ** ADAM_STEP)
    return -ADAM_LR * (m_hat / (jnp.sqrt(v_hat) + ADAM_EPS) + ADAM_WD * w), m_new, v_new


def _adamw_groups(name, groups):
    n = len(groups)

    def body(*refs):
        for k in range(n):
            g_ref, w_ref, m_ref, v_ref = refs[4 * k:4 * k + 4]
            g_out, d_out, m_out, v_out = refs[4 * n + 4 * k:4 * n + 4 * k + 4]
            g = g_ref[...]
            g_out[...] = g
            d_out[...], m_out[...], v_out[...] = _adam_update(g, w_ref[...], m_ref[...], v_ref[...])

    flat = [a for grp in groups for a in grp]
    out = pl.pallas_call(body, name=name,
                         out_shape=[jax.ShapeDtypeStruct(grp[0].shape, F32) for grp in groups for _ in range(4)])(*flat)
    return [out[4 * k:4 * k + 4] for k in range(n)]


def _adamw(name, parts, w, m, v, row_block):
    n_src, rows, cols = parts.shape
    rb = min(row_block, rows)

    def body(p_ref, w_ref, m_ref, v_ref, g_out, d_out, m_out, v_out):
        g = p_ref[0].astype(F32)
        for s in range(1, n_src):
            g = g + p_ref[s].astype(F32)
        g_out[...] = g
        d_out[...], m_out[...], v_out[...] = _adam_update(g, w_ref[...], m_ref[...], v_ref[...])

    spec = pl.BlockSpec((rb, cols), lambda i: (i, 0))
    return pl.pallas_call(
        body, name=name, grid=(rows // rb,),
        in_specs=[pl.BlockSpec((n_src, rb, cols), lambda i: (0, i, 0)), spec, spec, spec],
        out_specs=[spec] * 4, out_shape=[jax.ShapeDtypeStruct((rows, cols), F32)] * 4,
        compiler_params=pltpu.CompilerParams(dimension_semantics=("parallel",), vmem_limit_bytes=VMEM_LIMIT),
    )(parts, w, m, v)


def _block_diag(blocks):
    g, a, b = blocks.shape
    eye = jnp.eye(g, dtype=blocks.dtype)
    return (eye[:, None, :, None] * blocks[:, :, None, :]).reshape(g * a, g * b)


def kernel(x, p, norm_mix_g, w_in, pool_w, pool_b, pool_scale, conv_w, conv_b, gate_a_w, gate_a_b, gate_x_w, gate_x_b, lru_L, w_out, norm_mlp_g, w_up, w_down, norm_ple_g, w_ple_gate, b_ple_gate, w_ple_proj, norm_final_g, loss_target, m_norm_mix_g, m_w_in, m_pool_w, m_pool_b, m_pool_scale, m_conv_w, m_conv_b, m_gate_a_w, m_gate_a_b, m_gate_x_w, m_gate_x_b, m_lru_L, m_w_out, m_norm_mlp_g, m_w_up, m_w_down, m_norm_ple_g, m_w_ple_gate, m_b_ple_gate, m_w_ple_proj, m_norm_final_g, v_norm_mix_g, v_w_in, v_pool_w, v_pool_b, v_pool_scale, v_conv_w, v_conv_b, v_gate_a_w, v_gate_a_b, v_gate_x_w, v_gate_x_b, v_lru_L, v_w_out, v_norm_mlp_g, v_w_up, v_w_down, v_norm_ple_g, v_w_ple_gate, v_b_ple_gate, v_w_ple_proj, v_norm_final_g):
    t_len, d = x.shape[1], x.shape[2]
    tbs = {k: min(v, t_len) for k, v in TIME_BLOCKS.items()}
    me = _my_index()

    win_g, wout_g, convw_g = _gather("gather_mixer_weights", [w_in[0].astype(BF16), w_out[0].astype(BF16), conv_w[0]])
    w_in_f = jnp.transpose(win_g, (1, 0, 2)).reshape(d, -1)
    conv_w_f = jnp.transpose(convw_g, (1, 0, 2)).reshape(convw_g.shape[1], -1)
    wp_bd = _block_diag(pool_w[0]).astype(BF16)
    wg_bd = jnp.concatenate([_block_diag(gate_a_w[0]), _block_diag(gate_x_w[0])], axis=1).astype(BF16)
    gate_b2 = jnp.concatenate([gate_a_b.reshape(1, -1), gate_x_b.reshape(1, -1)], axis=1)
    mixer_small = (norm_mix_g, w_in_f, wp_bd, pool_b.reshape(1, -1), pool_scale, conv_w_f, conv_b, wg_bd, gate_b2, lru_L,
                   wout_g.reshape(-1, d))

    x2 = x[0]
    later = [w_up[0].astype(BF16), w_down[0].astype(BF16), w_ple_gate[0].astype(BF16), w_ple_proj[0].astype(BF16)]
    h1, z1, proj, hs, cat, lru_saved, wup_g, wdn_g, wgate_g, wproj_g = _mix_fwd(
        x2, *mixer_small, later, [_core_major_slot, _core_major_slot, None, None], GATHER_FORWARD_AT, tbs['mix_fwd'])
    w_down_f = wdn_g.reshape(-1, d)
    w_proj_f = jnp.transpose(wproj_g, (1, 0, 2)).reshape(wproj_g.shape[1], -1)
    h2, z2, up = _mlp_fwd(h1, norm_mlp_g, wup_g, w_down_f, tbs['mlp_fwd'])
    dh2, ple_vec, dw_gate, dw_proj = _ple(h2, p[0, 0], loss_target[0], norm_ple_g, wgate_g.reshape(-1, d), b_ple_gate,
                                          w_proj_f, norm_final_g.reshape(1, -1), tbs['ple'])
    everyone = list(range(N_DEV))
    n_proj = w_ple_proj.shape[2]
    scatter = _scatter_plan([dw_gate.reshape(N_DEV, -1, d), jnp.transpose(dw_proj.reshape(-1, N_DEV, n_proj), (1, 0, 2))],
                            [everyone, everyone], [None, None])
    dz2_0, dw_up_0, dw_down_0, recv_gate, recv_proj = _mlp_bwd_part(
        0, MLP_BWD_SPLIT, dh2, z2, up, wup_g, w_down_f, None, h1, norm_mlp_g, scatter, tbs['mlp_bwd'])
    half = N_DEV // MLP_BWD_SPLIT
    south = [_device_of_core_major_slot(k) for k in range(half)]
    north = [_device_of_core_major_slot(k) for k in range(half, N_DEV)]
    scatter = _scatter_plan([dw_up_0, dw_down_0], [south, south], [None, None])
    dh1, mlp_vec, dw_up_1, dw_down_1, recv_up, recv_down = _mlp_bwd_part(
        1, MLP_BWD_SPLIT, dh2, z2, up, wup_g, w_down_f, dz2_0, h1, norm_mlp_g, scatter, tbs['mlp_bwd'])
    dw_out, _, _ = _wgrad("wgrad_out", cat, dh1, [], [], tbs['wgrad_out'])
    scatter = _scatter_plan([dw_up_1, dw_down_1, dw_out.reshape(N_DEV, -1, d)], [north, north, everyone],
                            [recv_up, recv_down, None])
    dproj, v512, dpw, dga, dgx, recv_up, recv_down, recv_out = _mix_bwd(
        dh1, proj, hs, lru_saved, wp_bd, pool_b.reshape(1, -1), pool_scale, conv_w_f, wg_bd, lru_L, wout_g.reshape(-1, d),
        scatter, tbs['mix_bwd'])
    rows1024 = jnp.concatenate([jnp.zeros((1, d), F32), mlp_vec[0:1], ple_vec[1:2], ple_vec[0:1], ple_vec[2:4],
                                jnp.zeros((2, d), F32)], axis=0)
    dw_in, (rows1024, rows512), (g_pool_w, g_gate_a_w, g_gate_x_w) = _wgrad(
        "wgrad_in", z1, dproj, [rows1024, v512], [dpw, dga, dgx], tbs['wgrad_in'])
    n_in = w_in.shape[2]
    scatter = _scatter_plan([jnp.transpose(dw_in.reshape(d, N_DEV, n_in), (1, 0, 2))], [everyone], [None])
    dx, in_vec, recv_in = _in_bwd(dproj, x2, dh1, norm_mix_g, w_in_f, scatter, tbs['in_bwd'])
    rows1024 = jnp.concatenate([in_vec[0:1], rows1024[1:]], axis=0)
    received = [recv_in, recv_out, recv_up, recv_down, recv_gate, recv_proj]

    shard_w = [w_in[0], w_out[0], w_up[0], w_down[0], w_ple_gate[0], w_ple_proj[0]]
    shard_m = [m_w_in[0], m_w_out[0], m_w_up[0], m_w_down[0], m_w_ple_gate[0], m_w_ple_proj[0]]
    shard_v = [v_w_in[0], v_w_out[0], v_w_up[0], v_w_down[0], v_w_ple_gate[0], v_w_ple_proj[0]]
    names = ["w_in", "w_out", "w_up", "w_down", "w_ple_gate", "w_ple_proj"]
    res = {}
    for nm, parts, w_s, m_s, v_s in zip(names, received, shard_w, shard_m, shard_v):
        res[nm] = [r[None] for r in _adamw("adamw_" + nm, parts, w_s, m_s, v_s, ADAM_ROW_BLOCK)]

    def rows_of_1024(a, b, c, e, f):
        return jnp.concatenate([a, b, c, e, f.reshape(1, -1), jnp.zeros((3, d), F32)], axis=0)

    def rows_of_512(scale, bias, cb, lru, ga, gx):
        z = jnp.zeros((1, 512), F32)
        return jnp.concatenate([scale, bias.reshape(1, -1), cb, lru, z, z, z, z, ga.reshape(1, -1), gx.reshape(1, -1),
                                z, z, z, z, z, z], axis=0)

    n_conv = conv_w.shape[2]
    groups = [
        (rows1024, *[rows_of_1024(*t) for t in (
            (norm_mix_g, norm_mlp_g, norm_ple_g, b_ple_gate, norm_final_g),
            (m_norm_mix_g, m_norm_mlp_g, m_norm_ple_g, m_b_ple_gate, m_norm_final_g),
            (v_norm_mix_g, v_norm_mlp_g, v_norm_ple_g, v_b_ple_gate, v_norm_final_g))]),
        (rows512, *[rows_of_512(*t) for t in (
            (pool_scale, pool_b, conv_b, lru_L, gate_a_b, gate_x_b),
            (m_pool_scale, m_pool_b, m_conv_b, m_lru_L, m_gate_a_b, m_gate_x_b),
            (v_pool_scale, v_pool_b, v_conv_b, v_lru_L, v_gate_a_b, v_gate_x_b))]),
        (g_pool_w, *[a.reshape(-1, a.shape[-1]) for a in (pool_w, m_pool_w, v_pool_w)]),
        (g_gate_a_w, *[a.reshape(-1, a.shape[-1]) for a in (gate_a_w, m_gate_a_w, v_gate_a_w)]),
        (g_gate_x_w, *[a.reshape(-1, a.shape[-1]) for a in (gate_x_w, m_gate_x_w, v_gate_x_w)]),
        (lax.dynamic_slice_in_dim(rows512[4:8], me * n_conv, n_conv, axis=1), conv_w[0], m_conv_w[0], v_conv_w[0]),
    ]
    r1024, r512, r_pool, r_ga, r_gx, r_conv = _adamw_groups("adamw_small", groups)
    loss = rows1024[5, 0]
    for k, nm in enumerate(["norm_mix_g", "norm_mlp_g", "norm_ple_g", "b_ple_gate"]):
        res[nm] = [a[k:k + 1] for a in r1024]
    res["norm_final_g"] = [a[4] for a in r1024]
    res["pool_scale"] = [a[0:1] for a in r512]
    res["pool_b"] = [a[1:2].reshape(pool_b.shape) for a in r512]
    res["conv_b"] = [a[2:3] for a in r512]
    res["lru_L"] = [a[3:4] for a in r512]
    res["gate_a_b"] = [a[8:9].reshape(gate_a_b.shape) for a in r512]
    res["gate_x_b"] = [a[9:10].reshape(gate_x_b.shape) for a in r512]
    res["pool_w"] = [a.reshape(pool_w.shape) for a in r_pool]
    res["gate_a_w"] = [a.reshape(gate_a_w.shape) for a in r_ga]
    res["gate_x_w"] = [a.reshape(gate_x_w.shape) for a in r_gx]
    res["conv_w"] = [a[None] for a in r_conv]
    order = ["norm_mix_g", "w_in", "pool_w", "pool_b", "pool_scale", "conv_w", "conv_b", "gate_a_w", "gate_a_b",
             "gate_x_w", "gate_x_b", "lru_L", "w_out", "norm_mlp_g", "w_up", "w_down", "norm_ple_g", "w_ple_gate",
             "b_ple_gate", "w_ple_proj", "norm_final_g"]
    return (loss, dx[None], *[res[nm][kind] for kind in range(4) for nm in order])
```

```python
import jax
import jax.numpy as jnp
from jax import lax
from jax.experimental import pallas as pl
from jax.experimental.pallas import tpu as pltpu

F32 = jnp.float32
BF16 = jnp.bfloat16
MESH = pl.DeviceIdType.MESH

N_DEV = 8
RMS_EPS = 1e-6
LRU_C = 8.0
POOL_WINDOWS = (2, 4, 8, 16)
N_POOL_GROUPS = 4
LRU_HEADS = 8
HALO = 16
SUB = 8
GELU_C0 = 0.7978845608028654
GELU_C1 = 0.044715

ADAM_LR = 0.001
ADAM_B1 = 0.9
ADAM_B2 = 0.999
ADAM_EPS = 1e-08
ADAM_WD = 0.01
ADAM_STEP = 10

VMEM_LIMIT = 60 * 1024 * 1024
TIME_BLOCKS = dict(mix_fwd=512, mlp_fwd=512, ple=512, mlp_bwd=512, wgrad_out=1024, mix_bwd=512, wgrad_in=1024, in_bwd=512)
ADAM_ROW_BLOCK = 256
SCAN_UNROLL = 4
MLP_BWD_SPLIT = 2
GATHER_FORWARD_AT = (0.5, 0.875, 1.0, 1.0)


def _params(n_arbitrary=1):
    return pltpu.CompilerParams(dimension_semantics=("arbitrary",) * n_arbitrary, vmem_limit_bytes=VMEM_LIMIT)


def _dot(a, b):
    return jnp.dot(a, b, preferred_element_type=F32)


def _dot_nt(a, b):
    return lax.dot_general(a, b, (((1,), (1,)), ((), ())), preferred_element_type=F32)


def _dot_tn(a, b):
    return lax.dot_general(a, b, (((0,), (0,)), ((), ())), preferred_element_type=F32)


def _rms_fwd(x, g):
    r = lax.rsqrt(jnp.mean(x * x, axis=-1, keepdims=True) + RMS_EPS)
    xh = x * r
    return xh * g, xh, r


def _rms_bwd(xh, r, g, dz):
    dxh = dz * g
    return r * (dxh - xh * jnp.mean(dxh * xh, axis=-1, keepdims=True))


def _colsum(a):
    return jnp.sum(a, axis=0, keepdims=True)


def _sigmoid(a):
    return 0.5 * jnp.tanh(0.5 * a) + 0.5


def _gelu_parts(u):
    u2 = u * u
    th = jnp.tanh(GELU_C0 * (u + GELU_C1 * u * u2))
    gel = 0.5 * u * (1.0 + th)
    dgel = 0.5 * (1.0 + th) + 0.5 * u * (1.0 - th * th) * (GELU_C0 * (1.0 + 3.0 * GELU_C1 * u2))
    return gel, dgel


def _my_index():
    return 4 * lax.axis_index("x") + 2 * lax.axis_index("y") + lax.axis_index("c")


def _all_to_all(srcs_of, dsts, send_sems, recv_sems, local_sems, dests=None):
    n = len(dsts)
    me = _my_index()
    dests = [list(range(N_DEV))] * n if dests is None else dests

    def remote(t, s):
        return pltpu.make_async_remote_copy(
            src_ref=srcs_of[t](s), dst_ref=dsts[t].at[me], send_sem=send_sems.at[t, s], recv_sem=recv_sems.at[t, me],
            device_id=(s // 4, (s // 2) % 2, s % 2), device_id_type=MESH)

    def arrival(t, s):
        return pltpu.make_async_remote_copy(
            src_ref=srcs_of[t](dests[t][0]), dst_ref=dsts[t].at[s], send_sem=send_sems.at[t, s],
            recv_sem=recv_sems.at[t, s], device_id=(s // 4, (s // 2) % 2, s % 2), device_id_type=MESH)

    def local(t, s):
        return pltpu.make_async_copy(srcs_of[t](s), dsts[t].at[s], local_sems.at[t])

    def start():
        for s in range(N_DEV):
            to_s = [t for t in range(n) if s in dests[t]]

            @pl.when(s == me)
            def _():
                for t in to_s:
                    local(t, s).start()

            @pl.when(s != me)
            def _():
                for t in to_s:
                    remote(t, s).start()

    def wait():
        for s in range(N_DEV):
            to_s = [t for t in range(n) if s in dests[t]]

            @pl.when(s == me)
            def _():
                for t in to_s:
                    local(t, s).wait()
                    for src in range(N_DEV):
                        if src != s:
                            arrival(t, src).wait_recv()

            @pl.when(s != me)
            def _():
                for t in to_s:
                    remote(t, s).wait_send()

    return start, wait


N_GATHER_COPIES = 7


def _core_major_slot(dev):
    return 4 * dev[2] + 2 * dev[0] + dev[1]


def _device_of_core_major_slot(k):
    return (k % 4) * 2 + k // 4


def _two_level_gather(srcs, dsts, send_sems, recv_sems, local_sems, slots=None):
    n = len(dsts)
    x, y, c = lax.axis_index("x"), lax.axis_index("y"), lax.axis_index("c")
    me, sibling = (x, y, c), (x, y, 1 - c)
    chips = [(1 - x, y), (x, 1 - y), (1 - x, 1 - y)]

    def slot(t, dev):
        return 4 * dev[0] + 2 * dev[1] + dev[2] if slots is None or slots[t] is None else slots[t](dev)

    def copy(t, k, block, to, src=None):
        return pltpu.make_async_remote_copy(
            src_ref=dsts[t].at[slot(t, block)] if src is None else src, dst_ref=dsts[t].at[slot(t, block)],
            send_sem=send_sems.at[t, k], recv_sem=recv_sems.at[t, k], device_id=to, device_id_type=MESH)

    def local(t):
        return pltpu.make_async_copy(srcs[t], dsts[t].at[slot(t, me)], local_sems.at[t])

    def start():
        for t in range(n):
            local(t).start()
            for j, chip in enumerate(chips):
                copy(t, 1 + j, me, (*chip, c), src=srcs[t]).start()
            copy(t, 0, me, sibling, src=srcs[t]).start()

    def forward(t):
        for j, chip in enumerate(chips):
            copy(t, 1 + j, (*chip, c), me).wait_recv()
            copy(t, 4 + j, (*chip, c), sibling).start()

    def finish():
        for t in range(n):
            copy(t, 0, sibling, me).wait_recv()
            for j, chip in enumerate(chips):
                copy(t, 4 + j, (*chip, 1 - c), me).wait_recv()
            copy(t, 0, me, sibling, src=srcs[t]).wait_send()
            for j, chip in enumerate(chips):
                copy(t, 1 + j, me, (*chip, c), src=srcs[t]).wait_send()
                copy(t, 4 + j, (*chip, c), sibling).wait_send()
            local(t).wait()

    return start, forward, finish


def _hosted_gather(i, nb, forward_at, srcs, dsts, sems, slots=None):
    start, forward, finish = _two_level_gather(srcs, dsts, *sems, slots)

    def after_step():
        for t, f in enumerate(forward_at):
            @pl.when(i == min(nb - 1, int(f * nb)))
            def _():
                forward(t)

        @pl.when(i == nb - 1)
        def _():
            finish()

    return start, after_step


def _gather_scratch(n):
    return [pltpu.SemaphoreType.DMA((n, N_GATHER_COPIES)), pltpu.SemaphoreType.DMA((n, N_GATHER_COPIES)),
            pltpu.SemaphoreType.DMA((n,))]


def _gather(name, srcs):
    n = len(srcs)

    def body(*refs):
        start, forward, finish = _two_level_gather(refs[:n], refs[n:2 * n], *refs[2 * n:])
        start()
        for t in range(n):
            forward(t)
        finish()

    any_spec = pl.BlockSpec(memory_space=pl.ANY)
    return pl.pallas_call(
        body, name=name, in_specs=[any_spec] * n, out_specs=[any_spec] * n,
        out_shape=[jax.ShapeDtypeStruct((N_DEV,) + a.shape, a.dtype) for a in srcs], scratch_shapes=_gather_scratch(n),
    )(*srcs)


def _scatter_plan(blocks, dests, landing):
    return dict(blocks=list(blocks), dests=[list(dd) for dd in dests], landing=list(landing))


def _scatter_args(plan):
    return plan['blocks'] + [a for a in plan['landing'] if a is not None]


def _scatter_out_shape(plan):
    return [jax.ShapeDtypeStruct((N_DEV,) + b.shape[1:], b.dtype) for b in plan['blocks']]


def _scatter_aliases(plan, first_in, first_out):
    given = [t for t, a in enumerate(plan['landing']) if a is not None]
    return {first_in + len(plan['blocks']) + k: first_out + t for k, t in enumerate(given)}


def _scatter_ops(plan, in_refs, out_refs, sems):
    n = len(plan['blocks'])
    srcs_of = [(lambda s, r=in_refs[t], dd=plan['dests'][t]: r.at[dd.index(s)]) for t in range(n)]
    return _all_to_all(srcs_of, out_refs, *sems, dests=plan['dests'])


def _exchange_scratch(n):
    return [pltpu.SemaphoreType.DMA((n, N_DEV)), pltpu.SemaphoreType.DMA((n, N_DEV)), pltpu.SemaphoreType.DMA((n,))]


def _const_spec(shape):
    nd = len(shape)
    return pl.BlockSpec(shape, lambda i: (0,) * nd, pipeline_mode=pl.Buffered(1))


def _pool_windows(up_ext, n, forward):
    sh = (lambda k: k) if forward else (lambda k: n - k)
    s2 = up_ext + pltpu.roll(up_ext, sh(1), 0)
    t4 = s2[:, 128:]
    s4 = t4 + pltpu.roll(t4, sh(2), 0)
    t8 = s4[:, 128:]
    s8 = t8 + pltpu.roll(t8, sh(4), 0)
    t16 = s8[:, 128:]
    s16 = t16 + pltpu.roll(t16, sh(8), 0)
    return jnp.concatenate([s2[:, :128], s4[:, :128], s8[:, :128], s16], axis=1)


def _inv_count_head():
    t = jnp.arange(1, HALO + 1, dtype=F32)[:, None]
    return jnp.concatenate([jnp.broadcast_to(1.0 / jnp.minimum(t, float(w)), (HALO, 128)) for w in POOL_WINDOWS], axis=1)


def _scale_by_inv_count(v, is_first_block, inv_head):
    inv_row = jnp.concatenate([jnp.full((1, 128), 1.0 / w, F32) for w in POOL_WINDOWS], axis=1)
    head = v[0:HALO] * jnp.where(is_first_block, inv_head, inv_row)
    return jnp.concatenate([head, v[HALO:] * inv_row], axis=0)


def _lru_decay(r, a, c_l, first_row):
    a2 = a * a
    m2 = -jnp.tanh(c_l * r) * (a2 + 1.0)
    return a2, m2, jnp.where(first_row, 1.0, jnp.sqrt(m2))


def _log_sigmoid(v):
    return -(jnp.maximum(-v, 0.0) + jnp.log1p(jnp.exp(-jnp.abs(v))))


def _conv_fwd(ul_ext, cw, cb):
    return (cb + cw[3:4, :] * ul_ext + cw[2:3, :] * pltpu.roll(ul_ext, 1, 0)
            + cw[1:2, :] * pltpu.roll(ul_ext, 2, 0) + cw[0:1, :] * pltpu.roll(ul_ext, 3, 0))


def _mix_fwd(x, g_mix, w_in, wp_bd, pool_b, pool_scale, conv_w, conv_b, wg_bd, gate_b, lru_l, w_out, gather_srcs,
             gather_slots, forward_at, tb):
    t_len, d = x.shape
    nb = t_len // tb
    n_g = len(gather_srcs)

    def body(*refs):
        (x_ref, g_ref, win_ref, wp_ref, pb_ref, ps_ref, cw_ref, cb_ref, wg_ref, gb_ref, l_ref, wout_ref,
         invh_ref) = refs[:13]
        gsrc = refs[13:13 + n_g]
        h1_ref, z1_ref, proj_ref, hs_ref, cat_ref, lru_ref = refs[13 + n_g:19 + n_g]
        gdst = refs[19 + n_g:19 + 2 * n_g]
        ext_ref, a_ref, b_ref, hc_ref, send_sems, recv_sems, local_sems = refs[19 + 2 * n_g:]
        i = pl.program_id(0)
        start_gather, after_step = _hosted_gather(i, nb, forward_at, gsrc, gdst, (send_sems, recv_sems, local_sems),
                                                  gather_slots)

        @pl.when(i == 0)
        def _():
            start_gather()
            ext_ref[0:HALO, :] = jnp.zeros((HALO, 1024), F32)
            hc_ref[...] = jnp.zeros_like(hc_ref)

        xv = x_ref[...]
        z, _, _ = _rms_fwd(xv, g_ref[...])
        zb = z.astype(BF16)
        z1_ref[...] = zb
        proj = _dot(zb, win_ref[...])
        proj_ref[...] = proj
        ext_ref[HALO:, :] = proj[:, 0:1024]
        ug = proj[:, 1024:1536]
        n = tb + HALO
        up_ext = ext_ref[:, 0:512]
        win = _pool_windows(up_ext, n, True)[HALO:]
        dpool = _scale_by_inv_count(win, i == 0, invh_ref[...]) - proj[:, 0:512]
        q = _dot(dpool.astype(BF16), wp_ref[...]) + pb_ref[...]
        y_pool = q * ps_ref[...]
        xb = _conv_fwd(ext_ref[:, 512:1024], cw_ref[...], cb_ref[...])[HALO:]
        first_row = (i * tb + lax.broadcasted_iota(jnp.int32, (tb, 1), 0)) == 0
        c_l = LRU_C * _log_sigmoid(l_ref[...])
        gp = _dot(xb.astype(BF16), wg_ref[...]) + gb_ref[...]
        r = _sigmoid(gp[:, :512])
        ig = _sigmoid(gp[:, 512:])
        a = jnp.exp(c_l * r)
        _, _, mult = _lru_decay(r, a, c_l, first_row)
        lru_ref[:, 0:512] = xb
        lru_ref[:, 512:1024] = r
        lru_ref[:, 1024:1536] = ig
        lru_ref[:, 1536:2048] = a
        a_ref[...] = a
        b_ref[...] = mult * (ig * xb)
        row = lax.broadcasted_iota(jnp.int32, (SUB, 512), 0)

        def group(j, hprev):
            o = pl.multiple_of(j * SUB, SUB)
            a8 = a_ref[pl.ds(o, SUB), :]
            b8 = b_ref[pl.ds(o, SUB), :]
            for sh in (1, 2, 4):
                ash = jnp.where(row >= sh, pltpu.roll(a8, sh, 0), 1.0)
                bsh = jnp.where(row >= sh, pltpu.roll(b8, sh, 0), 0.0)
                b8 = a8 * bsh + b8
                a8 = a8 * ash
            h8 = a8 * hprev + b8
            hs_ref[pl.ds(o, SUB), :] = h8
            return jnp.broadcast_to(h8[SUB - 1:SUB, :], (SUB, 512))

        def trip(k, carry):
            for u in range(SCAN_UNROLL):
                carry = group(k * SCAN_UNROLL + u, carry)
            return carry

        hc_ref[...] = lax.fori_loop(0, tb // (SUB * SCAN_UNROLL), trip, hc_ref[...])
        gel, _ = _gelu_parts(ug)
        y_lru = hs_ref[...] * gel
        catb = jnp.concatenate([y_pool, y_lru], axis=1).astype(BF16)
        cat_ref[...] = catb
        h1_ref[...] = xv + _dot(catb, wout_ref[...])
        ext_ref[0:HALO, :] = ext_ref[tb:tb + HALO, :]

        after_step()

    row_spec = lambda w: pl.BlockSpec((tb, w), lambda i: (i, 0))
    any_spec = pl.BlockSpec(memory_space=pl.ANY)
    smalls = [g_mix, w_in, wp_bd, pool_b, pool_scale, conv_w, conv_b, wg_bd, gate_b, lru_l, w_out, _inv_count_head()]
    return pl.pallas_call(
        body, name="mix_fwd", grid=(nb,),
        in_specs=[row_spec(d)] + [_const_spec(s.shape) for s in smalls] + [any_spec] * n_g,
        out_specs=[row_spec(d), row_spec(d), row_spec(1536), row_spec(512), row_spec(1024), row_spec(2048)]
        + [any_spec] * n_g,
        out_shape=[jax.ShapeDtypeStruct((t_len, d), F32), jax.ShapeDtypeStruct((t_len, d), BF16),
                   jax.ShapeDtypeStruct((t_len, 1536), F32), jax.ShapeDtypeStruct((t_len, 512), F32),
                   jax.ShapeDtypeStruct((t_len, 1024), BF16), jax.ShapeDtypeStruct((t_len, 2048), F32)]
        + [jax.ShapeDtypeStruct((N_DEV,) + s.shape, s.dtype) for s in gather_srcs],
        scratch_shapes=[pltpu.VMEM((tb + HALO, 1024), F32), pltpu.VMEM((tb, 512), F32), pltpu.VMEM((tb, 512), F32),
                        pltpu.VMEM((SUB, 512), F32)] + _gather_scratch(n_g),
        compiler_params=_params(),
    )(x, *smalls, *gather_srcs)


def _mlp_fwd(h1, g_mlp, w_up, w_down, tb):
    t_len, d = h1.shape
    nb = t_len // tb
    n_chunk, _, fc = w_up.shape

    def body(h1_ref, g_ref, wup_ref, wdn_ref, h2_ref, z2_ref, up_ref):
        xv = h1_ref[...]
        z, _, _ = _rms_fwd(xv, g_ref[...])
        zb = z.astype(BF16)
        z2_ref[...] = zb
        acc = xv
        for c in range(n_chunk):
            u = _dot(zb, wup_ref[c])
            up_ref[:, c * fc:(c + 1) * fc] = u.astype(BF16)
            act = jnp.square(jnp.maximum(u, 0.0)).astype(BF16)
            acc = acc + _dot(act, wdn_ref[c * fc:(c + 1) * fc, :])
        h2_ref[...] = acc

    row_spec = lambda w: pl.BlockSpec((tb, w), lambda i: (i, 0))
    return pl.pallas_call(
        body, name="mlp_fwd", grid=(nb,),
        in_specs=[row_spec(d), _const_spec(g_mlp.shape), _const_spec(w_up.shape), _const_spec(w_down.shape)],
        out_specs=[row_spec(d), row_spec(d), row_spec(n_chunk * fc)],
        out_shape=[jax.ShapeDtypeStruct((t_len, d), F32), jax.ShapeDtypeStruct((t_len, d), BF16),
                   jax.ShapeDtypeStruct((t_len, n_chunk * fc), BF16)],
        compiler_params=_params(),
    )(h1, g_mlp, w_up, w_down)


def _ple(h2, p, target, g_ple, w_gate, b_gate, w_proj, g_final, tb):
    t_len, d = h2.shape
    nb = t_len // tb
    pd = p.shape[1]

    def body(h2_ref, p_ref, tgt_ref, g_ref, wg_ref, bg_ref, wp_ref, gf_ref,
             dh2_ref, vec_ref, dwg_out, dwp_out, dwg_acc, dwp_acc, dwg_stage, dwp_stage):
        i = pl.program_id(0)

        @pl.when(i == 0)
        def _():
            vec_ref[...] = jnp.zeros_like(vec_ref)
            dwg_acc[...] = jnp.zeros_like(dwg_acc)
            dwp_acc[...] = jnp.zeros_like(dwp_acc)

        h2 = h2_ref[...]
        g2 = g_ref[...]
        z3, xh2, r2 = _rms_fwd(h2, g2)
        z3b = z3.astype(BF16)
        gate = _sigmoid(_dot(z3b, wg_ref[...]) + bg_ref[...])
        pb = p_ref[...].astype(BF16)
        pp = _dot(pb, wp_ref[...])
        h3 = h2 + gate * pp
        gf = gf_ref[...]
        y, xh3, r3 = _rms_fwd(h3, gf)
        err = y - tgt_ref[...]
        loss_rows = jnp.mean(err * err, axis=-1, keepdims=True)
        dy = err * (1.0 / d)
        dh3 = _rms_bwd(xh3, r3, gf, dy)
        dgl = (dh3 * pp) * (gate * (1.0 - gate))
        dpp = dh3 * gate
        dglb = dgl.astype(BF16)
        dwg_acc[...] += _dot_tn(z3b, dglb)
        dwp_acc[...] += _dot_tn(pb, dpp.astype(BF16))
        dz3 = _dot_nt(dglb, wg_ref[...])
        dh2_ref[...] = dh3 + _rms_bwd(xh2, r2, g2, dz3)
        vec_ref[0:1, :] += _colsum(dgl)
        vec_ref[1:2, :] += _colsum(dz3 * xh2)
        vec_ref[2:3, :] += _colsum(dy * xh3)
        vec_ref[3:4, :] += 0.5 * jnp.sum(loss_rows)

        @pl.when(i == nb - 1)
        def _():
            dwg_stage[...] = dwg_acc[...].astype(BF16)
            dwp_stage[...] = dwp_acc[...].astype(BF16)
            pltpu.sync_copy(dwg_stage, dwg_out)
            pltpu.sync_copy(dwp_stage, dwp_out)

    row_spec = lambda w: pl.BlockSpec((tb, w), lambda i: (i, 0))
    any_spec = pl.BlockSpec(memory_space=pl.ANY)
    smalls = [g_ple, w_gate, b_gate, w_proj, g_final]
    return pl.pallas_call(
        body, name="ple_fwd_bwd", grid=(nb,),
        in_specs=[row_spec(d), row_spec(pd), row_spec(d)] + [_const_spec(s.shape) for s in smalls],
        out_specs=[row_spec(d), pl.BlockSpec((8, d), lambda i: (0, 0)), any_spec, any_spec],
        out_shape=[jax.ShapeDtypeStruct((t_len, d), F32), jax.ShapeDtypeStruct((8, d), F32),
                   jax.ShapeDtypeStruct(w_gate.shape, BF16), jax.ShapeDtypeStruct(w_proj.shape, BF16)],
        scratch_shapes=[pltpu.VMEM(w_gate.shape, F32), pltpu.VMEM(w_proj.shape, F32), pltpu.VMEM(w_gate.shape, BF16),
                        pltpu.VMEM(w_proj.shape, BF16)],
        compiler_params=_params(),
    )(h2, p, target, *smalls)


def _mlp_bwd_part(part, n_part, dh2, z2, up, w_up, w_down, dz2_prev, h1, g_mlp, scatter, tb):
    t_len, d = dh2.shape
    nb = t_len // tb
    n_chunk_all, _, fc = w_up.shape
    n_chunk = n_chunk_all // n_part
    first, last = part == 0, part == n_part - 1

    def body(*refs):
        refs = list(refs)
        dh2_ref, z2_ref, up_ref, wup_ref, wdn_ref = refs[:5]
        del refs[:5]
        dzp_ref = None if first else refs.pop(0)
        h1_ref, g_ref = (refs.pop(0), refs.pop(0)) if last else (None, None)
        scatter_in = [refs.pop(0) for _ in _scatter_args(scatter)]
        out_ref = refs.pop(0)
        vec_ref = refs.pop(0) if last else None
        dwup_out, dwdn_out = refs.pop(0), refs.pop(0)
        scatter_out = [refs.pop(0) for _ in scatter['blocks']]
        dwup_acc, dwdn_acc, up_stage, dn_stage = refs[:4]
        if scatter['blocks']:
            start_scatter, wait_scatter = _scatter_ops(scatter, scatter_in, scatter_out, refs[4:])
        i = pl.program_id(0)

        @pl.when(i == 0)
        def _():
            if scatter['blocks']:
                start_scatter()
            dwup_acc[...] = jnp.zeros_like(dwup_acc)
            dwdn_acc[...] = jnp.zeros_like(dwdn_acc)
            if last:
                vec_ref[...] = jnp.zeros_like(vec_ref)

        dh2 = dh2_ref[...]
        dh2b = dh2.astype(BF16)
        z2b = z2_ref[...]
        dz2 = jnp.zeros((tb, d), F32) if first else dzp_ref[...]
        for c in range(n_chunk):
            u = up_ref[:, c * fc:(c + 1) * fc].astype(F32)
            ur = jnp.maximum(u, 0.0)
            dact = _dot_nt(dh2b, wdn_ref[c * fc:(c + 1) * fc, :])
            dupb = (dact * (2.0 * ur)).astype(BF16)
            dwdn_acc[c * fc:(c + 1) * fc, :] += _dot_tn((ur * ur).astype(BF16), dh2b)
            dwup_acc[c] += _dot_tn(z2b, dupb)
            dz2 = dz2 + _dot_nt(dupb, wup_ref[c])
        if last:
            g = g_ref[...]
            _, xh, r = _rms_fwd(h1_ref[...], g)
            out_ref[...] = dh2 + _rms_bwd(xh, r, g, dz2)
            vec_ref[0:1, :] += _colsum(dz2 * xh)
        else:
            out_ref[...] = dz2

        @pl.when(i == nb - 1)
        def _():
            for c in range(n_chunk):
                up_stage[...] = dwup_acc[c].astype(BF16)
                dn_stage[...] = dwdn_acc[c * fc:(c + 1) * fc, :].astype(BF16)
                pltpu.sync_copy(up_stage, dwup_out.at[c])
                pltpu.sync_copy(dn_stage, dwdn_out.at[c])
            if scatter['blocks']:
                wait_scatter()

    row_spec = lambda w: pl.BlockSpec((tb, w), lambda i: (i, 0))
    any_spec = pl.BlockSpec(memory_space=pl.ANY)
    args = [dh2, z2, up, w_up, w_down]
    in_specs = [row_spec(d), row_spec(d), pl.BlockSpec((tb, n_chunk * fc), lambda i: (i, part)),
                pl.BlockSpec((n_chunk, d, fc), lambda i: (part, 0, 0), pipeline_mode=pl.Buffered(1)),
                pl.BlockSpec((n_chunk * fc, d), lambda i: (part, 0), pipeline_mode=pl.Buffered(1))]
    if not first:
        args.append(dz2_prev)
        in_specs.append(row_spec(d))
    if last:
        args += [h1, g_mlp]
        in_specs += [row_spec(d), _const_spec(g_mlp.shape)]
    n_in = len(args)
    args += _scatter_args(scatter)
    in_specs += [any_spec] * len(_scatter_args(scatter))
    out_specs = [row_spec(d)]
    out_shape = [jax.ShapeDtypeStruct((t_len, d), F32)]
    if last:
        out_specs.append(pl.BlockSpec((8, d), lambda i: (0, 0)))
        out_shape.append(jax.ShapeDtypeStruct((8, d), F32))
    out_specs += [any_spec, any_spec]
    out_shape += [jax.ShapeDtypeStruct((n_chunk, d, fc), BF16), jax.ShapeDtypeStruct((n_chunk, fc, d), BF16)]
    n_out = len(out_shape)
    out_specs += [any_spec] * len(scatter['blocks'])
    out_shape += _scatter_out_shape(scatter)
    return pl.pallas_call(
        body, name=f"mlp_bwd_{part}", grid=(nb,), in_specs=in_specs, out_specs=out_specs, out_shape=out_shape,
        scratch_shapes=[pltpu.VMEM((n_chunk, d, fc), F32), pltpu.VMEM((n_chunk * fc, d), F32),
                        pltpu.VMEM((d, fc), BF16), pltpu.VMEM((fc, d), BF16)]
        + (_exchange_scratch(len(scatter['blocks'])) if scatter['blocks'] else []),
        input_output_aliases=_scatter_aliases(scatter, n_in, n_out), compiler_params=_params(),
    )(*args)


def _mix_bwd(dh1, proj, hs, lru_saved, wp_bd, pool_b, pool_scale, conv_w, wg_bd, lru_l, w_out, scatter, tb):
    t_len, d = dh1.shape
    nb = t_len // tb
    n_s = len(scatter['blocks'])
    scatter_args = _scatter_args(scatter)

    def body(*refs):
        refs = list(refs)
        (dh1_ref, proj_ref, projh_ref, hs_ref, hsh_ref, lru_ref,
         wp_ref, pb_ref, ps_ref, cw_ref, wg_ref, l_ref, wout_ref, invh_ref) = refs[:14]
        del refs[:14]
        scatter_in = refs[:len(scatter_args)]
        del refs[:len(scatter_args)]
        dproj_ref, v512_ref, dpw_ref, dga_ref, dgx_ref = refs[:5]
        recv = refs[5:5 + n_s]
        (dwp_acc, dwg_acc, v1024_ref, ext_ref, b_ref, gs_ref, ehead_ref, dxbhead_ref, hc_ref,
         send_sems, recv_sems, local_sems) = refs[5 + n_s:]
        i = pl.program_id(0)
        tbk = nb - 1 - i

        start_scatter, wait_scatter = _scatter_ops(scatter, scatter_in, recv, (send_sems, recv_sems, local_sems))

        @pl.when(i == 0)
        def _():
            start_scatter()
            for ref in (v512_ref, v1024_ref, dwp_acc, dwg_acc, ehead_ref, dxbhead_ref, hc_ref):
                ref[...] = jnp.zeros_like(ref)

        dcat = _dot_nt(dh1_ref[...].astype(BF16), wout_ref[...])

        proj = proj_ref[...]
        has_prev = (tbk > 0).astype(F32)
        ext_ref[0:HALO, :] = projh_ref[:, 0:1024] * has_prev
        ext_ref[HALO:, :] = proj[:, 0:1024]
        ug = proj[:, 1024:1536]
        n = tb + HALO
        inv_head = invh_ref[...]

        up_ext = ext_ref[:, 0:512]
        win = _pool_windows(up_ext, n, True)[HALO:]
        dpool = _scale_by_inv_count(win, tbk == 0, inv_head) - proj[:, 0:512]
        dpoolb = dpool.astype(BF16)
        q = _dot(dpoolb, wp_ref[...]) + pb_ref[...]
        dyp = dcat[:, 0:512]
        dq = dyp * ps_ref[...]
        dqb = dq.astype(BF16)
        v512_ref[0:1, :] += _colsum(dyp * q)
        v512_ref[1:2, :] += _colsum(dq)
        dwp_acc[...] += _dot_tn(dpoolb, dqb)
        dd = _dot_nt(dqb, wp_ref[...])
        e = _scale_by_inv_count(dd, tbk == 0, inv_head)
        e_ext = jnp.concatenate([e, ehead_ref[...]], axis=0)
        du_pool = _pool_windows(e_ext, n, False)[0:tb] - dd
        ehead_ref[...] = e[0:HALO]

        gel, dgel = _gelu_parts(ug)
        hsv = hs_ref[...]
        dcl = dcat[:, 512:1024]
        dhs = dcl * gel
        dug = dcl * hsv * dgel
        ul_ext = ext_ref[:, 512:1024]
        cw = cw_ref[...]
        xb, r, ig, a = lru_ref[:, 0:512], lru_ref[:, 512:1024], lru_ref[:, 1024:1536], lru_ref[:, 1536:2048]
        first_row = (tbk * tb + lax.broadcasted_iota(jnp.int32, (tb, 1), 0)) == 0
        c_l = LRU_C * _log_sigmoid(l_ref[...])
        a2, m2, mult = _lru_decay(r, a, c_l, first_row)
        b_ref[...] = dhs
        row = lax.broadcasted_iota(jnp.int32, (SUB, 512), 0)

        def group(jj, hnext):
            o = pl.multiple_of((tb // SUB - 1 - jj) * SUB, SUB)
            a8 = lru_ref[pl.ds(o, SUB), 1536:2048]
            d8 = b_ref[pl.ds(o, SUB), :]
            b8 = a8 * d8
            for sh in (1, 2, 4):
                ash = jnp.where(row < SUB - sh, pltpu.roll(a8, SUB - sh, 0), 1.0)
                bsh = jnp.where(row < SUB - sh, pltpu.roll(b8, SUB - sh, 0), 0.0)
                b8 = a8 * bsh + b8
                a8 = a8 * ash
            h8 = a8 * hnext + b8
            gs_ref[pl.ds(o, SUB), :] = d8 + jnp.where(row < SUB - 1, pltpu.roll(h8, SUB - 1, 0), hnext)
            return jnp.broadcast_to(h8[0:1, :], (SUB, 512))

        def trip(k, carry):
            for u in range(SCAN_UNROLL):
                carry = group(k * SCAN_UNROLL + u, carry)
            return carry

        hc_ref[...] = lax.fori_loop(0, tb // (SUB * SCAN_UNROLL), trip, hc_ref[...])
        gsum = gs_ref[...]
        hs_ext = jnp.concatenate([hsh_ref[...] * has_prev, hsv], axis=0)
        hprev = pltpu.roll(hs_ext, 1, 0)[SUB:]
        da = gsum * hprev
        dmult = jnp.where(first_row, 0.0, gsum * (ig * xb))
        di = gsum * mult * xb
        dxb = gsum * mult * ig
        dla = da * a - dmult * a2 * lax.rsqrt(m2)
        dr = dla * c_l
        v512_ref[3:4, :] += _colsum(dla * r)
        dgp = jnp.concatenate([dr * r * (1.0 - r), di * ig * (1.0 - ig)], axis=1)
        dgpb = dgp.astype(BF16)
        v1024_ref[0:1, :] += _colsum(dgp)
        dwg_acc[...] += _dot_tn(xb.astype(BF16), dgpb)
        dxb = dxb + _dot_nt(dgpb, wg_ref[...])
        n8 = tb + SUB
        dxb_ext = jnp.concatenate([dxb, dxbhead_ref[...]], axis=0)
        du_lru = (cw[3:4, :] * dxb + cw[2:3, :] * pltpu.roll(dxb_ext, n8 - 1, 0)[0:tb]
                  + cw[1:2, :] * pltpu.roll(dxb_ext, n8 - 2, 0)[0:tb] + cw[0:1, :] * pltpu.roll(dxb_ext, n8 - 3, 0)[0:tb])
        dxbhead_ref[...] = dxb[0:SUB]
        v512_ref[2:3, :] += _colsum(dxb)
        for j in range(4):
            shifted = ul_ext if j == 0 else pltpu.roll(ul_ext, j, 0)
            v512_ref[4 + (3 - j):5 + (3 - j), :] += _colsum(dxb * shifted[HALO:])

        dproj_ref[...] = jnp.concatenate([du_pool, du_lru, dug], axis=1).astype(BF16)

        @pl.when(i == nb - 1)
        def _():
            v512_ref[3:4, :] = v512_ref[3:4, :] * (LRU_C * _sigmoid(-l_ref[...]))
            v512_ref[8:9, :] = v1024_ref[0:1, 0:512]
            v512_ref[9:10, :] = v1024_ref[0:1, 512:1024]
            for g in range(N_POOL_GROUPS):
                dpw_ref[g * 128:(g + 1) * 128, :] = dwp_acc[g * 128:(g + 1) * 128, g * 128:(g + 1) * 128]
            odd_head = (lax.broadcasted_iota(jnp.int32, (512, 128), 0) // 64) % 2 == 1
            for out_ref, col0 in ((dga_ref, 0), (dgx_ref, 512)):
                pairs = jnp.concatenate([dwg_acc[128 * k:128 * (k + 1), col0 + 128 * k:col0 + 128 * (k + 1)]
                                         for k in range(LRU_HEADS // 2)], axis=0)
                out_ref[...] = jnp.where(odd_head, pltpu.roll(pairs, 64, 1), pairs)[:, 0:64]
            wait_scatter()

    rev = lambda w: pl.BlockSpec((tb, w), lambda i: (nb - 1 - i, 0))
    halo = lambda rows, w: pl.BlockSpec((rows, w), lambda i: (jnp.maximum((nb - 1 - i) * (tb // rows) - 1, 0), 0))
    any_spec = pl.BlockSpec(memory_space=pl.ANY)
    smalls = [wp_bd, pool_b, pool_scale, conv_w, wg_bd, lru_l, w_out, _inv_count_head()]
    return pl.pallas_call(
        body, name="mix_bwd", grid=(nb,),
        in_specs=[rev(d), rev(1536), halo(HALO, 1536), rev(512), halo(SUB, 512), rev(2048)]
        + [_const_spec(s.shape) for s in smalls] + [any_spec] * len(scatter_args),
        out_specs=[rev(1536), pl.BlockSpec((16, 512), lambda i: (0, 0)), pl.BlockSpec((512, 128), lambda i: (0, 0)),
                   pl.BlockSpec((512, 64), lambda i: (0, 0)), pl.BlockSpec((512, 64), lambda i: (0, 0))]
        + [any_spec] * n_s,
        out_shape=[jax.ShapeDtypeStruct((t_len, 1536), BF16), jax.ShapeDtypeStruct((16, 512), F32),
                   jax.ShapeDtypeStruct((512, 128), F32), jax.ShapeDtypeStruct((512, 64), F32),
                   jax.ShapeDtypeStruct((512, 64), F32)]
        + _scatter_out_shape(scatter),
        scratch_shapes=[pltpu.VMEM(wp_bd.shape, F32), pltpu.VMEM(wg_bd.shape, F32), pltpu.VMEM((8, 1024), F32),
                        pltpu.VMEM((tb + HALO, 1024), F32),
                        pltpu.VMEM((tb, 512), F32), pltpu.VMEM((tb, 512), F32), pltpu.VMEM((HALO, 512), F32),
                        pltpu.VMEM((SUB, 512), F32), pltpu.VMEM((SUB, 512), F32)]
        + _exchange_scratch(n_s),
        input_output_aliases=_scatter_aliases(scatter, 6 + len(smalls), 5), compiler_params=_params(),
    )(dh1, proj, proj, hs, hs, lru_saved, *smalls, *scatter_args)


def _wgrad(name, a, b, whole, by_rows, tb):
    t_len, m = a.shape
    n = b.shape[1]
    nb = t_len // tb
    n_w, n_r = len(whole), len(by_rows)
    n_small = n_w + n_r

    def body(*refs):
        a_ref, b_ref = refs[:2]
        small_in = refs[2:2 + n_small]
        out_ref = refs[2 + n_small]
        small_out = refs[3 + n_small:3 + 2 * n_small]
        acc_ref, stage_ref = refs[3 + 2 * n_small:5 + 2 * n_small]
        rest = refs[5 + 2 * n_small:]
        if n_small:
            send_partials, reduce_and_send_sums, finish_small = _small_allreduce(
                small_in[:n_w], small_in[n_w:], small_out[:n_w], small_out[n_w:], rest[:n_w], rest[n_w:n_small],
                rest[n_small:n_small + n_r], *rest[n_small + n_r:])
        i = pl.program_id(0)

        @pl.when(i == 0)
        def _():
            if n_small:
                send_partials()
            acc_ref[...] = jnp.zeros_like(acc_ref)

        acc_ref[...] += _dot_tn(a_ref[...], b_ref[...].astype(BF16))

        if n_small:
            @pl.when(i == nb // 2)
            def _():
                reduce_and_send_sums()

        @pl.when(i == nb - 1)
        def _():
            stage_ref[...] = acc_ref[...].astype(BF16)
            pltpu.sync_copy(stage_ref, out_ref)
            if n_small:
                finish_small()

    small = list(whole) + list(by_rows)
    vmem_spec = pl.BlockSpec(memory_space=pltpu.VMEM)
    res = pl.pallas_call(
        body, name=name, grid=(nb,),
        in_specs=[pl.BlockSpec((tb, m), lambda i: (i, 0)), pl.BlockSpec((tb, n), lambda i: (i, 0))] + [vmem_spec] * n_small,
        out_specs=[pl.BlockSpec(memory_space=pl.ANY)] + [vmem_spec] * n_small,
        out_shape=[jax.ShapeDtypeStruct((m, n), BF16)] + [jax.ShapeDtypeStruct(s_.shape, F32) for s_ in small],
        scratch_shapes=[pltpu.VMEM((m, n), F32), pltpu.VMEM((m, n), BF16)]
        + (_small_allreduce_scratch(whole, by_rows) if n_small else []),
        compiler_params=_params(),
    )(a, b, *small)
    return res[0], res[1:1 + n_w], res[1 + n_w:]


def _in_bwd(dproj, x, dh1, g_mix, w_in, scatter, tb):
    t_len, d = x.shape
    nb = t_len // tb
    n_s = len(scatter['blocks'])
    scatter_args = _scatter_args(scatter)

    def body(*refs):
        dproj_ref, x_ref, dh1_ref, g_ref, win_ref = refs[:5]
        scatter_in = refs[5:5 + len(scatter_args)]
        dx_ref, vec_ref = refs[5 + len(scatter_args):7 + len(scatter_args)]
        recv = refs[7 + len(scatter_args):7 + len(scatter_args) + n_s]
        vec_acc, send_sems, recv_sems, local_sems, vec_land, small_send, small_recv = refs[7 + len(scatter_args) + n_s:]
        start_scatter, wait_scatter = _scatter_ops(scatter, scatter_in, recv, (send_sems, recv_sems, local_sems))
        send_partials, reduce_and_send_sums, finish_small = _small_allreduce(
            [vec_acc], [], [vec_ref], [], [vec_land], [], [], small_send, small_recv)
        i = pl.program_id(0)

        @pl.when(i == 0)
        def _():
            start_scatter()
            vec_acc[...] = jnp.zeros_like(vec_acc)

        dz1 = _dot_nt(dproj_ref[...], win_ref[...])
        g = g_ref[...]
        _, xh, rr = _rms_fwd(x_ref[...], g)
        dx_ref[...] = dh1_ref[...] + _rms_bwd(xh, rr, g, dz1)
        vec_acc[0:1, :] += _colsum(dz1 * xh)

        @pl.when(i == nb - 1)
        def _():
            send_partials()
            reduce_and_send_sums()
            finish_small()
            wait_scatter()

    row_spec = lambda w: pl.BlockSpec((tb, w), lambda i: (i, 0))
    any_spec = pl.BlockSpec(memory_space=pl.ANY)
    return pl.pallas_call(
        body, name="in_bwd", grid=(nb,),
        in_specs=[row_spec(dproj.shape[1]), row_spec(d), row_spec(d), _const_spec(g_mix.shape), _const_spec(w_in.shape)]
        + [any_spec] * len(scatter_args),
        out_specs=[row_spec(d), pl.BlockSpec((8, d), lambda i: (0, 0))] + [any_spec] * n_s,
        out_shape=[jax.ShapeDtypeStruct((t_len, d), F32), jax.ShapeDtypeStruct((8, d), F32)] + _scatter_out_shape(scatter),
        scratch_shapes=[pltpu.VMEM((8, d), F32)] + _exchange_scratch(n_s)
        + _small_allreduce_scratch([jax.ShapeDtypeStruct((8, d), F32)], []),
        input_output_aliases=_scatter_aliases(scatter, 5, 2), compiler_params=_params(),
    )(dproj, x, dh1, g_mix, w_in, *scatter_args)


def _small_allreduce(whole_in, rows_in, whole_out, rows_out, whole_land, rows_land, rows_sum, send_sems, recv_sems):
    n_w, n_r = len(whole_in), len(rows_in)
    per = [r.shape[0] // N_DEV for r in rows_in]
    me = _my_index()

    def dev(s):
        return (s // 4, (s // 2) % 2, s % 2)

    def rows_of(t, s):
        return pl.ds(s * per[t], per[t])

    def mine(t):
        return pl.ds(pl.multiple_of(me * per[t], 8), per[t])

    def partial(t, s, slot):
        if t < n_w:
            src, dst = whole_in[t], whole_land[t]
        else:
            src, dst = rows_in[t - n_w].at[rows_of(t - n_w, s)], rows_land[t - n_w]
        return pltpu.make_async_remote_copy(
            src_ref=src, dst_ref=dst.at[slot], send_sem=send_sems.at[t, s], recv_sem=recv_sems.at[t, slot],
            device_id=dev(s), device_id_type=MESH)

    def summed(t, s, rows, slot):
        return pltpu.make_async_remote_copy(
            src_ref=rows_sum[t].at[rows], dst_ref=rows_sum[t].at[rows], send_sem=send_sems.at[n_w + n_r + t, s],
            recv_sem=recv_sems.at[n_w + n_r + t, slot], device_id=dev(s), device_id_type=MESH)

    def send_partials():
        for s in range(N_DEV):
            @pl.when(s != me)
            def _():
                for t in range(n_w + n_r):
                    partial(t, s, me).start()
        for t in range(n_w):
            whole_land[t][me] = whole_in[t][...]
        for t in range(n_r):
            rows_land[t][me] = rows_in[t][mine(t), :]

    def reduce_and_send_sums():
        for s in range(N_DEV):
            @pl.when(s != me)
            def _():
                for t in range(n_w + n_r):
                    partial(t, s, s).wait_recv()
        for t in range(n_w):
            total = whole_land[t][0]
            for s in range(1, N_DEV):
                total = total + whole_land[t][s]
            whole_out[t][...] = total
        for t in range(n_r):
            total = rows_land[t][0]
            for s in range(1, N_DEV):
                total = total + rows_land[t][s]
            rows_sum[t][mine(t), :] = total
        for s in range(N_DEV):
            @pl.when(s != me)
            def _():
                for t in range(n_r):
                    summed(t, s, mine(t), me).start()

    def finish():
        for s in range(N_DEV):
            @pl.when(s != me)
            def _():
                for t in range(n_r):
                    summed(t, s, rows_of(t, s), s).wait_recv()
                    summed(t, s, mine(t), me).wait_send()
                for t in range(n_w + n_r):
                    partial(t, s, me).wait_send()
        for t in range(n_r):
            rows_out[t][...] = rows_sum[t][...]

    return send_partials, reduce_and_send_sums, finish


def _small_allreduce_scratch(whole, by_rows):
    n_sem = len(whole) + 2 * len(by_rows)
    return ([pltpu.VMEM((N_DEV,) + a.shape, F32) for a in whole]
            + [pltpu.VMEM((N_DEV, a.shape[0] // N_DEV, a.shape[1]), F32) for a in by_rows]
            + [pltpu.VMEM(a.shape, F32) for a in by_rows]
            + [pltpu.SemaphoreType.DMA((n_sem, N_DEV)), pltpu.SemaphoreType.DMA((n_sem, N_DEV))])


def _adam_update(g, w, m, v):
    m_new = ADAM_B1 * m + (1.0 - ADAM_B1) * g
    v_new = ADAM_B2 * v + (1.0 - ADAM_B2) * jnp.square(g)
    m_hat = m_new / (1.0 - ADAM_B1 ** ADAM_STEP)
    v_hat = v_new / (1.0 - ADAM_B2 ** ADAM_STEP)
    return -ADAM_LR * (m_hat / (jnp.sqrt(v_hat) + ADAM_EPS) + ADAM_WD * w), m_new, v_new


def _adamw_groups(name, groups):
    n = len(groups)

    def body(*refs):
        for k in range(n):
            g_ref, w_ref, m_ref, v_ref = refs[4 * k:4 * k + 4]
            g_out, d_out, m_out, v_out = refs[4 * n + 4 * k:4 * n + 4 * k + 4]
            g = g_ref[...]
            g_out[...] = g
            d_out[...], m_out[...], v_out[...] = _adam_update(g, w_ref[...], m_ref[...], v_ref[...])

    flat = [a for grp in groups for a in grp]
    out = pl.pallas_call(body, name=name,
                         out_shape=[jax.ShapeDtypeStruct(grp[0].shape, F32) for grp in groups for _ in range(4)])(*flat)
    return [out[4 * k:4 * k + 4] for k in range(n)]


def _adamw(name, parts, w, m, v, row_block):
    n_src, rows, cols = parts.shape
    rb = min(row_block, rows)

    def body(p_ref, w_ref, m_ref, v_ref, g_out, d_out, m_out, v_out):
        g = p_ref[0].astype(F32)
        for s in range(1, n_src):
            g = g + p_ref[s].astype(F32)
        g_out[...] = g
        d_out[...], m_out[...], v_out[...] = _adam_update(g, w_ref[...], m_ref[...], v_ref[...])

    spec = pl.BlockSpec((rb, cols), lambda i: (i, 0))
    return pl.pallas_call(
        body, name=name, grid=(rows // rb,),
        in_specs=[pl.BlockSpec((n_src, rb, cols), lambda i: (0, i, 0)), spec, spec, spec],
        out_specs=[spec] * 4, out_shape=[jax.ShapeDtypeStruct((rows, cols), F32)] * 4,
        compiler_params=pltpu.CompilerParams(dimension_semantics=("parallel",), vmem_limit_bytes=VMEM_LIMIT),
    )(parts, w, m, v)


def _block_diag(blocks):
    g, a, b = blocks.shape
    eye = jnp.eye(g, dtype=blocks.dtype)
    return (eye[:, None, :, None] * blocks[:, :, None, :]).reshape(g * a, g * b)


def kernel(x, p, norm_mix_g, w_in, pool_w, pool_b, pool_scale, conv_w, conv_b, gate_a_w, gate_a_b, gate_x_w, gate_x_b, lru_L, w_out, norm_mlp_g, w_up, w_down, norm_ple_g, w_ple_gate, b_ple_gate, w_ple_proj, norm_final_g, loss_target, m_norm_mix_g, m_w_in, m_pool_w, m_pool_b, m_pool_scale, m_conv_w, m_conv_b, m_gate_a_w, m_gate_a_b, m_gate_x_w, m_gate_x_b, m_lru_L, m_w_out, m_norm_mlp_g, m_w_up, m_w_down, m_norm_ple_g, m_w_ple_gate, m_b_ple_gate, m_w_ple_proj, m_norm_final_g, v_norm_mix_g, v_w_in, v_pool_w, v_pool_b, v_pool_scale, v_conv_w, v_conv_b, v_gate_a_w, v_gate_a_b, v_gate_x_w, v_gate_x_b, v_lru_L, v_w_out, v_norm_mlp_g, v_w_up, v_w_down, v_norm_ple_g, v_w_ple_gate, v_b_ple_gate, v_w_ple_proj, v_norm_final_g):
    t_len, d = x.shape[1], x.shape[2]
    tbs = {k: min(v, t_len) for k, v in TIME_BLOCKS.items()}
    me = _my_index()

    win_g, wout_g, convw_g = _gather("gather_mixer_weights", [w_in[0].astype(BF16), w_out[0].astype(BF16), conv_w[0]])
    w_in_f = jnp.transpose(win_g, (1, 0, 2)).reshape(d, -1)
    conv_w_f = jnp.transpose(convw_g, (1, 0, 2)).reshape(convw_g.shape[1], -1)
    wp_bd = _block_diag(pool_w[0]).astype(BF16)
    wg_bd = jnp.concatenate([_block_diag(gate_a_w[0]), _block_diag(gate_x_w[0])], axis=1).astype(BF16)
    gate_b2 = jnp.concatenate([gate_a_b.reshape(1, -1), gate_x_b.reshape(1, -1)], axis=1)
    mixer_small = (norm_mix_g, w_in_f, wp_bd, pool_b.reshape(1, -1), pool_scale, conv_w_f, conv_b, wg_bd, gate_b2, lru_L,
                   wout_g.reshape(-1, d))

    x2 = x[0]
    later = [w_up[0].astype(BF16), w_down[0].astype(BF16), w_ple_gate[0].astype(BF16), w_ple_proj[0].astype(BF16)]
    h1, z1, proj, hs, cat, lru_saved, wup_g, wdn_g, wgate_g, wproj_g = _mix_fwd(
        x2, *mixer_small, later, [_core_major_slot, _core_major_slot, None, None], GATHER_FORWARD_AT, tbs['mix_fwd'])
    w_down_f = wdn_g.reshape(-1, d)
    w_proj_f = jnp.transpose(wproj_g, (1, 0, 2)).reshape(wproj_g.shape[1], -1)
    h2, z2, up = _mlp_fwd(h1, norm_mlp_g, wup_g, w_down_f, tbs['mlp_fwd'])
    dh2, ple_vec, dw_gate, dw_proj = _ple(h2, p[0, 0], loss_target[0], norm_ple_g, wgate_g.reshape(-1, d), b_ple_gate,
                                          w_proj_f, norm_final_g.reshape(1, -1), tbs['ple'])
    everyone = list(range(N_DEV))
    n_proj = w_ple_proj.shape[2]
    dz2_0, dw_up_0, dw_down_0 = _mlp_bwd_part(
        0, MLP_BWD_SPLIT, dh2, z2, up, wup_g, w_down_f, None, h1, norm_mlp_g, _scatter_plan([], [], []), tbs['mlp_bwd'])
    half = N_DEV // MLP_BWD_SPLIT
    south = [_device_of_core_major_slot(k) for k in range(half)]
    north = [_device_of_core_major_slot(k) for k in range(half, N_DEV)]
    scatter = _scatter_plan(
        [dw_up_0, dw_down_0, dw_gate.reshape(N_DEV, -1, d), jnp.transpose(dw_proj.reshape(-1, N_DEV, n_proj), (1, 0, 2))],
        [south, south, everyone, everyone], [None, None, None, None])
    dh1, mlp_vec, dw_up_1, dw_down_1, recv_up, recv_down, recv_gate, recv_proj = _mlp_bwd_part(
        1, MLP_BWD_SPLIT, dh2, z2, up, wup_g, w_down_f, dz2_0, h1, norm_mlp_g, scatter, tbs['mlp_bwd'])
    dw_out, _, _ = _wgrad("wgrad_out", cat, dh1, [], [], tbs['wgrad_out'])
    scatter = _scatter_plan([dw_up_1, dw_down_1, dw_out.reshape(N_DEV, -1, d)], [north, north, everyone],
                            [recv_up, recv_down, None])
    dproj, v512, dpw, dga, dgx, recv_up, recv_down, recv_out = _mix_bwd(
        dh1, proj, hs, lru_saved, wp_bd, pool_b.reshape(1, -1), pool_scale, conv_w_f, wg_bd, lru_L, wout_g.reshape(-1, d),
        scatter, tbs['mix_bwd'])
    rows1024 = jnp.concatenate([jnp.zeros((1, d), F32), mlp_vec[0:1], ple_vec[1:2], ple_vec[0:1], ple_vec[2:4],
                                jnp.zeros((2, d), F32)], axis=0)
    dw_in, (rows1024, rows512), (g_pool_w, g_gate_a_w, g_gate_x_w) = _wgrad(
        "wgrad_in", z1, dproj, [rows1024, v512], [dpw, dga, dgx], tbs['wgrad_in'])
    n_in = w_in.shape[2]
    scatter = _scatter_plan([jnp.transpose(dw_in.reshape(d, N_DEV, n_in), (1, 0, 2))], [everyone], [None])
    dx, in_vec, recv_in = _in_bwd(dproj, x2, dh1, norm_mix_g, w_in_f, scatter, tbs['in_bwd'])
    rows1024 = jnp.concatenate([in_vec[0:1], rows1024[1:]], axis=0)
    received = [recv_in, recv_out, recv_up, recv_down, recv_gate, recv_proj]

    shard_w = [w_in[0], w_out[0], w_up[0], w_down[0], w_ple_gate[0], w_ple_proj[0]]
    shard_m = [m_w_in[0], m_w_out[0], m_w_up[0], m_w_down[0], m_w_ple_gate[0], m_w_ple_proj[0]]
    shard_v = [v_w_in[0], v_w_out[0], v_w_up[0], v_w_down[0], v_w_ple_gate[0], v_w_ple_proj[0]]
    names = ["w_in", "w_out", "w_up", "w_down", "w_ple_gate", "w_ple_proj"]
    res = {}
    for nm, parts, w_s, m_s, v_s in zip(names, received, shard_w, shard_m, shard_v):
        res[nm] = [r[None] for r in _adamw("adamw_" + nm, parts, w_s, m_s, v_s, ADAM_ROW_BLOCK)]

    def rows_of_1024(a, b, c, e, f):
        return jnp.concatenate([a, b, c, e, f.reshape(1, -1), jnp.zeros((3, d), F32)], axis=0)

    def rows_of_512(scale, bias, cb, lru, ga, gx):
        z = jnp.zeros((1, 512), F32)
        return jnp.concatenate([scale, bias.reshape(1, -1), cb, lru, z, z, z, z, ga.reshape(1, -1), gx.reshape(1, -1),
                                z, z, z, z, z, z], axis=0)

    n_conv = conv_w.shape[2]
    groups = [
        (rows1024, *[rows_of_1024(*t) for t in (
            (norm_mix_g, norm_mlp_g, norm_ple_g, b_ple_gate, norm_final_g),
            (m_norm_mix_g, m_norm_mlp_g, m_norm_ple_g, m_b_ple_gate, m_norm_final_g),
            (v_norm_mix_g, v_norm_mlp_g, v_norm_ple_g, v_b_ple_gate, v_norm_final_g))]),
        (rows512, *[rows_of_512(*t) for t in (
            (pool_scale, pool_b, conv_b, lru_L, gate_a_b, gate_x_b),
            (m_pool_scale, m_pool_b, m_conv_b, m_lru_L, m_gate_a_b, m_gate_x_b),
            (v_pool_scale, v_pool_b, v_conv_b, v_lru_L, v_gate_a_b, v_gate_x_b))]),
        (g_pool_w, *[a.reshape(-1, a.shape[-1]) for a in (pool_w, m_pool_w, v_pool_w)]),
        (g_gate_a_w, *[a.reshape(-1, a.shape[-1]) for a in (gate_a_w, m_gate_a_w, v_gate_a_w)]),
        (g_gate_x_w, *[a.reshape(-1, a.shape[-1]) for a in (gate_x_w, m_gate_x_w, v_gate_x_w)]),
        (lax.dynamic_slice_in_dim(rows512[4:8], me * n_conv, n_conv, axis=1), conv_w[0], m_conv_w[0], v_conv_w[0]),
    ]
    r1024, r512, r_pool, r_ga, r_gx, r_conv = _adamw_groups("adamw_small", groups)
    loss = rows1024[5, 0]
    for k, nm in enumerate(["norm_mix_g", "norm_mlp_g", "norm_ple_g", "b_ple_gate"]):
        res[nm] = [a[k:k + 1] for a in r1024]
    res["norm_final_g"] = [a[4] for a in r1024]
    res["pool_scale"] = [a[0:1] for a in r512]
    res["pool_b"] = [a[1:2].reshape(pool_b.shape) for a in r512]
    res["conv_b"] = [a[2:3] for a in r512]
    res["lru_L"] = [a[3:4] for a in r512]
    res["gate_a_b"] = [a[8:9].reshape(gate_a_b.shape) for a in r512]
    res["gate_x_b"] = [a[9:10].reshape(gate_x_b.shape) for a in r512]
    res["pool_w"] = [a.reshape(pool_w.shape) for a in r_pool]
    res["gate_a_w"] = [a.reshape(gate_a_w.shape) for a in r_ga]
    res["gate_x_w"] = [a.reshape(gate_x_w.shape) for a in r_gx]
    res["conv_w"] = [a[None] for a in r_conv]
    order = ["norm_mix_g", "w_in", "pool_w", "pool_b", "pool_scale", "conv_w", "conv_b", "gate_a_w", "gate_a_b",
             "gate_x_w", "gate_x_b", "lru_L", "w_out", "norm_mlp_g", "w_up", "w_down", "norm_ple_g", "w_ple_gate",
             "b_ple_gate", "w_ple_proj", "norm_final_g"]
    return (loss, dx[None], *[res[nm][kind] for kind in range(4) for nm in order])
```

```python
import jax
import jax.numpy as jnp
from jax import lax
from jax.experimental import pallas as pl
from jax.experimental.pallas import tpu as pltpu

F32 = jnp.float32
BF16 = jnp.bfloat16
MESH = pl.DeviceIdType.MESH

N_DEV = 8
RMS_EPS = 1e-6
LRU_C = 8.0
POOL_WINDOWS = (2, 4, 8, 16)
N_POOL_GROUPS = 4
LRU_HEADS = 8
HALO = 16
SUB = 8
GELU_C0 = 0.7978845608028654
GELU_C1 = 0.044715

ADAM_LR = 0.001
ADAM_B1 = 0.9
ADAM_B2 = 0.999
ADAM_EPS = 1e-08
ADAM_WD = 0.01
ADAM_STEP = 10

VMEM_LIMIT = 60 * 1024 * 1024
TIME_BLOCKS = dict(mix_fwd=512, mlp_fwd=512, ple=512, mlp_bwd=512, wgrad_out=1024, mix_bwd=512, wgrad_in=1024, in_bwd=512)
ADAM_ROW_BLOCK = 256
SCAN_UNROLL = 4
MLP_BWD_SPLIT = 2
GATHER_FORWARD_AT = (0.5, 0.875, 1.0, 1.0)


def _params(n_arbitrary=1):
    return pltpu.CompilerParams(dimension_semantics=("arbitrary",) * n_arbitrary, vmem_limit_bytes=VMEM_LIMIT)


def _dot(a, b):
    return jnp.dot(a, b, preferred_element_type=F32)


def _dot_nt(a, b):
    return lax.dot_general(a, b, (((1,), (1,)), ((), ())), preferred_element_type=F32)


def _dot_tn(a, b):
    return lax.dot_general(a, b, (((0,), (0,)), ((), ())), preferred_element_type=F32)


def _rms_fwd(x, g):
    r = lax.rsqrt(jnp.mean(x * x, axis=-1, keepdims=True) + RMS_EPS)
    xh = x * r
    return xh * g, xh, r


def _rms_bwd(xh, r, g, dz):
    dxh = dz * g
    return r * (dxh - xh * jnp.mean(dxh * xh, axis=-1, keepdims=True))


def _colsum(a):
    return jnp.sum(a, axis=0, keepdims=True)


def _sigmoid(a):
    return 0.5 * jnp.tanh(0.5 * a) + 0.5


def _gelu_parts(u):
    u2 = u * u
    th = jnp.tanh(GELU_C0 * (u + GELU_C1 * u * u2))
    gel = 0.5 * u * (1.0 + th)
    dgel = 0.5 * (1.0 + th) + 0.5 * u * (1.0 - th * th) * (GELU_C0 * (1.0 + 3.0 * GELU_C1 * u2))
    return gel, dgel


def _my_index():
    return 4 * lax.axis_index("x") + 2 * lax.axis_index("y") + lax.axis_index("c")


def _all_to_all(srcs_of, dsts, send_sems, recv_sems, local_sems, dests=None):
    n = len(dsts)
    me = _my_index()
    dests = [list(range(N_DEV))] * n if dests is None else dests

    def remote(t, s):
        return pltpu.make_async_remote_copy(
            src_ref=srcs_of[t](s), dst_ref=dsts[t].at[me], send_sem=send_sems.at[t, s], recv_sem=recv_sems.at[t, me],
            device_id=(s // 4, (s // 2) % 2, s % 2), device_id_type=MESH)

    def arrival(t, s):
        return pltpu.make_async_remote_copy(
            src_ref=srcs_of[t](dests[t][0]), dst_ref=dsts[t].at[s], send_sem=send_sems.at[t, s],
            recv_sem=recv_sems.at[t, s], device_id=(s // 4, (s // 2) % 2, s % 2), device_id_type=MESH)

    def local(t, s):
        return pltpu.make_async_copy(srcs_of[t](s), dsts[t].at[s], local_sems.at[t])

    def start():
        for s in range(N_DEV):
            to_s = [t for t in range(n) if s in dests[t]]

            @pl.when(s == me)
            def _():
                for t in to_s:
                    local(t, s).start()

            @pl.when(s != me)
            def _():
                for t in to_s:
                    remote(t, s).start()

    def wait():
        for s in range(N_DEV):
            to_s = [t for t in range(n) if s in dests[t]]

            @pl.when(s == me)
            def _():
                for t in to_s:
                    local(t, s).wait()
                    for src in range(N_DEV):
                        if src != s:
                            arrival(t, src).wait_recv()

            @pl.when(s != me)
            def _():
                for t in to_s:
                    remote(t, s).wait_send()

    return start, wait


N_GATHER_COPIES = 7


def _core_major_slot(dev):
    return 4 * dev[2] + 2 * dev[0] + dev[1]


def _device_of_core_major_slot(k):
    return (k % 4) * 2 + k // 4


def _two_level_gather(srcs, dsts, send_sems, recv_sems, local_sems, slots=None):
    n = len(dsts)
    x, y, c = lax.axis_index("x"), lax.axis_index("y"), lax.axis_index("c")
    me, sibling = (x, y, c), (x, y, 1 - c)
    chips = [(1 - x, y), (x, 1 - y), (1 - x, 1 - y)]

    def slot(t, dev):
        return 4 * dev[0] + 2 * dev[1] + dev[2] if slots is None or slots[t] is None else slots[t](dev)

    def copy(t, k, block, to, src=None):
        return pltpu.make_async_remote_copy(
            src_ref=dsts[t].at[slot(t, block)] if src is None else src, dst_ref=dsts[t].at[slot(t, block)],
            send_sem=send_sems.at[t, k], recv_sem=recv_sems.at[t, k], device_id=to, device_id_type=MESH)

    def local(t):
        return pltpu.make_async_copy(srcs[t], dsts[t].at[slot(t, me)], local_sems.at[t])

    def start():
        for t in range(n):
            local(t).start()
            for j, chip in enumerate(chips):
                copy(t, 1 + j, me, (*chip, c), src=srcs[t]).start()
            copy(t, 0, me, sibling, src=srcs[t]).start()

    def forward(t):
        for j, chip in enumerate(chips):
            copy(t, 1 + j, (*chip, c), me).wait_recv()
            copy(t, 4 + j, (*chip, c), sibling).start()

    def finish():
        for t in range(n):
            copy(t, 0, sibling, me).wait_recv()
            for j, chip in enumerate(chips):
                copy(t, 4 + j, (*chip, 1 - c), me).wait_recv()
            copy(t, 0, me, sibling, src=srcs[t]).wait_send()
            for j, chip in enumerate(chips):
                copy(t, 1 + j, me, (*chip, c), src=srcs[t]).wait_send()
                copy(t, 4 + j, (*chip, c), sibling).wait_send()
            local(t).wait()

    return start, forward, finish


def _hosted_gather(i, nb, forward_at, srcs, dsts, sems, slots=None):
    start, forward, finish = _two_level_gather(srcs, dsts, *sems, slots)

    def after_step():
        for t, f in enumerate(forward_at):
            @pl.when(i == min(nb - 1, int(f * nb)))
            def _():
                forward(t)

        @pl.when(i == nb - 1)
        def _():
            finish()

    return start, after_step


def _gather_scratch(n):
    return [pltpu.SemaphoreType.DMA((n, N_GATHER_COPIES)), pltpu.SemaphoreType.DMA((n, N_GATHER_COPIES)),
            pltpu.SemaphoreType.DMA((n,))]


def _gather(name, srcs):
    n = len(srcs)

    def body(*refs):
        start, forward, finish = _two_level_gather(refs[:n], refs[n:2 * n], *refs[2 * n:])
        start()
        for t in range(n):
            forward(t)
        finish()

    any_spec = pl.BlockSpec(memory_space=pl.ANY)
    return pl.pallas_call(
        body, name=name, in_specs=[any_spec] * n, out_specs=[any_spec] * n,
        out_shape=[jax.ShapeDtypeStruct((N_DEV,) + a.shape, a.dtype) for a in srcs], scratch_shapes=_gather_scratch(n),
    )(*srcs)


def _scatter_plan(blocks, dests, landing):
    return dict(blocks=list(blocks), dests=[list(dd) for dd in dests], landing=list(landing))


def _scatter_args(plan):
    return plan['blocks'] + [a for a in plan['landing'] if a is not None]


def _scatter_out_shape(plan):
    return [jax.ShapeDtypeStruct((N_DEV,) + b.shape[1:], b.dtype) for b in plan['blocks']]


def _scatter_aliases(plan, first_in, first_out):
    given = [t for t, a in enumerate(plan['landing']) if a is not None]
    return {first_in + len(plan['blocks']) + k: first_out + t for k, t in enumerate(given)}


def _scatter_ops(plan, in_refs, out_refs, sems):
    n = len(plan['blocks'])
    srcs_of = [(lambda s, r=in_refs[t], dd=plan['dests'][t]: r.at[dd.index(s)]) for t in range(n)]
    return _all_to_all(srcs_of, out_refs, *sems, dests=plan['dests'])


def _exchange_scratch(n):
    return [pltpu.SemaphoreType.DMA((n, N_DEV)), pltpu.SemaphoreType.DMA((n, N_DEV)), pltpu.SemaphoreType.DMA((n,))]


def _const_spec(shape):
    nd = len(shape)
    return pl.BlockSpec(shape, lambda i: (0,) * nd, pipeline_mode=pl.Buffered(1))


def _pool_windows(up_ext, n, forward):
    sh = (lambda k: k) if forward else (lambda k: n - k)
    s2 = up_ext + pltpu.roll(up_ext, sh(1), 0)
    t4 = s2[:, 128:]
    s4 = t4 + pltpu.roll(t4, sh(2), 0)
    t8 = s4[:, 128:]
    s8 = t8 + pltpu.roll(t8, sh(4), 0)
    t16 = s8[:, 128:]
    s16 = t16 + pltpu.roll(t16, sh(8), 0)
    return jnp.concatenate([s2[:, :128], s4[:, :128], s8[:, :128], s16], axis=1)


def _inv_count_head():
    t = jnp.arange(1, HALO + 1, dtype=F32)[:, None]
    return jnp.concatenate([jnp.broadcast_to(1.0 / jnp.minimum(t, float(w)), (HALO, 128)) for w in POOL_WINDOWS], axis=1)


def _scale_by_inv_count(v, is_first_block, inv_head):
    inv_row = jnp.concatenate([jnp.full((1, 128), 1.0 / w, F32) for w in POOL_WINDOWS], axis=1)
    head = v[0:HALO] * jnp.where(is_first_block, inv_head, inv_row)
    return jnp.concatenate([head, v[HALO:] * inv_row], axis=0)


def _lru_decay(r, a, c_l, first_row):
    a2 = a * a
    m2 = -jnp.tanh(c_l * r) * (a2 + 1.0)
    return a2, m2, jnp.where(first_row, 1.0, jnp.sqrt(m2))


def _log_sigmoid(v):
    return -(jnp.maximum(-v, 0.0) + jnp.log1p(jnp.exp(-jnp.abs(v))))


def _conv_fwd(ul_ext, cw, cb):
    return (cb + cw[3:4, :] * ul_ext + cw[2:3, :] * pltpu.roll(ul_ext, 1, 0)
            + cw[1:2, :] * pltpu.roll(ul_ext, 2, 0) + cw[0:1, :] * pltpu.roll(ul_ext, 3, 0))


def _mix_fwd(x, g_mix, w_in, wp_bd, pool_b, pool_scale, conv_w, conv_b, wg_bd, gate_b, lru_l, w_out, gather_srcs,
             gather_slots, forward_at, tb):
    t_len, d = x.shape
    nb = t_len // tb
    n_g = len(gather_srcs)

    def body(*refs):
        (x_ref, g_ref, win_ref, wp_ref, pb_ref, ps_ref, cw_ref, cb_ref, wg_ref, gb_ref, l_ref, wout_ref,
         invh_ref) = refs[:13]
        gsrc = refs[13:13 + n_g]
        h1_ref, z1_ref, proj_ref, hs_ref, cat_ref, lru_ref, dpool_ref = refs[13 + n_g:20 + n_g]
        gdst = refs[20 + n_g:20 + 2 * n_g]
        ext_ref, a_ref, b_ref, hc_ref, send_sems, recv_sems, local_sems = refs[20 + 2 * n_g:]
        i = pl.program_id(0)
        start_gather, after_step = _hosted_gather(i, nb, forward_at, gsrc, gdst, (send_sems, recv_sems, local_sems),
                                                  gather_slots)

        @pl.when(i == 0)
        def _():
            start_gather()
            ext_ref[0:HALO, :] = jnp.zeros((HALO, 1024), F32)
            hc_ref[...] = jnp.zeros_like(hc_ref)

        xv = x_ref[...]
        z, _, _ = _rms_fwd(xv, g_ref[...])
        zb = z.astype(BF16)
        z1_ref[...] = zb
        proj = _dot(zb, win_ref[...])
        proj_ref[...] = proj
        ext_ref[HALO:, :] = proj[:, 0:1024]
        ug = proj[:, 1024:1536]
        n = tb + HALO
        up_ext = ext_ref[:, 0:512]
        win = _pool_windows(up_ext, n, True)[HALO:]
        dpool = _scale_by_inv_count(win, i == 0, invh_ref[...]) - proj[:, 0:512]
        dpoolb = dpool.astype(BF16)
        dpool_ref[...] = dpoolb
        q = _dot(dpoolb, wp_ref[...]) + pb_ref[...]
        y_pool = q * ps_ref[...]
        xb = _conv_fwd(ext_ref[:, 512:1024], cw_ref[...], cb_ref[...])[HALO:]
        first_row = (i * tb + lax.broadcasted_iota(jnp.int32, (tb, 1), 0)) == 0
        c_l = LRU_C * _log_sigmoid(l_ref[...])
        gp = _dot(xb.astype(BF16), wg_ref[...]) + gb_ref[...]
        r = _sigmoid(gp[:, :512])
        ig = _sigmoid(gp[:, 512:])
        a = jnp.exp(c_l * r)
        _, _, mult = _lru_decay(r, a, c_l, first_row)
        lru_ref[:, 0:512] = xb
        lru_ref[:, 512:1024] = r
        lru_ref[:, 1024:1536] = ig
        lru_ref[:, 1536:2048] = a
        a_ref[...] = a
        b_ref[...] = mult * (ig * xb)
        row = lax.broadcasted_iota(jnp.int32, (SUB, 512), 0)

        def group(j, hprev):
            o = pl.multiple_of(j * SUB, SUB)
            a8 = a_ref[pl.ds(o, SUB), :]
            b8 = b_ref[pl.ds(o, SUB), :]
            for sh in (1, 2, 4):
                ash = jnp.where(row >= sh, pltpu.roll(a8, sh, 0), 1.0)
                bsh = jnp.where(row >= sh, pltpu.roll(b8, sh, 0), 0.0)
                b8 = a8 * bsh + b8
                a8 = a8 * ash
            h8 = a8 * hprev + b8
            hs_ref[pl.ds(o, SUB), :] = h8
            return jnp.broadcast_to(h8[SUB - 1:SUB, :], (SUB, 512))

        def trip(k, carry):
            for u in range(SCAN_UNROLL):
                carry = group(k * SCAN_UNROLL + u, carry)
            return carry

        hc_ref[...] = lax.fori_loop(0, tb // (SUB * SCAN_UNROLL), trip, hc_ref[...])
        gel, _ = _gelu_parts(ug)
        y_lru = hs_ref[...] * gel
        catb = jnp.concatenate([y_pool, y_lru], axis=1).astype(BF16)
        cat_ref[...] = catb
        h1_ref[...] = xv + _dot(catb, wout_ref[...])
        ext_ref[0:HALO, :] = ext_ref[tb:tb + HALO, :]

        after_step()

    row_spec = lambda w: pl.BlockSpec((tb, w), lambda i: (i, 0))
    any_spec = pl.BlockSpec(memory_space=pl.ANY)
    smalls = [g_mix, w_in, wp_bd, pool_b, pool_scale, conv_w, conv_b, wg_bd, gate_b, lru_l, w_out, _inv_count_head()]
    return pl.pallas_call(
        body, name="mix_fwd", grid=(nb,),
        in_specs=[row_spec(d)] + [_const_spec(s.shape) for s in smalls] + [any_spec] * n_g,
        out_specs=[row_spec(d), row_spec(d), row_spec(1536), row_spec(512), row_spec(1024), row_spec(2048), row_spec(512)]
        + [any_spec] * n_g,
        out_shape=[jax.ShapeDtypeStruct((t_len, d), F32), jax.ShapeDtypeStruct((t_len, d), BF16),
                   jax.ShapeDtypeStruct((t_len, 1536), F32), jax.ShapeDtypeStruct((t_len, 512), F32),
                   jax.ShapeDtypeStruct((t_len, 1024), BF16), jax.ShapeDtypeStruct((t_len, 2048), F32),
                   jax.ShapeDtypeStruct((t_len, 512), BF16)]
        + [jax.ShapeDtypeStruct((N_DEV,) + s.shape, s.dtype) for s in gather_srcs],
        scratch_shapes=[pltpu.VMEM((tb + HALO, 1024), F32), pltpu.VMEM((tb, 512), F32), pltpu.VMEM((tb, 512), F32),
                        pltpu.VMEM((SUB, 512), F32)] + _gather_scratch(n_g),
        compiler_params=_params(),
    )(x, *smalls, *gather_srcs)


def _mlp_fwd(h1, g_mlp, w_up, w_down, tb):
    t_len, d = h1.shape
    nb = t_len // tb
    n_chunk, _, fc = w_up.shape

    def body(h1_ref, g_ref, wup_ref, wdn_ref, h2_ref, z2_ref, up_ref):
        xv = h1_ref[...]
        z, _, _ = _rms_fwd(xv, g_ref[...])
        zb = z.astype(BF16)
        z2_ref[...] = zb
        acc = xv
        for c in range(n_chunk):
            u = _dot(zb, wup_ref[c])
            up_ref[:, c * fc:(c + 1) * fc] = u.astype(BF16)
            act = jnp.square(jnp.maximum(u, 0.0)).astype(BF16)
            acc = acc + _dot(act, wdn_ref[c * fc:(c + 1) * fc, :])
        h2_ref[...] = acc

    row_spec = lambda w: pl.BlockSpec((tb, w), lambda i: (i, 0))
    return pl.pallas_call(
        body, name="mlp_fwd", grid=(nb,),
        in_specs=[row_spec(d), _const_spec(g_mlp.shape), _const_spec(w_up.shape), _const_spec(w_down.shape)],
        out_specs=[row_spec(d), row_spec(d), row_spec(n_chunk * fc)],
        out_shape=[jax.ShapeDtypeStruct((t_len, d), F32), jax.ShapeDtypeStruct((t_len, d), BF16),
                   jax.ShapeDtypeStruct((t_len, n_chunk * fc), BF16)],
        compiler_params=_params(),
    )(h1, g_mlp, w_up, w_down)


def _ple(h2, p, target, g_ple, w_gate, b_gate, w_proj, g_final, tb):
    t_len, d = h2.shape
    nb = t_len // tb
    pd = p.shape[1]

    def body(h2_ref, p_ref, tgt_ref, g_ref, wg_ref, bg_ref, wp_ref, gf_ref,
             dh2_ref, vec_ref, dwg_out, dwp_out, dwg_acc, dwp_acc, dwg_stage, dwp_stage):
        i = pl.program_id(0)

        @pl.when(i == 0)
        def _():
            vec_ref[...] = jnp.zeros_like(vec_ref)
            dwg_acc[...] = jnp.zeros_like(dwg_acc)
            dwp_acc[...] = jnp.zeros_like(dwp_acc)

        h2 = h2_ref[...]
        g2 = g_ref[...]
        z3, xh2, r2 = _rms_fwd(h2, g2)
        z3b = z3.astype(BF16)
        gate = _sigmoid(_dot(z3b, wg_ref[...]) + bg_ref[...])
        pb = p_ref[...].astype(BF16)
        pp = _dot(pb, wp_ref[...])
        h3 = h2 + gate * pp
        gf = gf_ref[...]
        y, xh3, r3 = _rms_fwd(h3, gf)
        err = y - tgt_ref[...]
        loss_rows = jnp.mean(err * err, axis=-1, keepdims=True)
        dy = err * (1.0 / d)
        dh3 = _rms_bwd(xh3, r3, gf, dy)
        dgl = (dh3 * pp) * (gate * (1.0 - gate))
        dpp = dh3 * gate
        dglb = dgl.astype(BF16)
        dwg_acc[...] += _dot_tn(z3b, dglb)
        dwp_acc[...] += _dot_tn(pb, dpp.astype(BF16))
        dz3 = _dot_nt(dglb, wg_ref[...])
        dh2_ref[...] = dh3 + _rms_bwd(xh2, r2, g2, dz3)
        vec_ref[0:1, :] += _colsum(dgl)
        vec_ref[1:2, :] += _colsum(dz3 * xh2)
        vec_ref[2:3, :] += _colsum(dy * xh3)
        vec_ref[3:4, :] += 0.5 * jnp.sum(loss_rows)

        @pl.when(i == nb - 1)
        def _():
            dwg_stage[...] = dwg_acc[...].astype(BF16)
            dwp_stage[...] = dwp_acc[...].astype(BF16)
            pltpu.sync_copy(dwg_stage, dwg_out)
            pltpu.sync_copy(dwp_stage, dwp_out)

    row_spec = lambda w: pl.BlockSpec((tb, w), lambda i: (i, 0))
    any_spec = pl.BlockSpec(memory_space=pl.ANY)
    smalls = [g_ple, w_gate, b_gate, w_proj, g_final]
    return pl.pallas_call(
        body, name="ple_fwd_bwd", grid=(nb,),
        in_specs=[row_spec(d), row_spec(pd), row_spec(d)] + [_const_spec(s.shape) for s in smalls],
        out_specs=[row_spec(d), pl.BlockSpec((8, d), lambda i: (0, 0)), any_spec, any_spec],
        out_shape=[jax.ShapeDtypeStruct((t_len, d), F32), jax.ShapeDtypeStruct((8, d), F32),
                   jax.ShapeDtypeStruct(w_gate.shape, BF16), jax.ShapeDtypeStruct(w_proj.shape, BF16)],
        scratch_shapes=[pltpu.VMEM(w_gate.shape, F32), pltpu.VMEM(w_proj.shape, F32), pltpu.VMEM(w_gate.shape, BF16),
                        pltpu.VMEM(w_proj.shape, BF16)],
        compiler_params=_params(),
    )(h2, p, target, *smalls)


def _mlp_bwd_part(part, n_part, dh2, z2, up, w_up, w_down, dz2_prev, h1, g_mlp, scatter, tb):
    t_len, d = dh2.shape
    nb = t_len // tb
    n_chunk_all, _, fc = w_up.shape
    n_chunk = n_chunk_all // n_part
    first, last = part == 0, part == n_part - 1

    def body(*refs):
        refs = list(refs)
        dh2_ref, z2_ref, up_ref, wup_ref, wdn_ref = refs[:5]
        del refs[:5]
        dzp_ref = None if first else refs.pop(0)
        h1_ref, g_ref = (refs.pop(0), refs.pop(0)) if last else (None, None)
        scatter_in = [refs.pop(0) for _ in _scatter_args(scatter)]
        out_ref = refs.pop(0)
        vec_ref = refs.pop(0) if last else None
        dwup_out, dwdn_out = refs.pop(0), refs.pop(0)
        scatter_out = [refs.pop(0) for _ in scatter['blocks']]
        dwup_acc, dwdn_acc, up_stage, dn_stage = refs[:4]
        if scatter['blocks']:
            start_scatter, wait_scatter = _scatter_ops(scatter, scatter_in, scatter_out, refs[4:])
        i = pl.program_id(0)

        @pl.when(i == 0)
        def _():
            if scatter['blocks']:
                start_scatter()
            dwup_acc[...] = jnp.zeros_like(dwup_acc)
            dwdn_acc[...] = jnp.zeros_like(dwdn_acc)
            if last:
                vec_ref[...] = jnp.zeros_like(vec_ref)

        dh2 = dh2_ref[...]
        dh2b = dh2.astype(BF16)
        z2b = z2_ref[...]
        dz2 = jnp.zeros((tb, d), F32) if first else dzp_ref[...]
        for c in range(n_chunk):
            u = up_ref[:, c * fc:(c + 1) * fc].astype(F32)
            ur = jnp.maximum(u, 0.0)
            dact = _dot_nt(dh2b, wdn_ref[c * fc:(c + 1) * fc, :])
            dupb = (dact * (2.0 * ur)).astype(BF16)
            dwdn_acc[c * fc:(c + 1) * fc, :] += _dot_tn((ur * ur).astype(BF16), dh2b)
            dwup_acc[c] += _dot_tn(z2b, dupb)
            dz2 = dz2 + _dot_nt(dupb, wup_ref[c])
        if last:
            g = g_ref[...]
            _, xh, r = _rms_fwd(h1_ref[...], g)
            out_ref[...] = dh2 + _rms_bwd(xh, r, g, dz2)
            vec_ref[0:1, :] += _colsum(dz2 * xh)
        else:
            out_ref[...] = dz2

        @pl.when(i == nb - 1)
        def _():
            for c in range(n_chunk):
                up_stage[...] = dwup_acc[c].astype(BF16)
                dn_stage[...] = dwdn_acc[c * fc:(c + 1) * fc, :].astype(BF16)
                pltpu.sync_copy(up_stage, dwup_out.at[c])
                pltpu.sync_copy(dn_stage, dwdn_out.at[c])
            if scatter['blocks']:
                wait_scatter()

    row_spec = lambda w: pl.BlockSpec((tb, w), lambda i: (i, 0))
    any_spec = pl.BlockSpec(memory_space=pl.ANY)
    args = [dh2, z2, up, w_up, w_down]
    in_specs = [row_spec(d), row_spec(d), pl.BlockSpec((tb, n_chunk * fc), lambda i: (i, part)),
                pl.BlockSpec((n_chunk, d, fc), lambda i: (part, 0, 0), pipeline_mode=pl.Buffered(1)),
                pl.BlockSpec((n_chunk * fc, d), lambda i: (part, 0), pipeline_mode=pl.Buffered(1))]
    if not first:
        args.append(dz2_prev)
        in_specs.append(row_spec(d))
    if last:
        args += [h1, g_mlp]
        in_specs += [row_spec(d), _const_spec(g_mlp.shape)]
    n_in = len(args)
    args += _scatter_args(scatter)
    in_specs += [any_spec] * len(_scatter_args(scatter))
    out_specs = [row_spec(d)]
    out_shape = [jax.ShapeDtypeStruct((t_len, d), F32)]
    if last:
        out_specs.append(pl.BlockSpec((8, d), lambda i: (0, 0)))
        out_shape.append(jax.ShapeDtypeStruct((8, d), F32))
    out_specs += [any_spec, any_spec]
    out_shape += [jax.ShapeDtypeStruct((n_chunk, d, fc), BF16), jax.ShapeDtypeStruct((n_chunk, fc, d), BF16)]
    n_out = len(out_shape)
    out_specs += [any_spec] * len(scatter['blocks'])
    out_shape += _scatter_out_shape(scatter)
    return pl.pallas_call(
        body, name=f"mlp_bwd_{part}", grid=(nb,), in_specs=in_specs, out_specs=out_specs, out_shape=out_shape,
        scratch_shapes=[pltpu.VMEM((n_chunk, d, fc), F32), pltpu.VMEM((n_chunk * fc, d), F32),
                        pltpu.VMEM((d, fc), BF16), pltpu.VMEM((fc, d), BF16)]
        + (_exchange_scratch(len(scatter['blocks'])) if scatter['blocks'] else []),
        input_output_aliases=_scatter_aliases(scatter, n_in, n_out), compiler_params=_params(),
    )(*args)


def _mix_bwd(dh1, proj, hs, lru_saved, dpool_saved, wp_bd, pool_b, pool_scale, conv_w, wg_bd, lru_l, w_out, scatter, tb):
    t_len, d = dh1.shape
    nb = t_len // tb
    n_s = len(scatter['blocks'])
    scatter_args = _scatter_args(scatter)

    def body(*refs):
        refs = list(refs)
        (dh1_ref, ul_ref, ug_ref, hs_ref, hsh_ref, lru_ref, dpool_ref,
         wp_ref, pb_ref, ps_ref, cw_ref, wg_ref, l_ref, wout_ref, invh_ref) = refs[:15]
        del refs[:15]
        scatter_in = refs[:len(scatter_args)]
        del refs[:len(scatter_args)]
        dproj_ref, v512_ref, dpw_ref, dga_ref, dgx_ref = refs[:5]
        recv = refs[5:5 + n_s]
        (dwp_acc, dwg_acc, v1024_ref, b_ref, gs_ref, ehead_ref, dxbhead_ref, hc_ref,
         send_sems, recv_sems, local_sems) = refs[5 + n_s:]
        i = pl.program_id(0)
        tbk = nb - 1 - i

        start_scatter, wait_scatter = _scatter_ops(scatter, scatter_in, recv, (send_sems, recv_sems, local_sems))

        @pl.when(i == 0)
        def _():
            start_scatter()
            for ref in (v512_ref, v1024_ref, dwp_acc, dwg_acc, ehead_ref, dxbhead_ref, hc_ref):
                ref[...] = jnp.zeros_like(ref)

        dcat = _dot_nt(dh1_ref[...].astype(BF16), wout_ref[...])

        has_prev = (tbk > 0).astype(F32)
        ug = ug_ref[...]
        n = tb + HALO
        inv_head = invh_ref[...]

        dpoolb = dpool_ref[...]
        q = _dot(dpoolb, wp_ref[...]) + pb_ref[...]
        dyp = dcat[:, 0:512]
        dq = dyp * ps_ref[...]
        dqb = dq.astype(BF16)
        v512_ref[0:1, :] += _colsum(dyp * q)
        v512_ref[1:2, :] += _colsum(dq)
        dwp_acc[...] += _dot_tn(dpoolb, dqb)
        dd = _dot_nt(dqb, wp_ref[...])
        e = _scale_by_inv_count(dd, tbk == 0, inv_head)
        e_ext = jnp.concatenate([e, ehead_ref[...]], axis=0)
        du_pool = _pool_windows(e_ext, n, False)[0:tb] - dd
        ehead_ref[...] = e[0:HALO]

        gel, dgel = _gelu_parts(ug)
        hsv = hs_ref[...]
        dcl = dcat[:, 512:1024]
        dhs = dcl * gel
        dug = dcl * hsv * dgel
        cw = cw_ref[...]
        xb, r, ig, a = lru_ref[:, 0:512], lru_ref[:, 512:1024], lru_ref[:, 1024:1536], lru_ref[:, 1536:2048]
        first_row = (tbk * tb + lax.broadcasted_iota(jnp.int32, (tb, 1), 0)) == 0
        c_l = LRU_C * _log_sigmoid(l_ref[...])
        a2, m2, mult = _lru_decay(r, a, c_l, first_row)
        b_ref[...] = dhs
        row = lax.broadcasted_iota(jnp.int32, (SUB, 512), 0)

        def group(jj, hnext):
            o = pl.multiple_of((tb // SUB - 1 - jj) * SUB, SUB)
            a8 = lru_ref[pl.ds(o, SUB), 1536:2048]
            d8 = b_ref[pl.ds(o, SUB), :]
            b8 = a8 * d8
            for sh in (1, 2, 4):
                ash = jnp.where(row < SUB - sh, pltpu.roll(a8, SUB - sh, 0), 1.0)
                bsh = jnp.where(row < SUB - sh, pltpu.roll(b8, SUB - sh, 0), 0.0)
                b8 = a8 * bsh + b8
                a8 = a8 * ash
            h8 = a8 * hnext + b8
            gs_ref[pl.ds(o, SUB), :] = d8 + jnp.where(row < SUB - 1, pltpu.roll(h8, SUB - 1, 0), hnext)
            return jnp.broadcast_to(h8[0:1, :], (SUB, 512))

        def trip(k, carry):
            for u in range(SCAN_UNROLL):
                carry = group(k * SCAN_UNROLL + u, carry)
            return carry

        hc_ref[...] = lax.fori_loop(0, tb // (SUB * SCAN_UNROLL), trip, hc_ref[...])
        gsum = gs_ref[...]
        hs_ext = jnp.concatenate([hsh_ref[...] * has_prev, hsv], axis=0)
        hprev = pltpu.roll(hs_ext, 1, 0)[SUB:]
        da = gsum * hprev
        dmult = jnp.where(first_row, 0.0, gsum * (ig * xb))
        di = gsum * mult * xb
        dxb = gsum * mult * ig
        dla = da * a - dmult * a2 * lax.rsqrt(m2)
        dr = dla * c_l
        v512_ref[3:4, :] += _colsum(dla * r)
        dgp = jnp.concatenate([dr * r * (1.0 - r), di * ig * (1.0 - ig)], axis=1)
        dgpb = dgp.astype(BF16)
        v1024_ref[0:1, :] += _colsum(dgp)
        dwg_acc[...] += _dot_tn(xb.astype(BF16), dgpb)
        dxb = dxb + _dot_nt(dgpb, wg_ref[...])
        n8 = tb + SUB
        dxb_ext = jnp.concatenate([dxb, dxbhead_ref[...]], axis=0)
        ul = ul_ref[...]
        du_lru = cw[3:4, :] * dxb
        v512_ref[7:8, :] += _colsum(dxb * ul)
        for j in range(1, 4):
            ahead = pltpu.roll(dxb_ext, n8 - j, 0)[0:tb]
            du_lru = du_lru + cw[3 - j:4 - j, :] * ahead
            v512_ref[4 + (3 - j):5 + (3 - j), :] += _colsum(ahead * ul)
        dxbhead_ref[...] = dxb[0:SUB]
        v512_ref[2:3, :] += _colsum(dxb)

        dproj_ref[...] = jnp.concatenate([du_pool, du_lru, dug], axis=1).astype(BF16)

        @pl.when(i == nb - 1)
        def _():
            v512_ref[3:4, :] = v512_ref[3:4, :] * (LRU_C * _sigmoid(-l_ref[...]))
            v512_ref[8:9, :] = v1024_ref[0:1, 0:512]
            v512_ref[9:10, :] = v1024_ref[0:1, 512:1024]
            for g in range(N_POOL_GROUPS):
                dpw_ref[g * 128:(g + 1) * 128, :] = dwp_acc[g * 128:(g + 1) * 128, g * 128:(g + 1) * 128]
            odd_head = (lax.broadcasted_iota(jnp.int32, (512, 128), 0) // 64) % 2 == 1
            for out_ref, col0 in ((dga_ref, 0), (dgx_ref, 512)):
                pairs = jnp.concatenate([dwg_acc[128 * k:128 * (k + 1), col0 + 128 * k:col0 + 128 * (k + 1)]
                                         for k in range(LRU_HEADS // 2)], axis=0)
                out_ref[...] = jnp.where(odd_head, pltpu.roll(pairs, 64, 1), pairs)[:, 0:64]
            wait_scatter()

    rev = lambda w: pl.BlockSpec((tb, w), lambda i: (nb - 1 - i, 0))
    halo = lambda rows, w: pl.BlockSpec((rows, w), lambda i: (jnp.maximum((nb - 1 - i) * (tb // rows) - 1, 0), 0))
    any_spec = pl.BlockSpec(memory_space=pl.ANY)
    smalls = [wp_bd, pool_b, pool_scale, conv_w, wg_bd, lru_l, w_out, _inv_count_head()]
    third = lambda k: pl.BlockSpec((tb, 512), lambda i: (nb - 1 - i, k))
    return pl.pallas_call(
        body, name="mix_bwd", grid=(nb,),
        in_specs=[rev(d), third(1), third(2), rev(512), halo(SUB, 512), rev(2048), rev(512)]
        + [_const_spec(s.shape) for s in smalls] + [any_spec] * len(scatter_args),
        out_specs=[rev(1536), pl.BlockSpec((16, 512), lambda i: (0, 0)), pl.BlockSpec((512, 128), lambda i: (0, 0)),
                   pl.BlockSpec((512, 64), lambda i: (0, 0)), pl.BlockSpec((512, 64), lambda i: (0, 0))]
        + [any_spec] * n_s,
        out_shape=[jax.ShapeDtypeStruct((t_len, 1536), BF16), jax.ShapeDtypeStruct((16, 512), F32),
                   jax.ShapeDtypeStruct((512, 128), F32), jax.ShapeDtypeStruct((512, 64), F32),
                   jax.ShapeDtypeStruct((512, 64), F32)]
        + _scatter_out_shape(scatter),
        scratch_shapes=[pltpu.VMEM(wp_bd.shape, F32), pltpu.VMEM(wg_bd.shape, F32), pltpu.VMEM((8, 1024), F32),
                        pltpu.VMEM((tb, 512), F32), pltpu.VMEM((tb, 512), F32), pltpu.VMEM((HALO, 512), F32),
                        pltpu.VMEM((SUB, 512), F32), pltpu.VMEM((SUB, 512), F32)]
        + _exchange_scratch(n_s),
        input_output_aliases=_scatter_aliases(scatter, 7 + len(smalls), 5), compiler_params=_params(),
    )(dh1, proj, proj, hs, hs, lru_saved, dpool_saved, *smalls, *scatter_args)


def _wgrad(name, a, b, whole, by_rows, tb):
    t_len, m = a.shape
    n = b.shape[1]
    nb = t_len // tb
    n_w, n_r = len(whole), len(by_rows)
    n_small = n_w + n_r

    def body(*refs):
        a_ref, b_ref = refs[:2]
        small_in = refs[2:2 + n_small]
        out_ref = refs[2 + n_small]
        small_out = refs[3 + n_small:3 + 2 * n_small]
        acc_ref, stage_ref = refs[3 + 2 * n_small:5 + 2 * n_small]
        rest = refs[5 + 2 * n_small:]
        if n_small:
            send_partials, reduce_and_send_sums, finish_small = _small_allreduce(
                small_in[:n_w], small_in[n_w:], small_out[:n_w], small_out[n_w:], rest[:n_w], rest[n_w:n_small],
                rest[n_small:n_small + n_r], *rest[n_small + n_r:])
        i = pl.program_id(0)

        @pl.when(i == 0)
        def _():
            if n_small:
                send_partials()
            acc_ref[...] = jnp.zeros_like(acc_ref)

        acc_ref[...] += _dot_tn(a_ref[...], b_ref[...].astype(BF16))

        if n_small:
            @pl.when(i == nb // 2)
            def _():
                reduce_and_send_sums()

        @pl.when(i == nb - 1)
        def _():
            stage_ref[...] = acc_ref[...].astype(BF16)
            pltpu.sync_copy(stage_ref, out_ref)
            if n_small:
                finish_small()

    small = list(whole) + list(by_rows)
    vmem_spec = pl.BlockSpec(memory_space=pltpu.VMEM)
    res = pl.pallas_call(
        body, name=name, grid=(nb,),
        in_specs=[pl.BlockSpec((tb, m), lambda i: (i, 0)), pl.BlockSpec((tb, n), lambda i: (i, 0))] + [vmem_spec] * n_small,
        out_specs=[pl.BlockSpec(memory_space=pl.ANY)] + [vmem_spec] * n_small,
        out_shape=[jax.ShapeDtypeStruct((m, n), BF16)] + [jax.ShapeDtypeStruct(s_.shape, F32) for s_ in small],
        scratch_shapes=[pltpu.VMEM((m, n), F32), pltpu.VMEM((m, n), BF16)]
        + (_small_allreduce_scratch(whole, by_rows) if n_small else []),
        compiler_params=_params(),
    )(a, b, *small)
    return res[0], res[1:1 + n_w], res[1 + n_w:]


def _in_bwd(dproj, x, dh1, g_mix, w_in, scatter, tb):
    t_len, d = x.shape
    nb = t_len // tb
    n_s = len(scatter['blocks'])
    scatter_args = _scatter_args(scatter)

    def body(*refs):
        dproj_ref, x_ref, dh1_ref, g_ref, win_ref = refs[:5]
        scatter_in = refs[5:5 + len(scatter_args)]
        dx_ref, vec_ref = refs[5 + len(scatter_args):7 + len(scatter_args)]
        recv = refs[7 + len(scatter_args):7 + len(scatter_args) + n_s]
        vec_acc, send_sems, recv_sems, local_sems, vec_land, small_send, small_recv = refs[7 + len(scatter_args) + n_s:]
        start_scatter, wait_scatter = _scatter_ops(scatter, scatter_in, recv, (send_sems, recv_sems, local_sems))
        send_partials, reduce_and_send_sums, finish_small = _small_allreduce(
            [vec_acc], [], [vec_ref], [], [vec_land], [], [], small_send, small_recv)
        i = pl.program_id(0)

        @pl.when(i == 0)
        def _():
            start_scatter()
            vec_acc[...] = jnp.zeros_like(vec_acc)

        dz1 = _dot_nt(dproj_ref[...], win_ref[...])
        g = g_ref[...]
        _, xh, rr = _rms_fwd(x_ref[...], g)
        dx_ref[...] = dh1_ref[...] + _rms_bwd(xh, rr, g, dz1)
        vec_acc[0:1, :] += _colsum(dz1 * xh)

        @pl.when(i == nb - 1)
        def _():
            send_partials()
            reduce_and_send_sums()
            finish_small()
            wait_scatter()

    row_spec = lambda w: pl.BlockSpec((tb, w), lambda i: (i, 0))
    any_spec = pl.BlockSpec(memory_space=pl.ANY)
    return pl.pallas_call(
        body, name="in_bwd", grid=(nb,),
        in_specs=[row_spec(dproj.shape[1]), row_spec(d), row_spec(d), _const_spec(g_mix.shape), _const_spec(w_in.shape)]
        + [any_spec] * len(scatter_args),
        out_specs=[row_spec(d), pl.BlockSpec((8, d), lambda i: (0, 0))] + [any_spec] * n_s,
        out_shape=[jax.ShapeDtypeStruct((t_len, d), F32), jax.ShapeDtypeStruct((8, d), F32)] + _scatter_out_shape(scatter),
        scratch_shapes=[pltpu.VMEM((8, d), F32)] + _exchange_scratch(n_s)
        + _small_allreduce_scratch([jax.ShapeDtypeStruct((8, d), F32)], []),
        input_output_aliases=_scatter_aliases(scatter, 5, 2), compiler_params=_params(),
    )(dproj, x, dh1, g_mix, w_in, *scatter_args)


def _small_allreduce(whole_in, rows_in, whole_out, rows_out, whole_land, rows_land, rows_sum, send_sems, recv_sems):
    n_w, n_r = len(whole_in), len(rows_in)
    per = [r.shape[0] // N_DEV for r in rows_in]
    me = _my_index()

    def dev(s):
        return (s // 4, (s // 2) % 2, s % 2)

    def rows_of(t, s):
        return pl.ds(s * per[t], per[t])

    def mine(t):
        return pl.ds(pl.multiple_of(me * per[t], 8), per[t])

    def partial(t, s, slot):
        if t < n_w:
            src, dst = whole_in[t], whole_land[t]
        else:
            src, dst = rows_in[t - n_w].at[rows_of(t - n_w, s)], rows_land[t - n_w]
        return pltpu.make_async_remote_copy(
            src_ref=src, dst_ref=dst.at[slot], send_sem=send_sems.at[t, s], recv_sem=recv_sems.at[t, slot],
            device_id=dev(s), device_id_type=MESH)

    def summed(t, s, rows, slot):
        return pltpu.make_async_remote_copy(
            src_ref=rows_sum[t].at[rows], dst_ref=rows_sum[t].at[rows], send_sem=send_sems.at[n_w + n_r + t, s],
            recv_sem=recv_sems.at[n_w + n_r + t, slot], device_id=dev(s), device_id_type=MESH)

    def send_partials():
        for s in range(N_DEV):
            @pl.when(s != me)
            def _():
                for t in range(n_w + n_r):
                    partial(t, s, me).start()
        for t in range(n_w):
            whole_land[t][me] = whole_in[t][...]
        for t in range(n_r):
            rows_land[t][me] = rows_in[t][mine(t), :]

    def reduce_and_send_sums():
        for s in range(N_DEV):
            @pl.when(s != me)
            def _():
                for t in range(n_w + n_r):
                    partial(t, s, s).wait_recv()
        for t in range(n_w):
            total = whole_land[t][0]
            for s in range(1, N_DEV):
                total = total + whole_land[t][s]
            whole_out[t][...] = total
        for t in range(n_r):
            total = rows_land[t][0]
            for s in range(1, N_DEV):
                total = total + rows_land[t][s]
            rows_sum[t][mine(t), :] = total
        for s in range(N_DEV):
            @pl.when(s != me)
            def _():
                for t in range(n_r):
                    summed(t, s, mine(t), me).start()

    def finish():
        for s in range(N_DEV):
            @pl.when(s != me)
            def _():
                for t in range(n_r):
                    summed(t, s, rows_of(t, s), s).wait_recv()
                    summed(t, s, mine(t), me).wait_send()
                for t in range(n_w + n_r):
                    partial(t, s, me).wait_send()
        for t in range(n_r):
            rows_out[t][...] = rows_sum[t][...]

    return send_partials, reduce_and_send_sums, finish


def _small_allreduce_scratch(whole, by_rows):
    n_sem = len(whole) + 2 * len(by_rows)
    return ([pltpu.VMEM((N_DEV,) + a.shape, F32) for a in whole]
            + [pltpu.VMEM((N_DEV, a.shape[0] // N_DEV, a.shape[1]), F32) for a in by_rows]
            + [pltpu.VMEM(a.shape, F32) for a in by_rows]
            + [pltpu.SemaphoreType.DMA((n_sem, N_DEV)), pltpu.SemaphoreType.DMA((n_sem, N_DEV))])


def _adam_update(g, w, m, v):
    m_new = ADAM_B1 * m + (1.0 - ADAM_B1) * g
    v_new = ADAM_B2 * v + (1.0 - ADAM_B2) * jnp.square(g)
    m_hat = m_new / (1.0 - ADAM_B1 ** ADAM_STEP)
    v_hat = v_new / (1.0 - ADAM_B2 ** ADAM_STEP)
    return -ADAM_LR * (m_hat / (jnp.sqrt(v_hat) + ADAM_EPS) + ADAM_WD * w), m_new, v_new


def _adamw_groups(name, groups):
    n = len(groups)

    def body(*refs):
        for k in range(n):
            g_ref, w_ref, m_ref, v_ref = refs[4 * k:4 * k + 4]
            g_out, d_out, m_out, v_out = refs[4 * n + 4 * k:4 * n + 4 * k + 4]
            g = g_ref[...]
            g_out[...] = g
            d_out[...], m_out[...], v_out[...] = _adam_update(g, w_ref[...], m_ref[...], v_ref[...])

    flat = [a for grp in groups for a in grp]
    out = pl.pallas_call(body, name=name,
                         out_shape=[jax.ShapeDtypeStruct(grp[0].shape, F32) for grp in groups for _ in range(4)])(*flat)
    return [out[4 * k:4 * k + 4] for k in range(n)]


def _adamw(name, parts, w, m, v, row_block):
    n_src, rows, cols = parts.shape
    rb = min(row_block, rows)

    def body(p_ref, w_ref, m_ref, v_ref, g_out, d_out, m_out, v_out):
        g = p_ref[0].astype(F32)
        for s in range(1, n_src):
            g = g + p_ref[s].astype(F32)
        g_out[...] = g
        d_out[...], m_out[...], v_out[...] = _adam_update(g, w_ref[...], m_ref[...], v_ref[...])

    spec = pl.BlockSpec((rb, cols), lambda i: (i, 0))
    return pl.pallas_call(
        body, name=name, grid=(rows // rb,),
        in_specs=[pl.BlockSpec((n_src, rb, cols), lambda i: (0, i, 0)), spec, spec, spec],
        out_specs=[spec] * 4, out_shape=[jax.ShapeDtypeStruct((rows, cols), F32)] * 4,
        compiler_params=pltpu.CompilerParams(dimension_semantics=("parallel",), vmem_limit_bytes=VMEM_LIMIT),
    )(parts, w, m, v)


def _block_diag(blocks):
    g, a, b = blocks.shape
    eye = jnp.eye(g, dtype=blocks.dtype)
    return (eye[:, None, :, None] * blocks[:, :, None, :]).reshape(g * a, g * b)


def kernel(x, p, norm_mix_g, w_in, pool_w, pool_b, pool_scale, conv_w, conv_b, gate_a_w, gate_a_b, gate_x_w, gate_x_b, lru_L, w_out, norm_mlp_g, w_up, w_down, norm_ple_g, w_ple_gate, b_ple_gate, w_ple_proj, norm_final_g, loss_target, m_norm_mix_g, m_w_in, m_pool_w, m_pool_b, m_pool_scale, m_conv_w, m_conv_b, m_gate_a_w, m_gate_a_b, m_gate_x_w, m_gate_x_b, m_lru_L, m_w_out, m_norm_mlp_g, m_w_up, m_w_down, m_norm_ple_g, m_w_ple_gate, m_b_ple_gate, m_w_ple_proj, m_norm_final_g, v_norm_mix_g, v_w_in, v_pool_w, v_pool_b, v_pool_scale, v_conv_w, v_conv_b, v_gate_a_w, v_gate_a_b, v_gate_x_w, v_gate_x_b, v_lru_L, v_w_out, v_norm_mlp_g, v_w_up, v_w_down, v_norm_ple_g, v_w_ple_gate, v_b_ple_gate, v_w_ple_proj, v_norm_final_g):
    t_len, d = x.shape[1], x.shape[2]
    tbs = {k: min(v, t_len) for k, v in TIME_BLOCKS.items()}
    me = _my_index()

    win_g, wout_g, convw_g = _gather("gather_mixer_weights", [w_in[0].astype(BF16), w_out[0].astype(BF16), conv_w[0]])
    w_in_f = jnp.transpose(win_g, (1, 0, 2)).reshape(d, -1)
    conv_w_f = jnp.transpose(convw_g, (1, 0, 2)).reshape(convw_g.shape[1], -1)
    wp_bd = _block_diag(pool_w[0]).astype(BF16)
    wg_bd = jnp.concatenate([_block_diag(gate_a_w[0]), _block_diag(gate_x_w[0])], axis=1).astype(BF16)
    gate_b2 = jnp.concatenate([gate_a_b.reshape(1, -1), gate_x_b.reshape(1, -1)], axis=1)
    mixer_small = (norm_mix_g, w_in_f, wp_bd, pool_b.reshape(1, -1), pool_scale, conv_w_f, conv_b, wg_bd, gate_b2, lru_L,
                   wout_g.reshape(-1, d))

    x2 = x[0]
    later = [w_up[0].astype(BF16), w_down[0].astype(BF16), w_ple_gate[0].astype(BF16), w_ple_proj[0].astype(BF16)]
    h1, z1, proj, hs, cat, lru_saved, dpool_saved, wup_g, wdn_g, wgate_g, wproj_g = _mix_fwd(
        x2, *mixer_small, later, [_core_major_slot, _core_major_slot, None, None], GATHER_FORWARD_AT, tbs['mix_fwd'])
    w_down_f = wdn_g.reshape(-1, d)
    w_proj_f = jnp.transpose(wproj_g, (1, 0, 2)).reshape(wproj_g.shape[1], -1)
    h2, z2, up = _mlp_fwd(h1, norm_mlp_g, wup_g, w_down_f, tbs['mlp_fwd'])
    dh2, ple_vec, dw_gate, dw_proj = _ple(h2, p[0, 0], loss_target[0], norm_ple_g, wgate_g.reshape(-1, d), b_ple_gate,
                                          w_proj_f, norm_final_g.reshape(1, -1), tbs['ple'])
    everyone = list(range(N_DEV))
    n_proj = w_ple_proj.shape[2]
    dz2_0, dw_up_0, dw_down_0 = _mlp_bwd_part(
        0, MLP_BWD_SPLIT, dh2, z2, up, wup_g, w_down_f, None, h1, norm_mlp_g, _scatter_plan([], [], []), tbs['mlp_bwd'])
    half = N_DEV // MLP_BWD_SPLIT
    south = [_device_of_core_major_slot(k) for k in range(half)]
    north = [_device_of_core_major_slot(k) for k in range(half, N_DEV)]
    scatter = _scatter_plan(
        [dw_up_0, dw_down_0, dw_gate.reshape(N_DEV, -1, d), jnp.transpose(dw_proj.reshape(-1, N_DEV, n_proj), (1, 0, 2))],
        [south, south, everyone, everyone], [None, None, None, None])
    dh1, mlp_vec, dw_up_1, dw_down_1, recv_up, recv_down, recv_gate, recv_proj = _mlp_bwd_part(
        1, MLP_BWD_SPLIT, dh2, z2, up, wup_g, w_down_f, dz2_0, h1, norm_mlp_g, scatter, tbs['mlp_bwd'])
    dw_out, _, _ = _wgrad("wgrad_out", cat, dh1, [], [], tbs['wgrad_out'])
    scatter = _scatter_plan([dw_up_1, dw_down_1, dw_out.reshape(N_DEV, -1, d)], [north, north, everyone],
                            [recv_up, recv_down, None])
    dproj, v512, dpw, dga, dgx, recv_up, recv_down, recv_out = _mix_bwd(
        dh1, proj, hs, lru_saved, dpool_saved, wp_bd, pool_b.reshape(1, -1), pool_scale, conv_w_f, wg_bd, lru_L, wout_g.reshape(-1, d),
        scatter, tbs['mix_bwd'])
    rows1024 = jnp.concatenate([jnp.zeros((1, d), F32), mlp_vec[0:1], ple_vec[1:2], ple_vec[0:1], ple_vec[2:4],
                                jnp.zeros((2, d), F32)], axis=0)
    dw_in, (rows1024, rows512), (g_pool_w, g_gate_a_w, g_gate_x_w) = _wgrad(
        "wgrad_in", z1, dproj, [rows1024, v512], [dpw, dga, dgx], tbs['wgrad_in'])
    n_in = w_in.shape[2]
    scatter = _scatter_plan([jnp.transpose(dw_in.reshape(d, N_DEV, n_in), (1, 0, 2))], [everyone], [None])
    dx, in_vec, recv_in = _in_bwd(dproj, x2, dh1, norm_mix_g, w_in_f, scatter, tbs['in_bwd'])
    rows1024 = jnp.concatenate([in_vec[0:1], rows1024[1:]], axis=0)
    received = [recv_in, recv_out, recv_up, recv_down, recv_gate, recv_proj]

    shard_w = [w_in[0], w_out[0], w_up[0], w_down[0], w_ple_gate[0], w_ple_proj[0]]
    shard_m = [m_w_in[0], m_w_out[0], m_w_up[0], m_w_down[0], m_w_ple_gate[0], m_w_ple_proj[0]]
    shard_v = [v_w_in[0], v_w_out[0], v_w_up[0], v_w_down[0], v_w_ple_gate[0], v_w_ple_proj[0]]
    names = ["w_in", "w_out", "w_up", "w_down", "w_ple_gate", "w_ple_proj"]
    res = {}
    for nm, parts, w_s, m_s, v_s in zip(names, received, shard_w, shard_m, shard_v):
        res[nm] = [r[None] for r in _adamw("adamw_" + nm, parts, w_s, m_s, v_s, ADAM_ROW_BLOCK)]

    def rows_of_1024(a, b, c, e, f):
        return jnp.concatenate([a, b, c, e, f.reshape(1, -1), jnp.zeros((3, d), F32)], axis=0)

    def rows_of_512(scale, bias, cb, lru, ga, gx):
        z = jnp.zeros((1, 512), F32)
        return jnp.concatenate([scale, bias.reshape(1, -1), cb, lru, z, z, z, z, ga.reshape(1, -1), gx.reshape(1, -1),
                                z, z, z, z, z, z], axis=0)

    n_conv = conv_w.shape[2]
    groups = [
        (rows1024, *[rows_of_1024(*t) for t in (
            (norm_mix_g, norm_mlp_g, norm_ple_g, b_ple_gate, norm_final_g),
            (m_norm_mix_g, m_norm_mlp_g, m_norm_ple_g, m_b_ple_gate, m_norm_final_g),
            (v_norm_mix_g, v_norm_mlp_g, v_norm_ple_g, v_b_ple_gate, v_norm_final_g))]),
        (rows512, *[rows_of_512(*t) for t in (
            (pool_scale, pool_b, conv_b, lru_L, gate_a_b, gate_x_b),
            (m_pool_scale, m_pool_b, m_conv_b, m_lru_L, m_gate_a_b, m_gate_x_b),
            (v_pool_scale, v_pool_b, v_conv_b, v_lru_L, v_gate_a_b, v_gate_x_b))]),
        (g_pool_w, *[a.reshape(-1, a.shape[-1]) for a in (pool_w, m_pool_w, v_pool_w)]),
        (g_gate_a_w, *[a.reshape(-1, a.shape[-1]) for a in (gate_a_w, m_gate_a_w, v_gate_a_w)]),
        (g_gate_x_w, *[a.reshape(-1, a.shape[-1]) for a in (gate_x_w, m_gate_x_w, v_gate_x_w)]),
        (lax.dynamic_slice_in_dim(rows512[4:8], me * n_conv, n_conv, axis=1), conv_w[0], m_conv_w[0], v_conv_w[0]),
    ]
    r1024, r512, r_pool, r_ga, r_gx, r_conv = _adamw_groups("adamw_small", groups)
    loss = rows1024[5, 0]
    for k, nm in enumerate(["norm_mix_g", "norm_mlp_g", "norm_ple_g", "b_ple_gate"]):
        res[nm] = [a[k:k + 1] for a in r1024]
    res["norm_final_g"] = [a[4] for a in r1024]
    res["pool_scale"] = [a[0:1] for a in r512]
    res["pool_b"] = [a[1:2].reshape(pool_b.shape) for a in r512]
    res["conv_b"] = [a[2:3] for a in r512]
    res["lru_L"] = [a[3:4] for a in r512]
    res["gate_a_b"] = [a[8:9].reshape(gate_a_b.shape) for a in r512]
    res["gate_x_b"] = [a[9:10].reshape(gate_x_b.shape) for a in r512]
    res["pool_w"] = [a.reshape(pool_w.shape) for a in r_pool]
    res["gate_a_w"] = [a.reshape(gate_a_w.shape) for a in r_ga]
    res["gate_x_w"] = [a.reshape(gate_x_w.shape) for a in r_gx]
    res["conv_w"] = [a[None] for a in r_conv]
    order = ["norm_mix_g", "w_in", "pool_w", "pool_b", "pool_scale", "conv_w", "conv_b", "gate_a_w", "gate_a_b",
             "gate_x_w", "gate_x_b", "lru_L", "w_out", "norm_mlp_g", "w_up", "w_down", "norm_ple_g", "w_ple_gate",
             "b_ple_gate", "w_ple_proj", "norm_final_g"]
    return (loss, dx[None], *[res[nm][kind] for kind in range(4) for nm in order])
```

```python
import jax
import jax.numpy as jnp
from jax import lax
from jax.experimental import pallas as pl
from jax.experimental.pallas import tpu as pltpu

F32 = jnp.float32
BF16 = jnp.bfloat16
MESH = pl.DeviceIdType.MESH

N_DEV = 8
RMS_EPS = 1e-6
LRU_C = 8.0
POOL_WINDOWS = (2, 4, 8, 16)
N_POOL_GROUPS = 4
LRU_HEADS = 8
HALO = 16
SUB = 8
GELU_C0 = 0.7978845608028654
GELU_C1 = 0.044715

ADAM_LR = 0.001
ADAM_B1 = 0.9
ADAM_B2 = 0.999
ADAM_EPS = 1e-08
ADAM_WD = 0.01
ADAM_STEP = 10

VMEM_LIMIT = 60 * 1024 * 1024
TIME_BLOCKS = dict(mix_fwd=512, mlp_fwd=512, ple=512, mlp_bwd=512, wgrad_out=1024, mix_bwd=512, wgrad_in=1024, in_bwd=512)
ADAM_ROW_BLOCK = 256
SCAN_UNROLL = 4
MLP_BWD_SPLIT = 2
GATHER_FORWARD_AT = (0.5, 0.875, 1.0, 1.0)


def _params(n_arbitrary=1):
    return pltpu.CompilerParams(dimension_semantics=("arbitrary",) * n_arbitrary, vmem_limit_bytes=VMEM_LIMIT)


def _dot(a, b):
    return jnp.dot(a, b, preferred_element_type=F32)


def _dot_nt(a, b):
    return lax.dot_general(a, b, (((1,), (1,)), ((), ())), preferred_element_type=F32)


def _dot_tn(a, b):
    return lax.dot_general(a, b, (((0,), (0,)), ((), ())), preferred_element_type=F32)


def _rms_fwd(x, g):
    r = lax.rsqrt(jnp.mean(x * x, axis=-1, keepdims=True) + RMS_EPS)
    xh = x * r
    return xh * g, xh, r


def _rms_bwd(xh, r, g, dz):
    dxh = dz * g
    return r * (dxh - xh * jnp.mean(dxh * xh, axis=-1, keepdims=True))


def _colsum(a):
    return jnp.sum(a, axis=0, keepdims=True)


def _sigmoid(a):
    return 0.5 * jnp.tanh(0.5 * a) + 0.5


def _gelu_parts(u):
    u2 = u * u
    th = jnp.tanh(GELU_C0 * (u + GELU_C1 * u * u2))
    gel = 0.5 * u * (1.0 + th)
    dgel = 0.5 * (1.0 + th) + 0.5 * u * (1.0 - th * th) * (GELU_C0 * (1.0 + 3.0 * GELU_C1 * u2))
    return gel, dgel


def _my_index():
    return 4 * lax.axis_index("x") + 2 * lax.axis_index("y") + lax.axis_index("c")


def _all_to_all(srcs_of, dsts, send_sems, recv_sems, local_sems, dests=None):
    n = len(dsts)
    me = _my_index()
    dests = [list(range(N_DEV))] * n if dests is None else dests

    def remote(t, s):
        return pltpu.make_async_remote_copy(
            src_ref=srcs_of[t](s), dst_ref=dsts[t].at[me], send_sem=send_sems.at[t, s], recv_sem=recv_sems.at[t, me],
            device_id=(s // 4, (s // 2) % 2, s % 2), device_id_type=MESH)

    def arrival(t, s):
        return pltpu.make_async_remote_copy(
            src_ref=srcs_of[t](dests[t][0]), dst_ref=dsts[t].at[s], send_sem=send_sems.at[t, s],
            recv_sem=recv_sems.at[t, s], device_id=(s // 4, (s // 2) % 2, s % 2), device_id_type=MESH)

    def local(t, s):
        return pltpu.make_async_copy(srcs_of[t](s), dsts[t].at[s], local_sems.at[t])

    def start():
        for s in range(N_DEV):
            to_s = [t for t in range(n) if s in dests[t]]

            @pl.when(s == me)
            def _():
                for t in to_s:
                    local(t, s).start()

            @pl.when(s != me)
            def _():
                for t in to_s:
                    remote(t, s).start()

    def wait():
        for s in range(N_DEV):
            to_s = [t for t in range(n) if s in dests[t]]

            @pl.when(s == me)
            def _():
                for t in to_s:
                    local(t, s).wait()
                    for src in range(N_DEV):
                        if src != s:
                            arrival(t, src).wait_recv()

            @pl.when(s != me)
            def _():
                for t in to_s:
                    remote(t, s).wait_send()

    return start, wait


N_GATHER_COPIES = 7


def _core_major_slot(dev):
    return 4 * dev[2] + 2 * dev[0] + dev[1]


def _device_of_core_major_slot(k):
    return (k % 4) * 2 + k // 4


def _two_level_gather(srcs, dsts, send_sems, recv_sems, local_sems, slots=None):
    n = len(dsts)
    x, y, c = lax.axis_index("x"), lax.axis_index("y"), lax.axis_index("c")
    me, sibling = (x, y, c), (x, y, 1 - c)
    chips = [(1 - x, y), (x, 1 - y), (1 - x, 1 - y)]

    def slot(t, dev):
        return 4 * dev[0] + 2 * dev[1] + dev[2] if slots is None or slots[t] is None else slots[t](dev)

    def copy(t, k, block, to, src=None):
        return pltpu.make_async_remote_copy(
            src_ref=dsts[t].at[slot(t, block)] if src is None else src, dst_ref=dsts[t].at[slot(t, block)],
            send_sem=send_sems.at[t, k], recv_sem=recv_sems.at[t, k], device_id=to, device_id_type=MESH)

    def local(t):
        return pltpu.make_async_copy(srcs[t], dsts[t].at[slot(t, me)], local_sems.at[t])

    def start():
        for t in range(n):
            local(t).start()
            for j, chip in enumerate(chips):
                copy(t, 1 + j, me, (*chip, c), src=srcs[t]).start()
            copy(t, 0, me, sibling, src=srcs[t]).start()

    def forward(t):
        for j, chip in enumerate(chips):
            copy(t, 1 + j, (*chip, c), me).wait_recv()
            copy(t, 4 + j, (*chip, c), sibling).start()

    def finish():
        for t in range(n):
            copy(t, 0, sibling, me).wait_recv()
            for j, chip in enumerate(chips):
                copy(t, 4 + j, (*chip, 1 - c), me).wait_recv()
            copy(t, 0, me, sibling, src=srcs[t]).wait_send()
            for j, chip in enumerate(chips):
                copy(t, 1 + j, me, (*chip, c), src=srcs[t]).wait_send()
                copy(t, 4 + j, (*chip, c), sibling).wait_send()
            local(t).wait()

    return start, forward, finish


def _hosted_gather(i, nb, forward_at, srcs, dsts, sems, slots=None):
    start, forward, finish = _two_level_gather(srcs, dsts, *sems, slots)

    def after_step():
        for t, f in enumerate(forward_at):
            @pl.when(i == min(nb - 1, int(f * nb)))
            def _():
                forward(t)

        @pl.when(i == nb - 1)
        def _():
            finish()

    return start, after_step


def _gather_scratch(n):
    return [pltpu.SemaphoreType.DMA((n, N_GATHER_COPIES)), pltpu.SemaphoreType.DMA((n, N_GATHER_COPIES)),
            pltpu.SemaphoreType.DMA((n,))]


def _gather(name, srcs):
    n = len(srcs)

    def body(*refs):
        start, forward, finish = _two_level_gather(refs[:n], refs[n:2 * n], *refs[2 * n:])
        start()
        for t in range(n):
            forward(t)
        finish()

    any_spec = pl.BlockSpec(memory_space=pl.ANY)
    return pl.pallas_call(
        body, name=name, in_specs=[any_spec] * n, out_specs=[any_spec] * n,
        out_shape=[jax.ShapeDtypeStruct((N_DEV,) + a.shape, a.dtype) for a in srcs], scratch_shapes=_gather_scratch(n),
    )(*srcs)


def _scatter_plan(blocks, dests, landing):
    return dict(blocks=list(blocks), dests=[list(dd) for dd in dests], landing=list(landing))


def _scatter_args(plan):
    return plan['blocks'] + [a for a in plan['landing'] if a is not None]


def _scatter_out_shape(plan):
    return [jax.ShapeDtypeStruct((N_DEV,) + b.shape[1:], b.dtype) for b in plan['blocks']]


def _scatter_aliases(plan, first_in, first_out):
    given = [t for t, a in enumerate(plan['landing']) if a is not None]
    return {first_in + len(plan['blocks']) + k: first_out + t for k, t in enumerate(given)}


def _scatter_ops(plan, in_refs, out_refs, sems):
    n = len(plan['blocks'])
    srcs_of = [(lambda s, r=in_refs[t], dd=plan['dests'][t]: r.at[dd.index(s)]) for t in range(n)]
    return _all_to_all(srcs_of, out_refs, *sems, dests=plan['dests'])


def _exchange_scratch(n):
    return [pltpu.SemaphoreType.DMA((n, N_DEV)), pltpu.SemaphoreType.DMA((n, N_DEV)), pltpu.SemaphoreType.DMA((n,))]


def _const_spec(shape):
    nd = len(shape)
    return pl.BlockSpec(shape, lambda i: (0,) * nd, pipeline_mode=pl.Buffered(1))


def _pool_windows(up_ext, n, forward):
    sh = (lambda k: k) if forward else (lambda k: n - k)
    s2 = up_ext + pltpu.roll(up_ext, sh(1), 0)
    t4 = s2[:, 128:]
    s4 = t4 + pltpu.roll(t4, sh(2), 0)
    t8 = s4[:, 128:]
    s8 = t8 + pltpu.roll(t8, sh(4), 0)
    t16 = s8[:, 128:]
    s16 = t16 + pltpu.roll(t16, sh(8), 0)
    return jnp.concatenate([s2[:, :128], s4[:, :128], s8[:, :128], s16], axis=1)


def _inv_count_head():
    t = jnp.arange(1, HALO + 1, dtype=F32)[:, None]
    return jnp.concatenate([jnp.broadcast_to(1.0 / jnp.minimum(t, float(w)), (HALO, 128)) for w in POOL_WINDOWS], axis=1)


def _scale_by_inv_count(v, is_first_block, inv_head):
    inv_row = jnp.concatenate([jnp.full((1, 128), 1.0 / w, F32) for w in POOL_WINDOWS], axis=1)
    head = v[0:HALO] * jnp.where(is_first_block, inv_head, inv_row)
    return jnp.concatenate([head, v[HALO:] * inv_row], axis=0)


def _lru_decay(r, a, c_l, first_row):
    a2 = a * a
    m2 = -jnp.tanh(c_l * r) * (a2 + 1.0)
    return a2, m2, jnp.where(first_row, 1.0, jnp.sqrt(m2))


def _log_sigmoid(v):
    return -(jnp.maximum(-v, 0.0) + jnp.log1p(jnp.exp(-jnp.abs(v))))


def _conv_fwd(ul_ext, cw, cb):
    return (cb + cw[3:4, :] * ul_ext + cw[2:3, :] * pltpu.roll(ul_ext, 1, 0)
            + cw[1:2, :] * pltpu.roll(ul_ext, 2, 0) + cw[0:1, :] * pltpu.roll(ul_ext, 3, 0))


def _mix_fwd(x, g_mix, w_in, wp_bd, pool_b, pool_scale, conv_w, conv_b, wg_bd, gate_b, lru_l, w_out, gather_srcs,
             gather_slots, forward_at, tb):
    t_len, d = x.shape
    nb = t_len // tb
    n_g = len(gather_srcs)

    def body(*refs):
        (x_ref, g_ref, win_ref, wp_ref, pb_ref, ps_ref, cw_ref, cb_ref, wg_ref, gb_ref, l_ref, wout_ref,
         invh_ref) = refs[:13]
        gsrc = refs[13:13 + n_g]
        h1_ref, z1_ref, proj_ref, hs_ref, cat_ref, lru_ref, dpool_ref = refs[13 + n_g:20 + n_g]
        gdst = refs[20 + n_g:20 + 2 * n_g]
        ext_ref, a_ref, b_ref, hc_ref, send_sems, recv_sems, local_sems = refs[20 + 2 * n_g:]
        i = pl.program_id(0)
        start_gather, after_step = _hosted_gather(i, nb, forward_at, gsrc, gdst, (send_sems, recv_sems, local_sems),
                                                  gather_slots)

        @pl.when(i == 0)
        def _():
            start_gather()
            ext_ref[0:HALO, :] = jnp.zeros((HALO, 1024), F32)
            hc_ref[...] = jnp.zeros_like(hc_ref)

        xv = x_ref[...]
        z, _, _ = _rms_fwd(xv, g_ref[...])
        zb = z.astype(BF16)
        z1_ref[...] = zb
        proj = _dot(zb, win_ref[...])
        proj_ref[...] = proj
        ext_ref[HALO:, :] = proj[:, 0:1024]
        ug = proj[:, 1024:1536]
        n = tb + HALO
        up_ext = ext_ref[:, 0:512]
        win = _pool_windows(up_ext, n, True)[HALO:]
        dpool = _scale_by_inv_count(win, i == 0, invh_ref[...]) - proj[:, 0:512]
        dpoolb = dpool.astype(BF16)
        dpool_ref[...] = dpoolb
        q = _dot(dpoolb, wp_ref[...]) + pb_ref[...]
        y_pool = q * ps_ref[...]
        xb = _conv_fwd(ext_ref[:, 512:1024], cw_ref[...], cb_ref[...])[HALO:]
        first_row = (i * tb + lax.broadcasted_iota(jnp.int32, (tb, 1), 0)) == 0
        c_l = LRU_C * _log_sigmoid(l_ref[...])
        gp = _dot(xb.astype(BF16), wg_ref[...]) + gb_ref[...]
        r = _sigmoid(gp[:, :512])
        ig = _sigmoid(gp[:, 512:])
        a = jnp.exp(c_l * r)
        _, _, mult = _lru_decay(r, a, c_l, first_row)
        lru_ref[:, 0:512] = xb
        lru_ref[:, 512:1024] = r
        lru_ref[:, 1024:1536] = ig
        lru_ref[:, 1536:2048] = a
        a_ref[...] = a
        b_ref[...] = mult * (ig * xb)
        row = lax.broadcasted_iota(jnp.int32, (SUB, 512), 0)

        def group(j, hprev):
            o = pl.multiple_of(j * SUB, SUB)
            a8 = a_ref[pl.ds(o, SUB), :]
            b8 = b_ref[pl.ds(o, SUB), :]
            for sh in (1, 2, 4):
                ash = jnp.where(row >= sh, pltpu.roll(a8, sh, 0), 1.0)
                bsh = jnp.where(row >= sh, pltpu.roll(b8, sh, 0), 0.0)
                b8 = a8 * bsh + b8
                a8 = a8 * ash
            h8 = a8 * hprev + b8
            hs_ref[pl.ds(o, SUB), :] = h8
            return jnp.broadcast_to(h8[SUB - 1:SUB, :], (SUB, 512))

        def trip(k, carry):
            for u in range(SCAN_UNROLL):
                carry = group(k * SCAN_UNROLL + u, carry)
            return carry

        hc_ref[...] = lax.fori_loop(0, tb // (SUB * SCAN_UNROLL), trip, hc_ref[...])
        gel, dgel = _gelu_parts(ug)
        lru_ref[:, 2048:2560] = gel
        lru_ref[:, 2560:3072] = dgel
        y_lru = hs_ref[...] * gel
        catb = jnp.concatenate([y_pool, y_lru], axis=1).astype(BF16)
        cat_ref[...] = catb
        h1_ref[...] = xv + _dot(catb, wout_ref[...])
        ext_ref[0:HALO, :] = ext_ref[tb:tb + HALO, :]

        after_step()

    row_spec = lambda w: pl.BlockSpec((tb, w), lambda i: (i, 0))
    any_spec = pl.BlockSpec(memory_space=pl.ANY)
    smalls = [g_mix, w_in, wp_bd, pool_b, pool_scale, conv_w, conv_b, wg_bd, gate_b, lru_l, w_out, _inv_count_head()]
    return pl.pallas_call(
        body, name="mix_fwd", grid=(nb,),
        in_specs=[row_spec(d)] + [_const_spec(s.shape) for s in smalls] + [any_spec] * n_g,
        out_specs=[row_spec(d), row_spec(d), row_spec(1536), row_spec(512), row_spec(1024), row_spec(3072), row_spec(512)]
        + [any_spec] * n_g,
        out_shape=[jax.ShapeDtypeStruct((t_len, d), F32), jax.ShapeDtypeStruct((t_len, d), BF16),
                   jax.ShapeDtypeStruct((t_len, 1536), F32), jax.ShapeDtypeStruct((t_len, 512), F32),
                   jax.ShapeDtypeStruct((t_len, 1024), BF16), jax.ShapeDtypeStruct((t_len, 3072), F32),
                   jax.ShapeDtypeStruct((t_len, 512), BF16)]
        + [jax.ShapeDtypeStruct((N_DEV,) + s.shape, s.dtype) for s in gather_srcs],
        scratch_shapes=[pltpu.VMEM((tb + HALO, 1024), F32), pltpu.VMEM((tb, 512), F32), pltpu.VMEM((tb, 512), F32),
                        pltpu.VMEM((SUB, 512), F32)] + _gather_scratch(n_g),
        compiler_params=_params(),
    )(x, *smalls, *gather_srcs)


def _mlp_fwd(h1, g_mlp, w_up, w_down, tb):
    t_len, d = h1.shape
    nb = t_len // tb
    n_chunk, _, fc = w_up.shape

    def body(h1_ref, g_ref, wup_ref, wdn_ref, h2_ref, z2_ref, up_ref):
        xv = h1_ref[...]
        z, _, _ = _rms_fwd(xv, g_ref[...])
        zb = z.astype(BF16)
        z2_ref[...] = zb
        acc = xv
        for c in range(n_chunk):
            u = _dot(zb, wup_ref[c])
            up_ref[:, c * fc:(c + 1) * fc] = u.astype(BF16)
            act = jnp.square(jnp.maximum(u, 0.0)).astype(BF16)
            acc = acc + _dot(act, wdn_ref[c * fc:(c + 1) * fc, :])
        h2_ref[...] = acc

    row_spec = lambda w: pl.BlockSpec((tb, w), lambda i: (i, 0))
    return pl.pallas_call(
        body, name="mlp_fwd", grid=(nb,),
        in_specs=[row_spec(d), _const_spec(g_mlp.shape), _const_spec(w_up.shape), _const_spec(w_down.shape)],
        out_specs=[row_spec(d), row_spec(d), row_spec(n_chunk * fc)],
        out_shape=[jax.ShapeDtypeStruct((t_len, d), F32), jax.ShapeDtypeStruct((t_len, d), BF16),
                   jax.ShapeDtypeStruct((t_len, n_chunk * fc), BF16)],
        compiler_params=_params(),
    )(h1, g_mlp, w_up, w_down)


def _ple(h2, p, target, g_ple, w_gate, b_gate, w_proj, g_final, tb):
    t_len, d = h2.shape
    nb = t_len // tb
    pd = p.shape[1]

    def body(h2_ref, p_ref, tgt_ref, g_ref, wg_ref, bg_ref, wp_ref, gf_ref,
             dh2_ref, vec_ref, dwg_out, dwp_out, dwg_acc, dwp_acc, dwg_stage, dwp_stage):
        i = pl.program_id(0)

        @pl.when(i == 0)
        def _():
            vec_ref[...] = jnp.zeros_like(vec_ref)
            dwg_acc[...] = jnp.zeros_like(dwg_acc)
            dwp_acc[...] = jnp.zeros_like(dwp_acc)

        h2 = h2_ref[...]
        g2 = g_ref[...]
        z3, xh2, r2 = _rms_fwd(h2, g2)
        z3b = z3.astype(BF16)
        gate = _sigmoid(_dot(z3b, wg_ref[...]) + bg_ref[...])
        pb = p_ref[...].astype(BF16)
        pp = _dot(pb, wp_ref[...])
        h3 = h2 + gate * pp
        gf = gf_ref[...]
        y, xh3, r3 = _rms_fwd(h3, gf)
        err = y - tgt_ref[...]
        loss_rows = jnp.mean(err * err, axis=-1, keepdims=True)
        dy = err * (1.0 / d)
        dh3 = _rms_bwd(xh3, r3, gf, dy)
        dgl = (dh3 * pp) * (gate * (1.0 - gate))
        dpp = dh3 * gate
        dglb = dgl.astype(BF16)
        dwg_acc[...] += _dot_tn(z3b, dglb)
        dwp_acc[...] += _dot_tn(pb, dpp.astype(BF16))
        dz3 = _dot_nt(dglb, wg_ref[...])
        dh2_ref[...] = dh3 + _rms_bwd(xh2, r2, g2, dz3)
        vec_ref[0:1, :] += _colsum(dgl)
        vec_ref[1:2, :] += _colsum(dz3 * xh2)
        vec_ref[2:3, :] += _colsum(dy * xh3)
        vec_ref[3:4, :] += 0.5 * jnp.sum(loss_rows)

        @pl.when(i == nb - 1)
        def _():
            dwg_stage[...] = dwg_acc[...].astype(BF16)
            dwp_stage[...] = dwp_acc[...].astype(BF16)
            pltpu.sync_copy(dwg_stage, dwg_out)
            pltpu.sync_copy(dwp_stage, dwp_out)

    row_spec = lambda w: pl.BlockSpec((tb, w), lambda i: (i, 0))
    any_spec = pl.BlockSpec(memory_space=pl.ANY)
    smalls = [g_ple, w_gate, b_gate, w_proj, g_final]
    return pl.pallas_call(
        body, name="ple_fwd_bwd", grid=(nb,),
        in_specs=[row_spec(d), row_spec(pd), row_spec(d)] + [_const_spec(s.shape) for s in smalls],
        out_specs=[row_spec(d), pl.BlockSpec((8, d), lambda i: (0, 0)), any_spec, any_spec],
        out_shape=[jax.ShapeDtypeStruct((t_len, d), F32), jax.ShapeDtypeStruct((8, d), F32),
                   jax.ShapeDtypeStruct(w_gate.shape, BF16), jax.ShapeDtypeStruct(w_proj.shape, BF16)],
        scratch_shapes=[pltpu.VMEM(w_gate.shape, F32), pltpu.VMEM(w_proj.shape, F32), pltpu.VMEM(w_gate.shape, BF16),
                        pltpu.VMEM(w_proj.shape, BF16)],
        compiler_params=_params(),
    )(h2, p, target, *smalls)


def _mlp_bwd_part(part, n_part, dh2, z2, up, w_up, w_down, dz2_prev, h1, g_mlp, scatter, tb):
    t_len, d = dh2.shape
    nb = t_len // tb
    n_chunk_all, _, fc = w_up.shape
    n_chunk = n_chunk_all // n_part
    first, last = part == 0, part == n_part - 1

    def body(*refs):
        refs = list(refs)
        dh2_ref, z2_ref, up_ref, wup_ref, wdn_ref = refs[:5]
        del refs[:5]
        dzp_ref = None if first else refs.pop(0)
        h1_ref, g_ref = (refs.pop(0), refs.pop(0)) if last else (None, None)
        scatter_in = [refs.pop(0) for _ in _scatter_args(scatter)]
        out_ref = refs.pop(0)
        vec_ref = refs.pop(0) if last else None
        dwup_out, dwdn_out = refs.pop(0), refs.pop(0)
        scatter_out = [refs.pop(0) for _ in scatter['blocks']]
        dwup_acc, dwdn_acc, up_stage, dn_stage = refs[:4]
        if scatter['blocks']:
            start_scatter, wait_scatter = _scatter_ops(scatter, scatter_in, scatter_out, refs[4:])
        i = pl.program_id(0)

        @pl.when(i == 0)
        def _():
            if scatter['blocks']:
                start_scatter()
            dwup_acc[...] = jnp.zeros_like(dwup_acc)
            dwdn_acc[...] = jnp.zeros_like(dwdn_acc)
            if last:
                vec_ref[...] = jnp.zeros_like(vec_ref)

        dh2 = dh2_ref[...]
        dh2b = dh2.astype(BF16)
        z2b = z2_ref[...]
        dz2 = jnp.zeros((tb, d), F32) if first else dzp_ref[...]
        for c in range(n_chunk):
            u = up_ref[:, c * fc:(c + 1) * fc].astype(F32)
            ur = jnp.maximum(u, 0.0)
            dact = _dot_nt(dh2b, wdn_ref[c * fc:(c + 1) * fc, :])
            dupb = (dact * (2.0 * ur)).astype(BF16)
            dwdn_acc[c * fc:(c + 1) * fc, :] += _dot_tn((ur * ur).astype(BF16), dh2b)
            dwup_acc[c] += _dot_tn(z2b, dupb)
            dz2 = dz2 + _dot_nt(dupb, wup_ref[c])
        if last:
            g = g_ref[...]
            _, xh, r = _rms_fwd(h1_ref[...], g)
            out_ref[...] = dh2 + _rms_bwd(xh, r, g, dz2)
            vec_ref[0:1, :] += _colsum(dz2 * xh)
        else:
            out_ref[...] = dz2

        @pl.when(i == nb - 1)
        def _():
            for c in range(n_chunk):
                up_stage[...] = dwup_acc[c].astype(BF16)
                dn_stage[...] = dwdn_acc[c * fc:(c + 1) * fc, :].astype(BF16)
                pltpu.sync_copy(up_stage, dwup_out.at[c])
                pltpu.sync_copy(dn_stage, dwdn_out.at[c])
            if scatter['blocks']:
                wait_scatter()

    row_spec = lambda w: pl.BlockSpec((tb, w), lambda i: (i, 0))
    any_spec = pl.BlockSpec(memory_space=pl.ANY)
    args = [dh2, z2, up, w_up, w_down]
    in_specs = [row_spec(d), row_spec(d), pl.BlockSpec((tb, n_chunk * fc), lambda i: (i, part)),
                pl.BlockSpec((n_chunk, d, fc), lambda i: (part, 0, 0), pipeline_mode=pl.Buffered(1)),
                pl.BlockSpec((n_chunk * fc, d), lambda i: (part, 0), pipeline_mode=pl.Buffered(1))]
    if not first:
        args.append(dz2_prev)
        in_specs.append(row_spec(d))
    if last:
        args += [h1, g_mlp]
        in_specs += [row_spec(d), _const_spec(g_mlp.shape)]
    n_in = len(args)
    args += _scatter_args(scatter)
    in_specs += [any_spec] * len(_scatter_args(scatter))
    out_specs = [row_spec(d)]
    out_shape = [jax.ShapeDtypeStruct((t_len, d), F32)]
    if last:
        out_specs.append(pl.BlockSpec((8, d), lambda i: (0, 0)))
        out_shape.append(jax.ShapeDtypeStruct((8, d), F32))
    out_specs += [any_spec, any_spec]
    out_shape += [jax.ShapeDtypeStruct((n_chunk, d, fc), BF16), jax.ShapeDtypeStruct((n_chunk, fc, d), BF16)]
    n_out = len(out_shape)
    out_specs += [any_spec] * len(scatter['blocks'])
    out_shape += _scatter_out_shape(scatter)
    return pl.pallas_call(
        body, name=f"mlp_bwd_{part}", grid=(nb,), in_specs=in_specs, out_specs=out_specs, out_shape=out_shape,
        scratch_shapes=[pltpu.VMEM((n_chunk, d, fc), F32), pltpu.VMEM((n_chunk * fc, d), F32),
                        pltpu.VMEM((d, fc), BF16), pltpu.VMEM((fc, d), BF16)]
        + (_exchange_scratch(len(scatter['blocks'])) if scatter['blocks'] else []),
        input_output_aliases=_scatter_aliases(scatter, n_in, n_out), compiler_params=_params(),
    )(*args)


def _mix_bwd(dh1, proj, hs, lru_saved, dpool_saved, wp_bd, pool_b, pool_scale, conv_w, wg_bd, lru_l, w_out, scatter, tb):
    t_len, d = dh1.shape
    nb = t_len // tb
    n_s = len(scatter['blocks'])
    scatter_args = _scatter_args(scatter)

    def body(*refs):
        refs = list(refs)
        (dh1_ref, ul_ref, hs_ref, hsh_ref, lru_ref, dpool_ref,
         wp_ref, pb_ref, ps_ref, cw_ref, wg_ref, l_ref, wout_ref, invh_ref) = refs[:14]
        del refs[:14]
        scatter_in = refs[:len(scatter_args)]
        del refs[:len(scatter_args)]
        dproj_ref, v512_ref, dpw_ref, dga_ref, dgx_ref = refs[:5]
        recv = refs[5:5 + n_s]
        (dwp_acc, dwg_acc, v1024_ref, b_ref, gs_ref, ehead_ref, dxbhead_ref, hc_ref,
         send_sems, recv_sems, local_sems) = refs[5 + n_s:]
        i = pl.program_id(0)
        tbk = nb - 1 - i

        start_scatter, wait_scatter = _scatter_ops(scatter, scatter_in, recv, (send_sems, recv_sems, local_sems))

        @pl.when(i == 0)
        def _():
            start_scatter()
            for ref in (v512_ref, v1024_ref, dwp_acc, dwg_acc, ehead_ref, dxbhead_ref, hc_ref):
                ref[...] = jnp.zeros_like(ref)

        dcat = _dot_nt(dh1_ref[...].astype(BF16), wout_ref[...])

        has_prev = (tbk > 0).astype(F32)
        n = tb + HALO
        inv_head = invh_ref[...]

        dpoolb = dpool_ref[...]
        q = _dot(dpoolb, wp_ref[...]) + pb_ref[...]
        dyp = dcat[:, 0:512]
        dq = dyp * ps_ref[...]
        dqb = dq.astype(BF16)
        v512_ref[0:1, :] += _colsum(dyp * q)
        v512_ref[1:2, :] += _colsum(dq)
        dwp_acc[...] += _dot_tn(dpoolb, dqb)
        dd = _dot_nt(dqb, wp_ref[...])
        e = _scale_by_inv_count(dd, tbk == 0, inv_head)
        e_ext = jnp.concatenate([e, ehead_ref[...]], axis=0)
        du_pool = _pool_windows(e_ext, n, False)[0:tb] - dd
        ehead_ref[...] = e[0:HALO]

        gel, dgel = lru_ref[:, 2048:2560], lru_ref[:, 2560:3072]
        hsv = hs_ref[...]
        dcl = dcat[:, 512:1024]
        dhs = dcl * gel
        dug = dcl * hsv * dgel
        cw = cw_ref[...]
        xb, r, ig, a = lru_ref[:, 0:512], lru_ref[:, 512:1024], lru_ref[:, 1024:1536], lru_ref[:, 1536:2048]
        first_row = (tbk * tb + lax.broadcasted_iota(jnp.int32, (tb, 1), 0)) == 0
        c_l = LRU_C * _log_sigmoid(l_ref[...])
        a2, m2, mult = _lru_decay(r, a, c_l, first_row)
        b_ref[...] = dhs
        row = lax.broadcasted_iota(jnp.int32, (SUB, 512), 0)

        def group(jj, hnext):
            o = pl.multiple_of((tb // SUB - 1 - jj) * SUB, SUB)
            a8 = lru_ref[pl.ds(o, SUB), 1536:2048]
            d8 = b_ref[pl.ds(o, SUB), :]
            b8 = a8 * d8
            for sh in (1, 2, 4):
                ash = jnp.where(row < SUB - sh, pltpu.roll(a8, SUB - sh, 0), 1.0)
                bsh = jnp.where(row < SUB - sh, pltpu.roll(b8, SUB - sh, 0), 0.0)
                b8 = a8 * bsh + b8
                a8 = a8 * ash
            h8 = a8 * hnext + b8
            gs_ref[pl.ds(o, SUB), :] = d8 + jnp.where(row < SUB - 1, pltpu.roll(h8, SUB - 1, 0), hnext)
            return jnp.broadcast_to(h8[0:1, :], (SUB, 512))

        def trip(k, carry):
            for u in range(SCAN_UNROLL):
                carry = group(k * SCAN_UNROLL + u, carry)
            return carry

        hc_ref[...] = lax.fori_loop(0, tb // (SUB * SCAN_UNROLL), trip, hc_ref[...])
        gsum = gs_ref[...]
        hs_ext = jnp.concatenate([hsh_ref[...] * has_prev, hsv], axis=0)
        hprev = pltpu.roll(hs_ext, 1, 0)[SUB:]
        da = gsum * hprev
        dmult = jnp.where(first_row, 0.0, gsum * (ig * xb))
        di = gsum * mult * xb
        dxb = gsum * mult * ig
        dla = da * a - dmult * a2 * lax.rsqrt(m2)
        dr = dla * c_l
        v512_ref[3:4, :] += _colsum(dla * r)
        dgp = jnp.concatenate([dr * r * (1.0 - r), di * ig * (1.0 - ig)], axis=1)
        dgpb = dgp.astype(BF16)
        v1024_ref[0:1, :] += _colsum(dgp)
        dwg_acc[...] += _dot_tn(xb.astype(BF16), dgpb)
        dxb = dxb + _dot_nt(dgpb, wg_ref[...])
        n8 = tb + SUB
        dxb_ext = jnp.concatenate([dxb, dxbhead_ref[...]], axis=0)
        ul = ul_ref[...]
        du_lru = cw[3:4, :] * dxb
        v512_ref[7:8, :] += _colsum(dxb * ul)
        for j in range(1, 4):
            ahead = pltpu.roll(dxb_ext, n8 - j, 0)[0:tb]
            du_lru = du_lru + cw[3 - j:4 - j, :] * ahead
            v512_ref[4 + (3 - j):5 + (3 - j), :] += _colsum(ahead * ul)
        dxbhead_ref[...] = dxb[0:SUB]
        v512_ref[2:3, :] += _colsum(dxb)

        dproj_ref[...] = jnp.concatenate([du_pool, du_lru, dug], axis=1).astype(BF16)

        @pl.when(i == nb - 1)
        def _():
            v512_ref[3:4, :] = v512_ref[3:4, :] * (LRU_C * _sigmoid(-l_ref[...]))
            v512_ref[8:9, :] = v1024_ref[0:1, 0:512]
            v512_ref[9:10, :] = v1024_ref[0:1, 512:1024]
            for g in range(N_POOL_GROUPS):
                dpw_ref[g * 128:(g + 1) * 128, :] = dwp_acc[g * 128:(g + 1) * 128, g * 128:(g + 1) * 128]
            odd_head = (lax.broadcasted_iota(jnp.int32, (512, 128), 0) // 64) % 2 == 1
            for out_ref, col0 in ((dga_ref, 0), (dgx_ref, 512)):
                pairs = jnp.concatenate([dwg_acc[128 * k:128 * (k + 1), col0 + 128 * k:col0 + 128 * (k + 1)]
                                         for k in range(LRU_HEADS // 2)], axis=0)
                out_ref[...] = jnp.where(odd_head, pltpu.roll(pairs, 64, 1), pairs)[:, 0:64]
            wait_scatter()

    rev = lambda w: pl.BlockSpec((tb, w), lambda i: (nb - 1 - i, 0))
    halo = lambda rows, w: pl.BlockSpec((rows, w), lambda i: (jnp.maximum((nb - 1 - i) * (tb // rows) - 1, 0), 0))
    any_spec = pl.BlockSpec(memory_space=pl.ANY)
    smalls = [wp_bd, pool_b, pool_scale, conv_w, wg_bd, lru_l, w_out, _inv_count_head()]
    lru_third = pl.BlockSpec((tb, 512), lambda i: (nb - 1 - i, 1))
    return pl.pallas_call(
        body, name="mix_bwd", grid=(nb,),
        in_specs=[rev(d), lru_third, rev(512), halo(SUB, 512), rev(3072), rev(512)]
        + [_const_spec(s.shape) for s in smalls] + [any_spec] * len(scatter_args),
        out_specs=[rev(1536), pl.BlockSpec((16, 512), lambda i: (0, 0)), pl.BlockSpec((512, 128), lambda i: (0, 0)),
                   pl.BlockSpec((512, 64), lambda i: (0, 0)), pl.BlockSpec((512, 64), lambda i: (0, 0))]
        + [any_spec] * n_s,
        out_shape=[jax.ShapeDtypeStruct((t_len, 1536), BF16), jax.ShapeDtypeStruct((16, 512), F32),
                   jax.ShapeDtypeStruct((512, 128), F32), jax.ShapeDtypeStruct((512, 64), F32),
                   jax.ShapeDtypeStruct((512, 64), F32)]
        + _scatter_out_shape(scatter),
        scratch_shapes=[pltpu.VMEM(wp_bd.shape, F32), pltpu.VMEM(wg_bd.shape, F32), pltpu.VMEM((8, 1024), F32),
                        pltpu.VMEM((tb, 512), F32), pltpu.VMEM((tb, 512), F32), pltpu.VMEM((HALO, 512), F32),
                        pltpu.VMEM((SUB, 512), F32), pltpu.VMEM((SUB, 512), F32)]
        + _exchange_scratch(n_s),
        input_output_aliases=_scatter_aliases(scatter, 6 + len(smalls), 5), compiler_params=_params(),
    )(dh1, proj, hs, hs, lru_saved, dpool_saved, *smalls, *scatter_args)


def _wgrad(name, a, b, whole, by_rows, tb):
    t_len, m = a.shape
    n = b.shape[1]
    nb = t_len // tb
    n_w, n_r = len(whole), len(by_rows)
    n_small = n_w + n_r

    def body(*refs):
        a_ref, b_ref = refs[:2]
        small_in = refs[2:2 + n_small]
        out_ref = refs[2 + n_small]
        small_out = refs[3 + n_small:3 + 2 * n_small]
        acc_ref, stage_ref = refs[3 + 2 * n_small:5 + 2 * n_small]
        rest = refs[5 + 2 * n_small:]
        if n_small:
            send_partials, reduce_and_send_sums, finish_small = _small_allreduce(
                small_in[:n_w], small_in[n_w:], small_out[:n_w], small_out[n_w:], rest[:n_w], rest[n_w:n_small],
                rest[n_small:n_small + n_r], *rest[n_small + n_r:])
        i = pl.program_id(0)

        @pl.when(i == 0)
        def _():
            if n_small:
                send_partials()
            acc_ref[...] = jnp.zeros_like(acc_ref)

        acc_ref[...] += _dot_tn(a_ref[...], b_ref[...].astype(BF16))

        if n_small:
            @pl.when(i == nb // 2)
            def _():
                reduce_and_send_sums()

        @pl.when(i == nb - 1)
        def _():
            stage_ref[...] = acc_ref[...].astype(BF16)
            pltpu.sync_copy(stage_ref, out_ref)
            if n_small:
                finish_small()

    small = list(whole) + list(by_rows)
    vmem_spec = pl.BlockSpec(memory_space=pltpu.VMEM)
    res = pl.pallas_call(
        body, name=name, grid=(nb,),
        in_specs=[pl.BlockSpec((tb, m), lambda i: (i, 0)), pl.BlockSpec((tb, n), lambda i: (i, 0))] + [vmem_spec] * n_small,
        out_specs=[pl.BlockSpec(memory_space=pl.ANY)] + [vmem_spec] * n_small,
        out_shape=[jax.ShapeDtypeStruct((m, n), BF16)] + [jax.ShapeDtypeStruct(s_.shape, F32) for s_ in small],
        scratch_shapes=[pltpu.VMEM((m, n), F32), pltpu.VMEM((m, n), BF16)]
        + (_small_allreduce_scratch(whole, by_rows) if n_small else []),
        compiler_params=_params(),
    )(a, b, *small)
    return res[0], res[1:1 + n_w], res[1 + n_w:]


def _in_bwd(dproj, x, dh1, g_mix, w_in, scatter, tb):
    t_len, d = x.shape
    nb = t_len // tb
    n_s = len(scatter['blocks'])
    scatter_args = _scatter_args(scatter)

    def body(*refs):
        dproj_ref, x_ref, dh1_ref, g_ref, win_ref = refs[:5]
        scatter_in = refs[5:5 + len(scatter_args)]
        dx_ref, vec_ref = refs[5 + len(scatter_args):7 + len(scatter_args)]
        recv = refs[7 + len(scatter_args):7 + len(scatter_args) + n_s]
        vec_acc, send_sems, recv_sems, local_sems, vec_land, small_send, small_recv = refs[7 + len(scatter_args) + n_s:]
        start_scatter, wait_scatter = _scatter_ops(scatter, scatter_in, recv, (send_sems, recv_sems, local_sems))
        send_partials, reduce_and_send_sums, finish_small = _small_allreduce(
            [vec_acc], [], [vec_ref], [], [vec_land], [], [], small_send, small_recv)
        i = pl.program_id(0)

        @pl.when(i == 0)
        def _():
            start_scatter()
            vec_acc[...] = jnp.zeros_like(vec_acc)

        dz1 = _dot_nt(dproj_ref[...], win_ref[...])
        g = g_ref[...]
        _, xh, rr = _rms_fwd(x_ref[...], g)
        dx_ref[...] = dh1_ref[...] + _rms_bwd(xh, rr, g, dz1)
        vec_acc[0:1, :] += _colsum(dz1 * xh)

        @pl.when(i == nb - 1)
        def _():
            send_partials()
            reduce_and_send_sums()
            finish_small()
            wait_scatter()

    row_spec = lambda w: pl.BlockSpec((tb, w), lambda i: (i, 0))
    any_spec = pl.BlockSpec(memory_space=pl.ANY)
    return pl.pallas_call(
        body, name="in_bwd", grid=(nb,),
        in_specs=[row_spec(dproj.shape[1]), row_spec(d), row_spec(d), _const_spec(g_mix.shape), _const_spec(w_in.shape)]
        + [any_spec] * len(scatter_args),
        out_specs=[row_spec(d), pl.BlockSpec((8, d), lambda i: (0, 0))] + [any_spec] * n_s,
        out_shape=[jax.ShapeDtypeStruct((t_len, d), F32), jax.ShapeDtypeStruct((8, d), F32)] + _scatter_out_shape(scatter),
        scratch_shapes=[pltpu.VMEM((8, d), F32)] + _exchange_scratch(n_s)
        + _small_allreduce_scratch([jax.ShapeDtypeStruct((8, d), F32)], []),
        input_output_aliases=_scatter_aliases(scatter, 5, 2), compiler_params=_params(),
    )(dproj, x, dh1, g_mix, w_in, *scatter_args)


def _small_allreduce(whole_in, rows_in, whole_out, rows_out, whole_land, rows_land, rows_sum, send_sems, recv_sems):
    n_w, n_r = len(whole_in), len(rows_in)
    per = [r.shape[0] // N_DEV for r in rows_in]
    me = _my_index()

    def dev(s):
        return (s // 4, (s // 2) % 2, s % 2)

    def rows_of(t, s):
        return pl.ds(s * per[t], per[t])

    def mine(t):
        return pl.ds(pl.multiple_of(me * per[t], 8), per[t])

    def partial(t, s, slot):
        if t < n_w:
            src, dst = whole_in[t], whole_land[t]
        else:
            src, dst = rows_in[t - n_w].at[rows_of(t - n_w, s)], rows_land[t - n_w]
        return pltpu.make_async_remote_copy(
            src_ref=src, dst_ref=dst.at[slot], send_sem=send_sems.at[t, s], recv_sem=recv_sems.at[t, slot],
            device_id=dev(s), device_id_type=MESH)

    def summed(t, s, rows, slot):
        return pltpu.make_async_remote_copy(
            src_ref=rows_sum[t].at[rows], dst_ref=rows_sum[t].at[rows], send_sem=send_sems.at[n_w + n_r + t, s],
            recv_sem=recv_sems.at[n_w + n_r + t, slot], device_id=dev(s), device_id_type=MESH)

    def send_partials():
        for s in range(N_DEV):
            @pl.when(s != me)
            def _():
                for t in range(n_w + n_r):
                    partial(t, s, me).start()
        for t in range(n_w):
            whole_land[t][me] = whole_in[t][...]
        for t in range(n_r):
            rows_land[t][me] = rows_in[t][mine(t), :]

    def reduce_and_send_sums():
        for s in range(N_DEV):
            @pl.when(s != me)
            def _():
                for t in range(n_w + n_r):
                    partial(t, s, s).wait_recv()
        for t in range(n_w):
            total = whole_land[t][0]
            for s in range(1, N_DEV):
                total = total + whole_land[t][s]
            whole_out[t][...] = total
        for t in range(n_r):
            total = rows_land[t][0]
            for s in range(1, N_DEV):
                total = total + rows_land[t][s]
            rows_sum[t][mine(t), :] = total
        for s in range(N_DEV):
            @pl.when(s != me)
            def _():
                for t in range(n_r):
                    summed(t, s, mine(t), me).start()

    def finish():
        for s in range(N_DEV):
            @pl.when(s != me)
            def _():
                for t in range(n_r):
                    summed(t, s, rows_of(t, s), s).wait_recv()
                    summed(t, s, mine(t), me).wait_send()
                for t in range(n_w + n_r):
                    partial(t, s, me).wait_send()
        for t in range(n_r):
            rows_out[t][...] = rows_sum[t][...]

    return send_partials, reduce_and_send_sums, finish


def _small_allreduce_scratch(whole, by_rows):
    n_sem = len(whole) + 2 * len(by_rows)
    return ([pltpu.VMEM((N_DEV,) + a.shape, F32) for a in whole]
            + [pltpu.VMEM((N_DEV, a.shape[0] // N_DEV, a.shape[1]), F32) for a in by_rows]
            + [pltpu.VMEM(a.shape, F32) for a in by_rows]
            + [pltpu.SemaphoreType.DMA((n_sem, N_DEV)), pltpu.SemaphoreType.DMA((n_sem, N_DEV))])


def _adam_update(g, w, m, v):
    m_new = ADAM_B1 * m + (1.0 - ADAM_B1) * g
    v_new = ADAM_B2 * v + (1.0 - ADAM_B2) * jnp.square(g)
    m_hat = m_new / (1.0 - ADAM_B1 ** ADAM_STEP)
    v_hat = v_new / (1.0 - ADAM_B2 ** ADAM_STEP)
    return -ADAM_LR * (m_hat / (jnp.sqrt(v_hat) + ADAM_EPS) + ADAM_WD * w), m_new, v_new


def _adamw_groups(name, groups):
    n = len(groups)

    def body(*refs):
        for k in range(n):
            g_ref, w_ref, m_ref, v_ref = refs[4 * k:4 * k + 4]
            g_out, d_out, m_out, v_out = refs[4 * n + 4 * k:4 * n + 4 * k + 4]
            g = g_ref[...]
            g_out[...] = g
            d_out[...], m_out[...], v_out[...] = _adam_update(g, w_ref[...], m_ref[...], v_ref[...])

    flat = [a for grp in groups for a in grp]
    out = pl.pallas_call(body, name=name,
                         out_shape=[jax.ShapeDtypeStruct(grp[0].shape, F32) for grp in groups for _ in range(4)])(*flat)
    return [out[4 * k:4 * k + 4] for k in range(n)]


def _adamw(name, parts, w, m, v, row_block):
    n_src, rows, cols = parts.shape
    rb = min(row_block, rows)

    def body(p_ref, w_ref, m_ref, v_ref, g_out, d_out, m_out, v_out):
        g = p_ref[0].astype(F32)
        for s in range(1, n_src):
            g = g + p_ref[s].astype(F32)
        g_out[...] = g
        d_out[...], m_out[...], v_out[...] = _adam_update(g, w_ref[...], m_ref[...], v_ref[...])

    spec = pl.BlockSpec((rb, cols), lambda i: (i, 0))
    return pl.pallas_call(
        body, name=name, grid=(rows // rb,),
        in_specs=[pl.BlockSpec((n_src, rb, cols), lambda i: (0, i, 0)), spec, spec, spec],
        out_specs=[spec] * 4, out_shape=[jax.ShapeDtypeStruct((rows, cols), F32)] * 4,
        compiler_params=pltpu.CompilerParams(dimension_semantics=("parallel",), vmem_limit_bytes=VMEM_LIMIT),
    )(parts, w, m, v)


def _block_diag(blocks):
    g, a, b = blocks.shape
    eye = jnp.eye(g, dtype=blocks.dtype)
    return (eye[:, None, :, None] * blocks[:, :, None, :]).reshape(g * a, g * b)


def kernel(x, p, norm_mix_g, w_in, pool_w, pool_b, pool_scale, conv_w, conv_b, gate_a_w, gate_a_b, gate_x_w, gate_x_b, lru_L, w_out, norm_mlp_g, w_up, w_down, norm_ple_g, w_ple_gate, b_ple_gate, w_ple_proj, norm_final_g, loss_target, m_norm_mix_g, m_w_in, m_pool_w, m_pool_b, m_pool_scale, m_conv_w, m_conv_b, m_gate_a_w, m_gate_a_b, m_gate_x_w, m_gate_x_b, m_lru_L, m_w_out, m_norm_mlp_g, m_w_up, m_w_down, m_norm_ple_g, m_w_ple_gate, m_b_ple_gate, m_w_ple_proj, m_norm_final_g, v_norm_mix_g, v_w_in, v_pool_w, v_pool_b, v_pool_scale, v_conv_w, v_conv_b, v_gate_a_w, v_gate_a_b, v_gate_x_w, v_gate_x_b, v_lru_L, v_w_out, v_norm_mlp_g, v_w_up, v_w_down, v_norm_ple_g, v_w_ple_gate, v_b_ple_gate, v_w_ple_proj, v_norm_final_g):
    t_len, d = x.shape[1], x.shape[2]
    tbs = {k: min(v, t_len) for k, v in TIME_BLOCKS.items()}
    me = _my_index()

    win_g, wout_g, convw_g = _gather("gather_mixer_weights", [w_in[0].astype(BF16), w_out[0].astype(BF16), conv_w[0]])
    w_in_f = jnp.transpose(win_g, (1, 0, 2)).reshape(d, -1)
    conv_w_f = jnp.transpose(convw_g, (1, 0, 2)).reshape(convw_g.shape[1], -1)
    wp_bd = _block_diag(pool_w[0]).astype(BF16)
    wg_bd = jnp.concatenate([_block_diag(gate_a_w[0]), _block_diag(gate_x_w[0])], axis=1).astype(BF16)
    gate_b2 = jnp.concatenate([gate_a_b.reshape(1, -1), gate_x_b.reshape(1, -1)], axis=1)
    mixer_small = (norm_mix_g, w_in_f, wp_bd, pool_b.reshape(1, -1), pool_scale, conv_w_f, conv_b, wg_bd, gate_b2, lru_L,
                   wout_g.reshape(-1, d))

    x2 = x[0]
    later = [w_up[0].astype(BF16), w_down[0].astype(BF16), w_ple_gate[0].astype(BF16), w_ple_proj[0].astype(BF16)]
    h1, z1, proj, hs, cat, lru_saved, dpool_saved, wup_g, wdn_g, wgate_g, wproj_g = _mix_fwd(
        x2, *mixer_small, later, [_core_major_slot, _core_major_slot, None, None], GATHER_FORWARD_AT, tbs['mix_fwd'])
    w_down_f = wdn_g.reshape(-1, d)
    w_proj_f = jnp.transpose(wproj_g, (1, 0, 2)).reshape(wproj_g.shape[1], -1)
    h2, z2, up = _mlp_fwd(h1, norm_mlp_g, wup_g, w_down_f, tbs['mlp_fwd'])
    dh2, ple_vec, dw_gate, dw_proj = _ple(h2, p[0, 0], loss_target[0], norm_ple_g, wgate_g.reshape(-1, d), b_ple_gate,
                                          w_proj_f, norm_final_g.reshape(1, -1), tbs['ple'])
    everyone = list(range(N_DEV))
    n_proj = w_ple_proj.shape[2]
    dz2_0, dw_up_0, dw_down_0 = _mlp_bwd_part(
        0, MLP_BWD_SPLIT, dh2, z2, up, wup_g, w_down_f, None, h1, norm_mlp_g, _scatter_plan([], [], []), tbs['mlp_bwd'])
    half = N_DEV // MLP_BWD_SPLIT
    south = [_device_of_core_major_slot(k) for k in range(half)]
    north = [_device_of_core_major_slot(k) for k in range(half, N_DEV)]
    scatter = _scatter_plan(
        [dw_up_0, dw_down_0, dw_gate.reshape(N_DEV, -1, d), jnp.transpose(dw_proj.reshape(-1, N_DEV, n_proj), (1, 0, 2))],
        [south, south, everyone, everyone], [None, None, None, None])
    dh1, mlp_vec, dw_up_1, dw_down_1, recv_up, recv_down, recv_gate, recv_proj = _mlp_bwd_part(
        1, MLP_BWD_SPLIT, dh2, z2, up, wup_g, w_down_f, dz2_0, h1, norm_mlp_g, scatter, tbs['mlp_bwd'])
    dw_out, _, _ = _wgrad("wgrad_out", cat, dh1, [], [], tbs['wgrad_out'])
    scatter = _scatter_plan([dw_up_1, dw_down_1, dw_out.reshape(N_DEV, -1, d)], [north, north, everyone],
                            [recv_up, recv_down, None])
    dproj, v512, dpw, dga, dgx, recv_up, recv_down, recv_out = _mix_bwd(
        dh1, proj, hs, lru_saved, dpool_saved, wp_bd, pool_b.reshape(1, -1), pool_scale, conv_w_f, wg_bd, lru_L, wout_g.reshape(-1, d),
        scatter, tbs['mix_bwd'])
    rows1024 = jnp.concatenate([jnp.zeros((1, d), F32), mlp_vec[0:1], ple_vec[1:2], ple_vec[0:1], ple_vec[2:4],
                                jnp.zeros((2, d), F32)], axis=0)
    dw_in, (rows1024, rows512), (g_pool_w, g_gate_a_w, g_gate_x_w) = _wgrad(
        "wgrad_in", z1, dproj, [rows1024, v512], [dpw, dga, dgx], tbs['wgrad_in'])
    n_in = w_in.shape[2]
    scatter = _scatter_plan([jnp.transpose(dw_in.reshape(d, N_DEV, n_in), (1, 0, 2))], [everyone], [None])
    dx, in_vec, recv_in = _in_bwd(dproj, x2, dh1, norm_mix_g, w_in_f, scatter, tbs['in_bwd'])
    rows1024 = jnp.concatenate([in_vec[0:1], rows1024[1:]], axis=0)
    received = [recv_in, recv_out, recv_up, recv_down, recv_gate, recv_proj]

    shard_w = [w_in[0], w_out[0], w_up[0], w_down[0], w_ple_gate[0], w_ple_proj[0]]
    shard_m = [m_w_in[0], m_w_out[0], m_w_up[0], m_w_down[0], m_w_ple_gate[0], m_w_ple_proj[0]]
    shard_v = [v_w_in[0], v_w_out[0], v_w_up[0], v_w_down[0], v_w_ple_gate[0], v_w_ple_proj[0]]
    names = ["w_in", "w_out", "w_up", "w_down", "w_ple_gate", "w_ple_proj"]
    res = {}
    for nm, parts, w_s, m_s, v_s in zip(names, received, shard_w, shard_m, shard_v):
        res[nm] = [r[None] for r in _adamw("adamw_" + nm, parts, w_s, m_s, v_s, ADAM_ROW_BLOCK)]

    def rows_of_1024(a, b, c, e, f):
        return jnp.concatenate([a, b, c, e, f.reshape(1, -1), jnp.zeros((3, d), F32)], axis=0)

    def rows_of_512(scale, bias, cb, lru, ga, gx):
        z = jnp.zeros((1, 512), F32)
        return jnp.concatenate([scale, bias.reshape(1, -1), cb, lru, z, z, z, z, ga.reshape(1, -1), gx.reshape(1, -1),
                                z, z, z, z, z, z], axis=0)

    n_conv = conv_w.shape[2]
    groups = [
        (rows1024, *[rows_of_1024(*t) for t in (
            (norm_mix_g, norm_mlp_g, norm_ple_g, b_ple_gate, norm_final_g),
            (m_norm_mix_g, m_norm_mlp_g, m_norm_ple_g, m_b_ple_gate, m_norm_final_g),
            (v_norm_mix_g, v_norm_mlp_g, v_norm_ple_g, v_b_ple_gate, v_norm_final_g))]),
        (rows512, *[rows_of_512(*t) for t in (
            (pool_scale, pool_b, conv_b, lru_L, gate_a_b, gate_x_b),
            (m_pool_scale, m_pool_b, m_conv_b, m_lru_L, m_gate_a_b, m_gate_x_b),
            (v_pool_scale, v_pool_b, v_conv_b, v_lru_L, v_gate_a_b, v_gate_x_b))]),
        (g_pool_w, *[a.reshape(-1, a.shape[-1]) for a in (pool_w, m_pool_w, v_pool_w)]),
        (g_gate_a_w, *[a.reshape(-1, a.shape[-1]) for a in (gate_a_w, m_gate_a_w, v_gate_a_w)]),
        (g_gate_x_w, *[a.reshape(-1, a.shape[-1]) for a in (gate_x_w, m_gate_x_w, v_gate_x_w)]),
        (lax.dynamic_slice_in_dim(rows512[4:8], me * n_conv, n_conv, axis=1), conv_w[0], m_conv_w[0], v_conv_w[0]),
    ]
    r1024, r512, r_pool, r_ga, r_gx, r_conv = _adamw_groups("adamw_small", groups)
    loss = rows1024[5, 0]
    for k, nm in enumerate(["norm_mix_g", "norm_mlp_g", "norm_ple_g", "b_ple_gate"]):
        res[nm] = [a[k:k + 1] for a in r1024]
    res["norm_final_g"] = [a[4] for a in r1024]
    res["pool_scale"] = [a[0:1] for a in r512]
    res["pool_b"] = [a[1:2].reshape(pool_b.shape) for a in r512]
    res["conv_b"] = [a[2:3] for a in r512]
    res["lru_L"] = [a[3:4] for a in r512]
    res["gate_a_b"] = [a[8:9].reshape(gate_a_b.shape) for a in r512]
    res["gate_x_b"] = [a[9:10].reshape(gate_x_b.shape) for a in r512]
    res["pool_w"] = [a.reshape(pool_w.shape) for a in r_pool]
    res["gate_a_w"] = [a.reshape(gate_a_w.shape) for a in r_ga]
    res["gate_x_w"] = [a.reshape(gate_x_w.shape) for a in r_gx]
    res["conv_w"] = [a[None] for a in r_conv]
    order = ["norm_mix_g", "w_in", "pool_w", "pool_b", "pool_scale", "conv_w", "conv_b", "gate_a_w", "gate_a_b",
             "gate_x_w", "gate_x_b", "lru_L", "w_out", "norm_mlp_g", "w_up", "w_down", "norm_ple_g", "w_ple_gate",
             "b_ple_gate", "w_ple_proj", "norm_final_g"]
    return (loss, dx[None], *[res[nm][kind] for kind in range(4) for nm in order])
```

```python
import jax
import jax.numpy as jnp
from jax import lax
from jax.experimental import pallas as pl
from jax.experimental.pallas import tpu as pltpu

F32 = jnp.float32
BF16 = jnp.bfloat16
MESH = pl.DeviceIdType.MESH

N_DEV = 8
RMS_EPS = 1e-6
LRU_C = 8.0
POOL_WINDOWS = (2, 4, 8, 16)
N_POOL_GROUPS = 4
LRU_HEADS = 8
HALO = 16
SUB = 8
GELU_C0 = 0.7978845608028654
GELU_C1 = 0.044715

ADAM_LR = 0.001
ADAM_B1 = 0.9
ADAM_B2 = 0.999
ADAM_EPS = 1e-08
ADAM_WD = 0.01
ADAM_STEP = 10

VMEM_LIMIT = 60 * 1024 * 1024
TIME_BLOCKS = dict(mix_fwd=512, mlp_fwd=512, ple=512, mlp_bwd=512, wgrad_out=1024, mix_bwd=512, wgrad_in=1024, in_bwd=512)
ADAM_ROW_BLOCK = 256
SCAN_UNROLL = 4
MLP_BWD_SPLIT = 2
GATHER_FORWARD_AT = (0.5, 0.875, 1.0, 1.0)


def _params(n_arbitrary=1):
    return pltpu.CompilerParams(dimension_semantics=("arbitrary",) * n_arbitrary, vmem_limit_bytes=VMEM_LIMIT)


def _dot(a, b):
    return jnp.dot(a, b, preferred_element_type=F32)


def _dot_nt(a, b):
    return lax.dot_general(a, b, (((1,), (1,)), ((), ())), preferred_element_type=F32)


def _dot_tn(a, b):
    return lax.dot_general(a, b, (((0,), (0,)), ((), ())), preferred_element_type=F32)


def _rms_fwd(x, g):
    r = lax.rsqrt(jnp.mean(x * x, axis=-1, keepdims=True) + RMS_EPS)
    xh = x * r
    return xh * g, xh, r


def _rms_bwd(xh, r, g, dz):
    dxh = dz * g
    return r * (dxh - xh * jnp.mean(dxh * xh, axis=-1, keepdims=True))


def _colsum(a):
    return jnp.sum(a, axis=0, keepdims=True)


def _sigmoid(a):
    return 0.5 * jnp.tanh(0.5 * a) + 0.5


def _gelu_parts(u):
    u2 = u * u
    th = jnp.tanh(GELU_C0 * (u + GELU_C1 * u * u2))
    gel = 0.5 * u * (1.0 + th)
    dgel = 0.5 * (1.0 + th) + 0.5 * u * (1.0 - th * th) * (GELU_C0 * (1.0 + 3.0 * GELU_C1 * u2))
    return gel, dgel


def _my_index():
    return 4 * lax.axis_index("x") + 2 * lax.axis_index("y") + lax.axis_index("c")


def _all_to_all(srcs_of, dsts, send_sems, recv_sems, local_sems, dests=None):
    n = len(dsts)
    me = _my_index()
    dests = [list(range(N_DEV))] * n if dests is None else dests

    def remote(t, s):
        return pltpu.make_async_remote_copy(
            src_ref=srcs_of[t](s), dst_ref=dsts[t].at[me], send_sem=send_sems.at[t, s], recv_sem=recv_sems.at[t, me],
            device_id=(s // 4, (s // 2) % 2, s % 2), device_id_type=MESH)

    def arrival(t, s):
        return pltpu.make_async_remote_copy(
            src_ref=srcs_of[t](dests[t][0]), dst_ref=dsts[t].at[s], send_sem=send_sems.at[t, s],
            recv_sem=recv_sems.at[t, s], device_id=(s // 4, (s // 2) % 2, s % 2), device_id_type=MESH)

    def local(t, s):
        return pltpu.make_async_copy(srcs_of[t](s), dsts[t].at[s], local_sems.at[t])

    def start():
        for s in range(N_DEV):
            to_s = [t for t in range(n) if s in dests[t]]

            @pl.when(s == me)
            def _():
                for t in to_s:
                    local(t, s).start()

            @pl.when(s != me)
            def _():
                for t in to_s:
                    remote(t, s).start()

    def wait():
        for s in range(N_DEV):
            to_s = [t for t in range(n) if s in dests[t]]

            @pl.when(s == me)
            def _():
                for t in to_s:
                    local(t, s).wait()
                    for src in range(N_DEV):
                        if src != s:
                            arrival(t, src).wait_recv()

            @pl.when(s != me)
            def _():
                for t in to_s:
                    remote(t, s).wait_send()

    return start, wait


N_GATHER_COPIES = 7


def _core_major_slot(dev):
    return 4 * dev[2] + 2 * dev[0] + dev[1]


def _device_of_core_major_slot(k):
    return (k % 4) * 2 + k // 4


def _two_level_gather(srcs, dsts, send_sems, recv_sems, local_sems, slots=None):
    n = len(dsts)
    x, y, c = lax.axis_index("x"), lax.axis_index("y"), lax.axis_index("c")
    me, sibling = (x, y, c), (x, y, 1 - c)
    chips = [(1 - x, y), (x, 1 - y), (1 - x, 1 - y)]

    def slot(t, dev):
        return 4 * dev[0] + 2 * dev[1] + dev[2] if slots is None or slots[t] is None else slots[t](dev)

    def copy(t, k, block, to, src=None):
        return pltpu.make_async_remote_copy(
            src_ref=dsts[t].at[slot(t, block)] if src is None else src, dst_ref=dsts[t].at[slot(t, block)],
            send_sem=send_sems.at[t, k], recv_sem=recv_sems.at[t, k], device_id=to, device_id_type=MESH)

    def local(t):
        return pltpu.make_async_copy(srcs[t], dsts[t].at[slot(t, me)], local_sems.at[t])

    def start():
        for t in range(n):
            local(t).start()
            for j, chip in enumerate(chips):
                copy(t, 1 + j, me, (*chip, c), src=srcs[t]).start()
            copy(t, 0, me, sibling, src=srcs[t]).start()

    def forward(t):
        for j, chip in enumerate(chips):
            copy(t, 1 + j, (*chip, c), me).wait_recv()
            copy(t, 4 + j, (*chip, c), sibling).start()

    def finish():
        for t in range(n):
            copy(t, 0, sibling, me).wait_recv()
            for j, chip in enumerate(chips):
                copy(t, 4 + j, (*chip, 1 - c), me).wait_recv()
            copy(t, 0, me, sibling, src=srcs[t]).wait_send()
            for j, chip in enumerate(chips):
                copy(t, 1 + j, me, (*chip, c), src=srcs[t]).wait_send()
                copy(t, 4 + j, (*chip, c), sibling).wait_send()
            local(t).wait()

    return start, forward, finish


def _hosted_gather(i, nb, forward_at, srcs, dsts, sems, slots=None):
    start, forward, finish = _two_level_gather(srcs, dsts, *sems, slots)

    def after_step():
        for t, f in enumerate(forward_at):
            @pl.when(i == min(nb - 1, int(f * nb)))
            def _():
                forward(t)

        @pl.when(i == nb - 1)
        def _():
            finish()

    return start, after_step


def _gather_scratch(n):
    return [pltpu.SemaphoreType.DMA((n, N_GATHER_COPIES)), pltpu.SemaphoreType.DMA((n, N_GATHER_COPIES)),
            pltpu.SemaphoreType.DMA((n,))]


def _gather(name, srcs):
    n = len(srcs)

    def body(*refs):
        start, forward, finish = _two_level_gather(refs[:n], refs[n:2 * n], *refs[2 * n:])
        start()
        for t in range(n):
            forward(t)
        finish()

    any_spec = pl.BlockSpec(memory_space=pl.ANY)
    return pl.pallas_call(
        body, name=name, in_specs=[any_spec] * n, out_specs=[any_spec] * n,
        out_shape=[jax.ShapeDtypeStruct((N_DEV,) + a.shape, a.dtype) for a in srcs], scratch_shapes=_gather_scratch(n),
    )(*srcs)


def _scatter_plan(blocks, dests, landing):
    return dict(blocks=list(blocks), dests=[list(dd) for dd in dests], landing=list(landing))


def _scatter_args(plan):
    return plan['blocks'] + [a for a in plan['landing'] if a is not None]


def _scatter_out_shape(plan):
    return [jax.ShapeDtypeStruct((N_DEV,) + b.shape[1:], b.dtype) for b in plan['blocks']]


def _scatter_aliases(plan, first_in, first_out):
    given = [t for t, a in enumerate(plan['landing']) if a is not None]
    return {first_in + len(plan['blocks']) + k: first_out + t for k, t in enumerate(given)}


def _scatter_ops(plan, in_refs, out_refs, sems):
    n = len(plan['blocks'])
    srcs_of = [(lambda s, r=in_refs[t], dd=plan['dests'][t]: r.at[dd.index(s)]) for t in range(n)]
    return _all_to_all(srcs_of, out_refs, *sems, dests=plan['dests'])


def _exchange_scratch(n):
    return [pltpu.SemaphoreType.DMA((n, N_DEV)), pltpu.SemaphoreType.DMA((n, N_DEV)), pltpu.SemaphoreType.DMA((n,))]


def _const_spec(shape):
    nd = len(shape)
    return pl.BlockSpec(shape, lambda i: (0,) * nd, pipeline_mode=pl.Buffered(1))


def _pool_windows(up_ext, n, forward):
    sh = (lambda k: k) if forward else (lambda k: n - k)
    s2 = up_ext + pltpu.roll(up_ext, sh(1), 0)
    t4 = s2[:, 128:]
    s4 = t4 + pltpu.roll(t4, sh(2), 0)
    t8 = s4[:, 128:]
    s8 = t8 + pltpu.roll(t8, sh(4), 0)
    t16 = s8[:, 128:]
    s16 = t16 + pltpu.roll(t16, sh(8), 0)
    return jnp.concatenate([s2[:, :128], s4[:, :128], s8[:, :128], s16], axis=1)


def _inv_count_head():
    t = jnp.arange(1, HALO + 1, dtype=F32)[:, None]
    return jnp.concatenate([jnp.broadcast_to(1.0 / jnp.minimum(t, float(w)), (HALO, 128)) for w in POOL_WINDOWS], axis=1)


def _scale_by_inv_count(v, is_first_block, inv_head):
    inv_row = jnp.concatenate([jnp.full((1, 128), 1.0 / w, F32) for w in POOL_WINDOWS], axis=1)
    head = v[0:HALO] * jnp.where(is_first_block, inv_head, inv_row)
    return jnp.concatenate([head, v[HALO:] * inv_row], axis=0)


def _lru_decay(r, a, c_l, first_row):
    a2 = a * a
    m2 = -jnp.tanh(c_l * r) * (a2 + 1.0)
    return a2, m2, jnp.where(first_row, 1.0, jnp.sqrt(m2))


def _log_sigmoid(v):
    return -(jnp.maximum(-v, 0.0) + jnp.log1p(jnp.exp(-jnp.abs(v))))


def _conv_fwd(ul_ext, cw, cb):
    return (cb + cw[3:4, :] * ul_ext + cw[2:3, :] * pltpu.roll(ul_ext, 1, 0)
            + cw[1:2, :] * pltpu.roll(ul_ext, 2, 0) + cw[0:1, :] * pltpu.roll(ul_ext, 3, 0))


def _mix_fwd(x, g_mix, w_in, wp_bd, pool_b, pool_scale, conv_w, conv_b, wg_bd, gate_b, lru_l, w_out, gather_srcs,
             gather_slots, forward_at, tb):
    t_len, d = x.shape
    nb = t_len // tb
    n_g = len(gather_srcs)

    def body(*refs):
        (x_ref, g_ref, win_ref, wp_ref, pb_ref, ps_ref, cw_ref, cb_ref, wg_ref, gb_ref, l_ref, wout_ref,
         invh_ref) = refs[:13]
        gsrc = refs[13:13 + n_g]
        h1_ref, z1_ref, proj_ref, hs_ref, cat_ref, lru_ref, dpool_ref = refs[13 + n_g:20 + n_g]
        gdst = refs[20 + n_g:20 + 2 * n_g]
        ext_ref, a_ref, b_ref, hc_ref, send_sems, recv_sems, local_sems = refs[20 + 2 * n_g:]
        i = pl.program_id(0)
        start_gather, after_step = _hosted_gather(i, nb, forward_at, gsrc, gdst, (send_sems, recv_sems, local_sems),
                                                  gather_slots)

        @pl.when(i == 0)
        def _():
            start_gather()
            ext_ref[0:HALO, :] = jnp.zeros((HALO, 1024), F32)
            hc_ref[...] = jnp.zeros_like(hc_ref)

        xv = x_ref[...]
        z, _, _ = _rms_fwd(xv, g_ref[...])
        zb = z.astype(BF16)
        z1_ref[...] = zb
        proj = _dot_nt(zb, win_ref[...])
        proj_ref[...] = proj
        ext_ref[HALO:, :] = proj[:, 0:1024]
        ug = proj[:, 1024:1536]
        n = tb + HALO
        up_ext = ext_ref[:, 0:512]
        win = _pool_windows(up_ext, n, True)[HALO:]
        dpool = _scale_by_inv_count(win, i == 0, invh_ref[...]) - proj[:, 0:512]
        dpoolb = dpool.astype(BF16)
        dpool_ref[...] = dpoolb
        q = _dot(dpoolb, wp_ref[...]) + pb_ref[...]
        y_pool = q * ps_ref[...]
        xb = _conv_fwd(ext_ref[:, 512:1024], cw_ref[...], cb_ref[...])[HALO:]
        first_row = (i * tb + lax.broadcasted_iota(jnp.int32, (tb, 1), 0)) == 0
        c_l = LRU_C * _log_sigmoid(l_ref[...])
        gp = _dot(xb.astype(BF16), wg_ref[...]) + gb_ref[...]
        r = _sigmoid(gp[:, :512])
        ig = _sigmoid(gp[:, 512:])
        a = jnp.exp(c_l * r)
        _, _, mult = _lru_decay(r, a, c_l, first_row)
        lru_ref[:, 0:512] = xb
        lru_ref[:, 512:1024] = r
        lru_ref[:, 1024:1536] = ig
        lru_ref[:, 1536:2048] = a
        a_ref[...] = a
        b_ref[...] = mult * (ig * xb)
        row = lax.broadcasted_iota(jnp.int32, (SUB, 512), 0)

        def group(j, hprev):
            o = pl.multiple_of(j * SUB, SUB)
            a8 = a_ref[pl.ds(o, SUB), :]
            b8 = b_ref[pl.ds(o, SUB), :]
            for sh in (1, 2, 4):
                ash = jnp.where(row >= sh, pltpu.roll(a8, sh, 0), 1.0)
                bsh = jnp.where(row >= sh, pltpu.roll(b8, sh, 0), 0.0)
                b8 = a8 * bsh + b8
                a8 = a8 * ash
            h8 = a8 * hprev + b8
            hs_ref[pl.ds(o, SUB), :] = h8
            return jnp.broadcast_to(h8[SUB - 1:SUB, :], (SUB, 512))

        def trip(k, carry):
            for u in range(SCAN_UNROLL):
                carry = group(k * SCAN_UNROLL + u, carry)
            return carry

        hc_ref[...] = lax.fori_loop(0, tb // (SUB * SCAN_UNROLL), trip, hc_ref[...])
        gel, dgel = _gelu_parts(ug)
        lru_ref[:, 2048:2560] = gel
        lru_ref[:, 2560:3072] = dgel
        y_lru = hs_ref[...] * gel
        catb = jnp.concatenate([y_pool, y_lru], axis=1).astype(BF16)
        cat_ref[...] = catb
        h1_ref[...] = xv + _dot(catb, wout_ref[...])
        ext_ref[0:HALO, :] = ext_ref[tb:tb + HALO, :]

        after_step()

    row_spec = lambda w: pl.BlockSpec((tb, w), lambda i: (i, 0))
    any_spec = pl.BlockSpec(memory_space=pl.ANY)
    smalls = [g_mix, w_in, wp_bd, pool_b, pool_scale, conv_w, conv_b, wg_bd, gate_b, lru_l, w_out, _inv_count_head()]
    return pl.pallas_call(
        body, name="mix_fwd", grid=(nb,),
        in_specs=[row_spec(d)] + [_const_spec(s.shape) for s in smalls] + [any_spec] * n_g,
        out_specs=[row_spec(d), row_spec(d), row_spec(1536), row_spec(512), row_spec(1024), row_spec(3072), row_spec(512)]
        + [any_spec] * n_g,
        out_shape=[jax.ShapeDtypeStruct((t_len, d), F32), jax.ShapeDtypeStruct((t_len, d), BF16),
                   jax.ShapeDtypeStruct((t_len, 1536), F32), jax.ShapeDtypeStruct((t_len, 512), F32),
                   jax.ShapeDtypeStruct((t_len, 1024), BF16), jax.ShapeDtypeStruct((t_len, 3072), F32),
                   jax.ShapeDtypeStruct((t_len, 512), BF16)]
        + [jax.ShapeDtypeStruct((N_DEV,) + s.shape, s.dtype) for s in gather_srcs],
        scratch_shapes=[pltpu.VMEM((tb + HALO, 1024), F32), pltpu.VMEM((tb, 512), F32), pltpu.VMEM((tb, 512), F32),
                        pltpu.VMEM((SUB, 512), F32)] + _gather_scratch(n_g),
        compiler_params=_params(),
    )(x, *smalls, *gather_srcs)


def _mlp_fwd(h1, g_mlp, w_up, w_down, tb):
    t_len, d = h1.shape
    nb = t_len // tb
    n_chunk, _, fc = w_up.shape

    def body(h1_ref, g_ref, wup_ref, wdn_ref, h2_ref, z2_ref, up_ref):
        xv = h1_ref[...]
        z, _, _ = _rms_fwd(xv, g_ref[...])
        zb = z.astype(BF16)
        z2_ref[...] = zb
        acc = xv
        for c in range(n_chunk):
            u = _dot(zb, wup_ref[c])
            up_ref[:, c * fc:(c + 1) * fc] = u.astype(BF16)
            act = jnp.square(jnp.maximum(u, 0.0)).astype(BF16)
            acc = acc + _dot(act, wdn_ref[c * fc:(c + 1) * fc, :])
        h2_ref[...] = acc

    row_spec = lambda w: pl.BlockSpec((tb, w), lambda i: (i, 0))
    return pl.pallas_call(
        body, name="mlp_fwd", grid=(nb,),
        in_specs=[row_spec(d), _const_spec(g_mlp.shape), _const_spec(w_up.shape), _const_spec(w_down.shape)],
        out_specs=[row_spec(d), row_spec(d), row_spec(n_chunk * fc)],
        out_shape=[jax.ShapeDtypeStruct((t_len, d), F32), jax.ShapeDtypeStruct((t_len, d), BF16),
                   jax.ShapeDtypeStruct((t_len, n_chunk * fc), BF16)],
        compiler_params=_params(),
    )(h1, g_mlp, w_up, w_down)


def _ple(h2, p, target, g_ple, w_gate, b_gate, w_proj, g_final, tb):
    t_len, d = h2.shape
    nb = t_len // tb
    pd = p.shape[1]

    def body(h2_ref, p_ref, tgt_ref, g_ref, wg_ref, bg_ref, wp_ref, gf_ref,
             dh2_ref, vec_ref, dwg_out, dwp_out, dwg_acc, dwp_acc, dwg_stage, dwp_stage):
        i = pl.program_id(0)

        @pl.when(i == 0)
        def _():
            vec_ref[...] = jnp.zeros_like(vec_ref)
            dwg_acc[...] = jnp.zeros_like(dwg_acc)
            dwp_acc[...] = jnp.zeros_like(dwp_acc)

        h2 = h2_ref[...]
        g2 = g_ref[...]
        z3, xh2, r2 = _rms_fwd(h2, g2)
        z3b = z3.astype(BF16)
        gate = _sigmoid(_dot(z3b, wg_ref[...]) + bg_ref[...])
        pb = p_ref[...].astype(BF16)
        pp = _dot(pb, wp_ref[...])
        h3 = h2 + gate * pp
        gf = gf_ref[...]
        y, xh3, r3 = _rms_fwd(h3, gf)
        err = y - tgt_ref[...]
        loss_rows = jnp.mean(err * err, axis=-1, keepdims=True)
        dy = err * (1.0 / d)
        dh3 = _rms_bwd(xh3, r3, gf, dy)
        dgl = (dh3 * pp) * (gate * (1.0 - gate))
        dpp = dh3 * gate
        dglb = dgl.astype(BF16)
        dwg_acc[...] += _dot_tn(z3b, dglb)
        dwp_acc[...] += _dot_tn(pb, dpp.astype(BF16))
        dz3 = _dot_nt(dglb, wg_ref[...])
        dh2_ref[...] = dh3 + _rms_bwd(xh2, r2, g2, dz3)
        vec_ref[0:1, :] += _colsum(dgl)
        vec_ref[1:2, :] += _colsum(dz3 * xh2)
        vec_ref[2:3, :] += _colsum(dy * xh3)
        vec_ref[3:4, :] += 0.5 * jnp.sum(loss_rows)

        @pl.when(i == nb - 1)
        def _():
            dwg_stage[...] = dwg_acc[...].astype(BF16)
            dwp_stage[...] = dwp_acc[...].astype(BF16)
            pltpu.sync_copy(dwg_stage, dwg_out)
            pltpu.sync_copy(dwp_stage, dwp_out)

    row_spec = lambda w: pl.BlockSpec((tb, w), lambda i: (i, 0))
    any_spec = pl.BlockSpec(memory_space=pl.ANY)
    smalls = [g_ple, w_gate, b_gate, w_proj, g_final]
    return pl.pallas_call(
        body, name="ple_fwd_bwd", grid=(nb,),
        in_specs=[row_spec(d), row_spec(pd), row_spec(d)] + [_const_spec(s.shape) for s in smalls],
        out_specs=[row_spec(d), pl.BlockSpec((8, d), lambda i: (0, 0)), any_spec, any_spec],
        out_shape=[jax.ShapeDtypeStruct((t_len, d), F32), jax.ShapeDtypeStruct((8, d), F32),
                   jax.ShapeDtypeStruct(w_gate.shape, BF16), jax.ShapeDtypeStruct(w_proj.shape, BF16)],
        scratch_shapes=[pltpu.VMEM(w_gate.shape, F32), pltpu.VMEM(w_proj.shape, F32), pltpu.VMEM(w_gate.shape, BF16),
                        pltpu.VMEM(w_proj.shape, BF16)],
        compiler_params=_params(),
    )(h2, p, target, *smalls)


def _mlp_bwd_part(part, n_part, dh2, z2, up, w_up, w_down, dz2_prev, h1, g_mlp, scatter, tb):
    t_len, d = dh2.shape
    nb = t_len // tb
    n_chunk_all, _, fc = w_up.shape
    n_chunk = n_chunk_all // n_part
    first, last = part == 0, part == n_part - 1

    def body(*refs):
        refs = list(refs)
        dh2_ref, z2_ref, up_ref, wup_ref, wdn_ref = refs[:5]
        del refs[:5]
        dzp_ref = None if first else refs.pop(0)
        h1_ref, g_ref = (refs.pop(0), refs.pop(0)) if last else (None, None)
        scatter_in = [refs.pop(0) for _ in _scatter_args(scatter)]
        out_ref = refs.pop(0)
        vec_ref = refs.pop(0) if last else None
        dwup_out, dwdn_out = refs.pop(0), refs.pop(0)
        scatter_out = [refs.pop(0) for _ in scatter['blocks']]
        dwup_acc, dwdn_acc, up_stage, dn_stage = refs[:4]
        if scatter['blocks']:
            start_scatter, wait_scatter = _scatter_ops(scatter, scatter_in, scatter_out, refs[4:])
        i = pl.program_id(0)

        @pl.when(i == 0)
        def _():
            if scatter['blocks']:
                start_scatter()
            dwup_acc[...] = jnp.zeros_like(dwup_acc)
            dwdn_acc[...] = jnp.zeros_like(dwdn_acc)
            if last:
                vec_ref[...] = jnp.zeros_like(vec_ref)

        dh2 = dh2_ref[...]
        dh2b = dh2.astype(BF16)
        z2b = z2_ref[...]
        dz2 = jnp.zeros((tb, d), F32) if first else dzp_ref[...]
        for c in range(n_chunk):
            u = up_ref[:, c * fc:(c + 1) * fc].astype(F32)
            ur = jnp.maximum(u, 0.0)
            dact = _dot_nt(dh2b, wdn_ref[c * fc:(c + 1) * fc, :])
            dupb = (dact * (2.0 * ur)).astype(BF16)
            dwdn_acc[c * fc:(c + 1) * fc, :] += _dot_tn((ur * ur).astype(BF16), dh2b)
            dwup_acc[c] += _dot_tn(z2b, dupb)
            dz2 = dz2 + _dot_nt(dupb, wup_ref[c])
        if last:
            g = g_ref[...]
            _, xh, r = _rms_fwd(h1_ref[...], g)
            out_ref[...] = dh2 + _rms_bwd(xh, r, g, dz2)
            vec_ref[0:1, :] += _colsum(dz2 * xh)
        else:
            out_ref[...] = dz2

        @pl.when(i == nb - 1)
        def _():
            for c in range(n_chunk):
                up_stage[...] = dwup_acc[c].astype(BF16)
                dn_stage[...] = dwdn_acc[c * fc:(c + 1) * fc, :].astype(BF16)
                pltpu.sync_copy(up_stage, dwup_out.at[c])
                pltpu.sync_copy(dn_stage, dwdn_out.at[c])
            if scatter['blocks']:
                wait_scatter()

    row_spec = lambda w: pl.BlockSpec((tb, w), lambda i: (i, 0))
    any_spec = pl.BlockSpec(memory_space=pl.ANY)
    args = [dh2, z2, up, w_up, w_down]
    in_specs = [row_spec(d), row_spec(d), pl.BlockSpec((tb, n_chunk * fc), lambda i: (i, part)),
                pl.BlockSpec((n_chunk, d, fc), lambda i: (part, 0, 0), pipeline_mode=pl.Buffered(1)),
                pl.BlockSpec((n_chunk * fc, d), lambda i: (part, 0), pipeline_mode=pl.Buffered(1))]
    if not first:
        args.append(dz2_prev)
        in_specs.append(row_spec(d))
    if last:
        args += [h1, g_mlp]
        in_specs += [row_spec(d), _const_spec(g_mlp.shape)]
    n_in = len(args)
    args += _scatter_args(scatter)
    in_specs += [any_spec] * len(_scatter_args(scatter))
    out_specs = [row_spec(d)]
    out_shape = [jax.ShapeDtypeStruct((t_len, d), F32)]
    if last:
        out_specs.append(pl.BlockSpec((8, d), lambda i: (0, 0)))
        out_shape.append(jax.ShapeDtypeStruct((8, d), F32))
    out_specs += [any_spec, any_spec]
    out_shape += [jax.ShapeDtypeStruct((n_chunk, d, fc), BF16), jax.ShapeDtypeStruct((n_chunk, fc, d), BF16)]
    n_out = len(out_shape)
    out_specs += [any_spec] * len(scatter['blocks'])
    out_shape += _scatter_out_shape(scatter)
    return pl.pallas_call(
        body, name=f"mlp_bwd_{part}", grid=(nb,), in_specs=in_specs, out_specs=out_specs, out_shape=out_shape,
        scratch_shapes=[pltpu.VMEM((n_chunk, d, fc), F32), pltpu.VMEM((n_chunk * fc, d), F32),
                        pltpu.VMEM((d, fc), BF16), pltpu.VMEM((fc, d), BF16)]
        + (_exchange_scratch(len(scatter['blocks'])) if scatter['blocks'] else []),
        input_output_aliases=_scatter_aliases(scatter, n_in, n_out), compiler_params=_params(),
    )(*args)


def _mix_bwd(dh1, proj, hs, lru_saved, dpool_saved, wp_bd, pool_b, pool_scale, conv_w, wg_bd, lru_l, w_out, scatter, tb):
    t_len, d = dh1.shape
    nb = t_len // tb
    n_s = len(scatter['blocks'])
    scatter_args = _scatter_args(scatter)

    def body(*refs):
        refs = list(refs)
        (dh1_ref, ul_ref, hs_ref, hsh_ref, lru_ref, dpool_ref,
         wp_ref, pb_ref, ps_ref, cw_ref, wg_ref, l_ref, wout_ref, invh_ref) = refs[:14]
        del refs[:14]
        scatter_in = refs[:len(scatter_args)]
        del refs[:len(scatter_args)]
        dproj_ref, v512_ref, dpw_ref, dga_ref, dgx_ref = refs[:5]
        recv = refs[5:5 + n_s]
        (dwp_acc, dwg_acc, v1024_ref, b_ref, gs_ref, ehead_ref, dxbhead_ref, hc_ref,
         send_sems, recv_sems, local_sems) = refs[5 + n_s:]
        i = pl.program_id(0)
        tbk = nb - 1 - i

        start_scatter, wait_scatter = _scatter_ops(scatter, scatter_in, recv, (send_sems, recv_sems, local_sems))

        @pl.when(i == 0)
        def _():
            start_scatter()
            for ref in (v512_ref, v1024_ref, dwp_acc, dwg_acc, ehead_ref, dxbhead_ref, hc_ref):
                ref[...] = jnp.zeros_like(ref)

        dcat = _dot_nt(dh1_ref[...].astype(BF16), wout_ref[...])

        has_prev = (tbk > 0).astype(F32)
        n = tb + HALO
        inv_head = invh_ref[...]

        dpoolb = dpool_ref[...]
        q = _dot(dpoolb, wp_ref[...]) + pb_ref[...]
        dyp = dcat[:, 0:512]
        dq = dyp * ps_ref[...]
        dqb = dq.astype(BF16)
        v512_ref[0:1, :] += _colsum(dyp * q)
        v512_ref[1:2, :] += _colsum(dq)
        dwp_acc[...] += _dot_tn(dpoolb, dqb)
        dd = _dot_nt(dqb, wp_ref[...])
        e = _scale_by_inv_count(dd, tbk == 0, inv_head)
        e_ext = jnp.concatenate([e, ehead_ref[...]], axis=0)
        du_pool = _pool_windows(e_ext, n, False)[0:tb] - dd
        ehead_ref[...] = e[0:HALO]

        gel, dgel = lru_ref[:, 2048:2560], lru_ref[:, 2560:3072]
        hsv = hs_ref[...]
        dcl = dcat[:, 512:1024]
        dhs = dcl * gel
        dug = dcl * hsv * dgel
        cw = cw_ref[...]
        xb, r, ig, a = lru_ref[:, 0:512], lru_ref[:, 512:1024], lru_ref[:, 1024:1536], lru_ref[:, 1536:2048]
        first_row = (tbk * tb + lax.broadcasted_iota(jnp.int32, (tb, 1), 0)) == 0
        c_l = LRU_C * _log_sigmoid(l_ref[...])
        a2, m2, mult = _lru_decay(r, a, c_l, first_row)
        b_ref[...] = dhs
        row = lax.broadcasted_iota(jnp.int32, (SUB, 512), 0)

        def group(jj, hnext):
            o = pl.multiple_of((tb // SUB - 1 - jj) * SUB, SUB)
            a8 = lru_ref[pl.ds(o, SUB), 1536:2048]
            d8 = b_ref[pl.ds(o, SUB), :]
            b8 = a8 * d8
            for sh in (1, 2, 4):
                ash = jnp.where(row < SUB - sh, pltpu.roll(a8, SUB - sh, 0), 1.0)
                bsh = jnp.where(row < SUB - sh, pltpu.roll(b8, SUB - sh, 0), 0.0)
                b8 = a8 * bsh + b8
                a8 = a8 * ash
            h8 = a8 * hnext + b8
            gs_ref[pl.ds(o, SUB), :] = d8 + jnp.where(row < SUB - 1, pltpu.roll(h8, SUB - 1, 0), hnext)
            return jnp.broadcast_to(h8[0:1, :], (SUB, 512))

        def trip(k, carry):
            for u in range(SCAN_UNROLL):
                carry = group(k * SCAN_UNROLL + u, carry)
            return carry

        hc_ref[...] = lax.fori_loop(0, tb // (SUB * SCAN_UNROLL), trip, hc_ref[...])
        gsum = gs_ref[...]
        hs_ext = jnp.concatenate([hsh_ref[...] * has_prev, hsv], axis=0)
        hprev = pltpu.roll(hs_ext, 1, 0)[SUB:]
        da = gsum * hprev
        dmult = jnp.where(first_row, 0.0, gsum * (ig * xb))
        di = gsum * mult * xb
        dxb = gsum * mult * ig
        dla = da * a - dmult * a2 * lax.rsqrt(m2)
        dr = dla * c_l
        v512_ref[3:4, :] += _colsum(dla * r)
        dgp = jnp.concatenate([dr * r * (1.0 - r), di * ig * (1.0 - ig)], axis=1)
        dgpb = dgp.astype(BF16)
        v1024_ref[0:1, :] += _colsum(dgp)
        dwg_acc[...] += _dot_tn(xb.astype(BF16), dgpb)
        dxb = dxb + _dot_nt(dgpb, wg_ref[...])
        n8 = tb + SUB
        dxb_ext = jnp.concatenate([dxb, dxbhead_ref[...]], axis=0)
        ul = ul_ref[...]
        du_lru = cw[3:4, :] * dxb
        v512_ref[7:8, :] += _colsum(dxb * ul)
        for j in range(1, 4):
            ahead = pltpu.roll(dxb_ext, n8 - j, 0)[0:tb]
            du_lru = du_lru + cw[3 - j:4 - j, :] * ahead
            v512_ref[4 + (3 - j):5 + (3 - j), :] += _colsum(ahead * ul)
        dxbhead_ref[...] = dxb[0:SUB]
        v512_ref[2:3, :] += _colsum(dxb)

        dproj_ref[...] = jnp.concatenate([du_pool, du_lru, dug], axis=1).astype(BF16)

        @pl.when(i == nb - 1)
        def _():
            v512_ref[3:4, :] = v512_ref[3:4, :] * (LRU_C * _sigmoid(-l_ref[...]))
            v512_ref[8:9, :] = v1024_ref[0:1, 0:512]
            v512_ref[9:10, :] = v1024_ref[0:1, 512:1024]
            for g in range(N_POOL_GROUPS):
                dpw_ref[g * 128:(g + 1) * 128, :] = dwp_acc[g * 128:(g + 1) * 128, g * 128:(g + 1) * 128]
            odd_head = (lax.broadcasted_iota(jnp.int32, (512, 128), 0) // 64) % 2 == 1
            for out_ref, col0 in ((dga_ref, 0), (dgx_ref, 512)):
                pairs = jnp.concatenate([dwg_acc[128 * k:128 * (k + 1), col0 + 128 * k:col0 + 128 * (k + 1)]
                                         for k in range(LRU_HEADS // 2)], axis=0)
                out_ref[...] = jnp.where(odd_head, pltpu.roll(pairs, 64, 1), pairs)[:, 0:64]
            wait_scatter()

    rev = lambda w: pl.BlockSpec((tb, w), lambda i: (nb - 1 - i, 0))
    halo = lambda rows, w: pl.BlockSpec((rows, w), lambda i: (jnp.maximum((nb - 1 - i) * (tb // rows) - 1, 0), 0))
    any_spec = pl.BlockSpec(memory_space=pl.ANY)
    smalls = [wp_bd, pool_b, pool_scale, conv_w, wg_bd, lru_l, w_out, _inv_count_head()]
    lru_third = pl.BlockSpec((tb, 512), lambda i: (nb - 1 - i, 1))
    return pl.pallas_call(
        body, name="mix_bwd", grid=(nb,),
        in_specs=[rev(d), lru_third, rev(512), halo(SUB, 512), rev(3072), rev(512)]
        + [_const_spec(s.shape) for s in smalls] + [any_spec] * len(scatter_args),
        out_specs=[rev(1536), pl.BlockSpec((16, 512), lambda i: (0, 0)), pl.BlockSpec((512, 128), lambda i: (0, 0)),
                   pl.BlockSpec((512, 64), lambda i: (0, 0)), pl.BlockSpec((512, 64), lambda i: (0, 0))]
        + [any_spec] * n_s,
        out_shape=[jax.ShapeDtypeStruct((t_len, 1536), BF16), jax.ShapeDtypeStruct((16, 512), F32),
                   jax.ShapeDtypeStruct((512, 128), F32), jax.ShapeDtypeStruct((512, 64), F32),
                   jax.ShapeDtypeStruct((512, 64), F32)]
        + _scatter_out_shape(scatter),
        scratch_shapes=[pltpu.VMEM(wp_bd.shape, F32), pltpu.VMEM(wg_bd.shape, F32), pltpu.VMEM((8, 1024), F32),
                        pltpu.VMEM((tb, 512), F32), pltpu.VMEM((tb, 512), F32), pltpu.VMEM((HALO, 512), F32),
                        pltpu.VMEM((SUB, 512), F32), pltpu.VMEM((SUB, 512), F32)]
        + _exchange_scratch(n_s),
        input_output_aliases=_scatter_aliases(scatter, 6 + len(smalls), 5), compiler_params=_params(),
    )(dh1, proj, hs, hs, lru_saved, dpool_saved, *smalls, *scatter_args)


def _wgrad(name, a, b, whole, by_rows, tb):
    t_len, m = a.shape
    n = b.shape[1]
    nb = t_len // tb
    n_w, n_r = len(whole), len(by_rows)
    n_small = n_w + n_r

    def body(*refs):
        a_ref, b_ref = refs[:2]
        small_in = refs[2:2 + n_small]
        out_ref = refs[2 + n_small]
        small_out = refs[3 + n_small:3 + 2 * n_small]
        acc_ref, stage_ref = refs[3 + 2 * n_small:5 + 2 * n_small]
        rest = refs[5 + 2 * n_small:]
        if n_small:
            send_partials, reduce_and_send_sums, finish_small = _small_allreduce(
                small_in[:n_w], small_in[n_w:], small_out[:n_w], small_out[n_w:], rest[:n_w], rest[n_w:n_small],
                rest[n_small:n_small + n_r], *rest[n_small + n_r:])
        i = pl.program_id(0)

        @pl.when(i == 0)
        def _():
            if n_small:
                send_partials()
            acc_ref[...] = jnp.zeros_like(acc_ref)

        acc_ref[...] += _dot_tn(a_ref[...], b_ref[...].astype(BF16))

        if n_small:
            @pl.when(i == nb // 2)
            def _():
                reduce_and_send_sums()

        @pl.when(i == nb - 1)
        def _():
            stage_ref[...] = acc_ref[...].astype(BF16)
            pltpu.sync_copy(stage_ref, out_ref)
            if n_small:
                finish_small()

    small = list(whole) + list(by_rows)
    vmem_spec = pl.BlockSpec(memory_space=pltpu.VMEM)
    res = pl.pallas_call(
        body, name=name, grid=(nb,),
        in_specs=[pl.BlockSpec((tb, m), lambda i: (i, 0)), pl.BlockSpec((tb, n), lambda i: (i, 0))] + [vmem_spec] * n_small,
        out_specs=[pl.BlockSpec(memory_space=pl.ANY)] + [vmem_spec] * n_small,
        out_shape=[jax.ShapeDtypeStruct((m, n), BF16)] + [jax.ShapeDtypeStruct(s_.shape, F32) for s_ in small],
        scratch_shapes=[pltpu.VMEM((m, n), F32), pltpu.VMEM((m, n), BF16)]
        + (_small_allreduce_scratch(whole, by_rows) if n_small else []),
        compiler_params=_params(),
    )(a, b, *small)
    return res[0], res[1:1 + n_w], res[1 + n_w:]


def _in_bwd(dproj, x, dh1, g_mix, w_in, scatter, tb):
    t_len, d = x.shape
    nb = t_len // tb
    n_s = len(scatter['blocks'])
    scatter_args = _scatter_args(scatter)

    def body(*refs):
        dproj_ref, x_ref, dh1_ref, g_ref, win_ref = refs[:5]
        scatter_in = refs[5:5 + len(scatter_args)]
        dx_ref, vec_ref = refs[5 + len(scatter_args):7 + len(scatter_args)]
        recv = refs[7 + len(scatter_args):7 + len(scatter_args) + n_s]
        vec_acc, send_sems, recv_sems, local_sems, vec_land, small_send, small_recv = refs[7 + len(scatter_args) + n_s:]
        start_scatter, wait_scatter = _scatter_ops(scatter, scatter_in, recv, (send_sems, recv_sems, local_sems))
        send_partials, reduce_and_send_sums, finish_small = _small_allreduce(
            [vec_acc], [], [vec_ref], [], [vec_land], [], [], small_send, small_recv)
        i = pl.program_id(0)

        @pl.when(i == 0)
        def _():
            start_scatter()
            vec_acc[...] = jnp.zeros_like(vec_acc)

        dz1 = _dot(dproj_ref[...], win_ref[...])
        g = g_ref[...]
        _, xh, rr = _rms_fwd(x_ref[...], g)
        dx_ref[...] = dh1_ref[...] + _rms_bwd(xh, rr, g, dz1)
        vec_acc[0:1, :] += _colsum(dz1 * xh)

        @pl.when(i == nb - 1)
        def _():
            send_partials()
            reduce_and_send_sums()
            finish_small()
            wait_scatter()

    row_spec = lambda w: pl.BlockSpec((tb, w), lambda i: (i, 0))
    any_spec = pl.BlockSpec(memory_space=pl.ANY)
    return pl.pallas_call(
        body, name="in_bwd", grid=(nb,),
        in_specs=[row_spec(dproj.shape[1]), row_spec(d), row_spec(d), _const_spec(g_mix.shape), _const_spec(w_in.shape)]
        + [any_spec] * len(scatter_args),
        out_specs=[row_spec(d), pl.BlockSpec((8, d), lambda i: (0, 0))] + [any_spec] * n_s,
        out_shape=[jax.ShapeDtypeStruct((t_len, d), F32), jax.ShapeDtypeStruct((8, d), F32)] + _scatter_out_shape(scatter),
        scratch_shapes=[pltpu.VMEM((8, d), F32)] + _exchange_scratch(n_s)
        + _small_allreduce_scratch([jax.ShapeDtypeStruct((8, d), F32)], []),
        input_output_aliases=_scatter_aliases(scatter, 5, 2), compiler_params=_params(),
    )(dproj, x, dh1, g_mix, w_in, *scatter_args)


def _small_allreduce(whole_in, rows_in, whole_out, rows_out, whole_land, rows_land, rows_sum, send_sems, recv_sems):
    n_w, n_r = len(whole_in), len(rows_in)
    per = [r.shape[0] // N_DEV for r in rows_in]
    me = _my_index()

    def dev(s):
        return (s // 4, (s // 2) % 2, s % 2)

    def rows_of(t, s):
        return pl.ds(s * per[t], per[t])

    def mine(t):
        return pl.ds(pl.multiple_of(me * per[t], 8), per[t])

    def partial(t, s, slot):
        if t < n_w:
            src, dst = whole_in[t], whole_land[t]
        else:
            src, dst = rows_in[t - n_w].at[rows_of(t - n_w, s)], rows_land[t - n_w]
        return pltpu.make_async_remote_copy(
            src_ref=src, dst_ref=dst.at[slot], send_sem=send_sems.at[t, s], recv_sem=recv_sems.at[t, slot],
            device_id=dev(s), device_id_type=MESH)

    def summed(t, s, rows, slot):
        return pltpu.make_async_remote_copy(
            src_ref=rows_sum[t].at[rows], dst_ref=rows_sum[t].at[rows], send_sem=send_sems.at[n_w + n_r + t, s],
            recv_sem=recv_sems.at[n_w + n_r + t, slot], device_id=dev(s), device_id_type=MESH)

    def send_partials():
        for s in range(N_DEV):
            @pl.when(s != me)
            def _():
                for t in range(n_w + n_r):
                    partial(t, s, me).start()
        for t in range(n_w):
            whole_land[t][me] = whole_in[t][...]
        for t in range(n_r):
            rows_land[t][me] = rows_in[t][mine(t), :]

    def reduce_and_send_sums():
        for s in range(N_DEV):
            @pl.when(s != me)
            def _():
                for t in range(n_w + n_r):
                    partial(t, s, s).wait_recv()
        for t in range(n_w):
            total = whole_land[t][0]
            for s in range(1, N_DEV):
                total = total + whole_land[t][s]
            whole_out[t][...] = total
        for t in range(n_r):
            total = rows_land[t][0]
            for s in range(1, N_DEV):
                total = total + rows_land[t][s]
            rows_sum[t][mine(t), :] = total
        for s in range(N_DEV):
            @pl.when(s != me)
            def _():
                for t in range(n_r):
                    summed(t, s, mine(t), me).start()

    def finish():
        for s in range(N_DEV):
            @pl.when(s != me)
            def _():
                for t in range(n_r):
                    summed(t, s, rows_of(t, s), s).wait_recv()
                    summed(t, s, mine(t), me).wait_send()
                for t in range(n_w + n_r):
                    partial(t, s, me).wait_send()
        for t in range(n_r):
            rows_out[t][...] = rows_sum[t][...]

    return send_partials, reduce_and_send_sums, finish


def _small_allreduce_scratch(whole, by_rows):
    n_sem = len(whole) + 2 * len(by_rows)
    return ([pltpu.VMEM((N_DEV,) + a.shape, F32) for a in whole]
            + [pltpu.VMEM((N_DEV, a.shape[0] // N_DEV, a.shape[1]), F32) for a in by_rows]
            + [pltpu.VMEM(a.shape, F32) for a in by_rows]
            + [pltpu.SemaphoreType.DMA((n_sem, N_DEV)), pltpu.SemaphoreType.DMA((n_sem, N_DEV))])


def _adam_update(g, w, m, v):
    m_new = ADAM_B1 * m + (1.0 - ADAM_B1) * g
    v_new = ADAM_B2 * v + (1.0 - ADAM_B2) * jnp.square(g)
    m_hat = m_new / (1.0 - ADAM_B1 ** ADAM_STEP)
    v_hat = v_new / (1.0 - ADAM_B2 ** ADAM_STEP)
    return -ADAM_LR * (m_hat / (jnp.sqrt(v_hat) + ADAM_EPS) + ADAM_WD * w), m_new, v_new


def _adamw_groups(name, groups):
    n = len(groups)

    def body(*refs):
        for k in range(n):
            g_ref, w_ref, m_ref, v_ref = refs[4 * k:4 * k + 4]
            g_out, d_out, m_out, v_out = refs[4 * n + 4 * k:4 * n + 4 * k + 4]
            g = g_ref[...]
            g_out[...] = g
            d_out[...], m_out[...], v_out[...] = _adam_update(g, w_ref[...], m_ref[...], v_ref[...])

    flat = [a for grp in groups for a in grp]
    out = pl.pallas_call(body, name=name,
                         out_shape=[jax.ShapeDtypeStruct(grp[0].shape, F32) for grp in groups for _ in range(4)])(*flat)
    return [out[4 * k:4 * k + 4] for k in range(n)]


def _adamw(name, parts, w, m, v, row_block):
    n_src, rows, cols = parts.shape
    rb = min(row_block, rows)

    def body(p_ref, w_ref, m_ref, v_ref, g_out, d_out, m_out, v_out):
        g = p_ref[0].astype(F32)
        for s in range(1, n_src):
            g = g + p_ref[s].astype(F32)
        g_out[...] = g
        d_out[...], m_out[...], v_out[...] = _adam_update(g, w_ref[...], m_ref[...], v_ref[...])

    spec = pl.BlockSpec((rb, cols), lambda i: (i, 0))
    return pl.pallas_call(
        body, name=name, grid=(rows // rb,),
        in_specs=[pl.BlockSpec((n_src, rb, cols), lambda i: (0, i, 0)), spec, spec, spec],
        out_specs=[spec] * 4, out_shape=[jax.ShapeDtypeStruct((rows, cols), F32)] * 4,
        compiler_params=pltpu.CompilerParams(dimension_semantics=("parallel",), vmem_limit_bytes=VMEM_LIMIT),
    )(parts, w, m, v)


def _block_diag(blocks):
    g, a, b = blocks.shape
    eye = jnp.eye(g, dtype=blocks.dtype)
    return (eye[:, None, :, None] * blocks[:, :, None, :]).reshape(g * a, g * b)


def kernel(x, p, norm_mix_g, w_in, pool_w, pool_b, pool_scale, conv_w, conv_b, gate_a_w, gate_a_b, gate_x_w, gate_x_b, lru_L, w_out, norm_mlp_g, w_up, w_down, norm_ple_g, w_ple_gate, b_ple_gate, w_ple_proj, norm_final_g, loss_target, m_norm_mix_g, m_w_in, m_pool_w, m_pool_b, m_pool_scale, m_conv_w, m_conv_b, m_gate_a_w, m_gate_a_b, m_gate_x_w, m_gate_x_b, m_lru_L, m_w_out, m_norm_mlp_g, m_w_up, m_w_down, m_norm_ple_g, m_w_ple_gate, m_b_ple_gate, m_w_ple_proj, m_norm_final_g, v_norm_mix_g, v_w_in, v_pool_w, v_pool_b, v_pool_scale, v_conv_w, v_conv_b, v_gate_a_w, v_gate_a_b, v_gate_x_w, v_gate_x_b, v_lru_L, v_w_out, v_norm_mlp_g, v_w_up, v_w_down, v_norm_ple_g, v_w_ple_gate, v_b_ple_gate, v_w_ple_proj, v_norm_final_g):
    t_len, d = x.shape[1], x.shape[2]
    tbs = {k: min(v, t_len) for k, v in TIME_BLOCKS.items()}
    me = _my_index()

    win_g, wout_g, convw_g = _gather("gather_mixer_weights",
                                     [w_in[0].T.astype(BF16), w_out[0].astype(BF16), conv_w[0]])
    w_in_f = win_g.reshape(-1, d)
    conv_w_f = jnp.transpose(convw_g, (1, 0, 2)).reshape(convw_g.shape[1], -1)
    wp_bd = _block_diag(pool_w[0]).astype(BF16)
    wg_bd = jnp.concatenate([_block_diag(gate_a_w[0]), _block_diag(gate_x_w[0])], axis=1).astype(BF16)
    gate_b2 = jnp.concatenate([gate_a_b.reshape(1, -1), gate_x_b.reshape(1, -1)], axis=1)
    mixer_small = (norm_mix_g, w_in_f, wp_bd, pool_b.reshape(1, -1), pool_scale, conv_w_f, conv_b, wg_bd, gate_b2, lru_L,
                   wout_g.reshape(-1, d))

    x2 = x[0]
    later = [w_up[0].astype(BF16), w_down[0].astype(BF16), w_ple_gate[0].astype(BF16), w_ple_proj[0].astype(BF16)]
    h1, z1, proj, hs, cat, lru_saved, dpool_saved, wup_g, wdn_g, wgate_g, wproj_g = _mix_fwd(
        x2, *mixer_small, later, [_core_major_slot, _core_major_slot, None, None], GATHER_FORWARD_AT, tbs['mix_fwd'])
    w_down_f = wdn_g.reshape(-1, d)
    w_proj_f = jnp.transpose(wproj_g, (1, 0, 2)).reshape(wproj_g.shape[1], -1)
    h2, z2, up = _mlp_fwd(h1, norm_mlp_g, wup_g, w_down_f, tbs['mlp_fwd'])
    dh2, ple_vec, dw_gate, dw_proj = _ple(h2, p[0, 0], loss_target[0], norm_ple_g, wgate_g.reshape(-1, d), b_ple_gate,
                                          w_proj_f, norm_final_g.reshape(1, -1), tbs['ple'])
    everyone = list(range(N_DEV))
    n_proj = w_ple_proj.shape[2]
    dz2_0, dw_up_0, dw_down_0 = _mlp_bwd_part(
        0, MLP_BWD_SPLIT, dh2, z2, up, wup_g, w_down_f, None, h1, norm_mlp_g, _scatter_plan([], [], []), tbs['mlp_bwd'])
    half = N_DEV // MLP_BWD_SPLIT
    south = [_device_of_core_major_slot(k) for k in range(half)]
    north = [_device_of_core_major_slot(k) for k in range(half, N_DEV)]
    scatter = _scatter_plan(
        [dw_up_0, dw_down_0, dw_gate.reshape(N_DEV, -1, d), jnp.transpose(dw_proj.reshape(-1, N_DEV, n_proj), (1, 0, 2))],
        [south, south, everyone, everyone], [None, None, None, None])
    dh1, mlp_vec, dw_up_1, dw_down_1, recv_up, recv_down, recv_gate, recv_proj = _mlp_bwd_part(
        1, MLP_BWD_SPLIT, dh2, z2, up, wup_g, w_down_f, dz2_0, h1, norm_mlp_g, scatter, tbs['mlp_bwd'])
    dw_out, _, _ = _wgrad("wgrad_out", cat, dh1, [], [], tbs['wgrad_out'])
    scatter = _scatter_plan([dw_up_1, dw_down_1, dw_out.reshape(N_DEV, -1, d)], [north, north, everyone],
                            [recv_up, recv_down, None])
    dproj, v512, dpw, dga, dgx, recv_up, recv_down, recv_out = _mix_bwd(
        dh1, proj, hs, lru_saved, dpool_saved, wp_bd, pool_b.reshape(1, -1), pool_scale, conv_w_f, wg_bd, lru_L, wout_g.reshape(-1, d),
        scatter, tbs['mix_bwd'])
    rows1024 = jnp.concatenate([jnp.zeros((1, d), F32), mlp_vec[0:1], ple_vec[1:2], ple_vec[0:1], ple_vec[2:4],
                                jnp.zeros((2, d), F32)], axis=0)
    dw_in_t, (rows1024, rows512), (g_pool_w, g_gate_a_w, g_gate_x_w) = _wgrad(
        "wgrad_in", dproj, z1, [rows1024, v512], [dpw, dga, dgx], tbs['wgrad_in'])
    scatter = _scatter_plan([dw_in_t.reshape(N_DEV, -1, d)], [everyone], [None])
    dx, in_vec, recv_in = _in_bwd(dproj, x2, dh1, norm_mix_g, w_in_f, scatter, tbs['in_bwd'])
    rows1024 = jnp.concatenate([in_vec[0:1], rows1024[1:]], axis=0)
    received = [recv_in, recv_out, recv_up, recv_down, recv_gate, recv_proj]

    shard_w = [w_in[0].T, w_out[0], w_up[0], w_down[0], w_ple_gate[0], w_ple_proj[0]]
    shard_m = [m_w_in[0].T, m_w_out[0], m_w_up[0], m_w_down[0], m_w_ple_gate[0], m_w_ple_proj[0]]
    shard_v = [v_w_in[0].T, v_w_out[0], v_w_up[0], v_w_down[0], v_w_ple_gate[0], v_w_ple_proj[0]]
    names = ["w_in", "w_out", "w_up", "w_down", "w_ple_gate", "w_ple_proj"]
    res = {}
    for nm, parts, w_s, m_s, v_s in zip(names, received, shard_w, shard_m, shard_v):
        res[nm] = [r[None] for r in _adamw("adamw_" + nm, parts, w_s, m_s, v_s, ADAM_ROW_BLOCK)]
    res["w_in"] = [jnp.swapaxes(r, 1, 2) for r in res["w_in"]]

    def rows_of_1024(a, b, c, e, f):
        return jnp.concatenate([a, b, c, e, f.reshape(1, -1), jnp.zeros((3, d), F32)], axis=0)

    def rows_of_512(scale, bias, cb, lru, ga, gx):
        z = jnp.zeros((1, 512), F32)
        return jnp.concatenate([scale, bias.reshape(1, -1), cb, lru, z, z, z, z, ga.reshape(1, -1), gx.reshape(1, -1),
                                z, z, z, z, z, z], axis=0)

    n_conv = conv_w.shape[2]
    groups = [
        (rows1024, *[rows_of_1024(*t) for t in (
            (norm_mix_g, norm_mlp_g, norm_ple_g, b_ple_gate, norm_final_g),
            (m_norm_mix_g, m_norm_mlp_g, m_norm_ple_g, m_b_ple_gate, m_norm_final_g),
            (v_norm_mix_g, v_norm_mlp_g, v_norm_ple_g, v_b_ple_gate, v_norm_final_g))]),
        (rows512, *[rows_of_512(*t) for t in (
            (pool_scale, pool_b, conv_b, lru_L, gate_a_b, gate_x_b),
            (m_pool_scale, m_pool_b, m_conv_b, m_lru_L, m_gate_a_b, m_gate_x_b),
            (v_pool_scale, v_pool_b, v_conv_b, v_lru_L, v_gate_a_b, v_gate_x_b))]),
        (g_pool_w, *[a.reshape(-1, a.shape[-1]) for a in (pool_w, m_pool_w, v_pool_w)]),
        (g_gate_a_w, *[a.reshape(-1, a.shape[-1]) for a in (gate_a_w, m_gate_a_w, v_gate_a_w)]),
        (g_gate_x_w, *[a.reshape(-1, a.shape[-1]) for a in (gate_x_w, m_gate_x_w, v_gate_x_w)]),
        (lax.dynamic_slice_in_dim(rows512[4:8], me * n_conv, n_conv, axis=1), conv_w[0], m_conv_w[0], v_conv_w[0]),
    ]
    r1024, r512, r_pool, r_ga, r_gx, r_conv = _adamw_groups("adamw_small", groups)
    loss = rows1024[5, 0]
    for k, nm in enumerate(["norm_mix_g", "norm_mlp_g", "norm_ple_g", "b_ple_gate"]):
        res[nm] = [a[k:k + 1] for a in r1024]
    res["norm_final_g"] = [a[4] for a in r1024]
    res["pool_scale"] = [a[0:1] for a in r512]
    res["pool_b"] = [a[1:2].reshape(pool_b.shape) for a in r512]
    res["conv_b"] = [a[2:3] for a in r512]
    res["lru_L"] = [a[3:4] for a in r512]
    res["gate_a_b"] = [a[8:9].reshape(gate_a_b.shape) for a in r512]
    res["gate_x_b"] = [a[9:10].reshape(gate_x_b.shape) for a in r512]
    res["pool_w"] = [a.reshape(pool_w.shape) for a in r_pool]
    res["gate_a_w"] = [a.reshape(gate_a_w.shape) for a in r_ga]
    res["gate_x_w"] = [a.reshape(gate_x_w.shape) for a in r_gx]
    res["conv_w"] = [a[None] for a in r_conv]
    order = ["norm_mix_g", "w_in", "pool_w", "pool_b", "pool_scale", "conv_w", "conv_b", "gate_a_w", "gate_a_b",
             "gate_x_w", "gate_x_b", "lru_L", "w_out", "norm_mlp_g", "w_up", "w_down", "norm_ple_g", "w_ple_gate",
             "b_ple_gate", "w_ple_proj", "norm_final_g"]
    return (loss, dx[None], *[res[nm][kind] for kind in range(4) for nm in order])
```

```python
import jax
import jax.numpy as jnp
from jax import lax
from jax.experimental import pallas as pl
from jax.experimental.pallas import tpu as pltpu

F32 = jnp.float32
BF16 = jnp.bfloat16
MESH = pl.DeviceIdType.MESH

N_DEV = 8
RMS_EPS = 1e-6
LRU_C = 8.0
POOL_WINDOWS = (2, 4, 8, 16)
N_POOL_GROUPS = 4
LRU_HEADS = 8
HALO = 16
SUB = 8
GELU_C0 = 0.7978845608028654
GELU_C1 = 0.044715

ADAM_LR = 0.001
ADAM_B1 = 0.9
ADAM_B2 = 0.999
ADAM_EPS = 1e-08
ADAM_WD = 0.01
ADAM_STEP = 10

VMEM_LIMIT = 60 * 1024 * 1024
TIME_BLOCKS = dict(mix_fwd=512, mlp_fwd=512, ple=512, mlp_bwd=512, wgrad_out=1024, mix_bwd=512, wgrad_in=1024, in_bwd=512)
ADAM_ROW_BLOCK = 256
SCAN_UNROLL = 4
MLP_BWD_SPLIT = 2
GATHER_FORWARD_AT = (0.5, 0.875, 1.0, 1.0)


def _params(n_arbitrary=1):
    return pltpu.CompilerParams(dimension_semantics=("arbitrary",) * n_arbitrary, vmem_limit_bytes=VMEM_LIMIT)


def _dot(a, b):
    return jnp.dot(a, b, preferred_element_type=F32)


def _dot_nt(a, b):
    return lax.dot_general(a, b, (((1,), (1,)), ((), ())), preferred_element_type=F32)


def _dot_tn(a, b):
    return lax.dot_general(a, b, (((0,), (0,)), ((), ())), preferred_element_type=F32)


def _rms_fwd(x, g):
    r = lax.rsqrt(jnp.mean(x * x, axis=-1, keepdims=True) + RMS_EPS)
    xh = x * r
    return xh * g, xh, r


def _rms_bwd(xh, r, g, dz):
    dxh = dz * g
    return r * (dxh - xh * jnp.mean(dxh * xh, axis=-1, keepdims=True))


def _colsum(a):
    return jnp.sum(a, axis=0, keepdims=True)


def _sigmoid(a):
    return 0.5 * jnp.tanh(0.5 * a) + 0.5


def _gelu_parts(u):
    u2 = u * u
    th = jnp.tanh(GELU_C0 * (u + GELU_C1 * u * u2))
    gel = 0.5 * u * (1.0 + th)
    dgel = 0.5 * (1.0 + th) + 0.5 * u * (1.0 - th * th) * (GELU_C0 * (1.0 + 3.0 * GELU_C1 * u2))
    return gel, dgel


def _my_index():
    return 4 * lax.axis_index("x") + 2 * lax.axis_index("y") + lax.axis_index("c")


def _all_to_all(srcs_of, dsts, send_sems, recv_sems, local_sems, dests=None):
    n = len(dsts)
    me = _my_index()
    dests = [list(range(N_DEV))] * n if dests is None else dests

    def remote(t, s):
        return pltpu.make_async_remote_copy(
            src_ref=srcs_of[t](s), dst_ref=dsts[t].at[me], send_sem=send_sems.at[t, s], recv_sem=recv_sems.at[t, me],
            device_id=(s // 4, (s // 2) % 2, s % 2), device_id_type=MESH)

    def arrival(t, s):
        return pltpu.make_async_remote_copy(
            src_ref=srcs_of[t](dests[t][0]), dst_ref=dsts[t].at[s], send_sem=send_sems.at[t, s],
            recv_sem=recv_sems.at[t, s], device_id=(s // 4, (s // 2) % 2, s % 2), device_id_type=MESH)

    def local(t, s):
        return pltpu.make_async_copy(srcs_of[t](s), dsts[t].at[s], local_sems.at[t])

    def start():
        for s in range(N_DEV):
            to_s = [t for t in range(n) if s in dests[t]]

            @pl.when(s == me)
            def _():
                for t in to_s:
                    local(t, s).start()

            @pl.when(s != me)
            def _():
                for t in to_s:
                    remote(t, s).start()

    def wait():
        for s in range(N_DEV):
            to_s = [t for t in range(n) if s in dests[t]]

            @pl.when(s == me)
            def _():
                for t in to_s:
                    local(t, s).wait()
                    for src in range(N_DEV):
                        if src != s:
                            arrival(t, src).wait_recv()

            @pl.when(s != me)
            def _():
                for t in to_s:
                    remote(t, s).wait_send()

    return start, wait


N_GATHER_COPIES = 7


def _core_major_slot(dev):
    return 4 * dev[2] + 2 * dev[0] + dev[1]


def _device_of_core_major_slot(k):
    return (k % 4) * 2 + k // 4


def _two_level_gather(srcs, dsts, send_sems, recv_sems, local_sems, slots=None):
    n = len(dsts)
    x, y, c = lax.axis_index("x"), lax.axis_index("y"), lax.axis_index("c")
    me, sibling = (x, y, c), (x, y, 1 - c)
    chips = [(1 - x, y), (x, 1 - y), (1 - x, 1 - y)]

    def slot(t, dev):
        return 4 * dev[0] + 2 * dev[1] + dev[2] if slots is None or slots[t] is None else slots[t](dev)

    def copy(t, k, block, to, src=None):
        return pltpu.make_async_remote_copy(
            src_ref=dsts[t].at[slot(t, block)] if src is None else src, dst_ref=dsts[t].at[slot(t, block)],
            send_sem=send_sems.at[t, k], recv_sem=recv_sems.at[t, k], device_id=to, device_id_type=MESH)

    def local(t):
        return pltpu.make_async_copy(srcs[t], dsts[t].at[slot(t, me)], local_sems.at[t])

    def start():
        for t in range(n):
            local(t).start()
            for j, chip in enumerate(chips):
                copy(t, 1 + j, me, (*chip, c), src=srcs[t]).start()
            copy(t, 0, me, sibling, src=srcs[t]).start()

    def forward(t):
        for j, chip in enumerate(chips):
            copy(t, 1 + j, (*chip, c), me).wait_recv()
            copy(t, 4 + j, (*chip, c), sibling).start()

    def finish():
        for t in range(n):
            copy(t, 0, sibling, me).wait_recv()
            for j, chip in enumerate(chips):
                copy(t, 4 + j, (*chip, 1 - c), me).wait_recv()
            copy(t, 0, me, sibling, src=srcs[t]).wait_send()
            for j, chip in enumerate(chips):
                copy(t, 1 + j, me, (*chip, c), src=srcs[t]).wait_send()
                copy(t, 4 + j, (*chip, c), sibling).wait_send()
            local(t).wait()

    return start, forward, finish


def _hosted_gather(i, nb, forward_at, srcs, dsts, sems, slots=None):
    start, forward, finish = _two_level_gather(srcs, dsts, *sems, slots)

    def after_step():
        for t, f in enumerate(forward_at):
            @pl.when(i == min(nb - 1, int(f * nb)))
            def _():
                forward(t)

        @pl.when(i == nb - 1)
        def _():
            finish()

    return start, after_step


def _gather_scratch(n):
    return [pltpu.SemaphoreType.DMA((n, N_GATHER_COPIES)), pltpu.SemaphoreType.DMA((n, N_GATHER_COPIES)),
            pltpu.SemaphoreType.DMA((n,))]


def _gather(name, srcs):
    n = len(srcs)

    def body(*refs):
        start, forward, finish = _two_level_gather(refs[:n], refs[n:2 * n], *refs[2 * n:])
        start()
        for t in range(n):
            forward(t)
        finish()

    any_spec = pl.BlockSpec(memory_space=pl.ANY)
    return pl.pallas_call(
        body, name=name, in_specs=[any_spec] * n, out_specs=[any_spec] * n,
        out_shape=[jax.ShapeDtypeStruct((N_DEV,) + a.shape, a.dtype) for a in srcs], scratch_shapes=_gather_scratch(n),
    )(*srcs)


def _scatter_plan(blocks, dests, landing):
    return dict(blocks=list(blocks), dests=[list(dd) for dd in dests], landing=list(landing))


def _scatter_args(plan):
    return plan['blocks'] + [a for a in plan['landing'] if a is not None]


def _scatter_out_shape(plan):
    return [jax.ShapeDtypeStruct((N_DEV,) + b.shape[1:], b.dtype) for b in plan['blocks']]


def _scatter_aliases(plan, first_in, first_out):
    given = [t for t, a in enumerate(plan['landing']) if a is not None]
    return {first_in + len(plan['blocks']) + k: first_out + t for k, t in enumerate(given)}


def _scatter_ops(plan, in_refs, out_refs, sems):
    n = len(plan['blocks'])
    srcs_of = [(lambda s, r=in_refs[t], dd=plan['dests'][t]: r.at[dd.index(s)]) for t in range(n)]
    return _all_to_all(srcs_of, out_refs, *sems, dests=plan['dests'])


def _exchange_scratch(n):
    return [pltpu.SemaphoreType.DMA((n, N_DEV)), pltpu.SemaphoreType.DMA((n, N_DEV)), pltpu.SemaphoreType.DMA((n,))]


def _const_spec(shape):
    nd = len(shape)
    return pl.BlockSpec(shape, lambda i: (0,) * nd, pipeline_mode=pl.Buffered(1))


def _pool_windows(up_ext, n, forward):
    sh = (lambda k: k) if forward else (lambda k: n - k)
    s2 = up_ext + pltpu.roll(up_ext, sh(1), 0)
    t4 = s2[:, 128:]
    s4 = t4 + pltpu.roll(t4, sh(2), 0)
    t8 = s4[:, 128:]
    s8 = t8 + pltpu.roll(t8, sh(4), 0)
    t16 = s8[:, 128:]
    s16 = t16 + pltpu.roll(t16, sh(8), 0)
    return jnp.concatenate([s2[:, :128], s4[:, :128], s8[:, :128], s16], axis=1)


def _inv_count_head():
    t = jnp.arange(1, HALO + 1, dtype=F32)[:, None]
    return jnp.concatenate([jnp.broadcast_to(1.0 / jnp.minimum(t, float(w)), (HALO, 128)) for w in POOL_WINDOWS], axis=1)


def _scale_by_inv_count(v, is_first_block, inv_head):
    inv_row = jnp.concatenate([jnp.full((1, 128), 1.0 / w, F32) for w in POOL_WINDOWS], axis=1)
    head = v[0:HALO] * jnp.where(is_first_block, inv_head, inv_row)
    return jnp.concatenate([head, v[HALO:] * inv_row], axis=0)


def _lru_decay(r, a, c_l, first_row):
    a2 = a * a
    m2 = -jnp.tanh(c_l * r) * (a2 + 1.0)
    return a2, m2, jnp.where(first_row, 1.0, jnp.sqrt(m2))


def _log_sigmoid(v):
    return -(jnp.maximum(-v, 0.0) + jnp.log1p(jnp.exp(-jnp.abs(v))))


def _conv_fwd(ul_ext, cw, cb):
    return (cb + cw[3:4, :] * ul_ext + cw[2:3, :] * pltpu.roll(ul_ext, 1, 0)
            + cw[1:2, :] * pltpu.roll(ul_ext, 2, 0) + cw[0:1, :] * pltpu.roll(ul_ext, 3, 0))


def _mix_fwd(x, g_mix, w_in, wp_bd, pool_b, pool_scale, conv_w, conv_b, wg_bd, gate_b, lru_l, w_out, gather_srcs,
             gather_slots, forward_at, tb):
    t_len, d = x.shape
    nb = t_len // tb
    n_g = len(gather_srcs)

    def body(*refs):
        (x_ref, g_ref, win_ref, wp_ref, pb_ref, ps_ref, cw_ref, cb_ref, wg_ref, gb_ref, l_ref, wout_ref,
         invh_ref) = refs[:13]
        gsrc = refs[13:13 + n_g]
        h1_ref, z1_ref, proj_ref, hs_ref, cat_ref, lru_ref, dpool_ref = refs[13 + n_g:20 + n_g]
        gdst = refs[20 + n_g:20 + 2 * n_g]
        ext_ref, a_ref, b_ref, hc_ref, send_sems, recv_sems, local_sems = refs[20 + 2 * n_g:]
        i = pl.program_id(0)
        start_gather, after_step = _hosted_gather(i, nb, forward_at, gsrc, gdst, (send_sems, recv_sems, local_sems),
                                                  gather_slots)

        @pl.when(i == 0)
        def _():
            start_gather()
            ext_ref[0:HALO, :] = jnp.zeros((HALO, 1024), F32)
            hc_ref[...] = jnp.zeros_like(hc_ref)

        xv = x_ref[...]
        z, _, _ = _rms_fwd(xv, g_ref[...])
        zb = z.astype(BF16)
        z1_ref[...] = zb
        proj = _dot_nt(zb, win_ref[...])
        proj_ref[...] = proj
        ext_ref[HALO:, :] = proj[:, 0:1024]
        ug = proj[:, 1024:1536]
        n = tb + HALO
        up_ext = ext_ref[:, 0:512]
        win = _pool_windows(up_ext, n, True)[HALO:]
        dpool = _scale_by_inv_count(win, i == 0, invh_ref[...]) - proj[:, 0:512]
        dpoolb = dpool.astype(BF16)
        dpool_ref[...] = dpoolb
        q = _dot(dpoolb, wp_ref[...]) + pb_ref[...]
        y_pool = q * ps_ref[...]
        xb = _conv_fwd(ext_ref[:, 512:1024], cw_ref[...], cb_ref[...])[HALO:]
        first_row = (i * tb + lax.broadcasted_iota(jnp.int32, (tb, 1), 0)) == 0
        c_l = LRU_C * _log_sigmoid(l_ref[...])
        gp = _dot(xb.astype(BF16), wg_ref[...]) + gb_ref[...]
        r = _sigmoid(gp[:, :512])
        ig = _sigmoid(gp[:, 512:])
        a = jnp.exp(c_l * r)
        _, _, mult = _lru_decay(r, a, c_l, first_row)
        lru_ref[:, 0:512] = xb
        lru_ref[:, 512:1024] = r
        lru_ref[:, 1024:1536] = ig
        lru_ref[:, 1536:2048] = a
        a_ref[...] = a
        b_ref[...] = mult * (ig * xb)
        row = lax.broadcasted_iota(jnp.int32, (SUB, 512), 0)

        def group(j, hprev):
            o = pl.multiple_of(j * SUB, SUB)
            a8 = a_ref[pl.ds(o, SUB), :]
            b8 = b_ref[pl.ds(o, SUB), :]
            for sh in (1, 2, 4):
                ash = jnp.where(row >= sh, pltpu.roll(a8, sh, 0), 1.0)
                bsh = jnp.where(row >= sh, pltpu.roll(b8, sh, 0), 0.0)
                b8 = a8 * bsh + b8
                a8 = a8 * ash
            h8 = a8 * hprev + b8
            hs_ref[pl.ds(o, SUB), :] = h8
            return jnp.broadcast_to(h8[SUB - 1:SUB, :], (SUB, 512))

        def trip(k, carry):
            for u in range(SCAN_UNROLL):
                carry = group(k * SCAN_UNROLL + u, carry)
            return carry

        hc_ref[...] = lax.fori_loop(0, tb // (SUB * SCAN_UNROLL), trip, hc_ref[...])
        gel, dgel = _gelu_parts(ug)
        lru_ref[:, 2048:2560] = gel
        lru_ref[:, 2560:3072] = dgel
        y_lru = hs_ref[...] * gel
        catb = jnp.concatenate([y_pool, y_lru], axis=1).astype(BF16)
        cat_ref[...] = catb
        h1_ref[...] = xv + _dot(catb, wout_ref[...])
        ext_ref[0:HALO, :] = ext_ref[tb:tb + HALO, :]

        after_step()

    row_spec = lambda w: pl.BlockSpec((tb, w), lambda i: (i, 0))
    any_spec = pl.BlockSpec(memory_space=pl.ANY)
    smalls = [g_mix, w_in, wp_bd, pool_b, pool_scale, conv_w, conv_b, wg_bd, gate_b, lru_l, w_out, _inv_count_head()]
    return pl.pallas_call(
        body, name="mix_fwd", grid=(nb,),
        in_specs=[row_spec(d)] + [_const_spec(s.shape) for s in smalls] + [any_spec] * n_g,
        out_specs=[row_spec(d), row_spec(d), row_spec(1536), row_spec(512), row_spec(1024), row_spec(3072), row_spec(512)]
        + [any_spec] * n_g,
        out_shape=[jax.ShapeDtypeStruct((t_len, d), F32), jax.ShapeDtypeStruct((t_len, d), BF16),
                   jax.ShapeDtypeStruct((t_len, 1536), F32), jax.ShapeDtypeStruct((t_len, 512), F32),
                   jax.ShapeDtypeStruct((t_len, 1024), BF16), jax.ShapeDtypeStruct((t_len, 3072), F32),
                   jax.ShapeDtypeStruct((t_len, 512), BF16)]
        + [jax.ShapeDtypeStruct((N_DEV,) + s.shape, s.dtype) for s in gather_srcs],
        scratch_shapes=[pltpu.VMEM((tb + HALO, 1024), F32), pltpu.VMEM((tb, 512), F32), pltpu.VMEM((tb, 512), F32),
                        pltpu.VMEM((SUB, 512), F32)] + _gather_scratch(n_g),
        compiler_params=_params(),
    )(x, *smalls, *gather_srcs)


def _mlp_fwd(h1, g_mlp, w_up, w_down, tb):
    t_len, d = h1.shape
    nb = t_len // tb
    n_chunk, _, fc = w_up.shape

    def body(h1_ref, g_ref, wup_ref, wdn_ref, h2_ref, z2_ref, up_ref):
        xv = h1_ref[...]
        z, _, _ = _rms_fwd(xv, g_ref[...])
        zb = z.astype(BF16)
        z2_ref[...] = zb
        acc = xv
        for c in range(n_chunk):
            u = _dot(zb, wup_ref[c])
            up_ref[:, c * fc:(c + 1) * fc] = u.astype(BF16)
            act = jnp.square(jnp.maximum(u, 0.0)).astype(BF16)
            acc = acc + _dot(act, wdn_ref[c * fc:(c + 1) * fc, :])
        h2_ref[...] = acc

    row_spec = lambda w: pl.BlockSpec((tb, w), lambda i: (i, 0))
    return pl.pallas_call(
        body, name="mlp_fwd", grid=(nb,),
        in_specs=[row_spec(d), _const_spec(g_mlp.shape), _const_spec(w_up.shape), _const_spec(w_down.shape)],
        out_specs=[row_spec(d), row_spec(d), row_spec(n_chunk * fc)],
        out_shape=[jax.ShapeDtypeStruct((t_len, d), F32), jax.ShapeDtypeStruct((t_len, d), BF16),
                   jax.ShapeDtypeStruct((t_len, n_chunk * fc), BF16)],
        compiler_params=_params(),
    )(h1, g_mlp, w_up, w_down)


def _ple(h2, p, target, g_ple, w_gate, b_gate, w_proj, g_final, tb):
    t_len, d = h2.shape
    nb = t_len // tb
    pd = p.shape[1]

    def body(h2_ref, p_ref, tgt_ref, g_ref, wg_ref, bg_ref, wp_ref, gf_ref,
             dh2_ref, vec_ref, dwg_out, dwp_out, dwg_acc, dwp_acc, dwg_stage, dwp_stage):
        i = pl.program_id(0)

        @pl.when(i == 0)
        def _():
            vec_ref[...] = jnp.zeros_like(vec_ref)
            dwg_acc[...] = jnp.zeros_like(dwg_acc)
            dwp_acc[...] = jnp.zeros_like(dwp_acc)

        h2 = h2_ref[...]
        g2 = g_ref[...]
        z3, xh2, r2 = _rms_fwd(h2, g2)
        z3b = z3.astype(BF16)
        gate = _sigmoid(_dot(z3b, wg_ref[...]) + bg_ref[...])
        pb = p_ref[...].astype(BF16)
        pp = _dot(pb, wp_ref[...])
        h3 = h2 + gate * pp
        gf = gf_ref[...]
        y, xh3, r3 = _rms_fwd(h3, gf)
        err = y - tgt_ref[...]
        loss_rows = jnp.mean(err * err, axis=-1, keepdims=True)
        dy = err * (1.0 / d)
        dh3 = _rms_bwd(xh3, r3, gf, dy)
        dgl = (dh3 * pp) * (gate * (1.0 - gate))
        dpp = dh3 * gate
        dglb = dgl.astype(BF16)
        dwg_acc[...] += _dot_tn(z3b, dglb)
        dwp_acc[...] += _dot_tn(pb, dpp.astype(BF16))
        dz3 = _dot_nt(dglb, wg_ref[...])
        dh2_ref[...] = dh3 + _rms_bwd(xh2, r2, g2, dz3)
        vec_ref[0:1, :] += _colsum(dgl)
        vec_ref[1:2, :] += _colsum(dz3 * xh2)
        vec_ref[2:3, :] += _colsum(dy * xh3)
        vec_ref[3:4, :] += 0.5 * jnp.sum(loss_rows)

        @pl.when(i == nb - 1)
        def _():
            dwg_stage[...] = dwg_acc[...].astype(BF16)
            dwp_stage[...] = dwp_acc[...].astype(BF16)
            pltpu.sync_copy(dwg_stage, dwg_out)
            pltpu.sync_copy(dwp_stage, dwp_out)

    row_spec = lambda w: pl.BlockSpec((tb, w), lambda i: (i, 0))
    any_spec = pl.BlockSpec(memory_space=pl.ANY)
    smalls = [g_ple, w_gate, b_gate, w_proj, g_final]
    return pl.pallas_call(
        body, name="ple_fwd_bwd", grid=(nb,),
        in_specs=[row_spec(d), row_spec(pd), row_spec(d)] + [_const_spec(s.shape) for s in smalls],
        out_specs=[row_spec(d), pl.BlockSpec((8, d), lambda i: (0, 0)), any_spec, any_spec],
        out_shape=[jax.ShapeDtypeStruct((t_len, d), F32), jax.ShapeDtypeStruct((8, d), F32),
                   jax.ShapeDtypeStruct(w_gate.shape, BF16), jax.ShapeDtypeStruct(w_proj.shape, BF16)],
        scratch_shapes=[pltpu.VMEM(w_gate.shape, F32), pltpu.VMEM(w_proj.shape, F32), pltpu.VMEM(w_gate.shape, BF16),
                        pltpu.VMEM(w_proj.shape, BF16)],
        compiler_params=_params(),
    )(h2, p, target, *smalls)


def _mlp_bwd_part(part, n_part, dh2, z2, up, w_up, w_down, dz2_prev, h1, g_mlp, scatter, tb):
    t_len, d = dh2.shape
    nb = t_len // tb
    n_chunk_all, _, fc = w_up.shape
    n_chunk = n_chunk_all // n_part
    first, last = part == 0, part == n_part - 1

    def body(*refs):
        refs = list(refs)
        dh2_ref, z2_ref, up_ref, wup_ref, wdn_ref = refs[:5]
        del refs[:5]
        dzp_ref = None if first else refs.pop(0)
        h1_ref, g_ref = (refs.pop(0), refs.pop(0)) if last else (None, None)
        scatter_in = [refs.pop(0) for _ in _scatter_args(scatter)]
        out_ref = refs.pop(0)
        vec_ref = refs.pop(0) if last else None
        dwup_out, dwdn_out = refs.pop(0), refs.pop(0)
        scatter_out = [refs.pop(0) for _ in scatter['blocks']]
        dwup_acc, dwdn_acc, up_stage, dn_stage = refs[:4]
        if scatter['blocks']:
            start_scatter, wait_scatter = _scatter_ops(scatter, scatter_in, scatter_out, refs[4:])
        i = pl.program_id(0)

        @pl.when(i == 0)
        def _():
            if scatter['blocks']:
                start_scatter()
            dwup_acc[...] = jnp.zeros_like(dwup_acc)
            dwdn_acc[...] = jnp.zeros_like(dwdn_acc)
            if last:
                vec_ref[...] = jnp.zeros_like(vec_ref)

        dh2 = dh2_ref[...]
        dh2b = dh2.astype(BF16)
        z2b = z2_ref[...]
        dz2 = jnp.zeros((tb, d), F32) if first else dzp_ref[...]
        for c in range(n_chunk):
            u = up_ref[:, c * fc:(c + 1) * fc].astype(F32)
            ur = jnp.maximum(u, 0.0)
            dact = _dot_nt(dh2b, wdn_ref[c * fc:(c + 1) * fc, :])
            dupb = (dact * (2.0 * ur)).astype(BF16)
            dwdn_acc[c * fc:(c + 1) * fc, :] += _dot_tn((ur * ur).astype(BF16), dh2b)
            dwup_acc[c] += _dot_tn(z2b, dupb)
            dz2 = dz2 + _dot_nt(dupb, wup_ref[c])
        if last:
            g = g_ref[...]
            _, xh, r = _rms_fwd(h1_ref[...], g)
            out_ref[...] = dh2 + _rms_bwd(xh, r, g, dz2)
            vec_ref[0:1, :] += _colsum(dz2 * xh)
        else:
            out_ref[...] = dz2

        @pl.when(i == nb - 1)
        def _():
            for c in range(n_chunk):
                up_stage[...] = dwup_acc[c].astype(BF16)
                dn_stage[...] = dwdn_acc[c * fc:(c + 1) * fc, :].astype(BF16)
                pltpu.sync_copy(up_stage, dwup_out.at[c])
                pltpu.sync_copy(dn_stage, dwdn_out.at[c])
            if scatter['blocks']:
                wait_scatter()

    row_spec = lambda w: pl.BlockSpec((tb, w), lambda i: (i, 0))
    any_spec = pl.BlockSpec(memory_space=pl.ANY)
    args = [dh2, z2, up, w_up, w_down]
    in_specs = [row_spec(d), row_spec(d), pl.BlockSpec((tb, n_chunk * fc), lambda i: (i, part)),
                pl.BlockSpec((n_chunk, d, fc), lambda i: (part, 0, 0), pipeline_mode=pl.Buffered(1)),
                pl.BlockSpec((n_chunk * fc, d), lambda i: (part, 0), pipeline_mode=pl.Buffered(1))]
    if not first:
        args.append(dz2_prev)
        in_specs.append(row_spec(d))
    if last:
        args += [h1, g_mlp]
        in_specs += [row_spec(d), _const_spec(g_mlp.shape)]
    n_in = len(args)
    args += _scatter_args(scatter)
    in_specs += [any_spec] * len(_scatter_args(scatter))
    out_specs = [row_spec(d)]
    out_shape = [jax.ShapeDtypeStruct((t_len, d), F32)]
    if last:
        out_specs.append(pl.BlockSpec((8, d), lambda i: (0, 0)))
        out_shape.append(jax.ShapeDtypeStruct((8, d), F32))
    out_specs += [any_spec, any_spec]
    out_shape += [jax.ShapeDtypeStruct((n_chunk, d, fc), BF16), jax.ShapeDtypeStruct((n_chunk, fc, d), BF16)]
    n_out = len(out_shape)
    out_specs += [any_spec] * len(scatter['blocks'])
    out_shape += _scatter_out_shape(scatter)
    return pl.pallas_call(
        body, name=f"mlp_bwd_{part}", grid=(nb,), in_specs=in_specs, out_specs=out_specs, out_shape=out_shape,
        scratch_shapes=[pltpu.VMEM((n_chunk, d, fc), F32), pltpu.VMEM((n_chunk * fc, d), F32),
                        pltpu.VMEM((d, fc), BF16), pltpu.VMEM((fc, d), BF16)]
        + (_exchange_scratch(len(scatter['blocks'])) if scatter['blocks'] else []),
        input_output_aliases=_scatter_aliases(scatter, n_in, n_out), compiler_params=_params(),
    )(*args)


def _mix_bwd(dh1, proj, hs, lru_saved, dpool_saved, wp_bd, pool_b, pool_scale, conv_w, wg_bd, lru_l, w_out, scatter, tb):
    t_len, d = dh1.shape
    nb = t_len // tb
    n_s = len(scatter['blocks'])
    scatter_args = _scatter_args(scatter)

    def body(*refs):
        refs = list(refs)
        (dh1_ref, ul_ref, hs_ref, hsh_ref, lru_ref, dpool_ref,
         wp_ref, pb_ref, ps_ref, cw_ref, wg_ref, l_ref, wout_ref, invh_ref) = refs[:14]
        del refs[:14]
        scatter_in = refs[:len(scatter_args)]
        del refs[:len(scatter_args)]
        dproj_ref, v512_ref, dpw_ref, dga_ref, dgx_ref = refs[:5]
        recv = refs[5:5 + n_s]
        (dwp_acc, dwg_acc, v1024_ref, b_ref, gs_ref, ehead_ref, dxbhead_ref, hc_ref,
         send_sems, recv_sems, local_sems) = refs[5 + n_s:]
        i = pl.program_id(0)
        tbk = nb - 1 - i

        start_scatter, wait_scatter = _scatter_ops(scatter, scatter_in, recv, (send_sems, recv_sems, local_sems))

        @pl.when(i == 0)
        def _():
            start_scatter()
            for ref in (v512_ref, v1024_ref, dwp_acc, dwg_acc, ehead_ref, dxbhead_ref, hc_ref):
                ref[...] = jnp.zeros_like(ref)

        dcat = _dot_nt(dh1_ref[...].astype(BF16), wout_ref[...])

        has_prev = (tbk > 0).astype(F32)
        n = tb + HALO
        inv_head = invh_ref[...]

        dpoolb = dpool_ref[...]
        q = _dot(dpoolb, wp_ref[...]) + pb_ref[...]
        dyp = dcat[:, 0:512]
        dq = dyp * ps_ref[...]
        dqb = dq.astype(BF16)
        v512_ref[0:1, :] += _colsum(dyp * q)
        v512_ref[1:2, :] += _colsum(dq)
        dwp_acc[...] += _dot_tn(dpoolb, dqb)
        dd = _dot_nt(dqb, wp_ref[...])
        e = _scale_by_inv_count(dd, tbk == 0, inv_head)
        e_ext = jnp.concatenate([e, ehead_ref[...]], axis=0)
        du_pool = _pool_windows(e_ext, n, False)[0:tb] - dd
        ehead_ref[...] = e[0:HALO]

        gel, dgel = lru_ref[:, 2048:2560], lru_ref[:, 2560:3072]
        hsv = hs_ref[...]
        dcl = dcat[:, 512:1024]
        dhs = dcl * gel
        dug = dcl * hsv * dgel
        cw = cw_ref[...]
        xb, r, ig, a = lru_ref[:, 0:512], lru_ref[:, 512:1024], lru_ref[:, 1024:1536], lru_ref[:, 1536:2048]
        first_row = (tbk * tb + lax.broadcasted_iota(jnp.int32, (tb, 1), 0)) == 0
        c_l = LRU_C * _log_sigmoid(l_ref[...])
        a2, m2, mult = _lru_decay(r, a, c_l, first_row)
        b_ref[...] = dhs
        row = lax.broadcasted_iota(jnp.int32, (SUB, 512), 0)

        def group(jj, hnext):
            o = pl.multiple_of((tb // SUB - 1 - jj) * SUB, SUB)
            a8 = lru_ref[pl.ds(o, SUB), 1536:2048]
            d8 = b_ref[pl.ds(o, SUB), :]
            b8 = a8 * d8
            for sh in (1, 2, 4):
                ash = jnp.where(row < SUB - sh, pltpu.roll(a8, SUB - sh, 0), 1.0)
                bsh = jnp.where(row < SUB - sh, pltpu.roll(b8, SUB - sh, 0), 0.0)
                b8 = a8 * bsh + b8
                a8 = a8 * ash
            h8 = a8 * hnext + b8
            gs_ref[pl.ds(o, SUB), :] = d8 + jnp.where(row < SUB - 1, pltpu.roll(h8, SUB - 1, 0), hnext)
            return jnp.broadcast_to(h8[0:1, :], (SUB, 512))

        def trip(k, carry):
            for u in range(SCAN_UNROLL):
                carry = group(k * SCAN_UNROLL + u, carry)
            return carry

        hc_ref[...] = lax.fori_loop(0, tb // (SUB * SCAN_UNROLL), trip, hc_ref[...])
        gsum = gs_ref[...]
        hs_ext = jnp.concatenate([hsh_ref[...] * has_prev, hsv], axis=0)
        hprev = pltpu.roll(hs_ext, 1, 0)[SUB:]
        da = gsum * hprev
        dmult = jnp.where(first_row, 0.0, gsum * (ig * xb))
        di = gsum * mult * xb
        dxb = gsum * mult * ig
        dla = da * a - dmult * a2 * lax.rsqrt(m2)
        dr = dla * c_l
        v512_ref[3:4, :] += _colsum(dla * r)
        dgp = jnp.concatenate([dr * r * (1.0 - r), di * ig * (1.0 - ig)], axis=1)
        dgpb = dgp.astype(BF16)
        v1024_ref[0:1, :] += _colsum(dgp)
        dwg_acc[...] += _dot_tn(xb.astype(BF16), dgpb)
        dxb = dxb + _dot_nt(dgpb, wg_ref[...])
        n8 = tb + SUB
        dxb_ext = jnp.concatenate([dxb, dxbhead_ref[...]], axis=0)
        ul = ul_ref[...]
        du_lru = cw[3:4, :] * dxb
        v512_ref[7:8, :] += _colsum(dxb * ul)
        for j in range(1, 4):
            ahead = pltpu.roll(dxb_ext, n8 - j, 0)[0:tb]
            du_lru = du_lru + cw[3 - j:4 - j, :] * ahead
            v512_ref[4 + (3 - j):5 + (3 - j), :] += _colsum(ahead * ul)
        dxbhead_ref[...] = dxb[0:SUB]
        v512_ref[2:3, :] += _colsum(dxb)

        dproj_ref[...] = jnp.concatenate([du_pool, du_lru, dug], axis=1).astype(BF16)

        @pl.when(i == nb - 1)
        def _():
            v512_ref[3:4, :] = v512_ref[3:4, :] * (LRU_C * _sigmoid(-l_ref[...]))
            v512_ref[8:9, :] = v1024_ref[0:1, 0:512]
            v512_ref[9:10, :] = v1024_ref[0:1, 512:1024]
            for g in range(N_POOL_GROUPS):
                dpw_ref[g * 128:(g + 1) * 128, :] = dwp_acc[g * 128:(g + 1) * 128, g * 128:(g + 1) * 128]
            odd_head = (lax.broadcasted_iota(jnp.int32, (512, 128), 0) // 64) % 2 == 1
            for out_ref, col0 in ((dga_ref, 0), (dgx_ref, 512)):
                pairs = jnp.concatenate([dwg_acc[128 * k:128 * (k + 1), col0 + 128 * k:col0 + 128 * (k + 1)]
                                         for k in range(LRU_HEADS // 2)], axis=0)
                out_ref[...] = jnp.where(odd_head, pltpu.roll(pairs, 64, 1), pairs)[:, 0:64]
            wait_scatter()

    rev = lambda w: pl.BlockSpec((tb, w), lambda i: (nb - 1 - i, 0))
    halo = lambda rows, w: pl.BlockSpec((rows, w), lambda i: (jnp.maximum((nb - 1 - i) * (tb // rows) - 1, 0), 0))
    any_spec = pl.BlockSpec(memory_space=pl.ANY)
    smalls = [wp_bd, pool_b, pool_scale, conv_w, wg_bd, lru_l, w_out, _inv_count_head()]
    lru_third = pl.BlockSpec((tb, 512), lambda i: (nb - 1 - i, 1))
    return pl.pallas_call(
        body, name="mix_bwd", grid=(nb,),
        in_specs=[rev(d), lru_third, rev(512), halo(SUB, 512), rev(3072), rev(512)]
        + [_const_spec(s.shape) for s in smalls] + [any_spec] * len(scatter_args),
        out_specs=[rev(1536), pl.BlockSpec((16, 512), lambda i: (0, 0)), pl.BlockSpec((512, 128), lambda i: (0, 0)),
                   pl.BlockSpec((512, 64), lambda i: (0, 0)), pl.BlockSpec((512, 64), lambda i: (0, 0))]
        + [any_spec] * n_s,
        out_shape=[jax.ShapeDtypeStruct((t_len, 1536), BF16), jax.ShapeDtypeStruct((16, 512), F32),
                   jax.ShapeDtypeStruct((512, 128), F32), jax.ShapeDtypeStruct((512, 64), F32),
                   jax.ShapeDtypeStruct((512, 64), F32)]
        + _scatter_out_shape(scatter),
        scratch_shapes=[pltpu.VMEM(wp_bd.shape, F32), pltpu.VMEM(wg_bd.shape, F32), pltpu.VMEM((8, 1024), F32),
                        pltpu.VMEM((tb, 512), F32), pltpu.VMEM((tb, 512), F32), pltpu.VMEM((HALO, 512), F32),
                        pltpu.VMEM((SUB, 512), F32), pltpu.VMEM((SUB, 512), F32)]
        + _exchange_scratch(n_s),
        input_output_aliases=_scatter_aliases(scatter, 6 + len(smalls), 5), compiler_params=_params(),
    )(dh1, proj, hs, hs, lru_saved, dpool_saved, *smalls, *scatter_args)


def _wgrad(name, a, b, whole, by_rows, tb):
    t_len, m = a.shape
    n = b.shape[1]
    nb = t_len // tb
    n_w, n_r = len(whole), len(by_rows)
    n_small = n_w + n_r

    def body(*refs):
        a_ref, b_ref = refs[:2]
        small_in = refs[2:2 + n_small]
        out_ref = refs[2 + n_small]
        small_out = refs[3 + n_small:3 + 2 * n_small]
        acc_ref, stage_ref = refs[3 + 2 * n_small:5 + 2 * n_small]
        rest = refs[5 + 2 * n_small:]
        if n_small:
            send_partials, reduce_and_send_sums, finish_small = _small_allreduce(
                small_in[:n_w], small_in[n_w:], small_out[:n_w], small_out[n_w:], rest[:n_w], rest[n_w:n_small],
                rest[n_small:n_small + n_r], *rest[n_small + n_r:])
        i = pl.program_id(0)

        @pl.when(i == 0)
        def _():
            if n_small:
                send_partials()
            acc_ref[...] = jnp.zeros_like(acc_ref)

        acc_ref[...] += _dot_tn(a_ref[...], b_ref[...].astype(BF16))

        if n_small:
            @pl.when(i == nb // 2)
            def _():
                reduce_and_send_sums()

        @pl.when(i == nb - 1)
        def _():
            stage_ref[...] = acc_ref[...].astype(BF16)
            pltpu.sync_copy(stage_ref, out_ref)
            if n_small:
                finish_small()

    small = list(whole) + list(by_rows)
    vmem_spec = pl.BlockSpec(memory_space=pltpu.VMEM)
    res = pl.pallas_call(
        body, name=name, grid=(nb,),
        in_specs=[pl.BlockSpec((tb, m), lambda i: (i, 0)), pl.BlockSpec((tb, n), lambda i: (i, 0))] + [vmem_spec] * n_small,
        out_specs=[pl.BlockSpec(memory_space=pl.ANY)] + [vmem_spec] * n_small,
        out_shape=[jax.ShapeDtypeStruct((m, n), BF16)] + [jax.ShapeDtypeStruct(s_.shape, F32) for s_ in small],
        scratch_shapes=[pltpu.VMEM((m, n), F32), pltpu.VMEM((m, n), BF16)]
        + (_small_allreduce_scratch(whole, by_rows) if n_small else []),
        compiler_params=_params(),
    )(a, b, *small)
    return res[0], res[1:1 + n_w], res[1 + n_w:]


def _in_bwd(dproj, x, dh1, g_mix, w_in, scatter, tb):
    t_len, d = x.shape
    nb = t_len // tb
    n_s = len(scatter['blocks'])
    scatter_args = _scatter_args(scatter)

    def body(*refs):
        dproj_ref, x_ref, dh1_ref, g_ref, win_ref = refs[:5]
        scatter_in = refs[5:5 + len(scatter_args)]
        dx_ref, vec_ref = refs[5 + len(scatter_args):7 + len(scatter_args)]
        recv = refs[7 + len(scatter_args):7 + len(scatter_args) + n_s]
        vec_acc, send_sems, recv_sems, local_sems, vec_land, small_send, small_recv = refs[7 + len(scatter_args) + n_s:]
        start_scatter, wait_scatter = _scatter_ops(scatter, scatter_in, recv, (send_sems, recv_sems, local_sems))
        send_partials, reduce_and_send_sums, finish_small = _small_allreduce(
            [vec_acc], [], [vec_ref], [], [vec_land], [], [], small_send, small_recv)
        i = pl.program_id(0)

        @pl.when(i == 0)
        def _():
            start_scatter()
            vec_acc[...] = jnp.zeros_like(vec_acc)

        dz1 = _dot(dproj_ref[...], win_ref[...])
        g = g_ref[...]
        _, xh, rr = _rms_fwd(x_ref[...], g)
        dx_ref[...] = dh1_ref[...] + _rms_bwd(xh, rr, g, dz1)
        vec_acc[0:1, :] += _colsum(dz1 * xh)

        @pl.when(i == nb - 1)
        def _():
            send_partials()
            reduce_and_send_sums()
            finish_small()
            wait_scatter()

    row_spec = lambda w: pl.BlockSpec((tb, w), lambda i: (i, 0))
    any_spec = pl.BlockSpec(memory_space=pl.ANY)
    return pl.pallas_call(
        body, name="in_bwd", grid=(nb,),
        in_specs=[row_spec(dproj.shape[1]), row_spec(d), row_spec(d), _const_spec(g_mix.shape), _const_spec(w_in.shape)]
        + [any_spec] * len(scatter_args),
        out_specs=[row_spec(d), pl.BlockSpec((8, d), lambda i: (0, 0))] + [any_spec] * n_s,
        out_shape=[jax.ShapeDtypeStruct((t_len, d), F32), jax.ShapeDtypeStruct((8, d), F32)] + _scatter_out_shape(scatter),
        scratch_shapes=[pltpu.VMEM((8, d), F32)] + _exchange_scratch(n_s)
        + _small_allreduce_scratch([jax.ShapeDtypeStruct((8, d), F32)], []),
        input_output_aliases=_scatter_aliases(scatter, 5, 2), compiler_params=_params(),
    )(dproj, x, dh1, g_mix, w_in, *scatter_args)


def _small_allreduce(whole_in, rows_in, whole_out, rows_out, whole_land, rows_land, rows_sum, send_sems, recv_sems):
    n_w, n_r = len(whole_in), len(rows_in)
    per = [r.shape[0] // N_DEV for r in rows_in]
    me = _my_index()

    def dev(s):
        return (s // 4, (s // 2) % 2, s % 2)

    def rows_of(t, s):
        return pl.ds(s * per[t], per[t])

    def mine(t):
        return pl.ds(pl.multiple_of(me * per[t], 8), per[t])

    def partial(t, s, slot):
        if t < n_w:
            src, dst = whole_in[t], whole_land[t]
        else:
            src, dst = rows_in[t - n_w].at[rows_of(t - n_w, s)], rows_land[t - n_w]
        return pltpu.make_async_remote_copy(
            src_ref=src, dst_ref=dst.at[slot], send_sem=send_sems.at[t, s], recv_sem=recv_sems.at[t, slot],
            device_id=dev(s), device_id_type=MESH)

    def summed(t, s, rows, slot):
        return pltpu.make_async_remote_copy(
            src_ref=rows_sum[t].at[rows], dst_ref=rows_sum[t].at[rows], send_sem=send_sems.at[n_w + n_r + t, s],
            recv_sem=recv_sems.at[n_w + n_r + t, slot], device_id=dev(s), device_id_type=MESH)

    def send_partials():
        for s in range(N_DEV):
            @pl.when(s != me)
            def _():
                for t in range(n_w + n_r):
                    partial(t, s, me).start()
        for t in range(n_w):
            whole_land[t][me] = whole_in[t][...]
        for t in range(n_r):
            rows_land[t][me] = rows_in[t][mine(t), :]

    def reduce_and_send_sums():
        for s in range(N_DEV):
            @pl.when(s != me)
            def _():
                for t in range(n_w + n_r):
                    partial(t, s, s).wait_recv()
        for t in range(n_w):
            total = whole_land[t][0]
            for s in range(1, N_DEV):
                total = total + whole_land[t][s]
            whole_out[t][...] = total
        for t in range(n_r):
            total = rows_land[t][0]
            for s in range(1, N_DEV):
                total = total + rows_land[t][s]
            rows_sum[t][mine(t), :] = total
        for s in range(N_DEV):
            @pl.when(s != me)
            def _():
                for t in range(n_r):
                    summed(t, s, mine(t), me).start()

    def finish():
        for s in range(N_DEV):
            @pl.when(s != me)
            def _():
                for t in range(n_r):
                    summed(t, s, rows_of(t, s), s).wait_recv()
                    summed(t, s, mine(t), me).wait_send()
                for t in range(n_w + n_r):
                    partial(t, s, me).wait_send()
        for t in range(n_r):
            rows_out[t][...] = rows_sum[t][...]

    return send_partials, reduce_and_send_sums, finish


def _small_allreduce_scratch(whole, by_rows):
    n_sem = len(whole) + 2 * len(by_rows)
    return ([pltpu.VMEM((N_DEV,) + a.shape, F32) for a in whole]
            + [pltpu.VMEM((N_DEV, a.shape[0] // N_DEV, a.shape[1]), F32) for a in by_rows]
            + [pltpu.VMEM(a.shape, F32) for a in by_rows]
            + [pltpu.SemaphoreType.DMA((n_sem, N_DEV)), pltpu.SemaphoreType.DMA((n_sem, N_DEV))])


def _adam_update(g, w, m, v):
    m_new = ADAM_B1 * m + (1.0 - ADAM_B1) * g
    v_new = ADAM_B2 * v + (1.0 - ADAM_B2) * jnp.square(g)
    m_hat = m_new / (1.0 - ADAM_B1 ** ADAM_STEP)
    v_hat = v_new / (1.0 - ADAM_B2 ** ADAM_STEP)
    return -ADAM_LR * (m_hat / (jnp.sqrt(v_hat) + ADAM_EPS) + ADAM_WD * w), m_new, v_new


def _adamw_groups(name, groups):
    n = len(groups)

    def body(*refs):
        for k in range(n):
            g_ref, w_ref, m_ref, v_ref = refs[4 * k:4 * k + 4]
            g_out, d_out, m_out, v_out = refs[4 * n + 4 * k:4 * n + 4 * k + 4]
            if len(g_ref.shape) == 3:
                g = g_ref[0].astype(F32)
                for s in range(1, g_ref.shape[0]):
                    g = g + g_ref[s].astype(F32)
            else:
                g = g_ref[...]
            g_out[...] = g
            d_out[...], m_out[...], v_out[...] = _adam_update(g, w_ref[...], m_ref[...], v_ref[...])

    flat = [a for grp in groups for a in grp]
    out = pl.pallas_call(
        body, name=name, out_shape=[jax.ShapeDtypeStruct(grp[1].shape, F32) for grp in groups for _ in range(4)],
        compiler_params=pltpu.CompilerParams(vmem_limit_bytes=VMEM_LIMIT))(*flat)
    return [out[4 * k:4 * k + 4] for k in range(n)]


def _adamw(name, parts, w, m, v, row_block):
    n_src, rows, cols = parts.shape
    rb = min(row_block, rows)

    def body(p_ref, w_ref, m_ref, v_ref, g_out, d_out, m_out, v_out):
        g = p_ref[0].astype(F32)
        for s in range(1, n_src):
            g = g + p_ref[s].astype(F32)
        g_out[...] = g
        d_out[...], m_out[...], v_out[...] = _adam_update(g, w_ref[...], m_ref[...], v_ref[...])

    spec = pl.BlockSpec((rb, cols), lambda i: (i, 0))
    return pl.pallas_call(
        body, name=name, grid=(rows // rb,),
        in_specs=[pl.BlockSpec((n_src, rb, cols), lambda i: (0, i, 0)), spec, spec, spec],
        out_specs=[spec] * 4, out_shape=[jax.ShapeDtypeStruct((rows, cols), F32)] * 4,
        compiler_params=pltpu.CompilerParams(dimension_semantics=("parallel",), vmem_limit_bytes=VMEM_LIMIT),
    )(parts, w, m, v)


def _block_diag(blocks):
    g, a, b = blocks.shape
    eye = jnp.eye(g, dtype=blocks.dtype)
    return (eye[:, None, :, None] * blocks[:, :, None, :]).reshape(g * a, g * b)


def kernel(x, p, norm_mix_g, w_in, pool_w, pool_b, pool_scale, conv_w, conv_b, gate_a_w, gate_a_b, gate_x_w, gate_x_b, lru_L, w_out, norm_mlp_g, w_up, w_down, norm_ple_g, w_ple_gate, b_ple_gate, w_ple_proj, norm_final_g, loss_target, m_norm_mix_g, m_w_in, m_pool_w, m_pool_b, m_pool_scale, m_conv_w, m_conv_b, m_gate_a_w, m_gate_a_b, m_gate_x_w, m_gate_x_b, m_lru_L, m_w_out, m_norm_mlp_g, m_w_up, m_w_down, m_norm_ple_g, m_w_ple_gate, m_b_ple_gate, m_w_ple_proj, m_norm_final_g, v_norm_mix_g, v_w_in, v_pool_w, v_pool_b, v_pool_scale, v_conv_w, v_conv_b, v_gate_a_w, v_gate_a_b, v_gate_x_w, v_gate_x_b, v_lru_L, v_w_out, v_norm_mlp_g, v_w_up, v_w_down, v_norm_ple_g, v_w_ple_gate, v_b_ple_gate, v_w_ple_proj, v_norm_final_g):
    t_len, d = x.shape[1], x.shape[2]
    tbs = {k: min(v, t_len) for k, v in TIME_BLOCKS.items()}
    me = _my_index()

    win_g, wout_g, convw_g = _gather("gather_mixer_weights",
                                     [w_in[0].T.astype(BF16), w_out[0].astype(BF16), conv_w[0]])
    w_in_f = win_g.reshape(-1, d)
    conv_w_f = jnp.transpose(convw_g, (1, 0, 2)).reshape(convw_g.shape[1], -1)
    wp_bd = _block_diag(pool_w[0]).astype(BF16)
    wg_bd = jnp.concatenate([_block_diag(gate_a_w[0]), _block_diag(gate_x_w[0])], axis=1).astype(BF16)
    gate_b2 = jnp.concatenate([gate_a_b.reshape(1, -1), gate_x_b.reshape(1, -1)], axis=1)
    mixer_small = (norm_mix_g, w_in_f, wp_bd, pool_b.reshape(1, -1), pool_scale, conv_w_f, conv_b, wg_bd, gate_b2, lru_L,
                   wout_g.reshape(-1, d))

    x2 = x[0]
    later = [w_up[0].astype(BF16), w_down[0].astype(BF16), w_ple_gate[0].astype(BF16), w_ple_proj[0].astype(BF16)]
    h1, z1, proj, hs, cat, lru_saved, dpool_saved, wup_g, wdn_g, wgate_g, wproj_g = _mix_fwd(
        x2, *mixer_small, later, [_core_major_slot, _core_major_slot, None, None], GATHER_FORWARD_AT, tbs['mix_fwd'])
    w_down_f = wdn_g.reshape(-1, d)
    w_proj_f = jnp.transpose(wproj_g, (1, 0, 2)).reshape(wproj_g.shape[1], -1)
    h2, z2, up = _mlp_fwd(h1, norm_mlp_g, wup_g, w_down_f, tbs['mlp_fwd'])
    dh2, ple_vec, dw_gate, dw_proj = _ple(h2, p[0, 0], loss_target[0], norm_ple_g, wgate_g.reshape(-1, d), b_ple_gate,
                                          w_proj_f, norm_final_g.reshape(1, -1), tbs['ple'])
    everyone = list(range(N_DEV))
    n_proj = w_ple_proj.shape[2]
    dz2_0, dw_up_0, dw_down_0 = _mlp_bwd_part(
        0, MLP_BWD_SPLIT, dh2, z2, up, wup_g, w_down_f, None, h1, norm_mlp_g, _scatter_plan([], [], []), tbs['mlp_bwd'])
    half = N_DEV // MLP_BWD_SPLIT
    south = [_device_of_core_major_slot(k) for k in range(half)]
    north = [_device_of_core_major_slot(k) for k in range(half, N_DEV)]
    scatter = _scatter_plan(
        [dw_up_0, dw_down_0, dw_gate.reshape(N_DEV, -1, d), jnp.transpose(dw_proj.reshape(-1, N_DEV, n_proj), (1, 0, 2))],
        [south, south, everyone, everyone], [None, None, None, None])
    dh1, mlp_vec, dw_up_1, dw_down_1, recv_up, recv_down, recv_gate, recv_proj = _mlp_bwd_part(
        1, MLP_BWD_SPLIT, dh2, z2, up, wup_g, w_down_f, dz2_0, h1, norm_mlp_g, scatter, tbs['mlp_bwd'])
    dw_out, _, _ = _wgrad("wgrad_out", cat, dh1, [], [], tbs['wgrad_out'])
    scatter = _scatter_plan([dw_up_1, dw_down_1, dw_out.reshape(N_DEV, -1, d)], [north, north, everyone],
                            [recv_up, recv_down, None])
    dproj, v512, dpw, dga, dgx, recv_up, recv_down, recv_out = _mix_bwd(
        dh1, proj, hs, lru_saved, dpool_saved, wp_bd, pool_b.reshape(1, -1), pool_scale, conv_w_f, wg_bd, lru_L, wout_g.reshape(-1, d),
        scatter, tbs['mix_bwd'])
    rows1024 = jnp.concatenate([jnp.zeros((1, d), F32), mlp_vec[0:1], ple_vec[1:2], ple_vec[0:1], ple_vec[2:4],
                                jnp.zeros((2, d), F32)], axis=0)
    dw_in_t, (rows1024, rows512), (g_pool_w, g_gate_a_w, g_gate_x_w) = _wgrad(
        "wgrad_in", dproj, z1, [rows1024, v512], [dpw, dga, dgx], tbs['wgrad_in'])
    scatter = _scatter_plan([dw_in_t.reshape(N_DEV, -1, d)], [everyone], [None])
    dx, in_vec, recv_in = _in_bwd(dproj, x2, dh1, norm_mix_g, w_in_f, scatter, tbs['in_bwd'])
    rows1024 = jnp.concatenate([in_vec[0:1], rows1024[1:]], axis=0)
    received = [recv_in, recv_out, recv_up, recv_down, recv_gate, recv_proj]

    shard_w = [w_in[0].T, w_out[0], w_up[0], w_down[0], w_ple_gate[0], w_ple_proj[0]]
    shard_m = [m_w_in[0].T, m_w_out[0], m_w_up[0], m_w_down[0], m_w_ple_gate[0], m_w_ple_proj[0]]
    shard_v = [v_w_in[0].T, v_w_out[0], v_w_up[0], v_w_down[0], v_w_ple_gate[0], v_w_ple_proj[0]]
    names = ["w_in", "w_out", "w_up", "w_down", "w_ple_gate", "w_ple_proj"]
    res = {}
    gridded = ("w_up", "w_down")
    for nm, parts, w_s, m_s, v_s in zip(names, received, shard_w, shard_m, shard_v):
        if nm in gridded:
            res[nm] = [r[None] for r in _adamw("adamw_" + nm, parts, w_s, m_s, v_s, ADAM_ROW_BLOCK)]
    rest = [k for k, nm in enumerate(names) if nm not in gridded]
    for k, out in zip(rest, _adamw_groups("adamw_medium", [(received[k], shard_w[k], shard_m[k], shard_v[k]) for k in rest])):
        res[names[k]] = [r[None] for r in out]
    res["w_in"] = [jnp.swapaxes(r, 1, 2) for r in res["w_in"]]

    def rows_of_1024(a, b, c, e, f):
        return jnp.concatenate([a, b, c, e, f.reshape(1, -1), jnp.zeros((3, d), F32)], axis=0)

    def rows_of_512(scale, bias, cb, lru, ga, gx):
        z = jnp.zeros((1, 512), F32)
        return jnp.concatenate([scale, bias.reshape(1, -1), cb, lru, z, z, z, z, ga.reshape(1, -1), gx.reshape(1, -1),
                                z, z, z, z, z, z], axis=0)

    n_conv = conv_w.shape[2]
    groups = [
        (rows1024, *[rows_of_1024(*t) for t in (
            (norm_mix_g, norm_mlp_g, norm_ple_g, b_ple_gate, norm_final_g),
            (m_norm_mix_g, m_norm_mlp_g, m_norm_ple_g, m_b_ple_gate, m_norm_final_g),
            (v_norm_mix_g, v_norm_mlp_g, v_norm_ple_g, v_b_ple_gate, v_norm_final_g))]),
        (rows512, *[rows_of_512(*t) for t in (
            (pool_scale, pool_b, conv_b, lru_L, gate_a_b, gate_x_b),
            (m_pool_scale, m_pool_b, m_conv_b, m_lru_L, m_gate_a_b, m_gate_x_b),
            (v_pool_scale, v_pool_b, v_conv_b, v_lru_L, v_gate_a_b, v_gate_x_b))]),
        (g_pool_w, *[a.reshape(-1, a.shape[-1]) for a in (pool_w, m_pool_w, v_pool_w)]),
        (g_gate_a_w, *[a.reshape(-1, a.shape[-1]) for a in (gate_a_w, m_gate_a_w, v_gate_a_w)]),
        (g_gate_x_w, *[a.reshape(-1, a.shape[-1]) for a in (gate_x_w, m_gate_x_w, v_gate_x_w)]),
        (lax.dynamic_slice_in_dim(rows512[4:8], me * n_conv, n_conv, axis=1), conv_w[0], m_conv_w[0], v_conv_w[0]),
    ]
    r1024, r512, r_pool, r_ga, r_gx, r_conv = _adamw_groups("adamw_small", groups)
    loss = rows1024[5, 0]
    for k, nm in enumerate(["norm_mix_g", "norm_mlp_g", "norm_ple_g", "b_ple_gate"]):
        res[nm] = [a[k:k + 1] for a in r1024]
    res["norm_final_g"] = [a[4] for a in r1024]
    res["pool_scale"] = [a[0:1] for a in r512]
    res["pool_b"] = [a[1:2].reshape(pool_b.shape) for a in r512]
    res["conv_b"] = [a[2:3] for a in r512]
    res["lru_L"] = [a[3:4] for a in r512]
    res["gate_a_b"] = [a[8:9].reshape(gate_a_b.shape) for a in r512]
    res["gate_x_b"] = [a[9:10].reshape(gate_x_b.shape) for a in r512]
    res["pool_w"] = [a.reshape(pool_w.shape) for a in r_pool]
    res["gate_a_w"] = [a.reshape(gate_a_w.shape) for a in r_ga]
    res["gate_x_w"] = [a.reshape(gate_x_w.shape) for a in r_gx]
    res["conv_w"] = [a[None] for a in r_conv]
    order = ["norm_mix_g", "w_in", "pool_w", "pool_b", "pool_scale", "conv_w", "conv_b", "gate_a_w", "gate_a_b",
             "gate_x_w", "gate_x_b", "lru_L", "w_out", "norm_mlp_g", "w_up", "w_down", "norm_ple_g", "w_ple_gate",
             "b_ple_gate", "w_ple_proj", "norm_final_g"]
    return (loss, dx[None], *[res[nm][kind] for kind in range(4) for nm in order])
```

```python
import jax
import jax.numpy as jnp
from jax import lax
from jax.experimental import pallas as pl
from jax.experimental.pallas import tpu as pltpu

F32 = jnp.float32
BF16 = jnp.bfloat16
MESH = pl.DeviceIdType.MESH

N_DEV = 8
RMS_EPS = 1e-6
LRU_C = 8.0
POOL_WINDOWS = (2, 4, 8, 16)
N_POOL_GROUPS = 4
LRU_HEADS = 8
HALO = 16
SUB = 8
GELU_C0 = 0.7978845608028654
GELU_C1 = 0.044715

ADAM_LR = 0.001
ADAM_B1 = 0.9
ADAM_B2 = 0.999
ADAM_EPS = 1e-08
ADAM_WD = 0.01
ADAM_STEP = 10

VMEM_LIMIT = 60 * 1024 * 1024
TIME_BLOCKS = dict(mix_fwd=512, mlp_fwd=512, ple=512, mlp_bwd=512, wgrad_out=2048, mix_bwd=512, wgrad_in=2048, in_bwd=512)
ADAM_ROW_BLOCK = 256
SCAN_UNROLL = 8
MLP_BWD_SPLIT = 2
GATHER_FORWARD_AT = (0.5, 0.875, 1.0, 1.0)


def _params(n_arbitrary=1):
    return pltpu.CompilerParams(dimension_semantics=("arbitrary",) * n_arbitrary, vmem_limit_bytes=VMEM_LIMIT)


def _dot(a, b):
    return jnp.dot(a, b, preferred_element_type=F32)


def _dot_nt(a, b):
    return lax.dot_general(a, b, (((1,), (1,)), ((), ())), preferred_element_type=F32)


def _dot_tn(a, b):
    return lax.dot_general(a, b, (((0,), (0,)), ((), ())), preferred_element_type=F32)


def _rms_fwd(x, g):
    r = lax.rsqrt(jnp.mean(x * x, axis=-1, keepdims=True) + RMS_EPS)
    xh = x * r
    return xh * g, xh, r


def _rms_bwd(xh, r, g, dz):
    dxh = dz * g
    return r * (dxh - xh * jnp.mean(dxh * xh, axis=-1, keepdims=True))


def _colsum(a):
    return jnp.sum(a, axis=0, keepdims=True)


def _sigmoid(a):
    return 0.5 * jnp.tanh(0.5 * a) + 0.5


def _gelu_parts(u):
    u2 = u * u
    th = jnp.tanh(GELU_C0 * (u + GELU_C1 * u * u2))
    gel = 0.5 * u * (1.0 + th)
    dgel = 0.5 * (1.0 + th) + 0.5 * u * (1.0 - th * th) * (GELU_C0 * (1.0 + 3.0 * GELU_C1 * u2))
    return gel, dgel


def _my_index():
    return 4 * lax.axis_index("x") + 2 * lax.axis_index("y") + lax.axis_index("c")


def _all_to_all(srcs_of, dsts, send_sems, recv_sems, local_sems, dests=None):
    n = len(dsts)
    me = _my_index()
    dests = [list(range(N_DEV))] * n if dests is None else dests

    def remote(t, s):
        return pltpu.make_async_remote_copy(
            src_ref=srcs_of[t](s), dst_ref=dsts[t].at[me], send_sem=send_sems.at[t, s], recv_sem=recv_sems.at[t, me],
            device_id=(s // 4, (s // 2) % 2, s % 2), device_id_type=MESH)

    def arrival(t, s):
        return pltpu.make_async_remote_copy(
            src_ref=srcs_of[t](dests[t][0]), dst_ref=dsts[t].at[s], send_sem=send_sems.at[t, s],
            recv_sem=recv_sems.at[t, s], device_id=(s // 4, (s // 2) % 2, s % 2), device_id_type=MESH)

    def local(t, s):
        return pltpu.make_async_copy(srcs_of[t](s), dsts[t].at[s], local_sems.at[t])

    def start():
        for s in range(N_DEV):
            to_s = [t for t in range(n) if s in dests[t]]

            @pl.when(s == me)
            def _():
                for t in to_s:
                    local(t, s).start()

            @pl.when(s != me)
            def _():
                for t in to_s:
                    remote(t, s).start()

    def wait():
        for s in range(N_DEV):
            to_s = [t for t in range(n) if s in dests[t]]

            @pl.when(s == me)
            def _():
                for t in to_s:
                    local(t, s).wait()
                    for src in range(N_DEV):
                        if src != s:
                            arrival(t, src).wait_recv()

            @pl.when(s != me)
            def _():
                for t in to_s:
                    remote(t, s).wait_send()

    return start, wait


N_GATHER_COPIES = 7


def _core_major_slot(dev):
    return 4 * dev[2] + 2 * dev[0] + dev[1]


def _device_of_core_major_slot(k):
    return (k % 4) * 2 + k // 4


def _two_level_gather(srcs, dsts, send_sems, recv_sems, local_sems, slots=None):
    n = len(dsts)
    x, y, c = lax.axis_index("x"), lax.axis_index("y"), lax.axis_index("c")
    me, sibling = (x, y, c), (x, y, 1 - c)
    chips = [(1 - x, y), (x, 1 - y), (1 - x, 1 - y)]

    def slot(t, dev):
        return 4 * dev[0] + 2 * dev[1] + dev[2] if slots is None or slots[t] is None else slots[t](dev)

    def copy(t, k, block, to, src=None):
        return pltpu.make_async_remote_copy(
            src_ref=dsts[t].at[slot(t, block)] if src is None else src, dst_ref=dsts[t].at[slot(t, block)],
            send_sem=send_sems.at[t, k], recv_sem=recv_sems.at[t, k], device_id=to, device_id_type=MESH)

    def local(t):
        return pltpu.make_async_copy(srcs[t], dsts[t].at[slot(t, me)], local_sems.at[t])

    def start():
        for t in range(n):
            local(t).start()
            for j, chip in enumerate(chips):
                copy(t, 1 + j, me, (*chip, c), src=srcs[t]).start()
            copy(t, 0, me, sibling, src=srcs[t]).start()

    def forward(t):
        for j, chip in enumerate(chips):
            copy(t, 1 + j, (*chip, c), me).wait_recv()
            copy(t, 4 + j, (*chip, c), sibling).start()

    def finish():
        for t in range(n):
            copy(t, 0, sibling, me).wait_recv()
            for j, chip in enumerate(chips):
                copy(t, 4 + j, (*chip, 1 - c), me).wait_recv()
            copy(t, 0, me, sibling, src=srcs[t]).wait_send()
            for j, chip in enumerate(chips):
                copy(t, 1 + j, me, (*chip, c), src=srcs[t]).wait_send()
                copy(t, 4 + j, (*chip, c), sibling).wait_send()
            local(t).wait()

    return start, forward, finish


def _hosted_gather(i, nb, forward_at, srcs, dsts, sems, slots=None):
    start, forward, finish = _two_level_gather(srcs, dsts, *sems, slots)

    def after_step():
        for t, f in enumerate(forward_at):
            @pl.when(i == min(nb - 1, int(f * nb)))
            def _():
                forward(t)

        @pl.when(i == nb - 1)
        def _():
            finish()

    return start, after_step


def _gather_scratch(n):
    return [pltpu.SemaphoreType.DMA((n, N_GATHER_COPIES)), pltpu.SemaphoreType.DMA((n, N_GATHER_COPIES)),
            pltpu.SemaphoreType.DMA((n,))]


def _gather(name, srcs):
    n = len(srcs)

    def body(*refs):
        start, forward, finish = _two_level_gather(refs[:n], refs[n:2 * n], *refs[2 * n:])
        start()
        for t in range(n):
            forward(t)
        finish()

    any_spec = pl.BlockSpec(memory_space=pl.ANY)
    return pl.pallas_call(
        body, name=name, in_specs=[any_spec] * n, out_specs=[any_spec] * n,
        out_shape=[jax.ShapeDtypeStruct((N_DEV,) + a.shape, a.dtype) for a in srcs], scratch_shapes=_gather_scratch(n),
    )(*srcs)


def _scatter_plan(blocks, dests, landing):
    return dict(blocks=list(blocks), dests=[list(dd) for dd in dests], landing=list(landing))


def _scatter_args(plan):
    return plan['blocks'] + [a for a in plan['landing'] if a is not None]


def _scatter_out_shape(plan):
    return [jax.ShapeDtypeStruct((N_DEV,) + b.shape[1:], b.dtype) for b in plan['blocks']]


def _scatter_aliases(plan, first_in, first_out):
    given = [t for t, a in enumerate(plan['landing']) if a is not None]
    return {first_in + len(plan['blocks']) + k: first_out + t for k, t in enumerate(given)}


def _scatter_ops(plan, in_refs, out_refs, sems):
    n = len(plan['blocks'])
    srcs_of = [(lambda s, r=in_refs[t], dd=plan['dests'][t]: r.at[dd.index(s)]) for t in range(n)]
    return _all_to_all(srcs_of, out_refs, *sems, dests=plan['dests'])


def _exchange_scratch(n):
    return [pltpu.SemaphoreType.DMA((n, N_DEV)), pltpu.SemaphoreType.DMA((n, N_DEV)), pltpu.SemaphoreType.DMA((n,))]


def _const_spec(shape):
    nd = len(shape)
    return pl.BlockSpec(shape, lambda i: (0,) * nd, pipeline_mode=pl.Buffered(1))


def _pool_windows(up_ext, n, forward):
    sh = (lambda k: k) if forward else (lambda k: n - k)
    s2 = up_ext + pltpu.roll(up_ext, sh(1), 0)
    t4 = s2[:, 128:]
    s4 = t4 + pltpu.roll(t4, sh(2), 0)
    t8 = s4[:, 128:]
    s8 = t8 + pltpu.roll(t8, sh(4), 0)
    t16 = s8[:, 128:]
    s16 = t16 + pltpu.roll(t16, sh(8), 0)
    return jnp.concatenate([s2[:, :128], s4[:, :128], s8[:, :128], s16], axis=1)


def _inv_count_head():
    t = jnp.arange(1, HALO + 1, dtype=F32)[:, None]
    return jnp.concatenate([jnp.broadcast_to(1.0 / jnp.minimum(t, float(w)), (HALO, 128)) for w in POOL_WINDOWS], axis=1)


def _scale_by_inv_count(v, is_first_block, inv_head):
    inv_row = jnp.concatenate([jnp.full((1, 128), 1.0 / w, F32) for w in POOL_WINDOWS], axis=1)
    head = v[0:HALO] * jnp.where(is_first_block, inv_head, inv_row)
    return jnp.concatenate([head, v[HALO:] * inv_row], axis=0)


def _lru_decay(r, a, c_l, first_row):
    a2 = a * a
    m2 = -jnp.tanh(c_l * r) * (a2 + 1.0)
    return a2, m2, jnp.where(first_row, 1.0, jnp.sqrt(m2))


def _log_sigmoid(v):
    return -(jnp.maximum(-v, 0.0) + jnp.log1p(jnp.exp(-jnp.abs(v))))


def _conv_fwd(ul_ext, cw, cb):
    return (cb + cw[3:4, :] * ul_ext + cw[2:3, :] * pltpu.roll(ul_ext, 1, 0)
            + cw[1:2, :] * pltpu.roll(ul_ext, 2, 0) + cw[0:1, :] * pltpu.roll(ul_ext, 3, 0))


def _mix_fwd(x, g_mix, w_in, wp_bd, pool_b, pool_scale, conv_w, conv_b, wg_bd, gate_b, lru_l, w_out, gather_srcs,
             gather_slots, forward_at, tb):
    t_len, d = x.shape
    nb = t_len // tb
    n_g = len(gather_srcs)

    def body(*refs):
        (x_ref, g_ref, win_ref, wp_ref, pb_ref, ps_ref, cw_ref, cb_ref, wg_ref, gb_ref, l_ref, wout_ref,
         invh_ref) = refs[:13]
        gsrc = refs[13:13 + n_g]
        h1_ref, z1_ref, proj_ref, hs_ref, cat_ref, lru_ref, dpool_ref = refs[13 + n_g:20 + n_g]
        gdst = refs[20 + n_g:20 + 2 * n_g]
        ext_ref, a_ref, b_ref, hc_ref, send_sems, recv_sems, local_sems = refs[20 + 2 * n_g:]
        i = pl.program_id(0)
        start_gather, after_step = _hosted_gather(i, nb, forward_at, gsrc, gdst, (send_sems, recv_sems, local_sems),
                                                  gather_slots)

        @pl.when(i == 0)
        def _():
            start_gather()
            ext_ref[0:HALO, :] = jnp.zeros((HALO, 1024), F32)
            hc_ref[...] = jnp.zeros_like(hc_ref)

        xv = x_ref[...]
        z, _, _ = _rms_fwd(xv, g_ref[...])
        zb = z.astype(BF16)
        z1_ref[...] = zb
        proj = _dot_nt(zb, win_ref[...])
        proj_ref[...] = proj
        ext_ref[HALO:, :] = proj[:, 0:1024]
        ug = proj[:, 1024:1536]
        n = tb + HALO
        up_ext = ext_ref[:, 0:512]
        win = _pool_windows(up_ext, n, True)[HALO:]
        dpool = _scale_by_inv_count(win, i == 0, invh_ref[...]) - proj[:, 0:512]
        dpoolb = dpool.astype(BF16)
        dpool_ref[...] = dpoolb
        q = _dot(dpoolb, wp_ref[...]) + pb_ref[...]
        y_pool = q * ps_ref[...]
        xb = _conv_fwd(ext_ref[:, 512:1024], cw_ref[...], cb_ref[...])[HALO:]
        first_row = (i * tb + lax.broadcasted_iota(jnp.int32, (tb, 1), 0)) == 0
        c_l = LRU_C * _log_sigmoid(l_ref[...])
        gp = _dot(xb.astype(BF16), wg_ref[...]) + gb_ref[...]
        r = _sigmoid(gp[:, :512])
        ig = _sigmoid(gp[:, 512:])
        a = jnp.exp(c_l * r)
        _, _, mult = _lru_decay(r, a, c_l, first_row)
        lru_ref[:, 0:512] = xb
        lru_ref[:, 512:1024] = r
        lru_ref[:, 1024:1536] = ig
        lru_ref[:, 1536:2048] = a
        a_ref[...] = a
        b_ref[...] = mult * (ig * xb)
        row = lax.broadcasted_iota(jnp.int32, (SUB, 512), 0)

        def group(j, hprev):
            o = pl.multiple_of(j * SUB, SUB)
            a8 = a_ref[pl.ds(o, SUB), :]
            b8 = b_ref[pl.ds(o, SUB), :]
            for sh in (1, 2, 4):
                ash = jnp.where(row >= sh, pltpu.roll(a8, sh, 0), 1.0)
                bsh = jnp.where(row >= sh, pltpu.roll(b8, sh, 0), 0.0)
                b8 = a8 * bsh + b8
                a8 = a8 * ash
            h8 = a8 * hprev + b8
            hs_ref[pl.ds(o, SUB), :] = h8
            return jnp.broadcast_to(h8[SUB - 1:SUB, :], (SUB, 512))

        def trip(k, carry):
            for u in range(SCAN_UNROLL):
                carry = group(k * SCAN_UNROLL + u, carry)
            return carry

        hc_ref[...] = lax.fori_loop(0, tb // (SUB * SCAN_UNROLL), trip, hc_ref[...])
        gel, dgel = _gelu_parts(ug)
        lru_ref[:, 2048:2560] = gel
        lru_ref[:, 2560:3072] = dgel
        y_lru = hs_ref[...] * gel
        catb = jnp.concatenate([y_pool, y_lru], axis=1).astype(BF16)
        cat_ref[...] = catb
        h1_ref[...] = xv + _dot(catb, wout_ref[...])
        ext_ref[0:HALO, :] = ext_ref[tb:tb + HALO, :]

        after_step()

    row_spec = lambda w: pl.BlockSpec((tb, w), lambda i: (i, 0))
    any_spec = pl.BlockSpec(memory_space=pl.ANY)
    smalls = [g_mix, w_in, wp_bd, pool_b, pool_scale, conv_w, conv_b, wg_bd, gate_b, lru_l, w_out, _inv_count_head()]
    return pl.pallas_call(
        body, name="mix_fwd", grid=(nb,),
        in_specs=[row_spec(d)] + [_const_spec(s.shape) for s in smalls] + [any_spec] * n_g,
        out_specs=[row_spec(d), row_spec(d), row_spec(1536), row_spec(512), row_spec(1024), row_spec(3072), row_spec(512)]
        + [any_spec] * n_g,
        out_shape=[jax.ShapeDtypeStruct((t_len, d), F32), jax.ShapeDtypeStruct((t_len, d), BF16),
                   jax.ShapeDtypeStruct((t_len, 1536), F32), jax.ShapeDtypeStruct((t_len, 512), F32),
                   jax.ShapeDtypeStruct((t_len, 1024), BF16), jax.ShapeDtypeStruct((t_len, 3072), F32),
                   jax.ShapeDtypeStruct((t_len, 512), BF16)]
        + [jax.ShapeDtypeStruct((N_DEV,) + s.shape, s.dtype) for s in gather_srcs],
        scratch_shapes=[pltpu.VMEM((tb + HALO, 1024), F32), pltpu.VMEM((tb, 512), F32), pltpu.VMEM((tb, 512), F32),
                        pltpu.VMEM((SUB, 512), F32)] + _gather_scratch(n_g),
        compiler_params=_params(),
    )(x, *smalls, *gather_srcs)


def _mlp_fwd(h1, g_mlp, w_up, w_down, tb):
    t_len, d = h1.shape
    nb = t_len // tb
    n_chunk, _, fc = w_up.shape

    def body(h1_ref, g_ref, wup_ref, wdn_ref, h2_ref, z2_ref, up_ref):
        xv = h1_ref[...]
        z, _, _ = _rms_fwd(xv, g_ref[...])
        zb = z.astype(BF16)
        z2_ref[...] = zb
        acc = xv
        for c in range(n_chunk):
            u = _dot(zb, wup_ref[c])
            up_ref[:, c * fc:(c + 1) * fc] = u.astype(BF16)
            act = jnp.square(jnp.maximum(u, 0.0)).astype(BF16)
            acc = acc + _dot(act, wdn_ref[c * fc:(c + 1) * fc, :])
        h2_ref[...] = acc

    row_spec = lambda w: pl.BlockSpec((tb, w), lambda i: (i, 0))
    return pl.pallas_call(
        body, name="mlp_fwd", grid=(nb,),
        in_specs=[row_spec(d), _const_spec(g_mlp.shape), _const_spec(w_up.shape), _const_spec(w_down.shape)],
        out_specs=[row_spec(d), row_spec(d), row_spec(n_chunk * fc)],
        out_shape=[jax.ShapeDtypeStruct((t_len, d), F32), jax.ShapeDtypeStruct((t_len, d), BF16),
                   jax.ShapeDtypeStruct((t_len, n_chunk * fc), BF16)],
        compiler_params=_params(),
    )(h1, g_mlp, w_up, w_down)


def _ple(h2, p, target, g_ple, w_gate, b_gate, w_proj, g_final, tb):
    t_len, d = h2.shape
    nb = t_len // tb
    pd = p.shape[1]

    def body(h2_ref, p_ref, tgt_ref, g_ref, wg_ref, bg_ref, wp_ref, gf_ref,
             dh2_ref, vec_ref, dwg_out, dwp_out, dwg_acc, dwp_acc, dwg_stage, dwp_stage):
        i = pl.program_id(0)

        @pl.when(i == 0)
        def _():
            vec_ref[...] = jnp.zeros_like(vec_ref)
            dwg_acc[...] = jnp.zeros_like(dwg_acc)
            dwp_acc[...] = jnp.zeros_like(dwp_acc)

        h2 = h2_ref[...]
        g2 = g_ref[...]
        z3, xh2, r2 = _rms_fwd(h2, g2)
        z3b = z3.astype(BF16)
        gate = _sigmoid(_dot(z3b, wg_ref[...]) + bg_ref[...])
        pb = p_ref[...].astype(BF16)
        pp = _dot(pb, wp_ref[...])
        h3 = h2 + gate * pp
        gf = gf_ref[...]
        y, xh3, r3 = _rms_fwd(h3, gf)
        err = y - tgt_ref[...]
        loss_rows = jnp.mean(err * err, axis=-1, keepdims=True)
        dy = err * (1.0 / d)
        dh3 = _rms_bwd(xh3, r3, gf, dy)
        dgl = (dh3 * pp) * (gate * (1.0 - gate))
        dpp = dh3 * gate
        dglb = dgl.astype(BF16)
        dwg_acc[...] += _dot_tn(z3b, dglb)
        dwp_acc[...] += _dot_tn(pb, dpp.astype(BF16))
        dz3 = _dot_nt(dglb, wg_ref[...])
        dh2_ref[...] = dh3 + _rms_bwd(xh2, r2, g2, dz3)
        vec_ref[0:1, :] += _colsum(dgl)
        vec_ref[1:2, :] += _colsum(dz3 * xh2)
        vec_ref[2:3, :] += _colsum(dy * xh3)
        vec_ref[3:4, :] += 0.5 * jnp.sum(loss_rows)

        @pl.when(i == nb - 1)
        def _():
            dwg_stage[...] = dwg_acc[...].astype(BF16)
            dwp_stage[...] = dwp_acc[...].astype(BF16)
            pltpu.sync_copy(dwg_stage, dwg_out)
            pltpu.sync_copy(dwp_stage, dwp_out)

    row_spec = lambda w: pl.BlockSpec((tb, w), lambda i: (i, 0))
    any_spec = pl.BlockSpec(memory_space=pl.ANY)
    smalls = [g_ple, w_gate, b_gate, w_proj, g_final]
    return pl.pallas_call(
        body, name="ple_fwd_bwd", grid=(nb,),
        in_specs=[row_spec(d), row_spec(pd), row_spec(d)] + [_const_spec(s.shape) for s in smalls],
        out_specs=[row_spec(d), pl.BlockSpec((8, d), lambda i: (0, 0)), any_spec, any_spec],
        out_shape=[jax.ShapeDtypeStruct((t_len, d), F32), jax.ShapeDtypeStruct((8, d), F32),
                   jax.ShapeDtypeStruct(w_gate.shape, BF16), jax.ShapeDtypeStruct(w_proj.shape, BF16)],
        scratch_shapes=[pltpu.VMEM(w_gate.shape, F32), pltpu.VMEM(w_proj.shape, F32), pltpu.VMEM(w_gate.shape, BF16),
                        pltpu.VMEM(w_proj.shape, BF16)],
        compiler_params=_params(),
    )(h2, p, target, *smalls)


def _mlp_bwd_part(part, n_part, dh2, z2, up, w_up, w_down, dz2_prev, h1, g_mlp, scatter, tb):
    t_len, d = dh2.shape
    nb = t_len // tb
    n_chunk_all, _, fc = w_up.shape
    n_chunk = n_chunk_all // n_part
    first, last = part == 0, part == n_part - 1

    def body(*refs):
        refs = list(refs)
        dh2_ref, z2_ref, up_ref, wup_ref, wdn_ref = refs[:5]
        del refs[:5]
        dzp_ref = None if first else refs.pop(0)
        h1_ref, g_ref = (refs.pop(0), refs.pop(0)) if last else (None, None)
        scatter_in = [refs.pop(0) for _ in _scatter_args(scatter)]
        out_ref = refs.pop(0)
        vec_ref = refs.pop(0) if last else None
        dwup_out, dwdn_out = refs.pop(0), refs.pop(0)
        scatter_out = [refs.pop(0) for _ in scatter['blocks']]
        dwup_acc, dwdn_acc, up_stage, dn_stage = refs[:4]
        if scatter['blocks']:
            start_scatter, wait_scatter = _scatter_ops(scatter, scatter_in, scatter_out, refs[4:])
        i = pl.program_id(0)

        @pl.when(i == 0)
        def _():
            if scatter['blocks']:
                start_scatter()
            dwup_acc[...] = jnp.zeros_like(dwup_acc)
            dwdn_acc[...] = jnp.zeros_like(dwdn_acc)
            if last:
                vec_ref[...] = jnp.zeros_like(vec_ref)

        dh2 = dh2_ref[...]
        dh2b = dh2.astype(BF16)
        z2b = z2_ref[...]
        dz2 = jnp.zeros((tb, d), F32) if first else dzp_ref[...]
        for c in range(n_chunk):
            u = up_ref[:, c * fc:(c + 1) * fc].astype(F32)
            ur = jnp.maximum(u, 0.0)
            dact = _dot_nt(dh2b, wdn_ref[c * fc:(c + 1) * fc, :])
            dupb = (dact * (2.0 * ur)).astype(BF16)
            dwdn_acc[c * fc:(c + 1) * fc, :] += _dot_tn((ur * ur).astype(BF16), dh2b)
            dwup_acc[c] += _dot_tn(z2b, dupb)
            dz2 = dz2 + _dot_nt(dupb, wup_ref[c])
        if last:
            g = g_ref[...]
            _, xh, r = _rms_fwd(h1_ref[...], g)
            out_ref[...] = dh2 + _rms_bwd(xh, r, g, dz2)
            vec_ref[0:1, :] += _colsum(dz2 * xh)
        else:
            out_ref[...] = dz2

        @pl.when(i == nb - 1)
        def _():
            for c in range(n_chunk):
                up_stage[...] = dwup_acc[c].astype(BF16)
                dn_stage[...] = dwdn_acc[c * fc:(c + 1) * fc, :].astype(BF16)
                pltpu.sync_copy(up_stage, dwup_out.at[c])
                pltpu.sync_copy(dn_stage, dwdn_out.at[c])
            if scatter['blocks']:
                wait_scatter()

    row_spec = lambda w: pl.BlockSpec((tb, w), lambda i: (i, 0))
    any_spec = pl.BlockSpec(memory_space=pl.ANY)
    args = [dh2, z2, up, w_up, w_down]
    in_specs = [row_spec(d), row_spec(d), pl.BlockSpec((tb, n_chunk * fc), lambda i: (i, part)),
                pl.BlockSpec((n_chunk, d, fc), lambda i: (part, 0, 0), pipeline_mode=pl.Buffered(1)),
                pl.BlockSpec((n_chunk * fc, d), lambda i: (part, 0), pipeline_mode=pl.Buffered(1))]
    if not first:
        args.append(dz2_prev)
        in_specs.append(row_spec(d))
    if last:
        args += [h1, g_mlp]
        in_specs += [row_spec(d), _const_spec(g_mlp.shape)]
    n_in = len(args)
    args += _scatter_args(scatter)
    in_specs += [any_spec] * len(_scatter_args(scatter))
    out_specs = [row_spec(d)]
    out_shape = [jax.ShapeDtypeStruct((t_len, d), F32)]
    if last:
        out_specs.append(pl.BlockSpec((8, d), lambda i: (0, 0)))
        out_shape.append(jax.ShapeDtypeStruct((8, d), F32))
    out_specs += [any_spec, any_spec]
    out_shape += [jax.ShapeDtypeStruct((n_chunk, d, fc), BF16), jax.ShapeDtypeStruct((n_chunk, fc, d), BF16)]
    n_out = len(out_shape)
    out_specs += [any_spec] * len(scatter['blocks'])
    out_shape += _scatter_out_shape(scatter)
    return pl.pallas_call(
        body, name=f"mlp_bwd_{part}", grid=(nb,), in_specs=in_specs, out_specs=out_specs, out_shape=out_shape,
        scratch_shapes=[pltpu.VMEM((n_chunk, d, fc), F32), pltpu.VMEM((n_chunk * fc, d), F32),
                        pltpu.VMEM((d, fc), BF16), pltpu.VMEM((fc, d), BF16)]
        + (_exchange_scratch(len(scatter['blocks'])) if scatter['blocks'] else []),
        input_output_aliases=_scatter_aliases(scatter, n_in, n_out), compiler_params=_params(),
    )(*args)


def _mix_bwd(dh1, proj, hs, lru_saved, dpool_saved, wp_bd, pool_b, pool_scale, conv_w, wg_bd, lru_l, w_out, scatter, tb):
    t_len, d = dh1.shape
    nb = t_len // tb
    n_s = len(scatter['blocks'])
    scatter_args = _scatter_args(scatter)

    def body(*refs):
        refs = list(refs)
        (dh1_ref, ul_ref, hs_ref, hsh_ref, lru_ref, dpool_ref,
         wp_ref, pb_ref, ps_ref, cw_ref, wg_ref, l_ref, wout_ref, invh_ref) = refs[:14]
        del refs[:14]
        scatter_in = refs[:len(scatter_args)]
        del refs[:len(scatter_args)]
        dproj_ref, v512_ref, dpw_ref, dga_ref, dgx_ref = refs[:5]
        recv = refs[5:5 + n_s]
        (dwp_acc, dwg_acc, v1024_ref, b_ref, gs_ref, ehead_ref, dxbhead_ref, hc_ref,
         send_sems, recv_sems, local_sems) = refs[5 + n_s:]
        i = pl.program_id(0)
        tbk = nb - 1 - i

        start_scatter, wait_scatter = _scatter_ops(scatter, scatter_in, recv, (send_sems, recv_sems, local_sems))

        @pl.when(i == 0)
        def _():
            start_scatter()
            for ref in (v512_ref, v1024_ref, dwp_acc, dwg_acc, ehead_ref, dxbhead_ref, hc_ref):
                ref[...] = jnp.zeros_like(ref)

        dcat = _dot_nt(dh1_ref[...].astype(BF16), wout_ref[...])

        has_prev = (tbk > 0).astype(F32)
        n = tb + HALO
        inv_head = invh_ref[...]

        dpoolb = dpool_ref[...]
        q = _dot(dpoolb, wp_ref[...]) + pb_ref[...]
        dyp = dcat[:, 0:512]
        dq = dyp * ps_ref[...]
        dqb = dq.astype(BF16)
        v512_ref[0:1, :] += _colsum(dyp * q)
        v512_ref[1:2, :] += _colsum(dq)
        dwp_acc[...] += _dot_tn(dpoolb, dqb)
        dd = _dot_nt(dqb, wp_ref[...])
        e = _scale_by_inv_count(dd, tbk == 0, inv_head)
        e_ext = jnp.concatenate([e, ehead_ref[...]], axis=0)
        du_pool = _pool_windows(e_ext, n, False)[0:tb] - dd
        ehead_ref[...] = e[0:HALO]

        gel, dgel = lru_ref[:, 2048:2560], lru_ref[:, 2560:3072]
        hsv = hs_ref[...]
        dcl = dcat[:, 512:1024]
        dhs = dcl * gel
        dug = dcl * hsv * dgel
        cw = cw_ref[...]
        xb, r, ig, a = lru_ref[:, 0:512], lru_ref[:, 512:1024], lru_ref[:, 1024:1536], lru_ref[:, 1536:2048]
        first_row = (tbk * tb + lax.broadcasted_iota(jnp.int32, (tb, 1), 0)) == 0
        c_l = LRU_C * _log_sigmoid(l_ref[...])
        a2, m2, mult = _lru_decay(r, a, c_l, first_row)
        b_ref[...] = dhs
        row = lax.broadcasted_iota(jnp.int32, (SUB, 512), 0)

        def group(jj, hnext):
            o = pl.multiple_of((tb // SUB - 1 - jj) * SUB, SUB)
            a8 = lru_ref[pl.ds(o, SUB), 1536:2048]
            d8 = b_ref[pl.ds(o, SUB), :]
            b8 = a8 * d8
            for sh in (1, 2, 4):
                ash = jnp.where(row < SUB - sh, pltpu.roll(a8, SUB - sh, 0), 1.0)
                bsh = jnp.where(row < SUB - sh, pltpu.roll(b8, SUB - sh, 0), 0.0)
                b8 = a8 * bsh + b8
                a8 = a8 * ash
            h8 = a8 * hnext + b8
            gs_ref[pl.ds(o, SUB), :] = d8 + jnp.where(row < SUB - 1, pltpu.roll(h8, SUB - 1, 0), hnext)
            return jnp.broadcast_to(h8[0:1, :], (SUB, 512))

        def trip(k, carry):
            for u in range(SCAN_UNROLL):
                carry = group(k * SCAN_UNROLL + u, carry)
            return carry

        hc_ref[...] = lax.fori_loop(0, tb // (SUB * SCAN_UNROLL), trip, hc_ref[...])
        gsum = gs_ref[...]
        hs_ext = jnp.concatenate([hsh_ref[...] * has_prev, hsv], axis=0)
        hprev = pltpu.roll(hs_ext, 1, 0)[SUB:]
        da = gsum * hprev
        dmult = jnp.where(first_row, 0.0, gsum * (ig * xb))
        di = gsum * mult * xb
        dxb = gsum * mult * ig
        dla = da * a - dmult * a2 * lax.rsqrt(m2)
        dr = dla * c_l
        v512_ref[3:4, :] += _colsum(dla * r)
        dgp = jnp.concatenate([dr * r * (1.0 - r), di * ig * (1.0 - ig)], axis=1)
        dgpb = dgp.astype(BF16)
        v1024_ref[0:1, :] += _colsum(dgp)
        dwg_acc[...] += _dot_tn(xb.astype(BF16), dgpb)
        dxb = dxb + _dot_nt(dgpb, wg_ref[...])
        n8 = tb + SUB
        dxb_ext = jnp.concatenate([dxb, dxbhead_ref[...]], axis=0)
        ul = ul_ref[...]
        du_lru = cw[3:4, :] * dxb
        v512_ref[7:8, :] += _colsum(dxb * ul)
        for j in range(1, 4):
            ahead = pltpu.roll(dxb_ext, n8 - j, 0)[0:tb]
            du_lru = du_lru + cw[3 - j:4 - j, :] * ahead
            v512_ref[4 + (3 - j):5 + (3 - j), :] += _colsum(ahead * ul)
        dxbhead_ref[...] = dxb[0:SUB]
        v512_ref[2:3, :] += _colsum(dxb)

        dproj_ref[...] = jnp.concatenate([du_pool, du_lru, dug], axis=1).astype(BF16)

        @pl.when(i == nb - 1)
        def _():
            v512_ref[3:4, :] = v512_ref[3:4, :] * (LRU_C * _sigmoid(-l_ref[...]))
            v512_ref[8:9, :] = v1024_ref[0:1, 0:512]
            v512_ref[9:10, :] = v1024_ref[0:1, 512:1024]
            for g in range(N_POOL_GROUPS):
                dpw_ref[g * 128:(g + 1) * 128, :] = dwp_acc[g * 128:(g + 1) * 128, g * 128:(g + 1) * 128]
            odd_head = (lax.broadcasted_iota(jnp.int32, (512, 128), 0) // 64) % 2 == 1
            for out_ref, col0 in ((dga_ref, 0), (dgx_ref, 512)):
                pairs = jnp.concatenate([dwg_acc[128 * k:128 * (k + 1), col0 + 128 * k:col0 + 128 * (k + 1)]
                                         for k in range(LRU_HEADS // 2)], axis=0)
                out_ref[...] = jnp.where(odd_head, pltpu.roll(pairs, 64, 1), pairs)[:, 0:64]
            wait_scatter()

    rev = lambda w: pl.BlockSpec((tb, w), lambda i: (nb - 1 - i, 0))
    halo = lambda rows, w: pl.BlockSpec((rows, w), lambda i: (jnp.maximum((nb - 1 - i) * (tb // rows) - 1, 0), 0))
    any_spec = pl.BlockSpec(memory_space=pl.ANY)
    smalls = [wp_bd, pool_b, pool_scale, conv_w, wg_bd, lru_l, w_out, _inv_count_head()]
    lru_third = pl.BlockSpec((tb, 512), lambda i: (nb - 1 - i, 1))
    return pl.pallas_call(
        body, name="mix_bwd", grid=(nb,),
        in_specs=[rev(d), lru_third, rev(512), halo(SUB, 512), rev(3072), rev(512)]
        + [_const_spec(s.shape) for s in smalls] + [any_spec] * len(scatter_args),
        out_specs=[rev(1536), pl.BlockSpec((16, 512), lambda i: (0, 0)), pl.BlockSpec((512, 128), lambda i: (0, 0)),
                   pl.BlockSpec((512, 64), lambda i: (0, 0)), pl.BlockSpec((512, 64), lambda i: (0, 0))]
        + [any_spec] * n_s,
        out_shape=[jax.ShapeDtypeStruct((t_len, 1536), BF16), jax.ShapeDtypeStruct((16, 512), F32),
                   jax.ShapeDtypeStruct((512, 128), F32), jax.ShapeDtypeStruct((512, 64), F32),
                   jax.ShapeDtypeStruct((512, 64), F32)]
        + _scatter_out_shape(scatter),
        scratch_shapes=[pltpu.VMEM(wp_bd.shape, F32), pltpu.VMEM(wg_bd.shape, F32), pltpu.VMEM((8, 1024), F32),
                        pltpu.VMEM((tb, 512), F32), pltpu.VMEM((tb, 512), F32), pltpu.VMEM((HALO, 512), F32),
                        pltpu.VMEM((SUB, 512), F32), pltpu.VMEM((SUB, 512), F32)]
        + _exchange_scratch(n_s),
        input_output_aliases=_scatter_aliases(scatter, 6 + len(smalls), 5), compiler_params=_params(),
    )(dh1, proj, hs, hs, lru_saved, dpool_saved, *smalls, *scatter_args)


def _wgrad(name, a, b, whole, by_rows, tb):
    t_len, m = a.shape
    n = b.shape[1]
    nb = t_len // tb
    n_w, n_r = len(whole), len(by_rows)
    n_small = n_w + n_r

    def body(*refs):
        a_ref, b_ref = refs[:2]
        small_in = refs[2:2 + n_small]
        out_ref = refs[2 + n_small]
        small_out = refs[3 + n_small:3 + 2 * n_small]
        acc_ref, stage_ref = refs[3 + 2 * n_small:5 + 2 * n_small]
        rest = refs[5 + 2 * n_small:]
        if n_small:
            send_partials, reduce_and_send_sums, finish_small = _small_allreduce(
                small_in[:n_w], small_in[n_w:], small_out[:n_w], small_out[n_w:], rest[:n_w], rest[n_w:n_small],
                rest[n_small:n_small + n_r], *rest[n_small + n_r:])
        i = pl.program_id(0)

        @pl.when(i == 0)
        def _():
            if n_small:
                send_partials()
            acc_ref[...] = jnp.zeros_like(acc_ref)

        acc_ref[...] += _dot_tn(a_ref[...], b_ref[...].astype(BF16))

        if n_small:
            @pl.when(i == nb // 2)
            def _():
                reduce_and_send_sums()

        @pl.when(i == nb - 1)
        def _():
            stage_ref[...] = acc_ref[...].astype(BF16)
            pltpu.sync_copy(stage_ref, out_ref)
            if n_small:
                finish_small()

    small = list(whole) + list(by_rows)
    vmem_spec = pl.BlockSpec(memory_space=pltpu.VMEM)
    res = pl.pallas_call(
        body, name=name, grid=(nb,),
        in_specs=[pl.BlockSpec((tb, m), lambda i: (i, 0)), pl.BlockSpec((tb, n), lambda i: (i, 0))] + [vmem_spec] * n_small,
        out_specs=[pl.BlockSpec(memory_space=pl.ANY)] + [vmem_spec] * n_small,
        out_shape=[jax.ShapeDtypeStruct((m, n), BF16)] + [jax.ShapeDtypeStruct(s_.shape, F32) for s_ in small],
        scratch_shapes=[pltpu.VMEM((m, n), F32), pltpu.VMEM((m, n), BF16)]
        + (_small_allreduce_scratch(whole, by_rows) if n_small else []),
        compiler_params=_params(),
    )(a, b, *small)
    return res[0], res[1:1 + n_w], res[1 + n_w:]


def _in_bwd(dproj, x, dh1, g_mix, w_in, scatter, tb):
    t_len, d = x.shape
    nb = t_len // tb
    n_s = len(scatter['blocks'])
    scatter_args = _scatter_args(scatter)

    def body(*refs):
        dproj_ref, x_ref, dh1_ref, g_ref, win_ref = refs[:5]
        scatter_in = refs[5:5 + len(scatter_args)]
        dx_ref, vec_ref = refs[5 + len(scatter_args):7 + len(scatter_args)]
        recv = refs[7 + len(scatter_args):7 + len(scatter_args) + n_s]
        vec_acc, send_sems, recv_sems, local_sems, vec_land, small_send, small_recv = refs[7 + len(scatter_args) + n_s:]
        start_scatter, wait_scatter = _scatter_ops(scatter, scatter_in, recv, (send_sems, recv_sems, local_sems))
        send_partials, reduce_and_send_sums, finish_small = _small_allreduce(
            [vec_acc], [], [vec_ref], [], [vec_land], [], [], small_send, small_recv)
        i = pl.program_id(0)

        @pl.when(i == 0)
        def _():
            start_scatter()
            vec_acc[...] = jnp.zeros_like(vec_acc)

        dz1 = _dot(dproj_ref[...], win_ref[...])
        g = g_ref[...]
        _, xh, rr = _rms_fwd(x_ref[...], g)
        dx_ref[...] = dh1_ref[...] + _rms_bwd(xh, rr, g, dz1)
        vec_acc[0:1, :] += _colsum(dz1 * xh)

        @pl.when(i == nb - 1)
        def _():
            send_partials()
            reduce_and_send_sums()
            finish_small()
            wait_scatter()

    row_spec = lambda w: pl.BlockSpec((tb, w), lambda i: (i, 0))
    any_spec = pl.BlockSpec(memory_space=pl.ANY)
    return pl.pallas_call(
        body, name="in_bwd", grid=(nb,),
        in_specs=[row_spec(dproj.shape[1]), row_spec(d), row_spec(d), _const_spec(g_mix.shape), _const_spec(w_in.shape)]
        + [any_spec] * len(scatter_args),
        out_specs=[row_spec(d), pl.BlockSpec((8, d), lambda i: (0, 0))] + [any_spec] * n_s,
        out_shape=[jax.ShapeDtypeStruct((t_len, d), F32), jax.ShapeDtypeStruct((8, d), F32)] + _scatter_out_shape(scatter),
        scratch_shapes=[pltpu.VMEM((8, d), F32)] + _exchange_scratch(n_s)
        + _small_allreduce_scratch([jax.ShapeDtypeStruct((8, d), F32)], []),
        input_output_aliases=_scatter_aliases(scatter, 5, 2), compiler_params=_params(),
    )(dproj, x, dh1, g_mix, w_in, *scatter_args)


def _small_allreduce(whole_in, rows_in, whole_out, rows_out, whole_land, rows_land, rows_sum, send_sems, recv_sems):
    n_w, n_r = len(whole_in), len(rows_in)
    per = [r.shape[0] // N_DEV for r in rows_in]
    me = _my_index()

    def dev(s):
        return (s // 4, (s // 2) % 2, s % 2)

    def rows_of(t, s):
        return pl.ds(s * per[t], per[t])

    def mine(t):
        return pl.ds(pl.multiple_of(me * per[t], 8), per[t])

    def partial(t, s, slot):
        if t < n_w:
            src, dst = whole_in[t], whole_land[t]
        else:
            src, dst = rows_in[t - n_w].at[rows_of(t - n_w, s)], rows_land[t - n_w]
        return pltpu.make_async_remote_copy(
            src_ref=src, dst_ref=dst.at[slot], send_sem=send_sems.at[t, s], recv_sem=recv_sems.at[t, slot],
            device_id=dev(s), device_id_type=MESH)

    def summed(t, s, rows, slot):
        return pltpu.make_async_remote_copy(
            src_ref=rows_sum[t].at[rows], dst_ref=rows_sum[t].at[rows], send_sem=send_sems.at[n_w + n_r + t, s],
            recv_sem=recv_sems.at[n_w + n_r + t, slot], device_id=dev(s), device_id_type=MESH)

    def send_partials():
        for s in range(N_DEV):
            @pl.when(s != me)
            def _():
                for t in range(n_w + n_r):
                    partial(t, s, me).start()
        for t in range(n_w):
            whole_land[t][me] = whole_in[t][...]
        for t in range(n_r):
            rows_land[t][me] = rows_in[t][mine(t), :]

    def reduce_and_send_sums():
        for s in range(N_DEV):
            @pl.when(s != me)
            def _():
                for t in range(n_w + n_r):
                    partial(t, s, s).wait_recv()
        for t in range(n_w):
            total = whole_land[t][0]
            for s in range(1, N_DEV):
                total = total + whole_land[t][s]
            whole_out[t][...] = total
        for t in range(n_r):
            total = rows_land[t][0]
            for s in range(1, N_DEV):
                total = total + rows_land[t][s]
            rows_sum[t][mine(t), :] = total
        for s in range(N_DEV):
            @pl.when(s != me)
            def _():
                for t in range(n_r):
                    summed(t, s, mine(t), me).start()

    def finish():
        for s in range(N_DEV):
            @pl.when(s != me)
            def _():
                for t in range(n_r):
                    summed(t, s, rows_of(t, s), s).wait_recv()
                    summed(t, s, mine(t), me).wait_send()
                for t in range(n_w + n_r):
                    partial(t, s, me).wait_send()
        for t in range(n_r):
            rows_out[t][...] = rows_sum[t][...]

    return send_partials, reduce_and_send_sums, finish


def _small_allreduce_scratch(whole, by_rows):
    n_sem = len(whole) + 2 * len(by_rows)
    return ([pltpu.VMEM((N_DEV,) + a.shape, F32) for a in whole]
            + [pltpu.VMEM((N_DEV, a.shape[0] // N_DEV, a.shape[1]), F32) for a in by_rows]
            + [pltpu.VMEM(a.shape, F32) for a in by_rows]
            + [pltpu.SemaphoreType.DMA((n_sem, N_DEV)), pltpu.SemaphoreType.DMA((n_sem, N_DEV))])


def _adam_update(g, w, m, v):
    m_new = ADAM_B1 * m + (1.0 - ADAM_B1) * g
    v_new = ADAM_B2 * v + (1.0 - ADAM_B2) * jnp.square(g)
    m_hat = m_new / (1.0 - ADAM_B1 ** ADAM_STEP)
    v_hat = v_new / (1.0 - ADAM_B2 ** ADAM_STEP)
    return -ADAM_LR * (m_hat / (jnp.sqrt(v_hat) + ADAM_EPS) + ADAM_WD * w), m_new, v_new


def _adamw_groups(name, groups):
    n = len(groups)

    def body(*refs):
        for k in range(n):
            g_ref, w_ref, m_ref, v_ref = refs[4 * k:4 * k + 4]
            g_out, d_out, m_out, v_out = refs[4 * n + 4 * k:4 * n + 4 * k + 4]
            if len(g_ref.shape) == 3:
                g = g_ref[0].astype(F32)
                for s in range(1, g_ref.shape[0]):
                    g = g + g_ref[s].astype(F32)
            else:
                g = g_ref[...]
            g_out[...] = g
            d_out[...], m_out[...], v_out[...] = _adam_update(g, w_ref[...], m_ref[...], v_ref[...])

    flat = [a for grp in groups for a in grp]
    out = pl.pallas_call(
        body, name=name, out_shape=[jax.ShapeDtypeStruct(grp[1].shape, F32) for grp in groups for _ in range(4)],
        compiler_params=pltpu.CompilerParams(vmem_limit_bytes=VMEM_LIMIT))(*flat)
    return [out[4 * k:4 * k + 4] for k in range(n)]


def _adamw(name, parts, w, m, v, row_block):
    n_src, rows, cols = parts.shape
    rb = min(row_block, rows)

    def body(p_ref, w_ref, m_ref, v_ref, g_out, d_out, m_out, v_out):
        g = p_ref[0].astype(F32)
        for s in range(1, n_src):
            g = g + p_ref[s].astype(F32)
        g_out[...] = g
        d_out[...], m_out[...], v_out[...] = _adam_update(g, w_ref[...], m_ref[...], v_ref[...])

    spec = pl.BlockSpec((rb, cols), lambda i: (i, 0))
    return pl.pallas_call(
        body, name=name, grid=(rows // rb,),
        in_specs=[pl.BlockSpec((n_src, rb, cols), lambda i: (0, i, 0)), spec, spec, spec],
        out_specs=[spec] * 4, out_shape=[jax.ShapeDtypeStruct((rows, cols), F32)] * 4,
        compiler_params=pltpu.CompilerParams(dimension_semantics=("parallel",), vmem_limit_bytes=VMEM_LIMIT),
    )(parts, w, m, v)


def _block_diag(blocks):
    g, a, b = blocks.shape
    eye = jnp.eye(g, dtype=blocks.dtype)
    return (eye[:, None, :, None] * blocks[:, :, None, :]).reshape(g * a, g * b)


def kernel(x, p, norm_mix_g, w_in, pool_w, pool_b, pool_scale, conv_w, conv_b, gate_a_w, gate_a_b, gate_x_w, gate_x_b, lru_L, w_out, norm_mlp_g, w_up, w_down, norm_ple_g, w_ple_gate, b_ple_gate, w_ple_proj, norm_final_g, loss_target, m_norm_mix_g, m_w_in, m_pool_w, m_pool_b, m_pool_scale, m_conv_w, m_conv_b, m_gate_a_w, m_gate_a_b, m_gate_x_w, m_gate_x_b, m_lru_L, m_w_out, m_norm_mlp_g, m_w_up, m_w_down, m_norm_ple_g, m_w_ple_gate, m_b_ple_gate, m_w_ple_proj, m_norm_final_g, v_norm_mix_g, v_w_in, v_pool_w, v_pool_b, v_pool_scale, v_conv_w, v_conv_b, v_gate_a_w, v_gate_a_b, v_gate_x_w, v_gate_x_b, v_lru_L, v_w_out, v_norm_mlp_g, v_w_up, v_w_down, v_norm_ple_g, v_w_ple_gate, v_b_ple_gate, v_w_ple_proj, v_norm_final_g):
    t_len, d = x.shape[1], x.shape[2]
    tbs = {k: min(v, t_len) for k, v in TIME_BLOCKS.items()}
    me = _my_index()

    win_g, wout_g, convw_g = _gather("gather_mixer_weights",
                                     [w_in[0].T.astype(BF16), w_out[0].astype(BF16), conv_w[0]])
    w_in_f = win_g.reshape(-1, d)
    conv_w_f = jnp.transpose(convw_g, (1, 0, 2)).reshape(convw_g.shape[1], -1)
    wp_bd = _block_diag(pool_w[0]).astype(BF16)
    wg_bd = jnp.concatenate([_block_diag(gate_a_w[0]), _block_diag(gate_x_w[0])], axis=1).astype(BF16)
    gate_b2 = jnp.concatenate([gate_a_b.reshape(1, -1), gate_x_b.reshape(1, -1)], axis=1)
    mixer_small = (norm_mix_g, w_in_f, wp_bd, pool_b.reshape(1, -1), pool_scale, conv_w_f, conv_b, wg_bd, gate_b2, lru_L,
                   wout_g.reshape(-1, d))

    x2 = x[0]
    later = [w_up[0].astype(BF16), w_down[0].astype(BF16), w_ple_gate[0].astype(BF16), w_ple_proj[0].astype(BF16)]
    h1, z1, proj, hs, cat, lru_saved, dpool_saved, wup_g, wdn_g, wgate_g, wproj_g = _mix_fwd(
        x2, *mixer_small, later, [_core_major_slot, _core_major_slot, None, None], GATHER_FORWARD_AT, tbs['mix_fwd'])
    w_down_f = wdn_g.reshape(-1, d)
    w_proj_f = jnp.transpose(wproj_g, (1, 0, 2)).reshape(wproj_g.shape[1], -1)
    h2, z2, up = _mlp_fwd(h1, norm_mlp_g, wup_g, w_down_f, tbs['mlp_fwd'])
    dh2, ple_vec, dw_gate, dw_proj = _ple(h2, p[0, 0], loss_target[0], norm_ple_g, wgate_g.reshape(-1, d), b_ple_gate,
                                          w_proj_f, norm_final_g.reshape(1, -1), tbs['ple'])
    everyone = list(range(N_DEV))
    n_proj = w_ple_proj.shape[2]
    dz2_0, dw_up_0, dw_down_0 = _mlp_bwd_part(
        0, MLP_BWD_SPLIT, dh2, z2, up, wup_g, w_down_f, None, h1, norm_mlp_g, _scatter_plan([], [], []), tbs['mlp_bwd'])
    half = N_DEV // MLP_BWD_SPLIT
    south = [_device_of_core_major_slot(k) for k in range(half)]
    north = [_device_of_core_major_slot(k) for k in range(half, N_DEV)]
    scatter = _scatter_plan(
        [dw_up_0, dw_down_0, dw_gate.reshape(N_DEV, -1, d), jnp.transpose(dw_proj.reshape(-1, N_DEV, n_proj), (1, 0, 2))],
        [south, south, everyone, everyone], [None, None, None, None])
    dh1, mlp_vec, dw_up_1, dw_down_1, recv_up, recv_down, recv_gate, recv_proj = _mlp_bwd_part(
        1, MLP_BWD_SPLIT, dh2, z2, up, wup_g, w_down_f, dz2_0, h1, norm_mlp_g, scatter, tbs['mlp_bwd'])
    dw_out, _, _ = _wgrad("wgrad_out", cat, dh1, [], [], tbs['wgrad_out'])
    scatter = _scatter_plan([dw_up_1, dw_down_1, dw_out.reshape(N_DEV, -1, d)], [north, north, everyone],
                            [recv_up, recv_down, None])
    dproj, v512, dpw, dga, dgx, recv_up, recv_down, recv_out = _mix_bwd(
        dh1, proj, hs, lru_saved, dpool_saved, wp_bd, pool_b.reshape(1, -1), pool_scale, conv_w_f, wg_bd, lru_L, wout_g.reshape(-1, d),
        scatter, tbs['mix_bwd'])
    rows1024 = jnp.concatenate([jnp.zeros((1, d), F32), mlp_vec[0:1], ple_vec[1:2], ple_vec[0:1], ple_vec[2:4],
                                jnp.zeros((2, d), F32)], axis=0)
    dw_in_t, (rows1024, rows512), (g_pool_w, g_gate_a_w, g_gate_x_w) = _wgrad(
        "wgrad_in", dproj, z1, [rows1024, v512], [dpw, dga, dgx], tbs['wgrad_in'])
    scatter = _scatter_plan([dw_in_t.reshape(N_DEV, -1, d)], [everyone], [None])
    dx, in_vec, recv_in = _in_bwd(dproj, x2, dh1, norm_mix_g, w_in_f, scatter, tbs['in_bwd'])
    rows1024 = jnp.concatenate([in_vec[0:1], rows1024[1:]], axis=0)
    received = [recv_in, recv_out, recv_up, recv_down, recv_gate, recv_proj]

    shard_w = [w_in[0].T, w_out[0], w_up[0], w_down[0], w_ple_gate[0], w_ple_proj[0]]
    shard_m = [m_w_in[0].T, m_w_out[0], m_w_up[0], m_w_down[0], m_w_ple_gate[0], m_w_ple_proj[0]]
    shard_v = [v_w_in[0].T, v_w_out[0], v_w_up[0], v_w_down[0], v_w_ple_gate[0], v_w_ple_proj[0]]
    names = ["w_in", "w_out", "w_up", "w_down", "w_ple_gate", "w_ple_proj"]
    res = {}
    gridded = ("w_up", "w_down")
    for nm, parts, w_s, m_s, v_s in zip(names, received, shard_w, shard_m, shard_v):
        if nm in gridded:
            res[nm] = [r[None] for r in _adamw("adamw_" + nm, parts, w_s, m_s, v_s, ADAM_ROW_BLOCK)]
    rest = [k for k, nm in enumerate(names) if nm not in gridded]
    for k, out in zip(rest, _adamw_groups("adamw_medium", [(received[k], shard_w[k], shard_m[k], shard_v[k]) for k in rest])):
        res[names[k]] = [r[None] for r in out]
    res["w_in"] = [jnp.swapaxes(r, 1, 2) for r in res["w_in"]]

    def rows_of_1024(a, b, c, e, f):
        return jnp.concatenate([a, b, c, e, f.reshape(1, -1), jnp.zeros((3, d), F32)], axis=0)

    def rows_of_512(scale, bias, cb, lru, ga, gx):
        z = jnp.zeros((1, 512), F32)
        return jnp.concatenate([scale, bias.reshape(1, -1), cb, lru, z, z, z, z, ga.reshape(1, -1), gx.reshape(1, -1),
                                z, z, z, z, z, z], axis=0)

    n_conv = conv_w.shape[2]
    groups = [
        (rows1024, *[rows_of_1024(*t) for t in (
            (norm_mix_g, norm_mlp_g, norm_ple_g, b_ple_gate, norm_final_g),
            (m_norm_mix_g, m_norm_mlp_g, m_norm_ple_g, m_b_ple_gate, m_norm_final_g),
            (v_norm_mix_g, v_norm_mlp_g, v_norm_ple_g, v_b_ple_gate, v_norm_final_g))]),
        (rows512, *[rows_of_512(*t) for t in (
            (pool_scale, pool_b, conv_b, lru_L, gate_a_b, gate_x_b),
            (m_pool_scale, m_pool_b, m_conv_b, m_lru_L, m_gate_a_b, m_gate_x_b),
            (v_pool_scale, v_pool_b, v_conv_b, v_lru_L, v_gate_a_b, v_gate_x_b))]),
        (g_pool_w, *[a.reshape(-1, a.shape[-1]) for a in (pool_w, m_pool_w, v_pool_w)]),
        (g_gate_a_w, *[a.reshape(-1, a.shape[-1]) for a in (gate_a_w, m_gate_a_w, v_gate_a_w)]),
        (g_gate_x_w, *[a.reshape(-1, a.shape[-1]) for a in (gate_x_w, m_gate_x_w, v_gate_x_w)]),
        (lax.dynamic_slice_in_dim(rows512[4:8], me * n_conv, n_conv, axis=1), conv_w[0], m_conv_w[0], v_conv_w[0]),
    ]
    r1024, r512, r_pool, r_ga, r_gx, r_conv = _adamw_groups("adamw_small", groups)
    loss = rows1024[5, 0]
    for k, nm in enumerate(["norm_mix_g", "norm_mlp_g", "norm_ple_g", "b_ple_gate"]):
        res[nm] = [a[k:k + 1] for a in r1024]
    res["norm_final_g"] = [a[4] for a in r1024]
    res["pool_scale"] = [a[0:1] for a in r512]
    res["pool_b"] = [a[1:2].reshape(pool_b.shape) for a in r512]
    res["conv_b"] = [a[2:3] for a in r512]
    res["lru_L"] = [a[3:4] for a in r512]
    res["gate_a_b"] = [a[8:9].reshape(gate_a_b.shape) for a in r512]
    res["gate_x_b"] = [a[9:10].reshape(gate_x_b.shape) for a in r512]
    res["pool_w"] = [a.reshape(pool_w.shape) for a in r_pool]
    res["gate_a_w"] = [a.reshape(gate_a_w.shape) for a in r_ga]
    res["gate_x_w"] = [a.reshape(gate_x_w.shape) for a in r_gx]
    res["conv_w"] = [a[None] for a in r_conv]
    order = ["norm_mix_g", "w_in", "pool_w", "pool_b", "pool_scale", "conv_w", "conv_b", "gate_a_w", "gate_a_b",
             "gate_x_w", "gate_x_b", "lru_L", "w_out", "norm_mlp_g", "w_up", "w_down", "norm_ple_g", "w_ple_gate",
             "b_ple_gate", "w_ple_proj", "norm_final_g"]
    return (loss, dx[None], *[res[nm][kind] for kind in range(4) for nm in order])
```

```python
import jax
import jax.numpy as jnp
from jax import lax
from jax.experimental import pallas as pl
from jax.experimental.pallas import tpu as pltpu

F32 = jnp.float32
BF16 = jnp.bfloat16
MESH = pl.DeviceIdType.MESH

N_DEV = 8
RMS_EPS = 1e-6
LRU_C = 8.0
POOL_WINDOWS = (2, 4, 8, 16)
N_POOL_GROUPS = 4
LRU_HEADS = 8
HALO = 16
SUB = 8
GELU_C0 = 0.7978845608028654
GELU_C1 = 0.044715

ADAM_LR = 0.001
ADAM_B1 = 0.9
ADAM_B2 = 0.999
ADAM_EPS = 1e-08
ADAM_WD = 0.01
ADAM_STEP = 10

VMEM_LIMIT = 60 * 1024 * 1024
TIME_BLOCKS = dict(mix_fwd=512, mlp_fwd=512, ple=512, mlp_bwd=512, wgrad_out=1024, mix_bwd=512, wgrad_in=1024, in_bwd=512)
ADAM_ROW_BLOCK = 256
SCAN_UNROLL = 4
MLP_BWD_SPLIT = 2
GATHER_FORWARD_AT = (0.5, 0.875, 1.0, 1.0)


def _params(n_arbitrary=1):
    return pltpu.CompilerParams(dimension_semantics=("arbitrary",) * n_arbitrary, vmem_limit_bytes=VMEM_LIMIT)


def _dot(a, b):
    return jnp.dot(a, b, preferred_element_type=F32)


def _dot_nt(a, b):
    return lax.dot_general(a, b, (((1,), (1,)), ((), ())), preferred_element_type=F32)


def _dot_tn(a, b):
    return lax.dot_general(a, b, (((0,), (0,)), ((), ())), preferred_element_type=F32)


def _rms_fwd(x, g):
    r = lax.rsqrt(jnp.mean(x * x, axis=-1, keepdims=True) + RMS_EPS)
    xh = x * r
    return xh * g, xh, r


def _rms_bwd(xh, r, g, dz):
    dxh = dz * g
    return r * (dxh - xh * jnp.mean(dxh * xh, axis=-1, keepdims=True))


def _colsum(a):
    return jnp.sum(a, axis=0, keepdims=True)


def _sigmoid(a):
    return 0.5 * jnp.tanh(0.5 * a) + 0.5


def _gelu_parts(u):
    u2 = u * u
    th = jnp.tanh(GELU_C0 * (u + GELU_C1 * u * u2))
    gel = 0.5 * u * (1.0 + th)
    dgel = 0.5 * (1.0 + th) + 0.5 * u * (1.0 - th * th) * (GELU_C0 * (1.0 + 3.0 * GELU_C1 * u2))
    return gel, dgel


def _my_index():
    return 4 * lax.axis_index("x") + 2 * lax.axis_index("y") + lax.axis_index("c")


def _all_to_all(srcs_of, dsts, send_sems, recv_sems, local_sems, dests=None):
    n = len(dsts)
    me = _my_index()
    dests = [list(range(N_DEV))] * n if dests is None else dests

    def remote(t, s):
        return pltpu.make_async_remote_copy(
            src_ref=srcs_of[t](s), dst_ref=dsts[t].at[me], send_sem=send_sems.at[t, s], recv_sem=recv_sems.at[t, me],
            device_id=(s // 4, (s // 2) % 2, s % 2), device_id_type=MESH)

    def arrival(t, s):
        return pltpu.make_async_remote_copy(
            src_ref=srcs_of[t](dests[t][0]), dst_ref=dsts[t].at[s], send_sem=send_sems.at[t, s],
            recv_sem=recv_sems.at[t, s], device_id=(s // 4, (s // 2) % 2, s % 2), device_id_type=MESH)

    def local(t, s):
        return pltpu.make_async_copy(srcs_of[t](s), dsts[t].at[s], local_sems.at[t])

    def start():
        for s in range(N_DEV):
            to_s = [t for t in range(n) if s in dests[t]]

            @pl.when(s == me)
            def _():
                for t in to_s:
                    local(t, s).start()

            @pl.when(s != me)
            def _():
                for t in to_s:
                    remote(t, s).start()

    def wait():
        for s in range(N_DEV):
            to_s = [t for t in range(n) if s in dests[t]]

            @pl.when(s == me)
            def _():
                for t in to_s:
                    local(t, s).wait()
                    for src in range(N_DEV):
                        if src != s:
                            arrival(t, src).wait_recv()

            @pl.when(s != me)
            def _():
                for t in to_s:
                    remote(t, s).wait_send()

    return start, wait


N_GATHER_COPIES = 7


def _core_major_slot(dev):
    return 4 * dev[2] + 2 * dev[0] + dev[1]


def _device_of_core_major_slot(k):
    return (k % 4) * 2 + k // 4


def _two_level_gather(srcs, dsts, send_sems, recv_sems, local_sems, slots=None):
    n = len(dsts)
    x, y, c = lax.axis_index("x"), lax.axis_index("y"), lax.axis_index("c")
    me, sibling = (x, y, c), (x, y, 1 - c)
    chips = [(1 - x, y), (x, 1 - y), (1 - x, 1 - y)]

    def slot(t, dev):
        return 4 * dev[0] + 2 * dev[1] + dev[2] if slots is None or slots[t] is None else slots[t](dev)

    def copy(t, k, block, to, src=None):
        return pltpu.make_async_remote_copy(
            src_ref=dsts[t].at[slot(t, block)] if src is None else src, dst_ref=dsts[t].at[slot(t, block)],
            send_sem=send_sems.at[t, k], recv_sem=recv_sems.at[t, k], device_id=to, device_id_type=MESH)

    def local(t):
        return pltpu.make_async_copy(srcs[t], dsts[t].at[slot(t, me)], local_sems.at[t])

    def start():
        for t in range(n):
            local(t).start()
            for j, chip in enumerate(chips):
                copy(t, 1 + j, me, (*chip, c), src=srcs[t]).start()
            copy(t, 0, me, sibling, src=srcs[t]).start()

    def forward(t):
        for j, chip in enumerate(chips):
            copy(t, 1 + j, (*chip, c), me).wait_recv()
            copy(t, 4 + j, (*chip, c), sibling).start()

    def finish():
        for t in range(n):
            copy(t, 0, sibling, me).wait_recv()
            for j, chip in enumerate(chips):
                copy(t, 4 + j, (*chip, 1 - c), me).wait_recv()
            copy(t, 0, me, sibling, src=srcs[t]).wait_send()
            for j, chip in enumerate(chips):
                copy(t, 1 + j, me, (*chip, c), src=srcs[t]).wait_send()
                copy(t, 4 + j, (*chip, c), sibling).wait_send()
            local(t).wait()

    return start, forward, finish


def _hosted_gather(i, nb, forward_at, srcs, dsts, sems, slots=None):
    start, forward, finish = _two_level_gather(srcs, dsts, *sems, slots)

    def after_step():
        for t, f in enumerate(forward_at):
            @pl.when(i == min(nb - 1, int(f * nb)))
            def _():
                forward(t)

        @pl.when(i == nb - 1)
        def _():
            finish()

    return start, after_step


def _gather_scratch(n):
    return [pltpu.SemaphoreType.DMA((n, N_GATHER_COPIES)), pltpu.SemaphoreType.DMA((n, N_GATHER_COPIES)),
            pltpu.SemaphoreType.DMA((n,))]


def _gather(name, srcs):
    n = len(srcs)

    def body(*refs):
        start, forward, finish = _two_level_gather(refs[:n], refs[n:2 * n], *refs[2 * n:])
        start()
        for t in range(n):
            forward(t)
        finish()

    any_spec = pl.BlockSpec(memory_space=pl.ANY)
    return pl.pallas_call(
        body, name=name, in_specs=[any_spec] * n, out_specs=[any_spec] * n,
        out_shape=[jax.ShapeDtypeStruct((N_DEV,) + a.shape, a.dtype) for a in srcs], scratch_shapes=_gather_scratch(n),
    )(*srcs)


def _scatter_plan(blocks, dests, landing):
    return dict(blocks=list(blocks), dests=[list(dd) for dd in dests], landing=list(landing))


def _scatter_args(plan):
    return plan['blocks'] + [a for a in plan['landing'] if a is not None]


def _scatter_out_shape(plan):
    return [jax.ShapeDtypeStruct((N_DEV,) + b.shape[1:], b.dtype) for b in plan['blocks']]


def _scatter_aliases(plan, first_in, first_out):
    given = [t for t, a in enumerate(plan['landing']) if a is not None]
    return {first_in + len(plan['blocks']) + k: first_out + t for k, t in enumerate(given)}


def _scatter_ops(plan, in_refs, out_refs, sems):
    n = len(plan['blocks'])
    srcs_of = [(lambda s, r=in_refs[t], dd=plan['dests'][t]: r.at[dd.index(s)]) for t in range(n)]
    return _all_to_all(srcs_of, out_refs, *sems, dests=plan['dests'])


def _exchange_scratch(n):
    return [pltpu.SemaphoreType.DMA((n, N_DEV)), pltpu.SemaphoreType.DMA((n, N_DEV)), pltpu.SemaphoreType.DMA((n,))]


def _const_spec(shape):
    nd = len(shape)
    return pl.BlockSpec(shape, lambda i: (0,) * nd, pipeline_mode=pl.Buffered(1))


def _pool_windows(up_ext, n, forward):
    sh = (lambda k: k) if forward else (lambda k: n - k)
    s2 = up_ext + pltpu.roll(up_ext, sh(1), 0)
    t4 = s2[:, 128:]
    s4 = t4 + pltpu.roll(t4, sh(2), 0)
    t8 = s4[:, 128:]
    s8 = t8 + pltpu.roll(t8, sh(4), 0)
    t16 = s8[:, 128:]
    s16 = t16 + pltpu.roll(t16, sh(8), 0)
    return jnp.concatenate([s2[:, :128], s4[:, :128], s8[:, :128], s16], axis=1)


def _inv_count_head():
    t = jnp.arange(1, HALO + 1, dtype=F32)[:, None]
    return jnp.concatenate([jnp.broadcast_to(1.0 / jnp.minimum(t, float(w)), (HALO, 128)) for w in POOL_WINDOWS], axis=1)


def _scale_by_inv_count(v, is_first_block, inv_head):
    inv_row = jnp.concatenate([jnp.full((1, 128), 1.0 / w, F32) for w in POOL_WINDOWS], axis=1)
    head = v[0:HALO] * jnp.where(is_first_block, inv_head, inv_row)
    return jnp.concatenate([head, v[HALO:] * inv_row], axis=0)


def _lru_decay(r, a, c_l, first_row):
    a2 = a * a
    m2 = -jnp.tanh(c_l * r) * (a2 + 1.0)
    return a2, m2, jnp.where(first_row, 1.0, jnp.sqrt(m2))


def _log_sigmoid(v):
    return -(jnp.maximum(-v, 0.0) + jnp.log1p(jnp.exp(-jnp.abs(v))))


def _conv_fwd(ul_ext, cw, cb):
    return (cb + cw[3:4, :] * ul_ext + cw[2:3, :] * pltpu.roll(ul_ext, 1, 0)
            + cw[1:2, :] * pltpu.roll(ul_ext, 2, 0) + cw[0:1, :] * pltpu.roll(ul_ext, 3, 0))


def _mix_fwd(x, g_mix, w_in, wp_bd, pool_b, pool_scale, conv_w, conv_b, wg_bd, gate_b, lru_l, w_out, gather_srcs,
             gather_slots, forward_at, tb):
    t_len, d = x.shape
    nb = t_len // tb
    n_g = len(gather_srcs)

    def body(*refs):
        (x_ref, g_ref, win_ref, wp_ref, pb_ref, ps_ref, cw_ref, cb_ref, wg_ref, gb_ref, l_ref, wout_ref,
         invh_ref) = refs[:13]
        gsrc = refs[13:13 + n_g]
        h1_ref, z1_ref, proj_ref, hs_ref, cat_ref, lru_ref, dpool_ref = refs[13 + n_g:20 + n_g]
        gdst = refs[20 + n_g:20 + 2 * n_g]
        ext_ref, a_ref, b_ref, hc_ref, send_sems, recv_sems, local_sems = refs[20 + 2 * n_g:]
        i = pl.program_id(0)
        start_gather, after_step = _hosted_gather(i, nb, forward_at, gsrc, gdst, (send_sems, recv_sems, local_sems),
                                                  gather_slots)

        @pl.when(i == 0)
        def _():
            start_gather()
            ext_ref[0:HALO, :] = jnp.zeros((HALO, 1024), F32)
            hc_ref[...] = jnp.zeros_like(hc_ref)

        xv = x_ref[...]
        z, _, _ = _rms_fwd(xv, g_ref[...])
        zb = z.astype(BF16)
        z1_ref[...] = zb
        proj = _dot_nt(zb, win_ref[...])
        proj_ref[...] = proj
        ext_ref[HALO:, :] = proj[:, 0:1024]
        ug = proj[:, 1024:1536]
        n = tb + HALO
        up_ext = ext_ref[:, 0:512]
        win = _pool_windows(up_ext, n, True)[HALO:]
        dpool = _scale_by_inv_count(win, i == 0, invh_ref[...]) - proj[:, 0:512]
        dpoolb = dpool.astype(BF16)
        dpool_ref[...] = dpoolb
        q = _dot(dpoolb, wp_ref[...]) + pb_ref[...]
        y_pool = q * ps_ref[...]
        xb = _conv_fwd(ext_ref[:, 512:1024], cw_ref[...], cb_ref[...])[HALO:]
        first_row = (i * tb + lax.broadcasted_iota(jnp.int32, (tb, 1), 0)) == 0
        c_l = LRU_C * _log_sigmoid(l_ref[...])
        gp = _dot(xb.astype(BF16), wg_ref[...]) + gb_ref[...]
        r = 1.0 / (1.0 + jnp.exp(-gp[:, :512]))
        ig = _sigmoid(gp[:, 512:])
        a = jnp.exp(c_l * r)
        _, _, mult = _lru_decay(r, a, c_l, first_row)
        lru_ref[:, 0:512] = xb
        lru_ref[:, 512:1024] = r
        lru_ref[:, 1024:1536] = ig
        lru_ref[:, 1536:2048] = a
        a_ref[...] = a
        b_ref[...] = mult * (ig * xb)
        row = lax.broadcasted_iota(jnp.int32, (SUB, 512), 0)

        def group(j, hprev):
            o = pl.multiple_of(j * SUB, SUB)
            a8 = a_ref[pl.ds(o, SUB), :]
            b8 = b_ref[pl.ds(o, SUB), :]
            for sh in (1, 2, 4):
                ash = jnp.where(row >= sh, pltpu.roll(a8, sh, 0), 1.0)
                bsh = jnp.where(row >= sh, pltpu.roll(b8, sh, 0), 0.0)
                b8 = a8 * bsh + b8
                a8 = a8 * ash
            h8 = a8 * hprev + b8
            hs_ref[pl.ds(o, SUB), :] = h8
            return jnp.broadcast_to(h8[SUB - 1:SUB, :], (SUB, 512))

        def trip(k, carry):
            for u in range(SCAN_UNROLL):
                carry = group(k * SCAN_UNROLL + u, carry)
            return carry

        hc_ref[...] = lax.fori_loop(0, tb // (SUB * SCAN_UNROLL), trip, hc_ref[...])
        gel, dgel = _gelu_parts(ug)
        lru_ref[:, 2048:2560] = gel
        lru_ref[:, 2560:3072] = dgel
        y_lru = hs_ref[...] * gel
        catb = jnp.concatenate([y_pool, y_lru], axis=1).astype(BF16)
        cat_ref[...] = catb
        h1_ref[...] = xv + _dot(catb, wout_ref[...])
        ext_ref[0:HALO, :] = ext_ref[tb:tb + HALO, :]

        after_step()

    row_spec = lambda w: pl.BlockSpec((tb, w), lambda i: (i, 0))
    any_spec = pl.BlockSpec(memory_space=pl.ANY)
    smalls = [g_mix, w_in, wp_bd, pool_b, pool_scale, conv_w, conv_b, wg_bd, gate_b, lru_l, w_out, _inv_count_head()]
    return pl.pallas_call(
        body, name="mix_fwd", grid=(nb,),
        in_specs=[row_spec(d)] + [_const_spec(s.shape) for s in smalls] + [any_spec] * n_g,
        out_specs=[row_spec(d), row_spec(d), row_spec(1536), row_spec(512), row_spec(1024), row_spec(3072), row_spec(512)]
        + [any_spec] * n_g,
        out_shape=[jax.ShapeDtypeStruct((t_len, d), F32), jax.ShapeDtypeStruct((t_len, d), BF16),
                   jax.ShapeDtypeStruct((t_len, 1536), F32), jax.ShapeDtypeStruct((t_len, 512), F32),
                   jax.ShapeDtypeStruct((t_len, 1024), BF16), jax.ShapeDtypeStruct((t_len, 3072), F32),
                   jax.ShapeDtypeStruct((t_len, 512), BF16)]
        + [jax.ShapeDtypeStruct((N_DEV,) + s.shape, s.dtype) for s in gather_srcs],
        scratch_shapes=[pltpu.VMEM((tb + HALO, 1024), F32), pltpu.VMEM((tb, 512), F32), pltpu.VMEM((tb, 512), F32),
                        pltpu.VMEM((SUB, 512), F32)] + _gather_scratch(n_g),
        compiler_params=_params(),
    )(x, *smalls, *gather_srcs)


def _mlp_fwd(h1, g_mlp, w_up, w_down, tb):
    t_len, d = h1.shape
    nb = t_len // tb
    n_chunk, _, fc = w_up.shape

    def body(h1_ref, g_ref, wup_ref, wdn_ref, h2_ref, z2_ref, up_ref):
        xv = h1_ref[...]
        z, _, _ = _rms_fwd(xv, g_ref[...])
        zb = z.astype(BF16)
        z2_ref[...] = zb
        acc = xv
        for c in range(n_chunk):
            u = _dot(zb, wup_ref[c])
            up_ref[:, c * fc:(c + 1) * fc] = u.astype(BF16)
            act = jnp.square(jnp.maximum(u, 0.0)).astype(BF16)
            acc = acc + _dot(act, wdn_ref[c * fc:(c + 1) * fc, :])
        h2_ref[...] = acc

    row_spec = lambda w: pl.BlockSpec((tb, w), lambda i: (i, 0))
    return pl.pallas_call(
        body, name="mlp_fwd", grid=(nb,),
        in_specs=[row_spec(d), _const_spec(g_mlp.shape), _const_spec(w_up.shape), _const_spec(w_down.shape)],
        out_specs=[row_spec(d), row_spec(d), row_spec(n_chunk * fc)],
        out_shape=[jax.ShapeDtypeStruct((t_len, d), F32), jax.ShapeDtypeStruct((t_len, d), BF16),
                   jax.ShapeDtypeStruct((t_len, n_chunk * fc), BF16)],
        compiler_params=_params(),
    )(h1, g_mlp, w_up, w_down)


def _ple(h2, p, target, g_ple, w_gate, b_gate, w_proj, g_final, tb):
    t_len, d = h2.shape
    nb = t_len // tb
    pd = p.shape[1]

    def body(h2_ref, p_ref, tgt_ref, g_ref, wg_ref, bg_ref, wp_ref, gf_ref,
             dh2_ref, vec_ref, dwg_out, dwp_out, dwg_acc, dwp_acc, dwg_stage, dwp_stage):
        i = pl.program_id(0)

        @pl.when(i == 0)
        def _():
            vec_ref[...] = jnp.zeros_like(vec_ref)
            dwg_acc[...] = jnp.zeros_like(dwg_acc)
            dwp_acc[...] = jnp.zeros_like(dwp_acc)

        h2 = h2_ref[...]
        g2 = g_ref[...]
        z3, xh2, r2 = _rms_fwd(h2, g2)
        z3b = z3.astype(BF16)
        gate = _sigmoid(_dot(z3b, wg_ref[...]) + bg_ref[...])
        pb = p_ref[...].astype(BF16)
        pp = _dot(pb, wp_ref[...])
        h3 = h2 + gate * pp
        gf = gf_ref[...]
        y, xh3, r3 = _rms_fwd(h3, gf)
        err = y - tgt_ref[...]
        loss_rows = jnp.mean(err * err, axis=-1, keepdims=True)
        dy = err * (1.0 / d)
        dh3 = _rms_bwd(xh3, r3, gf, dy)
        dgl = (dh3 * pp) * (gate * (1.0 - gate))
        dpp = dh3 * gate
        dglb = dgl.astype(BF16)
        dwg_acc[...] += _dot_tn(z3b, dglb)
        dwp_acc[...] += _dot_tn(pb, dpp.astype(BF16))
        dz3 = _dot_nt(dglb, wg_ref[...])
        dh2_ref[...] = dh3 + _rms_bwd(xh2, r2, g2, dz3)
        vec_ref[0:1, :] += _colsum(dgl)
        vec_ref[1:2, :] += _colsum(dz3 * xh2)
        vec_ref[2:3, :] += _colsum(dy * xh3)
        vec_ref[3:4, :] += 0.5 * jnp.sum(loss_rows)

        @pl.when(i == nb - 1)
        def _():
            dwg_stage[...] = dwg_acc[...].astype(BF16)
            dwp_stage[...] = dwp_acc[...].astype(BF16)
            pltpu.sync_copy(dwg_stage, dwg_out)
            pltpu.sync_copy(dwp_stage, dwp_out)

    row_spec = lambda w: pl.BlockSpec((tb, w), lambda i: (i, 0))
    any_spec = pl.BlockSpec(memory_space=pl.ANY)
    smalls = [g_ple, w_gate, b_gate, w_proj, g_final]
    return pl.pallas_call(
        body, name="ple_fwd_bwd", grid=(nb,),
        in_specs=[row_spec(d), row_spec(pd), row_spec(d)] + [_const_spec(s.shape) for s in smalls],
        out_specs=[row_spec(d), pl.BlockSpec((8, d), lambda i: (0, 0)), any_spec, any_spec],
        out_shape=[jax.ShapeDtypeStruct((t_len, d), F32), jax.ShapeDtypeStruct((8, d), F32),
                   jax.ShapeDtypeStruct(w_gate.shape, BF16), jax.ShapeDtypeStruct(w_proj.shape, BF16)],
        scratch_shapes=[pltpu.VMEM(w_gate.shape, F32), pltpu.VMEM(w_proj.shape, F32), pltpu.VMEM(w_gate.shape, BF16),
                        pltpu.VMEM(w_proj.shape, BF16)],
        compiler_params=_params(),
    )(h2, p, target, *smalls)


def _mlp_bwd_part(part, n_part, dh2, z2, up, w_up, w_down, dz2_prev, h1, g_mlp, scatter, tb):
    t_len, d = dh2.shape
    nb = t_len // tb
    n_chunk_all, _, fc = w_up.shape
    n_chunk = n_chunk_all // n_part
    first, last = part == 0, part == n_part - 1

    def body(*refs):
        refs = list(refs)
        dh2_ref, z2_ref, up_ref, wup_ref, wdn_ref = refs[:5]
        del refs[:5]
        dzp_ref = None if first else refs.pop(0)
        h1_ref, g_ref = (refs.pop(0), refs.pop(0)) if last else (None, None)
        scatter_in = [refs.pop(0) for _ in _scatter_args(scatter)]
        out_ref = refs.pop(0)
        vec_ref = refs.pop(0) if last else None
        dwup_out, dwdn_out = refs.pop(0), refs.pop(0)
        scatter_out = [refs.pop(0) for _ in scatter['blocks']]
        dwup_acc, dwdn_acc, up_stage, dn_stage = refs[:4]
        if scatter['blocks']:
            start_scatter, wait_scatter = _scatter_ops(scatter, scatter_in, scatter_out, refs[4:])
        i = pl.program_id(0)

        @pl.when(i == 0)
        def _():
            if scatter['blocks']:
                start_scatter()
            dwup_acc[...] = jnp.zeros_like(dwup_acc)
            dwdn_acc[...] = jnp.zeros_like(dwdn_acc)
            if last:
                vec_ref[...] = jnp.zeros_like(vec_ref)

        dh2 = dh2_ref[...]
        dh2b = dh2.astype(BF16)
        z2b = z2_ref[...]
        dz2 = jnp.zeros((tb, d), F32) if first else dzp_ref[...]
        for c in range(n_chunk):
            u = up_ref[:, c * fc:(c + 1) * fc].astype(F32)
            ur = jnp.maximum(u, 0.0)
            dact = _dot_nt(dh2b, wdn_ref[c * fc:(c + 1) * fc, :])
            dupb = (dact * (2.0 * ur)).astype(BF16)
            dwdn_acc[c * fc:(c + 1) * fc, :] += _dot_tn((ur * ur).astype(BF16), dh2b)
            dwup_acc[c] += _dot_tn(z2b, dupb)
            dz2 = dz2 + _dot_nt(dupb, wup_ref[c])
        if last:
            g = g_ref[...]
            _, xh, r = _rms_fwd(h1_ref[...], g)
            out_ref[...] = dh2 + _rms_bwd(xh, r, g, dz2)
            vec_ref[0:1, :] += _colsum(dz2 * xh)
        else:
            out_ref[...] = dz2

        @pl.when(i == nb - 1)
        def _():
            for c in range(n_chunk):
                up_stage[...] = dwup_acc[c].astype(BF16)
                dn_stage[...] = dwdn_acc[c * fc:(c + 1) * fc, :].astype(BF16)
                pltpu.sync_copy(up_stage, dwup_out.at[c])
                pltpu.sync_copy(dn_stage, dwdn_out.at[c])
            if scatter['blocks']:
                wait_scatter()

    row_spec = lambda w: pl.BlockSpec((tb, w), lambda i: (i, 0))
    any_spec = pl.BlockSpec(memory_space=pl.ANY)
    args = [dh2, z2, up, w_up, w_down]
    in_specs = [row_spec(d), row_spec(d), pl.BlockSpec((tb, n_chunk * fc), lambda i: (i, part)),
                pl.BlockSpec((n_chunk, d, fc), lambda i: (part, 0, 0), pipeline_mode=pl.Buffered(1)),
                pl.BlockSpec((n_chunk * fc, d), lambda i: (part, 0), pipeline_mode=pl.Buffered(1))]
    if not first:
        args.append(dz2_prev)
        in_specs.append(row_spec(d))
    if last:
        args += [h1, g_mlp]
        in_specs += [row_spec(d), _const_spec(g_mlp.shape)]
    n_in = len(args)
    args += _scatter_args(scatter)
    in_specs += [any_spec] * len(_scatter_args(scatter))
    out_specs = [row_spec(d)]
    out_shape = [jax.ShapeDtypeStruct((t_len, d), F32)]
    if last:
        out_specs.append(pl.BlockSpec((8, d), lambda i: (0, 0)))
        out_shape.append(jax.ShapeDtypeStruct((8, d), F32))
    out_specs += [any_spec, any_spec]
    out_shape += [jax.ShapeDtypeStruct((n_chunk, d, fc), BF16), jax.ShapeDtypeStruct((n_chunk, fc, d), BF16)]
    n_out = len(out_shape)
    out_specs += [any_spec] * len(scatter['blocks'])
    out_shape += _scatter_out_shape(scatter)
    return pl.pallas_call(
        body, name=f"mlp_bwd_{part}", grid=(nb,), in_specs=in_specs, out_specs=out_specs, out_shape=out_shape,
        scratch_shapes=[pltpu.VMEM((n_chunk, d, fc), F32), pltpu.VMEM((n_chunk * fc, d), F32),
                        pltpu.VMEM((d, fc), BF16), pltpu.VMEM((fc, d), BF16)]
        + (_exchange_scratch(len(scatter['blocks'])) if scatter['blocks'] else []),
        input_output_aliases=_scatter_aliases(scatter, n_in, n_out), compiler_params=_params(),
    )(*args)


def _mix_bwd(dh1, proj, hs, lru_saved, dpool_saved, wp_bd, pool_b, pool_scale, conv_w, wg_bd, lru_l, w_out, scatter, tb):
    t_len, d = dh1.shape
    nb = t_len // tb
    n_s = len(scatter['blocks'])
    scatter_args = _scatter_args(scatter)

    def body(*refs):
        refs = list(refs)
        (dh1_ref, ul_ref, hs_ref, hsh_ref, lru_ref, dpool_ref,
         wp_ref, pb_ref, ps_ref, cw_ref, wg_ref, l_ref, wout_ref, invh_ref) = refs[:14]
        del refs[:14]
        scatter_in = refs[:len(scatter_args)]
        del refs[:len(scatter_args)]
        dproj_ref, v512_ref, dpw_ref, dga_ref, dgx_ref = refs[:5]
        recv = refs[5:5 + n_s]
        (dwp_acc, dwg_acc, v1024_ref, b_ref, gs_ref, ehead_ref, dxbhead_ref, hc_ref,
         send_sems, recv_sems, local_sems) = refs[5 + n_s:]
        i = pl.program_id(0)
        tbk = nb - 1 - i

        start_scatter, wait_scatter = _scatter_ops(scatter, scatter_in, recv, (send_sems, recv_sems, local_sems))

        @pl.when(i == 0)
        def _():
            start_scatter()
            for ref in (v512_ref, v1024_ref, dwp_acc, dwg_acc, ehead_ref, dxbhead_ref, hc_ref):
                ref[...] = jnp.zeros_like(ref)

        dcat = _dot_nt(dh1_ref[...].astype(BF16), wout_ref[...])

        has_prev = (tbk > 0).astype(F32)
        n = tb + HALO
        inv_head = invh_ref[...]

        dpoolb = dpool_ref[...]
        q = _dot(dpoolb, wp_ref[...]) + pb_ref[...]
        dyp = dcat[:, 0:512]
        dq = dyp * ps_ref[...]
        dqb = dq.astype(BF16)
        v512_ref[0:1, :] += _colsum(dyp * q)
        v512_ref[1:2, :] += _colsum(dq)
        dwp_acc[...] += _dot_tn(dpoolb, dqb)
        dd = _dot_nt(dqb, wp_ref[...])
        e = _scale_by_inv_count(dd, tbk == 0, inv_head)
        e_ext = jnp.concatenate([e, ehead_ref[...]], axis=0)
        du_pool = _pool_windows(e_ext, n, False)[0:tb] - dd
        ehead_ref[...] = e[0:HALO]

        gel, dgel = lru_ref[:, 2048:2560], lru_ref[:, 2560:3072]
        hsv = hs_ref[...]
        dcl = dcat[:, 512:1024]
        dhs = dcl * gel
        dug = dcl * hsv * dgel
        cw = cw_ref[...]
        xb, r, ig, a = lru_ref[:, 0:512], lru_ref[:, 512:1024], lru_ref[:, 1024:1536], lru_ref[:, 1536:2048]
        first_row = (tbk * tb + lax.broadcasted_iota(jnp.int32, (tb, 1), 0)) == 0
        c_l = LRU_C * _log_sigmoid(l_ref[...])
        a2, m2, mult = _lru_decay(r, a, c_l, first_row)
        b_ref[...] = dhs
        row = lax.broadcasted_iota(jnp.int32, (SUB, 512), 0)

        def group(jj, hnext):
            o = pl.multiple_of((tb // SUB - 1 - jj) * SUB, SUB)
            a8 = lru_ref[pl.ds(o, SUB), 1536:2048]
            d8 = b_ref[pl.ds(o, SUB), :]
            b8 = a8 * d8
            for sh in (1, 2, 4):
                ash = jnp.where(row < SUB - sh, pltpu.roll(a8, SUB - sh, 0), 1.0)
                bsh = jnp.where(row < SUB - sh, pltpu.roll(b8, SUB - sh, 0), 0.0)
                b8 = a8 * bsh + b8
                a8 = a8 * ash
            h8 = a8 * hnext + b8
            gs_ref[pl.ds(o, SUB), :] = d8 + jnp.where(row < SUB - 1, pltpu.roll(h8, SUB - 1, 0), hnext)
            return jnp.broadcast_to(h8[0:1, :], (SUB, 512))

        def trip(k, carry):
            for u in range(SCAN_UNROLL):
                carry = group(k * SCAN_UNROLL + u, carry)
            return carry

        hc_ref[...] = lax.fori_loop(0, tb // (SUB * SCAN_UNROLL), trip, hc_ref[...])
        gsum = gs_ref[...]
        hs_ext = jnp.concatenate([hsh_ref[...] * has_prev, hsv], axis=0)
        hprev = pltpu.roll(hs_ext, 1, 0)[SUB:]
        da = gsum * hprev
        dmult = jnp.where(first_row, 0.0, gsum * (ig * xb))
        di = gsum * mult * xb
        dxb = gsum * mult * ig
        dla = da * a - dmult * a2 * lax.rsqrt(m2)
        dr = dla * c_l
        v512_ref[3:4, :] += _colsum(dla * r)
        dgp = jnp.concatenate([dr * r * (1.0 - r), di * ig * (1.0 - ig)], axis=1)
        dgpb = dgp.astype(BF16)
        v1024_ref[0:1, :] += _colsum(dgp)
        dwg_acc[...] += _dot_tn(xb.astype(BF16), dgpb)
        dxb = dxb + _dot_nt(dgpb, wg_ref[...])
        n8 = tb + SUB
        dxb_ext = jnp.concatenate([dxb, dxbhead_ref[...]], axis=0)
        ul = ul_ref[...]
        du_lru = cw[3:4, :] * dxb
        v512_ref[7:8, :] += _colsum(dxb * ul)
        for j in range(1, 4):
            ahead = pltpu.roll(dxb_ext, n8 - j, 0)[0:tb]
            du_lru = du_lru + cw[3 - j:4 - j, :] * ahead
            v512_ref[4 + (3 - j):5 + (3 - j), :] += _colsum(ahead * ul)
        dxbhead_ref[...] = dxb[0:SUB]
        v512_ref[2:3, :] += _colsum(dxb)

        dproj_ref[...] = jnp.concatenate([du_pool, du_lru, dug], axis=1).astype(BF16)

        @pl.when(i == nb - 1)
        def _():
            v512_ref[3:4, :] = v512_ref[3:4, :] * (LRU_C * _sigmoid(-l_ref[...]))
            v512_ref[8:9, :] = v1024_ref[0:1, 0:512]
            v512_ref[9:10, :] = v1024_ref[0:1, 512:1024]
            for g in range(N_POOL_GROUPS):
                dpw_ref[g * 128:(g + 1) * 128, :] = dwp_acc[g * 128:(g + 1) * 128, g * 128:(g + 1) * 128]
            odd_head = (lax.broadcasted_iota(jnp.int32, (512, 128), 0) // 64) % 2 == 1
            for out_ref, col0 in ((dga_ref, 0), (dgx_ref, 512)):
                pairs = jnp.concatenate([dwg_acc[128 * k:128 * (k + 1), col0 + 128 * k:col0 + 128 * (k + 1)]
                                         for k in range(LRU_HEADS // 2)], axis=0)
                out_ref[...] = jnp.where(odd_head, pltpu.roll(pairs, 64, 1), pairs)[:, 0:64]
            wait_scatter()

    rev = lambda w: pl.BlockSpec((tb, w), lambda i: (nb - 1 - i, 0))
    halo = lambda rows, w: pl.BlockSpec((rows, w), lambda i: (jnp.maximum((nb - 1 - i) * (tb // rows) - 1, 0), 0))
    any_spec = pl.BlockSpec(memory_space=pl.ANY)
    smalls = [wp_bd, pool_b, pool_scale, conv_w, wg_bd, lru_l, w_out, _inv_count_head()]
    lru_third = pl.BlockSpec((tb, 512), lambda i: (nb - 1 - i, 1))
    return pl.pallas_call(
        body, name="mix_bwd", grid=(nb,),
        in_specs=[rev(d), lru_third, rev(512), halo(SUB, 512), rev(3072), rev(512)]
        + [_const_spec(s.shape) for s in smalls] + [any_spec] * len(scatter_args),
        out_specs=[rev(1536), pl.BlockSpec((16, 512), lambda i: (0, 0)), pl.BlockSpec((512, 128), lambda i: (0, 0)),
                   pl.BlockSpec((512, 64), lambda i: (0, 0)), pl.BlockSpec((512, 64), lambda i: (0, 0))]
        + [any_spec] * n_s,
        out_shape=[jax.ShapeDtypeStruct((t_len, 1536), BF16), jax.ShapeDtypeStruct((16, 512), F32),
                   jax.ShapeDtypeStruct((512, 128), F32), jax.ShapeDtypeStruct((512, 64), F32),
                   jax.ShapeDtypeStruct((512, 64), F32)]
        + _scatter_out_shape(scatter),
        scratch_shapes=[pltpu.VMEM(wp_bd.shape, F32), pltpu.VMEM(wg_bd.shape, F32), pltpu.VMEM((8, 1024), F32),
                        pltpu.VMEM((tb, 512), F32), pltpu.VMEM((tb, 512), F32), pltpu.VMEM((HALO, 512), F32),
                        pltpu.VMEM((SUB, 512), F32), pltpu.VMEM((SUB, 512), F32)]
        + _exchange_scratch(n_s),
        input_output_aliases=_scatter_aliases(scatter, 6 + len(smalls), 5), compiler_params=_params(),
    )(dh1, proj, hs, hs, lru_saved, dpool_saved, *smalls, *scatter_args)


def _wgrad(name, a, b, whole, by_rows, tb):
    t_len, m = a.shape
    n = b.shape[1]
    nb = t_len // tb
    n_w, n_r = len(whole), len(by_rows)
    n_small = n_w + n_r

    def body(*refs):
        a_ref, b_ref = refs[:2]
        small_in = refs[2:2 + n_small]
        out_ref = refs[2 + n_small]
        small_out = refs[3 + n_small:3 + 2 * n_small]
        acc_ref, stage_ref = refs[3 + 2 * n_small:5 + 2 * n_small]
        rest = refs[5 + 2 * n_small:]
        if n_small:
            send_partials, reduce_and_send_sums, finish_small = _small_allreduce(
                small_in[:n_w], small_in[n_w:], small_out[:n_w], small_out[n_w:], rest[:n_w], rest[n_w:n_small],
                rest[n_small:n_small + n_r], *rest[n_small + n_r:])
        i = pl.program_id(0)

        @pl.when(i == 0)
        def _():
            if n_small:
                send_partials()
            acc_ref[...] = jnp.zeros_like(acc_ref)

        acc_ref[...] += _dot_tn(a_ref[...], b_ref[...].astype(BF16))

        if n_small:
            @pl.when(i == nb // 2)
            def _():
                reduce_and_send_sums()

        @pl.when(i == nb - 1)
        def _():
            stage_ref[...] = acc_ref[...].astype(BF16)
            pltpu.sync_copy(stage_ref, out_ref)
            if n_small:
                finish_small()

    small = list(whole) + list(by_rows)
    vmem_spec = pl.BlockSpec(memory_space=pltpu.VMEM)
    res = pl.pallas_call(
        body, name=name, grid=(nb,),
        in_specs=[pl.BlockSpec((tb, m), lambda i: (i, 0)), pl.BlockSpec((tb, n), lambda i: (i, 0))] + [vmem_spec] * n_small,
        out_specs=[pl.BlockSpec(memory_space=pl.ANY)] + [vmem_spec] * n_small,
        out_shape=[jax.ShapeDtypeStruct((m, n), BF16)] + [jax.ShapeDtypeStruct(s_.shape, F32) for s_ in small],
        scratch_shapes=[pltpu.VMEM((m, n), F32), pltpu.VMEM((m, n), BF16)]
        + (_small_allreduce_scratch(whole, by_rows) if n_small else []),
        compiler_params=_params(),
    )(a, b, *small)
    return res[0], res[1:1 + n_w], res[1 + n_w:]


def _in_bwd(dproj, x, dh1, g_mix, w_in, scatter, tb):
    t_len, d = x.shape
    nb = t_len // tb
    n_s = len(scatter['blocks'])
    scatter_args = _scatter_args(scatter)

    def body(*refs):
        dproj_ref, x_ref, dh1_ref, g_ref, win_ref = refs[:5]
        scatter_in = refs[5:5 + len(scatter_args)]
        dx_ref, vec_ref = refs[5 + len(scatter_args):7 + len(scatter_args)]
        recv = refs[7 + len(scatter_args):7 + len(scatter_args) + n_s]
        vec_acc, send_sems, recv_sems, local_sems, vec_land, small_send, small_recv = refs[7 + len(scatter_args) + n_s:]
        start_scatter, wait_scatter = _scatter_ops(scatter, scatter_in, recv, (send_sems, recv_sems, local_sems))
        send_partials, reduce_and_send_sums, finish_small = _small_allreduce(
            [vec_acc], [], [vec_ref], [], [vec_land], [], [], small_send, small_recv)
        i = pl.program_id(0)

        @pl.when(i == 0)
        def _():
            start_scatter()
            vec_acc[...] = jnp.zeros_like(vec_acc)

        dz1 = _dot(dproj_ref[...], win_ref[...])
        g = g_ref[...]
        _, xh, rr = _rms_fwd(x_ref[...], g)
        dx_ref[...] = dh1_ref[...] + _rms_bwd(xh, rr, g, dz1)
        vec_acc[0:1, :] += _colsum(dz1 * xh)

        @pl.when(i == nb - 1)
        def _():
            send_partials()
            reduce_and_send_sums()
            finish_small()
            wait_scatter()

    row_spec = lambda w: pl.BlockSpec((tb, w), lambda i: (i, 0))
    any_spec = pl.BlockSpec(memory_space=pl.ANY)
    return pl.pallas_call(
        body, name="in_bwd", grid=(nb,),
        in_specs=[row_spec(dproj.shape[1]), row_spec(d), row_spec(d), _const_spec(g_mix.shape), _const_spec(w_in.shape)]
        + [any_spec] * len(scatter_args),
        out_specs=[row_spec(d), pl.BlockSpec((8, d), lambda i: (0, 0))] + [any_spec] * n_s,
        out_shape=[jax.ShapeDtypeStruct((t_len, d), F32), jax.ShapeDtypeStruct((8, d), F32)] + _scatter_out_shape(scatter),
        scratch_shapes=[pltpu.VMEM((8, d), F32)] + _exchange_scratch(n_s)
        + _small_allreduce_scratch([jax.ShapeDtypeStruct((8, d), F32)], []),
        input_output_aliases=_scatter_aliases(scatter, 5, 2), compiler_params=_params(),
    )(dproj, x, dh1, g_mix, w_in, *scatter_args)


def _small_allreduce(whole_in, rows_in, whole_out, rows_out, whole_land, rows_land, rows_sum, send_sems, recv_sems):
    n_w, n_r = len(whole_in), len(rows_in)
    per = [r.shape[0] // N_DEV for r in rows_in]
    me = _my_index()

    def dev(s):
        return (s // 4, (s // 2) % 2, s % 2)

    def rows_of(t, s):
        return pl.ds(s * per[t], per[t])

    def mine(t):
        return pl.ds(pl.multiple_of(me * per[t], 8), per[t])

    def partial(t, s, slot):
        if t < n_w:
            src, dst = whole_in[t], whole_land[t]
        else:
            src, dst = rows_in[t - n_w].at[rows_of(t - n_w, s)], rows_land[t - n_w]
        return pltpu.make_async_remote_copy(
            src_ref=src, dst_ref=dst.at[slot], send_sem=send_sems.at[t, s], recv_sem=recv_sems.at[t, slot],
            device_id=dev(s), device_id_type=MESH)

    def summed(t, s, rows, slot):
        return pltpu.make_async_remote_copy(
            src_ref=rows_sum[t].at[rows], dst_ref=rows_sum[t].at[rows], send_sem=send_sems.at[n_w + n_r + t, s],
            recv_sem=recv_sems.at[n_w + n_r + t, slot], device_id=dev(s), device_id_type=MESH)

    def send_partials():
        for s in range(N_DEV):
            @pl.when(s != me)
            def _():
                for t in range(n_w + n_r):
                    partial(t, s, me).start()
        for t in range(n_w):
            whole_land[t][me] = whole_in[t][...]
        for t in range(n_r):
            rows_land[t][me] = rows_in[t][mine(t), :]

    def reduce_and_send_sums():
        for s in range(N_DEV):
            @pl.when(s != me)
            def _():
                for t in range(n_w + n_r):
                    partial(t, s, s).wait_recv()
        for t in range(n_w):
            total = whole_land[t][0]
            for s in range(1, N_DEV):
                total = total + whole_land[t][s]
            whole_out[t][...] = total
        for t in range(n_r):
            total = rows_land[t][0]
            for s in range(1, N_DEV):
                total = total + rows_land[t][s]
            rows_sum[t][mine(t), :] = total
        for s in range(N_DEV):
            @pl.when(s != me)
            def _():
                for t in range(n_r):
                    summed(t, s, mine(t), me).start()

    def finish():
        for s in range(N_DEV):
            @pl.when(s != me)
            def _():
                for t in range(n_r):
                    summed(t, s, rows_of(t, s), s).wait_recv()
                    summed(t, s, mine(t), me).wait_send()
                for t in range(n_w + n_r):
                    partial(t, s, me).wait_send()
        for t in range(n_r):
            rows_out[t][...] = rows_sum[t][...]

    return send_partials, reduce_and_send_sums, finish


def _small_allreduce_scratch(whole, by_rows):
    n_sem = len(whole) + 2 * len(by_rows)
    return ([pltpu.VMEM((N_DEV,) + a.shape, F32) for a in whole]
            + [pltpu.VMEM((N_DEV, a.shape[0] // N_DEV, a.shape[1]), F32) for a in by_rows]
            + [pltpu.VMEM(a.shape, F32) for a in by_rows]
            + [pltpu.SemaphoreType.DMA((n_sem, N_DEV)), pltpu.SemaphoreType.DMA((n_sem, N_DEV))])


def _adam_update(g, w, m, v):
    m_new = ADAM_B1 * m + (1.0 - ADAM_B1) * g
    v_new = ADAM_B2 * v + (1.0 - ADAM_B2) * jnp.square(g)
    m_hat = m_new / (1.0 - ADAM_B1 ** ADAM_STEP)
    v_hat = v_new / (1.0 - ADAM_B2 ** ADAM_STEP)
    return -ADAM_LR * (m_hat / (jnp.sqrt(v_hat) + ADAM_EPS) + ADAM_WD * w), m_new, v_new


def _adamw_groups(name, groups):
    n = len(groups)

    def body(*refs):
        for k in range(n):
            g_ref, w_ref, m_ref, v_ref = refs[4 * k:4 * k + 4]
            g_out, d_out, m_out, v_out = refs[4 * n + 4 * k:4 * n + 4 * k + 4]
            if len(g_ref.shape) == 3:
                g = g_ref[0].astype(F32)
                for s in range(1, g_ref.shape[0]):
                    g = g + g_ref[s].astype(F32)
            else:
                g = g_ref[...]
            g_out[...] = g
            d_out[...], m_out[...], v_out[...] = _adam_update(g, w_ref[...], m_ref[...], v_ref[...])

    flat = [a for grp in groups for a in grp]
    out = pl.pallas_call(
        body, name=name, out_shape=[jax.ShapeDtypeStruct(grp[1].shape, F32) for grp in groups for _ in range(4)],
        compiler_params=pltpu.CompilerParams(vmem_limit_bytes=VMEM_LIMIT))(*flat)
    return [out[4 * k:4 * k + 4] for k in range(n)]


def _adamw(name, parts, w, m, v, row_block):
    n_src, rows, cols = parts.shape
    rb = min(row_block, rows)

    def body(p_ref, w_ref, m_ref, v_ref, g_out, d_out, m_out, v_out):
        g = p_ref[0].astype(F32)
        for s in range(1, n_src):
            g = g + p_ref[s].astype(F32)
        g_out[...] = g
        d_out[...], m_out[...], v_out[...] = _adam_update(g, w_ref[...], m_ref[...], v_ref[...])

    spec = pl.BlockSpec((rb, cols), lambda i: (i, 0))
    return pl.pallas_call(
        body, name=name, grid=(rows // rb,),
        in_specs=[pl.BlockSpec((n_src, rb, cols), lambda i: (0, i, 0)), spec, spec, spec],
        out_specs=[spec] * 4, out_shape=[jax.ShapeDtypeStruct((rows, cols), F32)] * 4,
        compiler_params=pltpu.CompilerParams(dimension_semantics=("parallel",), vmem_limit_bytes=VMEM_LIMIT),
    )(parts, w, m, v)


def _block_diag(blocks):
    g, a, b = blocks.shape
    eye = jnp.eye(g, dtype=blocks.dtype)
    return (eye[:, None, :, None] * blocks[:, :, None, :]).reshape(g * a, g * b)


def kernel(x, p, norm_mix_g, w_in, pool_w, pool_b, pool_scale, conv_w, conv_b, gate_a_w, gate_a_b, gate_x_w, gate_x_b, lru_L, w_out, norm_mlp_g, w_up, w_down, norm_ple_g, w_ple_gate, b_ple_gate, w_ple_proj, norm_final_g, loss_target, m_norm_mix_g, m_w_in, m_pool_w, m_pool_b, m_pool_scale, m_conv_w, m_conv_b, m_gate_a_w, m_gate_a_b, m_gate_x_w, m_gate_x_b, m_lru_L, m_w_out, m_norm_mlp_g, m_w_up, m_w_down, m_norm_ple_g, m_w_ple_gate, m_b_ple_gate, m_w_ple_proj, m_norm_final_g, v_norm_mix_g, v_w_in, v_pool_w, v_pool_b, v_pool_scale, v_conv_w, v_conv_b, v_gate_a_w, v_gate_a_b, v_gate_x_w, v_gate_x_b, v_lru_L, v_w_out, v_norm_mlp_g, v_w_up, v_w_down, v_norm_ple_g, v_w_ple_gate, v_b_ple_gate, v_w_ple_proj, v_norm_final_g):
    t_len, d = x.shape[1], x.shape[2]
    tbs = {k: min(v, t_len) for k, v in TIME_BLOCKS.items()}
    me = _my_index()

    win_g, wout_g, convw_g = _gather("gather_mixer_weights",
                                     [w_in[0].T.astype(BF16), w_out[0].astype(BF16), conv_w[0]])
    w_in_f = win_g.reshape(-1, d)
    conv_w_f = jnp.transpose(convw_g, (1, 0, 2)).reshape(convw_g.shape[1], -1)
    wp_bd = _block_diag(pool_w[0]).astype(BF16)
    wg_bd = jnp.concatenate([_block_diag(gate_a_w[0]), _block_diag(gate_x_w[0])], axis=1).astype(BF16)
    gate_b2 = jnp.concatenate([gate_a_b.reshape(1, -1), gate_x_b.reshape(1, -1)], axis=1)
    mixer_small = (norm_mix_g, w_in_f, wp_bd, pool_b.reshape(1, -1), pool_scale, conv_w_f, conv_b, wg_bd, gate_b2, lru_L,
                   wout_g.reshape(-1, d))

    x2 = x[0]
    later = [w_up[0].astype(BF16), w_down[0].astype(BF16), w_ple_gate[0].astype(BF16), w_ple_proj[0].astype(BF16)]
    h1, z1, proj, hs, cat, lru_saved, dpool_saved, wup_g, wdn_g, wgate_g, wproj_g = _mix_fwd(
        x2, *mixer_small, later, [_core_major_slot, _core_major_slot, None, None], GATHER_FORWARD_AT, tbs['mix_fwd'])
    w_down_f = wdn_g.reshape(-1, d)
    w_proj_f = jnp.transpose(wproj_g, (1, 0, 2)).reshape(wproj_g.shape[1], -1)
    h2, z2, up = _mlp_fwd(h1, norm_mlp_g, wup_g, w_down_f, tbs['mlp_fwd'])
    dh2, ple_vec, dw_gate, dw_proj = _ple(h2, p[0, 0], loss_target[0], norm_ple_g, wgate_g.reshape(-1, d), b_ple_gate,
                                          w_proj_f, norm_final_g.reshape(1, -1), tbs['ple'])
    everyone = list(range(N_DEV))
    n_proj = w_ple_proj.shape[2]
    dz2_0, dw_up_0, dw_down_0 = _mlp_bwd_part(
        0, MLP_BWD_SPLIT, dh2, z2, up, wup_g, w_down_f, None, h1, norm_mlp_g, _scatter_plan([], [], []), tbs['mlp_bwd'])
    half = N_DEV // MLP_BWD_SPLIT
    south = [_device_of_core_major_slot(k) for k in range(half)]
    north = [_device_of_core_major_slot(k) for k in range(half, N_DEV)]
    scatter = _scatter_plan(
        [dw_up_0, dw_down_0, dw_gate.reshape(N_DEV, -1, d), jnp.transpose(dw_proj.reshape(-1, N_DEV, n_proj), (1, 0, 2))],
        [south, south, everyone, everyone], [None, None, None, None])
    dh1, mlp_vec, dw_up_1, dw_down_1, recv_up, recv_down, recv_gate, recv_proj = _mlp_bwd_part(
        1, MLP_BWD_SPLIT, dh2, z2, up, wup_g, w_down_f, dz2_0, h1, norm_mlp_g, scatter, tbs['mlp_bwd'])
    dw_out, _, _ = _wgrad("wgrad_out", cat, dh1, [], [], tbs['wgrad_out'])
    scatter = _scatter_plan([dw_up_1, dw_down_1, dw_out.reshape(N_DEV, -1, d)], [north, north, everyone],
                            [recv_up, recv_down, None])
    dproj, v512, dpw, dga, dgx, recv_up, recv_down, recv_out = _mix_bwd(
        dh1, proj, hs, lru_saved, dpool_saved, wp_bd, pool_b.reshape(1, -1), pool_scale, conv_w_f, wg_bd, lru_L, wout_g.reshape(-1, d),
        scatter, tbs['mix_bwd'])
    rows1024 = jnp.concatenate([jnp.zeros((1, d), F32), mlp_vec[0:1], ple_vec[1:2], ple_vec[0:1], ple_vec[2:4],
                                jnp.zeros((2, d), F32)], axis=0)
    dw_in_t, (rows1024, rows512), (g_pool_w, g_gate_a_w, g_gate_x_w) = _wgrad(
        "wgrad_in", dproj, z1, [rows1024, v512], [dpw, dga, dgx], tbs['wgrad_in'])
    scatter = _scatter_plan([dw_in_t.reshape(N_DEV, -1, d)], [everyone], [None])
    dx, in_vec, recv_in = _in_bwd(dproj, x2, dh1, norm_mix_g, w_in_f, scatter, tbs['in_bwd'])
    rows1024 = jnp.concatenate([in_vec[0:1], rows1024[1:]], axis=0)
    received = [recv_in, recv_out, recv_up, recv_down, recv_gate, recv_proj]

    shard_w = [w_in[0].T, w_out[0], w_up[0], w_down[0], w_ple_gate[0], w_ple_proj[0]]
    shard_m = [m_w_in[0].T, m_w_out[0], m_w_up[0], m_w_down[0], m_w_ple_gate[0], m_w_ple_proj[0]]
    shard_v = [v_w_in[0].T, v_w_out[0], v_w_up[0], v_w_down[0], v_w_ple_gate[0], v_w_ple_proj[0]]
    names = ["w_in", "w_out", "w_up", "w_down", "w_ple_gate", "w_ple_proj"]
    res = {}
    gridded = ("w_up", "w_down")
    for nm, parts, w_s, m_s, v_s in zip(names, received, shard_w, shard_m, shard_v):
        if nm in gridded:
            res[nm] = [r[None] for r in _adamw("adamw_" + nm, parts, w_s, m_s, v_s, ADAM_ROW_BLOCK)]
    rest = [k for k, nm in enumerate(names) if nm not in gridded]
    for k, out in zip(rest, _adamw_groups("adamw_medium", [(received[k], shard_w[k], shard_m[k], shard_v[k]) for k in rest])):
        res[names[k]] = [r[None] for r in out]
    res["w_in"] = [jnp.swapaxes(r, 1, 2) for r in res["w_in"]]

    def rows_of_1024(a, b, c, e, f):
        return jnp.concatenate([a, b, c, e, f.reshape(1, -1), jnp.zeros((3, d), F32)], axis=0)

    def rows_of_512(scale, bias, cb, lru, ga, gx):
        z = jnp.zeros((1, 512), F32)
        return jnp.concatenate([scale, bias.reshape(1, -1), cb, lru, z, z, z, z, ga.reshape(1, -1), gx.reshape(1, -1),
                                z, z, z, z, z, z], axis=0)

    n_conv = conv_w.shape[2]
    groups = [
        (rows1024, *[rows_of_1024(*t) for t in (
            (norm_mix_g, norm_mlp_g, norm_ple_g, b_ple_gate, norm_final_g),
            (m_norm_mix_g, m_norm_mlp_g, m_norm_ple_g, m_b_ple_gate, m_norm_final_g),
            (v_norm_mix_g, v_norm_mlp_g, v_norm_ple_g, v_b_ple_gate, v_norm_final_g))]),
        (rows512, *[rows_of_512(*t) for t in (
            (pool_scale, pool_b, conv_b, lru_L, gate_a_b, gate_x_b),
            (m_pool_scale, m_pool_b, m_conv_b, m_lru_L, m_gate_a_b, m_gate_x_b),
            (v_pool_scale, v_pool_b, v_conv_b, v_lru_L, v_gate_a_b, v_gate_x_b))]),
        (g_pool_w, *[a.reshape(-1, a.shape[-1]) for a in (pool_w, m_pool_w, v_pool_w)]),
        (g_gate_a_w, *[a.reshape(-1, a.shape[-1]) for a in (gate_a_w, m_gate_a_w, v_gate_a_w)]),
        (g_gate_x_w, *[a.reshape(-1, a.shape[-1]) for a in (gate_x_w, m_gate_x_w, v_gate_x_w)]),
        (lax.dynamic_slice_in_dim(rows512[4:8], me * n_conv, n_conv, axis=1), conv_w[0], m_conv_w[0], v_conv_w[0]),
    ]
    r1024, r512, r_pool, r_ga, r_gx, r_conv = _adamw_groups("adamw_small", groups)
    loss = rows1024[5, 0]
    for k, nm in enumerate(["norm_mix_g", "norm_mlp_g", "norm_ple_g", "b_ple_gate"]):
        res[nm] = [a[k:k + 1] for a in r1024]
    res["norm_final_g"] = [a[4] for a in r1024]
    res["pool_scale"] = [a[0:1] for a in r512]
    res["pool_b"] = [a[1:2].reshape(pool_b.shape) for a in r512]
    res["conv_b"] = [a[2:3] for a in r512]
    res["lru_L"] = [a[3:4] for a in r512]
    res["gate_a_b"] = [a[8:9].reshape(gate_a_b.shape) for a in r512]
    res["gate_x_b"] = [a[9:10].reshape(gate_x_b.shape) for a in r512]
    res["pool_w"] = [a.reshape(pool_w.shape) for a in r_pool]
    res["gate_a_w"] = [a.reshape(gate_a_w.shape) for a in r_ga]
    res["gate_x_w"] = [a.reshape(gate_x_w.shape) for a in r_gx]
    res["conv_w"] = [a[None] for a in r_conv]
    order = ["norm_mix_g", "w_in", "pool_w", "pool_b", "pool_scale", "conv_w", "conv_b", "gate_a_w", "gate_a_b",
             "gate_x_w", "gate_x_b", "lru_L", "w_out", "norm_mlp_g", "w_up", "w_down", "norm_ple_g", "w_ple_gate",
             "b_ple_gate", "w_ple_proj", "norm_final_g"]
    return (loss, dx[None], *[res[nm][kind] for kind in range(4) for nm in order])
```

```python
import jax
import jax.numpy as jnp
from jax import lax
from jax.experimental import pallas as pl
from jax.experimental.pallas import tpu as pltpu

F32 = jnp.float32
BF16 = jnp.bfloat16
MESH = pl.DeviceIdType.MESH

N_DEV = 8
RMS_EPS = 1e-6
LRU_C = 8.0
POOL_WINDOWS = (2, 4, 8, 16)
N_POOL_GROUPS = 4
LRU_HEADS = 8
HALO = 16
SUB = 8
GELU_C0 = 0.7978845608028654
GELU_C1 = 0.044715

ADAM_LR = 0.001
ADAM_B1 = 0.9
ADAM_B2 = 0.999
ADAM_EPS = 1e-08
ADAM_WD = 0.01
ADAM_STEP = 10

VMEM_LIMIT = 60 * 1024 * 1024
TIME_BLOCKS = dict(mix_fwd=512, mlp_fwd=512, ple=512, mlp_bwd=512, wgrad_out=1024, mix_bwd=512, wgrad_in=1024, in_bwd=512)
ADAM_ROW_BLOCK = 256
SCAN_UNROLL = 4
MLP_BWD_SPLIT = 2
GATHER_FORWARD_AT = (0.5, 0.875, 1.0, 1.0)


def _params(n_arbitrary=1):
    return pltpu.CompilerParams(dimension_semantics=("arbitrary",) * n_arbitrary, vmem_limit_bytes=VMEM_LIMIT)


def _dot(a, b):
    return jnp.dot(a, b, preferred_element_type=F32)


def _dot_nt(a, b):
    return lax.dot_general(a, b, (((1,), (1,)), ((), ())), preferred_element_type=F32)


def _dot_tn(a, b):
    return lax.dot_general(a, b, (((0,), (0,)), ((), ())), preferred_element_type=F32)


def _rms_fwd(x, g):
    r = lax.rsqrt(jnp.mean(x * x, axis=-1, keepdims=True) + RMS_EPS)
    xh = x * r
    return xh * g, xh, r


def _rms_bwd(xh, r, g, dz):
    dxh = dz * g
    return r * (dxh - xh * jnp.mean(dxh * xh, axis=-1, keepdims=True))


def _colsum(a):
    return jnp.sum(a, axis=0, keepdims=True)


def _sigmoid(a):
    return 0.5 * jnp.tanh(0.5 * a) + 0.5


def _gelu_parts(u):
    u2 = u * u
    th = jnp.tanh(GELU_C0 * (u + GELU_C1 * u * u2))
    gel = 0.5 * u * (1.0 + th)
    dgel = 0.5 * (1.0 + th) + 0.5 * u * (1.0 - th * th) * (GELU_C0 * (1.0 + 3.0 * GELU_C1 * u2))
    return gel, dgel


def _my_index():
    return 4 * lax.axis_index("x") + 2 * lax.axis_index("y") + lax.axis_index("c")


def _all_to_all(srcs_of, dsts, send_sems, recv_sems, local_sems, dests=None):
    n = len(dsts)
    me = _my_index()
    dests = [list(range(N_DEV))] * n if dests is None else dests

    def remote(t, s):
        return pltpu.make_async_remote_copy(
            src_ref=srcs_of[t](s), dst_ref=dsts[t].at[me], send_sem=send_sems.at[t, s], recv_sem=recv_sems.at[t, me],
            device_id=(s // 4, (s // 2) % 2, s % 2), device_id_type=MESH)

    def arrival(t, s):
        return pltpu.make_async_remote_copy(
            src_ref=srcs_of[t](dests[t][0]), dst_ref=dsts[t].at[s], send_sem=send_sems.at[t, s],
            recv_sem=recv_sems.at[t, s], device_id=(s // 4, (s // 2) % 2, s % 2), device_id_type=MESH)

    def local(t, s):
        return pltpu.make_async_copy(srcs_of[t](s), dsts[t].at[s], local_sems.at[t])

    def start():
        for s in range(N_DEV):
            to_s = [t for t in range(n) if s in dests[t]]

            @pl.when(s == me)
            def _():
                for t in to_s:
                    local(t, s).start()

            @pl.when(s != me)
            def _():
                for t in to_s:
                    remote(t, s).start()

    def wait():
        for s in range(N_DEV):
            to_s = [t for t in range(n) if s in dests[t]]

            @pl.when(s == me)
            def _():
                for t in to_s:
                    local(t, s).wait()
                    for src in range(N_DEV):
                        if src != s:
                            arrival(t, src).wait_recv()

            @pl.when(s != me)
            def _():
                for t in to_s:
                    remote(t, s).wait_send()

    return start, wait


N_GATHER_COPIES = 7


def _core_major_slot(dev):
    return 4 * dev[2] + 2 * dev[0] + dev[1]


def _device_of_core_major_slot(k):
    return (k % 4) * 2 + k // 4


def _two_level_gather(srcs, dsts, send_sems, recv_sems, local_sems, slots=None):
    n = len(dsts)
    x, y, c = lax.axis_index("x"), lax.axis_index("y"), lax.axis_index("c")
    me, sibling = (x, y, c), (x, y, 1 - c)
    chips = [(1 - x, y), (x, 1 - y), (1 - x, 1 - y)]

    def slot(t, dev):
        return 4 * dev[0] + 2 * dev[1] + dev[2] if slots is None or slots[t] is None else slots[t](dev)

    def copy(t, k, block, to, src=None):
        return pltpu.make_async_remote_copy(
            src_ref=dsts[t].at[slot(t, block)] if src is None else src, dst_ref=dsts[t].at[slot(t, block)],
            send_sem=send_sems.at[t, k], recv_sem=recv_sems.at[t, k], device_id=to, device_id_type=MESH)

    def local(t):
        return pltpu.make_async_copy(srcs[t], dsts[t].at[slot(t, me)], local_sems.at[t])

    def start():
        for t in range(n):
            local(t).start()
            for j, chip in enumerate(chips):
                copy(t, 1 + j, me, (*chip, c), src=srcs[t]).start()
            copy(t, 0, me, sibling, src=srcs[t]).start()

    def forward(t):
        for j, chip in enumerate(chips):
            copy(t, 1 + j, (*chip, c), me).wait_recv()
            copy(t, 4 + j, (*chip, c), sibling).start()

    def finish():
        for t in range(n):
            copy(t, 0, sibling, me).wait_recv()
            for j, chip in enumerate(chips):
                copy(t, 4 + j, (*chip, 1 - c), me).wait_recv()
            copy(t, 0, me, sibling, src=srcs[t]).wait_send()
            for j, chip in enumerate(chips):
                copy(t, 1 + j, me, (*chip, c), src=srcs[t]).wait_send()
                copy(t, 4 + j, (*chip, c), sibling).wait_send()
            local(t).wait()

    return start, forward, finish


def _hosted_gather(i, nb, forward_at, srcs, dsts, sems, slots=None):
    start, forward, finish = _two_level_gather(srcs, dsts, *sems, slots)

    def after_step():
        for t, f in enumerate(forward_at):
            @pl.when(i == min(nb - 1, int(f * nb)))
            def _():
                forward(t)

        @pl.when(i == nb - 1)
        def _():
            finish()

    return start, after_step


def _gather_scratch(n):
    return [pltpu.SemaphoreType.DMA((n, N_GATHER_COPIES)), pltpu.SemaphoreType.DMA((n, N_GATHER_COPIES)),
            pltpu.SemaphoreType.DMA((n,))]


def _gather(name, srcs):
    n = len(srcs)

    def body(*refs):
        start, forward, finish = _two_level_gather(refs[:n], refs[n:2 * n], *refs[2 * n:])
        start()
        for t in range(n):
            forward(t)
        finish()

    any_spec = pl.BlockSpec(memory_space=pl.ANY)
    return pl.pallas_call(
        body, name=name, in_specs=[any_spec] * n, out_specs=[any_spec] * n,
        out_shape=[jax.ShapeDtypeStruct((N_DEV,) + a.shape, a.dtype) for a in srcs], scratch_shapes=_gather_scratch(n),
    )(*srcs)


def _scatter_plan(blocks, dests, landing):
    return dict(blocks=list(blocks), dests=[list(dd) for dd in dests], landing=list(landing))


def _scatter_args(plan):
    return plan['blocks'] + [a for a in plan['landing'] if a is not None]


def _scatter_out_shape(plan):
    return [jax.ShapeDtypeStruct((N_DEV,) + b.shape[1:], b.dtype) for b in plan['blocks']]


def _scatter_aliases(plan, first_in, first_out):
    given = [t for t, a in enumerate(plan['landing']) if a is not None]
    return {first_in + len(plan['blocks']) + k: first_out + t for k, t in enumerate(given)}


def _scatter_ops(plan, in_refs, out_refs, sems):
    n = len(plan['blocks'])
    srcs_of = [(lambda s, r=in_refs[t], dd=plan['dests'][t]: r.at[dd.index(s)]) for t in range(n)]
    return _all_to_all(srcs_of, out_refs, *sems, dests=plan['dests'])


def _exchange_scratch(n):
    return [pltpu.SemaphoreType.DMA((n, N_DEV)), pltpu.SemaphoreType.DMA((n, N_DEV)), pltpu.SemaphoreType.DMA((n,))]


def _const_spec(shape):
    nd = len(shape)
    return pl.BlockSpec(shape, lambda i: (0,) * nd, pipeline_mode=pl.Buffered(1))


def _pool_windows(up_ext, n, forward):
    sh = (lambda k: k) if forward else (lambda k: n - k)
    s2 = up_ext + pltpu.roll(up_ext, sh(1), 0)
    t4 = s2[:, 128:]
    s4 = t4 + pltpu.roll(t4, sh(2), 0)
    t8 = s4[:, 128:]
    s8 = t8 + pltpu.roll(t8, sh(4), 0)
    t16 = s8[:, 128:]
    s16 = t16 + pltpu.roll(t16, sh(8), 0)
    return jnp.concatenate([s2[:, :128], s4[:, :128], s8[:, :128], s16], axis=1)


def _inv_count_head():
    t = jnp.arange(1, HALO + 1, dtype=F32)[:, None]
    return jnp.concatenate([jnp.broadcast_to(1.0 / jnp.minimum(t, float(w)), (HALO, 128)) for w in POOL_WINDOWS], axis=1)


def _scale_by_inv_count(v, is_first_block, inv_head):
    inv_row = jnp.concatenate([jnp.full((1, 128), 1.0 / w, F32) for w in POOL_WINDOWS], axis=1)
    head = v[0:HALO] * jnp.where(is_first_block, inv_head, inv_row)
    return jnp.concatenate([head, v[HALO:] * inv_row], axis=0)


def _lru_decay(r, a, c_l, first_row):
    a2 = a * a
    m2 = -jnp.tanh(c_l * r) * (a2 + 1.0)
    return a2, m2, jnp.where(first_row, 1.0, jnp.sqrt(m2))


def _log_sigmoid(v):
    return -(jnp.maximum(-v, 0.0) + jnp.log1p(jnp.exp(-jnp.abs(v))))


def _conv_fwd(ul_ext, cw, cb):
    return (cb + cw[3:4, :] * ul_ext + cw[2:3, :] * pltpu.roll(ul_ext, 1, 0)
            + cw[1:2, :] * pltpu.roll(ul_ext, 2, 0) + cw[0:1, :] * pltpu.roll(ul_ext, 3, 0))


def _mix_fwd(x, g_mix, w_in, wp_bd, pool_b, pool_scale, conv_w, conv_b, wg_bd, gate_b, lru_l, w_out, gather_srcs,
             gather_slots, forward_at, tb):
    t_len, d = x.shape
    nb = t_len // tb
    n_g = len(gather_srcs)

    def body(*refs):
        (x_ref, g_ref, win_ref, wp_ref, pb_ref, ps_ref, cw_ref, cb_ref, wg_ref, gb_ref, l_ref, wout_ref,
         invh_ref) = refs[:13]
        gsrc = refs[13:13 + n_g]
        h1_ref, z1_ref, proj_ref, hs_ref, cat_ref, lru_ref, dpool_ref = refs[13 + n_g:20 + n_g]
        gdst = refs[20 + n_g:20 + 2 * n_g]
        ext_ref, a_ref, b_ref, hc_ref, send_sems, recv_sems, local_sems = refs[20 + 2 * n_g:]
        i = pl.program_id(0)
        start_gather, after_step = _hosted_gather(i, nb, forward_at, gsrc, gdst, (send_sems, recv_sems, local_sems),
                                                  gather_slots)

        @pl.when(i == 0)
        def _():
            start_gather()
            ext_ref[0:HALO, :] = jnp.zeros((HALO, 1024), F32)
            hc_ref[...] = jnp.zeros_like(hc_ref)

        xv = x_ref[...]
        z, _, _ = _rms_fwd(xv, g_ref[...])
        zb = z.astype(BF16)
        z1_ref[...] = zb
        proj = _dot_nt(zb, win_ref[...])
        proj_ref[...] = proj
        ext_ref[HALO:, :] = proj[:, 0:1024]
        ug = proj[:, 1024:1536]
        n = tb + HALO
        up_ext = ext_ref[:, 0:512]
        win = _pool_windows(up_ext, n, True)[HALO:]
        dpool = _scale_by_inv_count(win, i == 0, invh_ref[...]) - proj[:, 0:512]
        dpoolb = dpool.astype(BF16)
        dpool_ref[...] = dpoolb
        q = _dot(dpoolb, wp_ref[...]) + pb_ref[...]
        y_pool = q * ps_ref[...]
        xb = _conv_fwd(ext_ref[:, 512:1024], cw_ref[...], cb_ref[...])[HALO:]
        first_row = (i * tb + lax.broadcasted_iota(jnp.int32, (tb, 1), 0)) == 0
        c_l = LRU_C * _log_sigmoid(l_ref[...])
        gp = _dot(xb.astype(BF16), wg_ref[...]) + gb_ref[...]
        r = 1.0 / (1.0 + jnp.exp(-gp[:, :512]))
        ig = _sigmoid(gp[:, 512:])
        a = jnp.exp(c_l * r)
        _, _, mult = _lru_decay(r, a, c_l, first_row)
        lru_ref[:, 0:512] = xb
        lru_ref[:, 512:1024] = r
        lru_ref[:, 1024:1536] = ig
        lru_ref[:, 1536:2048] = a
        a_ref[...] = a
        b_ref[...] = mult * (ig * xb)
        row = lax.broadcasted_iota(jnp.int32, (SUB, 512), 0)

        def group(j, hprev):
            o = pl.multiple_of(j * SUB, SUB)
            a8 = a_ref[pl.ds(o, SUB), :]
            b8 = b_ref[pl.ds(o, SUB), :]
            for sh in (1, 2, 4):
                ash = jnp.where(row >= sh, pltpu.roll(a8, sh, 0), 1.0)
                bsh = jnp.where(row >= sh, pltpu.roll(b8, sh, 0), 0.0)
                b8 = a8 * bsh + b8
                a8 = a8 * ash
            h8 = a8 * hprev + b8
            hs_ref[pl.ds(o, SUB), :] = h8
            return jnp.broadcast_to(h8[SUB - 1:SUB, :], (SUB, 512))

        def trip(k, carry):
            for u in range(SCAN_UNROLL):
                carry = group(k * SCAN_UNROLL + u, carry)
            return carry

        hc_ref[...] = lax.fori_loop(0, tb // (SUB * SCAN_UNROLL), trip, hc_ref[...])
        gel, _ = _gelu_parts(ug)
        lru_ref[:, 2048:2560] = gel
        y_lru = hs_ref[...] * gel
        catb = jnp.concatenate([y_pool, y_lru], axis=1).astype(BF16)
        cat_ref[...] = catb
        h1_ref[...] = xv + _dot(catb, wout_ref[...])
        ext_ref[0:HALO, :] = ext_ref[tb:tb + HALO, :]

        after_step()

    row_spec = lambda w: pl.BlockSpec((tb, w), lambda i: (i, 0))
    any_spec = pl.BlockSpec(memory_space=pl.ANY)
    smalls = [g_mix, w_in, wp_bd, pool_b, pool_scale, conv_w, conv_b, wg_bd, gate_b, lru_l, w_out, _inv_count_head()]
    return pl.pallas_call(
        body, name="mix_fwd", grid=(nb,),
        in_specs=[row_spec(d)] + [_const_spec(s.shape) for s in smalls] + [any_spec] * n_g,
        out_specs=[row_spec(d), row_spec(d), row_spec(1536), row_spec(512), row_spec(1024), row_spec(2560), row_spec(512)]
        + [any_spec] * n_g,
        out_shape=[jax.ShapeDtypeStruct((t_len, d), F32), jax.ShapeDtypeStruct((t_len, d), BF16),
                   jax.ShapeDtypeStruct((t_len, 1536), F32), jax.ShapeDtypeStruct((t_len, 512), F32),
                   jax.ShapeDtypeStruct((t_len, 1024), BF16), jax.ShapeDtypeStruct((t_len, 2560), F32),
                   jax.ShapeDtypeStruct((t_len, 512), BF16)]
        + [jax.ShapeDtypeStruct((N_DEV,) + s.shape, s.dtype) for s in gather_srcs],
        scratch_shapes=[pltpu.VMEM((tb + HALO, 1024), F32), pltpu.VMEM((tb, 512), F32), pltpu.VMEM((tb, 512), F32),
                        pltpu.VMEM((SUB, 512), F32)] + _gather_scratch(n_g),
        compiler_params=_params(),
    )(x, *smalls, *gather_srcs)


def _mlp_fwd(h1, g_mlp, w_up, w_down, proj, tb):
    t_len, d = h1.shape
    nb = t_len // tb
    n_chunk, _, fc = w_up.shape

    def body(h1_ref, g_ref, wup_ref, wdn_ref, ug_ref, h2_ref, z2_ref, up_ref, dgel_ref):
        _, dgel = _gelu_parts(ug_ref[...])
        dgel_ref[...] = dgel
        xv = h1_ref[...]
        z, _, _ = _rms_fwd(xv, g_ref[...])
        zb = z.astype(BF16)
        z2_ref[...] = zb
        acc = xv
        for c in range(n_chunk):
            u = _dot(zb, wup_ref[c])
            up_ref[:, c * fc:(c + 1) * fc] = u.astype(BF16)
            act = jnp.square(jnp.maximum(u, 0.0)).astype(BF16)
            acc = acc + _dot(act, wdn_ref[c * fc:(c + 1) * fc, :])
        h2_ref[...] = acc

    row_spec = lambda w: pl.BlockSpec((tb, w), lambda i: (i, 0))
    return pl.pallas_call(
        body, name="mlp_fwd", grid=(nb,),
        in_specs=[row_spec(d), _const_spec(g_mlp.shape), _const_spec(w_up.shape), _const_spec(w_down.shape),
                  pl.BlockSpec((tb, 512), lambda i: (i, 2))],
        out_specs=[row_spec(d), row_spec(d), row_spec(n_chunk * fc), row_spec(512)],
        out_shape=[jax.ShapeDtypeStruct((t_len, d), F32), jax.ShapeDtypeStruct((t_len, d), BF16),
                   jax.ShapeDtypeStruct((t_len, n_chunk * fc), BF16), jax.ShapeDtypeStruct((t_len, 512), F32)],
        compiler_params=_params(),
    )(h1, g_mlp, w_up, w_down, proj)


def _ple(h2, p, target, g_ple, w_gate, b_gate, w_proj, g_final, tb):
    t_len, d = h2.shape
    nb = t_len // tb
    pd = p.shape[1]

    def body(h2_ref, p_ref, tgt_ref, g_ref, wg_ref, bg_ref, wp_ref, gf_ref,
             dh2_ref, vec_ref, dwg_out, dwp_out, dwg_acc, dwp_acc, dwg_stage, dwp_stage):
        i = pl.program_id(0)

        @pl.when(i == 0)
        def _():
            vec_ref[...] = jnp.zeros_like(vec_ref)
            dwg_acc[...] = jnp.zeros_like(dwg_acc)
            dwp_acc[...] = jnp.zeros_like(dwp_acc)

        h2 = h2_ref[...]
        g2 = g_ref[...]
        z3, xh2, r2 = _rms_fwd(h2, g2)
        z3b = z3.astype(BF16)
        gate = _sigmoid(_dot(z3b, wg_ref[...]) + bg_ref[...])
        pb = p_ref[...].astype(BF16)
        pp = _dot(pb, wp_ref[...])
        h3 = h2 + gate * pp
        gf = gf_ref[...]
        y, xh3, r3 = _rms_fwd(h3, gf)
        err = y - tgt_ref[...]
        loss_rows = jnp.mean(err * err, axis=-1, keepdims=True)
        dy = err * (1.0 / d)
        dh3 = _rms_bwd(xh3, r3, gf, dy)
        dgl = (dh3 * pp) * (gate * (1.0 - gate))
        dpp = dh3 * gate
        dglb = dgl.astype(BF16)
        dwg_acc[...] += _dot_tn(z3b, dglb)
        dwp_acc[...] += _dot_tn(pb, dpp.astype(BF16))
        dz3 = _dot_nt(dglb, wg_ref[...])
        dh2_ref[...] = dh3 + _rms_bwd(xh2, r2, g2, dz3)
        vec_ref[0:1, :] += _colsum(dgl)
        vec_ref[1:2, :] += _colsum(dz3 * xh2)
        vec_ref[2:3, :] += _colsum(dy * xh3)
        vec_ref[3:4, :] += 0.5 * jnp.sum(loss_rows)

        @pl.when(i == nb - 1)
        def _():
            dwg_stage[...] = dwg_acc[...].astype(BF16)
            dwp_stage[...] = dwp_acc[...].astype(BF16)
            pltpu.sync_copy(dwg_stage, dwg_out)
            pltpu.sync_copy(dwp_stage, dwp_out)

    row_spec = lambda w: pl.BlockSpec((tb, w), lambda i: (i, 0))
    any_spec = pl.BlockSpec(memory_space=pl.ANY)
    smalls = [g_ple, w_gate, b_gate, w_proj, g_final]
    return pl.pallas_call(
        body, name="ple_fwd_bwd", grid=(nb,),
        in_specs=[row_spec(d), row_spec(pd), row_spec(d)] + [_const_spec(s.shape) for s in smalls],
        out_specs=[row_spec(d), pl.BlockSpec((8, d), lambda i: (0, 0)), any_spec, any_spec],
        out_shape=[jax.ShapeDtypeStruct((t_len, d), F32), jax.ShapeDtypeStruct((8, d), F32),
                   jax.ShapeDtypeStruct(w_gate.shape, BF16), jax.ShapeDtypeStruct(w_proj.shape, BF16)],
        scratch_shapes=[pltpu.VMEM(w_gate.shape, F32), pltpu.VMEM(w_proj.shape, F32), pltpu.VMEM(w_gate.shape, BF16),
                        pltpu.VMEM(w_proj.shape, BF16)],
        compiler_params=_params(),
    )(h2, p, target, *smalls)


def _mlp_bwd_part(part, n_part, dh2, z2, up, w_up, w_down, dz2_prev, h1, g_mlp, scatter, tb):
    t_len, d = dh2.shape
    nb = t_len // tb
    n_chunk_all, _, fc = w_up.shape
    n_chunk = n_chunk_all // n_part
    first, last = part == 0, part == n_part - 1

    def body(*refs):
        refs = list(refs)
        dh2_ref, z2_ref, up_ref, wup_ref, wdn_ref = refs[:5]
        del refs[:5]
        dzp_ref = None if first else refs.pop(0)
        h1_ref, g_ref = (refs.pop(0), refs.pop(0)) if last else (None, None)
        scatter_in = [refs.pop(0) for _ in _scatter_args(scatter)]
        out_ref = refs.pop(0)
        vec_ref = refs.pop(0) if last else None
        dwup_out, dwdn_out = refs.pop(0), refs.pop(0)
        scatter_out = [refs.pop(0) for _ in scatter['blocks']]
        dwup_acc, dwdn_acc, up_stage, dn_stage = refs[:4]
        if scatter['blocks']:
            start_scatter, wait_scatter = _scatter_ops(scatter, scatter_in, scatter_out, refs[4:])
        i = pl.program_id(0)

        @pl.when(i == 0)
        def _():
            if scatter['blocks']:
                start_scatter()
            dwup_acc[...] = jnp.zeros_like(dwup_acc)
            dwdn_acc[...] = jnp.zeros_like(dwdn_acc)
            if last:
                vec_ref[...] = jnp.zeros_like(vec_ref)

        dh2 = dh2_ref[...]
        dh2b = dh2.astype(BF16)
        z2b = z2_ref[...]
        dz2 = jnp.zeros((tb, d), F32) if first else dzp_ref[...]
        for c in range(n_chunk):
            u = up_ref[:, c * fc:(c + 1) * fc].astype(F32)
            ur = jnp.maximum(u, 0.0)
            dact = _dot_nt(dh2b, wdn_ref[c * fc:(c + 1) * fc, :])
            dupb = (dact * (2.0 * ur)).astype(BF16)
            dwdn_acc[c * fc:(c + 1) * fc, :] += _dot_tn((ur * ur).astype(BF16), dh2b)
            dwup_acc[c] += _dot_tn(z2b, dupb)
            dz2 = dz2 + _dot_nt(dupb, wup_ref[c])
        if last:
            g = g_ref[...]
            _, xh, r = _rms_fwd(h1_ref[...], g)
            out_ref[...] = dh2 + _rms_bwd(xh, r, g, dz2)
            vec_ref[0:1, :] += _colsum(dz2 * xh)
        else:
            out_ref[...] = dz2

        @pl.when(i == nb - 1)
        def _():
            for c in range(n_chunk):
                up_stage[...] = dwup_acc[c].astype(BF16)
                dn_stage[...] = dwdn_acc[c * fc:(c + 1) * fc, :].astype(BF16)
                pltpu.sync_copy(up_stage, dwup_out.at[c])
                pltpu.sync_copy(dn_stage, dwdn_out.at[c])
            if scatter['blocks']:
                wait_scatter()

    row_spec = lambda w: pl.BlockSpec((tb, w), lambda i: (i, 0))
    any_spec = pl.BlockSpec(memory_space=pl.ANY)
    args = [dh2, z2, up, w_up, w_down]
    in_specs = [row_spec(d), row_spec(d), pl.BlockSpec((tb, n_chunk * fc), lambda i: (i, part)),
                pl.BlockSpec((n_chunk, d, fc), lambda i: (part, 0, 0), pipeline_mode=pl.Buffered(1)),
                pl.BlockSpec((n_chunk * fc, d), lambda i: (part, 0), pipeline_mode=pl.Buffered(1))]
    if not first:
        args.append(dz2_prev)
        in_specs.append(row_spec(d))
    if last:
        args += [h1, g_mlp]
        in_specs += [row_spec(d), _const_spec(g_mlp.shape)]
    n_in = len(args)
    args += _scatter_args(scatter)
    in_specs += [any_spec] * len(_scatter_args(scatter))
    out_specs = [row_spec(d)]
    out_shape = [jax.ShapeDtypeStruct((t_len, d), F32)]
    if last:
        out_specs.append(pl.BlockSpec((8, d), lambda i: (0, 0)))
        out_shape.append(jax.ShapeDtypeStruct((8, d), F32))
    out_specs += [any_spec, any_spec]
    out_shape += [jax.ShapeDtypeStruct((n_chunk, d, fc), BF16), jax.ShapeDtypeStruct((n_chunk, fc, d), BF16)]
    n_out = len(out_shape)
    out_specs += [any_spec] * len(scatter['blocks'])
    out_shape += _scatter_out_shape(scatter)
    return pl.pallas_call(
        body, name=f"mlp_bwd_{part}", grid=(nb,), in_specs=in_specs, out_specs=out_specs, out_shape=out_shape,
        scratch_shapes=[pltpu.VMEM((n_chunk, d, fc), F32), pltpu.VMEM((n_chunk * fc, d), F32),
                        pltpu.VMEM((d, fc), BF16), pltpu.VMEM((fc, d), BF16)]
        + (_exchange_scratch(len(scatter['blocks'])) if scatter['blocks'] else []),
        input_output_aliases=_scatter_aliases(scatter, n_in, n_out), compiler_params=_params(),
    )(*args)


def _mix_bwd(dh1, proj, hs, lru_saved, dpool_saved, dgel_saved, wp_bd, pool_b, pool_scale, conv_w, wg_bd, lru_l, w_out, scatter, tb):
    t_len, d = dh1.shape
    nb = t_len // tb
    n_s = len(scatter['blocks'])
    scatter_args = _scatter_args(scatter)

    def body(*refs):
        refs = list(refs)
        (dh1_ref, ul_ref, hs_ref, hsh_ref, lru_ref, dpool_ref, dgel_ref,
         wp_ref, pb_ref, ps_ref, cw_ref, wg_ref, l_ref, wout_ref, invh_ref) = refs[:15]
        del refs[:15]
        scatter_in = refs[:len(scatter_args)]
        del refs[:len(scatter_args)]
        dproj_ref, v512_ref, dpw_ref, dga_ref, dgx_ref = refs[:5]
        recv = refs[5:5 + n_s]
        (dwp_acc, dwg_acc, v1024_ref, b_ref, gs_ref, ehead_ref, dxbhead_ref, hc_ref,
         send_sems, recv_sems, local_sems) = refs[5 + n_s:]
        i = pl.program_id(0)
        tbk = nb - 1 - i

        start_scatter, wait_scatter = _scatter_ops(scatter, scatter_in, recv, (send_sems, recv_sems, local_sems))

        @pl.when(i == 0)
        def _():
            start_scatter()
            for ref in (v512_ref, v1024_ref, dwp_acc, dwg_acc, ehead_ref, dxbhead_ref, hc_ref):
                ref[...] = jnp.zeros_like(ref)

        dcat = _dot_nt(dh1_ref[...].astype(BF16), wout_ref[...])

        has_prev = (tbk > 0).astype(F32)
        n = tb + HALO
        inv_head = invh_ref[...]

        dpoolb = dpool_ref[...]
        q = _dot(dpoolb, wp_ref[...]) + pb_ref[...]
        dyp = dcat[:, 0:512]
        dq = dyp * ps_ref[...]
        dqb = dq.astype(BF16)
        v512_ref[0:1, :] += _colsum(dyp * q)
        v512_ref[1:2, :] += _colsum(dq)
        dwp_acc[...] += _dot_tn(dpoolb, dqb)
        dd = _dot_nt(dqb, wp_ref[...])
        e = _scale_by_inv_count(dd, tbk == 0, inv_head)
        e_ext = jnp.concatenate([e, ehead_ref[...]], axis=0)
        du_pool = _pool_windows(e_ext, n, False)[0:tb] - dd
        ehead_ref[...] = e[0:HALO]

        gel, dgel = lru_ref[:, 2048:2560], dgel_ref[...]
        hsv = hs_ref[...]
        dcl = dcat[:, 512:1024]
        dhs = dcl * gel
        dug = dcl * hsv * dgel
        cw = cw_ref[...]
        xb, r, ig, a = lru_ref[:, 0:512], lru_ref[:, 512:1024], lru_ref[:, 1024:1536], lru_ref[:, 1536:2048]
        first_row = (tbk * tb + lax.broadcasted_iota(jnp.int32, (tb, 1), 0)) == 0
        c_l = LRU_C * _log_sigmoid(l_ref[...])
        a2, m2, mult = _lru_decay(r, a, c_l, first_row)
        b_ref[...] = dhs
        row = lax.broadcasted_iota(jnp.int32, (SUB, 512), 0)

        def group(jj, hnext):
            o = pl.multiple_of((tb // SUB - 1 - jj) * SUB, SUB)
            a8 = lru_ref[pl.ds(o, SUB), 1536:2048]
            d8 = b_ref[pl.ds(o, SUB), :]
            b8 = a8 * d8
            for sh in (1, 2, 4):
                ash = jnp.where(row < SUB - sh, pltpu.roll(a8, SUB - sh, 0), 1.0)
                bsh = jnp.where(row < SUB - sh, pltpu.roll(b8, SUB - sh, 0), 0.0)
                b8 = a8 * bsh + b8
                a8 = a8 * ash
            h8 = a8 * hnext + b8
            gs_ref[pl.ds(o, SUB), :] = d8 + jnp.where(row < SUB - 1, pltpu.roll(h8, SUB - 1, 0), hnext)
            return jnp.broadcast_to(h8[0:1, :], (SUB, 512))

        def trip(k, carry):
            for u in range(SCAN_UNROLL):
                carry = group(k * SCAN_UNROLL + u, carry)
            return carry

        hc_ref[...] = lax.fori_loop(0, tb // (SUB * SCAN_UNROLL), trip, hc_ref[...])
        gsum = gs_ref[...]
        hs_ext = jnp.concatenate([hsh_ref[...] * has_prev, hsv], axis=0)
        hprev = pltpu.roll(hs_ext, 1, 0)[SUB:]
        da = gsum * hprev
        dmult = jnp.where(first_row, 0.0, gsum * (ig * xb))
        di = gsum * mult * xb
        dxb = gsum * mult * ig
        dla = da * a - dmult * a2 * lax.rsqrt(m2)
        dr = dla * c_l
        v512_ref[3:4, :] += _colsum(dla * r)
        dgp = jnp.concatenate([dr * r * (1.0 - r), di * ig * (1.0 - ig)], axis=1)
        dgpb = dgp.astype(BF16)
        v1024_ref[0:1, :] += _colsum(dgp)
        dwg_acc[...] += _dot_tn(xb.astype(BF16), dgpb)
        dxb = dxb + _dot_nt(dgpb, wg_ref[...])
        n8 = tb + SUB
        dxb_ext = jnp.concatenate([dxb, dxbhead_ref[...]], axis=0)
        ul = ul_ref[...]
        du_lru = cw[3:4, :] * dxb
        v512_ref[7:8, :] += _colsum(dxb * ul)
        for j in range(1, 4):
            ahead = pltpu.roll(dxb_ext, n8 - j, 0)[0:tb]
            du_lru = du_lru + cw[3 - j:4 - j, :] * ahead
            v512_ref[4 + (3 - j):5 + (3 - j), :] += _colsum(ahead * ul)
        dxbhead_ref[...] = dxb[0:SUB]
        v512_ref[2:3, :] += _colsum(dxb)

        dproj_ref[...] = jnp.concatenate([du_pool, du_lru, dug], axis=1).astype(BF16)

        @pl.when(i == nb - 1)
        def _():
            v512_ref[3:4, :] = v512_ref[3:4, :] * (LRU_C * _sigmoid(-l_ref[...]))
            v512_ref[8:9, :] = v1024_ref[0:1, 0:512]
            v512_ref[9:10, :] = v1024_ref[0:1, 512:1024]
            for g in range(N_POOL_GROUPS):
                dpw_ref[g * 128:(g + 1) * 128, :] = dwp_acc[g * 128:(g + 1) * 128, g * 128:(g + 1) * 128]
            odd_head = (lax.broadcasted_iota(jnp.int32, (512, 128), 0) // 64) % 2 == 1
            for out_ref, col0 in ((dga_ref, 0), (dgx_ref, 512)):
                pairs = jnp.concatenate([dwg_acc[128 * k:128 * (k + 1), col0 + 128 * k:col0 + 128 * (k + 1)]
                                         for k in range(LRU_HEADS // 2)], axis=0)
                out_ref[...] = jnp.where(odd_head, pltpu.roll(pairs, 64, 1), pairs)[:, 0:64]
            wait_scatter()

    rev = lambda w: pl.BlockSpec((tb, w), lambda i: (nb - 1 - i, 0))
    halo = lambda rows, w: pl.BlockSpec((rows, w), lambda i: (jnp.maximum((nb - 1 - i) * (tb // rows) - 1, 0), 0))
    any_spec = pl.BlockSpec(memory_space=pl.ANY)
    smalls = [wp_bd, pool_b, pool_scale, conv_w, wg_bd, lru_l, w_out, _inv_count_head()]
    lru_third = pl.BlockSpec((tb, 512), lambda i: (nb - 1 - i, 1))
    return pl.pallas_call(
        body, name="mix_bwd", grid=(nb,),
        in_specs=[rev(d), lru_third, rev(512), halo(SUB, 512), rev(2560), rev(512), rev(512)]
        + [_const_spec(s.shape) for s in smalls] + [any_spec] * len(scatter_args),
        out_specs=[rev(1536), pl.BlockSpec((16, 512), lambda i: (0, 0)), pl.BlockSpec((512, 128), lambda i: (0, 0)),
                   pl.BlockSpec((512, 64), lambda i: (0, 0)), pl.BlockSpec((512, 64), lambda i: (0, 0))]
        + [any_spec] * n_s,
        out_shape=[jax.ShapeDtypeStruct((t_len, 1536), BF16), jax.ShapeDtypeStruct((16, 512), F32),
                   jax.ShapeDtypeStruct((512, 128), F32), jax.ShapeDtypeStruct((512, 64), F32),
                   jax.ShapeDtypeStruct((512, 64), F32)]
        + _scatter_out_shape(scatter),
        scratch_shapes=[pltpu.VMEM(wp_bd.shape, F32), pltpu.VMEM(wg_bd.shape, F32), pltpu.VMEM((8, 1024), F32),
                        pltpu.VMEM((tb, 512), F32), pltpu.VMEM((tb, 512), F32), pltpu.VMEM((HALO, 512), F32),
                        pltpu.VMEM((SUB, 512), F32), pltpu.VMEM((SUB, 512), F32)]
        + _exchange_scratch(n_s),
        input_output_aliases=_scatter_aliases(scatter, 7 + len(smalls), 5), compiler_params=_params(),
    )(dh1, proj, hs, hs, lru_saved, dpool_saved, dgel_saved, *smalls, *scatter_args)


def _wgrad(name, a, b, whole, by_rows, tb):
    t_len, m = a.shape
    n = b.shape[1]
    nb = t_len // tb
    n_w, n_r = len(whole), len(by_rows)
    n_small = n_w + n_r

    def body(*refs):
        a_ref, b_ref = refs[:2]
        small_in = refs[2:2 + n_small]
        out_ref = refs[2 + n_small]
        small_out = refs[3 + n_small:3 + 2 * n_small]
        acc_ref, stage_ref = refs[3 + 2 * n_small:5 + 2 * n_small]
        rest = refs[5 + 2 * n_small:]
        if n_small:
            send_partials, reduce_and_send_sums, finish_small = _small_allreduce(
                small_in[:n_w], small_in[n_w:], small_out[:n_w], small_out[n_w:], rest[:n_w], rest[n_w:n_small],
                rest[n_small:n_small + n_r], *rest[n_small + n_r:])
        i = pl.program_id(0)

        @pl.when(i == 0)
        def _():
            if n_small:
                send_partials()
            acc_ref[...] = jnp.zeros_like(acc_ref)

        acc_ref[...] += _dot_tn(a_ref[...], b_ref[...].astype(BF16))

        if n_small:
            @pl.when(i == nb // 2)
            def _():
                reduce_and_send_sums()

        @pl.when(i == nb - 1)
        def _():
            stage_ref[...] = acc_ref[...].astype(BF16)
            pltpu.sync_copy(stage_ref, out_ref)
            if n_small:
                finish_small()

    small = list(whole) + list(by_rows)
    vmem_spec = pl.BlockSpec(memory_space=pltpu.VMEM)
    res = pl.pallas_call(
        body, name=name, grid=(nb,),
        in_specs=[pl.BlockSpec((tb, m), lambda i: (i, 0)), pl.BlockSpec((tb, n), lambda i: (i, 0))] + [vmem_spec] * n_small,
        out_specs=[pl.BlockSpec(memory_space=pl.ANY)] + [vmem_spec] * n_small,
        out_shape=[jax.ShapeDtypeStruct((m, n), BF16)] + [jax.ShapeDtypeStruct(s_.shape, F32) for s_ in small],
        scratch_shapes=[pltpu.VMEM((m, n), F32), pltpu.VMEM((m, n), BF16)]
        + (_small_allreduce_scratch(whole, by_rows) if n_small else []),
        compiler_params=_params(),
    )(a, b, *small)
    return res[0], res[1:1 + n_w], res[1 + n_w:]


def _in_bwd(dproj, x, dh1, g_mix, w_in, scatter, tb):
    t_len, d = x.shape
    nb = t_len // tb
    n_s = len(scatter['blocks'])
    scatter_args = _scatter_args(scatter)

    def body(*refs):
        dproj_ref, x_ref, dh1_ref, g_ref, win_ref = refs[:5]
        scatter_in = refs[5:5 + len(scatter_args)]
        dx_ref, vec_ref = refs[5 + len(scatter_args):7 + len(scatter_args)]
        recv = refs[7 + len(scatter_args):7 + len(scatter_args) + n_s]
        vec_acc, send_sems, recv_sems, local_sems, vec_land, small_send, small_recv = refs[7 + len(scatter_args) + n_s:]
        start_scatter, wait_scatter = _scatter_ops(scatter, scatter_in, recv, (send_sems, recv_sems, local_sems))
        send_partials, reduce_and_send_sums, finish_small = _small_allreduce(
            [vec_acc], [], [vec_ref], [], [vec_land], [], [], small_send, small_recv)
        i = pl.program_id(0)

        @pl.when(i == 0)
        def _():
            start_scatter()
            vec_acc[...] = jnp.zeros_like(vec_acc)

        dz1 = _dot(dproj_ref[...], win_ref[...])
        g = g_ref[...]
        _, xh, rr = _rms_fwd(x_ref[...], g)
        dx_ref[...] = dh1_ref[...] + _rms_bwd(xh, rr, g, dz1)
        vec_acc[0:1, :] += _colsum(dz1 * xh)

        @pl.when(i == nb - 1)
        def _():
            send_partials()
            reduce_and_send_sums()
            finish_small()
            wait_scatter()

    row_spec = lambda w: pl.BlockSpec((tb, w), lambda i: (i, 0))
    any_spec = pl.BlockSpec(memory_space=pl.ANY)
    return pl.pallas_call(
        body, name="in_bwd", grid=(nb,),
        in_specs=[row_spec(dproj.shape[1]), row_spec(d), row_spec(d), _const_spec(g_mix.shape), _const_spec(w_in.shape)]
        + [any_spec] * len(scatter_args),
        out_specs=[row_spec(d), pl.BlockSpec((8, d), lambda i: (0, 0))] + [any_spec] * n_s,
        out_shape=[jax.ShapeDtypeStruct((t_len, d), F32), jax.ShapeDtypeStruct((8, d), F32)] + _scatter_out_shape(scatter),
        scratch_shapes=[pltpu.VMEM((8, d), F32)] + _exchange_scratch(n_s)
        + _small_allreduce_scratch([jax.ShapeDtypeStruct((8, d), F32)], []),
        input_output_aliases=_scatter_aliases(scatter, 5, 2), compiler_params=_params(),
    )(dproj, x, dh1, g_mix, w_in, *scatter_args)


def _small_allreduce(whole_in, rows_in, whole_out, rows_out, whole_land, rows_land, rows_sum, send_sems, recv_sems):
    n_w, n_r = len(whole_in), len(rows_in)
    per = [r.shape[0] // N_DEV for r in rows_in]
    me = _my_index()

    def dev(s):
        return (s // 4, (s // 2) % 2, s % 2)

    def rows_of(t, s):
        return pl.ds(s * per[t], per[t])

    def mine(t):
        return pl.ds(pl.multiple_of(me * per[t], 8), per[t])

    def partial(t, s, slot):
        if t < n_w:
            src, dst = whole_in[t], whole_land[t]
        else:
            src, dst = rows_in[t - n_w].at[rows_of(t - n_w, s)], rows_land[t - n_w]
        return pltpu.make_async_remote_copy(
            src_ref=src, dst_ref=dst.at[slot], send_sem=send_sems.at[t, s], recv_sem=recv_sems.at[t, slot],
            device_id=dev(s), device_id_type=MESH)

    def summed(t, s, rows, slot):
        return pltpu.make_async_remote_copy(
            src_ref=rows_sum[t].at[rows], dst_ref=rows_sum[t].at[rows], send_sem=send_sems.at[n_w + n_r + t, s],
            recv_sem=recv_sems.at[n_w + n_r + t, slot], device_id=dev(s), device_id_type=MESH)

    def send_partials():
        for s in range(N_DEV):
            @pl.when(s != me)
            def _():
                for t in range(n_w + n_r):
                    partial(t, s, me).start()
        for t in range(n_w):
            whole_land[t][me] = whole_in[t][...]
        for t in range(n_r):
            rows_land[t][me] = rows_in[t][mine(t), :]

    def reduce_and_send_sums():
        for s in range(N_DEV):
            @pl.when(s != me)
            def _():
                for t in range(n_w + n_r):
                    partial(t, s, s).wait_recv()
        for t in range(n_w):
            total = whole_land[t][0]
            for s in range(1, N_DEV):
                total = total + whole_land[t][s]
            whole_out[t][...] = total
        for t in range(n_r):
            total = rows_land[t][0]
            for s in range(1, N_DEV):
                total = total + rows_land[t][s]
            rows_sum[t][mine(t), :] = total
        for s in range(N_DEV):
            @pl.when(s != me)
            def _():
                for t in range(n_r):
                    summed(t, s, mine(t), me).start()

    def finish():
        for s in range(N_DEV):
            @pl.when(s != me)
            def _():
                for t in range(n_r):
                    summed(t, s, rows_of(t, s), s).wait_recv()
                    summed(t, s, mine(t), me).wait_send()
                for t in range(n_w + n_r):
                    partial(t, s, me).wait_send()
        for t in range(n_r):
            rows_out[t][...] = rows_sum[t][...]

    return send_partials, reduce_and_send_sums, finish


def _small_allreduce_scratch(whole, by_rows):
    n_sem = len(whole) + 2 * len(by_rows)
    return ([pltpu.VMEM((N_DEV,) + a.shape, F32) for a in whole]
            + [pltpu.VMEM((N_DEV, a.shape[0] // N_DEV, a.shape[1]), F32) for a in by_rows]
            + [pltpu.VMEM(a.shape, F32) for a in by_rows]
            + [pltpu.SemaphoreType.DMA((n_sem, N_DEV)), pltpu.SemaphoreType.DMA((n_sem, N_DEV))])


def _adam_update(g, w, m, v):
    m_new = ADAM_B1 * m + (1.0 - ADAM_B1) * g
    v_new = ADAM_B2 * v + (1.0 - ADAM_B2) * jnp.square(g)
    m_hat = m_new / (1.0 - ADAM_B1 ** ADAM_STEP)
    v_hat = v_new / (1.0 - ADAM_B2 ** ADAM_STEP)
    return -ADAM_LR * (m_hat / (jnp.sqrt(v_hat) + ADAM_EPS) + ADAM_WD * w), m_new, v_new


def _adamw_groups(name, groups):
    n = len(groups)

    def body(*refs):
        for k in range(n):
            g_ref, w_ref, m_ref, v_ref = refs[4 * k:4 * k + 4]
            g_out, d_out, m_out, v_out = refs[4 * n + 4 * k:4 * n + 4 * k + 4]
            if len(g_ref.shape) == 3:
                g = g_ref[0].astype(F32)
                for s in range(1, g_ref.shape[0]):
                    g = g + g_ref[s].astype(F32)
            else:
                g = g_ref[...]
            g_out[...] = g
            d_out[...], m_out[...], v_out[...] = _adam_update(g, w_ref[...], m_ref[...], v_ref[...])

    flat = [a for grp in groups for a in grp]
    out = pl.pallas_call(
        body, name=name, out_shape=[jax.ShapeDtypeStruct(grp[1].shape, F32) for grp in groups for _ in range(4)],
        compiler_params=pltpu.CompilerParams(vmem_limit_bytes=VMEM_LIMIT))(*flat)
    return [out[4 * k:4 * k + 4] for k in range(n)]


def _adamw(name, parts, w, m, v, row_block):
    n_src, rows, cols = parts.shape
    rb = min(row_block, rows)

    def body(p_ref, w_ref, m_ref, v_ref, g_out, d_out, m_out, v_out):
        g = p_ref[0].astype(F32)
        for s in range(1, n_src):
            g = g + p_ref[s].astype(F32)
        g_out[...] = g
        d_out[...], m_out[...], v_out[...] = _adam_update(g, w_ref[...], m_ref[...], v_ref[...])

    spec = pl.BlockSpec((rb, cols), lambda i: (i, 0))
    return pl.pallas_call(
        body, name=name, grid=(rows // rb,),
        in_specs=[pl.BlockSpec((n_src, rb, cols), lambda i: (0, i, 0)), spec, spec, spec],
        out_specs=[spec] * 4, out_shape=[jax.ShapeDtypeStruct((rows, cols), F32)] * 4,
        compiler_params=pltpu.CompilerParams(dimension_semantics=("parallel",), vmem_limit_bytes=VMEM_LIMIT),
    )(parts, w, m, v)


def _block_diag(blocks):
    g, a, b = blocks.shape
    eye = jnp.eye(g, dtype=blocks.dtype)
    return (eye[:, None, :, None] * blocks[:, :, None, :]).reshape(g * a, g * b)


def kernel(x, p, norm_mix_g, w_in, pool_w, pool_b, pool_scale, conv_w, conv_b, gate_a_w, gate_a_b, gate_x_w, gate_x_b, lru_L, w_out, norm_mlp_g, w_up, w_down, norm_ple_g, w_ple_gate, b_ple_gate, w_ple_proj, norm_final_g, loss_target, m_norm_mix_g, m_w_in, m_pool_w, m_pool_b, m_pool_scale, m_conv_w, m_conv_b, m_gate_a_w, m_gate_a_b, m_gate_x_w, m_gate_x_b, m_lru_L, m_w_out, m_norm_mlp_g, m_w_up, m_w_down, m_norm_ple_g, m_w_ple_gate, m_b_ple_gate, m_w_ple_proj, m_norm_final_g, v_norm_mix_g, v_w_in, v_pool_w, v_pool_b, v_pool_scale, v_conv_w, v_conv_b, v_gate_a_w, v_gate_a_b, v_gate_x_w, v_gate_x_b, v_lru_L, v_w_out, v_norm_mlp_g, v_w_up, v_w_down, v_norm_ple_g, v_w_ple_gate, v_b_ple_gate, v_w_ple_proj, v_norm_final_g):
    t_len, d = x.shape[1], x.shape[2]
    tbs = {k: min(v, t_len) for k, v in TIME_BLOCKS.items()}
    me = _my_index()

    win_g, wout_g, convw_g = _gather("gather_mixer_weights",
                                     [w_in[0].T.astype(BF16), w_out[0].astype(BF16), conv_w[0]])
    w_in_f = win_g.reshape(-1, d)
    conv_w_f = jnp.transpose(convw_g, (1, 0, 2)).reshape(convw_g.shape[1], -1)
    wp_bd = _block_diag(pool_w[0]).astype(BF16)
    wg_bd = jnp.concatenate([_block_diag(gate_a_w[0]), _block_diag(gate_x_w[0])], axis=1).astype(BF16)
    gate_b2 = jnp.concatenate([gate_a_b.reshape(1, -1), gate_x_b.reshape(1, -1)], axis=1)
    mixer_small = (norm_mix_g, w_in_f, wp_bd, pool_b.reshape(1, -1), pool_scale, conv_w_f, conv_b, wg_bd, gate_b2, lru_L,
                   wout_g.reshape(-1, d))

    x2 = x[0]
    later = [w_up[0].astype(BF16), w_down[0].astype(BF16), w_ple_gate[0].astype(BF16), w_ple_proj[0].astype(BF16)]
    h1, z1, proj, hs, cat, lru_saved, dpool_saved, wup_g, wdn_g, wgate_g, wproj_g = _mix_fwd(
        x2, *mixer_small, later, [_core_major_slot, _core_major_slot, None, None], GATHER_FORWARD_AT, tbs['mix_fwd'])
    w_down_f = wdn_g.reshape(-1, d)
    w_proj_f = jnp.transpose(wproj_g, (1, 0, 2)).reshape(wproj_g.shape[1], -1)
    h2, z2, up, dgel_saved = _mlp_fwd(h1, norm_mlp_g, wup_g, w_down_f, proj, tbs['mlp_fwd'])
    dh2, ple_vec, dw_gate, dw_proj = _ple(h2, p[0, 0], loss_target[0], norm_ple_g, wgate_g.reshape(-1, d), b_ple_gate,
                                          w_proj_f, norm_final_g.reshape(1, -1), tbs['ple'])
    everyone = list(range(N_DEV))
    n_proj = w_ple_proj.shape[2]
    dz2_0, dw_up_0, dw_down_0 = _mlp_bwd_part(
        0, MLP_BWD_SPLIT, dh2, z2, up, wup_g, w_down_f, None, h1, norm_mlp_g, _scatter_plan([], [], []), tbs['mlp_bwd'])
    half = N_DEV // MLP_BWD_SPLIT
    south = [_device_of_core_major_slot(k) for k in range(half)]
    north = [_device_of_core_major_slot(k) for k in range(half, N_DEV)]
    scatter = _scatter_plan(
        [dw_up_0, dw_down_0, dw_gate.reshape(N_DEV, -1, d), jnp.transpose(dw_proj.reshape(-1, N_DEV, n_proj), (1, 0, 2))],
        [south, south, everyone, everyone], [None, None, None, None])
    dh1, mlp_vec, dw_up_1, dw_down_1, recv_up, recv_down, recv_gate, recv_proj = _mlp_bwd_part(
        1, MLP_BWD_SPLIT, dh2, z2, up, wup_g, w_down_f, dz2_0, h1, norm_mlp_g, scatter, tbs['mlp_bwd'])
    dw_out, _, _ = _wgrad("wgrad_out", cat, dh1, [], [], tbs['wgrad_out'])
    scatter = _scatter_plan([dw_up_1, dw_down_1, dw_out.reshape(N_DEV, -1, d)], [north, north, everyone],
                            [recv_up, recv_down, None])
    dproj, v512, dpw, dga, dgx, recv_up, recv_down, recv_out = _mix_bwd(
        dh1, proj, hs, lru_saved, dpool_saved, dgel_saved, wp_bd, pool_b.reshape(1, -1), pool_scale, conv_w_f, wg_bd, lru_L, wout_g.reshape(-1, d),
        scatter, tbs['mix_bwd'])
    rows1024 = jnp.concatenate([jnp.zeros((1, d), F32), mlp_vec[0:1], ple_vec[1:2], ple_vec[0:1], ple_vec[2:4],
                                jnp.zeros((2, d), F32)], axis=0)
    dw_in_t, (rows1024, rows512), (g_pool_w, g_gate_a_w, g_gate_x_w) = _wgrad(
        "wgrad_in", dproj, z1, [rows1024, v512], [dpw, dga, dgx], tbs['wgrad_in'])
    scatter = _scatter_plan([dw_in_t.reshape(N_DEV, -1, d)], [everyone], [None])
    dx, in_vec, recv_in = _in_bwd(dproj, x2, dh1, norm_mix_g, w_in_f, scatter, tbs['in_bwd'])
    rows1024 = jnp.concatenate([in_vec[0:1], rows1024[1:]], axis=0)
    received = [recv_in, recv_out, recv_up, recv_down, recv_gate, recv_proj]

    shard_w = [w_in[0].T, w_out[0], w_up[0], w_down[0], w_ple_gate[0], w_ple_proj[0]]
    shard_m = [m_w_in[0].T, m_w_out[0], m_w_up[0], m_w_down[0], m_w_ple_gate[0], m_w_ple_proj[0]]
    shard_v = [v_w_in[0].T, v_w_out[0], v_w_up[0], v_w_down[0], v_w_ple_gate[0], v_w_ple_proj[0]]
    names = ["w_in", "w_out", "w_up", "w_down", "w_ple_gate", "w_ple_proj"]
    res = {}
    gridded = ("w_up", "w_down")
    for nm, parts, w_s, m_s, v_s in zip(names, received, shard_w, shard_m, shard_v):
        if nm in gridded:
            res[nm] = [r[None] for r in _adamw("adamw_" + nm, parts, w_s, m_s, v_s, ADAM_ROW_BLOCK)]
    rest = [k for k, nm in enumerate(names) if nm not in gridded]
    for k, out in zip(rest, _adamw_groups("adamw_medium", [(received[k], shard_w[k], shard_m[k], shard_v[k]) for k in rest])):
        res[names[k]] = [r[None] for r in out]
    res["w_in"] = [jnp.swapaxes(r, 1, 2) for r in res["w_in"]]

    def rows_of_1024(a, b, c, e, f):
        return jnp.concatenate([a, b, c, e, f.reshape(1, -1), jnp.zeros((3, d), F32)], axis=0)

    def rows_of_512(scale, bias, cb, lru, ga, gx):
        z = jnp.zeros((1, 512), F32)
        return jnp.concatenate([scale, bias.reshape(1, -1), cb, lru, z, z, z, z, ga.reshape(1, -1), gx.reshape(1, -1),
                                z, z, z, z, z, z], axis=0)

    n_conv = conv_w.shape[2]
    groups = [
        (rows1024, *[rows_of_1024(*t) for t in (
            (norm_mix_g, norm_mlp_g, norm_ple_g, b_ple_gate, norm_final_g),
            (m_norm_mix_g, m_norm_mlp_g, m_norm_ple_g, m_b_ple_gate, m_norm_final_g),
            (v_norm_mix_g, v_norm_mlp_g, v_norm_ple_g, v_b_ple_gate, v_norm_final_g))]),
        (rows512, *[rows_of_512(*t) for t in (
            (pool_scale, pool_b, conv_b, lru_L, gate_a_b, gate_x_b),
            (m_pool_scale, m_pool_b, m_conv_b, m_lru_L, m_gate_a_b, m_gate_x_b),
            (v_pool_scale, v_pool_b, v_conv_b, v_lru_L, v_gate_a_b, v_gate_x_b))]),
        (g_pool_w, *[a.reshape(-1, a.shape[-1]) for a in (pool_w, m_pool_w, v_pool_w)]),
        (g_gate_a_w, *[a.reshape(-1, a.shape[-1]) for a in (gate_a_w, m_gate_a_w, v_gate_a_w)]),
        (g_gate_x_w, *[a.reshape(-1, a.shape[-1]) for a in (gate_x_w, m_gate_x_w, v_gate_x_w)]),
        (lax.dynamic_slice_in_dim(rows512[4:8], me * n_conv, n_conv, axis=1), conv_w[0], m_conv_w[0], v_conv_w[0]),
    ]
    r1024, r512, r_pool, r_ga, r_gx, r_conv = _adamw_groups("adamw_small", groups)
    loss = rows1024[5, 0]
    for k, nm in enumerate(["norm_mix_g", "norm_mlp_g", "norm_ple_g", "b_ple_gate"]):
        res[nm] = [a[k:k + 1] for a in r1024]
    res["norm_final_g"] = [a[4] for a in r1024]
    res["pool_scale"] = [a[0:1] for a in r512]
    res["pool_b"] = [a[1:2].reshape(pool_b.shape) for a in r512]
    res["conv_b"] = [a[2:3] for a in r512]
    res["lru_L"] = [a[3:4] for a in r512]
    res["gate_a_b"] = [a[8:9].reshape(gate_a_b.shape) for a in r512]
    res["gate_x_b"] = [a[9:10].reshape(gate_x_b.shape) for a in r512]
    res["pool_w"] = [a.reshape(pool_w.shape) for a in r_pool]
    res["gate_a_w"] = [a.reshape(gate_a_w.shape) for a in r_ga]
    res["gate_x_w"] = [a.reshape(gate_x_w.shape) for a in r_gx]
    res["conv_w"] = [a[None] for a in r_conv]
    order = ["norm_mix_g", "w_in", "pool_w", "pool_b", "pool_scale", "conv_w", "conv_b", "gate_a_w", "gate_a_b",
             "gate_x_w", "gate_x_b", "lru_L", "w_out", "norm_mlp_g", "w_up", "w_down", "norm_ple_g", "w_ple_gate",
             "b_ple_gate", "w_ple_proj", "norm_final_g"]
    return (loss, dx[None], *[res[nm][kind] for kind in range(4) for nm in order])
```

```python
import jax
import jax.numpy as jnp
from jax import lax
from jax.experimental import pallas as pl
from jax.experimental.pallas import tpu as pltpu

F32 = jnp.float32
BF16 = jnp.bfloat16
MESH = pl.DeviceIdType.MESH

N_DEV = 8
RMS_EPS = 1e-6
LRU_C = 8.0
POOL_WINDOWS = (2, 4, 8, 16)
N_POOL_GROUPS = 4
LRU_HEADS = 8
HALO = 16
SUB = 8
GELU_C0 = 0.7978845608028654
GELU_C1 = 0.044715

ADAM_LR = 0.001
ADAM_B1 = 0.9
ADAM_B2 = 0.999
ADAM_EPS = 1e-08
ADAM_WD = 0.01
ADAM_STEP = 10

VMEM_LIMIT = 60 * 1024 * 1024
TIME_BLOCKS = dict(mix_fwd=512, mlp_fwd=512, ple=512, mlp_bwd=512, wgrad_out=1024, mix_bwd=512, wgrad_in=1024, in_bwd=512)
ADAM_ROW_BLOCK = 256
SCAN_UNROLL = 8
MLP_BWD_SPLIT = 2
GATHER_FORWARD_AT = (0.5, 0.8125, 0.9375, 0.9375)


def _params(n_arbitrary=1):
    return pltpu.CompilerParams(dimension_semantics=("arbitrary",) * n_arbitrary, vmem_limit_bytes=VMEM_LIMIT)


def _dot(a, b):
    return jnp.dot(a, b, preferred_element_type=F32)


def _dot_nt(a, b):
    return lax.dot_general(a, b, (((1,), (1,)), ((), ())), preferred_element_type=F32)


def _dot_tn(a, b):
    return lax.dot_general(a, b, (((0,), (0,)), ((), ())), preferred_element_type=F32)


def _rms_fwd(x, g):
    r = lax.rsqrt(jnp.mean(x * x, axis=-1, keepdims=True) + RMS_EPS)
    xh = x * r
    return xh * g, xh, r


def _rms_bwd(xh, r, g, dz):
    dxh = dz * g
    return r * (dxh - xh * jnp.mean(dxh * xh, axis=-1, keepdims=True))


def _colsum(a):
    return jnp.sum(a, axis=0, keepdims=True)


def _sigmoid(a):
    return 0.5 * jnp.tanh(0.5 * a) + 0.5


def _gelu_parts(u):
    u2 = u * u
    th = jnp.tanh(GELU_C0 * (u + GELU_C1 * u * u2))
    gel = 0.5 * u * (1.0 + th)
    dgel = 0.5 * (1.0 + th) + 0.5 * u * (1.0 - th * th) * (GELU_C0 * (1.0 + 3.0 * GELU_C1 * u2))
    return gel, dgel


def _my_index():
    return 4 * lax.axis_index("x") + 2 * lax.axis_index("y") + lax.axis_index("c")


def _all_to_all(srcs_of, dsts, send_sems, recv_sems, local_sems, dests=None):
    n = len(dsts)
    me = _my_index()
    dests = [list(range(N_DEV))] * n if dests is None else dests

    def remote(t, s):
        return pltpu.make_async_remote_copy(
            src_ref=srcs_of[t](s), dst_ref=dsts[t].at[me], send_sem=send_sems.at[t, s], recv_sem=recv_sems.at[t, me],
            device_id=(s // 4, (s // 2) % 2, s % 2), device_id_type=MESH)

    def arrival(t, s):
        return pltpu.make_async_remote_copy(
            src_ref=srcs_of[t](dests[t][0]), dst_ref=dsts[t].at[s], send_sem=send_sems.at[t, s],
            recv_sem=recv_sems.at[t, s], device_id=(s // 4, (s // 2) % 2, s % 2), device_id_type=MESH)

    def local(t, s):
        return pltpu.make_async_copy(srcs_of[t](s), dsts[t].at[s], local_sems.at[t])

    def start():
        for s in range(N_DEV):
            to_s = [t for t in range(n) if s in dests[t]]

            @pl.when(s == me)
            def _():
                for t in to_s:
                    local(t, s).start()

            @pl.when(s != me)
            def _():
                for t in to_s:
                    remote(t, s).start()

    def wait():
        for s in range(N_DEV):
            to_s = [t for t in range(n) if s in dests[t]]

            @pl.when(s == me)
            def _():
                for t in to_s:
                    local(t, s).wait()
                    for src in range(N_DEV):
                        if src != s:
                            arrival(t, src).wait_recv()

            @pl.when(s != me)
            def _():
                for t in to_s:
                    remote(t, s).wait_send()

    return start, wait


N_GATHER_COPIES = 7


def _core_major_slot(dev):
    return 4 * dev[2] + 2 * dev[0] + dev[1]


def _device_of_core_major_slot(k):
    return (k % 4) * 2 + k // 4


def _two_level_gather(srcs, dsts, send_sems, recv_sems, local_sems, slots=None):
    n = len(dsts)
    x, y, c = lax.axis_index("x"), lax.axis_index("y"), lax.axis_index("c")
    me, sibling = (x, y, c), (x, y, 1 - c)
    chips = [(1 - x, y), (x, 1 - y), (1 - x, 1 - y)]

    def slot(t, dev):
        return 4 * dev[0] + 2 * dev[1] + dev[2] if slots is None or slots[t] is None else slots[t](dev)

    def copy(t, k, block, to, src=None):
        return pltpu.make_async_remote_copy(
            src_ref=dsts[t].at[slot(t, block)] if src is None else src, dst_ref=dsts[t].at[slot(t, block)],
            send_sem=send_sems.at[t, k], recv_sem=recv_sems.at[t, k], device_id=to, device_id_type=MESH)

    def local(t):
        return pltpu.make_async_copy(srcs[t], dsts[t].at[slot(t, me)], local_sems.at[t])

    def start():
        for t in range(n):
            local(t).start()
            for j, chip in enumerate(chips):
                copy(t, 1 + j, me, (*chip, c), src=srcs[t]).start()
            copy(t, 0, me, sibling, src=srcs[t]).start()

    def forward(t):
        for j, chip in enumerate(chips):
            copy(t, 1 + j, (*chip, c), me).wait_recv()
            copy(t, 4 + j, (*chip, c), sibling).start()

    def finish():
        for t in range(n):
            copy(t, 0, sibling, me).wait_recv()
            for j, chip in enumerate(chips):
                copy(t, 4 + j, (*chip, 1 - c), me).wait_recv()
            copy(t, 0, me, sibling, src=srcs[t]).wait_send()
            for j, chip in enumerate(chips):
                copy(t, 1 + j, me, (*chip, c), src=srcs[t]).wait_send()
                copy(t, 4 + j, (*chip, c), sibling).wait_send()
            local(t).wait()

    return start, forward, finish


def _hosted_gather(i, nb, forward_at, srcs, dsts, sems, slots=None):
    start, forward, finish = _two_level_gather(srcs, dsts, *sems, slots)

    def after_step():
        for t, f in enumerate(forward_at):
            @pl.when(i == min(nb - 1, int(f * nb)))
            def _():
                forward(t)

        @pl.when(i == nb - 1)
        def _():
            finish()

    return start, after_step


def _gather_scratch(n):
    return [pltpu.SemaphoreType.DMA((n, N_GATHER_COPIES)), pltpu.SemaphoreType.DMA((n, N_GATHER_COPIES)),
            pltpu.SemaphoreType.DMA((n,))]


def _gather(name, srcs):
    n = len(srcs)

    def body(*refs):
        start, forward, finish = _two_level_gather(refs[:n], refs[n:2 * n], *refs[2 * n:])
        start()
        for t in range(n):
            forward(t)
        finish()

    any_spec = pl.BlockSpec(memory_space=pl.ANY)
    return pl.pallas_call(
        body, name=name, in_specs=[any_spec] * n, out_specs=[any_spec] * n,
        out_shape=[jax.ShapeDtypeStruct((N_DEV,) + a.shape, a.dtype) for a in srcs], scratch_shapes=_gather_scratch(n),
    )(*srcs)


def _scatter_plan(blocks, dests, landing):
    return dict(blocks=list(blocks), dests=[list(dd) for dd in dests], landing=list(landing))


def _scatter_args(plan):
    return plan['blocks'] + [a for a in plan['landing'] if a is not None]


def _scatter_out_shape(plan):
    return [jax.ShapeDtypeStruct((N_DEV,) + b.shape[1:], b.dtype) for b in plan['blocks']]


def _scatter_aliases(plan, first_in, first_out):
    given = [t for t, a in enumerate(plan['landing']) if a is not None]
    return {first_in + len(plan['blocks']) + k: first_out + t for k, t in enumerate(given)}


def _scatter_ops(plan, in_refs, out_refs, sems):
    n = len(plan['blocks'])
    srcs_of = [(lambda s, r=in_refs[t], dd=plan['dests'][t]: r.at[dd.index(s)]) for t in range(n)]
    return _all_to_all(srcs_of, out_refs, *sems, dests=plan['dests'])


def _exchange_scratch(n):
    return [pltpu.SemaphoreType.DMA((n, N_DEV)), pltpu.SemaphoreType.DMA((n, N_DEV)), pltpu.SemaphoreType.DMA((n,))]


def _const_spec(shape):
    nd = len(shape)
    return pl.BlockSpec(shape, lambda i: (0,) * nd, pipeline_mode=pl.Buffered(1))


def _pool_windows(up_ext, n, forward):
    sh = (lambda k: k) if forward else (lambda k: n - k)
    s2 = up_ext + pltpu.roll(up_ext, sh(1), 0)
    t4 = s2[:, 128:]
    s4 = t4 + pltpu.roll(t4, sh(2), 0)
    t8 = s4[:, 128:]
    s8 = t8 + pltpu.roll(t8, sh(4), 0)
    t16 = s8[:, 128:]
    s16 = t16 + pltpu.roll(t16, sh(8), 0)
    return jnp.concatenate([s2[:, :128], s4[:, :128], s8[:, :128], s16], axis=1)


def _inv_count_head():
    t = jnp.arange(1, HALO + 1, dtype=F32)[:, None]
    return jnp.concatenate([jnp.broadcast_to(1.0 / jnp.minimum(t, float(w)), (HALO, 128)) for w in POOL_WINDOWS], axis=1)


def _scale_by_inv_count(v, is_first_block, inv_head):
    inv_row = jnp.concatenate([jnp.full((1, 128), 1.0 / w, F32) for w in POOL_WINDOWS], axis=1)
    head = v[0:HALO] * jnp.where(is_first_block, inv_head, inv_row)
    return jnp.concatenate([head, v[HALO:] * inv_row], axis=0)


def _lru_decay(r, a, c_l, first_row):
    a2 = a * a
    m2 = -jnp.tanh(c_l * r) * (a2 + 1.0)
    return a2, m2, jnp.where(first_row, 1.0, jnp.sqrt(m2))


def _log_sigmoid(v):
    return -(jnp.maximum(-v, 0.0) + jnp.log1p(jnp.exp(-jnp.abs(v))))


def _conv_fwd(ul_ext, cw, cb):
    return (cb + cw[3:4, :] * ul_ext + cw[2:3, :] * pltpu.roll(ul_ext, 1, 0)
            + cw[1:2, :] * pltpu.roll(ul_ext, 2, 0) + cw[0:1, :] * pltpu.roll(ul_ext, 3, 0))


def _mix_fwd(x, g_mix, w_in, wp_bd, pool_b, pool_scale, conv_w, conv_b, wg_bd, gate_b, lru_l, w_out, gather_srcs,
             gather_slots, forward_at, tb):
    t_len, d = x.shape
    nb = t_len // tb
    n_g = len(gather_srcs)

    def body(*refs):
        (x_ref, g_ref, win_ref, wp_ref, pb_ref, ps_ref, cw_ref, cb_ref, wg_ref, gb_ref, l_ref, wout_ref,
         invh_ref) = refs[:13]
        gsrc = refs[13:13 + n_g]
        h1_ref, z1_ref, proj_ref, hs_ref, cat_ref, lru_ref, dpool_ref = refs[13 + n_g:20 + n_g]
        gdst = refs[20 + n_g:20 + 2 * n_g]
        ext_ref, a_ref, b_ref, hc_ref, send_sems, recv_sems, local_sems = refs[20 + 2 * n_g:]
        i = pl.program_id(0)
        start_gather, after_step = _hosted_gather(i, nb, forward_at, gsrc, gdst, (send_sems, recv_sems, local_sems),
                                                  gather_slots)

        @pl.when(i == 0)
        def _():
            start_gather()
            ext_ref[0:HALO, :] = jnp.zeros((HALO, 1024), F32)
            hc_ref[...] = jnp.zeros_like(hc_ref)

        xv = x_ref[...]
        z, _, _ = _rms_fwd(xv, g_ref[...])
        zb = z.astype(BF16)
        z1_ref[...] = zb
        proj = _dot_nt(zb, win_ref[...])
        proj_ref[...] = proj
        ext_ref[HALO:, :] = proj[:, 0:1024]
        ug = proj[:, 1024:1536]
        n = tb + HALO
        up_ext = ext_ref[:, 0:512]
        win = _pool_windows(up_ext, n, True)[HALO:]
        dpool = _scale_by_inv_count(win, i == 0, invh_ref[...]) - proj[:, 0:512]
        dpoolb = dpool.astype(BF16)
        dpool_ref[...] = dpoolb
        q = _dot(dpoolb, wp_ref[...]) + pb_ref[...]
        y_pool = q * ps_ref[...]
        xb = _conv_fwd(ext_ref[:, 512:1024], cw_ref[...], cb_ref[...])[HALO:]
        first_row = (i * tb + lax.broadcasted_iota(jnp.int32, (tb, 1), 0)) == 0
        c_l = LRU_C * _log_sigmoid(l_ref[...])
        gp = _dot(xb.astype(BF16), wg_ref[...]) + gb_ref[...]
        r = 1.0 / (1.0 + jnp.exp(-gp[:, :512]))
        ig = _sigmoid(gp[:, 512:])
        a = jnp.exp(c_l * r)
        _, _, mult = _lru_decay(r, a, c_l, first_row)
        lru_ref[:, 0:512] = xb
        lru_ref[:, 512:1024] = r
        lru_ref[:, 1024:1536] = ig
        lru_ref[:, 1536:2048] = a
        a_ref[...] = a
        b_ref[...] = mult * (ig * xb)
        row = lax.broadcasted_iota(jnp.int32, (SUB, 512), 0)

        def group(j, hprev):
            o = pl.multiple_of(j * SUB, SUB)
            a8 = a_ref[pl.ds(o, SUB), :]
            b8 = b_ref[pl.ds(o, SUB), :]
            for sh in (1, 2, 4):
                ash = jnp.where(row >= sh, pltpu.roll(a8, sh, 0), 1.0)
                bsh = jnp.where(row >= sh, pltpu.roll(b8, sh, 0), 0.0)
                b8 = a8 * bsh + b8
                a8 = a8 * ash
            h8 = a8 * hprev + b8
            hs_ref[pl.ds(o, SUB), :] = h8
            return jnp.broadcast_to(h8[SUB - 1:SUB, :], (SUB, 512))

        def trip(k, carry):
            for u in range(SCAN_UNROLL):
                carry = group(k * SCAN_UNROLL + u, carry)
            return carry

        hc_ref[...] = lax.fori_loop(0, tb // (SUB * SCAN_UNROLL), trip, hc_ref[...])
        gel, dgel = _gelu_parts(ug)
        lru_ref[:, 2048:2560] = gel
        lru_ref[:, 2560:3072] = dgel
        y_lru = hs_ref[...] * gel
        catb = jnp.concatenate([y_pool, y_lru], axis=1).astype(BF16)
        cat_ref[...] = catb
        h1_ref[...] = xv + _dot(catb, wout_ref[...])
        ext_ref[0:HALO, :] = ext_ref[tb:tb + HALO, :]

        after_step()

    row_spec = lambda w: pl.BlockSpec((tb, w), lambda i: (i, 0))
    any_spec = pl.BlockSpec(memory_space=pl.ANY)
    smalls = [g_mix, w_in, wp_bd, pool_b, pool_scale, conv_w, conv_b, wg_bd, gate_b, lru_l, w_out, _inv_count_head()]
    return pl.pallas_call(
        body, name="mix_fwd", grid=(nb,),
        in_specs=[row_spec(d)] + [_const_spec(s.shape) for s in smalls] + [any_spec] * n_g,
        out_specs=[row_spec(d), row_spec(d), row_spec(1536), row_spec(512), row_spec(1024), row_spec(3072), row_spec(512)]
        + [any_spec] * n_g,
        out_shape=[jax.ShapeDtypeStruct((t_len, d), F32), jax.ShapeDtypeStruct((t_len, d), BF16),
                   jax.ShapeDtypeStruct((t_len, 1536), F32), jax.ShapeDtypeStruct((t_len, 512), F32),
                   jax.ShapeDtypeStruct((t_len, 1024), BF16), jax.ShapeDtypeStruct((t_len, 3072), F32),
                   jax.ShapeDtypeStruct((t_len, 512), BF16)]
        + [jax.ShapeDtypeStruct((N_DEV,) + s.shape, s.dtype) for s in gather_srcs],
        scratch_shapes=[pltpu.VMEM((tb + HALO, 1024), F32), pltpu.VMEM((tb, 512), F32), pltpu.VMEM((tb, 512), F32),
                        pltpu.VMEM((SUB, 512), F32)] + _gather_scratch(n_g),
        compiler_params=_params(),
    )(x, *smalls, *gather_srcs)


def _mlp_fwd(h1, g_mlp, w_up, w_down, tb):
    t_len, d = h1.shape
    nb = t_len // tb
    n_chunk, _, fc = w_up.shape

    def body(h1_ref, g_ref, wup_ref, wdn_ref, h2_ref, z2_ref, up_ref):
        xv = h1_ref[...]
        z, _, _ = _rms_fwd(xv, g_ref[...])
        zb = z.astype(BF16)
        z2_ref[...] = zb
        acc = xv
        for c in range(n_chunk):
            u = _dot(zb, wup_ref[c])
            up_ref[:, c * fc:(c + 1) * fc] = u.astype(BF16)
            act = jnp.square(jnp.maximum(u, 0.0)).astype(BF16)
            acc = acc + _dot(act, wdn_ref[c * fc:(c + 1) * fc, :])
        h2_ref[...] = acc

    row_spec = lambda w: pl.BlockSpec((tb, w), lambda i: (i, 0))
    return pl.pallas_call(
        body, name="mlp_fwd", grid=(nb,),
        in_specs=[row_spec(d), _const_spec(g_mlp.shape), _const_spec(w_up.shape), _const_spec(w_down.shape)],
        out_specs=[row_spec(d), row_spec(d), row_spec(n_chunk * fc)],
        out_shape=[jax.ShapeDtypeStruct((t_len, d), F32), jax.ShapeDtypeStruct((t_len, d), BF16),
                   jax.ShapeDtypeStruct((t_len, n_chunk * fc), BF16)],
        compiler_params=_params(),
    )(h1, g_mlp, w_up, w_down)


def _ple(h2, p, target, g_ple, w_gate, b_gate, w_proj, g_final, tb):
    t_len, d = h2.shape
    nb = t_len // tb
    pd = p.shape[1]

    def body(h2_ref, p_ref, tgt_ref, g_ref, wg_ref, bg_ref, wp_ref, gf_ref,
             dh2_ref, vec_ref, dwg_out, dwp_out, dwg_acc, dwp_acc, dwg_stage, dwp_stage):
        i = pl.program_id(0)

        @pl.when(i == 0)
        def _():
            vec_ref[...] = jnp.zeros_like(vec_ref)
            dwg_acc[...] = jnp.zeros_like(dwg_acc)
            dwp_acc[...] = jnp.zeros_like(dwp_acc)

        h2 = h2_ref[...]
        g2 = g_ref[...]
        z3, xh2, r2 = _rms_fwd(h2, g2)
        z3b = z3.astype(BF16)
        gate = _sigmoid(_dot(z3b, wg_ref[...]) + bg_ref[...])
        pb = p_ref[...].astype(BF16)
        pp = _dot(pb, wp_ref[...])
        h3 = h2 + gate * pp
        gf = gf_ref[...]
        y, xh3, r3 = _rms_fwd(h3, gf)
        err = y - tgt_ref[...]
        loss_rows = jnp.mean(err * err, axis=-1, keepdims=True)
        dy = err * (1.0 / d)
        dh3 = _rms_bwd(xh3, r3, gf, dy)
        dgl = (dh3 * pp) * (gate * (1.0 - gate))
        dpp = dh3 * gate
        dglb = dgl.astype(BF16)
        dwg_acc[...] += _dot_tn(z3b, dglb)
        dwp_acc[...] += _dot_tn(pb, dpp.astype(BF16))
        dz3 = _dot_nt(dglb, wg_ref[...])
        dh2_ref[...] = dh3 + _rms_bwd(xh2, r2, g2, dz3)
        vec_ref[0:1, :] += _colsum(dgl)
        vec_ref[1:2, :] += _colsum(dz3 * xh2)
        vec_ref[2:3, :] += _colsum(dy * xh3)
        vec_ref[3:4, :] += 0.5 * jnp.sum(loss_rows)

        @pl.when(i == nb - 1)
        def _():
            dwg_stage[...] = dwg_acc[...].astype(BF16)
            dwp_stage[...] = dwp_acc[...].astype(BF16)
            pltpu.sync_copy(dwg_stage, dwg_out)
            pltpu.sync_copy(dwp_stage, dwp_out)

    row_spec = lambda w: pl.BlockSpec((tb, w), lambda i: (i, 0))
    any_spec = pl.BlockSpec(memory_space=pl.ANY)
    smalls = [g_ple, w_gate, b_gate, w_proj, g_final]
    return pl.pallas_call(
        body, name="ple_fwd_bwd", grid=(nb,),
        in_specs=[row_spec(d), row_spec(pd), row_spec(d)] + [_const_spec(s.shape) for s in smalls],
        out_specs=[row_spec(d), pl.BlockSpec((8, d), lambda i: (0, 0)), any_spec, any_spec],
        out_shape=[jax.ShapeDtypeStruct((t_len, d), F32), jax.ShapeDtypeStruct((8, d), F32),
                   jax.ShapeDtypeStruct(w_gate.shape, BF16), jax.ShapeDtypeStruct(w_proj.shape, BF16)],
        scratch_shapes=[pltpu.VMEM(w_gate.shape, F32), pltpu.VMEM(w_proj.shape, F32), pltpu.VMEM(w_gate.shape, BF16),
                        pltpu.VMEM(w_proj.shape, BF16)],
        compiler_params=_params(),
    )(h2, p, target, *smalls)


def _mlp_bwd_part(part, n_part, dh2, z2, up, w_up, w_down, dz2_prev, h1, g_mlp, scatter, tb):
    t_len, d = dh2.shape
    nb = t_len // tb
    n_chunk_all, _, fc = w_up.shape
    n_chunk = n_chunk_all // n_part
    first, last = part == 0, part == n_part - 1

    def body(*refs):
        refs = list(refs)
        dh2_ref, z2_ref, up_ref, wup_ref, wdn_ref = refs[:5]
        del refs[:5]
        dzp_ref = None if first else refs.pop(0)
        h1_ref, g_ref = (refs.pop(0), refs.pop(0)) if last else (None, None)
        scatter_in = [refs.pop(0) for _ in _scatter_args(scatter)]
        out_ref = refs.pop(0)
        vec_ref = refs.pop(0) if last else None
        dwup_out, dwdn_out = refs.pop(0), refs.pop(0)
        scatter_out = [refs.pop(0) for _ in scatter['blocks']]
        dwup_acc, dwdn_acc, up_stage, dn_stage = refs[:4]
        if scatter['blocks']:
            start_scatter, wait_scatter = _scatter_ops(scatter, scatter_in, scatter_out, refs[4:])
        i = pl.program_id(0)

        @pl.when(i == 0)
        def _():
            if scatter['blocks']:
                start_scatter()
            dwup_acc[...] = jnp.zeros_like(dwup_acc)
            dwdn_acc[...] = jnp.zeros_like(dwdn_acc)
            if last:
                vec_ref[...] = jnp.zeros_like(vec_ref)

        dh2 = dh2_ref[...]
        dh2b = dh2.astype(BF16)
        z2b = z2_ref[...]
        dz2 = jnp.zeros((tb, d), F32) if first else dzp_ref[...]
        for c in range(n_chunk):
            u = up_ref[:, c * fc:(c + 1) * fc].astype(F32)
            ur = jnp.maximum(u, 0.0)
            dact = _dot_nt(dh2b, wdn_ref[c * fc:(c + 1) * fc, :])
            dupb = (dact * (2.0 * ur)).astype(BF16)
            dwdn_acc[c * fc:(c + 1) * fc, :] += _dot_tn((ur * ur).astype(BF16), dh2b)
            dwup_acc[c] += _dot_tn(z2b, dupb)
            dz2 = dz2 + _dot_nt(dupb, wup_ref[c])
        if last:
            g = g_ref[...]
            _, xh, r = _rms_fwd(h1_ref[...], g)
            out_ref[...] = dh2 + _rms_bwd(xh, r, g, dz2)
            vec_ref[0:1, :] += _colsum(dz2 * xh)
        else:
            out_ref[...] = dz2

        @pl.when(i == nb - 1)
        def _():
            for c in range(n_chunk):
                up_stage[...] = dwup_acc[c].astype(BF16)
                dn_stage[...] = dwdn_acc[c * fc:(c + 1) * fc, :].astype(BF16)
                pltpu.sync_copy(up_stage, dwup_out.at[c])
                pltpu.sync_copy(dn_stage, dwdn_out.at[c])
            if scatter['blocks']:
                wait_scatter()

    row_spec = lambda w: pl.BlockSpec((tb, w), lambda i: (i, 0))
    any_spec = pl.BlockSpec(memory_space=pl.ANY)
    args = [dh2, z2, up, w_up, w_down]
    in_specs = [row_spec(d), row_spec(d), pl.BlockSpec((tb, n_chunk * fc), lambda i: (i, part)),
                pl.BlockSpec((n_chunk, d, fc), lambda i: (part, 0, 0), pipeline_mode=pl.Buffered(1)),
                pl.BlockSpec((n_chunk * fc, d), lambda i: (part, 0), pipeline_mode=pl.Buffered(1))]
    if not first:
        args.append(dz2_prev)
        in_specs.append(row_spec(d))
    if last:
        args += [h1, g_mlp]
        in_specs += [row_spec(d), _const_spec(g_mlp.shape)]
    n_in = len(args)
    args += _scatter_args(scatter)
    in_specs += [any_spec] * len(_scatter_args(scatter))
    out_specs = [row_spec(d)]
    out_shape = [jax.ShapeDtypeStruct((t_len, d), F32)]
    if last:
        out_specs.append(pl.BlockSpec((8, d), lambda i: (0, 0)))
        out_shape.append(jax.ShapeDtypeStruct((8, d), F32))
    out_specs += [any_spec, any_spec]
    out_shape += [jax.ShapeDtypeStruct((n_chunk, d, fc), BF16), jax.ShapeDtypeStruct((n_chunk, fc, d), BF16)]
    n_out = len(out_shape)
    out_specs += [any_spec] * len(scatter['blocks'])
    out_shape += _scatter_out_shape(scatter)
    return pl.pallas_call(
        body, name=f"mlp_bwd_{part}", grid=(nb,), in_specs=in_specs, out_specs=out_specs, out_shape=out_shape,
        scratch_shapes=[pltpu.VMEM((n_chunk, d, fc), F32), pltpu.VMEM((n_chunk * fc, d), F32),
                        pltpu.VMEM((d, fc), BF16), pltpu.VMEM((fc, d), BF16)]
        + (_exchange_scratch(len(scatter['blocks'])) if scatter['blocks'] else []),
        input_output_aliases=_scatter_aliases(scatter, n_in, n_out), compiler_params=_params(),
    )(*args)


def _mix_bwd(dh1, proj, hs, lru_saved, dpool_saved, wp_bd, pool_b, pool_scale, conv_w, wg_bd, lru_l, w_out, scatter, tb):
    t_len, d = dh1.shape
    nb = t_len // tb
    n_s = len(scatter['blocks'])
    scatter_args = _scatter_args(scatter)

    def body(*refs):
        refs = list(refs)
        (dh1_ref, ul_ref, hs_ref, hsh_ref, lru_ref, dpool_ref,
         wp_ref, pb_ref, ps_ref, cw_ref, wg_ref, l_ref, wout_ref, invh_ref) = refs[:14]
        del refs[:14]
        scatter_in = refs[:len(scatter_args)]
        del refs[:len(scatter_args)]
        dproj_ref, v512_ref, dpw_ref, dga_ref, dgx_ref = refs[:5]
        recv = refs[5:5 + n_s]
        (dwp_acc, dwg_acc, v1024_ref, b_ref, gs_ref, ehead_ref, dxbhead_ref, hc_ref,
         send_sems, recv_sems, local_sems) = refs[5 + n_s:]
        i = pl.program_id(0)
        tbk = nb - 1 - i

        start_scatter, wait_scatter = _scatter_ops(scatter, scatter_in, recv, (send_sems, recv_sems, local_sems))

        @pl.when(i == 0)
        def _():
            start_scatter()
            for ref in (v512_ref, v1024_ref, dwp_acc, dwg_acc, ehead_ref, dxbhead_ref, hc_ref):
                ref[...] = jnp.zeros_like(ref)

        dcat = _dot_nt(dh1_ref[...].astype(BF16), wout_ref[...])

        has_prev = (tbk > 0).astype(F32)
        n = tb + HALO
        inv_head = invh_ref[...]

        dpoolb = dpool_ref[...]
        q = _dot(dpoolb, wp_ref[...]) + pb_ref[...]
        dyp = dcat[:, 0:512]
        dq = dyp * ps_ref[...]
        dqb = dq.astype(BF16)
        v512_ref[0:1, :] += _colsum(dyp * q)
        v512_ref[1:2, :] += _colsum(dq)
        dwp_acc[...] += _dot_tn(dpoolb, dqb)
        dd = _dot_nt(dqb, wp_ref[...])
        e = _scale_by_inv_count(dd, tbk == 0, inv_head)
        e_ext = jnp.concatenate([e, ehead_ref[...]], axis=0)
        du_pool = _pool_windows(e_ext, n, False)[0:tb] - dd
        ehead_ref[...] = e[0:HALO]

        gel, dgel = lru_ref[:, 2048:2560], lru_ref[:, 2560:3072]
        hsv = hs_ref[...]
        dcl = dcat[:, 512:1024]
        dhs = dcl * gel
        dug = dcl * hsv * dgel
        cw = cw_ref[...]
        xb, r, ig, a = lru_ref[:, 0:512], lru_ref[:, 512:1024], lru_ref[:, 1024:1536], lru_ref[:, 1536:2048]
        first_row = (tbk * tb + lax.broadcasted_iota(jnp.int32, (tb, 1), 0)) == 0
        c_l = LRU_C * _log_sigmoid(l_ref[...])
        a2, m2, mult = _lru_decay(r, a, c_l, first_row)
        b_ref[...] = dhs
        row = lax.broadcasted_iota(jnp.int32, (SUB, 512), 0)

        def group(jj, hnext):
            o = pl.multiple_of((tb // SUB - 1 - jj) * SUB, SUB)
            a8 = lru_ref[pl.ds(o, SUB), 1536:2048]
            d8 = b_ref[pl.ds(o, SUB), :]
            b8 = a8 * d8
            for sh in (1, 2, 4):
                ash = jnp.where(row < SUB - sh, pltpu.roll(a8, SUB - sh, 0), 1.0)
                bsh = jnp.where(row < SUB - sh, pltpu.roll(b8, SUB - sh, 0), 0.0)
                b8 = a8 * bsh + b8
                a8 = a8 * ash
            h8 = a8 * hnext + b8
            gs_ref[pl.ds(o, SUB), :] = d8 + jnp.where(row < SUB - 1, pltpu.roll(h8, SUB - 1, 0), hnext)
            return jnp.broadcast_to(h8[0:1, :], (SUB, 512))

        def trip(k, carry):
            for u in range(SCAN_UNROLL):
                carry = group(k * SCAN_UNROLL + u, carry)
            return carry

        hc_ref[...] = lax.fori_loop(0, tb // (SUB * SCAN_UNROLL), trip, hc_ref[...])
        gsum = gs_ref[...]
        hs_ext = jnp.concatenate([hsh_ref[...] * has_prev, hsv], axis=0)
        hprev = pltpu.roll(hs_ext, 1, 0)[SUB:]
        da = gsum * hprev
        dmult = jnp.where(first_row, 0.0, gsum * (ig * xb))
        di = gsum * mult * xb
        dxb = gsum * mult * ig
        dla = da * a - dmult * a2 * lax.rsqrt(m2)
        dr = dla * c_l
        v512_ref[3:4, :] += _colsum(dla * r)
        dgp = jnp.concatenate([dr * r * (1.0 - r), di * ig * (1.0 - ig)], axis=1)
        dgpb = dgp.astype(BF16)
        v1024_ref[0:1, :] += _colsum(dgp)
        dwg_acc[...] += _dot_tn(xb.astype(BF16), dgpb)
        dxb = dxb + _dot_nt(dgpb, wg_ref[...])
        n8 = tb + SUB
        dxb_ext = jnp.concatenate([dxb, dxbhead_ref[...]], axis=0)
        ul = ul_ref[...]
        du_lru = cw[3:4, :] * dxb
        v512_ref[7:8, :] += _colsum(dxb * ul)
        for j in range(1, 4):
            ahead = pltpu.roll(dxb_ext, n8 - j, 0)[0:tb]
            du_lru = du_lru + cw[3 - j:4 - j, :] * ahead
            v512_ref[4 + (3 - j):5 + (3 - j), :] += _colsum(ahead * ul)
        dxbhead_ref[...] = dxb[0:SUB]
        v512_ref[2:3, :] += _colsum(dxb)

        dproj_ref[...] = jnp.concatenate([du_pool, du_lru, dug], axis=1).astype(BF16)

        @pl.when(i == nb - 1)
        def _():
            v512_ref[3:4, :] = v512_ref[3:4, :] * (LRU_C * _sigmoid(-l_ref[...]))
            v512_ref[8:9, :] = v1024_ref[0:1, 0:512]
            v512_ref[9:10, :] = v1024_ref[0:1, 512:1024]
            for g in range(N_POOL_GROUPS):
                dpw_ref[g * 128:(g + 1) * 128, :] = dwp_acc[g * 128:(g + 1) * 128, g * 128:(g + 1) * 128]
            odd_head = (lax.broadcasted_iota(jnp.int32, (512, 128), 0) // 64) % 2 == 1
            for out_ref, col0 in ((dga_ref, 0), (dgx_ref, 512)):
                pairs = jnp.concatenate([dwg_acc[128 * k:128 * (k + 1), col0 + 128 * k:col0 + 128 * (k + 1)]
                                         for k in range(LRU_HEADS // 2)], axis=0)
                out_ref[...] = jnp.where(odd_head, pltpu.roll(pairs, 64, 1), pairs)[:, 0:64]
            wait_scatter()

    rev = lambda w: pl.BlockSpec((tb, w), lambda i: (nb - 1 - i, 0))
    halo = lambda rows, w: pl.BlockSpec((rows, w), lambda i: (jnp.maximum((nb - 1 - i) * (tb // rows) - 1, 0), 0))
    any_spec = pl.BlockSpec(memory_space=pl.ANY)
    smalls = [wp_bd, pool_b, pool_scale, conv_w, wg_bd, lru_l, w_out, _inv_count_head()]
    lru_third = pl.BlockSpec((tb, 512), lambda i: (nb - 1 - i, 1))
    return pl.pallas_call(
        body, name="mix_bwd", grid=(nb,),
        in_specs=[rev(d), lru_third, rev(512), halo(SUB, 512), rev(3072), rev(512)]
        + [_const_spec(s.shape) for s in smalls] + [any_spec] * len(scatter_args),
        out_specs=[rev(1536), pl.BlockSpec((16, 512), lambda i: (0, 0)), pl.BlockSpec((512, 128), lambda i: (0, 0)),
                   pl.BlockSpec((512, 64), lambda i: (0, 0)), pl.BlockSpec((512, 64), lambda i: (0, 0))]
        + [any_spec] * n_s,
        out_shape=[jax.ShapeDtypeStruct((t_len, 1536), BF16), jax.ShapeDtypeStruct((16, 512), F32),
                   jax.ShapeDtypeStruct((512, 128), F32), jax.ShapeDtypeStruct((512, 64), F32),
                   jax.ShapeDtypeStruct((512, 64), F32)]
        + _scatter_out_shape(scatter),
        scratch_shapes=[pltpu.VMEM(wp_bd.shape, F32), pltpu.VMEM(wg_bd.shape, F32), pltpu.VMEM((8, 1024), F32),
                        pltpu.VMEM((tb, 512), F32), pltpu.VMEM((tb, 512), F32), pltpu.VMEM((HALO, 512), F32),
                        pltpu.VMEM((SUB, 512), F32), pltpu.VMEM((SUB, 512), F32)]
        + _exchange_scratch(n_s),
        input_output_aliases=_scatter_aliases(scatter, 6 + len(smalls), 5), compiler_params=_params(),
    )(dh1, proj, hs, hs, lru_saved, dpool_saved, *smalls, *scatter_args)


def _wgrad(name, a, b, whole, by_rows, tb):
    t_len, m = a.shape
    n = b.shape[1]
    nb = t_len // tb
    n_w, n_r = len(whole), len(by_rows)
    n_small = n_w + n_r

    def body(*refs):
        a_ref, b_ref = refs[:2]
        small_in = refs[2:2 + n_small]
        out_ref = refs[2 + n_small]
        small_out = refs[3 + n_small:3 + 2 * n_small]
        acc_ref, stage_ref = refs[3 + 2 * n_small:5 + 2 * n_small]
        rest = refs[5 + 2 * n_small:]
        if n_small:
            send_partials, reduce_and_send_sums, finish_small = _small_allreduce(
                small_in[:n_w], small_in[n_w:], small_out[:n_w], small_out[n_w:], rest[:n_w], rest[n_w:n_small],
                rest[n_small:n_small + n_r], *rest[n_small + n_r:])
        i = pl.program_id(0)

        @pl.when(i == 0)
        def _():
            if n_small:
                send_partials()
            acc_ref[...] = jnp.zeros_like(acc_ref)

        acc_ref[...] += _dot_tn(a_ref[...], b_ref[...].astype(BF16))

        if n_small:
            @pl.when(i == nb // 2)
            def _():
                reduce_and_send_sums()

        @pl.when(i == nb - 1)
        def _():
            stage_ref[...] = acc_ref[...].astype(BF16)
            pltpu.sync_copy(stage_ref, out_ref)
            if n_small:
                finish_small()

    small = list(whole) + list(by_rows)
    vmem_spec = pl.BlockSpec(memory_space=pltpu.VMEM)
    res = pl.pallas_call(
        body, name=name, grid=(nb,),
        in_specs=[pl.BlockSpec((tb, m), lambda i: (i, 0)), pl.BlockSpec((tb, n), lambda i: (i, 0))] + [vmem_spec] * n_small,
        out_specs=[pl.BlockSpec(memory_space=pl.ANY)] + [vmem_spec] * n_small,
        out_shape=[jax.ShapeDtypeStruct((m, n), BF16)] + [jax.ShapeDtypeStruct(s_.shape, F32) for s_ in small],
        scratch_shapes=[pltpu.VMEM((m, n), F32), pltpu.VMEM((m, n), BF16)]
        + (_small_allreduce_scratch(whole, by_rows) if n_small else []),
        compiler_params=_params(),
    )(a, b, *small)
    return res[0], res[1:1 + n_w], res[1 + n_w:]


def _in_bwd(dproj, x, dh1, g_mix, w_in, scatter, tb):
    t_len, d = x.shape
    nb = t_len // tb
    n_s = len(scatter['blocks'])
    scatter_args = _scatter_args(scatter)

    def body(*refs):
        dproj_ref, x_ref, dh1_ref, g_ref, win_ref = refs[:5]
        scatter_in = refs[5:5 + len(scatter_args)]
        dx_ref, vec_ref = refs[5 + len(scatter_args):7 + len(scatter_args)]
        recv = refs[7 + len(scatter_args):7 + len(scatter_args) + n_s]
        vec_acc, send_sems, recv_sems, local_sems, vec_land, small_send, small_recv = refs[7 + len(scatter_args) + n_s:]
        start_scatter, wait_scatter = _scatter_ops(scatter, scatter_in, recv, (send_sems, recv_sems, local_sems))
        send_partials, reduce_and_send_sums, finish_small = _small_allreduce(
            [vec_acc], [], [vec_ref], [], [vec_land], [], [], small_send, small_recv)
        i = pl.program_id(0)

        @pl.when(i == 0)
        def _():
            start_scatter()
            vec_acc[...] = jnp.zeros_like(vec_acc)

        dz1 = _dot(dproj_ref[...], win_ref[...])
        g = g_ref[...]
        _, xh, rr = _rms_fwd(x_ref[...], g)
        dx_ref[...] = dh1_ref[...] + _rms_bwd(xh, rr, g, dz1)
        vec_acc[0:1, :] += _colsum(dz1 * xh)

        @pl.when(i == nb - 1)
        def _():
            send_partials()
            reduce_and_send_sums()
            finish_small()
            wait_scatter()

    row_spec = lambda w: pl.BlockSpec((tb, w), lambda i: (i, 0))
    any_spec = pl.BlockSpec(memory_space=pl.ANY)
    return pl.pallas_call(
        body, name="in_bwd", grid=(nb,),
        in_specs=[row_spec(dproj.shape[1]), row_spec(d), row_spec(d), _const_spec(g_mix.shape), _const_spec(w_in.shape)]
        + [any_spec] * len(scatter_args),
        out_specs=[row_spec(d), pl.BlockSpec((8, d), lambda i: (0, 0))] + [any_spec] * n_s,
        out_shape=[jax.ShapeDtypeStruct((t_len, d), F32), jax.ShapeDtypeStruct((8, d), F32)] + _scatter_out_shape(scatter),
        scratch_shapes=[pltpu.VMEM((8, d), F32)] + _exchange_scratch(n_s)
        + _small_allreduce_scratch([jax.ShapeDtypeStruct((8, d), F32)], []),
        input_output_aliases=_scatter_aliases(scatter, 5, 2), compiler_params=_params(),
    )(dproj, x, dh1, g_mix, w_in, *scatter_args)


def _small_allreduce(whole_in, rows_in, whole_out, rows_out, whole_land, rows_land, rows_sum, send_sems, recv_sems):
    n_w, n_r = len(whole_in), len(rows_in)
    per = [r.shape[0] // N_DEV for r in rows_in]
    me = _my_index()

    def dev(s):
        return (s // 4, (s // 2) % 2, s % 2)

    def rows_of(t, s):
        return pl.ds(s * per[t], per[t])

    def mine(t):
        return pl.ds(pl.multiple_of(me * per[t], 8), per[t])

    def partial(t, s, slot):
        if t < n_w:
            src, dst = whole_in[t], whole_land[t]
        else:
            src, dst = rows_in[t - n_w].at[rows_of(t - n_w, s)], rows_land[t - n_w]
        return pltpu.make_async_remote_copy(
            src_ref=src, dst_ref=dst.at[slot], send_sem=send_sems.at[t, s], recv_sem=recv_sems.at[t, slot],
            device_id=dev(s), device_id_type=MESH)

    def summed(t, s, rows, slot):
        return pltpu.make_async_remote_copy(
            src_ref=rows_sum[t].at[rows], dst_ref=rows_sum[t].at[rows], send_sem=send_sems.at[n_w + n_r + t, s],
            recv_sem=recv_sems.at[n_w + n_r + t, slot], device_id=dev(s), device_id_type=MESH)

    def send_partials():
        for s in range(N_DEV):
            @pl.when(s != me)
            def _():
                for t in range(n_w + n_r):
                    partial(t, s, me).start()
        for t in range(n_w):
            whole_land[t][me] = whole_in[t][...]
        for t in range(n_r):
            rows_land[t][me] = rows_in[t][mine(t), :]

    def reduce_and_send_sums():
        for s in range(N_DEV):
            @pl.when(s != me)
            def _():
                for t in range(n_w + n_r):
                    partial(t, s, s).wait_recv()
        for t in range(n_w):
            total = whole_land[t][0]
            for s in range(1, N_DEV):
                total = total + whole_land[t][s]
            whole_out[t][...] = total
        for t in range(n_r):
            total = rows_land[t][0]
            for s in range(1, N_DEV):
                total = total + rows_land[t][s]
            rows_sum[t][mine(t), :] = total
        for s in range(N_DEV):
            @pl.when(s != me)
            def _():
                for t in range(n_r):
                    summed(t, s, mine(t), me).start()

    def finish():
        for s in range(N_DEV):
            @pl.when(s != me)
            def _():
                for t in range(n_r):
                    summed(t, s, rows_of(t, s), s).wait_recv()
                    summed(t, s, mine(t), me).wait_send()
                for t in range(n_w + n_r):
                    partial(t, s, me).wait_send()
        for t in range(n_r):
            rows_out[t][...] = rows_sum[t][...]

    return send_partials, reduce_and_send_sums, finish


def _small_allreduce_scratch(whole, by_rows):
    n_sem = len(whole) + 2 * len(by_rows)
    return ([pltpu.VMEM((N_DEV,) + a.shape, F32) for a in whole]
            + [pltpu.VMEM((N_DEV, a.shape[0] // N_DEV, a.shape[1]), F32) for a in by_rows]
            + [pltpu.VMEM(a.shape, F32) for a in by_rows]
            + [pltpu.SemaphoreType.DMA((n_sem, N_DEV)), pltpu.SemaphoreType.DMA((n_sem, N_DEV))])


def _adam_update(g, w, m, v):
    m_new = ADAM_B1 * m + (1.0 - ADAM_B1) * g
    v_new = ADAM_B2 * v + (1.0 - ADAM_B2) * jnp.square(g)
    m_hat = m_new / (1.0 - ADAM_B1 ** ADAM_STEP)
    v_hat = v_new / (1.0 - ADAM_B2 ** ADAM_STEP)
    return -ADAM_LR * (m_hat / (jnp.sqrt(v_hat) + ADAM_EPS) + ADAM_WD * w), m_new, v_new


def _adamw_groups(name, groups):
    n = len(groups)

    def body(*refs):
        for k in range(n):
            g_ref, w_ref, m_ref, v_ref = refs[4 * k:4 * k + 4]
            g_out, d_out, m_out, v_out = refs[4 * n + 4 * k:4 * n + 4 * k + 4]
            if len(g_ref.shape) == 3:
                g = g_ref[0].astype(F32)
                for s in range(1, g_ref.shape[0]):
                    g = g + g_ref[s].astype(F32)
            else:
                g = g_ref[...]
            g_out[...] = g
            d_out[...], m_out[...], v_out[...] = _adam_update(g, w_ref[...], m_ref[...], v_ref[...])

    flat = [a for grp in groups for a in grp]
    out = pl.pallas_call(
        body, name=name, out_shape=[jax.ShapeDtypeStruct(grp[1].shape, F32) for grp in groups for _ in range(4)],
        compiler_params=pltpu.CompilerParams(vmem_limit_bytes=VMEM_LIMIT))(*flat)
    return [out[4 * k:4 * k + 4] for k in range(n)]


def _adamw(name, parts, w, m, v, row_block):
    n_src, rows, cols = parts.shape
    rb = min(row_block, rows)

    def body(p_ref, w_ref, m_ref, v_ref, g_out, d_out, m_out, v_out):
        g = p_ref[0].astype(F32)
        for s in range(1, n_src):
            g = g + p_ref[s].astype(F32)
        g_out[...] = g
        d_out[...], m_out[...], v_out[...] = _adam_update(g, w_ref[...], m_ref[...], v_ref[...])

    spec = pl.BlockSpec((rb, cols), lambda i: (i, 0))
    return pl.pallas_call(
        body, name=name, grid=(rows // rb,),
        in_specs=[pl.BlockSpec((n_src, rb, cols), lambda i: (0, i, 0)), spec, spec, spec],
        out_specs=[spec] * 4, out_shape=[jax.ShapeDtypeStruct((rows, cols), F32)] * 4,
        compiler_params=pltpu.CompilerParams(dimension_semantics=("parallel",), vmem_limit_bytes=VMEM_LIMIT),
    )(parts, w, m, v)


def _block_diag(blocks):
    g, a, b = blocks.shape
    eye = jnp.eye(g, dtype=blocks.dtype)
    return (eye[:, None, :, None] * blocks[:, :, None, :]).reshape(g * a, g * b)


def kernel(x, p, norm_mix_g, w_in, pool_w, pool_b, pool_scale, conv_w, conv_b, gate_a_w, gate_a_b, gate_x_w, gate_x_b, lru_L, w_out, norm_mlp_g, w_up, w_down, norm_ple_g, w_ple_gate, b_ple_gate, w_ple_proj, norm_final_g, loss_target, m_norm_mix_g, m_w_in, m_pool_w, m_pool_b, m_pool_scale, m_conv_w, m_conv_b, m_gate_a_w, m_gate_a_b, m_gate_x_w, m_gate_x_b, m_lru_L, m_w_out, m_norm_mlp_g, m_w_up, m_w_down, m_norm_ple_g, m_w_ple_gate, m_b_ple_gate, m_w_ple_proj, m_norm_final_g, v_norm_mix_g, v_w_in, v_pool_w, v_pool_b, v_pool_scale, v_conv_w, v_conv_b, v_gate_a_w, v_gate_a_b, v_gate_x_w, v_gate_x_b, v_lru_L, v_w_out, v_norm_mlp_g, v_w_up, v_w_down, v_norm_ple_g, v_w_ple_gate, v_b_ple_gate, v_w_ple_proj, v_norm_final_g):
    t_len, d = x.shape[1], x.shape[2]
    tbs = {k: min(v, t_len) for k, v in TIME_BLOCKS.items()}
    me = _my_index()

    win_g, wout_g, convw_g = _gather("gather_mixer_weights",
                                     [w_in[0].T.astype(BF16), w_out[0].astype(BF16), conv_w[0]])
    w_in_f = win_g.reshape(-1, d)
    conv_w_f = jnp.transpose(convw_g, (1, 0, 2)).reshape(convw_g.shape[1], -1)
    wp_bd = _block_diag(pool_w[0]).astype(BF16)
    wg_bd = jnp.concatenate([_block_diag(gate_a_w[0]), _block_diag(gate_x_w[0])], axis=1).astype(BF16)
    gate_b2 = jnp.concatenate([gate_a_b.reshape(1, -1), gate_x_b.reshape(1, -1)], axis=1)
    mixer_small = (norm_mix_g, w_in_f, wp_bd, pool_b.reshape(1, -1), pool_scale, conv_w_f, conv_b, wg_bd, gate_b2, lru_L,
                   wout_g.reshape(-1, d))

    x2 = x[0]
    later = [w_up[0].astype(BF16), w_down[0].astype(BF16), w_ple_gate[0].astype(BF16), w_ple_proj[0].astype(BF16)]
    h1, z1, proj, hs, cat, lru_saved, dpool_saved, wup_g, wdn_g, wgate_g, wproj_g = _mix_fwd(
        x2, *mixer_small, later, [_core_major_slot, _core_major_slot, None, None], GATHER_FORWARD_AT, tbs['mix_fwd'])
    w_down_f = wdn_g.reshape(-1, d)
    w_proj_f = jnp.transpose(wproj_g, (1, 0, 2)).reshape(wproj_g.shape[1], -1)
    h2, z2, up = _mlp_fwd(h1, norm_mlp_g, wup_g, w_down_f, tbs['mlp_fwd'])
    dh2, ple_vec, dw_gate, dw_proj = _ple(h2, p[0, 0], loss_target[0], norm_ple_g, wgate_g.reshape(-1, d), b_ple_gate,
                                          w_proj_f, norm_final_g.reshape(1, -1), tbs['ple'])
    everyone = list(range(N_DEV))
    n_proj = w_ple_proj.shape[2]
    dz2_0, dw_up_0, dw_down_0 = _mlp_bwd_part(
        0, MLP_BWD_SPLIT, dh2, z2, up, wup_g, w_down_f, None, h1, norm_mlp_g, _scatter_plan([], [], []), tbs['mlp_bwd'])
    half = N_DEV // MLP_BWD_SPLIT
    south = [_device_of_core_major_slot(k) for k in range(half)]
    north = [_device_of_core_major_slot(k) for k in range(half, N_DEV)]
    scatter = _scatter_plan(
        [dw_up_0, dw_down_0, dw_gate.reshape(N_DEV, -1, d), jnp.transpose(dw_proj.reshape(-1, N_DEV, n_proj), (1, 0, 2))],
        [south, south, everyone, everyone], [None, None, None, None])
    dh1, mlp_vec, dw_up_1, dw_down_1, recv_up, recv_down, recv_gate, recv_proj = _mlp_bwd_part(
        1, MLP_BWD_SPLIT, dh2, z2, up, wup_g, w_down_f, dz2_0, h1, norm_mlp_g, scatter, tbs['mlp_bwd'])
    dw_out, _, _ = _wgrad("wgrad_out", cat, dh1, [], [], tbs['wgrad_out'])
    scatter = _scatter_plan([dw_up_1, dw_down_1, dw_out.reshape(N_DEV, -1, d)], [north, north, everyone],
                            [recv_up, recv_down, None])
    dproj, v512, dpw, dga, dgx, recv_up, recv_down, recv_out = _mix_bwd(
        dh1, proj, hs, lru_saved, dpool_saved, wp_bd, pool_b.reshape(1, -1), pool_scale, conv_w_f, wg_bd, lru_L, wout_g.reshape(-1, d),
        scatter, tbs['mix_bwd'])
    rows1024 = jnp.concatenate([jnp.zeros((1, d), F32), mlp_vec[0:1], ple_vec[1:2], ple_vec[0:1], ple_vec[2:4],
                                jnp.zeros((2, d), F32)], axis=0)
    dw_in_t, (rows1024, rows512), (g_pool_w, g_gate_a_w, g_gate_x_w) = _wgrad(
        "wgrad_in", dproj, z1, [rows1024, v512], [dpw, dga, dgx], tbs['wgrad_in'])
    scatter = _scatter_plan([dw_in_t.reshape(N_DEV, -1, d)], [everyone], [None])
    dx, in_vec, recv_in = _in_bwd(dproj, x2, dh1, norm_mix_g, w_in_f, scatter, tbs['in_bwd'])
    rows1024 = jnp.concatenate([in_vec[0:1], rows1024[1:]], axis=0)
    received = [recv_in, recv_out, recv_up, recv_down, recv_gate, recv_proj]

    shard_w = [w_in[0].T, w_out[0], w_up[0], w_down[0], w_ple_gate[0], w_ple_proj[0]]
    shard_m = [m_w_in[0].T, m_w_out[0], m_w_up[0], m_w_down[0], m_w_ple_gate[0], m_w_ple_proj[0]]
    shard_v = [v_w_in[0].T, v_w_out[0], v_w_up[0], v_w_down[0], v_w_ple_gate[0], v_w_ple_proj[0]]
    names = ["w_in", "w_out", "w_up", "w_down", "w_ple_gate", "w_ple_proj"]
    res = {}
    gridded = ("w_up", "w_down")
    for nm, parts, w_s, m_s, v_s in zip(names, received, shard_w, shard_m, shard_v):
        if nm in gridded:
            res[nm] = [r[None] for r in _adamw("adamw_" + nm, parts, w_s, m_s, v_s, ADAM_ROW_BLOCK)]
    rest = [k for k, nm in enumerate(names) if nm not in gridded]
    for k, out in zip(rest, _adamw_groups("adamw_medium", [(received[k], shard_w[k], shard_m[k], shard_v[k]) for k in rest])):
        res[names[k]] = [r[None] for r in out]
    res["w_in"] = [jnp.swapaxes(r, 1, 2) for r in res["w_in"]]

    def rows_of_1024(a, b, c, e, f):
        return jnp.concatenate([a, b, c, e, f.reshape(1, -1), jnp.zeros((3, d), F32)], axis=0)

    def rows_of_512(scale, bias, cb, lru, ga, gx):
        z = jnp.zeros((1, 512), F32)
        return jnp.concatenate([scale, bias.reshape(1, -1), cb, lru, z, z, z, z, ga.reshape(1, -1), gx.reshape(1, -1),
                                z, z, z, z, z, z], axis=0)

    n_conv = conv_w.shape[2]
    groups = [
        (rows1024, *[rows_of_1024(*t) for t in (
            (norm_mix_g, norm_mlp_g, norm_ple_g, b_ple_gate, norm_final_g),
            (m_norm_mix_g, m_norm_mlp_g, m_norm_ple_g, m_b_ple_gate, m_norm_final_g),
            (v_norm_mix_g, v_norm_mlp_g, v_norm_ple_g, v_b_ple_gate, v_norm_final_g))]),
        (rows512, *[rows_of_512(*t) for t in (
            (pool_scale, pool_b, conv_b, lru_L, gate_a_b, gate_x_b),
            (m_pool_scale, m_pool_b, m_conv_b, m_lru_L, m_gate_a_b, m_gate_x_b),
            (v_pool_scale, v_pool_b, v_conv_b, v_lru_L, v_gate_a_b, v_gate_x_b))]),
        (g_pool_w, *[a.reshape(-1, a.shape[-1]) for a in (pool_w, m_pool_w, v_pool_w)]),
        (g_gate_a_w, *[a.reshape(-1, a.shape[-1]) for a in (gate_a_w, m_gate_a_w, v_gate_a_w)]),
        (g_gate_x_w, *[a.reshape(-1, a.shape[-1]) for a in (gate_x_w, m_gate_x_w, v_gate_x_w)]),
        (lax.dynamic_slice_in_dim(rows512[4:8], me * n_conv, n_conv, axis=1), conv_w[0], m_conv_w[0], v_conv_w[0]),
    ]
    r1024, r512, r_pool, r_ga, r_gx, r_conv = _adamw_groups("adamw_small", groups)
    loss = rows1024[5, 0]
    for k, nm in enumerate(["norm_mix_g", "norm_mlp_g", "norm_ple_g", "b_ple_gate"]):
        res[nm] = [a[k:k + 1] for a in r1024]
    res["norm_final_g"] = [a[4] for a in r1024]
    res["pool_scale"] = [a[0:1] for a in r512]
    res["pool_b"] = [a[1:2].reshape(pool_b.shape) for a in r512]
    res["conv_b"] = [a[2:3] for a in r512]
    res["lru_L"] = [a[3:4] for a in r512]
    res["gate_a_b"] = [a[8:9].reshape(gate_a_b.shape) for a in r512]
    res["gate_x_b"] = [a[9:10].reshape(gate_x_b.shape) for a in r512]
    res["pool_w"] = [a.reshape(pool_w.shape) for a in r_pool]
    res["gate_a_w"] = [a.reshape(gate_a_w.shape) for a in r_ga]
    res["gate_x_w"] = [a.reshape(gate_x_w.shape) for a in r_gx]
    res["conv_w"] = [a[None] for a in r_conv]
    order = ["norm_mix_g", "w_in", "pool_w", "pool_b", "pool_scale", "conv_w", "conv_b", "gate_a_w", "gate_a_b",
             "gate_x_w", "gate_x_b", "lru_L", "w_out", "norm_mlp_g", "w_up", "w_down", "norm_ple_g", "w_ple_gate",
             "b_ple_gate", "w_ple_proj", "norm_final_g"]
    return (loss, dx[None], *[res[nm][kind] for kind in range(4) for nm in order])
```

```python
import jax
import jax.numpy as jnp
from jax import lax
from jax.experimental import pallas as pl
from jax.experimental.pallas import tpu as pltpu

F32 = jnp.float32
BF16 = jnp.bfloat16
MESH = pl.DeviceIdType.MESH

N_DEV = 8
RMS_EPS = 1e-6
LRU_C = 8.0
POOL_WINDOWS = (2, 4, 8, 16)
N_POOL_GROUPS = 4
LRU_HEADS = 8
HALO = 16
SUB = 8
GELU_C0 = 0.7978845608028654
GELU_C1 = 0.044715

ADAM_LR = 0.001
ADAM_B1 = 0.9
ADAM_B2 = 0.999
ADAM_EPS = 1e-08
ADAM_WD = 0.01
ADAM_STEP = 10

VMEM_LIMIT = 60 * 1024 * 1024
TIME_BLOCKS = dict(mix_fwd=512, mlp_fwd=512, ple=512, mlp_bwd=512, wgrad_out=1024, mix_bwd=512, wgrad_in=1024, in_bwd=512)
ADAM_ROW_BLOCK = 256
SCAN_UNROLL = 8
MLP_BWD_SPLIT = 2
GATHER_FORWARD_AT = (0.5, 0.8125, 0.9375, 0.9375)


def _params(n_arbitrary=1):
    return pltpu.CompilerParams(dimension_semantics=("arbitrary",) * n_arbitrary, vmem_limit_bytes=VMEM_LIMIT)


def _dot(a, b):
    return jnp.dot(a, b, preferred_element_type=F32)


def _dot_nt(a, b):
    return lax.dot_general(a, b, (((1,), (1,)), ((), ())), preferred_element_type=F32)


def _dot_tn(a, b):
    return lax.dot_general(a, b, (((0,), (0,)), ((), ())), preferred_element_type=F32)


def _rms_fwd(x, g):
    r = lax.rsqrt(jnp.mean(x * x, axis=-1, keepdims=True) + RMS_EPS)
    xh = x * r
    return xh * g, xh, r


def _rms_bwd(xh, r, g, dz):
    dxh = dz * g
    return r * (dxh - xh * jnp.mean(dxh * xh, axis=-1, keepdims=True))


def _colsum(a):
    return jnp.sum(a, axis=0, keepdims=True)


def _sigmoid(a):
    return 0.5 * jnp.tanh(0.5 * a) + 0.5


def _gelu_parts(u):
    u2 = u * u
    th = jnp.tanh(GELU_C0 * (u + GELU_C1 * u * u2))
    gel = 0.5 * u * (1.0 + th)
    dgel = 0.5 * (1.0 + th) + 0.5 * u * (1.0 - th * th) * (GELU_C0 * (1.0 + 3.0 * GELU_C1 * u2))
    return gel, dgel


def _my_index():
    return 4 * lax.axis_index("x") + 2 * lax.axis_index("y") + lax.axis_index("c")


def _all_to_all(srcs_of, dsts, send_sems, recv_sems, local_sems, dests=None):
    n = len(dsts)
    me = _my_index()
    dests = [list(range(N_DEV))] * n if dests is None else dests

    def remote(t, s):
        return pltpu.make_async_remote_copy(
            src_ref=srcs_of[t](s), dst_ref=dsts[t].at[me], send_sem=send_sems.at[t, s], recv_sem=recv_sems.at[t, me],
            device_id=(s // 4, (s // 2) % 2, s % 2), device_id_type=MESH)

    def arrival(t, s):
        return pltpu.make_async_remote_copy(
            src_ref=srcs_of[t](dests[t][0]), dst_ref=dsts[t].at[s], send_sem=send_sems.at[t, s],
            recv_sem=recv_sems.at[t, s], device_id=(s // 4, (s // 2) % 2, s % 2), device_id_type=MESH)

    def local(t, s):
        return pltpu.make_async_copy(srcs_of[t](s), dsts[t].at[s], local_sems.at[t])

    def start():
        for s in range(N_DEV):
            to_s = [t for t in range(n) if s in dests[t]]

            @pl.when(s == me)
            def _():
                for t in to_s:
                    local(t, s).start()

            @pl.when(s != me)
            def _():
                for t in to_s:
                    remote(t, s).start()

    def wait():
        for s in range(N_DEV):
            to_s = [t for t in range(n) if s in dests[t]]

            @pl.when(s == me)
            def _():
                for t in to_s:
                    local(t, s).wait()
                    for src in range(N_DEV):
                        if src != s:
                            arrival(t, src).wait_recv()

            @pl.when(s != me)
            def _():
                for t in to_s:
                    remote(t, s).wait_send()

    return start, wait


N_GATHER_COPIES = 7


def _core_major_slot(dev):
    return 4 * dev[2] + 2 * dev[0] + dev[1]


def _device_of_core_major_slot(k):
    return (k % 4) * 2 + k // 4


def _two_level_gather(srcs, dsts, send_sems, recv_sems, local_sems, slots=None):
    n = len(dsts)
    x, y, c = lax.axis_index("x"), lax.axis_index("y"), lax.axis_index("c")
    me, sibling = (x, y, c), (x, y, 1 - c)
    chips = [(1 - x, y), (x, 1 - y), (1 - x, 1 - y)]

    def slot(t, dev):
        return 4 * dev[0] + 2 * dev[1] + dev[2] if slots is None or slots[t] is None else slots[t](dev)

    def copy(t, k, block, to, src=None):
        return pltpu.make_async_remote_copy(
            src_ref=dsts[t].at[slot(t, block)] if src is None else src, dst_ref=dsts[t].at[slot(t, block)],
            send_sem=send_sems.at[t, k], recv_sem=recv_sems.at[t, k], device_id=to, device_id_type=MESH)

    def local(t):
        return pltpu.make_async_copy(srcs[t], dsts[t].at[slot(t, me)], local_sems.at[t])

    def start():
        for t in range(n):
            local(t).start()
            for j, chip in enumerate(chips):
                copy(t, 1 + j, me, (*chip, c), src=srcs[t]).start()
            copy(t, 0, me, sibling, src=srcs[t]).start()

    def forward(t):
        for j, chip in enumerate(chips):
            copy(t, 1 + j, (*chip, c), me).wait_recv()
            copy(t, 4 + j, (*chip, c), sibling).start()

    def finish():
        for t in range(n):
            copy(t, 0, sibling, me).wait_recv()
            for j, chip in enumerate(chips):
                copy(t, 4 + j, (*chip, 1 - c), me).wait_recv()
            copy(t, 0, me, sibling, src=srcs[t]).wait_send()
            for j, chip in enumerate(chips):
                copy(t, 1 + j, me, (*chip, c), src=srcs[t]).wait_send()
                copy(t, 4 + j, (*chip, c), sibling).wait_send()
            local(t).wait()

    return start, forward, finish


def _hosted_gather(i, nb, forward_at, srcs, dsts, sems, slots=None):
    start, forward, finish = _two_level_gather(srcs, dsts, *sems, slots)

    def after_step():
        for t, f in enumerate(forward_at):
            @pl.when(i == min(nb - 1, int(f * nb)))
            def _():
                forward(t)

        @pl.when(i == nb - 1)
        def _():
            finish()

    return start, after_step


def _gather_scratch(n):
    return [pltpu.SemaphoreType.DMA((n, N_GATHER_COPIES)), pltpu.SemaphoreType.DMA((n, N_GATHER_COPIES)),
            pltpu.SemaphoreType.DMA((n,))]


def _gather(name, srcs):
    n = len(srcs)

    def body(*refs):
        start, forward, finish = _two_level_gather(refs[:n], refs[n:2 * n], *refs[2 * n:])
        start()
        for t in range(n):
            forward(t)
        finish()

    any_spec = pl.BlockSpec(memory_space=pl.ANY)
    return pl.pallas_call(
        body, name=name, in_specs=[any_spec] * n, out_specs=[any_spec] * n,
        out_shape=[jax.ShapeDtypeStruct((N_DEV,) + a.shape, a.dtype) for a in srcs], scratch_shapes=_gather_scratch(n),
    )(*srcs)


def _scatter_plan(blocks, dests, landing):
    return dict(blocks=list(blocks), dests=[list(dd) for dd in dests], landing=list(landing))


def _scatter_args(plan):
    return plan['blocks'] + [a for a in plan['landing'] if a is not None]


def _scatter_out_shape(plan):
    return [jax.ShapeDtypeStruct((N_DEV,) + b.shape[1:], b.dtype) for b in plan['blocks']]


def _scatter_aliases(plan, first_in, first_out):
    given = [t for t, a in enumerate(plan['landing']) if a is not None]
    return {first_in + len(plan['blocks']) + k: first_out + t for k, t in enumerate(given)}


def _scatter_ops(plan, in_refs, out_refs, sems):
    n = len(plan['blocks'])
    srcs_of = [(lambda s, r=in_refs[t], dd=plan['dests'][t]: r.at[dd.index(s)]) for t in range(n)]
    return _all_to_all(srcs_of, out_refs, *sems, dests=plan['dests'])


def _exchange_scratch(n):
    return [pltpu.SemaphoreType.DMA((n, N_DEV)), pltpu.SemaphoreType.DMA((n, N_DEV)), pltpu.SemaphoreType.DMA((n,))]


def _const_spec(shape):
    nd = len(shape)
    return pl.BlockSpec(shape, lambda i: (0,) * nd, pipeline_mode=pl.Buffered(1))


def _pool_windows(up_ext, n, forward):
    sh = (lambda k: k) if forward else (lambda k: n - k)
    s2 = up_ext + pltpu.roll(up_ext, sh(1), 0)
    t4 = s2[:, 128:]
    s4 = t4 + pltpu.roll(t4, sh(2), 0)
    t8 = s4[:, 128:]
    s8 = t8 + pltpu.roll(t8, sh(4), 0)
    t16 = s8[:, 128:]
    s16 = t16 + pltpu.roll(t16, sh(8), 0)
    return jnp.concatenate([s2[:, :128], s4[:, :128], s8[:, :128], s16], axis=1)


def _inv_count_head():
    t = jnp.arange(1, HALO + 1, dtype=F32)[:, None]
    return jnp.concatenate([jnp.broadcast_to(1.0 / jnp.minimum(t, float(w)), (HALO, 128)) for w in POOL_WINDOWS], axis=1)


def _scale_by_inv_count(v, is_first_block, inv_head):
    inv_row = jnp.concatenate([jnp.full((1, 128), 1.0 / w, F32) for w in POOL_WINDOWS], axis=1)
    head = v[0:HALO] * jnp.where(is_first_block, inv_head, inv_row)
    return jnp.concatenate([head, v[HALO:] * inv_row], axis=0)


def _lru_decay(r, a, c_l, first_row):
    a2 = a * a
    m2 = -jnp.tanh(c_l * r) * (a2 + 1.0)
    return a2, m2, jnp.where(first_row, 1.0, jnp.sqrt(m2))


def _log_sigmoid(v):
    return -(jnp.maximum(-v, 0.0) + jnp.log1p(jnp.exp(-jnp.abs(v))))


def _conv_fwd(ul_ext, cw, cb):
    return (cb + cw[3:4, :] * ul_ext + cw[2:3, :] * pltpu.roll(ul_ext, 1, 0)
            + cw[1:2, :] * pltpu.roll(ul_ext, 2, 0) + cw[0:1, :] * pltpu.roll(ul_ext, 3, 0))


def _mix_fwd(x, g_mix, w_in, wp_bd, pool_b, pool_scale, conv_w, conv_b, wg_bd, gate_b, lru_l, w_out, gather_srcs,
             gather_slots, forward_at, tb):
    t_len, d = x.shape
    nb = t_len // tb
    n_g = len(gather_srcs)

    def body(*refs):
        (x_ref, g_ref, win_ref, wp_ref, pb_ref, ps_ref, cw_ref, cb_ref, wg_ref, gb_ref, l_ref, wout_ref,
         invh_ref) = refs[:13]
        gsrc = refs[13:13 + n_g]
        h1_ref, z1_ref, proj_ref, hs_ref, cat_ref, lru_ref, dpool_ref = refs[13 + n_g:20 + n_g]
        gdst = refs[20 + n_g:20 + 2 * n_g]
        ext_ref, a_ref, b_ref, hc_ref, send_sems, recv_sems, local_sems = refs[20 + 2 * n_g:]
        i = pl.program_id(0)
        start_gather, after_step = _hosted_gather(i, nb, forward_at, gsrc, gdst, (send_sems, recv_sems, local_sems),
                                                  gather_slots)

        @pl.when(i == 0)
        def _():
            start_gather()
            ext_ref[0:HALO, :] = jnp.zeros((HALO, 1024), F32)
            hc_ref[...] = jnp.zeros_like(hc_ref)

        xv = x_ref[...]
        z, _, _ = _rms_fwd(xv, g_ref[...])
        zb = z.astype(BF16)
        z1_ref[...] = zb
        proj = _dot_nt(zb, win_ref[...])
        proj_ref[...] = proj
        ext_ref[HALO:, :] = proj[:, 0:1024]
        ug = proj[:, 1024:1536]
        n = tb + HALO
        up_ext = ext_ref[:, 0:512]
        win = _pool_windows(up_ext, n, True)[HALO:]
        dpool = _scale_by_inv_count(win, i == 0, invh_ref[...]) - proj[:, 0:512]
        dpoolb = dpool.astype(BF16)
        dpool_ref[...] = dpoolb
        q = _dot(dpoolb, wp_ref[...]) + pb_ref[...]
        y_pool = q * ps_ref[...]
        xb = _conv_fwd(ext_ref[:, 512:1024], cw_ref[...], cb_ref[...])[HALO:]
        first_row = (i * tb + lax.broadcasted_iota(jnp.int32, (tb, 1), 0)) == 0
        c_l = LRU_C * _log_sigmoid(l_ref[...])
        gp = _dot(xb.astype(BF16), wg_ref[...]) + gb_ref[...]
        r = 1.0 / (1.0 + jnp.exp(-gp[:, :512]))
        ig = _sigmoid(gp[:, 512:])
        a = jnp.exp(c_l * r)
        _, _, mult = _lru_decay(r, a, c_l, first_row)
        lru_ref[:, 0:512] = xb
        lru_ref[:, 512:1024] = r
        lru_ref[:, 1024:1536] = ig
        lru_ref[:, 1536:2048] = a
        a_ref[...] = a
        b_ref[...] = mult * (ig * xb)
        row = lax.broadcasted_iota(jnp.int32, (SUB, 512), 0)

        def group(j, hprev):
            o = pl.multiple_of(j * SUB, SUB)
            a8 = a_ref[pl.ds(o, SUB), :]
            b8 = b_ref[pl.ds(o, SUB), :]
            for sh in (1, 2, 4):
                ash = jnp.where(row >= sh, pltpu.roll(a8, sh, 0), 1.0)
                bsh = jnp.where(row >= sh, pltpu.roll(b8, sh, 0), 0.0)
                b8 = a8 * bsh + b8
                a8 = a8 * ash
            h8 = a8 * hprev + b8
            hs_ref[pl.ds(o, SUB), :] = h8
            return jnp.broadcast_to(h8[SUB - 1:SUB, :], (SUB, 512))

        def trip(k, carry):
            for u in range(SCAN_UNROLL):
                carry = group(k * SCAN_UNROLL + u, carry)
            return carry

        hc_ref[...] = lax.fori_loop(0, tb // (SUB * SCAN_UNROLL), trip, hc_ref[...])
        gel, dgel = _gelu_parts(ug)
        lru_ref[:, 2048:2560] = gel
        lru_ref[:, 2560:3072] = dgel
        y_lru = hs_ref[...] * gel
        catb = jnp.concatenate([y_pool, y_lru], axis=1).astype(BF16)
        cat_ref[...] = catb
        h1_ref[...] = xv + _dot(catb, wout_ref[...])
        ext_ref[0:HALO, :] = ext_ref[tb:tb + HALO, :]

        after_step()

    row_spec = lambda w: pl.BlockSpec((tb, w), lambda i: (i, 0))
    any_spec = pl.BlockSpec(memory_space=pl.ANY)
    smalls = [g_mix, w_in, wp_bd, pool_b, pool_scale, conv_w, conv_b, wg_bd, gate_b, lru_l, w_out, _inv_count_head()]
    return pl.pallas_call(
        body, name="mix_fwd", grid=(nb,),
        in_specs=[row_spec(d)] + [_const_spec(s.shape) for s in smalls] + [any_spec] * n_g,
        out_specs=[row_spec(d), row_spec(d), row_spec(1536), row_spec(512), row_spec(1024), row_spec(3072), row_spec(512)]
        + [any_spec] * n_g,
        out_shape=[jax.ShapeDtypeStruct((t_len, d), F32), jax.ShapeDtypeStruct((t_len, d), BF16),
                   jax.ShapeDtypeStruct((t_len, 1536), F32), jax.ShapeDtypeStruct((t_len, 512), F32),
                   jax.ShapeDtypeStruct((t_len, 1024), BF16), jax.ShapeDtypeStruct((t_len, 3072), F32),
                   jax.ShapeDtypeStruct((t_len, 512), BF16)]
        + [jax.ShapeDtypeStruct((N_DEV,) + s.shape, s.dtype) for s in gather_srcs],
        scratch_shapes=[pltpu.VMEM((tb + HALO, 1024), F32), pltpu.VMEM((tb, 512), F32), pltpu.VMEM((tb, 512), F32),
                        pltpu.VMEM((SUB, 512), F32)] + _gather_scratch(n_g),
        compiler_params=_params(),
    )(x, *smalls, *gather_srcs)


def _mlp_fwd(h1, g_mlp, w_up, w_down, tb):
    t_len, d = h1.shape
    nb = t_len // tb
    n_chunk, _, fc = w_up.shape

    def body(h1_ref, g_ref, wup_ref, wdn_ref, h2_ref, z2_ref, up_ref):
        xv = h1_ref[...]
        z, _, _ = _rms_fwd(xv, g_ref[...])
        zb = z.astype(BF16)
        z2_ref[...] = zb
        acc = xv
        for c in range(n_chunk):
            u = _dot(zb, wup_ref[c])
            up_ref[:, c * fc:(c + 1) * fc] = u.astype(BF16)
            act = jnp.square(jnp.maximum(u, 0.0)).astype(BF16)
            acc = acc + _dot(act, wdn_ref[c * fc:(c + 1) * fc, :])
        h2_ref[...] = acc

    row_spec = lambda w: pl.BlockSpec((tb, w), lambda i: (i, 0))
    return pl.pallas_call(
        body, name="mlp_fwd", grid=(nb,),
        in_specs=[row_spec(d), _const_spec(g_mlp.shape), _const_spec(w_up.shape), _const_spec(w_down.shape)],
        out_specs=[row_spec(d), row_spec(d), row_spec(n_chunk * fc)],
        out_shape=[jax.ShapeDtypeStruct((t_len, d), F32), jax.ShapeDtypeStruct((t_len, d), BF16),
                   jax.ShapeDtypeStruct((t_len, n_chunk * fc), BF16)],
        compiler_params=_params(),
    )(h1, g_mlp, w_up, w_down)


def _ple(h2, p, target, g_ple, w_gate, b_gate, w_proj, g_final, tb):
    t_len, d = h2.shape
    nb = t_len // tb
    pd = p.shape[1]

    def body(h2_ref, p_ref, tgt_ref, g_ref, wg_ref, bg_ref, wp_ref, gf_ref,
             dh2_ref, vec_ref, dwg_out, dwp_out, dwg_acc, dwp_acc, dwg_stage, dwp_stage):
        i = pl.program_id(0)

        @pl.when(i == 0)
        def _():
            vec_ref[...] = jnp.zeros_like(vec_ref)
            dwg_acc[...] = jnp.zeros_like(dwg_acc)
            dwp_acc[...] = jnp.zeros_like(dwp_acc)

        h2 = h2_ref[...]
        g2 = g_ref[...]
        z3, xh2, r2 = _rms_fwd(h2, g2)
        z3b = z3.astype(BF16)
        gate = _sigmoid(_dot(z3b, wg_ref[...]) + bg_ref[...])
        pb = p_ref[...].astype(BF16)
        pp = _dot(pb, wp_ref[...])
        h3 = h2 + gate * pp
        gf = gf_ref[...]
        y, xh3, r3 = _rms_fwd(h3, gf)
        err = y - tgt_ref[...]
        loss_rows = jnp.mean(err * err, axis=-1, keepdims=True)
        dy = err * (1.0 / d)
        dh3 = _rms_bwd(xh3, r3, gf, dy)
        dgl = (dh3 * pp) * (gate * (1.0 - gate))
        dpp = dh3 * gate
        dglb = dgl.astype(BF16)
        dwg_acc[...] += _dot_tn(z3b, dglb)
        dwp_acc[...] += _dot_tn(pb, dpp.astype(BF16))
        dz3 = _dot_nt(dglb, wg_ref[...])
        dh2_ref[...] = dh3 + _rms_bwd(xh2, r2, g2, dz3)
        vec_ref[0:1, :] += _colsum(dgl)
        vec_ref[1:2, :] += _colsum(dz3 * xh2)
        vec_ref[2:3, :] += _colsum(dy * xh3)
        vec_ref[3:4, :] += 0.5 * jnp.sum(loss_rows)

        @pl.when(i == nb - 1)
        def _():
            dwg_stage[...] = dwg_acc[...].astype(BF16)
            dwp_stage[...] = dwp_acc[...].astype(BF16)
            pltpu.sync_copy(dwg_stage, dwg_out)
            pltpu.sync_copy(dwp_stage, dwp_out)

    row_spec = lambda w: pl.BlockSpec((tb, w), lambda i: (i, 0))
    any_spec = pl.BlockSpec(memory_space=pl.ANY)
    smalls = [g_ple, w_gate, b_gate, w_proj, g_final]
    return pl.pallas_call(
        body, name="ple_fwd_bwd", grid=(nb,),
        in_specs=[row_spec(d), row_spec(pd), row_spec(d)] + [_const_spec(s.shape) for s in smalls],
        out_specs=[row_spec(d), pl.BlockSpec((8, d), lambda i: (0, 0)), any_spec, any_spec],
        out_shape=[jax.ShapeDtypeStruct((t_len, d), F32), jax.ShapeDtypeStruct((8, d), F32),
                   jax.ShapeDtypeStruct(w_gate.shape, BF16), jax.ShapeDtypeStruct(w_proj.shape, BF16)],
        scratch_shapes=[pltpu.VMEM(w_gate.shape, F32), pltpu.VMEM(w_proj.shape, F32), pltpu.VMEM(w_gate.shape, BF16),
                        pltpu.VMEM(w_proj.shape, BF16)],
        compiler_params=_params(),
    )(h2, p, target, *smalls)


def _mlp_bwd_part(part, n_part, dh2, z2, up, w_up, w_down, dz2_prev, h1, g_mlp, scatter, tb):
    t_len, d = dh2.shape
    nb = t_len // tb
    n_chunk_all, _, fc = w_up.shape
    n_chunk = n_chunk_all // n_part
    first, last = part == 0, part == n_part - 1

    def body(*refs):
        refs = list(refs)
        dh2_ref, z2_ref, up_ref, wup_ref, wdn_ref = refs[:5]
        del refs[:5]
        dzp_ref = None if first else refs.pop(0)
        h1_ref, g_ref = (refs.pop(0), refs.pop(0)) if last else (None, None)
        scatter_in = [refs.pop(0) for _ in _scatter_args(scatter)]
        out_ref = refs.pop(0)
        vec_ref = refs.pop(0) if last else None
        dwup_out, dwdn_out = refs.pop(0), refs.pop(0)
        scatter_out = [refs.pop(0) for _ in scatter['blocks']]
        dwup_acc, dwdn_acc, up_stage, dn_stage = refs[:4]
        if scatter['blocks']:
            start_scatter, wait_scatter = _scatter_ops(scatter, scatter_in, scatter_out, refs[4:])
        i = pl.program_id(0)

        @pl.when(i == 0)
        def _():
            if scatter['blocks']:
                start_scatter()
            dwup_acc[...] = jnp.zeros_like(dwup_acc)
            dwdn_acc[...] = jnp.zeros_like(dwdn_acc)
            if last:
                vec_ref[...] = jnp.zeros_like(vec_ref)

        dh2 = dh2_ref[...]
        dh2b = dh2.astype(BF16)
        z2b = z2_ref[...]
        dz2 = jnp.zeros((tb, d), F32) if first else dzp_ref[...]
        for c in range(n_chunk):
            u = up_ref[:, c * fc:(c + 1) * fc].astype(F32)
            ur = jnp.maximum(u, 0.0)
            dact = _dot_nt(dh2b, wdn_ref[c * fc:(c + 1) * fc, :])
            dupb = (dact * (2.0 * ur)).astype(BF16)
            dwdn_acc[c * fc:(c + 1) * fc, :] += _dot_tn((ur * ur).astype(BF16), dh2b)
            dwup_acc[c] += _dot_tn(z2b, dupb)
            dz2 = dz2 + _dot_nt(dupb, wup_ref[c])
        if last:
            g = g_ref[...]
            _, xh, r = _rms_fwd(h1_ref[...], g)
            out_ref[...] = dh2 + _rms_bwd(xh, r, g, dz2)
            vec_ref[0:1, :] += _colsum(dz2 * xh)
        else:
            out_ref[...] = dz2

        @pl.when(i == nb - 1)
        def _():
            for c in range(n_chunk):
                up_stage[...] = dwup_acc[c].astype(BF16)
                dn_stage[...] = dwdn_acc[c * fc:(c + 1) * fc, :].astype(BF16)
                pltpu.sync_copy(up_stage, dwup_out.at[c])
                pltpu.sync_copy(dn_stage, dwdn_out.at[c])
            if scatter['blocks']:
                wait_scatter()

    row_spec = lambda w: pl.BlockSpec((tb, w), lambda i: (i, 0))
    any_spec = pl.BlockSpec(memory_space=pl.ANY)
    args = [dh2, z2, up, w_up, w_down]
    in_specs = [row_spec(d), row_spec(d), pl.BlockSpec((tb, n_chunk * fc), lambda i: (i, part)),
                pl.BlockSpec((n_chunk, d, fc), lambda i: (part, 0, 0), pipeline_mode=pl.Buffered(1)),
                pl.BlockSpec((n_chunk * fc, d), lambda i: (part, 0), pipeline_mode=pl.Buffered(1))]
    if not first:
        args.append(dz2_prev)
        in_specs.append(row_spec(d))
    if last:
        args += [h1, g_mlp]
        in_specs += [row_spec(d), _const_spec(g_mlp.shape)]
    n_in = len(args)
    args += _scatter_args(scatter)
    in_specs += [any_spec] * len(_scatter_args(scatter))
    out_specs = [row_spec(d)]
    out_shape = [jax.ShapeDtypeStruct((t_len, d), F32)]
    if last:
        out_specs.append(pl.BlockSpec((8, d), lambda i: (0, 0)))
        out_shape.append(jax.ShapeDtypeStruct((8, d), F32))
    out_specs += [any_spec, any_spec]
    out_shape += [jax.ShapeDtypeStruct((n_chunk, d, fc), BF16), jax.ShapeDtypeStruct((n_chunk, fc, d), BF16)]
    n_out = len(out_shape)
    out_specs += [any_spec] * len(scatter['blocks'])
    out_shape += _scatter_out_shape(scatter)
    return pl.pallas_call(
        body, name=f"mlp_bwd_{part}", grid=(nb,), in_specs=in_specs, out_specs=out_specs, out_shape=out_shape,
        scratch_shapes=[pltpu.VMEM((n_chunk, d, fc), F32), pltpu.VMEM((n_chunk * fc, d), F32),
                        pltpu.VMEM((d, fc), BF16), pltpu.VMEM((fc, d), BF16)]
        + (_exchange_scratch(len(scatter['blocks'])) if scatter['blocks'] else []),
        input_output_aliases=_scatter_aliases(scatter, n_in, n_out), compiler_params=_params(),
    )(*args)


def _mix_bwd(dh1, proj, hs, lru_saved, dpool_saved, wp_bd, pool_b, pool_scale, conv_w, wg_bd, lru_l, w_out, scatter, tb):
    t_len, d = dh1.shape
    nb = t_len // tb
    n_s = len(scatter['blocks'])
    scatter_args = _scatter_args(scatter)

    def body(*refs):
        refs = list(refs)
        (dh1_ref, ul_ref, hs_ref, hsh_ref, lru_ref, dpool_ref,
         wp_ref, pb_ref, ps_ref, cw_ref, wg_ref, l_ref, wout_ref, invh_ref) = refs[:14]
        del refs[:14]
        scatter_in = refs[:len(scatter_args)]
        del refs[:len(scatter_args)]
        dproj_ref, v512_ref, dpw_ref, dga_ref, dgx_ref = refs[:5]
        recv = refs[5:5 + n_s]
        (dwp_acc, dwg_acc, v1024_ref, b_ref, gs_ref, ehead_ref, dxbhead_ref, hc_ref,
         send_sems, recv_sems, local_sems) = refs[5 + n_s:]
        i = pl.program_id(0)
        tbk = nb - 1 - i

        start_scatter, wait_scatter = _scatter_ops(scatter, scatter_in, recv, (send_sems, recv_sems, local_sems))

        @pl.when(i == 0)
        def _():
            start_scatter()
            for ref in (v512_ref, v1024_ref, dwp_acc, dwg_acc, ehead_ref, dxbhead_ref, hc_ref):
                ref[...] = jnp.zeros_like(ref)

        dcat = _dot_nt(dh1_ref[...].astype(BF16), wout_ref[...])

        has_prev = (tbk > 0).astype(F32)
        n = tb + HALO
        inv_head = invh_ref[...]

        dpoolb = dpool_ref[...]
        q = _dot(dpoolb, wp_ref[...]) + pb_ref[...]
        dyp = dcat[:, 0:512]
        dq = dyp * ps_ref[...]
        dqb = dq.astype(BF16)
        v512_ref[0:1, :] += _colsum(dyp * q)
        v512_ref[1:2, :] += _colsum(dq)
        dwp_acc[...] += _dot_tn(dpoolb, dqb)
        dd = _dot_nt(dqb, wp_ref[...])
        e = _scale_by_inv_count(dd, tbk == 0, inv_head)
        e_ext = jnp.concatenate([e, ehead_ref[...]], axis=0)
        du_pool = _pool_windows(e_ext, n, False)[0:tb] - dd
        ehead_ref[...] = e[0:HALO]

        gel, dgel = lru_ref[:, 2048:2560], lru_ref[:, 2560:3072]
        hsv = hs_ref[...]
        dcl = dcat[:, 512:1024]
        dhs = dcl * gel
        dug = dcl * hsv * dgel
        cw = cw_ref[...]
        first_row = (tbk * tb + lax.broadcasted_iota(jnp.int32, (tb, 1), 0)) == 0
        c_l = LRU_C * _log_sigmoid(l_ref[...])
        b_ref[...] = dhs
        row = lax.broadcasted_iota(jnp.int32, (SUB, 512), 0)

        def group(jj, hnext):
            o = pl.multiple_of((tb // SUB - 1 - jj) * SUB, SUB)
            a8 = lru_ref[pl.ds(o, SUB), 1536:2048]
            d8 = b_ref[pl.ds(o, SUB), :]
            b8 = a8 * d8
            for sh in (1, 2, 4):
                ash = jnp.where(row < SUB - sh, pltpu.roll(a8, SUB - sh, 0), 1.0)
                bsh = jnp.where(row < SUB - sh, pltpu.roll(b8, SUB - sh, 0), 0.0)
                b8 = a8 * bsh + b8
                a8 = a8 * ash
            h8 = a8 * hnext + b8
            gs_ref[pl.ds(o, SUB), :] = d8 + jnp.where(row < SUB - 1, pltpu.roll(h8, SUB - 1, 0), hnext)
            return jnp.broadcast_to(h8[0:1, :], (SUB, 512))

        def trip(k, carry):
            for u in range(SCAN_UNROLL):
                carry = group(k * SCAN_UNROLL + u, carry)
            return carry

        hc_ref[...] = lax.fori_loop(0, tb // (SUB * SCAN_UNROLL), trip, hc_ref[...])
        def channel_part(lo, hi):
            xb, r, ig, a = (lru_ref[:, lo:hi], lru_ref[:, 512 + lo:512 + hi], lru_ref[:, 1024 + lo:1024 + hi],
                            lru_ref[:, 1536 + lo:1536 + hi])
            c_part = c_l[:, lo:hi]
            a2, m2, mult = _lru_decay(r, a, c_part, first_row)
            gsum = gs_ref[:, lo:hi]
            hs_ext = jnp.concatenate([hsh_ref[:, lo:hi] * has_prev, hs_ref[:, lo:hi]], axis=0)
            hprev = pltpu.roll(hs_ext, 1, 0)[SUB:]
            dmult = jnp.where(first_row, 0.0, gsum * (ig * xb))
            di = gsum * mult * xb
            dla = gsum * hprev * a - dmult * a2 * lax.rsqrt(m2)
            return ((dla * c_part) * r * (1.0 - r), di * ig * (1.0 - ig), gsum * mult * ig, _colsum(dla * r),
                    xb.astype(BF16))

        parts = [channel_part(lo, lo + 256) for lo in (0, 256)]
        v512_ref[3:4, :] += jnp.concatenate([p[3] for p in parts], axis=1)
        dgp = jnp.concatenate([p[0] for p in parts] + [p[1] for p in parts], axis=1)
        dgpb = dgp.astype(BF16)
        v1024_ref[0:1, :] += _colsum(dgp)
        dwg_acc[...] += _dot_tn(jnp.concatenate([p[4] for p in parts], axis=1), dgpb)
        dxb = jnp.concatenate([p[2] for p in parts], axis=1) + _dot_nt(dgpb, wg_ref[...])
        n8 = tb + SUB
        dxb_ext = jnp.concatenate([dxb, dxbhead_ref[...]], axis=0)
        ul = ul_ref[...]
        du_lru = cw[3:4, :] * dxb
        v512_ref[7:8, :] += _colsum(dxb * ul)
        for j in range(1, 4):
            ahead = pltpu.roll(dxb_ext, n8 - j, 0)[0:tb]
            du_lru = du_lru + cw[3 - j:4 - j, :] * ahead
            v512_ref[4 + (3 - j):5 + (3 - j), :] += _colsum(ahead * ul)
        dxbhead_ref[...] = dxb[0:SUB]
        v512_ref[2:3, :] += _colsum(dxb)

        dproj_ref[...] = jnp.concatenate([du_pool, du_lru, dug], axis=1).astype(BF16)

        @pl.when(i == nb - 1)
        def _():
            v512_ref[3:4, :] = v512_ref[3:4, :] * (LRU_C * _sigmoid(-l_ref[...]))
            v512_ref[8:9, :] = v1024_ref[0:1, 0:512]
            v512_ref[9:10, :] = v1024_ref[0:1, 512:1024]
            for g in range(N_POOL_GROUPS):
                dpw_ref[g * 128:(g + 1) * 128, :] = dwp_acc[g * 128:(g + 1) * 128, g * 128:(g + 1) * 128]
            odd_head = (lax.broadcasted_iota(jnp.int32, (512, 128), 0) // 64) % 2 == 1
            for out_ref, col0 in ((dga_ref, 0), (dgx_ref, 512)):
                pairs = jnp.concatenate([dwg_acc[128 * k:128 * (k + 1), col0 + 128 * k:col0 + 128 * (k + 1)]
                                         for k in range(LRU_HEADS // 2)], axis=0)
                out_ref[...] = jnp.where(odd_head, pltpu.roll(pairs, 64, 1), pairs)[:, 0:64]
            wait_scatter()

    rev = lambda w: pl.BlockSpec((tb, w), lambda i: (nb - 1 - i, 0))
    halo = lambda rows, w: pl.BlockSpec((rows, w), lambda i: (jnp.maximum((nb - 1 - i) * (tb // rows) - 1, 0), 0))
    any_spec = pl.BlockSpec(memory_space=pl.ANY)
    smalls = [wp_bd, pool_b, pool_scale, conv_w, wg_bd, lru_l, w_out, _inv_count_head()]
    lru_third = pl.BlockSpec((tb, 512), lambda i: (nb - 1 - i, 1))
    return pl.pallas_call(
        body, name="mix_bwd", grid=(nb,),
        in_specs=[rev(d), lru_third, rev(512), halo(SUB, 512), rev(3072), rev(512)]
        + [_const_spec(s.shape) for s in smalls] + [any_spec] * len(scatter_args),
        out_specs=[rev(1536), pl.BlockSpec((16, 512), lambda i: (0, 0)), pl.BlockSpec((512, 128), lambda i: (0, 0)),
                   pl.BlockSpec((512, 64), lambda i: (0, 0)), pl.BlockSpec((512, 64), lambda i: (0, 0))]
        + [any_spec] * n_s,
        out_shape=[jax.ShapeDtypeStruct((t_len, 1536), BF16), jax.ShapeDtypeStruct((16, 512), F32),
                   jax.ShapeDtypeStruct((512, 128), F32), jax.ShapeDtypeStruct((512, 64), F32),
                   jax.ShapeDtypeStruct((512, 64), F32)]
        + _scatter_out_shape(scatter),
        scratch_shapes=[pltpu.VMEM(wp_bd.shape, F32), pltpu.VMEM(wg_bd.shape, F32), pltpu.VMEM((8, 1024), F32),
                        pltpu.VMEM((tb, 512), F32), pltpu.VMEM((tb, 512), F32), pltpu.VMEM((HALO, 512), F32),
                        pltpu.VMEM((SUB, 512), F32), pltpu.VMEM((SUB, 512), F32)]
        + _exchange_scratch(n_s),
        input_output_aliases=_scatter_aliases(scatter, 6 + len(smalls), 5), compiler_params=_params(),
    )(dh1, proj, hs, hs, lru_saved, dpool_saved, *smalls, *scatter_args)


def _wgrad(name, a, b, whole, by_rows, tb):
    t_len, m = a.shape
    n = b.shape[1]
    nb = t_len // tb
    n_w, n_r = len(whole), len(by_rows)
    n_small = n_w + n_r

    def body(*refs):
        a_ref, b_ref = refs[:2]
        small_in = refs[2:2 + n_small]
        out_ref = refs[2 + n_small]
        small_out = refs[3 + n_small:3 + 2 * n_small]
        acc_ref, stage_ref = refs[3 + 2 * n_small:5 + 2 * n_small]
        rest = refs[5 + 2 * n_small:]
        if n_small:
            send_partials, reduce_and_send_sums, finish_small = _small_allreduce(
                small_in[:n_w], small_in[n_w:], small_out[:n_w], small_out[n_w:], rest[:n_w], rest[n_w:n_small],
                rest[n_small:n_small + n_r], *rest[n_small + n_r:])
        i = pl.program_id(0)

        @pl.when(i == 0)
        def _():
            if n_small:
                send_partials()
            acc_ref[...] = jnp.zeros_like(acc_ref)

        acc_ref[...] += _dot_tn(a_ref[...], b_ref[...].astype(BF16))

        if n_small:
            @pl.when(i == nb // 2)
            def _():
                reduce_and_send_sums()

        @pl.when(i == nb - 1)
        def _():
            stage_ref[...] = acc_ref[...].astype(BF16)
            pltpu.sync_copy(stage_ref, out_ref)
            if n_small:
                finish_small()

    small = list(whole) + list(by_rows)
    vmem_spec = pl.BlockSpec(memory_space=pltpu.VMEM)
    res = pl.pallas_call(
        body, name=name, grid=(nb,),
        in_specs=[pl.BlockSpec((tb, m), lambda i: (i, 0)), pl.BlockSpec((tb, n), lambda i: (i, 0))] + [vmem_spec] * n_small,
        out_specs=[pl.BlockSpec(memory_space=pl.ANY)] + [vmem_spec] * n_small,
        out_shape=[jax.ShapeDtypeStruct((m, n), BF16)] + [jax.ShapeDtypeStruct(s_.shape, F32) for s_ in small],
        scratch_shapes=[pltpu.VMEM((m, n), F32), pltpu.VMEM((m, n), BF16)]
        + (_small_allreduce_scratch(whole, by_rows) if n_small else []),
        compiler_params=_params(),
    )(a, b, *small)
    return res[0], res[1:1 + n_w], res[1 + n_w:]


def _in_bwd(dproj, x, dh1, g_mix, w_in, scatter, tb):
    t_len, d = x.shape
    nb = t_len // tb
    n_s = len(scatter['blocks'])
    scatter_args = _scatter_args(scatter)

    def body(*refs):
        dproj_ref, x_ref, dh1_ref, g_ref, win_ref = refs[:5]
        scatter_in = refs[5:5 + len(scatter_args)]
        dx_ref, vec_ref = refs[5 + len(scatter_args):7 + len(scatter_args)]
        recv = refs[7 + len(scatter_args):7 + len(scatter_args) + n_s]
        vec_acc, send_sems, recv_sems, local_sems, vec_land, small_send, small_recv = refs[7 + len(scatter_args) + n_s:]
        start_scatter, wait_scatter = _scatter_ops(scatter, scatter_in, recv, (send_sems, recv_sems, local_sems))
        send_partials, reduce_and_send_sums, finish_small = _small_allreduce(
            [vec_acc], [], [vec_ref], [], [vec_land], [], [], small_send, small_recv)
        i = pl.program_id(0)

        @pl.when(i == 0)
        def _():
            start_scatter()
            vec_acc[...] = jnp.zeros_like(vec_acc)

        dz1 = _dot(dproj_ref[...], win_ref[...])
        g = g_ref[...]
        _, xh, rr = _rms_fwd(x_ref[...], g)
        dx_ref[...] = dh1_ref[...] + _rms_bwd(xh, rr, g, dz1)
        vec_acc[0:1, :] += _colsum(dz1 * xh)

        @pl.when(i == nb - 1)
        def _():
            send_partials()
            reduce_and_send_sums()
            finish_small()
            wait_scatter()

    row_spec = lambda w: pl.BlockSpec((tb, w), lambda i: (i, 0))
    any_spec = pl.BlockSpec(memory_space=pl.ANY)
    return pl.pallas_call(
        body, name="in_bwd", grid=(nb,),
        in_specs=[row_spec(dproj.shape[1]), row_spec(d), row_spec(d), _const_spec(g_mix.shape), _const_spec(w_in.shape)]
        + [any_spec] * len(scatter_args),
        out_specs=[row_spec(d), pl.BlockSpec((8, d), lambda i: (0, 0))] + [any_spec] * n_s,
        out_shape=[jax.ShapeDtypeStruct((t_len, d), F32), jax.ShapeDtypeStruct((8, d), F32)] + _scatter_out_shape(scatter),
        scratch_shapes=[pltpu.VMEM((8, d), F32)] + _exchange_scratch(n_s)
        + _small_allreduce_scratch([jax.ShapeDtypeStruct((8, d), F32)], []),
        input_output_aliases=_scatter_aliases(scatter, 5, 2), compiler_params=_params(),
    )(dproj, x, dh1, g_mix, w_in, *scatter_args)


def _small_allreduce(whole_in, rows_in, whole_out, rows_out, whole_land, rows_land, rows_sum, send_sems, recv_sems):
    n_w, n_r = len(whole_in), len(rows_in)
    per = [r.shape[0] // N_DEV for r in rows_in]
    me = _my_index()

    def dev(s):
        return (s // 4, (s // 2) % 2, s % 2)

    def rows_of(t, s):
        return pl.ds(s * per[t], per[t])

    def mine(t):
        return pl.ds(pl.multiple_of(me * per[t], 8), per[t])

    def partial(t, s, slot):
        if t < n_w:
            src, dst = whole_in[t], whole_land[t]
        else:
            src, dst = rows_in[t - n_w].at[rows_of(t - n_w, s)], rows_land[t - n_w]
        return pltpu.make_async_remote_copy(
            src_ref=src, dst_ref=dst.at[slot], send_sem=send_sems.at[t, s], recv_sem=recv_sems.at[t, slot],
            device_id=dev(s), device_id_type=MESH)

    def summed(t, s, rows, slot):
        return pltpu.make_async_remote_copy(
            src_ref=rows_sum[t].at[rows], dst_ref=rows_sum[t].at[rows], send_sem=send_sems.at[n_w + n_r + t, s],
            recv_sem=recv_sems.at[n_w + n_r + t, slot], device_id=dev(s), device_id_type=MESH)

    def send_partials():
        for s in range(N_DEV):
            @pl.when(s != me)
            def _():
                for t in range(n_w + n_r):
                    partial(t, s, me).start()
        for t in range(n_w):
            whole_land[t][me] = whole_in[t][...]
        for t in range(n_r):
            rows_land[t][me] = rows_in[t][mine(t), :]

    def reduce_and_send_sums():
        for s in range(N_DEV):
            @pl.when(s != me)
            def _():
                for t in range(n_w + n_r):
                    partial(t, s, s).wait_recv()
        for t in range(n_w):
            total = whole_land[t][0]
            for s in range(1, N_DEV):
                total = total + whole_land[t][s]
            whole_out[t][...] = total
        for t in range(n_r):
            total = rows_land[t][0]
            for s in range(1, N_DEV):
                total = total + rows_land[t][s]
            rows_sum[t][mine(t), :] = total
        for s in range(N_DEV):
            @pl.when(s != me)
            def _():
                for t in range(n_r):
                    summed(t, s, mine(t), me).start()

    def finish():
        for s in range(N_DEV):
            @pl.when(s != me)
            def _():
                for t in range(n_r):
                    summed(t, s, rows_of(t, s), s).wait_recv()
                    summed(t, s, mine(t), me).wait_send()
                for t in range(n_w + n_r):
                    partial(t, s, me).wait_send()
        for t in range(n_r):
            rows_out[t][...] = rows_sum[t][...]

    return send_partials, reduce_and_send_sums, finish


def _small_allreduce_scratch(whole, by_rows):
    n_sem = len(whole) + 2 * len(by_rows)
    return ([pltpu.VMEM((N_DEV,) + a.shape, F32) for a in whole]
            + [pltpu.VMEM((N_DEV, a.shape[0] // N_DEV, a.shape[1]), F32) for a in by_rows]
            + [pltpu.VMEM(a.shape, F32) for a in by_rows]
            + [pltpu.SemaphoreType.DMA((n_sem, N_DEV)), pltpu.SemaphoreType.DMA((n_sem, N_DEV))])


def _adam_update(g, w, m, v):
    m_new = ADAM_B1 * m + (1.0 - ADAM_B1) * g
    v_new = ADAM_B2 * v + (1.0 - ADAM_B2) * jnp.square(g)
    m_hat = m_new / (1.0 - ADAM_B1 ** ADAM_STEP)
    v_hat = v_new / (1.0 - ADAM_B2 ** ADAM_STEP)
    return -ADAM_LR * (m_hat / (jnp.sqrt(v_hat) + ADAM_EPS) + ADAM_WD * w), m_new, v_new


def _adamw_groups(name, groups):
    n = len(groups)

    def body(*refs):
        for k in range(n):
            g_ref, w_ref, m_ref, v_ref = refs[4 * k:4 * k + 4]
            g_out, d_out, m_out, v_out = refs[4 * n + 4 * k:4 * n + 4 * k + 4]
            if len(g_ref.shape) == 3:
                g = g_ref[0].astype(F32)
                for s in range(1, g_ref.shape[0]):
                    g = g + g_ref[s].astype(F32)
            else:
                g = g_ref[...]
            g_out[...] = g
            d_out[...], m_out[...], v_out[...] = _adam_update(g, w_ref[...], m_ref[...], v_ref[...])

    flat = [a for grp in groups for a in grp]
    out = pl.pallas_call(
        body, name=name, out_shape=[jax.ShapeDtypeStruct(grp[1].shape, F32) for grp in groups for _ in range(4)],
        compiler_params=pltpu.CompilerParams(vmem_limit_bytes=VMEM_LIMIT))(*flat)
    return [out[4 * k:4 * k + 4] for k in range(n)]


def _adamw(name, parts, w, m, v, row_block):
    n_src, rows, cols = parts.shape
    rb = min(row_block, rows)

    def body(p_ref, w_ref, m_ref, v_ref, g_out, d_out, m_out, v_out):
        g = p_ref[0].astype(F32)
        for s in range(1, n_src):
            g = g + p_ref[s].astype(F32)
        g_out[...] = g
        d_out[...], m_out[...], v_out[...] = _adam_update(g, w_ref[...], m_ref[...], v_ref[...])

    spec = pl.BlockSpec((rb, cols), lambda i: (i, 0))
    return pl.pallas_call(
        body, name=name, grid=(rows // rb,),
        in_specs=[pl.BlockSpec((n_src, rb, cols), lambda i: (0, i, 0)), spec, spec, spec],
        out_specs=[spec] * 4, out_shape=[jax.ShapeDtypeStruct((rows, cols), F32)] * 4,
        compiler_params=pltpu.CompilerParams(dimension_semantics=("parallel",), vmem_limit_bytes=VMEM_LIMIT),
    )(parts, w, m, v)


def _block_diag(blocks):
    g, a, b = blocks.shape
    eye = jnp.eye(g, dtype=blocks.dtype)
    return (eye[:, None, :, None] * blocks[:, :, None, :]).reshape(g * a, g * b)


def kernel(x, p, norm_mix_g, w_in, pool_w, pool_b, pool_scale, conv_w, conv_b, gate_a_w, gate_a_b, gate_x_w, gate_x_b, lru_L, w_out, norm_mlp_g, w_up, w_down, norm_ple_g, w_ple_gate, b_ple_gate, w_ple_proj, norm_final_g, loss_target, m_norm_mix_g, m_w_in, m_pool_w, m_pool_b, m_pool_scale, m_conv_w, m_conv_b, m_gate_a_w, m_gate_a_b, m_gate_x_w, m_gate_x_b, m_lru_L, m_w_out, m_norm_mlp_g, m_w_up, m_w_down, m_norm_ple_g, m_w_ple_gate, m_b_ple_gate, m_w_ple_proj, m_norm_final_g, v_norm_mix_g, v_w_in, v_pool_w, v_pool_b, v_pool_scale, v_conv_w, v_conv_b, v_gate_a_w, v_gate_a_b, v_gate_x_w, v_gate_x_b, v_lru_L, v_w_out, v_norm_mlp_g, v_w_up, v_w_down, v_norm_ple_g, v_w_ple_gate, v_b_ple_gate, v_w_ple_proj, v_norm_final_g):
    t_len, d = x.shape[1], x.shape[2]
    tbs = {k: min(v, t_len) for k, v in TIME_BLOCKS.items()}
    me = _my_index()

    win_g, wout_g, convw_g = _gather("gather_mixer_weights",
                                     [w_in[0].T.astype(BF16), w_out[0].astype(BF16), conv_w[0]])
    w_in_f = win_g.reshape(-1, d)
    conv_w_f = jnp.transpose(convw_g, (1, 0, 2)).reshape(convw_g.shape[1], -1)
    wp_bd = _block_diag(pool_w[0]).astype(BF16)
    wg_bd = jnp.concatenate([_block_diag(gate_a_w[0]), _block_diag(gate_x_w[0])], axis=1).astype(BF16)
    gate_b2 = jnp.concatenate([gate_a_b.reshape(1, -1), gate_x_b.reshape(1, -1)], axis=1)
    mixer_small = (norm_mix_g, w_in_f, wp_bd, pool_b.reshape(1, -1), pool_scale, conv_w_f, conv_b, wg_bd, gate_b2, lru_L,
                   wout_g.reshape(-1, d))

    x2 = x[0]
    later = [w_up[0].astype(BF16), w_down[0].astype(BF16), w_ple_gate[0].astype(BF16), w_ple_proj[0].astype(BF16)]
    h1, z1, proj, hs, cat, lru_saved, dpool_saved, wup_g, wdn_g, wgate_g, wproj_g = _mix_fwd(
        x2, *mixer_small, later, [_core_major_slot, _core_major_slot, None, None], GATHER_FORWARD_AT, tbs['mix_fwd'])
    w_down_f = wdn_g.reshape(-1, d)
    w_proj_f = jnp.transpose(wproj_g, (1, 0, 2)).reshape(wproj_g.shape[1], -1)
    h2, z2, up = _mlp_fwd(h1, norm_mlp_g, wup_g, w_down_f, tbs['mlp_fwd'])
    dh2, ple_vec, dw_gate, dw_proj = _ple(h2, p[0, 0], loss_target[0], norm_ple_g, wgate_g.reshape(-1, d), b_ple_gate,
                                          w_proj_f, norm_final_g.reshape(1, -1), tbs['ple'])
    everyone = list(range(N_DEV))
    n_proj = w_ple_proj.shape[2]
    dz2_0, dw_up_0, dw_down_0 = _mlp_bwd_part(
        0, MLP_BWD_SPLIT, dh2, z2, up, wup_g, w_down_f, None, h1, norm_mlp_g, _scatter_plan([], [], []), tbs['mlp_bwd'])
    half = N_DEV // MLP_BWD_SPLIT
    south = [_device_of_core_major_slot(k) for k in range(half)]
    north = [_device_of_core_major_slot(k) for k in range(half, N_DEV)]
    scatter = _scatter_plan(
        [dw_up_0, dw_down_0, dw_gate.reshape(N_DEV, -1, d), jnp.transpose(dw_proj.reshape(-1, N_DEV, n_proj), (1, 0, 2))],
        [south, south, everyone, everyone], [None, None, None, None])
    dh1, mlp_vec, dw_up_1, dw_down_1, recv_up, recv_down, recv_gate, recv_proj = _mlp_bwd_part(
        1, MLP_BWD_SPLIT, dh2, z2, up, wup_g, w_down_f, dz2_0, h1, norm_mlp_g, scatter, tbs['mlp_bwd'])
    dw_out, _, _ = _wgrad("wgrad_out", cat, dh1, [], [], tbs['wgrad_out'])
    scatter = _scatter_plan([dw_up_1, dw_down_1, dw_out.reshape(N_DEV, -1, d)], [north, north, everyone],
                            [recv_up, recv_down, None])
    dproj, v512, dpw, dga, dgx, recv_up, recv_down, recv_out = _mix_bwd(
        dh1, proj, hs, lru_saved, dpool_saved, wp_bd, pool_b.reshape(1, -1), pool_scale, conv_w_f, wg_bd, lru_L, wout_g.reshape(-1, d),
        scatter, tbs['mix_bwd'])
    rows1024 = jnp.concatenate([jnp.zeros((1, d), F32), mlp_vec[0:1], ple_vec[1:2], ple_vec[0:1], ple_vec[2:4],
                                jnp.zeros((2, d), F32)], axis=0)
    dw_in_t, (rows1024, rows512), (g_pool_w, g_gate_a_w, g_gate_x_w) = _wgrad(
        "wgrad_in", dproj, z1, [rows1024, v512], [dpw, dga, dgx], tbs['wgrad_in'])
    scatter = _scatter_plan([dw_in_t.reshape(N_DEV, -1, d)], [everyone], [None])
    dx, in_vec, recv_in = _in_bwd(dproj, x2, dh1, norm_mix_g, w_in_f, scatter, tbs['in_bwd'])
    rows1024 = jnp.concatenate([in_vec[0:1], rows1024[1:]], axis=0)
    received = [recv_in, recv_out, recv_up, recv_down, recv_gate, recv_proj]

    shard_w = [w_in[0].T, w_out[0], w_up[0], w_down[0], w_ple_gate[0], w_ple_proj[0]]
    shard_m = [m_w_in[0].T, m_w_out[0], m_w_up[0], m_w_down[0], m_w_ple_gate[0], m_w_ple_proj[0]]
    shard_v = [v_w_in[0].T, v_w_out[0], v_w_up[0], v_w_down[0], v_w_ple_gate[0], v_w_ple_proj[0]]
    names = ["w_in", "w_out", "w_up", "w_down", "w_ple_gate", "w_ple_proj"]
    res = {}
    gridded = ("w_up", "w_down")
    for nm, parts, w_s, m_s, v_s in zip(names, received, shard_w, shard_m, shard_v):
        if nm in gridded:
            res[nm] = [r[None] for r in _adamw("adamw_" + nm, parts, w_s, m_s, v_s, ADAM_ROW_BLOCK)]
    rest = [k for k, nm in enumerate(names) if nm not in gridded]
    for k, out in zip(rest, _adamw_groups("adamw_medium", [(received[k], shard_w[k], shard_m[k], shard_v[k]) for k in rest])):
        res[names[k]] = [r[None] for r in out]
    res["w_in"] = [jnp.swapaxes(r, 1, 2) for r in res["w_in"]]

    def rows_of_1024(a, b, c, e, f):
        return jnp.concatenate([a, b, c, e, f.reshape(1, -1), jnp.zeros((3, d), F32)], axis=0)

    def rows_of_512(scale, bias, cb, lru, ga, gx):
        z = jnp.zeros((1, 512), F32)
        return jnp.concatenate([scale, bias.reshape(1, -1), cb, lru, z, z, z, z, ga.reshape(1, -1), gx.reshape(1, -1),
                                z, z, z, z, z, z], axis=0)

    n_conv = conv_w.shape[2]
    groups = [
        (rows1024, *[rows_of_1024(*t) for t in (
            (norm_mix_g, norm_mlp_g, norm_ple_g, b_ple_gate, norm_final_g),
            (m_norm_mix_g, m_norm_mlp_g, m_norm_ple_g, m_b_ple_gate, m_norm_final_g),
            (v_norm_mix_g, v_norm_mlp_g, v_norm_ple_g, v_b_ple_gate, v_norm_final_g))]),
        (rows512, *[rows_of_512(*t) for t in (
            (pool_scale, pool_b, conv_b, lru_L, gate_a_b, gate_x_b),
            (m_pool_scale, m_pool_b, m_conv_b, m_lru_L, m_gate_a_b, m_gate_x_b),
            (v_pool_scale, v_pool_b, v_conv_b, v_lru_L, v_gate_a_b, v_gate_x_b))]),
        (g_pool_w, *[a.reshape(-1, a.shape[-1]) for a in (pool_w, m_pool_w, v_pool_w)]),
        (g_gate_a_w, *[a.reshape(-1, a.shape[-1]) for a in (gate_a_w, m_gate_a_w, v_gate_a_w)]),
        (g_gate_x_w, *[a.reshape(-1, a.shape[-1]) for a in (gate_x_w, m_gate_x_w, v_gate_x_w)]),
        (lax.dynamic_slice_in_dim(rows512[4:8], me * n_conv, n_conv, axis=1), conv_w[0], m_conv_w[0], v_conv_w[0]),
    ]
    r1024, r512, r_pool, r_ga, r_gx, r_conv = _adamw_groups("adamw_small", groups)
    loss = rows1024[5, 0]
    for k, nm in enumerate(["norm_mix_g", "norm_mlp_g", "norm_ple_g", "b_ple_gate"]):
        res[nm] = [a[k:k + 1] for a in r1024]
    res["norm_final_g"] = [a[4] for a in r1024]
    res["pool_scale"] = [a[0:1] for a in r512]
    res["pool_b"] = [a[1:2].reshape(pool_b.shape) for a in r512]
    res["conv_b"] = [a[2:3] for a in r512]
    res["lru_L"] = [a[3:4] for a in r512]
    res["gate_a_b"] = [a[8:9].reshape(gate_a_b.shape) for a in r512]
    res["gate_x_b"] = [a[9:10].reshape(gate_x_b.shape) for a in r512]
    res["pool_w"] = [a.reshape(pool_w.shape) for a in r_pool]
    res["gate_a_w"] = [a.reshape(gate_a_w.shape) for a in r_ga]
    res["gate_x_w"] = [a.reshape(gate_x_w.shape) for a in r_gx]
    res["conv_w"] = [a[None] for a in r_conv]
    order = ["norm_mix_g", "w_in", "pool_w", "pool_b", "pool_scale", "conv_w", "conv_b", "gate_a_w", "gate_a_b",
             "gate_x_w", "gate_x_b", "lru_L", "w_out", "norm_mlp_g", "w_up", "w_down", "norm_ple_g", "w_ple_gate",
             "b_ple_gate", "w_ple_proj", "norm_final_g"]
    return (loss, dx[None], *[res[nm][kind] for kind in range(4) for nm in order])
```
